```python
import jax, jax.numpy as jnp
from jax import lax
import numpy as np

D_MODEL = 1024
BATCH = 16
SEQ = 2048
DEPTH = 1

PLE_DIM = 256
FOX_HEADS = 8
FOX_HEAD_DIM = 64
Q_BLOCK = 128
RET_HEADS = 4
RET_KEY_DIM = 64
RET_VAL_DIM = 128
RET_CHUNK = 128
ROPE_BASE = 10000.0
N_BRANCHES = 2
BRANCH_WIDTH = 512
N_EXPERTS = 32
TOP_K = 4
EXPERT_FF = 1024
SWIGLU_LIMIT = 7.0
SWIGLU_ALPHA = 1.702
MOE_BLOCK = 128
EPS = 1e-6

FOX_W = FOX_HEADS * FOX_HEAD_DIM
RET_QK_W = RET_HEADS * RET_KEY_DIM
RET_V_W = RET_HEADS * RET_VAL_DIM
IN_SPLITS = list(np.cumsum([FOX_W, FOX_W, FOX_W, FOX_HEADS, RET_QK_W, RET_QK_W, RET_V_W]))
IN_WIDTH = 3 * FOX_W + FOX_HEADS + 2 * RET_QK_W + 2 * RET_V_W

kernel_name = "fox_retnet_gated_moe_ple_block"


def rms_norm(x, g):
    x32 = x.astype(jnp.float32)
    y = x32 * lax.rsqrt(jnp.mean(x32 * x32, axis=-1, keepdims=True) + EPS)
    return (y * g.astype(jnp.float32)).astype(x.dtype)


def rotary(x, pos):
    half = x.shape[-1] // 2
    inv = ROPE_BASE ** (-jnp.arange(half, dtype=jnp.float32) / half)
    ang = pos.astype(jnp.float32)[:, None] * inv[None, :]
    cos = jnp.cos(ang)[None, :, None, :]
    sin = jnp.sin(ang)[None, :, None, :]
    x32 = x.astype(jnp.float32)
    x1, x2 = x32[..., :half], x32[..., half:]
    return jnp.concatenate([x1 * cos - x2 * sin, x2 * cos + x1 * sin], axis=-1).astype(x.dtype)


def fox_attention(q, k, v, log_f):
    B, S, H, d = q.shape
    scale = d ** -0.5
    qh = q.transpose(0, 2, 1, 3)
    kh = k.transpose(0, 2, 1, 3)
    vh = v.transpose(0, 2, 1, 3)
    c = jnp.cumsum(log_f, axis=1).transpose(0, 2, 1)
    outs = []
    for blk in range(S // Q_BLOCK):
        q0, q1 = blk * Q_BLOCK, (blk + 1) * Q_BLOCK
        qb, kb, vb = qh[:, :, q0:q1], kh[:, :, :q1], vh[:, :, :q1]
        logits = (jnp.einsum('bhqd,bhkd->bhqk', qb, kb).astype(jnp.float32) * scale
                  + c[:, :, q0:q1, None] - c[:, :, None, :q1])
        qpos = jnp.arange(q0, q1)
        kpos = jnp.arange(q1)
        logits = jnp.where(kpos[None, :] <= qpos[:, None], logits, -jnp.inf)
        probs = jax.nn.softmax(logits, axis=-1).astype(vb.dtype)
        outs.append(jnp.einsum('bhqk,bhkd->bhqd', probs, vb))
    o = jnp.concatenate(outs, axis=2)
    return o.transpose(0, 2, 1, 3).reshape(B, S, H * d)


def retention(q, k, v):
    B, S, H, dk = q.shape
    dv = v.shape[-1]
    C = RET_CHUNK
    NC = S // C
    lg = jnp.log1p(-jnp.exp2(-5.0 - jnp.arange(H, dtype=jnp.float32)))
    i = jnp.arange(C, dtype=jnp.float32)
    diff = i[:, None] - i[None, :]
    inner_decay = jnp.where(diff >= 0, jnp.exp(jnp.maximum(diff, 0.0)[None] * lg[:, None, None]), 0.0)
    q_decay = jnp.exp((i + 1.0)[None, :] * lg[:, None])
    k_decay = jnp.exp((C - 1.0 - i)[None, :] * lg[:, None])
    chunk_decay = jnp.exp(C * lg)

    def to_chunks(t):
        return t.reshape(B, NC, C, H, t.shape[-1]).transpose(1, 0, 3, 2, 4)

    def step(state, inp):
        qi, ki, vi = inp
        qi32, ki32, vi32 = qi.astype(jnp.float32), ki.astype(jnp.float32), vi.astype(jnp.float32)
        scores = jnp.einsum('bhid,bhjd->bhij', qi32, ki32) * inner_decay[None]
        o_intra = jnp.einsum('bhij,bhjv->bhiv', scores, vi32)
        o_inter = jnp.einsum('bhid,bhdv->bhiv', qi32, state) * q_decay[None, :, :, None]
        new_state = (state * chunk_decay[None, :, None, None]
                     + jnp.einsum('bhjd,bhjv->bhdv', ki32 * k_decay[None, :, :, None], vi32))
        return new_state, o_intra + o_inter

    init = jnp.zeros((B, H, dk, dv), jnp.float32)
    _, oc = lax.scan(step, init, (to_chunks(q), to_chunks(k), to_chunks(v)))
    return oc.transpose(1, 0, 3, 2, 4).reshape(B, S, H, dv)


def mixer_block(h, mix_norm, w_in, b_forget, w_branch, w_merge_gate, b_merge_gate, w_out):
    B, S, D = h.shape
    xn = rms_norm(h, mix_norm)
    u = xn @ w_in
    fq, fk, fv, ff, rq, rk, rv, rg = jnp.split(u, IN_SPLITS, axis=-1)
    log_f = jax.nn.log_sigmoid((ff + b_forget).astype(jnp.float32))
    fox = fox_attention(fq.reshape(B, S, FOX_HEADS, FOX_HEAD_DIM),
                        fk.reshape(B, S, FOX_HEADS, FOX_HEAD_DIM),
                        fv.reshape(B, S, FOX_HEADS, FOX_HEAD_DIM), log_f)
    pos = jnp.arange(S)
    rq = rotary(rq.reshape(B, S, RET_HEADS, RET_KEY_DIM), pos)
    rk = rotary(rk.reshape(B, S, RET_HEADS, RET_KEY_DIM), pos) * (RET_KEY_DIM ** -0.5)
    ro = retention(rq, rk, rv.reshape(B, S, RET_HEADS, RET_VAL_DIM))
    ro = ro * lax.rsqrt(jnp.mean(ro * ro, axis=-1, keepdims=True) + EPS)
    ro = ro.reshape(B, S, RET_V_W).astype(h.dtype) * jax.nn.silu(rg)
    branches = jnp.stack([fox, ro], axis=2)
    proj = jnp.einsum('bsgc,gcd->bsgd', branches, w_branch)
    gates = jax.nn.sigmoid(xn @ w_merge_gate + b_merge_gate).reshape(B, S, N_BRANCHES, D)
    merged = jnp.sum(gates * proj, axis=2)
    return merged @ w_out


def moe_block(hn, w_router, b_router, w_gate_up, b_gate_up, w_down, b_down):
    B, S, D = hn.shape
    N = B * S
    A = N * TOP_K
    xf = hn.reshape(N, D)
    logits = (xf @ w_router + b_router).astype(jnp.float32)
    top_val, top_idx = lax.top_k(logits, TOP_K)
    top_w = jax.nn.softmax(top_val, axis=-1)
    flat_e = top_idx.reshape(A).astype(jnp.int32)
    flat_w = top_w.reshape(A)
    flat_tok = jnp.arange(A, dtype=jnp.int32) // TOP_K
    order = jnp.argsort(flat_e, stable=True)
    sorted_e, sorted_tok, sorted_w = flat_e[order], flat_tok[order], flat_w[order]
    counts = jnp.bincount(flat_e, length=N_EXPERTS)
    group_start = jnp.cumsum(counts) - counts
    padded = (counts + MOE_BLOCK - 1) // MOE_BLOCK * MOE_BLOCK
    pad_end = jnp.cumsum(padded)
    pad_start = pad_end - padded
    dest = pad_start[sorted_e] + jnp.arange(A, dtype=jnp.int32) - group_start[sorted_e]
    n_blocks = -(-(A + N_EXPERTS * (MOE_BLOCK - 1)) // MOE_BLOCK)
    P = n_blocks * MOE_BLOCK
    buf_tok = jnp.full((P,), N, jnp.int32).at[dest].set(sorted_tok)
    buf_w = jnp.zeros((P,), jnp.float32).at[dest].set(sorted_w)
    x_pad = jnp.concatenate([xf, jnp.zeros((1, D), xf.dtype)], axis=0)
    x_buf = x_pad[buf_tok].reshape(n_blocks, MOE_BLOCK, D)
    block_start = jnp.arange(n_blocks, dtype=jnp.int32) * MOE_BLOCK
    block_e = jnp.minimum(jnp.sum(block_start[:, None] >= pad_end[None, :], axis=1), N_EXPERTS - 1)

    def expert_block(args):
        xb, e = args
        gu = xb @ w_gate_up[e] + b_gate_up[e]
        glu = jnp.minimum(gu[:, ::2], SWIGLU_LIMIT)
        lin = jnp.clip(gu[:, 1::2], -SWIGLU_LIMIT, SWIGLU_LIMIT)
        act = glu * jax.nn.sigmoid(SWIGLU_ALPHA * glu) * (lin + 1.0)
        return act @ w_down[e] + b_down[e]

    y_buf = lax.map(expert_block, (x_buf, block_e)).reshape(P, D)
    y = y_buf * buf_w[:, None].astype(y_buf.dtype)
    out = jax.ops.segment_sum(y, buf_tok, num_segments=N + 1)[:N]
    return out.reshape(B, S, D)


def setup_inputs(seed: int = 0) -> dict:
    key = jax.random.key(seed)
    ks = jax.random.split(key, 20)
    f32 = jnp.float32
    L, D = DEPTH, D_MODEL

    def nrm(k, shape, fan_in):
        return jax.random.normal(k, shape, f32) * (fan_in ** -0.5)

    def gain(k, shape):
        return 1.0 + 0.01 * jax.random.normal(k, shape, f32)

    return {
        "x": jax.random.normal(ks[0], (BATCH, SEQ, D), f32),
        "p": jax.random.normal(ks[1], (L, BATCH, SEQ, PLE_DIM), f32),
        "mix_norm": gain(ks[2], (L, D)),
        "w_in": nrm(ks[3], (L, D, IN_WIDTH), D),
        "b_forget": jax.random.uniform(ks[4], (L, FOX_HEADS), f32, minval=1.0, maxval=5.0),
        "w_branch": nrm(ks[5], (L, N_BRANCHES, BRANCH_WIDTH, D), BRANCH_WIDTH),
        "w_merge_gate": nrm(ks[6], (L, D, N_BRANCHES * D), D),
        "b_merge_gate": 0.01 * jax.random.normal(ks[7], (L, N_BRANCHES * D), f32),
        "w_out": nrm(ks[8], (L, D, D), D),
        "ffn_norm": gain(ks[9], (L, D)),
        "w_router": nrm(ks[10], (L, D, N_EXPERTS), D),
        "b_router": 0.01 * jax.random.normal(ks[11], (L, N_EXPERTS), f32),
        "w_gate_up": nrm(ks[12], (L, N_EXPERTS, D, 2 * EXPERT_FF), D),
        "b_gate_up": 0.01 * jax.random.normal(ks[13], (L, N_EXPERTS, 2 * EXPERT_FF), f32),
        "w_down": nrm(ks[14], (L, N_EXPERTS, EXPERT_FF, D), EXPERT_FF),
        "b_down": 0.01 * jax.random.normal(ks[15], (L, N_EXPERTS, D), f32),
        "ple_norm": gain(ks[16], (L, D)),
        "w_ple_gate": nrm(ks[17], (L, D, D), D),
        "w_ple_proj": nrm(ks[18], (L, PLE_DIM, D), PLE_DIM),
        "final_norm": gain(ks[19], (D,)),
    }


def reference(x, p, mix_norm, w_in, b_forget, w_branch, w_merge_gate, b_merge_gate, w_out,
              ffn_norm, w_router, b_router, w_gate_up, b_gate_up, w_down, b_down,
              ple_norm, w_ple_gate, w_ple_proj, final_norm):
    h = x
    for i in range(DEPTH):
        h = h + mixer_block(h, mix_norm[i], w_in[i], b_forget[i], w_branch[i],
                            w_merge_gate[i], b_merge_gate[i], w_out[i])
        h = h + moe_block(rms_norm(h, ffn_norm[i]), w_router[i], b_router[i],
                          w_gate_up[i], b_gate_up[i], w_down[i], b_down[i])
        ple_gate = jax.nn.sigmoid(rms_norm(h, ple_norm[i]) @ w_ple_gate[i])
        h = h + ple_gate * (p[i] @ w_ple_proj[i])
    return rms_norm(h, final_norm)
```

```python
import functools

import numpy as np
import jax
import jax.numpy as jnp
from jax import lax
from jax.experimental import pallas as pl
from jax.experimental.pallas import tpu as pltpu

FOX_HEADS = 8
FOX_HEAD_DIM = 64
RET_HEADS = 4
RET_KEY_DIM = 64
RET_VAL_DIM = 128
ROPE_BASE = 10000.0
N_EXPERTS = 32
TOP_K = 4
SWIGLU_LIMIT = 7.0
SWIGLU_ALPHA = 1.702
EPS = 1e-6

LANES = 128
FOX_W = FOX_HEADS * FOX_HEAD_DIM
RET_QK_W = RET_HEADS * RET_KEY_DIM
RET_V_W = RET_HEADS * RET_VAL_DIM
VMEM_LIMIT = 56 * 1024 * 1024

F32 = jnp.float32
BF16 = jnp.bfloat16


def _rms(x, g):
    return x * lax.rsqrt(jnp.mean(x * x, axis=-1, keepdims=True) + EPS) * g


def _sigmoid(x):
    return 1.0 / (1.0 + jnp.exp(-x))


def _dot(a, b):
    return jnp.dot(a, b, preferred_element_type=F32)


def _dot_nt(a, b):
    return lax.dot_general(a, b, (((1,), (1,)), ((), ())), preferred_element_type=F32)


def _params(n_axes):
    return pltpu.CompilerParams(dimension_semantics=("arbitrary",) * n_axes,
                                vmem_limit_bytes=VMEM_LIMIT)


def _inproj_kernel(x_ref, g_ref, w_ref, wf_ref, bf_ref, cos_ref, sin_ref,
                   fq_ref, fk_ref, fv_ref, lf_ref, rq_ref, rk_ref, rv_ref, rg_ref):
    xn = _rms(x_ref[...], g_ref[...]).astype(BF16)
    u = _dot(xn, w_ref[...])
    o = 0
    fq_ref[...] = (u[:, o:o + FOX_W] * (FOX_HEAD_DIM ** -0.5)).astype(BF16); o += FOX_W
    fk_ref[...] = u[:, o:o + FOX_W].astype(BF16); o += FOX_W
    fv_ref[...] = u[:, o:o + FOX_W].astype(BF16); o += FOX_W
    rq = u[:, o:o + RET_QK_W]; o += RET_QK_W
    rk = u[:, o:o + RET_QK_W]; o += RET_QK_W
    rv_ref[...] = u[:, o:o + RET_V_W].astype(BF16); o += RET_V_W
    rg_ref[...] = u[:, o:o + RET_V_W]

    z = _dot(xn, wf_ref[...]) + bf_ref[...]
    lf_ref[...] = jnp.minimum(z, 0.0) - jnp.log1p(jnp.exp(-jnp.abs(z)))

    half = RET_KEY_DIM // 2
    lane = lax.broadcasted_iota(jnp.int32, rq.shape, 1)
    first = (lane % RET_KEY_DIM) < half
    cos = cos_ref[...]
    sin = sin_ref[...]

    def rot(t):
        partner = jnp.where(first, -pltpu.roll(t, RET_QK_W - half, 1), pltpu.roll(t, half, 1))
        return t * cos + partner * sin

    rq_ref[...] = rot(rq).astype(BF16)
    rk_ref[...] = (rot(rk) * (RET_KEY_DIM ** -0.5)).astype(BF16)


def _inproj(x2, g, w_main, w_f, b_f, cos_t, sin_t, seq, tm):
    n, d = x2.shape
    wn = w_main.shape[1]
    spt = seq // tm
    row = lambda i: (i, 0)
    const = lambda i: (0, 0)
    pos = lambda i: (i % spt, 0)
    out_shape = [
        jax.ShapeDtypeStruct((n, FOX_W), BF16), jax.ShapeDtypeStruct((n, FOX_W), BF16),
        jax.ShapeDtypeStruct((n, FOX_W), BF16), jax.ShapeDtypeStruct((n, LANES), F32),
        jax.ShapeDtypeStruct((n, RET_QK_W), BF16), jax.ShapeDtypeStruct((n, RET_QK_W), BF16),
        jax.ShapeDtypeStruct((n, RET_V_W), BF16), jax.ShapeDtypeStruct((n, RET_V_W), F32),
    ]
    out_specs = [
        pl.BlockSpec((tm, FOX_W), row), pl.BlockSpec((tm, FOX_W), row), pl.BlockSpec((tm, FOX_W), row),
        pl.BlockSpec((tm, LANES), row),
        pl.BlockSpec((tm, RET_QK_W), row), pl.BlockSpec((tm, RET_QK_W), row),
        pl.BlockSpec((tm, RET_V_W), row), pl.BlockSpec((tm, RET_V_W), row),
    ]
    return pl.pallas_call(
        _inproj_kernel,
        grid=(n // tm,),
        in_specs=[
            pl.BlockSpec((tm, d), row), pl.BlockSpec((1, d), const),
            pl.BlockSpec((d, wn), const), pl.BlockSpec((d, LANES), const), pl.BlockSpec((1, LANES), const),
            pl.BlockSpec((tm, RET_QK_W), pos), pl.BlockSpec((tm, RET_QK_W), pos),
        ],
        out_specs=out_specs,
        out_shape=out_shape,
        compiler_params=_params(1),
        name="inproj",
    )(x2, g, w_main, w_f, b_f, cos_t, sin_t)


def _cumsum_kernel(x_ref, o_ref):
    x = x_ref[0]
    s = x.shape[1]
    lane = lax.broadcasted_iota(jnp.int32, x.shape, 1)
    k = 1
    while k < s:
        x = x + jnp.where(lane >= k, pltpu.roll(x, k, 1), 0.0)
        k *= 2
    o_ref[0] = x


def _cumsum(lf_t):
    b, h, s = lf_t.shape
    return pl.pallas_call(
        _cumsum_kernel,
        grid=(b,),
        in_specs=[pl.BlockSpec((1, h, s), lambda i: (i, 0, 0))],
        out_specs=pl.BlockSpec((1, h, s), lambda i: (i, 0, 0)),
        out_shape=jax.ShapeDtypeStruct((b, h, s), F32),
        compiler_params=_params(1),
        name="forget_cumsum",
    )(lf_t)


def _fox_kernel(q_ref, k_ref, v_ref, cc_ref, cr_ref, o_ref, m_scr, l_scr, acc_scr, *, tq):
    qi = pl.program_id(2)
    q = q_ref[0]
    lane = lax.broadcasted_iota(jnp.int32, q.shape, 1)
    zero = jnp.zeros_like(q)
    qm = (jnp.where(lane < FOX_HEAD_DIM, q, zero), jnp.where(lane >= FOX_HEAD_DIM, q, zero))
    cc = cc_ref[0, 0]
    m_scr[...] = jnp.full(m_scr.shape, -jnp.inf, F32)
    l_scr[...] = jnp.zeros(l_scr.shape, F32)
    acc_scr[...] = jnp.zeros(acc_scr.shape, F32)
    row = lax.broadcasted_iota(jnp.int32, (tq, tq), 0)
    col = lax.broadcasted_iota(jnp.int32, (tq, tq), 1)

    def step(j, diagonal):
        start = pl.multiple_of(j * tq, tq)
        k = k_ref[0, pl.ds(start, tq), :]
        v = v_ref[0, pl.ds(start, tq), :]
        cr = cr_ref[0, 0, :, pl.ds(start, tq)]
        for a in range(2):
            s = _dot_nt(qm[a], k) + (cc[:, a:a + 1] - cr[a:a + 1, :])
            if diagonal:
                s = jnp.where(col <= row, s, -jnp.inf)
            m_prev = m_scr[a]
            m_new = jnp.maximum(m_prev, jnp.max(s, axis=-1, keepdims=True))
            alpha = jnp.exp(m_prev - m_new)
            p = jnp.exp(s - m_new)
            l_scr[a] = alpha * l_scr[a] + jnp.sum(p, axis=-1, keepdims=True)
            acc_scr[a] = alpha * acc_scr[a] + _dot(p.astype(BF16), v)
            m_scr[a] = m_new

    def body(j, carry):
        step(j, False)
        return carry

    lax.fori_loop(0, qi, body, 0)
    step(qi, True)
    o0 = acc_scr[0] / l_scr[0]
    o1 = acc_scr[1] / l_scr[1]
    o_ref[0] = jnp.where(lane < FOX_HEAD_DIM, o0, o1).astype(o_ref.dtype)


def _fox(fq, fk, fv, c_col, c_row, tq):
    b, s, w = fq.shape
    pairs = w // LANES
    kern = functools.partial(_fox_kernel, tq=tq)
    return pl.pallas_call(
        kern,
        grid=(b, pairs, s // tq),
        in_specs=[
            pl.BlockSpec((1, tq, LANES), lambda i, h, q: (i, q, h)),
            pl.BlockSpec((1, s, LANES), lambda i, h, q: (i, 0, h)),
            pl.BlockSpec((1, s, LANES), lambda i, h, q: (i, 0, h)),
            pl.BlockSpec((1, 1, tq, 2), lambda i, h, q: (i, h, q, 0)),
            pl.BlockSpec((1, 1, 2, s), lambda i, h, q: (i, h, 0, 0)),
        ],
        out_specs=pl.BlockSpec((1, tq, LANES), lambda i, h, q: (i, q, h)),
        out_shape=jax.ShapeDtypeStruct((b, s, w), BF16),
        scratch_shapes=[pltpu.VMEM((2, tq, 1), F32), pltpu.VMEM((2, tq, 1), F32),
                        pltpu.VMEM((2, tq, LANES), F32)],
        compiler_params=_params(3),
        name="fox_attention",
    )(fq, fk, fv, c_col, c_row)


def _ret_kernel(lg_ref, q_ref, k_ref, v_ref, g_ref, o_ref, state_scr, *, chunk):
    h = pl.program_id(1)
    lg = lg_ref[h]
    s = q_ref.shape[1]
    lane = lax.broadcasted_iota(jnp.int32, (chunk, LANES), 1)
    mine = (lane // RET_KEY_DIM) == (h % 2)
    ri = lax.broadcasted_iota(jnp.int32, (chunk, chunk), 0)
    ci = lax.broadcasted_iota(jnp.int32, (chunk, chunk), 1)
    diff = (ri - ci).astype(F32)
    inner = jnp.where(diff >= 0, jnp.exp(jnp.maximum(diff, 0.0) * lg), 0.0)
    pos = lax.broadcasted_iota(jnp.int32, (chunk, 1), 0).astype(F32)
    q_decay = jnp.exp((pos + 1.0) * lg)
    k_decay = jnp.exp((chunk - 1.0 - pos) * lg)
    chunk_decay = jnp.exp(jnp.full((1, 1), chunk, F32) * lg)
    state_scr[...] = jnp.zeros(state_scr.shape, F32)
    for c in range(s // chunk):
        sl = pl.ds(c * chunk, chunk)
        q = jnp.where(mine, q_ref[0, sl, :], jnp.zeros((), BF16))
        k = jnp.where(mine, k_ref[0, sl, :], jnp.zeros((), BF16))
        v = v_ref[0, sl, :]
        scores = _dot_nt(q, k) * inner
        state = state_scr[...]
        o = _dot(scores.astype(BF16), v) + _dot(q, state.astype(BF16)) * q_decay
        kd = (k.astype(F32) * k_decay).astype(BF16)
        state_scr[...] = state * chunk_decay + lax.dot_general(
            kd, v, (((0,), (0,)), ((), ())), preferred_element_type=F32)
        o = o * lax.rsqrt(jnp.mean(o * o, axis=-1, keepdims=True) + EPS)
        g = g_ref[0, sl, :]
        o_ref[0, sl, :] = (o * (g * _sigmoid(g))).astype(o_ref.dtype)


def _retention(lg, rq, rk, rv, rg, chunk):
    b, s, _ = rq.shape
    kern = functools.partial(_ret_kernel, chunk=chunk)
    qk_spec = pl.BlockSpec((1, s, LANES), lambda i, h, lg_ref: (i, 0, h // 2))
    v_spec = pl.BlockSpec((1, s, LANES), lambda i, h, lg_ref: (i, 0, h))
    return pl.pallas_call(
        kern,
        grid_spec=pltpu.PrefetchScalarGridSpec(
            num_scalar_prefetch=1,
            grid=(b, RET_HEADS),
            in_specs=[qk_spec, qk_spec, v_spec, v_spec],
            out_specs=v_spec,
            scratch_shapes=[pltpu.VMEM((LANES, RET_VAL_DIM), F32)],
        ),
        out_shape=jax.ShapeDtypeStruct((b, s, RET_V_W), BF16),
        compiler_params=_params(2),
        name="retention",
    )(lg, rq, rk, rv, rg)


def _merge_kernel(x_ref, fox_ref, ro_ref, gmix_ref, wmg_ref, bmg_ref, wb_ref, wout_ref, gffn_ref,
                  wr_ref, br_ref, h1_ref, hn_ref, rf_ref, ri_ref, cnt_ref, base_scr):
    i = pl.program_id(0)
    tm, d = x_ref.shape

    @pl.when(i == 0)
    def _():
        base_scr[...] = jnp.zeros(base_scr.shape, F32)

    x = x_ref[...]
    xn = _rms(x, gmix_ref[...]).astype(BF16)
    gate = _sigmoid(_dot(xn, wmg_ref[...]) + bmg_ref[...])
    merged = gate[:, :d] * _dot(fox_ref[...], wb_ref[0]) + gate[:, d:] * _dot(ro_ref[...], wb_ref[1])
    h1 = x + _dot(merged.astype(BF16), wout_ref[...])
    h1_ref[...] = h1
    hn = _rms(h1, gffn_ref[...]).astype(BF16)
    hn_ref[...] = hn

    logits = _dot(hn, wr_ref[...]) + br_ref[...]
    lane = lax.broadcasted_iota(jnp.int32, logits.shape, 1)
    lane_f = lane.astype(F32)
    vals, idxs = [], []
    cur = logits
    for _ in range(TOP_K):
        m = jnp.max(cur, axis=-1, keepdims=True)
        idx = jnp.min(jnp.where(cur == m, lane_f, float(LANES)), axis=-1, keepdims=True)
        vals.append(m)
        idxs.append(idx)
        cur = jnp.where(lane_f == idx, -jnp.inf, cur)
    exps = [jnp.exp(v - vals[0]) for v in vals]
    denom = exps[0] + exps[1] + exps[2] + exps[3]

    onehot = jnp.zeros(logits.shape, F32)
    for idx in idxs:
        onehot = onehot + jnp.where(lane_f == idx, 1.0, 0.0)
    r = lax.broadcasted_iota(jnp.int32, (tm, tm), 0)
    c = lax.broadcasted_iota(jnp.int32, (tm, tm), 1)
    tri = jnp.where(c < r, 1.0, 0.0).astype(BF16)
    before = _dot(tri, onehot.astype(BF16)) + base_scr[...]
    rf = jnp.zeros(logits.shape, F32)
    ri = jnp.zeros(logits.shape, F32)
    for j in range(TOP_K):
        rank = jnp.sum(jnp.where(lane_f == idxs[j], before, 0.0), axis=-1, keepdims=True)
        rf = jnp.where(lane == j, exps[j] / denom, rf)
        ri = jnp.where(lane == j, idxs[j], ri)
        ri = jnp.where(lane == TOP_K + j, rank, ri)
    rf_ref[...] = rf
    ri_ref[...] = ri.astype(jnp.int32)
    base_scr[...] = base_scr[...] + jnp.sum(onehot, axis=0, keepdims=True)
    cnt_ref[...] = base_scr[...]


def _merge(x2, fox, ro, gmix, wmg, bmg, wb, wout, gffn, wr, br, tm):
    n, d = x2.shape
    row = lambda i: (i, 0)
    const = lambda i: (0, 0)
    return pl.pallas_call(
        _merge_kernel,
        grid=(n // tm,),
        in_specs=[
            pl.BlockSpec((tm, d), row), pl.BlockSpec((tm, FOX_W), row), pl.BlockSpec((tm, RET_V_W), row),
            pl.BlockSpec((1, d), const), pl.BlockSpec((d, 2 * d), const), pl.BlockSpec((1, 2 * d), const),
            pl.BlockSpec((2, FOX_W, d), lambda i: (0, 0, 0)), pl.BlockSpec((d, d), const),
            pl.BlockSpec((1, d), const), pl.BlockSpec((d, LANES), const), pl.BlockSpec((1, LANES), const),
        ],
        out_specs=[
            pl.BlockSpec((tm, d), row), pl.BlockSpec((tm, d), row),
            pl.BlockSpec((tm, LANES), row), pl.BlockSpec((tm, LANES), row),
            pl.BlockSpec((1, LANES), const),
        ],
        out_shape=[
            jax.ShapeDtypeStruct((n, d), F32), jax.ShapeDtypeStruct((n, d), BF16),
            jax.ShapeDtypeStruct((n, LANES), F32), jax.ShapeDtypeStruct((n, LANES), jnp.int32),
            jax.ShapeDtypeStruct((1, LANES), F32),
        ],
        scratch_shapes=[pltpu.VMEM((1, LANES), F32)],
        compiler_params=_params(1),
        name="merge_router",
    )(x2, fox, ro, gmix, wmg, bmg, wb, wout, gffn, wr, br)


def _expert_kernel(be_ref, x_ref, wgu_ref, bgu_ref, wd_ref, bd_ref, y_ref):
    f = wd_ref.shape[1]
    gu = _dot(x_ref[...], wgu_ref[0]) + bgu_ref[0]
    glu = jnp.minimum(gu[:, :f], SWIGLU_LIMIT)
    lin = jnp.clip(gu[:, f:], -SWIGLU_LIMIT, SWIGLU_LIMIT)
    act = glu * _sigmoid(SWIGLU_ALPHA * glu) * (lin + 1.0)
    y_ref[...] = _dot(act.astype(BF16), wd_ref[0]) + bd_ref[0]


def _experts(block_e, x_buf, wgu, bgu, wd, bd, bm):
    p, d = x_buf.shape
    f = wd.shape[1]
    return pl.pallas_call(
        _expert_kernel,
        grid_spec=pltpu.PrefetchScalarGridSpec(
            num_scalar_prefetch=1,
            grid=(p // bm,),
            in_specs=[
                pl.BlockSpec((bm, d), lambda i, be: (i, 0)),
                pl.BlockSpec((1, d, 2 * f), lambda i, be: (be[i], 0, 0)),
                pl.BlockSpec((1, 1, 2 * f), lambda i, be: (be[i], 0, 0)),
                pl.BlockSpec((1, f, d), lambda i, be: (be[i], 0, 0)),
                pl.BlockSpec((1, 1, d), lambda i, be: (be[i], 0, 0)),
            ],
            out_specs=pl.BlockSpec((bm, d), lambda i, be: (i, 0)),
        ),
        out_shape=jax.ShapeDtypeStruct((p, d), F32),
        compiler_params=_params(1),
        name="expert_ffn",
    )(block_e, x_buf, wgu, bgu, wd, bd)


def _final_kernel(h1_ref, yg_ref, rf_ref, p_ref, gple_ref, wpg_ref, wpp_ref, gfin_ref, o_ref):
    rf = rf_ref[...]
    h2 = h1_ref[...]
    for j in range(TOP_K):
        h2 = h2 + yg_ref[j] * rf[:, j:j + 1]
    gate = _sigmoid(_dot(_rms(h2, gple_ref[...]).astype(BF16), wpg_ref[...]))
    h3 = h2 + gate * _dot(p_ref[...].astype(BF16), wpp_ref[...])
    o_ref[...] = _rms(h3, gfin_ref[...])


def _final(h1, yg, rf, p2, gple, wpg, wpp, gfin, tm):
    n, d = h1.shape
    pd = p2.shape[1]
    row = lambda i: (i, 0)
    const = lambda i: (0, 0)
    return pl.pallas_call(
        _final_kernel,
        grid=(n // tm,),
        in_specs=[
            pl.BlockSpec((tm, d), row), pl.BlockSpec((TOP_K, tm, d), lambda i: (0, i, 0)),
            pl.BlockSpec((tm, LANES), row), pl.BlockSpec((tm, pd), row),
            pl.BlockSpec((1, d), const), pl.BlockSpec((d, d), const), pl.BlockSpec((pd, d), const),
            pl.BlockSpec((1, d), const),
        ],
        out_specs=pl.BlockSpec((tm, d), row),
        out_shape=jax.ShapeDtypeStruct((n, d), F32),
        compiler_params=_params(1),
        name="combine_ple_norm",
    )(h1, yg, rf, p2, gple, wpg, wpp, gfin)


def _layer(h, p, mix_norm, w_in, b_forget, w_branch, w_merge_gate, b_merge_gate, w_out, ffn_norm,
           w_router, b_router, w_gate_up, b_gate_up, w_down, b_down, ple_norm, w_ple_gate, w_ple_proj,
           final_norm, *, tm, tq, chunk, bm):
    b, s, d = h.shape
    n = b * s
    x2 = h.reshape(n, d)
    row = lambda t: t.reshape(1, -1)

    c0 = 3 * FOX_W
    w_main = jnp.concatenate([w_in[:, :c0], w_in[:, c0 + FOX_HEADS:]], axis=1).astype(BF16)
    w_f = jnp.pad(w_in[:, c0:c0 + FOX_HEADS], ((0, 0), (0, LANES - FOX_HEADS))).astype(BF16)
    b_f = jnp.pad(b_forget, (0, LANES - FOX_HEADS)).reshape(1, LANES)
    half = RET_KEY_DIM // 2
    inv = ROPE_BASE ** (-jnp.arange(half, dtype=F32) / half)
    ang = jnp.arange(s).astype(F32)[:, None] * inv[None, :]
    cos_t = jnp.tile(jnp.cos(ang), (1, RET_QK_W // half))
    sin_t = jnp.tile(jnp.sin(ang), (1, RET_QK_W // half))

    fq, fk, fv, lf, rq, rk, rv, rg = _inproj(x2, row(mix_norm), w_main, w_f, b_f, cos_t, sin_t, s, tm)

    lf_t = lf[:, :FOX_HEADS].reshape(b, s, FOX_HEADS).transpose(0, 2, 1)
    c = _cumsum(lf_t)
    pairs = FOX_HEADS // 2
    c_row = c.reshape(b, pairs, 2, s)
    c_col = c_row.transpose(0, 1, 3, 2)
    fox = _fox(fq.reshape(b, s, FOX_W), fk.reshape(b, s, FOX_W), fv.reshape(b, s, FOX_W), c_col, c_row, tq)

    lg = jnp.log1p(-jnp.exp2(-5.0 - jnp.arange(RET_HEADS, dtype=F32)))
    ro = _retention(lg, rq.reshape(b, s, RET_QK_W), rk.reshape(b, s, RET_QK_W),
                    rv.reshape(b, s, RET_V_W), rg.reshape(b, s, RET_V_W), chunk)

    w_r = jnp.pad(w_router, ((0, 0), (0, LANES - N_EXPERTS))).astype(BF16)
    b_r = jnp.concatenate([b_router, jnp.full((LANES - N_EXPERTS,), -1e30, F32)]).reshape(1, LANES)
    h1, hn, rf, ri, cnt = _merge(
        x2, fox.reshape(n, FOX_W), ro.reshape(n, RET_V_W), row(mix_norm), w_merge_gate.astype(BF16),
        row(b_merge_gate), w_branch.astype(BF16), w_out.astype(BF16), row(ffn_norm), w_r, b_r, tm)

    a = n * TOP_K
    nb = -(-(a + N_EXPERTS * (bm - 1)) // bm)
    pr = nb * bm
    e_idx = ri[:, :TOP_K]
    rank = ri[:, TOP_K:2 * TOP_K]
    counts = cnt[0, :N_EXPERTS].astype(jnp.int32)
    padded = (counts + bm - 1) // bm * bm
    pad_end = jnp.cumsum(padded)
    pad_start = pad_end - padded
    dest = pad_start[e_idx] + rank
    block_start = jnp.arange(nb, dtype=jnp.int32) * bm
    block_e = jnp.minimum(jnp.sum(block_start[:, None] >= pad_end[None, :], axis=1),
                          N_EXPERTS - 1).astype(jnp.int32)
    tok = jnp.broadcast_to(jnp.arange(n, dtype=jnp.int32)[:, None], (n, TOP_K))
    buf_tok = jnp.zeros((pr,), jnp.int32).at[dest.reshape(a)].set(tok.reshape(a))

    ne, dd, f2 = w_gate_up.shape
    f = f2 // 2
    wgu = w_gate_up.reshape(ne, dd, f, 2).transpose(0, 1, 3, 2).reshape(ne, dd, f2).astype(BF16)
    bgu = b_gate_up.reshape(ne, f, 2).transpose(0, 2, 1).reshape(ne, 1, f2)
    x_buf = jnp.take(hn, buf_tok, axis=0)
    y_buf = _experts(block_e, x_buf, wgu, bgu, w_down.astype(BF16), b_down.reshape(ne, 1, dd), bm)
    yg = jnp.take(y_buf, dest.T, axis=0)

    out = _final(h1, yg, rf, p.reshape(n, -1), row(ple_norm), w_ple_gate.astype(BF16),
                 w_ple_proj.astype(BF16), row(final_norm), tm // 2)
    return out.reshape(b, s, d)


def kernel(x, p, mix_norm, w_in, b_forget, w_branch, w_merge_gate, b_merge_gate, w_out, ffn_norm, w_router,
           b_router, w_gate_up, b_gate_up, w_down, b_down, ple_norm, w_ple_gate, w_ple_proj, final_norm):
    depth = p.shape[0]
    assert depth == 1, "the final norm is fused into the (single) layer"
    return _layer(x, p[0], mix_norm[0], w_in[0], b_forget[0], w_branch[0], w_merge_gate[0], b_merge_gate[0],
                  w_out[0], ffn_norm[0], w_router[0], b_router[0], w_gate_up[0], b_gate_up[0], w_down[0],
                  b_down[0], ple_norm[0], w_ple_gate[0], w_ple_proj[0], final_norm,
                  tm=512, tq=256, chunk=256, bm=512)
```

```python
import functools

import numpy as np
import jax
import jax.numpy as jnp
from jax import lax
from jax.experimental import pallas as pl
from jax.experimental.pallas import tpu as pltpu

FOX_HEADS = 8
FOX_HEAD_DIM = 64
RET_HEADS = 4
RET_KEY_DIM = 64
RET_VAL_DIM = 128
ROPE_BASE = 10000.0
N_EXPERTS = 32
TOP_K = 4
SWIGLU_LIMIT = 7.0
SWIGLU_ALPHA = 1.702
EPS = 1e-6

LANES = 128
FOX_W = FOX_HEADS * FOX_HEAD_DIM
RET_QK_W = RET_HEADS * RET_KEY_DIM
RET_V_W = RET_HEADS * RET_VAL_DIM
VMEM_LIMIT = 56 * 1024 * 1024

F32 = jnp.float32
BF16 = jnp.bfloat16


def _rms(x, g):
    return x * lax.rsqrt(jnp.mean(x * x, axis=-1, keepdims=True) + EPS) * g


def _sigmoid(x):
    return 1.0 / (1.0 + jnp.exp(-x))


def _dot(a, b):
    return jnp.dot(a, b, preferred_element_type=F32)


def _dot_nt(a, b):
    return lax.dot_general(a, b, (((1,), (1,)), ((), ())), preferred_element_type=F32)


def _params(n_axes):
    return pltpu.CompilerParams(dimension_semantics=("arbitrary",) * n_axes,
                                vmem_limit_bytes=VMEM_LIMIT)


C_MID = FOX_HEADS
C_LO = 2 * FOX_HEADS
C_ONE = 3 * FOX_HEADS
N_AUG = 3


def _split3(t):
    hi = t.astype(BF16).astype(F32)
    r = t - hi
    mid = r.astype(BF16).astype(F32)
    lo = (r - mid).astype(BF16).astype(F32)
    return hi + pltpu.roll(mid, C_MID, 1) + pltpu.roll(lo, C_LO, 1)


def _inproj_kernel(x_ref, g_ref, w_ref, wvt_ref, wf_ref, bf_ref, cos_ref, sin_ref, pq_ref, pk_ref,
                   q_ref, k_ref, vt_ref, rq_ref, rk_ref, rv_ref, rg_ref, carry_scr, *, tiles_per_seq):
    i = pl.program_id(0)
    tm = x_ref.shape[0]

    @pl.when(i % tiles_per_seq == 0)
    def _():
        carry_scr[...] = jnp.zeros(carry_scr.shape, F32)

    xn = _rms(x_ref[...], g_ref[...]).astype(BF16)
    u = _dot(xn, w_ref[...])
    vt_ref[...] = _dot_nt(wvt_ref[...], xn).astype(BF16)

    lane = lax.broadcasted_iota(jnp.int32, (tm, LANES), 1)
    z = _dot(xn, wf_ref[...]) + bf_ref[...]
    lf = jnp.where(lane < FOX_HEADS, jnp.minimum(z, 0.0) - jnp.log1p(jnp.exp(-jnp.abs(z))), 0.0)
    r = lax.broadcasted_iota(jnp.int32, (tm, tm), 0)
    c = lax.broadcasted_iota(jnp.int32, (tm, tm), 1)
    tri = jnp.where(c <= r, 1.0, 0.0).astype(BF16)
    ps = _dot(tri, _split3(lf).astype(BF16))
    cum = ps + pltpu.roll(ps, LANES - C_MID, 1) + pltpu.roll(ps, LANES - C_LO, 1)
    cum = jnp.where(lane < FOX_HEADS, cum, 0.0) + carry_scr[...]
    carry_scr[...] = cum[tm - 1:tm, :]
    c3 = (_split3(cum) + jnp.where(lane == C_ONE, 1.0, 0.0)).astype(BF16)
    aug_q = _dot(c3, pq_ref[...])
    aug_k = _dot(c3, pk_ref[...])

    own = lane < FOX_HEAD_DIM
    fq = u[:, 0:FOX_W] * (FOX_HEAD_DIM ** -0.5)
    fk = u[:, FOX_W:2 * FOX_W]
    for src, aug, dst in ((fq, aug_q, q_ref), (fk, aug_k, k_ref)):
        for j in range(FOX_W // LANES):
            blk = src[:, j * LANES:(j + 1) * LANES]
            e0, e1 = 2 * j * LANES, (2 * j + 1) * LANES
            dst[:, e0:e0 + LANES] = jnp.where(own, blk, aug[:, e0:e0 + LANES]).astype(BF16)
            dst[:, e1:e1 + LANES] = jnp.where(own, pltpu.roll(blk, FOX_HEAD_DIM, 1),
                                               aug[:, e1:e1 + LANES]).astype(BF16)

    o = 2 * FOX_W
    rq = u[:, o:o + RET_QK_W]; o += RET_QK_W
    rk = u[:, o:o + RET_QK_W]; o += RET_QK_W
    rv_ref[...] = u[:, o:o + RET_V_W].astype(BF16); o += RET_V_W
    rg_ref[...] = u[:, o:o + RET_V_W]

    half = RET_KEY_DIM // 2
    lane_r = lax.broadcasted_iota(jnp.int32, rq.shape, 1)
    first = (lane_r % RET_KEY_DIM) < half
    cos = cos_ref[...]
    sin = sin_ref[...]

    def rot(t):
        partner = jnp.where(first, -pltpu.roll(t, RET_QK_W - half, 1), pltpu.roll(t, half, 1))
        return t * cos + partner * sin

    rq_ref[...] = rot(rq).astype(BF16)
    rk_ref[...] = (rot(rk) * (RET_KEY_DIM ** -0.5)).astype(BF16)


def _placement():
    pq = np.zeros((LANES, FOX_HEADS * LANES), np.float32)
    pk = np.zeros((LANES, FOX_HEADS * LANES), np.float32)
    for h in range(FOX_HEADS):
        base = h * LANES + FOX_HEAD_DIM
        for part, src in enumerate((h, C_MID + h, C_LO + h)):
            pq[C_ONE, base + part] = 1.0
            pq[src, base + N_AUG + part] = 1.0
            pk[src, base + part] = -1.0
            pk[C_ONE, base + N_AUG + part] = 1.0
    return jnp.asarray(pq, BF16), jnp.asarray(pk, BF16)


def _inproj(x2, g, w_main, w_vt, w_f, b_f, cos_t, sin_t, seq, tm):
    n, d = x2.shape
    wn = w_main.shape[1]
    spt = seq // tm
    aw = FOX_HEADS * LANES
    pq, pk = _placement()
    row = lambda i: (i, 0)
    const = lambda i: (0, 0)
    pos = lambda i: (i % spt, 0)
    out_shape = [
        jax.ShapeDtypeStruct((n, aw), BF16), jax.ShapeDtypeStruct((n, aw), BF16),
        jax.ShapeDtypeStruct((FOX_W, n), BF16),
        jax.ShapeDtypeStruct((n, RET_QK_W), BF16), jax.ShapeDtypeStruct((n, RET_QK_W), BF16),
        jax.ShapeDtypeStruct((n, RET_V_W), BF16), jax.ShapeDtypeStruct((n, RET_V_W), F32),
    ]
    out_specs = [
        pl.BlockSpec((tm, aw), row), pl.BlockSpec((tm, aw), row),
        pl.BlockSpec((FOX_W, tm), lambda i: (0, i)),
        pl.BlockSpec((tm, RET_QK_W), row), pl.BlockSpec((tm, RET_QK_W), row),
        pl.BlockSpec((tm, RET_V_W), row), pl.BlockSpec((tm, RET_V_W), row),
    ]
    return pl.pallas_call(
        functools.partial(_inproj_kernel, tiles_per_seq=spt),
        grid=(n // tm,),
        in_specs=[
            pl.BlockSpec((tm, d), row), pl.BlockSpec((1, d), const),
            pl.BlockSpec((d, wn), const), pl.BlockSpec((FOX_W, d), const),
            pl.BlockSpec((d, LANES), const), pl.BlockSpec((1, LANES), const),
            pl.BlockSpec((tm, RET_QK_W), pos), pl.BlockSpec((tm, RET_QK_W), pos),
            pl.BlockSpec((LANES, aw), const), pl.BlockSpec((LANES, aw), const),
        ],
        out_specs=out_specs,
        out_shape=out_shape,
        scratch_shapes=[pltpu.VMEM((1, LANES), F32)],
        compiler_params=_params(1),
        name="inproj",
    )(x2, g, w_main, w_vt, w_f, b_f, cos_t, sin_t, pq, pk)


def _fox_kernel(q_ref, k_ref, vt_ref, o_ref, m_scr, acc_scr, *, tq):
    qi = pl.program_id(2)
    m_scr[...] = jnp.full(m_scr.shape, -jnp.inf, F32)
    acc_scr[...] = jnp.zeros(acc_scr.shape, F32)
    key = lax.broadcasted_iota(jnp.int32, (tq, tq), 0)
    qry = lax.broadcasted_iota(jnp.int32, (tq, tq), 1)
    vrow = lax.broadcasted_iota(jnp.int32, (LANES, tq), 0)
    one = jnp.ones((), BF16)

    def step(j, diagonal):
        start = pl.multiple_of(j * tq, tq)
        v = vt_ref[:, pl.ds(start, tq)]
        for a in range(2):
            sl = slice(a * LANES, (a + 1) * LANES)
            s = _dot_nt(k_ref[pl.ds(start, tq), sl], q_ref[:, sl])
            if diagonal:
                s = jnp.where(key <= qry, s, -jnp.inf)
            m_prev = m_scr[a]
            m_new = jnp.maximum(m_prev, jnp.max(s, axis=0, keepdims=True))
            alpha = jnp.exp(m_prev - m_new)
            p = jnp.exp(s - m_new).astype(BF16)
            va = jnp.where((vrow // FOX_HEAD_DIM) == a, v, one)
            acc_scr[a] = alpha * acc_scr[a] + _dot(va, p)
            m_scr[a] = m_new

    def body(j, carry):
        step(j, False)
        return carry

    lax.fori_loop(0, qi, body, 0)
    step(qi, True)
    acc0 = acc_scr[0]
    acc1 = acc_scr[1]
    ot = jnp.where(vrow < FOX_HEAD_DIM, acc0 / acc0[FOX_HEAD_DIM:FOX_HEAD_DIM + 1, :], acc1 / acc1[0:1, :])
    o_ref[...] = ot.T.astype(o_ref.dtype)


def _fox(q_aug, k_aug, v_t, batch, seq, tq):
    n = q_aug.shape[0]
    pairs = FOX_HEADS // 2
    nq = seq // tq
    kern = functools.partial(_fox_kernel, tq=tq)
    return pl.pallas_call(
        kern,
        grid=(batch, pairs, nq),
        in_specs=[
            pl.BlockSpec((tq, 2 * LANES), lambda i, h, q: (i * nq + q, h)),
            pl.BlockSpec((seq, 2 * LANES), lambda i, h, q: (i, h)),
            pl.BlockSpec((LANES, seq), lambda i, h, q: (h, i)),
        ],
        out_specs=pl.BlockSpec((tq, LANES), lambda i, h, q: (i * nq + q, h)),
        out_shape=jax.ShapeDtypeStruct((n, FOX_W), BF16),
        scratch_shapes=[pltpu.VMEM((2, 1, tq), F32), pltpu.VMEM((2, LANES, tq), F32)],
        compiler_params=_params(3),
        name="fox_attention",
    )(q_aug, k_aug, v_t)


def _ret_kernel(lg_ref, q_ref, k_ref, v_ref, g_ref, o_ref, state_scr, *, chunk):
    h = pl.program_id(1)
    lg = lg_ref[h]
    s = q_ref.shape[1]
    lane = lax.broadcasted_iota(jnp.int32, (chunk, LANES), 1)
    mine = (lane // RET_KEY_DIM) == (h % 2)
    ri = lax.broadcasted_iota(jnp.int32, (chunk, chunk), 0)
    ci = lax.broadcasted_iota(jnp.int32, (chunk, chunk), 1)
    diff = (ri - ci).astype(F32)
    inner = jnp.where(diff >= 0, jnp.exp(jnp.maximum(diff, 0.0) * lg), 0.0)
    pos = lax.broadcasted_iota(jnp.int32, (chunk, 1), 0).astype(F32)
    q_decay = jnp.exp((pos + 1.0) * lg)
    k_decay = jnp.exp((chunk - 1.0 - pos) * lg)
    chunk_decay = jnp.exp(jnp.full((1, 1), chunk, F32) * lg)
    state_scr[...] = jnp.zeros(state_scr.shape, F32)
    for c in range(s // chunk):
        sl = pl.ds(c * chunk, chunk)
        q = jnp.where(mine, q_ref[0, sl, :], jnp.zeros((), BF16))
        k = jnp.where(mine, k_ref[0, sl, :], jnp.zeros((), BF16))
        v = v_ref[0, sl, :]
        scores = _dot_nt(q, k) * inner
        state = state_scr[...]
        o = _dot(scores.astype(BF16), v) + _dot(q, state.astype(BF16)) * q_decay
        kd = (k.astype(F32) * k_decay).astype(BF16)
        state_scr[...] = state * chunk_decay + lax.dot_general(
            kd, v, (((0,), (0,)), ((), ())), preferred_element_type=F32)
        o = o * lax.rsqrt(jnp.mean(o * o, axis=-1, keepdims=True) + EPS)
        g = g_ref[0, sl, :]
        o_ref[0, sl, :] = (o * (g * _sigmoid(g))).astype(o_ref.dtype)


def _retention(lg, rq, rk, rv, rg, chunk):
    b, s, _ = rq.shape
    kern = functools.partial(_ret_kernel, chunk=chunk)
    qk_spec = pl.BlockSpec((1, s, LANES), lambda i, h, lg_ref: (i, 0, h // 2))
    v_spec = pl.BlockSpec((1, s, LANES), lambda i, h, lg_ref: (i, 0, h))
    return pl.pallas_call(
        kern,
        grid_spec=pltpu.PrefetchScalarGridSpec(
            num_scalar_prefetch=1,
            grid=(b, RET_HEADS),
            in_specs=[qk_spec, qk_spec, v_spec, v_spec],
            out_specs=v_spec,
            scratch_shapes=[pltpu.VMEM((LANES, RET_VAL_DIM), F32)],
        ),
        out_shape=jax.ShapeDtypeStruct((b, s, RET_V_W), BF16),
        compiler_params=_params(2),
        name="retention",
    )(lg, rq, rk, rv, rg)


def _merge_kernel(x_ref, fox_ref, ro_ref, gmix_ref, wmg_ref, bmg_ref, wb_ref, wout_ref, gffn_ref,
                  wr_ref, br_ref, h1_ref, hn_ref, rf_ref, ri_ref, cnt_ref, base_scr):
    i = pl.program_id(0)
    tm, d = x_ref.shape

    @pl.when(i == 0)
    def _():
        base_scr[...] = jnp.zeros(base_scr.shape, F32)

    x = x_ref[...]
    xn = _rms(x, gmix_ref[...]).astype(BF16)
    gate = _sigmoid(_dot(xn, wmg_ref[...]) + bmg_ref[...])
    merged = gate[:, :d] * _dot(fox_ref[...], wb_ref[0]) + gate[:, d:] * _dot(ro_ref[...], wb_ref[1])
    h1 = x + _dot(merged.astype(BF16), wout_ref[...])
    h1_ref[...] = h1
    hn = _rms(h1, gffn_ref[...]).astype(BF16)
    hn_ref[...] = hn

    logits = _dot(hn, wr_ref[...]) + br_ref[...]
    lane = lax.broadcasted_iota(jnp.int32, logits.shape, 1)
    lane_f = lane.astype(F32)
    vals, idxs = [], []
    cur = logits
    for _ in range(TOP_K):
        m = jnp.max(cur, axis=-1, keepdims=True)
        idx = jnp.min(jnp.where(cur == m, lane_f, float(LANES)), axis=-1, keepdims=True)
        vals.append(m)
        idxs.append(idx)
        cur = jnp.where(lane_f == idx, -jnp.inf, cur)
    exps = [jnp.exp(v - vals[0]) for v in vals]
    denom = exps[0] + exps[1] + exps[2] + exps[3]

    onehot = jnp.zeros(logits.shape, F32)
    for idx in idxs:
        onehot = onehot + jnp.where(lane_f == idx, 1.0, 0.0)
    r = lax.broadcasted_iota(jnp.int32, (tm, tm), 0)
    c = lax.broadcasted_iota(jnp.int32, (tm, tm), 1)
    tri = jnp.where(c < r, 1.0, 0.0).astype(BF16)
    before = _dot(tri, onehot.astype(BF16)) + base_scr[...]
    rf = jnp.zeros(logits.shape, F32)
    ri = jnp.zeros(logits.shape, F32)
    for j in range(TOP_K):
        rank = jnp.sum(jnp.where(lane_f == idxs[j], before, 0.0), axis=-1, keepdims=True)
        rf = jnp.where(lane == j, exps[j] / denom, rf)
        ri = jnp.where(lane == j, idxs[j], ri)
        ri = jnp.where(lane == TOP_K + j, rank, ri)
    rf_ref[...] = rf
    ri_ref[...] = ri.astype(jnp.int32)
    base_scr[...] = base_scr[...] + jnp.sum(onehot, axis=0, keepdims=True)
    cnt_ref[...] = base_scr[...]


def _merge(x2, fox, ro, gmix, wmg, bmg, wb, wout, gffn, wr, br, tm):
    n, d = x2.shape
    row = lambda i: (i, 0)
    const = lambda i: (0, 0)
    return pl.pallas_call(
        _merge_kernel,
        grid=(n // tm,),
        in_specs=[
            pl.BlockSpec((tm, d), row), pl.BlockSpec((tm, FOX_W), row), pl.BlockSpec((tm, RET_V_W), row),
            pl.BlockSpec((1, d), const), pl.BlockSpec((d, 2 * d), const), pl.BlockSpec((1, 2 * d), const),
            pl.BlockSpec((2, FOX_W, d), lambda i: (0, 0, 0)), pl.BlockSpec((d, d), const),
            pl.BlockSpec((1, d), const), pl.BlockSpec((d, LANES), const), pl.BlockSpec((1, LANES), const),
        ],
        out_specs=[
            pl.BlockSpec((tm, d), row), pl.BlockSpec((tm, d), row),
            pl.BlockSpec((tm, LANES), row), pl.BlockSpec((tm, LANES), row),
            pl.BlockSpec((1, LANES), const),
        ],
        out_shape=[
            jax.ShapeDtypeStruct((n, d), F32), jax.ShapeDtypeStruct((n, d), BF16),
            jax.ShapeDtypeStruct((n, LANES), F32), jax.ShapeDtypeStruct((n, LANES), jnp.int32),
            jax.ShapeDtypeStruct((1, LANES), F32),
        ],
        scratch_shapes=[pltpu.VMEM((1, LANES), F32)],
        compiler_params=_params(1),
        name="merge_router",
    )(x2, fox, ro, gmix, wmg, bmg, wb, wout, gffn, wr, br)


def _expert_kernel(be_ref, x_ref, wgu_ref, bgu_ref, wd_ref, bd_ref, y_ref):
    f = wd_ref.shape[1]
    gu = _dot(x_ref[...], wgu_ref[0]) + bgu_ref[0]
    glu = jnp.minimum(gu[:, :f], SWIGLU_LIMIT)
    lin = jnp.clip(gu[:, f:], -SWIGLU_LIMIT, SWIGLU_LIMIT)
    act = glu * _sigmoid(SWIGLU_ALPHA * glu) * (lin + 1.0)
    y_ref[...] = _dot(act.astype(BF16), wd_ref[0]) + bd_ref[0]


def _experts(block_e, x_buf, wgu, bgu, wd, bd, bm):
    p, d = x_buf.shape
    f = wd.shape[1]
    return pl.pallas_call(
        _expert_kernel,
        grid_spec=pltpu.PrefetchScalarGridSpec(
            num_scalar_prefetch=1,
            grid=(p // bm,),
            in_specs=[
                pl.BlockSpec((bm, d), lambda i, be: (i, 0)),
                pl.BlockSpec((1, d, 2 * f), lambda i, be: (be[i], 0, 0)),
                pl.BlockSpec((1, 1, 2 * f), lambda i, be: (be[i], 0, 0)),
                pl.BlockSpec((1, f, d), lambda i, be: (be[i], 0, 0)),
                pl.BlockSpec((1, 1, d), lambda i, be: (be[i], 0, 0)),
            ],
            out_specs=pl.BlockSpec((bm, d), lambda i, be: (i, 0)),
        ),
        out_shape=jax.ShapeDtypeStruct((p, d), F32),
        compiler_params=_params(1),
        name="expert_ffn",
    )(block_e, x_buf, wgu, bgu, wd, bd)


def _final_kernel(h1_ref, yg_ref, rf_ref, p_ref, gple_ref, wpg_ref, wpp_ref, gfin_ref, o_ref):
    rf = rf_ref[...]
    h2 = h1_ref[...]
    for j in range(TOP_K):
        h2 = h2 + yg_ref[j] * rf[:, j:j + 1]
    gate = _sigmoid(_dot(_rms(h2, gple_ref[...]).astype(BF16), wpg_ref[...]))
    h3 = h2 + gate * _dot(p_ref[...].astype(BF16), wpp_ref[...])
    o_ref[...] = _rms(h3, gfin_ref[...])


def _final(h1, yg, rf, p2, gple, wpg, wpp, gfin, tm):
    n, d = h1.shape
    pd = p2.shape[1]
    row = lambda i: (i, 0)
    const = lambda i: (0, 0)
    return pl.pallas_call(
        _final_kernel,
        grid=(n // tm,),
        in_specs=[
            pl.BlockSpec((tm, d), row), pl.BlockSpec((TOP_K, tm, d), lambda i: (0, i, 0)),
            pl.BlockSpec((tm, LANES), row), pl.BlockSpec((tm, pd), row),
            pl.BlockSpec((1, d), const), pl.BlockSpec((d, d), const), pl.BlockSpec((pd, d), const),
            pl.BlockSpec((1, d), const),
        ],
        out_specs=pl.BlockSpec((tm, d), row),
        out_shape=jax.ShapeDtypeStruct((n, d), F32),
        compiler_params=_params(1),
        name="combine_ple_norm",
    )(h1, yg, rf, p2, gple, wpg, wpp, gfin)


def _layer(h, p, mix_norm, w_in, b_forget, w_branch, w_merge_gate, b_merge_gate, w_out, ffn_norm,
           w_router, b_router, w_gate_up, b_gate_up, w_down, b_down, ple_norm, w_ple_gate, w_ple_proj,
           final_norm, *, tm, tq, chunk, bm):
    b, s, d = h.shape
    n = b * s
    x2 = h.reshape(n, d)
    row = lambda t: t.reshape(1, -1)

    c0 = 3 * FOX_W
    w_main = jnp.concatenate([w_in[:, :2 * FOX_W], w_in[:, c0 + FOX_HEADS:]], axis=1).astype(BF16)
    w_vt = w_in[:, 2 * FOX_W:c0].T.astype(BF16)
    w_f = jnp.pad(w_in[:, c0:c0 + FOX_HEADS], ((0, 0), (0, LANES - FOX_HEADS))).astype(BF16)
    b_f = jnp.pad(b_forget, (0, LANES - FOX_HEADS)).reshape(1, LANES)
    half = RET_KEY_DIM // 2
    inv = ROPE_BASE ** (-jnp.arange(half, dtype=F32) / half)
    ang = jnp.arange(s).astype(F32)[:, None] * inv[None, :]
    cos_t = jnp.tile(jnp.cos(ang), (1, RET_QK_W // half))
    sin_t = jnp.tile(jnp.sin(ang), (1, RET_QK_W // half))

    q_aug, k_aug, v_t, rq, rk, rv, rg = _inproj(x2, row(mix_norm), w_main, w_vt, w_f, b_f, cos_t, sin_t, s, tm)
    fox = _fox(q_aug, k_aug, v_t, b, s, tq)

    lg = jnp.log1p(-jnp.exp2(-5.0 - jnp.arange(RET_HEADS, dtype=F32)))
    ro = _retention(lg, rq.reshape(b, s, RET_QK_W), rk.reshape(b, s, RET_QK_W),
                    rv.reshape(b, s, RET_V_W), rg.reshape(b, s, RET_V_W), chunk)

    w_r = jnp.pad(w_router, ((0, 0), (0, LANES - N_EXPERTS))).astype(BF16)
    b_r = jnp.concatenate([b_router, jnp.full((LANES - N_EXPERTS,), -1e30, F32)]).reshape(1, LANES)
    h1, hn, rf, ri, cnt = _merge(
        x2, fox.reshape(n, FOX_W), ro.reshape(n, RET_V_W), row(mix_norm), w_merge_gate.astype(BF16),
        row(b_merge_gate), w_branch.astype(BF16), w_out.astype(BF16), row(ffn_norm), w_r, b_r, tm)

    a = n * TOP_K
    nb = -(-(a + N_EXPERTS * (bm - 1)) // bm)
    pr = nb * bm
    e_idx = ri[:, :TOP_K]
    rank = ri[:, TOP_K:2 * TOP_K]
    counts = cnt[0, :N_EXPERTS].astype(jnp.int32)
    padded = (counts + bm - 1) // bm * bm
    pad_end = jnp.cumsum(padded)
    pad_start = pad_end - padded
    dest = pad_start[e_idx] + rank
    block_start = jnp.arange(nb, dtype=jnp.int32) * bm
    block_e = jnp.minimum(jnp.sum(block_start[:, None] >= pad_end[None, :], axis=1),
                          N_EXPERTS - 1).astype(jnp.int32)
    tok = jnp.broadcast_to(jnp.arange(n, dtype=jnp.int32)[:, None], (n, TOP_K))
    buf_tok = jnp.zeros((pr,), jnp.int32).at[dest.reshape(a)].set(tok.reshape(a))

    ne, dd, f2 = w_gate_up.shape
    f = f2 // 2
    wgu = w_gate_up.reshape(ne, dd, f, 2).transpose(0, 1, 3, 2).reshape(ne, dd, f2).astype(BF16)
    bgu = b_gate_up.reshape(ne, f, 2).transpose(0, 2, 1).reshape(ne, 1, f2)
    x_buf = jnp.take(hn, buf_tok, axis=0)
    y_buf = _experts(block_e, x_buf, wgu, bgu, w_down.astype(BF16), b_down.reshape(ne, 1, dd), bm)
    yg = jnp.take(y_buf, dest.T, axis=0)

    out = _final(h1, yg, rf, p.reshape(n, -1), row(ple_norm), w_ple_gate.astype(BF16),
                 w_ple_proj.astype(BF16), row(final_norm), tm // 2)
    return out.reshape(b, s, d)


def kernel(x, p, mix_norm, w_in, b_forget, w_branch, w_merge_gate, b_merge_gate, w_out, ffn_norm, w_router,
           b_router, w_gate_up, b_gate_up, w_down, b_down, ple_norm, w_ple_gate, w_ple_proj, final_norm):
    depth = p.shape[0]
    assert depth == 1, "the final norm is fused into the (single) layer"
    return _layer(x, p[0], mix_norm[0], w_in[0], b_forget[0], w_branch[0], w_merge_gate[0], b_merge_gate[0],
                  w_out[0], ffn_norm[0], w_router[0], b_router[0], w_gate_up[0], b_gate_up[0], w_down[0],
                  b_down[0], ple_norm[0], w_ple_gate[0], w_ple_proj[0], final_norm,
                  tm=512, tq=256, chunk=256, bm=512)
```

```python
import functools

import numpy as np
import jax
import jax.numpy as jnp
from jax import lax
from jax.experimental import pallas as pl
from jax.experimental.pallas import tpu as pltpu

FOX_HEADS = 8
FOX_HEAD_DIM = 64
RET_HEADS = 4
RET_KEY_DIM = 64
RET_VAL_DIM = 128
ROPE_BASE = 10000.0
N_EXPERTS = 32
TOP_K = 4
SWIGLU_LIMIT = 7.0
SWIGLU_ALPHA = 1.702
EPS = 1e-6

LANES = 128
FOX_W = FOX_HEADS * FOX_HEAD_DIM
RET_QK_W = RET_HEADS * RET_KEY_DIM
RET_V_W = RET_HEADS * RET_VAL_DIM
VMEM_LIMIT = 56 * 1024 * 1024

F32 = jnp.float32
BF16 = jnp.bfloat16


def _rms(x, g):
    return x * lax.rsqrt(jnp.mean(x * x, axis=-1, keepdims=True) + EPS) * g


def _sigmoid(x):
    return 1.0 / (1.0 + jnp.exp(-x))


def _dot(a, b):
    return jnp.dot(a, b, preferred_element_type=F32)


def _dot_nt(a, b):
    return lax.dot_general(a, b, (((1,), (1,)), ((), ())), preferred_element_type=F32)


def _params(n_axes):
    return pltpu.CompilerParams(dimension_semantics=("arbitrary",) * n_axes,
                                vmem_limit_bytes=VMEM_LIMIT)


C_MID = FOX_HEADS
C_LO = 2 * FOX_HEADS
C_ONE = 3 * FOX_HEADS
N_AUG = 3


def _split3(t):
    hi = t.astype(BF16).astype(F32)
    r = t - hi
    mid = r.astype(BF16).astype(F32)
    lo = (r - mid).astype(BF16).astype(F32)
    return hi + pltpu.roll(mid, C_MID, 1) + pltpu.roll(lo, C_LO, 1)


def _inproj_kernel(x_ref, g_ref, w_ref, wvt_ref, wf_ref, bf_ref, cos_ref, sin_ref, pq_ref, pk_ref,
                   q_ref, k_ref, vt_ref, rq_ref, rk_ref, rv_ref, rg_ref, carry_scr, *, tiles_per_seq):
    i = pl.program_id(0)
    tm = x_ref.shape[0]

    @pl.when(i % tiles_per_seq == 0)
    def _():
        carry_scr[...] = jnp.zeros(carry_scr.shape, F32)

    xn = _rms(x_ref[...], g_ref[...]).astype(BF16)
    u = _dot(xn, w_ref[...])
    vt_ref[...] = _dot_nt(wvt_ref[...], xn).astype(BF16)

    lane = lax.broadcasted_iota(jnp.int32, (tm, LANES), 1)
    z = _dot(xn, wf_ref[...]) + bf_ref[...]
    lf = jnp.where(lane < FOX_HEADS, jnp.minimum(z, 0.0) - jnp.log1p(jnp.exp(-jnp.abs(z))), 0.0)
    r = lax.broadcasted_iota(jnp.int32, (tm, tm), 0)
    c = lax.broadcasted_iota(jnp.int32, (tm, tm), 1)
    tri = jnp.where(c <= r, 1.0, 0.0).astype(BF16)
    ps = _dot(tri, _split3(lf).astype(BF16))
    cum = ps + pltpu.roll(ps, LANES - C_MID, 1) + pltpu.roll(ps, LANES - C_LO, 1)
    cum = jnp.where(lane < FOX_HEADS, cum, 0.0) + carry_scr[...]
    carry_scr[...] = cum[tm - 1:tm, :]
    c3 = (_split3(cum) + jnp.where(lane == C_ONE, 1.0, 0.0)).astype(BF16)
    aug_q = _dot(c3, pq_ref[...])
    aug_k = _dot(c3, pk_ref[...])

    own = lane < FOX_HEAD_DIM
    fq = u[:, 0:FOX_W] * (FOX_HEAD_DIM ** -0.5)
    fk = u[:, FOX_W:2 * FOX_W]
    for src, aug, dst in ((fq, aug_q, q_ref), (fk, aug_k, k_ref)):
        for j in range(FOX_W // LANES):
            blk = src[:, j * LANES:(j + 1) * LANES]
            e0, e1 = 2 * j * LANES, (2 * j + 1) * LANES
            dst[:, e0:e0 + LANES] = jnp.where(own, blk, aug[:, e0:e0 + LANES]).astype(BF16)
            dst[:, e1:e1 + LANES] = jnp.where(own, pltpu.roll(blk, FOX_HEAD_DIM, 1),
                                               aug[:, e1:e1 + LANES]).astype(BF16)

    o = 2 * FOX_W
    rq = u[:, o:o + RET_QK_W]; o += RET_QK_W
    rk = u[:, o:o + RET_QK_W]; o += RET_QK_W
    rv_ref[...] = u[:, o:o + RET_V_W].astype(BF16); o += RET_V_W
    rg_ref[...] = u[:, o:o + RET_V_W]

    half = RET_KEY_DIM // 2
    lane_r = lax.broadcasted_iota(jnp.int32, rq.shape, 1)
    first = (lane_r % RET_KEY_DIM) < half
    cos = cos_ref[...]
    sin = sin_ref[...]

    def rot(t):
        partner = jnp.where(first, -pltpu.roll(t, RET_QK_W - half, 1), pltpu.roll(t, half, 1))
        return t * cos + partner * sin

    rq_ref[...] = rot(rq).astype(BF16)
    rk_ref[...] = (rot(rk) * (RET_KEY_DIM ** -0.5)).astype(BF16)


def _placement():
    pq = np.zeros((LANES, FOX_HEADS * LANES), np.float32)
    pk = np.zeros((LANES, FOX_HEADS * LANES), np.float32)
    for h in range(FOX_HEADS):
        base = h * LANES + FOX_HEAD_DIM
        for part, src in enumerate((h, C_MID + h, C_LO + h)):
            pq[C_ONE, base + part] = 1.0
            pq[src, base + N_AUG + part] = 1.0
            pk[src, base + part] = -1.0
            pk[C_ONE, base + N_AUG + part] = 1.0
    return jnp.asarray(pq, BF16), jnp.asarray(pk, BF16)


def _inproj(x2, g, w_main, w_vt, w_f, b_f, cos_t, sin_t, seq, tm):
    n, d = x2.shape
    wn = w_main.shape[1]
    spt = seq // tm
    aw = FOX_HEADS * LANES
    pq, pk = _placement()
    row = lambda i: (i, 0)
    const = lambda i: (0, 0)
    pos = lambda i: (i % spt, 0)
    out_shape = [
        jax.ShapeDtypeStruct((n, aw), BF16), jax.ShapeDtypeStruct((n, aw), BF16),
        jax.ShapeDtypeStruct((FOX_W, n), BF16),
        jax.ShapeDtypeStruct((n, RET_QK_W), BF16), jax.ShapeDtypeStruct((n, RET_QK_W), BF16),
        jax.ShapeDtypeStruct((n, RET_V_W), BF16), jax.ShapeDtypeStruct((n, RET_V_W), F32),
    ]
    out_specs = [
        pl.BlockSpec((tm, aw), row), pl.BlockSpec((tm, aw), row),
        pl.BlockSpec((FOX_W, tm), lambda i: (0, i)),
        pl.BlockSpec((tm, RET_QK_W), row), pl.BlockSpec((tm, RET_QK_W), row),
        pl.BlockSpec((tm, RET_V_W), row), pl.BlockSpec((tm, RET_V_W), row),
    ]
    return pl.pallas_call(
        functools.partial(_inproj_kernel, tiles_per_seq=spt),
        grid=(n // tm,),
        in_specs=[
            pl.BlockSpec((tm, d), row), pl.BlockSpec((1, d), const),
            pl.BlockSpec((d, wn), const), pl.BlockSpec((FOX_W, d), const),
            pl.BlockSpec((d, LANES), const), pl.BlockSpec((1, LANES), const),
            pl.BlockSpec((tm, RET_QK_W), pos), pl.BlockSpec((tm, RET_QK_W), pos),
            pl.BlockSpec((LANES, aw), const), pl.BlockSpec((LANES, aw), const),
        ],
        out_specs=out_specs,
        out_shape=out_shape,
        scratch_shapes=[pltpu.VMEM((1, LANES), F32)],
        compiler_params=_params(1),
        name="inproj",
    )(x2, g, w_main, w_vt, w_f, b_f, cos_t, sin_t, pq, pk)


def _fox_kernel(q_ref, k_ref, vt_ref, o_ref, *, tq):
    qi = pl.program_id(2)
    key = lax.broadcasted_iota(jnp.int32, (tq, tq), 0)
    qry = lax.broadcasted_iota(jnp.int32, (tq, tq), 1)
    vrow = lax.broadcasted_iota(jnp.int32, (LANES, tq), 0)
    one = jnp.ones((), BF16)

    def step(j, carry, diagonal):
        start = pl.multiple_of(j * tq, tq)
        v = vt_ref[:, pl.ds(start, tq)]
        out = []
        for a in range(2):
            m_prev, acc = carry[a]
            sl = slice(a * LANES, (a + 1) * LANES)
            s = _dot_nt(k_ref[pl.ds(start, tq), sl], q_ref[:, sl])
            if diagonal:
                s = jnp.where(key <= qry, s, -jnp.inf)
            m_new = jnp.maximum(m_prev, jnp.max(s, axis=0, keepdims=True))
            alpha = jnp.exp(m_prev - m_new)
            p = jnp.exp(s - m_new).astype(BF16)
            va = jnp.where((vrow // FOX_HEAD_DIM) == a, v, one)
            out.append((m_new, alpha * acc + _dot(va, p)))
        return tuple(out)

    init = tuple((jnp.full((1, tq), -jnp.inf, F32), jnp.zeros((LANES, tq), F32)) for _ in range(2))
    carry = lax.fori_loop(0, qi, lambda j, c: step(j, c, False), init)
    (_, acc0), (_, acc1) = step(qi, carry, True)
    ot = jnp.where(vrow < FOX_HEAD_DIM, acc0 / acc0[FOX_HEAD_DIM:FOX_HEAD_DIM + 1, :], acc1 / acc1[0:1, :])
    o_ref[...] = ot.T.astype(o_ref.dtype)


def _fox(q_aug, k_aug, v_t, batch, seq, tq):
    n = q_aug.shape[0]
    pairs = FOX_HEADS // 2
    nq = seq // tq
    kern = functools.partial(_fox_kernel, tq=tq)
    return pl.pallas_call(
        kern,
        grid=(batch, pairs, nq),
        in_specs=[
            pl.BlockSpec((tq, 2 * LANES), lambda i, h, q: (i * nq + q, h)),
            pl.BlockSpec((seq, 2 * LANES), lambda i, h, q: (i, h)),
            pl.BlockSpec((LANES, seq), lambda i, h, q: (h, i)),
        ],
        out_specs=pl.BlockSpec((tq, LANES), lambda i, h, q: (i * nq + q, h)),
        out_shape=jax.ShapeDtypeStruct((n, FOX_W), BF16),
        compiler_params=_params(3),
        name="fox_attention",
    )(q_aug, k_aug, v_t)


def _ret_kernel(lg_ref, q_ref, k_ref, v_ref, g_ref, o_ref, state_scr, *, chunk):
    h = pl.program_id(1)
    lg = lg_ref[h]
    s = q_ref.shape[1]
    lane = lax.broadcasted_iota(jnp.int32, (chunk, LANES), 1)
    mine = (lane // RET_KEY_DIM) == (h % 2)
    ri = lax.broadcasted_iota(jnp.int32, (chunk, chunk), 0)
    ci = lax.broadcasted_iota(jnp.int32, (chunk, chunk), 1)
    diff = (ri - ci).astype(F32)
    inner = jnp.where(diff >= 0, jnp.exp(jnp.maximum(diff, 0.0) * lg), 0.0)
    pos = lax.broadcasted_iota(jnp.int32, (chunk, 1), 0).astype(F32)
    q_decay = jnp.exp((pos + 1.0) * lg)
    k_decay = jnp.exp((chunk - 1.0 - pos) * lg)
    chunk_decay = jnp.exp(jnp.full((1, 1), chunk, F32) * lg)
    state_scr[...] = jnp.zeros(state_scr.shape, F32)
    for c in range(s // chunk):
        sl = pl.ds(c * chunk, chunk)
        q = jnp.where(mine, q_ref[0, sl, :], jnp.zeros((), BF16))
        k = jnp.where(mine, k_ref[0, sl, :], jnp.zeros((), BF16))
        v = v_ref[0, sl, :]
        scores = _dot_nt(q, k) * inner
        state = state_scr[...]
        o = _dot(scores.astype(BF16), v) + _dot(q, state.astype(BF16)) * q_decay
        kd = (k.astype(F32) * k_decay).astype(BF16)
        state_scr[...] = state * chunk_decay + lax.dot_general(
            kd, v, (((0,), (0,)), ((), ())), preferred_element_type=F32)
        o = o * lax.rsqrt(jnp.mean(o * o, axis=-1, keepdims=True) + EPS)
        g = g_ref[0, sl, :]
        o_ref[0, sl, :] = (o * (g * _sigmoid(g))).astype(o_ref.dtype)


def _retention(lg, rq, rk, rv, rg, chunk):
    b, s, _ = rq.shape
    kern = functools.partial(_ret_kernel, chunk=chunk)
    qk_spec = pl.BlockSpec((1, s, LANES), lambda i, h, lg_ref: (i, 0, h // 2))
    v_spec = pl.BlockSpec((1, s, LANES), lambda i, h, lg_ref: (i, 0, h))
    return pl.pallas_call(
        kern,
        grid_spec=pltpu.PrefetchScalarGridSpec(
            num_scalar_prefetch=1,
            grid=(b, RET_HEADS),
            in_specs=[qk_spec, qk_spec, v_spec, v_spec],
            out_specs=v_spec,
            scratch_shapes=[pltpu.VMEM((LANES, RET_VAL_DIM), F32)],
        ),
        out_shape=jax.ShapeDtypeStruct((b, s, RET_V_W), BF16),
        compiler_params=_params(2),
        name="retention",
    )(lg, rq, rk, rv, rg)


def _merge_kernel(x_ref, fox_ref, ro_ref, gmix_ref, wmg_ref, bmg_ref, wb_ref, wout_ref, gffn_ref,
                  wr_ref, br_ref, h1_ref, hn_ref, rf_ref, ri_ref, cnt_ref, base_scr):
    i = pl.program_id(0)
    tm, d = x_ref.shape

    @pl.when(i == 0)
    def _():
        base_scr[...] = jnp.zeros(base_scr.shape, F32)

    x = x_ref[...]
    xn = _rms(x, gmix_ref[...]).astype(BF16)
    gate = _sigmoid(_dot(xn, wmg_ref[...]) + bmg_ref[...])
    merged = gate[:, :d] * _dot(fox_ref[...], wb_ref[0]) + gate[:, d:] * _dot(ro_ref[...], wb_ref[1])
    h1 = x + _dot(merged.astype(BF16), wout_ref[...])
    h1_ref[...] = h1
    hn = _rms(h1, gffn_ref[...]).astype(BF16)
    hn_ref[...] = hn

    logits = _dot(hn, wr_ref[...]) + br_ref[...]
    lane = lax.broadcasted_iota(jnp.int32, logits.shape, 1)
    lane_f = lane.astype(F32)
    vals, idxs = [], []
    cur = logits
    for _ in range(TOP_K):
        m = jnp.max(cur, axis=-1, keepdims=True)
        idx = jnp.min(jnp.where(cur == m, lane_f, float(LANES)), axis=-1, keepdims=True)
        vals.append(m)
        idxs.append(idx)
        cur = jnp.where(lane_f == idx, -jnp.inf, cur)
    exps = [jnp.exp(v - vals[0]) for v in vals]
    denom = exps[0] + exps[1] + exps[2] + exps[3]

    onehot = jnp.zeros(logits.shape, F32)
    for idx in idxs:
        onehot = onehot + jnp.where(lane_f == idx, 1.0, 0.0)
    r = lax.broadcasted_iota(jnp.int32, (tm, tm), 0)
    c = lax.broadcasted_iota(jnp.int32, (tm, tm), 1)
    tri = jnp.where(c < r, 1.0, 0.0).astype(BF16)
    before = _dot(tri, onehot.astype(BF16)) + base_scr[...]
    rf = jnp.zeros(logits.shape, F32)
    ri = jnp.zeros(logits.shape, F32)
    for j in range(TOP_K):
        rank = jnp.sum(jnp.where(lane_f == idxs[j], before, 0.0), axis=-1, keepdims=True)
        rf = jnp.where(lane == j, exps[j] / denom, rf)
        ri = jnp.where(lane == j, idxs[j], ri)
        ri = jnp.where(lane == TOP_K + j, rank, ri)
    rf_ref[...] = rf
    ri_ref[...] = ri.astype(jnp.int32)
    base_scr[...] = base_scr[...] + jnp.sum(onehot, axis=0, keepdims=True)
    cnt_ref[...] = base_scr[...]


def _merge(x2, fox, ro, gmix, wmg, bmg, wb, wout, gffn, wr, br, tm):
    n, d = x2.shape
    row = lambda i: (i, 0)
    const = lambda i: (0, 0)
    return pl.pallas_call(
        _merge_kernel,
        grid=(n // tm,),
        in_specs=[
            pl.BlockSpec((tm, d), row), pl.BlockSpec((tm, FOX_W), row), pl.BlockSpec((tm, RET_V_W), row),
            pl.BlockSpec((1, d), const), pl.BlockSpec((d, 2 * d), const), pl.BlockSpec((1, 2 * d), const),
            pl.BlockSpec((2, FOX_W, d), lambda i: (0, 0, 0)), pl.BlockSpec((d, d), const),
            pl.BlockSpec((1, d), const), pl.BlockSpec((d, LANES), const), pl.BlockSpec((1, LANES), const),
        ],
        out_specs=[
            pl.BlockSpec((tm, d), row), pl.BlockSpec((tm, d), row),
            pl.BlockSpec((tm, LANES), row), pl.BlockSpec((tm, LANES), row),
            pl.BlockSpec((1, LANES), const),
        ],
        out_shape=[
            jax.ShapeDtypeStruct((n, d), F32), jax.ShapeDtypeStruct((n, d), BF16),
            jax.ShapeDtypeStruct((n, LANES), F32), jax.ShapeDtypeStruct((n, LANES), jnp.int32),
            jax.ShapeDtypeStruct((1, LANES), F32),
        ],
        scratch_shapes=[pltpu.VMEM((1, LANES), F32)],
        compiler_params=_params(1),
        name="merge_router",
    )(x2, fox, ro, gmix, wmg, bmg, wb, wout, gffn, wr, br)


GLU_GROUP = 2 * LANES


def _expert_kernel(be_ref, x_ref, wgu_ref, bgu_ref, wd_ref, bd_ref, perm_ref, y_ref, wgu_scr, wd_scr):
    i = pl.program_id(0)
    f2 = wgu_ref.shape[2]

    @pl.when(jnp.logical_or(i == 0, be_ref[i] != be_ref[jnp.maximum(i - 1, 0)]))
    def _():
        for b in range(f2 // GLU_GROUP):
            cols = slice(b * GLU_GROUP, (b + 1) * GLU_GROUP)
            wgu_scr[:, cols] = _dot(wgu_ref[0, :, cols].astype(BF16), perm_ref[...]).astype(BF16)
        wd_scr[...] = wd_ref[0].astype(BF16)

    gu = _dot(x_ref[...], wgu_scr[...]) + bgu_ref[0]
    acts = []
    for b in range(f2 // GLU_GROUP):
        glu = jnp.minimum(gu[:, b * GLU_GROUP:b * GLU_GROUP + LANES], SWIGLU_LIMIT)
        lin = jnp.clip(gu[:, b * GLU_GROUP + LANES:(b + 1) * GLU_GROUP], -SWIGLU_LIMIT, SWIGLU_LIMIT)
        acts.append((glu * _sigmoid(SWIGLU_ALPHA * glu) * (lin + 1.0)).astype(BF16))
    act = jnp.concatenate(acts, axis=1)
    y_ref[...] = _dot(act, wd_scr[...]) + bd_ref[0]


def _experts(block_e, x_buf, wgu, bgu, wd, bd, bm):
    p, d = x_buf.shape
    f = wd.shape[1]
    perm = np.zeros((GLU_GROUP, GLU_GROUP), np.float32)
    for c in range(LANES):
        perm[2 * c, c] = 1.0
        perm[2 * c + 1, LANES + c] = 1.0
    return pl.pallas_call(
        _expert_kernel,
        grid_spec=pltpu.PrefetchScalarGridSpec(
            num_scalar_prefetch=1,
            grid=(p // bm,),
            in_specs=[
                pl.BlockSpec((bm, d), lambda i, be: (i, 0)),
                pl.BlockSpec((1, d, 2 * f), lambda i, be: (be[i], 0, 0)),
                pl.BlockSpec((1, 1, 2 * f), lambda i, be: (be[i], 0, 0)),
                pl.BlockSpec((1, f, d), lambda i, be: (be[i], 0, 0)),
                pl.BlockSpec((1, 1, d), lambda i, be: (be[i], 0, 0)),
                pl.BlockSpec((GLU_GROUP, GLU_GROUP), lambda i, be: (0, 0)),
            ],
            out_specs=pl.BlockSpec((bm, d), lambda i, be: (i, 0)),
            scratch_shapes=[pltpu.VMEM((d, 2 * f), BF16), pltpu.VMEM((f, d), BF16)],
        ),
        out_shape=jax.ShapeDtypeStruct((p, d), F32),
        compiler_params=_params(1),
        name="expert_ffn",
    )(block_e, x_buf, wgu, bgu, wd, bd, jnp.asarray(perm, BF16))


def _final_kernel(h1_ref, yg_ref, rf_ref, p_ref, gple_ref, wpg_ref, wpp_ref, gfin_ref, o_ref):
    rf = rf_ref[...]
    h2 = h1_ref[...]
    for j in range(TOP_K):
        h2 = h2 + yg_ref[j] * rf[:, j:j + 1]
    gate = _sigmoid(_dot(_rms(h2, gple_ref[...]).astype(BF16), wpg_ref[...]))
    h3 = h2 + gate * _dot(p_ref[...].astype(BF16), wpp_ref[...])
    o_ref[...] = _rms(h3, gfin_ref[...])


def _final(h1, yg, rf, p2, gple, wpg, wpp, gfin, tm):
    n, d = h1.shape
    pd = p2.shape[1]
    row = lambda i: (i, 0)
    const = lambda i: (0, 0)
    return pl.pallas_call(
        _final_kernel,
        grid=(n // tm,),
        in_specs=[
            pl.BlockSpec((tm, d), row), pl.BlockSpec((TOP_K, tm, d), lambda i: (0, i, 0)),
            pl.BlockSpec((tm, LANES), row), pl.BlockSpec((tm, pd), row),
            pl.BlockSpec((1, d), const), pl.BlockSpec((d, d), const), pl.BlockSpec((pd, d), const),
            pl.BlockSpec((1, d), const),
        ],
        out_specs=pl.BlockSpec((tm, d), row),
        out_shape=jax.ShapeDtypeStruct((n, d), F32),
        compiler_params=_params(1),
        name="combine_ple_norm",
    )(h1, yg, rf, p2, gple, wpg, wpp, gfin)


def _layer(h, p, mix_norm, w_in, b_forget, w_branch, w_merge_gate, b_merge_gate, w_out, ffn_norm,
           w_router, b_router, w_gate_up, b_gate_up, w_down, b_down, ple_norm, w_ple_gate, w_ple_proj,
           final_norm, *, tm, tq, chunk, bm):
    b, s, d = h.shape
    n = b * s
    x2 = h.reshape(n, d)
    row = lambda t: t.reshape(1, -1)

    c0 = 3 * FOX_W
    w_main = jnp.concatenate([w_in[:, :2 * FOX_W], w_in[:, c0 + FOX_HEADS:]], axis=1).astype(BF16)
    w_vt = w_in[:, 2 * FOX_W:c0].T.astype(BF16)
    w_f = jnp.pad(w_in[:, c0:c0 + FOX_HEADS], ((0, 0), (0, LANES - FOX_HEADS))).astype(BF16)
    b_f = jnp.pad(b_forget, (0, LANES - FOX_HEADS)).reshape(1, LANES)
    half = RET_KEY_DIM // 2
    inv = ROPE_BASE ** (-jnp.arange(half, dtype=F32) / half)
    ang = jnp.arange(s).astype(F32)[:, None] * inv[None, :]
    cos_t = jnp.tile(jnp.cos(ang), (1, RET_QK_W // half))
    sin_t = jnp.tile(jnp.sin(ang), (1, RET_QK_W // half))

    q_aug, k_aug, v_t, rq, rk, rv, rg = _inproj(x2, row(mix_norm), w_main, w_vt, w_f, b_f, cos_t, sin_t, s, tm)
    fox = _fox(q_aug, k_aug, v_t, b, s, tq)

    lg = jnp.log1p(-jnp.exp2(-5.0 - jnp.arange(RET_HEADS, dtype=F32)))
    ro = _retention(lg, rq.reshape(b, s, RET_QK_W), rk.reshape(b, s, RET_QK_W),
                    rv.reshape(b, s, RET_V_W), rg.reshape(b, s, RET_V_W), chunk)

    w_r = jnp.pad(w_router, ((0, 0), (0, LANES - N_EXPERTS))).astype(BF16)
    b_r = jnp.concatenate([b_router, jnp.full((LANES - N_EXPERTS,), -1e30, F32)]).reshape(1, LANES)
    h1, hn, rf, ri, cnt = _merge(
        x2, fox.reshape(n, FOX_W), ro.reshape(n, RET_V_W), row(mix_norm), w_merge_gate.astype(BF16),
        row(b_merge_gate), w_branch.astype(BF16), w_out.astype(BF16), row(ffn_norm), w_r, b_r, tm)

    a = n * TOP_K
    nb = -(-(a + N_EXPERTS * (bm - 1)) // bm)
    pr = nb * bm
    e_idx = ri[:, :TOP_K]
    rank = ri[:, TOP_K:2 * TOP_K]
    counts = cnt[0, :N_EXPERTS].astype(jnp.int32)
    padded = (counts + bm - 1) // bm * bm
    pad_end = jnp.cumsum(padded)
    pad_start = pad_end - padded
    dest = pad_start[e_idx] + rank
    block_start = jnp.arange(nb, dtype=jnp.int32) * bm
    block_e = jnp.minimum(jnp.sum(block_start[:, None] >= pad_end[None, :], axis=1),
                          N_EXPERTS - 1).astype(jnp.int32)
    tok = jnp.broadcast_to(jnp.arange(n, dtype=jnp.int32)[:, None], (n, TOP_K))
    buf_tok = jnp.zeros((pr,), jnp.int32).at[dest.reshape(a)].set(tok.reshape(a))

    ne, dd, f2 = w_gate_up.shape
    bgu = b_gate_up.reshape(ne, f2 // GLU_GROUP, LANES, 2).transpose(0, 1, 3, 2).reshape(ne, 1, f2)
    x_buf = jnp.take(hn, buf_tok, axis=0)
    y_buf = _experts(block_e, x_buf, w_gate_up, bgu, w_down, b_down.reshape(ne, 1, dd), bm)
    yg = jnp.take(y_buf, dest.T, axis=0)

    out = _final(h1, yg, rf, p.reshape(n, -1), row(ple_norm), w_ple_gate.astype(BF16),
                 w_ple_proj.astype(BF16), row(final_norm), tm // 2)
    return out.reshape(b, s, d)


def kernel(x, p, mix_norm, w_in, b_forget, w_branch, w_merge_gate, b_merge_gate, w_out, ffn_norm, w_router,
           b_router, w_gate_up, b_gate_up, w_down, b_down, ple_norm, w_ple_gate, w_ple_proj, final_norm):
    depth = p.shape[0]
    assert depth == 1, "the final norm is fused into the (single) layer"
    return _layer(x, p[0], mix_norm[0], w_in[0], b_forget[0], w_branch[0], w_merge_gate[0], b_merge_gate[0],
                  w_out[0], ffn_norm[0], w_router[0], b_router[0], w_gate_up[0], b_gate_up[0], w_down[0],
                  b_down[0], ple_norm[0], w_ple_gate[0], w_ple_proj[0], final_norm,
                  tm=512, tq=512, chunk=256, bm=512)
```

```python
import functools

import numpy as np
import jax
import jax.numpy as jnp
from jax import lax
from jax.experimental import pallas as pl
from jax.experimental.pallas import tpu as pltpu
from jax.experimental.pallas import tpu_sc as plsc

FOX_HEADS = 8
FOX_HEAD_DIM = 64
RET_HEADS = 4
RET_KEY_DIM = 64
RET_VAL_DIM = 128
ROPE_BASE = 10000.0
N_EXPERTS = 32
TOP_K = 4
SWIGLU_LIMIT = 7.0
SWIGLU_ALPHA = 1.702
EPS = 1e-6

LANES = 128
FOX_W = FOX_HEADS * FOX_HEAD_DIM
RET_QK_W = RET_HEADS * RET_KEY_DIM
RET_V_W = RET_HEADS * RET_VAL_DIM
VMEM_LIMIT = 56 * 1024 * 1024
HIGH_HALF = -65536
SC_WINDOW = 128
SC_COLS = 256

F32 = jnp.float32
BF16 = jnp.bfloat16


def _rms(x, g):
    return x * lax.rsqrt(jnp.mean(x * x, axis=-1, keepdims=True) + EPS) * g


def _sigmoid(x):
    return 1.0 / (1.0 + jnp.exp(-x))


def _dot(a, b):
    return jnp.dot(a, b, preferred_element_type=F32)


def _dot_nt(a, b):
    return lax.dot_general(a, b, (((1,), (1,)), ((), ())), preferred_element_type=F32)


def _params(n_axes):
    return pltpu.CompilerParams(dimension_semantics=("arbitrary",) * n_axes,
                                vmem_limit_bytes=VMEM_LIMIT)


C_MID = FOX_HEADS
C_LO = 2 * FOX_HEADS
C_ONE = 3 * FOX_HEADS
N_AUG = 3


def _split3(t):
    hi = t.astype(BF16).astype(F32)
    r = t - hi
    mid = r.astype(BF16).astype(F32)
    lo = (r - mid).astype(BF16).astype(F32)
    return hi + pltpu.roll(mid, C_MID, 1) + pltpu.roll(lo, C_LO, 1)


def _inproj_kernel(x_ref, g_ref, w_ref, wvt_ref, wf_ref, bf_ref, cos_ref, sin_ref, pq_ref, pk_ref,
                   q_ref, k_ref, vt_ref, rq_ref, rk_ref, rv_ref, rg_ref, carry_scr, *, tiles_per_seq):
    i = pl.program_id(0)
    tm = x_ref.shape[0]

    @pl.when(i % tiles_per_seq == 0)
    def _():
        carry_scr[...] = jnp.zeros(carry_scr.shape, F32)

    xn = _rms(x_ref[...], g_ref[...]).astype(BF16)
    u = _dot(xn, w_ref[...])
    vt_ref[...] = _dot_nt(wvt_ref[...], xn).astype(BF16)

    lane = lax.broadcasted_iota(jnp.int32, (tm, LANES), 1)
    z = _dot(xn, wf_ref[...]) + bf_ref[...]
    lf = jnp.where(lane < FOX_HEADS, jnp.minimum(z, 0.0) - jnp.log1p(jnp.exp(-jnp.abs(z))), 0.0)
    r = lax.broadcasted_iota(jnp.int32, (tm, tm), 0)
    c = lax.broadcasted_iota(jnp.int32, (tm, tm), 1)
    tri = jnp.where(c <= r, 1.0, 0.0).astype(BF16)
    ps = _dot(tri, _split3(lf).astype(BF16))
    cum = ps + pltpu.roll(ps, LANES - C_MID, 1) + pltpu.roll(ps, LANES - C_LO, 1)
    cum = jnp.where(lane < FOX_HEADS, cum, 0.0) + carry_scr[...]
    carry_scr[...] = cum[tm - 1:tm, :]
    c3 = (_split3(cum) + jnp.where(lane == C_ONE, 1.0, 0.0)).astype(BF16)
    aug_q = _dot(c3, pq_ref[...])
    aug_k = _dot(c3, pk_ref[...])

    own = lane < FOX_HEAD_DIM
    fq = u[:, 0:FOX_W] * (FOX_HEAD_DIM ** -0.5)
    fk = u[:, FOX_W:2 * FOX_W]
    for src, aug, dst in ((fq, aug_q, q_ref), (fk, aug_k, k_ref)):
        for j in range(FOX_W // LANES):
            blk = src[:, j * LANES:(j + 1) * LANES]
            e0, e1 = 2 * j * LANES, (2 * j + 1) * LANES
            dst[:, e0:e0 + LANES] = jnp.where(own, blk, aug[:, e0:e0 + LANES]).astype(BF16)
            dst[:, e1:e1 + LANES] = jnp.where(own, pltpu.roll(blk, FOX_HEAD_DIM, 1),
                                               aug[:, e1:e1 + LANES]).astype(BF16)

    o = 2 * FOX_W
    rq = u[:, o:o + RET_QK_W]; o += RET_QK_W
    rk = u[:, o:o + RET_QK_W]; o += RET_QK_W
    rv_ref[...] = u[:, o:o + RET_V_W].astype(BF16); o += RET_V_W
    rg_ref[...] = u[:, o:o + RET_V_W]

    half = RET_KEY_DIM // 2
    lane_r = lax.broadcasted_iota(jnp.int32, rq.shape, 1)
    first = (lane_r % RET_KEY_DIM) < half
    cos = cos_ref[...]
    sin = sin_ref[...]

    def rot(t):
        partner = jnp.where(first, -pltpu.roll(t, RET_QK_W - half, 1), pltpu.roll(t, half, 1))
        return t * cos + partner * sin

    rq_ref[...] = rot(rq).astype(BF16)
    rk_ref[...] = (rot(rk) * (RET_KEY_DIM ** -0.5)).astype(BF16)


def _placement():
    pq = np.zeros((LANES, FOX_HEADS * LANES), np.float32)
    pk = np.zeros((LANES, FOX_HEADS * LANES), np.float32)
    for h in range(FOX_HEADS):
        base = h * LANES + FOX_HEAD_DIM
        for part, src in enumerate((h, C_MID + h, C_LO + h)):
            pq[C_ONE, base + part] = 1.0
            pq[src, base + N_AUG + part] = 1.0
            pk[src, base + part] = -1.0
            pk[C_ONE, base + N_AUG + part] = 1.0
    return jnp.asarray(pq, BF16), jnp.asarray(pk, BF16)


def _inproj(x2, g, w_main, w_vt, w_f, b_f, cos_t, sin_t, seq, tm):
    n, d = x2.shape
    wn = w_main.shape[1]
    spt = seq // tm
    aw = FOX_HEADS * LANES
    pq, pk = _placement()
    row = lambda i: (i, 0)
    const = lambda i: (0, 0)
    pos = lambda i: (i % spt, 0)
    out_shape = [
        jax.ShapeDtypeStruct((n, aw), BF16), jax.ShapeDtypeStruct((n, aw), BF16),
        jax.ShapeDtypeStruct((FOX_W, n), BF16),
        jax.ShapeDtypeStruct((n, RET_QK_W), BF16), jax.ShapeDtypeStruct((n, RET_QK_W), BF16),
        jax.ShapeDtypeStruct((n, RET_V_W), BF16), jax.ShapeDtypeStruct((n, RET_V_W), F32),
    ]
    out_specs = [
        pl.BlockSpec((tm, aw), row), pl.BlockSpec((tm, aw), row),
        pl.BlockSpec((FOX_W, tm), lambda i: (0, i)),
        pl.BlockSpec((tm, RET_QK_W), row), pl.BlockSpec((tm, RET_QK_W), row),
        pl.BlockSpec((tm, RET_V_W), row), pl.BlockSpec((tm, RET_V_W), row),
    ]
    return pl.pallas_call(
        functools.partial(_inproj_kernel, tiles_per_seq=spt),
        grid=(n // tm,),
        in_specs=[
            pl.BlockSpec((tm, d), row), pl.BlockSpec((1, d), const),
            pl.BlockSpec((d, wn), const), pl.BlockSpec((FOX_W, d), const),
            pl.BlockSpec((d, LANES), const), pl.BlockSpec((1, LANES), const),
            pl.BlockSpec((tm, RET_QK_W), pos), pl.BlockSpec((tm, RET_QK_W), pos),
            pl.BlockSpec((LANES, aw), const), pl.BlockSpec((LANES, aw), const),
        ],
        out_specs=out_specs,
        out_shape=out_shape,
        scratch_shapes=[pltpu.VMEM((1, LANES), F32)],
        compiler_params=_params(1),
        name="inproj",
    )(x2, g, w_main, w_vt, w_f, b_f, cos_t, sin_t, pq, pk)


def _fox_kernel(q_ref, k_ref, vt_ref, o_ref, *, tq):
    qi = pl.program_id(2)
    key = lax.broadcasted_iota(jnp.int32, (tq, tq), 0)
    qry = lax.broadcasted_iota(jnp.int32, (tq, tq), 1)
    vrow = lax.broadcasted_iota(jnp.int32, (LANES, tq), 0)
    one = jnp.ones((), BF16)

    def step(j, carry, diagonal):
        start = pl.multiple_of(j * tq, tq)
        v = vt_ref[:, pl.ds(start, tq)]
        out = []
        for a in range(2):
            m_prev, acc = carry[a]
            sl = slice(a * LANES, (a + 1) * LANES)
            s = _dot_nt(k_ref[pl.ds(start, tq), sl], q_ref[:, sl])
            if diagonal:
                s = jnp.where(key <= qry, s, -jnp.inf)
            m_new = jnp.maximum(m_prev, jnp.max(s, axis=0, keepdims=True))
            alpha = jnp.exp(m_prev - m_new)
            p = jnp.exp(s - m_new).astype(BF16)
            va = jnp.where((vrow // FOX_HEAD_DIM) == a, v, one)
            out.append((m_new, alpha * acc + _dot(va, p)))
        return tuple(out)

    init = tuple((jnp.full((1, tq), -jnp.inf, F32), jnp.zeros((LANES, tq), F32)) for _ in range(2))
    carry = lax.fori_loop(0, qi, lambda j, c: step(j, c, False), init)
    (_, acc0), (_, acc1) = step(qi, carry, True)
    ot = jnp.where(vrow < FOX_HEAD_DIM, acc0 / acc0[FOX_HEAD_DIM:FOX_HEAD_DIM + 1, :], acc1 / acc1[0:1, :])
    o_ref[...] = ot.T.astype(o_ref.dtype)


def _fox(q_aug, k_aug, v_t, batch, seq, tq):
    n = q_aug.shape[0]
    pairs = FOX_HEADS // 2
    nq = seq // tq
    kern = functools.partial(_fox_kernel, tq=tq)
    return pl.pallas_call(
        kern,
        grid=(batch, pairs, nq),
        in_specs=[
            pl.BlockSpec((tq, 2 * LANES), lambda i, h, q: (i * nq + q, h)),
            pl.BlockSpec((seq, 2 * LANES), lambda i, h, q: (i, h)),
            pl.BlockSpec((LANES, seq), lambda i, h, q: (h, i)),
        ],
        out_specs=pl.BlockSpec((tq, LANES), lambda i, h, q: (i * nq + q, h)),
        out_shape=jax.ShapeDtypeStruct((n, FOX_W), BF16),
        compiler_params=_params(3),
        name="fox_attention",
    )(q_aug, k_aug, v_t)


def _ret_kernel(lg_ref, q_ref, k_ref, v_ref, g_ref, o_ref, state_scr, *, chunk):
    h = pl.program_id(1)
    lg = lg_ref[h]
    s = q_ref.shape[1]
    lane = lax.broadcasted_iota(jnp.int32, (chunk, LANES), 1)
    mine = (lane // RET_KEY_DIM) == (h % 2)
    ri = lax.broadcasted_iota(jnp.int32, (chunk, chunk), 0)
    ci = lax.broadcasted_iota(jnp.int32, (chunk, chunk), 1)
    diff = (ri - ci).astype(F32)
    inner = jnp.where(diff >= 0, jnp.exp(jnp.maximum(diff, 0.0) * lg), 0.0)
    pos = lax.broadcasted_iota(jnp.int32, (chunk, 1), 0).astype(F32)
    q_decay = jnp.exp((pos + 1.0) * lg)
    k_decay = jnp.exp((chunk - 1.0 - pos) * lg)
    chunk_decay = jnp.exp(jnp.full((1, 1), chunk, F32) * lg)
    state_scr[...] = jnp.zeros(state_scr.shape, F32)
    for c in range(s // chunk):
        sl = pl.ds(c * chunk, chunk)
        q = jnp.where(mine, q_ref[0, sl, :], jnp.zeros((), BF16))
        k = jnp.where(mine, k_ref[0, sl, :], jnp.zeros((), BF16))
        v = v_ref[0, sl, :]
        scores = _dot_nt(q, k) * inner
        state = state_scr[...]
        o = _dot(scores.astype(BF16), v) + _dot(q, state.astype(BF16)) * q_decay
        kd = (k.astype(F32) * k_decay).astype(BF16)
        state_scr[...] = state * chunk_decay + lax.dot_general(
            kd, v, (((0,), (0,)), ((), ())), preferred_element_type=F32)
        o = o * lax.rsqrt(jnp.mean(o * o, axis=-1, keepdims=True) + EPS)
        g = g_ref[0, sl, :]
        o_ref[0, sl, :] = (o * (g * _sigmoid(g))).astype(o_ref.dtype)


def _retention(lg, rq, rk, rv, rg, chunk):
    b, s, _ = rq.shape
    kern = functools.partial(_ret_kernel, chunk=chunk)
    qk_spec = pl.BlockSpec((1, s, LANES), lambda i, h, lg_ref: (i, 0, h // 2))
    v_spec = pl.BlockSpec((1, s, LANES), lambda i, h, lg_ref: (i, 0, h))
    return pl.pallas_call(
        kern,
        grid_spec=pltpu.PrefetchScalarGridSpec(
            num_scalar_prefetch=1,
            grid=(b, RET_HEADS),
            in_specs=[qk_spec, qk_spec, v_spec, v_spec],
            out_specs=v_spec,
            scratch_shapes=[pltpu.VMEM((LANES, RET_VAL_DIM), F32)],
        ),
        out_shape=jax.ShapeDtypeStruct((b, s, RET_V_W), BF16),
        compiler_params=_params(2),
        name="retention",
    )(lg, rq, rk, rv, rg)


def _merge_kernel(x_ref, fox_ref, ro_ref, gmix_ref, wmg_ref, bmg_ref, wb_ref, wout_ref, gffn_ref,
                  wr_ref, br_ref, h1_ref, hn_ref, rf_ref, ri_ref, cnt_ref, base_scr):
    i = pl.program_id(0)
    tm, d = x_ref.shape

    @pl.when(i == 0)
    def _():
        base_scr[...] = jnp.zeros(base_scr.shape, F32)

    x = x_ref[...]
    xn = _rms(x, gmix_ref[...]).astype(BF16)
    gate = _sigmoid(_dot(xn, wmg_ref[...]) + bmg_ref[...])
    merged = gate[:, :d] * _dot(fox_ref[...], wb_ref[0]) + gate[:, d:] * _dot(ro_ref[...], wb_ref[1])
    h1 = x + _dot(merged.astype(BF16), wout_ref[...])
    h1_ref[...] = h1
    hn = _rms(h1, gffn_ref[...]).astype(BF16)
    bits = pltpu.bitcast(hn.astype(F32), jnp.int32)
    words = lax.shift_right_logical(bits[:, :d // 2], 16) | (bits[:, d // 2:] & HIGH_HALF)
    for c in range(hn_ref.shape[0]):
        hn_ref[c] = words[:, c * SC_COLS:(c + 1) * SC_COLS]

    logits = _dot(hn, wr_ref[...]) + br_ref[...]
    lane = lax.broadcasted_iota(jnp.int32, logits.shape, 1)
    lane_f = lane.astype(F32)
    vals, idxs = [], []
    cur = logits
    for _ in range(TOP_K):
        m = jnp.max(cur, axis=-1, keepdims=True)
        idx = jnp.min(jnp.where(cur == m, lane_f, float(LANES)), axis=-1, keepdims=True)
        vals.append(m)
        idxs.append(idx)
        cur = jnp.where(lane_f == idx, -jnp.inf, cur)
    exps = [jnp.exp(v - vals[0]) for v in vals]
    denom = exps[0] + exps[1] + exps[2] + exps[3]

    onehot = jnp.zeros(logits.shape, F32)
    for idx in idxs:
        onehot = onehot + jnp.where(lane_f == idx, 1.0, 0.0)
    r = lax.broadcasted_iota(jnp.int32, (tm, tm), 0)
    c = lax.broadcasted_iota(jnp.int32, (tm, tm), 1)
    tri = jnp.where(c < r, 1.0, 0.0).astype(BF16)
    before = _dot(tri, onehot.astype(BF16)) + base_scr[...]
    rf = jnp.zeros(logits.shape, F32)
    ri = jnp.zeros(logits.shape, F32)
    for j in range(TOP_K):
        rank = jnp.sum(jnp.where(lane_f == idxs[j], before, 0.0), axis=-1, keepdims=True)
        rf = jnp.where(lane == j, exps[j] / denom, rf)
        ri = jnp.where(lane == j, idxs[j], ri)
        ri = jnp.where(lane == TOP_K + j, rank, ri)
    rf_ref[...] = rf
    ri_ref[...] = ri.astype(jnp.int32)
    base_scr[...] = base_scr[...] + jnp.sum(onehot, axis=0, keepdims=True)
    cnt_ref[...] = base_scr[...]


def _merge(x2, fox, ro, gmix, wmg, bmg, wb, wout, gffn, wr, br, tm):
    n, d = x2.shape
    row = lambda i: (i, 0)
    const = lambda i: (0, 0)
    return pl.pallas_call(
        _merge_kernel,
        grid=(n // tm,),
        in_specs=[
            pl.BlockSpec((tm, d), row), pl.BlockSpec((tm, FOX_W), row), pl.BlockSpec((tm, RET_V_W), row),
            pl.BlockSpec((1, d), const), pl.BlockSpec((d, 2 * d), const), pl.BlockSpec((1, 2 * d), const),
            pl.BlockSpec((2, FOX_W, d), lambda i: (0, 0, 0)), pl.BlockSpec((d, d), const),
            pl.BlockSpec((1, d), const), pl.BlockSpec((d, LANES), const), pl.BlockSpec((1, LANES), const),
        ],
        out_specs=[
            pl.BlockSpec((tm, d), row), pl.BlockSpec((d // 2 // SC_COLS, tm, SC_COLS), lambda i: (0, i, 0)),
            pl.BlockSpec((tm, LANES), row), pl.BlockSpec((tm, LANES), row),
            pl.BlockSpec((1, LANES), const),
        ],
        out_shape=[
            jax.ShapeDtypeStruct((n, d), F32), jax.ShapeDtypeStruct((d // 2 // SC_COLS, n, SC_COLS), jnp.int32),
            jax.ShapeDtypeStruct((n, LANES), F32), jax.ShapeDtypeStruct((n, LANES), jnp.int32),
            jax.ShapeDtypeStruct((1, LANES), F32),
        ],
        scratch_shapes=[pltpu.VMEM((1, LANES), F32)],
        compiler_params=_params(1),
        name="merge_router",
    )(x2, fox, ro, gmix, wmg, bmg, wb, wout, gffn, wr, br)


def _sc_mesh():
    return plsc.VectorSubcoreMesh(core_axis_name="core", subcore_axis_name="subcore")


def _sc_dispatch(rows, dest_t, n_out):
    chunks, n, w = rows.shape

    @functools.partial(pl.kernel, out_type=jax.ShapeDtypeStruct((chunks, n_out, w), rows.dtype),
                       mesh=_sc_mesh(), scratch_types=[], name="moe_dispatch")
    def run(x_hbm, i_hbm, o_hbm):
        for c in range(chunks):
            def body(x_vmem, i_vmem, c=c):
                for j in range(TOP_K):
                    pltpu.sync_copy(x_vmem, o_hbm.at[c].at[i_vmem.at[j]])

            pltpu.emit_pipeline(
                body,
                grid=(n // SC_WINDOW,),
                in_specs=[pl.BlockSpec((SC_WINDOW, w), lambda i: (i, 0)),
                          pl.BlockSpec((TOP_K, SC_WINDOW), lambda i: (0, i))],
                out_specs=[],
                core_axis_name=("core", "subcore"),
                dimension_semantics=(pltpu.PARALLEL,),
            )(x_hbm.at[c], i_hbm)

    return run(rows, dest_t)


def _sc_gather(table, idx):
    chunks, _, w = table.shape
    m = idx.shape[1]

    @functools.partial(pl.kernel, out_type=jax.ShapeDtypeStruct((chunks, m, w), table.dtype),
                       mesh=_sc_mesh(), scratch_types=[], name="moe_combine_gather")
    def run(t_hbm, i_hbm, o_hbm):
        for c in range(chunks):
            def body(i_vmem, o_vmem, c=c):
                pltpu.sync_copy(t_hbm.at[c].at[i_vmem.at[0]], o_vmem)

            pltpu.emit_pipeline(
                body,
                grid=(m // SC_WINDOW,),
                in_specs=[pl.BlockSpec((1, SC_WINDOW), lambda i: (0, i))],
                out_specs=[pl.BlockSpec((SC_WINDOW, w), lambda i: (i, 0))],
                core_axis_name=("core", "subcore"),
                dimension_semantics=(pltpu.PARALLEL,),
            )(i_hbm, o_hbm.at[c])

    return run(table, idx)


GLU_GROUP = 2 * LANES


def _expert_kernel(be_ref, x_ref, wgu_ref, bgu_ref, wd_ref, bd_ref, perm_ref, y_ref, wgu_scr, wd_scr):
    i = pl.program_id(0)
    f2 = wgu_ref.shape[2]

    @pl.when(jnp.logical_or(i == 0, be_ref[i] != be_ref[jnp.maximum(i - 1, 0)]))
    def _():
        for b in range(f2 // GLU_GROUP):
            cols = slice(b * GLU_GROUP, (b + 1) * GLU_GROUP)
            wgu_scr[:, cols] = _dot(wgu_ref[0, :, cols].astype(BF16), perm_ref[...]).astype(BF16)
        wd_scr[...] = wd_ref[0].astype(BF16)

    words = [x_ref[c] for c in range(x_ref.shape[0])]
    x = jnp.concatenate([pltpu.bitcast(lax.shift_left(w, 16), F32) for w in words]
                        + [pltpu.bitcast(w & HIGH_HALF, F32) for w in words], axis=1).astype(BF16)
    gu = _dot(x, wgu_scr[...]) + bgu_ref[0]
    acts = []
    for b in range(f2 // GLU_GROUP):
        glu = jnp.minimum(gu[:, b * GLU_GROUP:b * GLU_GROUP + LANES], SWIGLU_LIMIT)
        lin = jnp.clip(gu[:, b * GLU_GROUP + LANES:(b + 1) * GLU_GROUP], -SWIGLU_LIMIT, SWIGLU_LIMIT)
        acts.append((glu * _sigmoid(SWIGLU_ALPHA * glu) * (lin + 1.0)).astype(BF16))
    act = jnp.concatenate(acts, axis=1)
    y = _dot(act, wd_scr[...]) + bd_ref[0]
    for c in range(y_ref.shape[0]):
        y_ref[c] = y[:, c * SC_COLS:(c + 1) * SC_COLS]


def _experts(block_e, x_buf, wgu, bgu, wd, bd, bm):
    p = x_buf.shape[1]
    f, d = wd.shape[1:]
    perm = np.zeros((GLU_GROUP, GLU_GROUP), np.float32)
    for c in range(LANES):
        perm[2 * c, c] = 1.0
        perm[2 * c + 1, LANES + c] = 1.0
    return pl.pallas_call(
        _expert_kernel,
        grid_spec=pltpu.PrefetchScalarGridSpec(
            num_scalar_prefetch=1,
            grid=(p // bm,),
            in_specs=[
                pl.BlockSpec((d // 2 // SC_COLS, bm, SC_COLS), lambda i, be: (0, i, 0)),
                pl.BlockSpec((1, d, 2 * f), lambda i, be: (be[i], 0, 0)),
                pl.BlockSpec((1, 1, 2 * f), lambda i, be: (be[i], 0, 0)),
                pl.BlockSpec((1, f, d), lambda i, be: (be[i], 0, 0)),
                pl.BlockSpec((1, 1, d), lambda i, be: (be[i], 0, 0)),
                pl.BlockSpec((GLU_GROUP, GLU_GROUP), lambda i, be: (0, 0)),
            ],
            out_specs=pl.BlockSpec((d // SC_COLS, bm, SC_COLS), lambda i, be: (0, i, 0)),
            scratch_shapes=[pltpu.VMEM((d, 2 * f), BF16), pltpu.VMEM((f, d), BF16)],
        ),
        out_shape=jax.ShapeDtypeStruct((d // SC_COLS, p, SC_COLS), F32),
        compiler_params=_params(1),
        name="expert_ffn",
    )(block_e, x_buf, wgu, bgu, wd, bd, jnp.asarray(perm, BF16))


def _final_kernel(h1_ref, yg_ref, rf_ref, p_ref, gple_ref, wpg_ref, wpp_ref, gfin_ref, o_ref):
    rf = rf_ref[...]
    h2 = h1_ref[...]
    for j in range(TOP_K):
        yj = jnp.concatenate([yg_ref[c, j] for c in range(yg_ref.shape[0])], axis=1)
        h2 = h2 + yj * rf[:, j:j + 1]
    gate = _sigmoid(_dot(_rms(h2, gple_ref[...]).astype(BF16), wpg_ref[...]))
    h3 = h2 + gate * _dot(p_ref[...].astype(BF16), wpp_ref[...])
    o_ref[...] = _rms(h3, gfin_ref[...])


def _final(h1, yg, rf, p2, gple, wpg, wpp, gfin, tm):
    n, d = h1.shape
    pd = p2.shape[1]
    row = lambda i: (i, 0)
    const = lambda i: (0, 0)
    return pl.pallas_call(
        _final_kernel,
        grid=(n // tm,),
        in_specs=[
            pl.BlockSpec((tm, d), row), pl.BlockSpec((d // SC_COLS, TOP_K, tm, SC_COLS), lambda i: (0, 0, i, 0)),
            pl.BlockSpec((tm, LANES), row), pl.BlockSpec((tm, pd), row),
            pl.BlockSpec((1, d), const), pl.BlockSpec((d, d), const), pl.BlockSpec((pd, d), const),
            pl.BlockSpec((1, d), const),
        ],
        out_specs=pl.BlockSpec((tm, d), row),
        out_shape=jax.ShapeDtypeStruct((n, d), F32),
        compiler_params=_params(1),
        name="combine_ple_norm",
    )(h1, yg, rf, p2, gple, wpg, wpp, gfin)


def _layer(h, p, mix_norm, w_in, b_forget, w_branch, w_merge_gate, b_merge_gate, w_out, ffn_norm,
           w_router, b_router, w_gate_up, b_gate_up, w_down, b_down, ple_norm, w_ple_gate, w_ple_proj,
           final_norm, *, tm, tq, chunk, bm):
    b, s, d = h.shape
    n = b * s
    x2 = h.reshape(n, d)
    row = lambda t: t.reshape(1, -1)

    c0 = 3 * FOX_W
    w_main = jnp.concatenate([w_in[:, :2 * FOX_W], w_in[:, c0 + FOX_HEADS:]], axis=1).astype(BF16)
    w_vt = w_in[:, 2 * FOX_W:c0].T.astype(BF16)
    w_f = jnp.pad(w_in[:, c0:c0 + FOX_HEADS], ((0, 0), (0, LANES - FOX_HEADS))).astype(BF16)
    b_f = jnp.pad(b_forget, (0, LANES - FOX_HEADS)).reshape(1, LANES)
    half = RET_KEY_DIM // 2
    inv = ROPE_BASE ** (-jnp.arange(half, dtype=F32) / half)
    ang = jnp.arange(s).astype(F32)[:, None] * inv[None, :]
    cos_t = jnp.tile(jnp.cos(ang), (1, RET_QK_W // half))
    sin_t = jnp.tile(jnp.sin(ang), (1, RET_QK_W // half))

    q_aug, k_aug, v_t, rq, rk, rv, rg = _inproj(x2, row(mix_norm), w_main, w_vt, w_f, b_f, cos_t, sin_t, s, tm)
    fox = _fox(q_aug, k_aug, v_t, b, s, tq)

    lg = jnp.log1p(-jnp.exp2(-5.0 - jnp.arange(RET_HEADS, dtype=F32)))
    ro = _retention(lg, rq.reshape(b, s, RET_QK_W), rk.reshape(b, s, RET_QK_W),
                    rv.reshape(b, s, RET_V_W), rg.reshape(b, s, RET_V_W), chunk)

    w_r = jnp.pad(w_router, ((0, 0), (0, LANES - N_EXPERTS))).astype(BF16)
    b_r = jnp.concatenate([b_router, jnp.full((LANES - N_EXPERTS,), -1e30, F32)]).reshape(1, LANES)
    h1, hn, rf, ri, cnt = _merge(
        x2, fox.reshape(n, FOX_W), ro.reshape(n, RET_V_W), row(mix_norm), w_merge_gate.astype(BF16),
        row(b_merge_gate), w_branch.astype(BF16), w_out.astype(BF16), row(ffn_norm), w_r, b_r, tm)

    a = n * TOP_K
    nb = -(-(a + N_EXPERTS * (bm - 1)) // bm)
    pr = nb * bm
    e_idx = ri[:, :TOP_K]
    rank = ri[:, TOP_K:2 * TOP_K]
    counts = cnt[0, :N_EXPERTS].astype(jnp.int32)
    padded = (counts + bm - 1) // bm * bm
    pad_end = jnp.cumsum(padded)
    pad_start = pad_end - padded
    dest = pad_start[e_idx] + rank
    block_start = jnp.arange(nb, dtype=jnp.int32) * bm
    block_e = jnp.minimum(jnp.sum(block_start[:, None] >= pad_end[None, :], axis=1),
                          N_EXPERTS - 1).astype(jnp.int32)
    dest_t = dest.T

    ne, dd, f2 = w_gate_up.shape
    bgu = b_gate_up.reshape(ne, f2 // GLU_GROUP, LANES, 2).transpose(0, 1, 3, 2).reshape(ne, 1, f2)
    x_buf = _sc_dispatch(hn, dest_t, pr)
    y_buf = _experts(block_e, x_buf, w_gate_up, bgu, w_down, b_down.reshape(ne, 1, dd), bm)
    yg = _sc_gather(y_buf, dest_t.reshape(1, a)).reshape(d // SC_COLS, TOP_K, n, SC_COLS)

    out = _final(h1, yg, rf, p.reshape(n, -1), row(ple_norm), w_ple_gate.astype(BF16),
                 w_ple_proj.astype(BF16), row(final_norm), tm // 2)
    return out.reshape(b, s, d)


def kernel(x, p, mix_norm, w_in, b_forget, w_branch, w_merge_gate, b_merge_gate, w_out, ffn_norm, w_router,
           b_router, w_gate_up, b_gate_up, w_down, b_down, ple_norm, w_ple_gate, w_ple_proj, final_norm):
    depth = p.shape[0]
    assert depth == 1, "the final norm is fused into the (single) layer"
    return _layer(x, p[0], mix_norm[0], w_in[0], b_forget[0], w_branch[0], w_merge_gate[0], b_merge_gate[0],
                  w_out[0], ffn_norm[0], w_router[0], b_router[0], w_gate_up[0], b_gate_up[0], w_down[0],
                  b_down[0], ple_norm[0], w_ple_gate[0], w_ple_proj[0], final_norm,
                  tm=512, tq=512, chunk=256, bm=512)
```

```python
import functools

import numpy as np
import jax
import jax.numpy as jnp
from jax import lax
from jax.experimental import pallas as pl
from jax.experimental.pallas import tpu as pltpu
from jax.experimental.pallas import tpu_sc as plsc

FOX_HEADS = 8
FOX_HEAD_DIM = 64
RET_HEADS = 4
RET_KEY_DIM = 64
RET_VAL_DIM = 128
ROPE_BASE = 10000.0
N_EXPERTS = 32
TOP_K = 4
SWIGLU_LIMIT = 7.0
SWIGLU_ALPHA = 1.702
EPS = 1e-6

LANES = 128
FOX_W = FOX_HEADS * FOX_HEAD_DIM
RET_QK_W = RET_HEADS * RET_KEY_DIM
RET_V_W = RET_HEADS * RET_VAL_DIM
VMEM_LIMIT = 56 * 1024 * 1024
HIGH_HALF = -65536
SC_WINDOW = 128
SC_COLS = 256

F32 = jnp.float32
BF16 = jnp.bfloat16


def _rms(x, g):
    return x * lax.rsqrt(jnp.mean(x * x, axis=-1, keepdims=True) + EPS) * g


def _sigmoid(x):
    return 1.0 / (1.0 + jnp.exp(-x))


def _dot(a, b):
    return jnp.dot(a, b, preferred_element_type=F32)


def _dot_nt(a, b):
    return lax.dot_general(a, b, (((1,), (1,)), ((), ())), preferred_element_type=F32)


def _pack_pairs(v):
    bits = pltpu.bitcast(v.astype(BF16).astype(F32), jnp.int32)
    half = v.shape[1] // 2
    return lax.shift_right_logical(bits[:, :half], 16) | (bits[:, half:] & HIGH_HALF)


def _unpack_pairs(chunks):
    return jnp.concatenate([pltpu.bitcast(lax.shift_left(w, 16), F32) for w in chunks]
                           + [pltpu.bitcast(w & HIGH_HALF, F32) for w in chunks], axis=1)


def _store_chunks(ref, v):
    for c in range(ref.shape[0]):
        ref[c] = v[:, c * SC_COLS:(c + 1) * SC_COLS]


def _params(n_axes):
    return pltpu.CompilerParams(dimension_semantics=("arbitrary",) * n_axes,
                                vmem_limit_bytes=VMEM_LIMIT)


C_MID = FOX_HEADS
C_LO = 2 * FOX_HEADS
C_ONE = 3 * FOX_HEADS
N_AUG = 3


def _split3(t):
    hi = t.astype(BF16).astype(F32)
    r = t - hi
    mid = r.astype(BF16).astype(F32)
    lo = (r - mid).astype(BF16).astype(F32)
    return hi + pltpu.roll(mid, C_MID, 1) + pltpu.roll(lo, C_LO, 1)


def _inproj_kernel(x_ref, g_ref, w_ref, wvt_ref, wf_ref, bf_ref, cos_ref, sin_ref, pq_ref, pk_ref,
                   q_ref, k_ref, vt_ref, rq_ref, rk_ref, rv_ref, rg_ref, carry_scr, *, tiles_per_seq):
    i = pl.program_id(0)
    tm = x_ref.shape[0]

    @pl.when(i % tiles_per_seq == 0)
    def _():
        carry_scr[...] = jnp.zeros(carry_scr.shape, F32)

    xn = _rms(x_ref[...], g_ref[...]).astype(BF16)
    u = _dot(xn, w_ref[...])
    vt_ref[...] = _dot_nt(wvt_ref[...], xn).astype(BF16)

    lane = lax.broadcasted_iota(jnp.int32, (tm, LANES), 1)
    z = _dot(xn, wf_ref[...]) + bf_ref[...]
    lf = jnp.where(lane < FOX_HEADS, jnp.minimum(z, 0.0) - jnp.log1p(jnp.exp(-jnp.abs(z))), 0.0)
    r = lax.broadcasted_iota(jnp.int32, (tm, tm), 0)
    c = lax.broadcasted_iota(jnp.int32, (tm, tm), 1)
    tri = jnp.where(c <= r, 1.0, 0.0).astype(BF16)
    ps = _dot(tri, _split3(lf).astype(BF16))
    cum = ps + pltpu.roll(ps, LANES - C_MID, 1) + pltpu.roll(ps, LANES - C_LO, 1)
    cum = jnp.where(lane < FOX_HEADS, cum, 0.0) + carry_scr[...]
    carry_scr[...] = cum[tm - 1:tm, :]
    c3 = (_split3(cum) + jnp.where(lane == C_ONE, 1.0, 0.0)).astype(BF16)
    aug_q = _dot(c3, pq_ref[...])
    aug_k = _dot(c3, pk_ref[...])

    own = lane < FOX_HEAD_DIM
    fq = u[:, 0:FOX_W] * (FOX_HEAD_DIM ** -0.5)
    fk = u[:, FOX_W:2 * FOX_W]
    for src, aug, dst in ((fq, aug_q, q_ref), (fk, aug_k, k_ref)):
        for j in range(FOX_W // LANES):
            blk = src[:, j * LANES:(j + 1) * LANES]
            e0, e1 = 2 * j * LANES, (2 * j + 1) * LANES
            dst[:, e0:e0 + LANES] = jnp.where(own, blk, aug[:, e0:e0 + LANES]).astype(BF16)
            dst[:, e1:e1 + LANES] = jnp.where(own, pltpu.roll(blk, FOX_HEAD_DIM, 1),
                                               aug[:, e1:e1 + LANES]).astype(BF16)

    o = 2 * FOX_W
    rq = u[:, o:o + RET_QK_W]; o += RET_QK_W
    rk = u[:, o:o + RET_QK_W]; o += RET_QK_W
    rv_ref[...] = u[:, o:o + RET_V_W].astype(BF16); o += RET_V_W
    rg_ref[...] = u[:, o:o + RET_V_W]

    half = RET_KEY_DIM // 2
    lane_r = lax.broadcasted_iota(jnp.int32, rq.shape, 1)
    first = (lane_r % RET_KEY_DIM) < half
    cos = cos_ref[...]
    sin = sin_ref[...]

    def rot(t):
        partner = jnp.where(first, -pltpu.roll(t, RET_QK_W - half, 1), pltpu.roll(t, half, 1))
        return t * cos + partner * sin

    rq_ref[...] = rot(rq).astype(BF16)
    rk_ref[...] = (rot(rk) * (RET_KEY_DIM ** -0.5)).astype(BF16)


def _placement():
    pq = np.zeros((LANES, FOX_HEADS * LANES), np.float32)
    pk = np.zeros((LANES, FOX_HEADS * LANES), np.float32)
    for h in range(FOX_HEADS):
        base = h * LANES + FOX_HEAD_DIM
        for part, src in enumerate((h, C_MID + h, C_LO + h)):
            pq[C_ONE, base + part] = 1.0
            pq[src, base + N_AUG + part] = 1.0
            pk[src, base + part] = -1.0
            pk[C_ONE, base + N_AUG + part] = 1.0
    return jnp.asarray(pq, BF16), jnp.asarray(pk, BF16)


def _inproj(x2, g, w_main, w_vt, w_f, b_f, cos_t, sin_t, seq, tm):
    n, d = x2.shape
    wn = w_main.shape[1]
    spt = seq // tm
    aw = FOX_HEADS * LANES
    pq, pk = _placement()
    row = lambda i: (i, 0)
    const = lambda i: (0, 0)
    pos = lambda i: (i % spt, 0)
    out_shape = [
        jax.ShapeDtypeStruct((n, aw), BF16), jax.ShapeDtypeStruct((n, aw), BF16),
        jax.ShapeDtypeStruct((FOX_W, n), BF16),
        jax.ShapeDtypeStruct((n, RET_QK_W), BF16), jax.ShapeDtypeStruct((n, RET_QK_W), BF16),
        jax.ShapeDtypeStruct((n, RET_V_W), BF16), jax.ShapeDtypeStruct((n, RET_V_W), F32),
    ]
    out_specs = [
        pl.BlockSpec((tm, aw), row), pl.BlockSpec((tm, aw), row),
        pl.BlockSpec((FOX_W, tm), lambda i: (0, i)),
        pl.BlockSpec((tm, RET_QK_W), row), pl.BlockSpec((tm, RET_QK_W), row),
        pl.BlockSpec((tm, RET_V_W), row), pl.BlockSpec((tm, RET_V_W), row),
    ]
    return pl.pallas_call(
        functools.partial(_inproj_kernel, tiles_per_seq=spt),
        grid=(n // tm,),
        in_specs=[
            pl.BlockSpec((tm, d), row), pl.BlockSpec((1, d), const),
            pl.BlockSpec((d, wn), const), pl.BlockSpec((FOX_W, d), const),
            pl.BlockSpec((d, LANES), const), pl.BlockSpec((1, LANES), const),
            pl.BlockSpec((tm, RET_QK_W), pos), pl.BlockSpec((tm, RET_QK_W), pos),
            pl.BlockSpec((LANES, aw), const), pl.BlockSpec((LANES, aw), const),
        ],
        out_specs=out_specs,
        out_shape=out_shape,
        scratch_shapes=[pltpu.VMEM((1, LANES), F32)],
        compiler_params=_params(1),
        name="inproj",
    )(x2, g, w_main, w_vt, w_f, b_f, cos_t, sin_t, pq, pk)


def _fox_kernel(q_ref, k_ref, vt_ref, o_ref, *, tq):
    qi = pl.program_id(2)
    key = lax.broadcasted_iota(jnp.int32, (tq, tq), 0)
    qry = lax.broadcasted_iota(jnp.int32, (tq, tq), 1)
    vrow = lax.broadcasted_iota(jnp.int32, (LANES, tq), 0)
    one = jnp.ones((), BF16)

    def step(j, carry, diagonal):
        start = pl.multiple_of(j * tq, tq)
        v = vt_ref[:, pl.ds(start, tq)]
        out = []
        for a in range(2):
            m_prev, acc = carry[a]
            sl = slice(a * LANES, (a + 1) * LANES)
            s = _dot_nt(k_ref[pl.ds(start, tq), sl], q_ref[:, sl])
            if diagonal:
                s = jnp.where(key <= qry, s, -jnp.inf)
            m_new = jnp.maximum(m_prev, jnp.max(s, axis=0, keepdims=True))
            alpha = jnp.exp(m_prev - m_new)
            p = jnp.exp(s - m_new).astype(BF16)
            va = jnp.where((vrow // FOX_HEAD_DIM) == a, v, one)
            out.append((m_new, alpha * acc + _dot(va, p)))
        return tuple(out)

    init = tuple((jnp.full((1, tq), -jnp.inf, F32), jnp.zeros((LANES, tq), F32)) for _ in range(2))
    carry = lax.fori_loop(0, qi, lambda j, c: step(j, c, False), init)
    (_, acc0), (_, acc1) = step(qi, carry, True)
    ot = jnp.where(vrow < FOX_HEAD_DIM, acc0 / acc0[FOX_HEAD_DIM:FOX_HEAD_DIM + 1, :], acc1 / acc1[0:1, :])
    o_ref[...] = ot.T.astype(o_ref.dtype)


def _fox(q_aug, k_aug, v_t, batch, seq, tq):
    n = q_aug.shape[0]
    pairs = FOX_HEADS // 2
    nq = seq // tq
    kern = functools.partial(_fox_kernel, tq=tq)
    return pl.pallas_call(
        kern,
        grid=(batch, pairs, nq),
        in_specs=[
            pl.BlockSpec((tq, 2 * LANES), lambda i, h, q: (i * nq + q, h)),
            pl.BlockSpec((seq, 2 * LANES), lambda i, h, q: (i, h)),
            pl.BlockSpec((LANES, seq), lambda i, h, q: (h, i)),
        ],
        out_specs=pl.BlockSpec((tq, LANES), lambda i, h, q: (i * nq + q, h)),
        out_shape=jax.ShapeDtypeStruct((n, FOX_W), BF16),
        compiler_params=_params(3),
        name="fox_attention",
    )(q_aug, k_aug, v_t)


def _ret_kernel(lg_ref, q_ref, k_ref, v_ref, g_ref, o_ref, state_scr, *, chunk):
    h = pl.program_id(1)
    lg = lg_ref[h]
    s = q_ref.shape[1]
    lane = lax.broadcasted_iota(jnp.int32, (chunk, LANES), 1)
    mine = (lane // RET_KEY_DIM) == (h % 2)
    ri = lax.broadcasted_iota(jnp.int32, (chunk, chunk), 0)
    ci = lax.broadcasted_iota(jnp.int32, (chunk, chunk), 1)
    diff = (ri - ci).astype(F32)
    inner = jnp.where(diff >= 0, jnp.exp(jnp.maximum(diff, 0.0) * lg), 0.0)
    pos = lax.broadcasted_iota(jnp.int32, (chunk, 1), 0).astype(F32)
    q_decay = jnp.exp((pos + 1.0) * lg)
    k_decay = jnp.exp((chunk - 1.0 - pos) * lg)
    chunk_decay = jnp.exp(jnp.full((1, 1), chunk, F32) * lg)
    state_scr[...] = jnp.zeros(state_scr.shape, F32)
    for c in range(s // chunk):
        sl = pl.ds(c * chunk, chunk)
        q = jnp.where(mine, q_ref[0, sl, :], jnp.zeros((), BF16))
        k = jnp.where(mine, k_ref[0, sl, :], jnp.zeros((), BF16))
        v = v_ref[0, sl, :]
        scores = _dot_nt(q, k) * inner
        state = state_scr[...]
        o = _dot(scores.astype(BF16), v) + _dot(q, state.astype(BF16)) * q_decay
        kd = (k.astype(F32) * k_decay).astype(BF16)
        state_scr[...] = state * chunk_decay + lax.dot_general(
            kd, v, (((0,), (0,)), ((), ())), preferred_element_type=F32)
        o = o * lax.rsqrt(jnp.mean(o * o, axis=-1, keepdims=True) + EPS)
        g = g_ref[0, sl, :]
        o_ref[0, sl, :] = (o * (g * _sigmoid(g))).astype(o_ref.dtype)


def _retention(lg, rq, rk, rv, rg, chunk):
    b, s, _ = rq.shape
    kern = functools.partial(_ret_kernel, chunk=chunk)
    qk_spec = pl.BlockSpec((1, s, LANES), lambda i, h, lg_ref: (i, 0, h // 2))
    v_spec = pl.BlockSpec((1, s, LANES), lambda i, h, lg_ref: (i, 0, h))
    return pl.pallas_call(
        kern,
        grid_spec=pltpu.PrefetchScalarGridSpec(
            num_scalar_prefetch=1,
            grid=(b, RET_HEADS),
            in_specs=[qk_spec, qk_spec, v_spec, v_spec],
            out_specs=v_spec,
            scratch_shapes=[pltpu.VMEM((LANES, RET_VAL_DIM), F32)],
        ),
        out_shape=jax.ShapeDtypeStruct((b, s, RET_V_W), BF16),
        compiler_params=_params(2),
        name="retention",
    )(lg, rq, rk, rv, rg)


def _merge_kernel(x_ref, fox_ref, ro_ref, gmix_ref, wmg_ref, bmg_ref, wb_ref, wout_ref, gffn_ref,
                  wr_ref, br_ref, h1_ref, hn_ref, rf_ref, ri_ref, cnt_ref, base_scr):
    i = pl.program_id(0)
    tm, d = x_ref.shape

    @pl.when(i == 0)
    def _():
        base_scr[...] = jnp.zeros(base_scr.shape, F32)

    x = x_ref[...]
    xn = _rms(x, gmix_ref[...]).astype(BF16)
    gate = _sigmoid(_dot(xn, wmg_ref[...]) + bmg_ref[...])
    merged = gate[:, :d] * _dot(fox_ref[...], wb_ref[0]) + gate[:, d:] * _dot(ro_ref[...], wb_ref[1])
    h1 = x + _dot(merged.astype(BF16), wout_ref[...])
    h1_ref[...] = h1
    hn = _rms(h1, gffn_ref[...]).astype(BF16)
    _store_chunks(hn_ref, _pack_pairs(hn))

    logits = _dot(hn, wr_ref[...]) + br_ref[...]
    lane = lax.broadcasted_iota(jnp.int32, logits.shape, 1)
    lane_f = lane.astype(F32)
    vals, idxs = [], []
    cur = logits
    for _ in range(TOP_K):
        m = jnp.max(cur, axis=-1, keepdims=True)
        idx = jnp.min(jnp.where(cur == m, lane_f, float(LANES)), axis=-1, keepdims=True)
        vals.append(m)
        idxs.append(idx)
        cur = jnp.where(lane_f == idx, -jnp.inf, cur)
    exps = [jnp.exp(v - vals[0]) for v in vals]
    denom = exps[0] + exps[1] + exps[2] + exps[3]

    onehot = jnp.zeros(logits.shape, F32)
    for idx in idxs:
        onehot = onehot + jnp.where(lane_f == idx, 1.0, 0.0)
    r = lax.broadcasted_iota(jnp.int32, (tm, tm), 0)
    c = lax.broadcasted_iota(jnp.int32, (tm, tm), 1)
    tri = jnp.where(c < r, 1.0, 0.0).astype(BF16)
    before = _dot(tri, onehot.astype(BF16)) + base_scr[...]
    rf = jnp.zeros(logits.shape, F32)
    ri = jnp.zeros(logits.shape, F32)
    for j in range(TOP_K):
        rank = jnp.sum(jnp.where(lane_f == idxs[j], before, 0.0), axis=-1, keepdims=True)
        rf = jnp.where(lane == j, exps[j] / denom, rf)
        ri = jnp.where(lane == j, idxs[j], ri)
        ri = jnp.where(lane == TOP_K + j, rank, ri)
    rf_ref[...] = rf
    ri_ref[...] = ri.astype(jnp.int32)
    base_scr[...] = base_scr[...] + jnp.sum(onehot, axis=0, keepdims=True)
    cnt_ref[...] = base_scr[...]


def _merge(x2, fox, ro, gmix, wmg, bmg, wb, wout, gffn, wr, br, tm):
    n, d = x2.shape
    row = lambda i: (i, 0)
    const = lambda i: (0, 0)
    return pl.pallas_call(
        _merge_kernel,
        grid=(n // tm,),
        in_specs=[
            pl.BlockSpec((tm, d), row), pl.BlockSpec((tm, FOX_W), row), pl.BlockSpec((tm, RET_V_W), row),
            pl.BlockSpec((1, d), const), pl.BlockSpec((d, 2 * d), const), pl.BlockSpec((1, 2 * d), const),
            pl.BlockSpec((2, FOX_W, d), lambda i: (0, 0, 0)), pl.BlockSpec((d, d), const),
            pl.BlockSpec((1, d), const), pl.BlockSpec((d, LANES), const), pl.BlockSpec((1, LANES), const),
        ],
        out_specs=[
            pl.BlockSpec((tm, d), row), pl.BlockSpec((d // 2 // SC_COLS, tm, SC_COLS), lambda i: (0, i, 0)),
            pl.BlockSpec((tm, LANES), row), pl.BlockSpec((tm, LANES), row),
            pl.BlockSpec((1, LANES), const),
        ],
        out_shape=[
            jax.ShapeDtypeStruct((n, d), F32), jax.ShapeDtypeStruct((d // 2 // SC_COLS, n, SC_COLS), jnp.int32),
            jax.ShapeDtypeStruct((n, LANES), F32), jax.ShapeDtypeStruct((n, LANES), jnp.int32),
            jax.ShapeDtypeStruct((1, LANES), F32),
        ],
        scratch_shapes=[pltpu.VMEM((1, LANES), F32)],
        compiler_params=_params(1),
        name="merge_router",
    )(x2, fox, ro, gmix, wmg, bmg, wb, wout, gffn, wr, br)


def _sc_mesh():
    return plsc.VectorSubcoreMesh(core_axis_name="core", subcore_axis_name="subcore")


def _sc_dispatch(rows, dest_t, n_out):
    chunks, n, w = rows.shape

    @functools.partial(pl.kernel, out_type=jax.ShapeDtypeStruct((chunks, n_out, w), rows.dtype),
                       mesh=_sc_mesh(), scratch_types=[], name="moe_dispatch")
    def run(x_hbm, i_hbm, o_hbm):
        for c in range(chunks):
            def body(x_vmem, i_vmem, c=c):
                for j in range(TOP_K):
                    pltpu.sync_copy(x_vmem, o_hbm.at[c].at[i_vmem.at[j]])

            pltpu.emit_pipeline(
                body,
                grid=(n // SC_WINDOW,),
                in_specs=[pl.BlockSpec((SC_WINDOW, w), lambda i: (i, 0)),
                          pl.BlockSpec((TOP_K, SC_WINDOW), lambda i: (0, i))],
                out_specs=[],
                core_axis_name=("core", "subcore"),
                dimension_semantics=(pltpu.PARALLEL,),
            )(x_hbm.at[c], i_hbm)

    return run(rows, dest_t)


def _sc_gather(table, idx):
    chunks, _, w = table.shape
    m = idx.shape[1]

    @functools.partial(pl.kernel, out_type=jax.ShapeDtypeStruct((chunks, m, w), table.dtype),
                       mesh=_sc_mesh(), scratch_types=[], name="moe_combine_gather")
    def run(t_hbm, i_hbm, o_hbm):
        for c in range(chunks):
            def body(i_vmem, o_vmem, c=c):
                pltpu.sync_copy(t_hbm.at[c].at[i_vmem.at[0]], o_vmem)

            pltpu.emit_pipeline(
                body,
                grid=(m // SC_WINDOW,),
                in_specs=[pl.BlockSpec((1, SC_WINDOW), lambda i: (0, i))],
                out_specs=[pl.BlockSpec((SC_WINDOW, w), lambda i: (i, 0))],
                core_axis_name=("core", "subcore"),
                dimension_semantics=(pltpu.PARALLEL,),
            )(i_hbm, o_hbm.at[c])

    return run(table, idx)


GLU_GROUP = 2 * LANES


def _expert_kernel(be_ref, used_ref, x_ref, wgu_ref, bgu_ref, wd_ref, bd_ref, perm_ref, y_ref, wgu_scr, wd_scr):
    i = pl.program_id(0)
    f2 = wgu_ref.shape[2]
    live = i < used_ref[0]

    @pl.when(jnp.logical_and(live, jnp.logical_or(i == 0, be_ref[i] != be_ref[jnp.maximum(i - 1, 0)])))
    def _():
        for b in range(f2 // GLU_GROUP):
            cols = slice(b * GLU_GROUP, (b + 1) * GLU_GROUP)
            wgu_scr[:, cols] = _dot(wgu_ref[0, :, cols].astype(BF16), perm_ref[...]).astype(BF16)
        wd_scr[...] = wd_ref[0].astype(BF16)

    @pl.when(live)
    def _():
        x = _unpack_pairs([x_ref[c] for c in range(x_ref.shape[0])]).astype(BF16)
        gu = _dot(x, wgu_scr[...]) + bgu_ref[0]
        acts = []
        for b in range(f2 // GLU_GROUP):
            glu = jnp.minimum(gu[:, b * GLU_GROUP:b * GLU_GROUP + LANES], SWIGLU_LIMIT)
            lin = jnp.clip(gu[:, b * GLU_GROUP + LANES:(b + 1) * GLU_GROUP], -SWIGLU_LIMIT, SWIGLU_LIMIT)
            acts.append((glu * _sigmoid(SWIGLU_ALPHA * glu) * (lin + 1.0)).astype(BF16))
        y = _dot(jnp.concatenate(acts, axis=1), wd_scr[...]) + bd_ref[0]
        _store_chunks(y_ref, _pack_pairs(y))


def _experts(block_e, n_used, x_buf, wgu, bgu, wd, bd, bm):
    p = x_buf.shape[1]
    f, d = wd.shape[1:]
    perm = np.zeros((GLU_GROUP, GLU_GROUP), np.float32)
    for c in range(LANES):
        perm[2 * c, c] = 1.0
        perm[2 * c + 1, LANES + c] = 1.0
    return pl.pallas_call(
        _expert_kernel,
        grid_spec=pltpu.PrefetchScalarGridSpec(
            num_scalar_prefetch=2,
            grid=(p // bm,),
            in_specs=[
                pl.BlockSpec((d // 2 // SC_COLS, bm, SC_COLS), lambda i, be, nu: (0, i, 0)),
                pl.BlockSpec((1, d, 2 * f), lambda i, be, nu: (be[i], 0, 0)),
                pl.BlockSpec((1, 1, 2 * f), lambda i, be, nu: (be[i], 0, 0)),
                pl.BlockSpec((1, f, d), lambda i, be, nu: (be[i], 0, 0)),
                pl.BlockSpec((1, 1, d), lambda i, be, nu: (be[i], 0, 0)),
                pl.BlockSpec((GLU_GROUP, GLU_GROUP), lambda i, be, nu: (0, 0)),
            ],
            out_specs=pl.BlockSpec((d // 2 // SC_COLS, bm, SC_COLS), lambda i, be, nu: (0, i, 0)),
            scratch_shapes=[pltpu.VMEM((d, 2 * f), BF16), pltpu.VMEM((f, d), BF16)],
        ),
        out_shape=jax.ShapeDtypeStruct((d // 2 // SC_COLS, p, SC_COLS), jnp.int32),
        compiler_params=_params(1),
        name="expert_ffn",
    )(block_e, n_used, x_buf, wgu, bgu, wd, bd, jnp.asarray(perm, BF16))


def _final_kernel(h1_ref, yg_ref, rf_ref, p_ref, gple_ref, wpg_ref, wpp_ref, gfin_ref, o_ref):
    rf = rf_ref[...]
    h2 = h1_ref[...]
    for j in range(TOP_K):
        h2 = h2 + _unpack_pairs([yg_ref[c, j] for c in range(yg_ref.shape[0])]) * rf[:, j:j + 1]
    gate = _sigmoid(_dot(_rms(h2, gple_ref[...]).astype(BF16), wpg_ref[...]))
    h3 = h2 + gate * _dot(p_ref[...].astype(BF16), wpp_ref[...])
    o_ref[...] = _rms(h3, gfin_ref[...])


def _final(h1, yg, rf, p2, gple, wpg, wpp, gfin, tm):
    n, d = h1.shape
    pd = p2.shape[1]
    row = lambda i: (i, 0)
    const = lambda i: (0, 0)
    return pl.pallas_call(
        _final_kernel,
        grid=(n // tm,),
        in_specs=[
            pl.BlockSpec((tm, d), row), pl.BlockSpec((d // 2 // SC_COLS, TOP_K, tm, SC_COLS), lambda i: (0, 0, i, 0)),
            pl.BlockSpec((tm, LANES), row), pl.BlockSpec((tm, pd), row),
            pl.BlockSpec((1, d), const), pl.BlockSpec((d, d), const), pl.BlockSpec((pd, d), const),
            pl.BlockSpec((1, d), const),
        ],
        out_specs=pl.BlockSpec((tm, d), row),
        out_shape=jax.ShapeDtypeStruct((n, d), F32),
        compiler_params=_params(1),
        name="combine_ple_norm",
    )(h1, yg, rf, p2, gple, wpg, wpp, gfin)


def _layer(h, p, mix_norm, w_in, b_forget, w_branch, w_merge_gate, b_merge_gate, w_out, ffn_norm,
           w_router, b_router, w_gate_up, b_gate_up, w_down, b_down, ple_norm, w_ple_gate, w_ple_proj,
           final_norm, *, tm, tq, chunk, bm):
    b, s, d = h.shape
    n = b * s
    x2 = h.reshape(n, d)
    row = lambda t: t.reshape(1, -1)

    c0 = 3 * FOX_W
    w_main = jnp.concatenate([w_in[:, :2 * FOX_W], w_in[:, c0 + FOX_HEADS:]], axis=1).astype(BF16)
    w_vt = w_in[:, 2 * FOX_W:c0].T.astype(BF16)
    w_f = jnp.pad(w_in[:, c0:c0 + FOX_HEADS], ((0, 0), (0, LANES - FOX_HEADS))).astype(BF16)
    b_f = jnp.pad(b_forget, (0, LANES - FOX_HEADS)).reshape(1, LANES)
    half = RET_KEY_DIM // 2
    inv = ROPE_BASE ** (-jnp.arange(half, dtype=F32) / half)
    ang = jnp.arange(s).astype(F32)[:, None] * inv[None, :]
    cos_t = jnp.tile(jnp.cos(ang), (1, RET_QK_W // half))
    sin_t = jnp.tile(jnp.sin(ang), (1, RET_QK_W // half))

    q_aug, k_aug, v_t, rq, rk, rv, rg = _inproj(x2, row(mix_norm), w_main, w_vt, w_f, b_f, cos_t, sin_t, s, tm)
    fox = _fox(q_aug, k_aug, v_t, b, s, tq)

    lg = jnp.log1p(-jnp.exp2(-5.0 - jnp.arange(RET_HEADS, dtype=F32)))
    ro = _retention(lg, rq.reshape(b, s, RET_QK_W), rk.reshape(b, s, RET_QK_W),
                    rv.reshape(b, s, RET_V_W), rg.reshape(b, s, RET_V_W), chunk)

    w_r = jnp.pad(w_router, ((0, 0), (0, LANES - N_EXPERTS))).astype(BF16)
    b_r = jnp.concatenate([b_router, jnp.full((LANES - N_EXPERTS,), -1e30, F32)]).reshape(1, LANES)
    h1, hn, rf, ri, cnt = _merge(
        x2, fox.reshape(n, FOX_W), ro.reshape(n, RET_V_W), row(mix_norm), w_merge_gate.astype(BF16),
        row(b_merge_gate), w_branch.astype(BF16), w_out.astype(BF16), row(ffn_norm), w_r, b_r, tm)

    a = n * TOP_K
    nb = -(-(a + N_EXPERTS * (bm - 1)) // bm)
    pr = nb * bm
    e_idx = ri[:, :TOP_K]
    rank = ri[:, TOP_K:2 * TOP_K]
    counts = cnt[0, :N_EXPERTS].astype(jnp.int32)
    padded = (counts + bm - 1) // bm * bm
    pad_end = jnp.cumsum(padded)
    pad_start = pad_end - padded
    dest = pad_start[e_idx] + rank
    block_start = jnp.arange(nb, dtype=jnp.int32) * bm
    block_e = jnp.minimum(jnp.sum(block_start[:, None] >= pad_end[None, :], axis=1),
                          N_EXPERTS - 1).astype(jnp.int32)
    dest_t = dest.T

    ne, dd, f2 = w_gate_up.shape
    bgu = b_gate_up.reshape(ne, f2 // GLU_GROUP, LANES, 2).transpose(0, 1, 3, 2).reshape(ne, 1, f2)
    x_buf = _sc_dispatch(hn, dest_t, pr)
    n_used = (pad_end[-1:] // bm).astype(jnp.int32)
    y_buf = _experts(block_e, n_used, x_buf, w_gate_up, bgu, w_down, b_down.reshape(ne, 1, dd), bm)
    yg = _sc_gather(y_buf, dest_t.reshape(1, a)).reshape(d // 2 // SC_COLS, TOP_K, n, SC_COLS)

    out = _final(h1, yg, rf, p.reshape(n, -1), row(ple_norm), w_ple_gate.astype(BF16),
                 w_ple_proj.astype(BF16), row(final_norm), tm // 2)
    return out.reshape(b, s, d)


def kernel(x, p, mix_norm, w_in, b_forget, w_branch, w_merge_gate, b_merge_gate, w_out, ffn_norm, w_router,
           b_router, w_gate_up, b_gate_up, w_down, b_down, ple_norm, w_ple_gate, w_ple_proj, final_norm):
    depth = p.shape[0]
    assert depth == 1, "the final norm is fused into the (single) layer"
    return _layer(x, p[0], mix_norm[0], w_in[0], b_forget[0], w_branch[0], w_merge_gate[0], b_merge_gate[0],
                  w_out[0], ffn_norm[0], w_router[0], b_router[0], w_gate_up[0], b_gate_up[0], w_down[0],
                  b_down[0], ple_norm[0], w_ple_gate[0], w_ple_proj[0], final_norm,
                  tm=512, tq=512, chunk=256, bm=512)
```

```python
import functools

import numpy as np
import jax
import jax.numpy as jnp
from jax import lax
from jax.experimental import pallas as pl
from jax.experimental.pallas import tpu as pltpu
from jax.experimental.pallas import tpu_sc as plsc

FOX_HEADS = 8
FOX_HEAD_DIM = 64
RET_HEADS = 4
RET_KEY_DIM = 64
RET_VAL_DIM = 128
ROPE_BASE = 10000.0
N_EXPERTS = 32
TOP_K = 4
SWIGLU_LIMIT = 7.0
SWIGLU_ALPHA = 1.702
EPS = 1e-6

LANES = 128
FOX_W = FOX_HEADS * FOX_HEAD_DIM
RET_QK_W = RET_HEADS * RET_KEY_DIM
RET_V_W = RET_HEADS * RET_VAL_DIM
VMEM_LIMIT = 56 * 1024 * 1024
HIGH_HALF = -65536
SC_WINDOW = 128
SC_COLS = 256

F32 = jnp.float32
BF16 = jnp.bfloat16


def _rms(x, g):
    return x * lax.rsqrt(jnp.mean(x * x, axis=-1, keepdims=True) + EPS) * g


def _sigmoid(x):
    return 1.0 / (1.0 + jnp.exp(-x))


def _dot(a, b):
    return jnp.dot(a, b, preferred_element_type=F32)


def _dot_nt(a, b):
    return lax.dot_general(a, b, (((1,), (1,)), ((), ())), preferred_element_type=F32)


def _pack_pairs(v):
    bits = pltpu.bitcast(v.astype(BF16).astype(F32), jnp.int32)
    half = v.shape[1] // 2
    return lax.shift_right_logical(bits[:, :half], 16) | (bits[:, half:] & HIGH_HALF)


def _unpack_pairs(chunks):
    return jnp.concatenate([pltpu.bitcast(lax.shift_left(w, 16), F32) for w in chunks]
                           + [pltpu.bitcast(w & HIGH_HALF, F32) for w in chunks], axis=1)


def _store_chunks(ref, v):
    for c in range(ref.shape[0]):
        ref[c] = v[:, c * SC_COLS:(c + 1) * SC_COLS]


def _params(n_axes):
    return pltpu.CompilerParams(dimension_semantics=("arbitrary",) * n_axes,
                                vmem_limit_bytes=VMEM_LIMIT)


C_MID = FOX_HEADS
C_LO = 2 * FOX_HEADS
C_ONE = 3 * FOX_HEADS
N_AUG = 3


def _split3(t):
    hi = t.astype(BF16).astype(F32)
    r = t - hi
    mid = r.astype(BF16).astype(F32)
    lo = (r - mid).astype(BF16).astype(F32)
    return hi + pltpu.roll(mid, C_MID, 1) + pltpu.roll(lo, C_LO, 1)


def _inproj_kernel(x_ref, g_ref, w_ref, wvt_ref, wf_ref, bf_ref, cos_ref, sin_ref, pq_ref, pk_ref,
                   q_ref, k_ref, vt_ref, rq_ref, rk_ref, rv_ref, rg_ref, carry_scr, *, tiles_per_seq):
    i = pl.program_id(0)
    tm = x_ref.shape[0]

    @pl.when(i % tiles_per_seq == 0)
    def _():
        carry_scr[...] = jnp.zeros(carry_scr.shape, F32)

    xn = _rms(x_ref[...], g_ref[...]).astype(BF16)
    u = _dot(xn, w_ref[...])
    vt_ref[...] = _dot_nt(wvt_ref[...], xn).astype(BF16)

    lane = lax.broadcasted_iota(jnp.int32, (tm, LANES), 1)
    z = _dot(xn, wf_ref[...]) + bf_ref[...]
    lf = jnp.where(lane < FOX_HEADS, jnp.minimum(z, 0.0) - jnp.log1p(jnp.exp(-jnp.abs(z))), 0.0)
    r = lax.broadcasted_iota(jnp.int32, (tm, tm), 0)
    c = lax.broadcasted_iota(jnp.int32, (tm, tm), 1)
    tri = jnp.where(c <= r, 1.0, 0.0).astype(BF16)
    ps = _dot(tri, _split3(lf).astype(BF16))
    cum = ps + pltpu.roll(ps, LANES - C_MID, 1) + pltpu.roll(ps, LANES - C_LO, 1)
    cum = jnp.where(lane < FOX_HEADS, cum, 0.0) + carry_scr[...]
    carry_scr[...] = cum[tm - 1:tm, :]
    c3 = (_split3(cum) + jnp.where(lane == C_ONE, 1.0, 0.0)).astype(BF16)
    aug_q = _dot(c3, pq_ref[...])
    aug_k = _dot(c3, pk_ref[...])

    own = lane < FOX_HEAD_DIM
    fq = u[:, 0:FOX_W] * (FOX_HEAD_DIM ** -0.5)
    fk = u[:, FOX_W:2 * FOX_W]
    for src, aug, dst in ((fq, aug_q, q_ref), (fk, aug_k, k_ref)):
        for j in range(FOX_W // LANES):
            blk = src[:, j * LANES:(j + 1) * LANES]
            e0, e1 = 2 * j * LANES, (2 * j + 1) * LANES
            dst[:, e0:e0 + LANES] = jnp.where(own, blk, aug[:, e0:e0 + LANES]).astype(BF16)
            dst[:, e1:e1 + LANES] = jnp.where(own, pltpu.roll(blk, FOX_HEAD_DIM, 1),
                                               aug[:, e1:e1 + LANES]).astype(BF16)

    o = 2 * FOX_W
    rq = u[:, o:o + RET_QK_W]; o += RET_QK_W
    rk = u[:, o:o + RET_QK_W]; o += RET_QK_W
    rv_ref[...] = u[:, o:o + RET_V_W].astype(BF16); o += RET_V_W
    rg_ref[...] = u[:, o:o + RET_V_W]

    half = RET_KEY_DIM // 2
    lane_r = lax.broadcasted_iota(jnp.int32, rq.shape, 1)
    first = (lane_r % RET_KEY_DIM) < half
    cos = cos_ref[...]
    sin = sin_ref[...]

    def rot(t):
        partner = jnp.where(first, -pltpu.roll(t, RET_QK_W - half, 1), pltpu.roll(t, half, 1))
        return t * cos + partner * sin

    rq_ref[...] = rot(rq).astype(BF16)
    rk_ref[...] = (rot(rk) * (RET_KEY_DIM ** -0.5)).astype(BF16)


def _placement():
    pq = np.zeros((LANES, FOX_HEADS * LANES), np.float32)
    pk = np.zeros((LANES, FOX_HEADS * LANES), np.float32)
    for h in range(FOX_HEADS):
        base = h * LANES + FOX_HEAD_DIM
        for part, src in enumerate((h, C_MID + h, C_LO + h)):
            pq[C_ONE, base + part] = 1.0
            pq[src, base + N_AUG + part] = 1.0
            pk[src, base + part] = -1.0
            pk[C_ONE, base + N_AUG + part] = 1.0
    return jnp.asarray(pq, BF16), jnp.asarray(pk, BF16)


def _inproj(x2, g, w_main, w_vt, w_f, b_f, cos_t, sin_t, seq, tm):
    n, d = x2.shape
    wn = w_main.shape[1]
    spt = seq // tm
    aw = FOX_HEADS * LANES
    pq, pk = _placement()
    row = lambda i: (i, 0)
    const = lambda i: (0, 0)
    pos = lambda i: (i % spt, 0)
    out_shape = [
        jax.ShapeDtypeStruct((n, aw), BF16), jax.ShapeDtypeStruct((n, aw), BF16),
        jax.ShapeDtypeStruct((FOX_W, n), BF16),
        jax.ShapeDtypeStruct((n, RET_QK_W), BF16), jax.ShapeDtypeStruct((n, RET_QK_W), BF16),
        jax.ShapeDtypeStruct((n, RET_V_W), BF16), jax.ShapeDtypeStruct((n, RET_V_W), F32),
    ]
    out_specs = [
        pl.BlockSpec((tm, aw), row), pl.BlockSpec((tm, aw), row),
        pl.BlockSpec((FOX_W, tm), lambda i: (0, i)),
        pl.BlockSpec((tm, RET_QK_W), row), pl.BlockSpec((tm, RET_QK_W), row),
        pl.BlockSpec((tm, RET_V_W), row), pl.BlockSpec((tm, RET_V_W), row),
    ]
    return pl.pallas_call(
        functools.partial(_inproj_kernel, tiles_per_seq=spt),
        grid=(n // tm,),
        in_specs=[
            pl.BlockSpec((tm, d), row), pl.BlockSpec((1, d), const),
            pl.BlockSpec((d, wn), const), pl.BlockSpec((FOX_W, d), const),
            pl.BlockSpec((d, LANES), const), pl.BlockSpec((1, LANES), const),
            pl.BlockSpec((tm, RET_QK_W), pos), pl.BlockSpec((tm, RET_QK_W), pos),
            pl.BlockSpec((LANES, aw), const), pl.BlockSpec((LANES, aw), const),
        ],
        out_specs=out_specs,
        out_shape=out_shape,
        scratch_shapes=[pltpu.VMEM((1, LANES), F32)],
        compiler_params=_params(1),
        name="inproj",
    )(x2, g, w_main, w_vt, w_f, b_f, cos_t, sin_t, pq, pk)


def _fox_kernel(q_ref, k_ref, vt_ref, o_ref, *, tq):
    seq = q_ref.shape[0]
    key = lax.broadcasted_iota(jnp.int32, (tq, tq), 0)
    qry = lax.broadcasted_iota(jnp.int32, (tq, tq), 1)
    vrow = lax.broadcasted_iota(jnp.int32, (LANES, tq), 0)
    one = jnp.ones((), BF16)
    items = [(qi, j) for qi in range(seq // tq) for j in range(qi + 1)]

    def logits(qi, j):
        out = []
        for a in range(2):
            sl = slice(a * LANES, (a + 1) * LANES)
            s = _dot_nt(k_ref[j * tq:(j + 1) * tq, sl], q_ref[qi * tq:(qi + 1) * tq, sl])
            out.append(jnp.where(key <= qry, s, -jnp.inf) if j == qi else s)
        return out

    s_cur = logits(*items[0])
    carry = None
    for w, (qi, j) in enumerate(items):
        s_next = logits(*items[w + 1]) if w + 1 < len(items) else None
        if j == 0:
            carry = [(jnp.full((1, tq), -jnp.inf, F32), jnp.zeros((LANES, tq), F32)) for _ in range(2)]
        v = vt_ref[:, j * tq:(j + 1) * tq]
        for a in range(2):
            m_prev, acc = carry[a]
            m_new = jnp.maximum(m_prev, jnp.max(s_cur[a], axis=0, keepdims=True))
            alpha = jnp.exp(m_prev - m_new)
            p = jnp.exp(s_cur[a] - m_new).astype(BF16)
            va = jnp.where((vrow // FOX_HEAD_DIM) == a, v, one)
            carry[a] = (m_new, alpha * acc + _dot(va, p))
        if j == qi:
            acc0, acc1 = carry[0][1], carry[1][1]
            ot = jnp.where(vrow < FOX_HEAD_DIM, acc0 / acc0[FOX_HEAD_DIM:FOX_HEAD_DIM + 1, :],
                           acc1 / acc1[0:1, :])
            o_ref[qi * tq:(qi + 1) * tq, :] = ot.T.astype(o_ref.dtype)
        s_cur = s_next


def _fox(q_aug, k_aug, v_t, batch, seq, tq):
    n = q_aug.shape[0]
    pairs = FOX_HEADS // 2
    kern = functools.partial(_fox_kernel, tq=tq)
    return pl.pallas_call(
        kern,
        grid=(batch, pairs),
        in_specs=[
            pl.BlockSpec((seq, 2 * LANES), lambda i, h: (i, h)),
            pl.BlockSpec((seq, 2 * LANES), lambda i, h: (i, h)),
            pl.BlockSpec((LANES, seq), lambda i, h: (h, i)),
        ],
        out_specs=pl.BlockSpec((seq, LANES), lambda i, h: (i, h)),
        out_shape=jax.ShapeDtypeStruct((n, FOX_W), BF16),
        compiler_params=_params(2),
        name="fox_attention",
    )(q_aug, k_aug, v_t)


def _ret_kernel(lg_ref, q_ref, k_ref, v_ref, g_ref, o_ref, state_scr, *, chunk):
    h = pl.program_id(1)
    lg = lg_ref[h]
    s = q_ref.shape[1]
    lane = lax.broadcasted_iota(jnp.int32, (chunk, LANES), 1)
    mine = (lane // RET_KEY_DIM) == (h % 2)
    ri = lax.broadcasted_iota(jnp.int32, (chunk, chunk), 0)
    ci = lax.broadcasted_iota(jnp.int32, (chunk, chunk), 1)
    diff = (ri - ci).astype(F32)
    inner = jnp.where(diff >= 0, jnp.exp(jnp.maximum(diff, 0.0) * lg), 0.0)
    pos = lax.broadcasted_iota(jnp.int32, (chunk, 1), 0).astype(F32)
    q_decay = jnp.exp((pos + 1.0) * lg)
    k_decay = jnp.exp((chunk - 1.0 - pos) * lg)
    chunk_decay = jnp.exp(jnp.full((1, 1), chunk, F32) * lg)
    state_scr[...] = jnp.zeros(state_scr.shape, F32)
    for c in range(s // chunk):
        sl = pl.ds(c * chunk, chunk)
        q = jnp.where(mine, q_ref[0, sl, :], jnp.zeros((), BF16))
        k = jnp.where(mine, k_ref[0, sl, :], jnp.zeros((), BF16))
        v = v_ref[0, sl, :]
        scores = _dot_nt(q, k) * inner
        state = state_scr[...]
        o = _dot(scores.astype(BF16), v) + _dot(q, state.astype(BF16)) * q_decay
        kd = (k.astype(F32) * k_decay).astype(BF16)
        state_scr[...] = state * chunk_decay + lax.dot_general(
            kd, v, (((0,), (0,)), ((), ())), preferred_element_type=F32)
        o = o * lax.rsqrt(jnp.mean(o * o, axis=-1, keepdims=True) + EPS)
        g = g_ref[0, sl, :]
        o_ref[0, sl, :] = (o * (g * _sigmoid(g))).astype(o_ref.dtype)


def _retention(lg, rq, rk, rv, rg, chunk):
    b, s, _ = rq.shape
    kern = functools.partial(_ret_kernel, chunk=chunk)
    qk_spec = pl.BlockSpec((1, s, LANES), lambda i, h, lg_ref: (i, 0, h // 2))
    v_spec = pl.BlockSpec((1, s, LANES), lambda i, h, lg_ref: (i, 0, h))
    return pl.pallas_call(
        kern,
        grid_spec=pltpu.PrefetchScalarGridSpec(
            num_scalar_prefetch=1,
            grid=(b, RET_HEADS),
            in_specs=[qk_spec, qk_spec, v_spec, v_spec],
            out_specs=v_spec,
            scratch_shapes=[pltpu.VMEM((LANES, RET_VAL_DIM), F32)],
        ),
        out_shape=jax.ShapeDtypeStruct((b, s, RET_V_W), BF16),
        compiler_params=_params(2),
        name="retention",
    )(lg, rq, rk, rv, rg)


def _merge_kernel(x_ref, fox_ref, ro_ref, gmix_ref, wmg_ref, bmg_ref, wb_ref, wout_ref, gffn_ref,
                  wr_ref, br_ref, h1_ref, hn_ref, rf_ref, ri_ref, cnt_ref, base_scr):
    i = pl.program_id(0)
    tm, d = x_ref.shape

    @pl.when(i == 0)
    def _():
        base_scr[...] = jnp.zeros(base_scr.shape, F32)

    x = x_ref[...]
    xn = _rms(x, gmix_ref[...]).astype(BF16)
    gate = _sigmoid(_dot(xn, wmg_ref[...]) + bmg_ref[...])
    merged = gate[:, :d] * _dot(fox_ref[...], wb_ref[0]) + gate[:, d:] * _dot(ro_ref[...], wb_ref[1])
    h1 = x + _dot(merged.astype(BF16), wout_ref[...])
    h1_ref[...] = h1
    hn = _rms(h1, gffn_ref[...]).astype(BF16)
    _store_chunks(hn_ref, _pack_pairs(hn))

    logits = _dot(hn, wr_ref[...]) + br_ref[...]
    lane = lax.broadcasted_iota(jnp.int32, logits.shape, 1)
    lane_f = lane.astype(F32)
    vals, idxs = [], []
    cur = logits
    for _ in range(TOP_K):
        m = jnp.max(cur, axis=-1, keepdims=True)
        idx = jnp.min(jnp.where(cur == m, lane_f, float(LANES)), axis=-1, keepdims=True)
        vals.append(m)
        idxs.append(idx)
        cur = jnp.where(lane_f == idx, -jnp.inf, cur)
    exps = [jnp.exp(v - vals[0]) for v in vals]
    denom = exps[0] + exps[1] + exps[2] + exps[3]

    onehot = jnp.zeros(logits.shape, F32)
    for idx in idxs:
        onehot = onehot + jnp.where(lane_f == idx, 1.0, 0.0)
    r = lax.broadcasted_iota(jnp.int32, (tm, tm), 0)
    c = lax.broadcasted_iota(jnp.int32, (tm, tm), 1)
    tri = jnp.where(c < r, 1.0, 0.0).astype(BF16)
    before = _dot(tri, onehot.astype(BF16)) + base_scr[...]
    rf = jnp.zeros(logits.shape, F32)
    ri = jnp.zeros(logits.shape, F32)
    for j in range(TOP_K):
        rank = jnp.sum(jnp.where(lane_f == idxs[j], before, 0.0), axis=-1, keepdims=True)
        rf = jnp.where(lane == j, exps[j] / denom, rf)
        ri = jnp.where(lane == j, idxs[j], ri)
        ri = jnp.where(lane == TOP_K + j, rank, ri)
    rf_ref[...] = rf
    ri_ref[...] = ri.astype(jnp.int32)
    base_scr[...] = base_scr[...] + jnp.sum(onehot, axis=0, keepdims=True)
    cnt_ref[...] = base_scr[...]


def _merge(x2, fox, ro, gmix, wmg, bmg, wb, wout, gffn, wr, br, tm):
    n, d = x2.shape
    row = lambda i: (i, 0)
    const = lambda i: (0, 0)
    return pl.pallas_call(
        _merge_kernel,
        grid=(n // tm,),
        in_specs=[
            pl.BlockSpec((tm, d), row), pl.BlockSpec((tm, FOX_W), row), pl.BlockSpec((tm, RET_V_W), row),
            pl.BlockSpec((1, d), const), pl.BlockSpec((d, 2 * d), const), pl.BlockSpec((1, 2 * d), const),
            pl.BlockSpec((2, FOX_W, d), lambda i: (0, 0, 0)), pl.BlockSpec((d, d), const),
            pl.BlockSpec((1, d), const), pl.BlockSpec((d, LANES), const), pl.BlockSpec((1, LANES), const),
        ],
        out_specs=[
            pl.BlockSpec((tm, d), row), pl.BlockSpec((d // 2 // SC_COLS, tm, SC_COLS), lambda i: (0, i, 0)),
            pl.BlockSpec((tm, LANES), row), pl.BlockSpec((tm, LANES), row),
            pl.BlockSpec((1, LANES), const),
        ],
        out_shape=[
            jax.ShapeDtypeStruct((n, d), F32), jax.ShapeDtypeStruct((d // 2 // SC_COLS, n, SC_COLS), jnp.int32),
            jax.ShapeDtypeStruct((n, LANES), F32), jax.ShapeDtypeStruct((n, LANES), jnp.int32),
            jax.ShapeDtypeStruct((1, LANES), F32),
        ],
        scratch_shapes=[pltpu.VMEM((1, LANES), F32)],
        compiler_params=_params(1),
        name="merge_router",
    )(x2, fox, ro, gmix, wmg, bmg, wb, wout, gffn, wr, br)


def _sc_mesh():
    return plsc.VectorSubcoreMesh(core_axis_name="core", subcore_axis_name="subcore")


def _sc_dispatch(rows, dest_t, n_out):
    chunks, n, w = rows.shape

    @functools.partial(pl.kernel, out_type=jax.ShapeDtypeStruct((chunks, n_out, w), rows.dtype),
                       mesh=_sc_mesh(), scratch_types=[], name="moe_dispatch")
    def run(x_hbm, i_hbm, o_hbm):
        for c in range(chunks):
            def body(x_vmem, i_vmem, c=c):
                for j in range(TOP_K):
                    pltpu.sync_copy(x_vmem, o_hbm.at[c].at[i_vmem.at[j]])

            pltpu.emit_pipeline(
                body,
                grid=(n // SC_WINDOW,),
                in_specs=[pl.BlockSpec((SC_WINDOW, w), lambda i: (i, 0)),
                          pl.BlockSpec((TOP_K, SC_WINDOW), lambda i: (0, i))],
                out_specs=[],
                core_axis_name=("core", "subcore"),
                dimension_semantics=(pltpu.PARALLEL,),
            )(x_hbm.at[c], i_hbm)

    return run(rows, dest_t)


def _sc_gather(table, idx):
    chunks, _, w = table.shape
    m = idx.shape[1]

    @functools.partial(pl.kernel, out_type=jax.ShapeDtypeStruct((chunks, m, w), table.dtype),
                       mesh=_sc_mesh(), scratch_types=[], name="moe_combine_gather")
    def run(t_hbm, i_hbm, o_hbm):
        for c in range(chunks):
            def body(i_vmem, o_vmem, c=c):
                pltpu.sync_copy(t_hbm.at[c].at[i_vmem.at[0]], o_vmem)

            pltpu.emit_pipeline(
                body,
                grid=(m // SC_WINDOW,),
                in_specs=[pl.BlockSpec((1, SC_WINDOW), lambda i: (0, i))],
                out_specs=[pl.BlockSpec((SC_WINDOW, w), lambda i: (i, 0))],
                core_axis_name=("core", "subcore"),
                dimension_semantics=(pltpu.PARALLEL,),
            )(i_hbm, o_hbm.at[c])

    return run(table, idx)


GLU_GROUP = 2 * LANES


def _expert_kernel(be_ref, used_ref, x_ref, wgu_ref, bgu_ref, wd_ref, bd_ref, perm_ref, y_ref, wgu_scr, wd_scr):
    i = pl.program_id(0)
    f2 = wgu_ref.shape[2]
    live = i < used_ref[0]

    @pl.when(jnp.logical_and(live, jnp.logical_or(i == 0, be_ref[i] != be_ref[jnp.maximum(i - 1, 0)])))
    def _():
        for b in range(f2 // GLU_GROUP):
            cols = slice(b * GLU_GROUP, (b + 1) * GLU_GROUP)
            wgu_scr[:, cols] = _dot(wgu_ref[0, :, cols].astype(BF16), perm_ref[...]).astype(BF16)
        wd_scr[...] = wd_ref[0].astype(BF16)

    @pl.when(live)
    def _():
        x = _unpack_pairs([x_ref[c] for c in range(x_ref.shape[0])]).astype(BF16)
        gu = _dot(x, wgu_scr[...]) + bgu_ref[0]
        acts = []
        for b in range(f2 // GLU_GROUP):
            glu = jnp.minimum(gu[:, b * GLU_GROUP:b * GLU_GROUP + LANES], SWIGLU_LIMIT)
            lin = jnp.clip(gu[:, b * GLU_GROUP + LANES:(b + 1) * GLU_GROUP], -SWIGLU_LIMIT, SWIGLU_LIMIT)
            acts.append((glu * _sigmoid(SWIGLU_ALPHA * glu) * (lin + 1.0)).astype(BF16))
        y = _dot(jnp.concatenate(acts, axis=1), wd_scr[...]) + bd_ref[0]
        _store_chunks(y_ref, _pack_pairs(y))


def _experts(block_e, n_used, x_buf, wgu, bgu, wd, bd, bm):
    p = x_buf.shape[1]
    f, d = wd.shape[1:]
    perm = np.zeros((GLU_GROUP, GLU_GROUP), np.float32)
    for c in range(LANES):
        perm[2 * c, c] = 1.0
        perm[2 * c + 1, LANES + c] = 1.0
    return pl.pallas_call(
        _expert_kernel,
        grid_spec=pltpu.PrefetchScalarGridSpec(
            num_scalar_prefetch=2,
            grid=(p // bm,),
            in_specs=[
                pl.BlockSpec((d // 2 // SC_COLS, bm, SC_COLS), lambda i, be, nu: (0, i, 0)),
                pl.BlockSpec((1, d, 2 * f), lambda i, be, nu: (be[i], 0, 0)),
                pl.BlockSpec((1, 1, 2 * f), lambda i, be, nu: (be[i], 0, 0)),
                pl.BlockSpec((1, f, d), lambda i, be, nu: (be[i], 0, 0)),
                pl.BlockSpec((1, 1, d), lambda i, be, nu: (be[i], 0, 0)),
                pl.BlockSpec((GLU_GROUP, GLU_GROUP), lambda i, be, nu: (0, 0)),
            ],
            out_specs=pl.BlockSpec((d // 2 // SC_COLS, bm, SC_COLS), lambda i, be, nu: (0, i, 0)),
            scratch_shapes=[pltpu.VMEM((d, 2 * f), BF16), pltpu.VMEM((f, d), BF16)],
        ),
        out_shape=jax.ShapeDtypeStruct((d // 2 // SC_COLS, p, SC_COLS), jnp.int32),
        compiler_params=_params(1),
        name="expert_ffn",
    )(block_e, n_used, x_buf, wgu, bgu, wd, bd, jnp.asarray(perm, BF16))


def _final_kernel(h1_ref, yg_ref, rf_ref, p_ref, gple_ref, wpg_ref, wpp_ref, gfin_ref, o_ref):
    rf = rf_ref[...]
    h2 = h1_ref[...]
    for j in range(TOP_K):
        h2 = h2 + _unpack_pairs([yg_ref[c, j] for c in range(yg_ref.shape[0])]) * rf[:, j:j + 1]
    gate = _sigmoid(_dot(_rms(h2, gple_ref[...]).astype(BF16), wpg_ref[...]))
    h3 = h2 + gate * _dot(p_ref[...].astype(BF16), wpp_ref[...])
    o_ref[...] = _rms(h3, gfin_ref[...])


def _final(h1, yg, rf, p2, gple, wpg, wpp, gfin, tm):
    n, d = h1.shape
    pd = p2.shape[1]
    row = lambda i: (i, 0)
    const = lambda i: (0, 0)
    return pl.pallas_call(
        _final_kernel,
        grid=(n // tm,),
        in_specs=[
            pl.BlockSpec((tm, d), row), pl.BlockSpec((d // 2 // SC_COLS, TOP_K, tm, SC_COLS), lambda i: (0, 0, i, 0)),
            pl.BlockSpec((tm, LANES), row), pl.BlockSpec((tm, pd), row),
            pl.BlockSpec((1, d), const), pl.BlockSpec((d, d), const), pl.BlockSpec((pd, d), const),
            pl.BlockSpec((1, d), const),
        ],
        out_specs=pl.BlockSpec((tm, d), row),
        out_shape=jax.ShapeDtypeStruct((n, d), F32),
        compiler_params=_params(1),
        name="combine_ple_norm",
    )(h1, yg, rf, p2, gple, wpg, wpp, gfin)


def _layer(h, p, mix_norm, w_in, b_forget, w_branch, w_merge_gate, b_merge_gate, w_out, ffn_norm,
           w_router, b_router, w_gate_up, b_gate_up, w_down, b_down, ple_norm, w_ple_gate, w_ple_proj,
           final_norm, *, tm, tq, chunk, bm):
    b, s, d = h.shape
    n = b * s
    x2 = h.reshape(n, d)
    row = lambda t: t.reshape(1, -1)

    c0 = 3 * FOX_W
    w_main = jnp.concatenate([w_in[:, :2 * FOX_W], w_in[:, c0 + FOX_HEADS:]], axis=1).astype(BF16)
    w_vt = w_in[:, 2 * FOX_W:c0].T.astype(BF16)
    w_f = jnp.pad(w_in[:, c0:c0 + FOX_HEADS], ((0, 0), (0, LANES - FOX_HEADS))).astype(BF16)
    b_f = jnp.pad(b_forget, (0, LANES - FOX_HEADS)).reshape(1, LANES)
    half = RET_KEY_DIM // 2
    inv = ROPE_BASE ** (-jnp.arange(half, dtype=F32) / half)
    ang = jnp.arange(s).astype(F32)[:, None] * inv[None, :]
    cos_t = jnp.tile(jnp.cos(ang), (1, RET_QK_W // half))
    sin_t = jnp.tile(jnp.sin(ang), (1, RET_QK_W // half))

    q_aug, k_aug, v_t, rq, rk, rv, rg = _inproj(x2, row(mix_norm), w_main, w_vt, w_f, b_f, cos_t, sin_t, s, tm)
    fox = _fox(q_aug, k_aug, v_t, b, s, tq)

    lg = jnp.log1p(-jnp.exp2(-5.0 - jnp.arange(RET_HEADS, dtype=F32)))
    ro = _retention(lg, rq.reshape(b, s, RET_QK_W), rk.reshape(b, s, RET_QK_W),
                    rv.reshape(b, s, RET_V_W), rg.reshape(b, s, RET_V_W), chunk)

    w_r = jnp.pad(w_router, ((0, 0), (0, LANES - N_EXPERTS))).astype(BF16)
    b_r = jnp.concatenate([b_router, jnp.full((LANES - N_EXPERTS,), -1e30, F32)]).reshape(1, LANES)
    h1, hn, rf, ri, cnt = _merge(
        x2, fox.reshape(n, FOX_W), ro.reshape(n, RET_V_W), row(mix_norm), w_merge_gate.astype(BF16),
        row(b_merge_gate), w_branch.astype(BF16), w_out.astype(BF16), row(ffn_norm), w_r, b_r, tm)

    a = n * TOP_K
    nb = -(-(a + N_EXPERTS * (bm - 1)) // bm)
    pr = nb * bm
    e_idx = ri[:, :TOP_K]
    rank = ri[:, TOP_K:2 * TOP_K]
    counts = cnt[0, :N_EXPERTS].astype(jnp.int32)
    padded = (counts + bm - 1) // bm * bm
    pad_end = jnp.cumsum(padded)
    pad_start = pad_end - padded
    dest = pad_start[e_idx] + rank
    block_start = jnp.arange(nb, dtype=jnp.int32) * bm
    block_e = jnp.minimum(jnp.sum(block_start[:, None] >= pad_end[None, :], axis=1),
                          N_EXPERTS - 1).astype(jnp.int32)
    dest_t = dest.T

    ne, dd, f2 = w_gate_up.shape
    bgu = b_gate_up.reshape(ne, f2 // GLU_GROUP, LANES, 2).transpose(0, 1, 3, 2).reshape(ne, 1, f2)
    x_buf = _sc_dispatch(hn, dest_t, pr)
    n_used = (pad_end[-1:] // bm).astype(jnp.int32)
    y_buf = _experts(block_e, n_used, x_buf, w_gate_up, bgu, w_down, b_down.reshape(ne, 1, dd), bm)
    yg = _sc_gather(y_buf, dest_t.reshape(1, a)).reshape(d // 2 // SC_COLS, TOP_K, n, SC_COLS)

    out = _final(h1, yg, rf, p.reshape(n, -1), row(ple_norm), w_ple_gate.astype(BF16),
                 w_ple_proj.astype(BF16), row(final_norm), tm // 2)
    return out.reshape(b, s, d)


def kernel(x, p, mix_norm, w_in, b_forget, w_branch, w_merge_gate, b_merge_gate, w_out, ffn_norm, w_router,
           b_router, w_gate_up, b_gate_up, w_down, b_down, ple_norm, w_ple_gate, w_ple_proj, final_norm):
    depth = p.shape[0]
    assert depth == 1, "the final norm is fused into the (single) layer"
    return _layer(x, p[0], mix_norm[0], w_in[0], b_forget[0], w_branch[0], w_merge_gate[0], b_merge_gate[0],
                  w_out[0], ffn_norm[0], w_router[0], b_router[0], w_gate_up[0], b_gate_up[0], w_down[0],
                  b_down[0], ple_norm[0], w_ple_gate[0], w_ple_proj[0], final_norm,
                  tm=512, tq=512, chunk=256, bm=512)
```

```python
import functools

import numpy as np
import jax
import jax.numpy as jnp
from jax import lax
from jax.experimental import pallas as pl
from jax.experimental.pallas import tpu as pltpu
from jax.experimental.pallas import tpu_sc as plsc

FOX_HEADS = 8
FOX_HEAD_DIM = 64
RET_HEADS = 4
RET_KEY_DIM = 64
RET_VAL_DIM = 128
ROPE_BASE = 10000.0
N_EXPERTS = 32
TOP_K = 4
SWIGLU_LIMIT = 7.0
SWIGLU_ALPHA = 1.702
EPS = 1e-6

LANES = 128
FOX_W = FOX_HEADS * FOX_HEAD_DIM
RET_QK_W = RET_HEADS * RET_KEY_DIM
RET_V_W = RET_HEADS * RET_VAL_DIM
VMEM_LIMIT = 56 * 1024 * 1024
HIGH_HALF = -65536
SC_WINDOW = 128
SC_COLS = 256

F32 = jnp.float32
BF16 = jnp.bfloat16


def _rms(x, g):
    return x * lax.rsqrt(jnp.mean(x * x, axis=-1, keepdims=True) + EPS) * g


def _sigmoid(x):
    return 1.0 / (1.0 + jnp.exp(-x))


def _dot(a, b):
    return jnp.dot(a, b, preferred_element_type=F32)


def _dot_nt(a, b):
    return lax.dot_general(a, b, (((1,), (1,)), ((), ())), preferred_element_type=F32)


def _pack_pairs(v):
    bits = pltpu.bitcast(v.astype(BF16).astype(F32), jnp.int32)
    half = v.shape[1] // 2
    return lax.shift_right_logical(bits[:, :half], 16) | (bits[:, half:] & HIGH_HALF)


def _unpack_pairs(chunks):
    return jnp.concatenate([pltpu.bitcast(lax.shift_left(w, 16), F32) for w in chunks]
                           + [pltpu.bitcast(w & HIGH_HALF, F32) for w in chunks], axis=1)


def _store_chunks(ref, v, rows=slice(None)):
    for c in range(ref.shape[0]):
        ref[c, rows, :] = v[:, c * SC_COLS:(c + 1) * SC_COLS]


def _params(n_axes):
    return pltpu.CompilerParams(dimension_semantics=("arbitrary",) * n_axes,
                                vmem_limit_bytes=VMEM_LIMIT)


C_MID = FOX_HEADS
C_LO = 2 * FOX_HEADS
C_ONE = 3 * FOX_HEADS
N_AUG = 3


def _split3(t):
    hi = t.astype(BF16).astype(F32)
    r = t - hi
    mid = r.astype(BF16).astype(F32)
    lo = (r - mid).astype(BF16).astype(F32)
    return hi + pltpu.roll(mid, C_MID, 1) + pltpu.roll(lo, C_LO, 1)


def _inproj_kernel(x_ref, g_ref, w_ref, wvt_ref, wf_ref, bf_ref, cos_ref, sin_ref, pq_ref, pk_ref,
                   q_ref, k_ref, vt_ref, rq_ref, rk_ref, rv_ref, rg_ref, carry_scr, *, tiles_per_seq):
    i = pl.program_id(0)
    tm = x_ref.shape[0]

    @pl.when(i % tiles_per_seq == 0)
    def _():
        carry_scr[...] = jnp.zeros(carry_scr.shape, F32)

    xn = _rms(x_ref[...], g_ref[...]).astype(BF16)
    u = _dot(xn, w_ref[...])
    vt_ref[...] = _dot_nt(wvt_ref[...], xn).astype(BF16)

    lane = lax.broadcasted_iota(jnp.int32, (tm, LANES), 1)
    z = _dot(xn, wf_ref[...]) + bf_ref[...]
    lf = jnp.where(lane < FOX_HEADS, jnp.minimum(z, 0.0) - jnp.log1p(jnp.exp(-jnp.abs(z))), 0.0)
    r = lax.broadcasted_iota(jnp.int32, (tm, tm), 0)
    c = lax.broadcasted_iota(jnp.int32, (tm, tm), 1)
    tri = jnp.where(c <= r, 1.0, 0.0).astype(BF16)
    ps = _dot(tri, _split3(lf).astype(BF16))
    cum = ps + pltpu.roll(ps, LANES - C_MID, 1) + pltpu.roll(ps, LANES - C_LO, 1)
    cum = jnp.where(lane < FOX_HEADS, cum, 0.0) + carry_scr[...]
    carry_scr[...] = cum[tm - 1:tm, :]
    c3 = (_split3(cum) + jnp.where(lane == C_ONE, 1.0, 0.0)).astype(BF16)
    aug_q = _dot(c3, pq_ref[...])
    aug_k = _dot(c3, pk_ref[...])

    own = lane < FOX_HEAD_DIM
    fq = u[:, 0:FOX_W] * (FOX_HEAD_DIM ** -0.5)
    fk = u[:, FOX_W:2 * FOX_W]
    for src, aug, dst in ((fq, aug_q, q_ref), (fk, aug_k, k_ref)):
        for j in range(FOX_W // LANES):
            blk = src[:, j * LANES:(j + 1) * LANES]
            e0, e1 = 2 * j * LANES, (2 * j + 1) * LANES
            dst[:, e0:e0 + LANES] = jnp.where(own, blk, aug[:, e0:e0 + LANES]).astype(BF16)
            dst[:, e1:e1 + LANES] = jnp.where(own, pltpu.roll(blk, FOX_HEAD_DIM, 1),
                                               aug[:, e1:e1 + LANES]).astype(BF16)

    o = 2 * FOX_W
    rq = u[:, o:o + RET_QK_W]; o += RET_QK_W
    rk = u[:, o:o + RET_QK_W]; o += RET_QK_W
    rv_ref[...] = u[:, o:o + RET_V_W].astype(BF16); o += RET_V_W
    rg_ref[...] = u[:, o:o + RET_V_W]

    half = RET_KEY_DIM // 2
    lane_r = lax.broadcasted_iota(jnp.int32, rq.shape, 1)
    first = (lane_r % RET_KEY_DIM) < half
    cos = cos_ref[...]
    sin = sin_ref[...]

    def rot(t):
        partner = jnp.where(first, -pltpu.roll(t, RET_QK_W - half, 1), pltpu.roll(t, half, 1))
        return t * cos + partner * sin

    rq_ref[...] = rot(rq).astype(BF16)
    rk_ref[...] = (rot(rk) * (RET_KEY_DIM ** -0.5)).astype(BF16)


def _placement():
    pq = np.zeros((LANES, FOX_HEADS * LANES), np.float32)
    pk = np.zeros((LANES, FOX_HEADS * LANES), np.float32)
    for h in range(FOX_HEADS):
        base = h * LANES + FOX_HEAD_DIM
        for part, src in enumerate((h, C_MID + h, C_LO + h)):
            pq[C_ONE, base + part] = 1.0
            pq[src, base + N_AUG + part] = 1.0
            pk[src, base + part] = -1.0
            pk[C_ONE, base + N_AUG + part] = 1.0
    return jnp.asarray(pq, BF16), jnp.asarray(pk, BF16)


def _inproj(x2, g, w_main, w_vt, w_f, b_f, cos_t, sin_t, seq, tm):
    n, d = x2.shape
    wn = w_main.shape[1]
    spt = seq // tm
    aw = FOX_HEADS * LANES
    pq, pk = _placement()
    row = lambda i: (i, 0)
    const = lambda i: (0, 0)
    pos = lambda i: (i % spt, 0)
    out_shape = [
        jax.ShapeDtypeStruct((n, aw), BF16), jax.ShapeDtypeStruct((n, aw), BF16),
        jax.ShapeDtypeStruct((FOX_W, n), BF16),
        jax.ShapeDtypeStruct((n, RET_QK_W), BF16), jax.ShapeDtypeStruct((n, RET_QK_W), BF16),
        jax.ShapeDtypeStruct((n, RET_V_W), BF16), jax.ShapeDtypeStruct((n, RET_V_W), F32),
    ]
    out_specs = [
        pl.BlockSpec((tm, aw), row), pl.BlockSpec((tm, aw), row),
        pl.BlockSpec((FOX_W, tm), lambda i: (0, i)),
        pl.BlockSpec((tm, RET_QK_W), row), pl.BlockSpec((tm, RET_QK_W), row),
        pl.BlockSpec((tm, RET_V_W), row), pl.BlockSpec((tm, RET_V_W), row),
    ]
    return pl.pallas_call(
        functools.partial(_inproj_kernel, tiles_per_seq=spt),
        grid=(n // tm,),
        in_specs=[
            pl.BlockSpec((tm, d), row), pl.BlockSpec((1, d), const),
            pl.BlockSpec((d, wn), const), pl.BlockSpec((FOX_W, d), const),
            pl.BlockSpec((d, LANES), const), pl.BlockSpec((1, LANES), const),
            pl.BlockSpec((tm, RET_QK_W), pos), pl.BlockSpec((tm, RET_QK_W), pos),
            pl.BlockSpec((LANES, aw), const), pl.BlockSpec((LANES, aw), const),
        ],
        out_specs=out_specs,
        out_shape=out_shape,
        scratch_shapes=[pltpu.VMEM((1, LANES), F32)],
        compiler_params=_params(1),
        name="inproj",
    )(x2, g, w_main, w_vt, w_f, b_f, cos_t, sin_t, pq, pk)


def _fox_kernel(q_ref, k_ref, vt_ref, o_ref, *, tq):
    seq = q_ref.shape[0]
    key = lax.broadcasted_iota(jnp.int32, (tq, tq), 0)
    qry = lax.broadcasted_iota(jnp.int32, (tq, tq), 1)
    vrow = lax.broadcasted_iota(jnp.int32, (LANES, tq), 0)
    one = jnp.ones((), BF16)
    items = [(qi, j) for qi in range(seq // tq) for j in range(qi + 1)]

    def logits(qi, j):
        out = []
        for a in range(2):
            sl = slice(a * LANES, (a + 1) * LANES)
            s = _dot_nt(k_ref[j * tq:(j + 1) * tq, sl], q_ref[qi * tq:(qi + 1) * tq, sl])
            out.append(jnp.where(key <= qry, s, -jnp.inf) if j == qi else s)
        return out

    s_cur = logits(*items[0])
    carry = None
    for w, (qi, j) in enumerate(items):
        s_next = logits(*items[w + 1]) if w + 1 < len(items) else None
        if j == 0:
            carry = [(jnp.full((1, tq), -jnp.inf, F32), jnp.zeros((LANES, tq), F32)) for _ in range(2)]
        v = vt_ref[:, j * tq:(j + 1) * tq]
        for a in range(2):
            m_prev, acc = carry[a]
            m_new = jnp.maximum(m_prev, jnp.max(s_cur[a], axis=0, keepdims=True))
            alpha = jnp.exp(m_prev - m_new)
            p = jnp.exp(s_cur[a] - m_new).astype(BF16)
            va = jnp.where((vrow // FOX_HEAD_DIM) == a, v, one)
            carry[a] = (m_new, alpha * acc + _dot(va, p))
        if j == qi:
            acc0, acc1 = carry[0][1], carry[1][1]
            ot = jnp.where(vrow < FOX_HEAD_DIM, acc0 / acc0[FOX_HEAD_DIM:FOX_HEAD_DIM + 1, :],
                           acc1 / acc1[0:1, :])
            o_ref[qi * tq:(qi + 1) * tq, :] = ot.T.astype(o_ref.dtype)
        s_cur = s_next


def _fox(q_aug, k_aug, v_t, batch, seq, tq):
    n = q_aug.shape[0]
    pairs = FOX_HEADS // 2
    kern = functools.partial(_fox_kernel, tq=tq)
    return pl.pallas_call(
        kern,
        grid=(batch, pairs),
        in_specs=[
            pl.BlockSpec((seq, 2 * LANES), lambda i, h: (i, h)),
            pl.BlockSpec((seq, 2 * LANES), lambda i, h: (i, h)),
            pl.BlockSpec((LANES, seq), lambda i, h: (h, i)),
        ],
        out_specs=pl.BlockSpec((seq, LANES), lambda i, h: (i, h)),
        out_shape=jax.ShapeDtypeStruct((n, FOX_W), BF16),
        compiler_params=_params(2),
        name="fox_attention",
    )(q_aug, k_aug, v_t)


def _ret_kernel(lg_ref, q_ref, k_ref, v_ref, g_ref, o_ref, state_scr, *, chunk):
    h = pl.program_id(1)
    lg = lg_ref[h]
    s = q_ref.shape[1]
    lane = lax.broadcasted_iota(jnp.int32, (chunk, LANES), 1)
    mine = (lane // RET_KEY_DIM) == (h % 2)
    ri = lax.broadcasted_iota(jnp.int32, (chunk, chunk), 0)
    ci = lax.broadcasted_iota(jnp.int32, (chunk, chunk), 1)
    diff = (ri - ci).astype(F32)
    inner = jnp.where(diff >= 0, jnp.exp(jnp.maximum(diff, 0.0) * lg), 0.0)
    pos = lax.broadcasted_iota(jnp.int32, (chunk, 1), 0).astype(F32)
    q_decay = jnp.exp((pos + 1.0) * lg)
    k_decay = jnp.exp((chunk - 1.0 - pos) * lg)
    chunk_decay = jnp.exp(jnp.full((1, 1), chunk, F32) * lg)
    state_scr[...] = jnp.zeros(state_scr.shape, F32)
    for c in range(s // chunk):
        sl = pl.ds(c * chunk, chunk)
        q = jnp.where(mine, q_ref[0, sl, :], jnp.zeros((), BF16))
        k = jnp.where(mine, k_ref[0, sl, :], jnp.zeros((), BF16))
        v = v_ref[0, sl, :]
        scores = _dot_nt(q, k) * inner
        state = state_scr[...]
        o = _dot(scores.astype(BF16), v) + _dot(q, state.astype(BF16)) * q_decay
        kd = (k.astype(F32) * k_decay).astype(BF16)
        state_scr[...] = state * chunk_decay + lax.dot_general(
            kd, v, (((0,), (0,)), ((), ())), preferred_element_type=F32)
        o = o * lax.rsqrt(jnp.mean(o * o, axis=-1, keepdims=True) + EPS)
        g = g_ref[0, sl, :]
        o_ref[0, sl, :] = (o * (g * _sigmoid(g))).astype(o_ref.dtype)


def _retention(lg, rq, rk, rv, rg, chunk):
    b, s, _ = rq.shape
    kern = functools.partial(_ret_kernel, chunk=chunk)
    qk_spec = pl.BlockSpec((1, s, LANES), lambda i, h, lg_ref: (i, 0, h // 2))
    v_spec = pl.BlockSpec((1, s, LANES), lambda i, h, lg_ref: (i, 0, h))
    return pl.pallas_call(
        kern,
        grid_spec=pltpu.PrefetchScalarGridSpec(
            num_scalar_prefetch=1,
            grid=(b, RET_HEADS),
            in_specs=[qk_spec, qk_spec, v_spec, v_spec],
            out_specs=v_spec,
            scratch_shapes=[pltpu.VMEM((LANES, RET_VAL_DIM), F32)],
        ),
        out_shape=jax.ShapeDtypeStruct((b, s, RET_V_W), BF16),
        compiler_params=_params(2),
        name="retention",
    )(lg, rq, rk, rv, rg)


MERGE_SUB = 256


def _interleave(generators):
    live = []
    pending = list(generators)
    while pending or live:
        if pending:
            live.append(pending.pop(0))
        for g in list(live):
            try:
                next(g)
            except StopIteration:
                live.remove(g)


def _merge_kernel(x_ref, fox_ref, ro_ref, gmix_ref, wmg_ref, bmg_ref, wb_ref, wout_ref, gffn_ref,
                  wr_ref, br_ref, h1_ref, hn_ref, rf_ref, ri_ref, cnt_ref, base_scr):
    i = pl.program_id(0)
    tm, d = x_ref.shape
    sub = min(MERGE_SUB, tm)

    @pl.when(i == 0)
    def _():
        base_scr[...] = jnp.zeros(base_scr.shape, F32)

    lane = lax.broadcasted_iota(jnp.int32, (sub, LANES), 1)
    lane_f = lane.astype(F32)
    r = lax.broadcasted_iota(jnp.int32, (sub, sub), 0)
    c = lax.broadcasted_iota(jnp.int32, (sub, sub), 1)
    tri = jnp.where(c < r, 1.0, 0.0).astype(BF16)
    counts = [base_scr[...]]

    def sub_tile(t):
        rows = slice(t * sub, (t + 1) * sub)
        x = x_ref[rows, :]
        xn = _rms(x, gmix_ref[...]).astype(BF16)
        yield
        pre = _dot(xn, wmg_ref[...])
        yield
        gate = _sigmoid(pre + bmg_ref[...])
        yield
        pf = _dot(fox_ref[rows, :], wb_ref[0])
        pr = _dot(ro_ref[rows, :], wb_ref[1])
        yield
        merged = (gate[:, :d] * pf + gate[:, d:] * pr).astype(BF16)
        yield
        h1 = x + _dot(merged, wout_ref[...])
        yield
        h1_ref[rows, :] = h1
        hn = _rms(h1, gffn_ref[...]).astype(BF16)
        _store_chunks(hn_ref, _pack_pairs(hn), rows)
        yield
        logits = _dot(hn, wr_ref[...]) + br_ref[...]
        yield
        vals, idxs = [], []
        cur = logits
        for _ in range(TOP_K):
            m = jnp.max(cur, axis=-1, keepdims=True)
            idx = jnp.min(jnp.where(cur == m, lane_f, float(LANES)), axis=-1, keepdims=True)
            vals.append(m)
            idxs.append(idx)
            cur = jnp.where(lane_f == idx, -jnp.inf, cur)
        exps = [jnp.exp(v - vals[0]) for v in vals]
        denom = exps[0] + exps[1] + exps[2] + exps[3]
        onehot = jnp.zeros(logits.shape, F32)
        for idx in idxs:
            onehot = onehot + jnp.where(lane_f == idx, 1.0, 0.0)
        yield
        assert len(counts) == t + 1
        before = _dot(tri, onehot.astype(BF16)) + counts[t]
        counts.append(counts[t] + jnp.sum(onehot, axis=0, keepdims=True))
        rf = jnp.zeros(logits.shape, F32)
        ri = jnp.zeros(logits.shape, F32)
        for j in range(TOP_K):
            rank = jnp.sum(jnp.where(lane_f == idxs[j], before, 0.0), axis=-1, keepdims=True)
            rf = jnp.where(lane == j, exps[j] / denom, rf)
            ri = jnp.where(lane == j, idxs[j], ri)
            ri = jnp.where(lane == TOP_K + j, rank, ri)
        rf_ref[rows, :] = rf
        ri_ref[rows, :] = ri.astype(jnp.int32)

    _interleave(sub_tile(t) for t in range(tm // sub))
    base_scr[...] = counts[-1]
    cnt_ref[...] = counts[-1]


def _merge(x2, fox, ro, gmix, wmg, bmg, wb, wout, gffn, wr, br, tm):
    n, d = x2.shape
    row = lambda i: (i, 0)
    const = lambda i: (0, 0)
    return pl.pallas_call(
        _merge_kernel,
        grid=(n // tm,),
        in_specs=[
            pl.BlockSpec((tm, d), row), pl.BlockSpec((tm, FOX_W), row), pl.BlockSpec((tm, RET_V_W), row),
            pl.BlockSpec((1, d), const), pl.BlockSpec((d, 2 * d), const), pl.BlockSpec((1, 2 * d), const),
            pl.BlockSpec((2, FOX_W, d), lambda i: (0, 0, 0)), pl.BlockSpec((d, d), const),
            pl.BlockSpec((1, d), const), pl.BlockSpec((d, LANES), const), pl.BlockSpec((1, LANES), const),
        ],
        out_specs=[
            pl.BlockSpec((tm, d), row), pl.BlockSpec((d // 2 // SC_COLS, tm, SC_COLS), lambda i: (0, i, 0)),
            pl.BlockSpec((tm, LANES), row), pl.BlockSpec((tm, LANES), row),
            pl.BlockSpec((1, LANES), const),
        ],
        out_shape=[
            jax.ShapeDtypeStruct((n, d), F32), jax.ShapeDtypeStruct((d // 2 // SC_COLS, n, SC_COLS), jnp.int32),
            jax.ShapeDtypeStruct((n, LANES), F32), jax.ShapeDtypeStruct((n, LANES), jnp.int32),
            jax.ShapeDtypeStruct((1, LANES), F32),
        ],
        scratch_shapes=[pltpu.VMEM((1, LANES), F32)],
        compiler_params=_params(1),
        name="merge_router",
    )(x2, fox, ro, gmix, wmg, bmg, wb, wout, gffn, wr, br)


def _sc_mesh():
    return plsc.VectorSubcoreMesh(core_axis_name="core", subcore_axis_name="subcore")


def _sc_dispatch(rows, dest_t, n_out):
    chunks, n, w = rows.shape

    @functools.partial(pl.kernel, out_type=jax.ShapeDtypeStruct((chunks, n_out, w), rows.dtype),
                       mesh=_sc_mesh(), scratch_types=[], name="moe_dispatch")
    def run(x_hbm, i_hbm, o_hbm):
        for c in range(chunks):
            def body(x_vmem, i_vmem, c=c):
                for j in range(TOP_K):
                    pltpu.sync_copy(x_vmem, o_hbm.at[c].at[i_vmem.at[j]])

            pltpu.emit_pipeline(
                body,
                grid=(n // SC_WINDOW,),
                in_specs=[pl.BlockSpec((SC_WINDOW, w), lambda i: (i, 0)),
                          pl.BlockSpec((TOP_K, SC_WINDOW), lambda i: (0, i))],
                out_specs=[],
                core_axis_name=("core", "subcore"),
                dimension_semantics=(pltpu.PARALLEL,),
            )(x_hbm.at[c], i_hbm)

    return run(rows, dest_t)


def _sc_gather(table, idx):
    chunks, _, w = table.shape
    m = idx.shape[1]

    @functools.partial(pl.kernel, out_type=jax.ShapeDtypeStruct((chunks, m, w), table.dtype),
                       mesh=_sc_mesh(), scratch_types=[], name="moe_combine_gather")
    def run(t_hbm, i_hbm, o_hbm):
        for c in range(chunks):
            def body(i_vmem, o_vmem, c=c):
                pltpu.sync_copy(t_hbm.at[c].at[i_vmem.at[0]], o_vmem)

            pltpu.emit_pipeline(
                body,
                grid=(m // SC_WINDOW,),
                in_specs=[pl.BlockSpec((1, SC_WINDOW), lambda i: (0, i))],
                out_specs=[pl.BlockSpec((SC_WINDOW, w), lambda i: (i, 0))],
                core_axis_name=("core", "subcore"),
                dimension_semantics=(pltpu.PARALLEL,),
            )(i_hbm, o_hbm.at[c])

    return run(table, idx)


GLU_GROUP = 2 * LANES
EXPERT_SUB = 256


def _expert_kernel(be_ref, used_ref, x_ref, wgu_ref, bgu_ref, wd_ref, bd_ref, perm_ref, y_ref, wgu_scr, wd_scr):
    i = pl.program_id(0)
    f2 = wgu_ref.shape[2]
    live = i < used_ref[0]

    @pl.when(jnp.logical_and(live, jnp.logical_or(i == 0, be_ref[i] != be_ref[jnp.maximum(i - 1, 0)])))
    def _():
        for b in range(f2 // GLU_GROUP):
            cols = slice(b * GLU_GROUP, (b + 1) * GLU_GROUP)
            wgu_scr[:, cols] = _dot(wgu_ref[0, :, cols].astype(BF16), perm_ref[...]).astype(BF16)
        wd_scr[...] = wd_ref[0].astype(BF16)

    def sub_block(t, sub):
        rows = slice(t * sub, (t + 1) * sub)
        x = _unpack_pairs([x_ref[c, rows, :] for c in range(x_ref.shape[0])]).astype(BF16)
        yield
        gu = _dot(x, wgu_scr[...])
        yield
        gu = gu + bgu_ref[0]
        acts = []
        for b in range(f2 // GLU_GROUP):
            glu = jnp.minimum(gu[:, b * GLU_GROUP:b * GLU_GROUP + LANES], SWIGLU_LIMIT)
            lin = jnp.clip(gu[:, b * GLU_GROUP + LANES:(b + 1) * GLU_GROUP], -SWIGLU_LIMIT, SWIGLU_LIMIT)
            acts.append((glu * _sigmoid(SWIGLU_ALPHA * glu) * (lin + 1.0)).astype(BF16))
        act = jnp.concatenate(acts, axis=1)
        yield
        y = _dot(act, wd_scr[...])
        yield
        _store_chunks(y_ref, _pack_pairs(y + bd_ref[0]), rows)

    @pl.when(live)
    def _():
        bm = x_ref.shape[1]
        sub = min(EXPERT_SUB, bm)
        _interleave(sub_block(t, sub) for t in range(bm // sub))


def _experts(block_e, n_used, x_buf, wgu, bgu, wd, bd, bm):
    p = x_buf.shape[1]
    f, d = wd.shape[1:]
    perm = np.zeros((GLU_GROUP, GLU_GROUP), np.float32)
    for c in range(LANES):
        perm[2 * c, c] = 1.0
        perm[2 * c + 1, LANES + c] = 1.0
    return pl.pallas_call(
        _expert_kernel,
        grid_spec=pltpu.PrefetchScalarGridSpec(
            num_scalar_prefetch=2,
            grid=(p // bm,),
            in_specs=[
                pl.BlockSpec((d // 2 // SC_COLS, bm, SC_COLS), lambda i, be, nu: (0, i, 0)),
                pl.BlockSpec((1, d, 2 * f), lambda i, be, nu: (be[i], 0, 0)),
                pl.BlockSpec((1, 1, 2 * f), lambda i, be, nu: (be[i], 0, 0)),
                pl.BlockSpec((1, f, d), lambda i, be, nu: (be[i], 0, 0)),
                pl.BlockSpec((1, 1, d), lambda i, be, nu: (be[i], 0, 0)),
                pl.BlockSpec((GLU_GROUP, GLU_GROUP), lambda i, be, nu: (0, 0)),
            ],
            out_specs=pl.BlockSpec((d // 2 // SC_COLS, bm, SC_COLS), lambda i, be, nu: (0, i, 0)),
            scratch_shapes=[pltpu.VMEM((d, 2 * f), BF16), pltpu.VMEM((f, d), BF16)],
        ),
        out_shape=jax.ShapeDtypeStruct((d // 2 // SC_COLS, p, SC_COLS), jnp.int32),
        compiler_params=_params(1),
        name="expert_ffn",
    )(block_e, n_used, x_buf, wgu, bgu, wd, bd, jnp.asarray(perm, BF16))


def _final_kernel(h1_ref, yg_ref, rf_ref, p_ref, gple_ref, wpg_ref, wpp_ref, gfin_ref, o_ref):
    rf = rf_ref[...]
    h2 = h1_ref[...]
    for j in range(TOP_K):
        h2 = h2 + _unpack_pairs([yg_ref[c, j] for c in range(yg_ref.shape[0])]) * rf[:, j:j + 1]
    gate = _sigmoid(_dot(_rms(h2, gple_ref[...]).astype(BF16), wpg_ref[...]))
    h3 = h2 + gate * _dot(p_ref[...].astype(BF16), wpp_ref[...])
    o_ref[...] = _rms(h3, gfin_ref[...])


def _final(h1, yg, rf, p2, gple, wpg, wpp, gfin, tm):
    n, d = h1.shape
    pd = p2.shape[1]
    row = lambda i: (i, 0)
    const = lambda i: (0, 0)
    return pl.pallas_call(
        _final_kernel,
        grid=(n // tm,),
        in_specs=[
            pl.BlockSpec((tm, d), row), pl.BlockSpec((d // 2 // SC_COLS, TOP_K, tm, SC_COLS), lambda i: (0, 0, i, 0)),
            pl.BlockSpec((tm, LANES), row), pl.BlockSpec((tm, pd), row),
            pl.BlockSpec((1, d), const), pl.BlockSpec((d, d), const), pl.BlockSpec((pd, d), const),
            pl.BlockSpec((1, d), const),
        ],
        out_specs=pl.BlockSpec((tm, d), row),
        out_shape=jax.ShapeDtypeStruct((n, d), F32),
        compiler_params=_params(1),
        name="combine_ple_norm",
    )(h1, yg, rf, p2, gple, wpg, wpp, gfin)


def _layer(h, p, mix_norm, w_in, b_forget, w_branch, w_merge_gate, b_merge_gate, w_out, ffn_norm,
           w_router, b_router, w_gate_up, b_gate_up, w_down, b_down, ple_norm, w_ple_gate, w_ple_proj,
           final_norm, *, tm, tq, chunk, bm):
    b, s, d = h.shape
    n = b * s
    x2 = h.reshape(n, d)
    row = lambda t: t.reshape(1, -1)

    c0 = 3 * FOX_W
    w_main = jnp.concatenate([w_in[:, :2 * FOX_W], w_in[:, c0 + FOX_HEADS:]], axis=1).astype(BF16)
    w_vt = w_in[:, 2 * FOX_W:c0].T.astype(BF16)
    w_f = jnp.pad(w_in[:, c0:c0 + FOX_HEADS], ((0, 0), (0, LANES - FOX_HEADS))).astype(BF16)
    b_f = jnp.pad(b_forget, (0, LANES - FOX_HEADS)).reshape(1, LANES)
    half = RET_KEY_DIM // 2
    inv = ROPE_BASE ** (-jnp.arange(half, dtype=F32) / half)
    ang = jnp.arange(s).astype(F32)[:, None] * inv[None, :]
    cos_t = jnp.tile(jnp.cos(ang), (1, RET_QK_W // half))
    sin_t = jnp.tile(jnp.sin(ang), (1, RET_QK_W // half))

    q_aug, k_aug, v_t, rq, rk, rv, rg = _inproj(x2, row(mix_norm), w_main, w_vt, w_f, b_f, cos_t, sin_t, s, tm)
    fox = _fox(q_aug, k_aug, v_t, b, s, tq)

    lg = jnp.log1p(-jnp.exp2(-5.0 - jnp.arange(RET_HEADS, dtype=F32)))
    ro = _retention(lg, rq.reshape(b, s, RET_QK_W), rk.reshape(b, s, RET_QK_W),
                    rv.reshape(b, s, RET_V_W), rg.reshape(b, s, RET_V_W), chunk)

    w_r = jnp.pad(w_router, ((0, 0), (0, LANES - N_EXPERTS))).astype(BF16)
    b_r = jnp.concatenate([b_router, jnp.full((LANES - N_EXPERTS,), -1e30, F32)]).reshape(1, LANES)
    h1, hn, rf, ri, cnt = _merge(
        x2, fox.reshape(n, FOX_W), ro.reshape(n, RET_V_W), row(mix_norm), w_merge_gate.astype(BF16),
        row(b_merge_gate), w_branch.astype(BF16), w_out.astype(BF16), row(ffn_norm), w_r, b_r, tm)

    a = n * TOP_K
    nb = -(-(a + N_EXPERTS * (bm - 1)) // bm)
    pr = nb * bm
    e_idx = ri[:, :TOP_K]
    rank = ri[:, TOP_K:2 * TOP_K]
    counts = cnt[0, :N_EXPERTS].astype(jnp.int32)
    padded = (counts + bm - 1) // bm * bm
    pad_end = jnp.cumsum(padded)
    pad_start = pad_end - padded
    dest = pad_start[e_idx] + rank
    block_start = jnp.arange(nb, dtype=jnp.int32) * bm
    block_e = jnp.minimum(jnp.sum(block_start[:, None] >= pad_end[None, :], axis=1),
                          N_EXPERTS - 1).astype(jnp.int32)
    dest_t = dest.T

    ne, dd, f2 = w_gate_up.shape
    bgu = b_gate_up.reshape(ne, f2 // GLU_GROUP, LANES, 2).transpose(0, 1, 3, 2).reshape(ne, 1, f2)
    x_buf = _sc_dispatch(hn, dest_t, pr)
    n_used = (pad_end[-1:] // bm).astype(jnp.int32)
    y_buf = _experts(block_e, n_used, x_buf, w_gate_up, bgu, w_down, b_down.reshape(ne, 1, dd), bm)
    yg = _sc_gather(y_buf, dest_t.reshape(1, a)).reshape(d // 2 // SC_COLS, TOP_K, n, SC_COLS)

    out = _final(h1, yg, rf, p.reshape(n, -1), row(ple_norm), w_ple_gate.astype(BF16),
                 w_ple_proj.astype(BF16), row(final_norm), tm // 2)
    return out.reshape(b, s, d)


def kernel(x, p, mix_norm, w_in, b_forget, w_branch, w_merge_gate, b_merge_gate, w_out, ffn_norm, w_router,
           b_router, w_gate_up, b_gate_up, w_down, b_down, ple_norm, w_ple_gate, w_ple_proj, final_norm):
    depth = p.shape[0]
    assert depth == 1, "the final norm is fused into the (single) layer"
    return _layer(x, p[0], mix_norm[0], w_in[0], b_forget[0], w_branch[0], w_merge_gate[0], b_merge_gate[0],
                  w_out[0], ffn_norm[0], w_router[0], b_router[0], w_gate_up[0], b_gate_up[0], w_down[0],
                  b_down[0], ple_norm[0], w_ple_gate[0], w_ple_proj[0], final_norm,
                  tm=512, tq=512, chunk=256, bm=512)
```

```python
import functools

import numpy as np
import jax
import jax.numpy as jnp
from jax import lax
from jax.experimental import pallas as pl
from jax.experimental.pallas import tpu as pltpu
from jax.experimental.pallas import tpu_sc as plsc

FOX_HEADS = 8
FOX_HEAD_DIM = 64
RET_HEADS = 4
RET_KEY_DIM = 64
RET_VAL_DIM = 128
ROPE_BASE = 10000.0
N_EXPERTS = 32
TOP_K = 4
SWIGLU_LIMIT = 7.0
SWIGLU_ALPHA = 1.702
EPS = 1e-6

LANES = 128
FOX_W = FOX_HEADS * FOX_HEAD_DIM
RET_QK_W = RET_HEADS * RET_KEY_DIM
RET_V_W = RET_HEADS * RET_VAL_DIM
VMEM_LIMIT = 56 * 1024 * 1024
HIGH_HALF = -65536
SC_WINDOW = 128
SC_COLS = 256

F32 = jnp.float32
BF16 = jnp.bfloat16


def _rms(x, g):
    return x * lax.rsqrt(jnp.mean(x * x, axis=-1, keepdims=True) + EPS) * g


def _sigmoid(x):
    return 1.0 / (1.0 + jnp.exp(-x))


def _dot(a, b):
    return jnp.dot(a, b, preferred_element_type=F32)


def _dot_nt(a, b):
    return lax.dot_general(a, b, (((1,), (1,)), ((), ())), preferred_element_type=F32)


def _pack_pairs(v):
    bits = pltpu.bitcast(v.astype(BF16).astype(F32), jnp.int32)
    half = v.shape[1] // 2
    return lax.shift_right_logical(bits[:, :half], 16) | (bits[:, half:] & HIGH_HALF)


def _unpack_pairs(chunks):
    return jnp.concatenate([pltpu.bitcast(lax.shift_left(w, 16), F32) for w in chunks]
                           + [pltpu.bitcast(w & HIGH_HALF, F32) for w in chunks], axis=1)


def _store_chunks(ref, v, rows=slice(None)):
    for c in range(ref.shape[0]):
        ref[c, rows, :] = v[:, c * SC_COLS:(c + 1) * SC_COLS]


def _params(n_axes):
    return pltpu.CompilerParams(dimension_semantics=("arbitrary",) * n_axes,
                                vmem_limit_bytes=VMEM_LIMIT)


C_MID = FOX_HEADS
C_LO = 2 * FOX_HEADS
C_ONE = 3 * FOX_HEADS
N_AUG = 3


def _split3(t):
    hi = t.astype(BF16).astype(F32)
    r = t - hi
    mid = r.astype(BF16).astype(F32)
    lo = (r - mid).astype(BF16).astype(F32)
    return hi + pltpu.roll(mid, C_MID, 1) + pltpu.roll(lo, C_LO, 1)


def _inproj_kernel(x_ref, g_ref, w_ref, wvt_ref, wf_ref, bf_ref, cos_ref, sin_ref, pq_ref, pk_ref,
                   q_ref, k_ref, vt_ref, rq_ref, rk_ref, rv_ref, rg_ref, carry_scr, *, tiles_per_seq):
    i = pl.program_id(0)
    tm = x_ref.shape[0]

    @pl.when(i % tiles_per_seq == 0)
    def _():
        carry_scr[...] = jnp.zeros(carry_scr.shape, F32)

    xn = _rms(x_ref[...], g_ref[...]).astype(BF16)
    u = _dot(xn, w_ref[...])
    vt_ref[...] = _dot_nt(wvt_ref[...], xn).astype(BF16)

    lane = lax.broadcasted_iota(jnp.int32, (tm, LANES), 1)
    z = _dot(xn, wf_ref[...]) + bf_ref[...]
    lf = jnp.where(lane < FOX_HEADS, jnp.minimum(z, 0.0) - jnp.log1p(jnp.exp(-jnp.abs(z))), 0.0)
    r = lax.broadcasted_iota(jnp.int32, (tm, tm), 0)
    c = lax.broadcasted_iota(jnp.int32, (tm, tm), 1)
    tri = jnp.where(c <= r, 1.0, 0.0).astype(BF16)
    ps = _dot(tri, _split3(lf).astype(BF16))
    cum = ps + pltpu.roll(ps, LANES - C_MID, 1) + pltpu.roll(ps, LANES - C_LO, 1)
    cum = jnp.where(lane < FOX_HEADS, cum, 0.0) + carry_scr[...]
    carry_scr[...] = cum[tm - 1:tm, :]
    c3 = (_split3(cum) + jnp.where(lane == C_ONE, 1.0, 0.0)).astype(BF16)
    aug_q = _dot(c3, pq_ref[...])
    aug_k = _dot(c3, pk_ref[...])

    own = lane < FOX_HEAD_DIM
    fq = u[:, 0:FOX_W] * (FOX_HEAD_DIM ** -0.5)
    fk = u[:, FOX_W:2 * FOX_W]
    for src, aug, dst in ((fq, aug_q, q_ref), (fk, aug_k, k_ref)):
        for j in range(FOX_W // LANES):
            blk = src[:, j * LANES:(j + 1) * LANES]
            e0, e1 = 2 * j * LANES, (2 * j + 1) * LANES
            dst[:, e0:e0 + LANES] = jnp.where(own, blk, aug[:, e0:e0 + LANES]).astype(BF16)
            dst[:, e1:e1 + LANES] = jnp.where(own, pltpu.roll(blk, FOX_HEAD_DIM, 1),
                                               aug[:, e1:e1 + LANES]).astype(BF16)

    o = 2 * FOX_W
    rq = u[:, o:o + RET_QK_W]; o += RET_QK_W
    rk = u[:, o:o + RET_QK_W]; o += RET_QK_W
    rv_ref[...] = u[:, o:o + RET_V_W].astype(BF16); o += RET_V_W
    rg_ref[...] = u[:, o:o + RET_V_W]

    half = RET_KEY_DIM // 2
    lane_r = lax.broadcasted_iota(jnp.int32, rq.shape, 1)
    first = (lane_r % RET_KEY_DIM) < half
    cos = cos_ref[...]
    sin = sin_ref[...]

    def rot(t):
        partner = jnp.where(first, -pltpu.roll(t, RET_QK_W - half, 1), pltpu.roll(t, half, 1))
        return t * cos + partner * sin

    rq_ref[...] = rot(rq).astype(BF16)
    rk_ref[...] = (rot(rk) * (RET_KEY_DIM ** -0.5)).astype(BF16)


def _placement():
    pq = np.zeros((LANES, FOX_HEADS * LANES), np.float32)
    pk = np.zeros((LANES, FOX_HEADS * LANES), np.float32)
    for h in range(FOX_HEADS):
        base = h * LANES + FOX_HEAD_DIM
        for part, src in enumerate((h, C_MID + h, C_LO + h)):
            pq[C_ONE, base + part] = 1.0
            pq[src, base + N_AUG + part] = 1.0
            pk[src, base + part] = -1.0
            pk[C_ONE, base + N_AUG + part] = 1.0
    return jnp.asarray(pq, BF16), jnp.asarray(pk, BF16)


def _inproj(x2, g, w_main, w_vt, w_f, b_f, cos_t, sin_t, seq, tm):
    n, d = x2.shape
    wn = w_main.shape[1]
    spt = seq // tm
    aw = FOX_HEADS * LANES
    pq, pk = _placement()
    row = lambda i: (i, 0)
    const = lambda i: (0, 0)
    pos = lambda i: (i % spt, 0)
    out_shape = [
        jax.ShapeDtypeStruct((n, aw), BF16), jax.ShapeDtypeStruct((n, aw), BF16),
        jax.ShapeDtypeStruct((FOX_W, n), BF16),
        jax.ShapeDtypeStruct((n, RET_QK_W), BF16), jax.ShapeDtypeStruct((n, RET_QK_W), BF16),
        jax.ShapeDtypeStruct((n, RET_V_W), BF16), jax.ShapeDtypeStruct((n, RET_V_W), F32),
    ]
    out_specs = [
        pl.BlockSpec((tm, aw), row), pl.BlockSpec((tm, aw), row),
        pl.BlockSpec((FOX_W, tm), lambda i: (0, i)),
        pl.BlockSpec((tm, RET_QK_W), row), pl.BlockSpec((tm, RET_QK_W), row),
        pl.BlockSpec((tm, RET_V_W), row), pl.BlockSpec((tm, RET_V_W), row),
    ]
    return pl.pallas_call(
        functools.partial(_inproj_kernel, tiles_per_seq=spt),
        grid=(n // tm,),
        in_specs=[
            pl.BlockSpec((tm, d), row), pl.BlockSpec((1, d), const),
            pl.BlockSpec((d, wn), const), pl.BlockSpec((FOX_W, d), const),
            pl.BlockSpec((d, LANES), const), pl.BlockSpec((1, LANES), const),
            pl.BlockSpec((tm, RET_QK_W), pos), pl.BlockSpec((tm, RET_QK_W), pos),
            pl.BlockSpec((LANES, aw), const), pl.BlockSpec((LANES, aw), const),
        ],
        out_specs=out_specs,
        out_shape=out_shape,
        scratch_shapes=[pltpu.VMEM((1, LANES), F32)],
        compiler_params=_params(1),
        name="inproj",
    )(x2, g, w_main, w_vt, w_f, b_f, cos_t, sin_t, pq, pk)


def _fox_kernel(q_ref, k_ref, vt_ref, o_ref, *, tq):
    seq = q_ref.shape[0]
    key = lax.broadcasted_iota(jnp.int32, (tq, tq), 0)
    qry = lax.broadcasted_iota(jnp.int32, (tq, tq), 1)
    vrow = lax.broadcasted_iota(jnp.int32, (LANES, tq), 0)
    one = jnp.ones((), BF16)
    items = [(qi, j) for qi in range(seq // tq) for j in range(qi + 1)]

    def logits(qi, j):
        out = []
        for a in range(2):
            sl = slice(a * LANES, (a + 1) * LANES)
            s = _dot_nt(k_ref[j * tq:(j + 1) * tq, sl], q_ref[qi * tq:(qi + 1) * tq, sl])
            out.append(jnp.where(key <= qry, s, -jnp.inf) if j == qi else s)
        return out

    s_cur = logits(*items[0])
    carry = None
    for w, (qi, j) in enumerate(items):
        s_next = logits(*items[w + 1]) if w + 1 < len(items) else None
        if j == 0:
            carry = [(jnp.full((1, tq), -jnp.inf, F32), jnp.zeros((LANES, tq), F32)) for _ in range(2)]
        v = vt_ref[:, j * tq:(j + 1) * tq]
        for a in range(2):
            m_prev, acc = carry[a]
            m_new = jnp.maximum(m_prev, jnp.max(s_cur[a], axis=0, keepdims=True))
            alpha = jnp.exp(m_prev - m_new)
            p = jnp.exp(s_cur[a] - m_new).astype(BF16)
            va = jnp.where((vrow // FOX_HEAD_DIM) == a, v, one)
            carry[a] = (m_new, alpha * acc + _dot(va, p))
        if j == qi:
            acc0, acc1 = carry[0][1], carry[1][1]
            ot = jnp.where(vrow < FOX_HEAD_DIM, acc0 / acc0[FOX_HEAD_DIM:FOX_HEAD_DIM + 1, :],
                           acc1 / acc1[0:1, :])
            o_ref[qi * tq:(qi + 1) * tq, :] = ot.T.astype(o_ref.dtype)
        s_cur = s_next


def _fox(q_aug, k_aug, v_t, batch, seq, tq):
    n = q_aug.shape[0]
    pairs = FOX_HEADS // 2
    kern = functools.partial(_fox_kernel, tq=tq)
    return pl.pallas_call(
        kern,
        grid=(batch, pairs),
        in_specs=[
            pl.BlockSpec((seq, 2 * LANES), lambda i, h: (i, h)),
            pl.BlockSpec((seq, 2 * LANES), lambda i, h: (i, h)),
            pl.BlockSpec((LANES, seq), lambda i, h: (h, i)),
        ],
        out_specs=pl.BlockSpec((seq, LANES), lambda i, h: (i, h)),
        out_shape=jax.ShapeDtypeStruct((n, FOX_W), BF16),
        compiler_params=_params(2),
        name="fox_attention",
    )(q_aug, k_aug, v_t)


def _ret_kernel(lg_ref, q_ref, k_ref, v_ref, g_ref, o_ref, state_scr, *, chunk):
    h = pl.program_id(1)
    lg = lg_ref[h]
    s = q_ref.shape[1]
    lane = lax.broadcasted_iota(jnp.int32, (chunk, LANES), 1)
    mine = (lane // RET_KEY_DIM) == (h % 2)
    ri = lax.broadcasted_iota(jnp.int32, (chunk, chunk), 0)
    ci = lax.broadcasted_iota(jnp.int32, (chunk, chunk), 1)
    diff = (ri - ci).astype(F32)
    inner = jnp.where(diff >= 0, jnp.exp(jnp.maximum(diff, 0.0) * lg), 0.0)
    pos = lax.broadcasted_iota(jnp.int32, (chunk, 1), 0).astype(F32)
    q_decay = jnp.exp((pos + 1.0) * lg)
    k_decay = jnp.exp((chunk - 1.0 - pos) * lg)
    chunk_decay = jnp.exp(jnp.full((1, 1), chunk, F32) * lg)
    state_scr[...] = jnp.zeros(state_scr.shape, F32)
    for c in range(s // chunk):
        sl = pl.ds(c * chunk, chunk)
        q = jnp.where(mine, q_ref[0, sl, :], jnp.zeros((), BF16))
        k = jnp.where(mine, k_ref[0, sl, :], jnp.zeros((), BF16))
        v = v_ref[0, sl, :]
        scores = _dot_nt(q, k) * inner
        state = state_scr[...]
        o = _dot(scores.astype(BF16), v) + _dot(q, state.astype(BF16)) * q_decay
        kd = (k.astype(F32) * k_decay).astype(BF16)
        state_scr[...] = state * chunk_decay + lax.dot_general(
            kd, v, (((0,), (0,)), ((), ())), preferred_element_type=F32)
        o = o * lax.rsqrt(jnp.mean(o * o, axis=-1, keepdims=True) + EPS)
        g = g_ref[0, sl, :]
        o_ref[0, sl, :] = (o * (g * _sigmoid(g))).astype(o_ref.dtype)


def _retention(lg, rq, rk, rv, rg, chunk):
    b, s, _ = rq.shape
    kern = functools.partial(_ret_kernel, chunk=chunk)
    qk_spec = pl.BlockSpec((1, s, LANES), lambda i, h, lg_ref: (i, 0, h // 2))
    v_spec = pl.BlockSpec((1, s, LANES), lambda i, h, lg_ref: (i, 0, h))
    return pl.pallas_call(
        kern,
        grid_spec=pltpu.PrefetchScalarGridSpec(
            num_scalar_prefetch=1,
            grid=(b, RET_HEADS),
            in_specs=[qk_spec, qk_spec, v_spec, v_spec],
            out_specs=v_spec,
            scratch_shapes=[pltpu.VMEM((LANES, RET_VAL_DIM), F32)],
        ),
        out_shape=jax.ShapeDtypeStruct((b, s, RET_V_W), BF16),
        compiler_params=_params(2),
        name="retention",
    )(lg, rq, rk, rv, rg)


MERGE_SUB = 256


def _interleave(generators):
    live = []
    pending = list(generators)
    while pending or live:
        if pending:
            live.append(pending.pop(0))
        for g in list(live):
            try:
                next(g)
            except StopIteration:
                live.remove(g)


def _merge_kernel(x_ref, fox_ref, ro_ref, gmix_ref, wmg_ref, bmg_ref, wb_ref, wout_ref, gffn_ref,
                  wr_ref, br_ref, h1_ref, hn_ref, rf_ref, ri_ref, cnt_ref, base_scr):
    i = pl.program_id(0)
    tm, d = x_ref.shape
    sub = min(MERGE_SUB, tm)

    @pl.when(i == 0)
    def _():
        base_scr[...] = jnp.zeros(base_scr.shape, F32)

    lane = lax.broadcasted_iota(jnp.int32, (sub, LANES), 1)
    lane_f = lane.astype(F32)
    r = lax.broadcasted_iota(jnp.int32, (sub, sub), 0)
    c = lax.broadcasted_iota(jnp.int32, (sub, sub), 1)
    tri = jnp.where(c < r, 1.0, 0.0).astype(BF16)
    counts = [base_scr[...]]

    def sub_tile(t):
        rows = slice(t * sub, (t + 1) * sub)
        x = x_ref[rows, :]
        xn = _rms(x, gmix_ref[...]).astype(BF16)
        yield
        pre = _dot(xn, wmg_ref[...])
        yield
        gate = _sigmoid(pre + bmg_ref[...])
        yield
        pf = _dot(fox_ref[rows, :], wb_ref[0])
        pr = _dot(ro_ref[rows, :], wb_ref[1])
        yield
        merged = (gate[:, :d] * pf + gate[:, d:] * pr).astype(BF16)
        yield
        h1 = x + _dot(merged, wout_ref[...])
        yield
        h1_ref[rows, :] = h1
        hn = _rms(h1, gffn_ref[...]).astype(BF16)
        _store_chunks(hn_ref, _pack_pairs(hn), rows)
        yield
        logits = _dot(hn, wr_ref[...]) + br_ref[...]
        yield
        vals, idxs = [], []
        cur = logits
        for _ in range(TOP_K):
            m = jnp.max(cur, axis=-1, keepdims=True)
            idx = jnp.min(jnp.where(cur == m, lane_f, float(LANES)), axis=-1, keepdims=True)
            vals.append(m)
            idxs.append(idx)
            cur = jnp.where(lane_f == idx, -jnp.inf, cur)
        exps = [jnp.exp(v - vals[0]) for v in vals]
        denom = exps[0] + exps[1] + exps[2] + exps[3]
        onehot = jnp.zeros(logits.shape, F32)
        for idx in idxs:
            onehot = onehot + jnp.where(lane_f == idx, 1.0, 0.0)
        yield
        assert len(counts) == t + 1
        before = _dot(tri, onehot.astype(BF16)) + counts[t]
        counts.append(counts[t] + jnp.sum(onehot, axis=0, keepdims=True))
        rf = jnp.zeros(logits.shape, F32)
        ri = jnp.zeros(logits.shape, F32)
        for j in range(TOP_K):
            rank = jnp.sum(jnp.where(lane_f == idxs[j], before, 0.0), axis=-1, keepdims=True)
            rf = jnp.where(lane == j, exps[j] / denom, rf)
            ri = jnp.where(lane == j, idxs[j], ri)
            ri = jnp.where(lane == TOP_K + j, rank, ri)
        rf_ref[rows, :] = rf
        ri_ref[rows, :] = ri.astype(jnp.int32)

    _interleave(sub_tile(t) for t in range(tm // sub))
    base_scr[...] = counts[-1]
    cnt_ref[...] = counts[-1]


def _merge(x2, fox, ro, gmix, wmg, bmg, wb, wout, gffn, wr, br, tm, first_tile, n):
    d = x2.shape[1]
    row = lambda i: (i, 0)
    src = lambda i: (i + first_tile, 0)
    const = lambda i: (0, 0)
    return pl.pallas_call(
        _merge_kernel,
        grid=(n // tm,),
        in_specs=[
            pl.BlockSpec((tm, d), src), pl.BlockSpec((tm, FOX_W), src), pl.BlockSpec((tm, RET_V_W), src),
            pl.BlockSpec((1, d), const), pl.BlockSpec((d, 2 * d), const), pl.BlockSpec((1, 2 * d), const),
            pl.BlockSpec((2, FOX_W, d), lambda i: (0, 0, 0)), pl.BlockSpec((d, d), const),
            pl.BlockSpec((1, d), const), pl.BlockSpec((d, LANES), const), pl.BlockSpec((1, LANES), const),
        ],
        out_specs=[
            pl.BlockSpec((tm, d), row), pl.BlockSpec((d // 2 // SC_COLS, tm, SC_COLS), lambda i: (0, i, 0)),
            pl.BlockSpec((tm, LANES), row), pl.BlockSpec((tm, LANES), row),
            pl.BlockSpec((1, LANES), const),
        ],
        out_shape=[
            jax.ShapeDtypeStruct((n, d), F32), jax.ShapeDtypeStruct((d // 2 // SC_COLS, n, SC_COLS), jnp.int32),
            jax.ShapeDtypeStruct((n, LANES), F32), jax.ShapeDtypeStruct((n, LANES), jnp.int32),
            jax.ShapeDtypeStruct((1, LANES), F32),
        ],
        scratch_shapes=[pltpu.VMEM((1, LANES), F32)],
        compiler_params=_params(1),
        name="merge_router",
    )(x2, fox, ro, gmix, wmg, bmg, wb, wout, gffn, wr, br)


def _sc_mesh():
    return plsc.VectorSubcoreMesh(core_axis_name="core", subcore_axis_name="subcore")


def _sc_dispatch(rows, dest_t, n_out):
    chunks, n, w = rows.shape

    @functools.partial(pl.kernel, out_type=jax.ShapeDtypeStruct((chunks, n_out, w), rows.dtype),
                       mesh=_sc_mesh(), scratch_types=[], name="moe_dispatch")
    def run(x_hbm, i_hbm, o_hbm):
        for c in range(chunks):
            def body(x_vmem, i_vmem, c=c):
                for j in range(TOP_K):
                    pltpu.sync_copy(x_vmem, o_hbm.at[c].at[i_vmem.at[j]])

            pltpu.emit_pipeline(
                body,
                grid=(n // SC_WINDOW,),
                in_specs=[pl.BlockSpec((SC_WINDOW, w), lambda i: (i, 0)),
                          pl.BlockSpec((TOP_K, SC_WINDOW), lambda i: (0, i))],
                out_specs=[],
                core_axis_name=("core", "subcore"),
                dimension_semantics=(pltpu.PARALLEL,),
            )(x_hbm.at[c], i_hbm)

    return run(rows, dest_t)


def _sc_gather(table, idx):
    chunks, _, w = table.shape
    m = idx.shape[1]

    @functools.partial(pl.kernel, out_type=jax.ShapeDtypeStruct((chunks, m, w), table.dtype),
                       mesh=_sc_mesh(), scratch_types=[], name="moe_combine_gather")
    def run(t_hbm, i_hbm, o_hbm):
        for c in range(chunks):
            def body(i_vmem, o_vmem, c=c):
                pltpu.sync_copy(t_hbm.at[c].at[i_vmem.at[0]], o_vmem)

            pltpu.emit_pipeline(
                body,
                grid=(m // SC_WINDOW,),
                in_specs=[pl.BlockSpec((1, SC_WINDOW), lambda i: (0, i))],
                out_specs=[pl.BlockSpec((SC_WINDOW, w), lambda i: (i, 0))],
                core_axis_name=("core", "subcore"),
                dimension_semantics=(pltpu.PARALLEL,),
            )(i_hbm, o_hbm.at[c])

    return run(table, idx)


GLU_GROUP = 2 * LANES
EXPERT_SUB = 256


def _expert_kernel(be_ref, used_ref, x_ref, wgu_ref, bgu_ref, wd_ref, bd_ref, perm_ref, y_ref, wgu_scr, wd_scr):
    i = pl.program_id(0)
    f2 = wgu_ref.shape[2]
    live = i < used_ref[0]

    @pl.when(jnp.logical_and(live, jnp.logical_or(i == 0, be_ref[i] != be_ref[jnp.maximum(i - 1, 0)])))
    def _():
        for b in range(f2 // GLU_GROUP):
            cols = slice(b * GLU_GROUP, (b + 1) * GLU_GROUP)
            wgu_scr[:, cols] = _dot(wgu_ref[0, :, cols].astype(BF16), perm_ref[...]).astype(BF16)
        wd_scr[...] = wd_ref[0].astype(BF16)

    def sub_block(t, sub):
        rows = slice(t * sub, (t + 1) * sub)
        x = _unpack_pairs([x_ref[c, rows, :] for c in range(x_ref.shape[0])]).astype(BF16)
        yield
        gu = _dot(x, wgu_scr[...])
        yield
        gu = gu + bgu_ref[0]
        acts = []
        for b in range(f2 // GLU_GROUP):
            glu = jnp.minimum(gu[:, b * GLU_GROUP:b * GLU_GROUP + LANES], SWIGLU_LIMIT)
            lin = jnp.clip(gu[:, b * GLU_GROUP + LANES:(b + 1) * GLU_GROUP], -SWIGLU_LIMIT, SWIGLU_LIMIT)
            acts.append((glu * _sigmoid(SWIGLU_ALPHA * glu) * (lin + 1.0)).astype(BF16))
        act = jnp.concatenate(acts, axis=1)
        yield
        y = _dot(act, wd_scr[...])
        yield
        _store_chunks(y_ref, _pack_pairs(y + bd_ref[0]), rows)

    @pl.when(live)
    def _():
        bm = x_ref.shape[1]
        sub = min(EXPERT_SUB, bm)
        _interleave(sub_block(t, sub) for t in range(bm // sub))


def _experts(block_e, n_used, x_buf, wgu, bgu, wd, bd, bm):
    p = x_buf.shape[1]
    f, d = wd.shape[1:]
    perm = np.zeros((GLU_GROUP, GLU_GROUP), np.float32)
    for c in range(LANES):
        perm[2 * c, c] = 1.0
        perm[2 * c + 1, LANES + c] = 1.0
    return pl.pallas_call(
        _expert_kernel,
        grid_spec=pltpu.PrefetchScalarGridSpec(
            num_scalar_prefetch=2,
            grid=(p // bm,),
            in_specs=[
                pl.BlockSpec((d // 2 // SC_COLS, bm, SC_COLS), lambda i, be, nu: (0, i, 0)),
                pl.BlockSpec((1, d, 2 * f), lambda i, be, nu: (be[i], 0, 0)),
                pl.BlockSpec((1, 1, 2 * f), lambda i, be, nu: (be[i], 0, 0)),
                pl.BlockSpec((1, f, d), lambda i, be, nu: (be[i], 0, 0)),
                pl.BlockSpec((1, 1, d), lambda i, be, nu: (be[i], 0, 0)),
                pl.BlockSpec((GLU_GROUP, GLU_GROUP), lambda i, be, nu: (0, 0)),
            ],
            out_specs=pl.BlockSpec((d // 2 // SC_COLS, bm, SC_COLS), lambda i, be, nu: (0, i, 0)),
            scratch_shapes=[pltpu.VMEM((d, 2 * f), BF16), pltpu.VMEM((f, d), BF16)],
        ),
        out_shape=jax.ShapeDtypeStruct((d // 2 // SC_COLS, p, SC_COLS), jnp.int32),
        compiler_params=_params(1),
        name="expert_ffn",
    )(block_e, n_used, x_buf, wgu, bgu, wd, bd, jnp.asarray(perm, BF16))


def _final_kernel(h1_ref, yg_ref, rf_ref, p_ref, gple_ref, wpg_ref, wpp_ref, gfin_ref, *rest):
    o_ref = rest[-1]
    rf = rf_ref[...]
    h2 = h1_ref[...]
    for j in range(TOP_K):
        h2 = h2 + _unpack_pairs([yg_ref[c, j] for c in range(yg_ref.shape[0])]) * rf[:, j:j + 1]
    gate = _sigmoid(_dot(_rms(h2, gple_ref[...]).astype(BF16), wpg_ref[...]))
    h3 = h2 + gate * _dot(p_ref[...].astype(BF16), wpp_ref[...])
    o_ref[...] = _rms(h3, gfin_ref[...])


def _final(h1, yg, rf, p2, gple, wpg, wpp, gfin, tm, first_tile, n_total, prev_out):
    n, d = h1.shape
    pd = p2.shape[1]
    row = lambda i: (i, 0)
    dst = lambda i: (i + first_tile, 0)
    const = lambda i: (0, 0)
    in_specs = [
        pl.BlockSpec((tm, d), row), pl.BlockSpec((d // 2 // SC_COLS, TOP_K, tm, SC_COLS), lambda i: (0, 0, i, 0)),
        pl.BlockSpec((tm, LANES), row), pl.BlockSpec((tm, pd), dst),
        pl.BlockSpec((1, d), const), pl.BlockSpec((d, d), const), pl.BlockSpec((pd, d), const),
        pl.BlockSpec((1, d), const),
    ]
    args = [h1, yg, rf, p2, gple, wpg, wpp, gfin]
    aliases = {}
    if prev_out is not None:
        in_specs.append(pl.BlockSpec(memory_space=pl.ANY))
        aliases = {len(args): 0}
        args.append(prev_out)
    return pl.pallas_call(
        _final_kernel,
        grid=(n // tm,),
        in_specs=in_specs,
        out_specs=pl.BlockSpec((tm, d), dst),
        out_shape=jax.ShapeDtypeStruct((n_total, d), F32),
        input_output_aliases=aliases,
        compiler_params=_params(1),
        name="combine_ple_norm",
    )(*args)


def _layer(h, p, mix_norm, w_in, b_forget, w_branch, w_merge_gate, b_merge_gate, w_out, ffn_norm,
           w_router, b_router, w_gate_up, b_gate_up, w_down, b_down, ple_norm, w_ple_gate, w_ple_proj,
           final_norm, *, tm, tq, chunk, bm, groups):
    b, s, d = h.shape
    n = b * s
    x2 = h.reshape(n, d)
    row = lambda t: t.reshape(1, -1)

    c0 = 3 * FOX_W
    w_main = jnp.concatenate([w_in[:, :2 * FOX_W], w_in[:, c0 + FOX_HEADS:]], axis=1).astype(BF16)
    w_vt = w_in[:, 2 * FOX_W:c0].T.astype(BF16)
    w_f = jnp.pad(w_in[:, c0:c0 + FOX_HEADS], ((0, 0), (0, LANES - FOX_HEADS))).astype(BF16)
    b_f = jnp.pad(b_forget, (0, LANES - FOX_HEADS)).reshape(1, LANES)
    half = RET_KEY_DIM // 2
    inv = ROPE_BASE ** (-jnp.arange(half, dtype=F32) / half)
    ang = jnp.arange(s).astype(F32)[:, None] * inv[None, :]
    cos_t = jnp.tile(jnp.cos(ang), (1, RET_QK_W // half))
    sin_t = jnp.tile(jnp.sin(ang), (1, RET_QK_W // half))

    q_aug, k_aug, v_t, rq, rk, rv, rg = _inproj(x2, row(mix_norm), w_main, w_vt, w_f, b_f, cos_t, sin_t, s, tm)
    fox = _fox(q_aug, k_aug, v_t, b, s, tq)

    lg = jnp.log1p(-jnp.exp2(-5.0 - jnp.arange(RET_HEADS, dtype=F32)))
    ro = _retention(lg, rq.reshape(b, s, RET_QK_W), rk.reshape(b, s, RET_QK_W),
                    rv.reshape(b, s, RET_V_W), rg.reshape(b, s, RET_V_W), chunk)

    w_r = jnp.pad(w_router, ((0, 0), (0, LANES - N_EXPERTS))).astype(BF16)
    b_r = jnp.concatenate([b_router, jnp.full((LANES - N_EXPERTS,), -1e30, F32)]).reshape(1, LANES)
    ne, dd, f2 = w_gate_up.shape
    bgu = b_gate_up.reshape(ne, f2 // GLU_GROUP, LANES, 2).transpose(0, 1, 3, 2).reshape(ne, 1, f2)
    merge_w = (row(mix_norm), w_merge_gate.astype(BF16), row(b_merge_gate), w_branch.astype(BF16),
               w_out.astype(BF16), row(ffn_norm), w_r, b_r)
    final_w = (row(ple_norm), w_ple_gate.astype(BF16), w_ple_proj.astype(BF16), row(final_norm))
    p2 = p.reshape(n, -1)
    tf = tm // 2

    ng = n // groups
    a = ng * TOP_K
    nb = -(-(a + N_EXPERTS * (bm - 1)) // bm)
    block_start = jnp.arange(nb, dtype=jnp.int32) * bm
    out = None
    for g in range(groups):
        h1, hn, rf, ri, cnt = _merge(x2, fox, ro.reshape(n, RET_V_W), *merge_w, tm, g * (ng // tm), ng)

        e_idx = ri[:, :TOP_K]
        rank = ri[:, TOP_K:2 * TOP_K]
        counts = cnt[0, :N_EXPERTS].astype(jnp.int32)
        padded = (counts + bm - 1) // bm * bm
        pad_end = jnp.cumsum(padded)
        pad_start = pad_end - padded
        dest_t = (pad_start[e_idx] + rank).T
        block_e = jnp.minimum(jnp.sum(block_start[:, None] >= pad_end[None, :], axis=1),
                              N_EXPERTS - 1).astype(jnp.int32)
        n_used = (pad_end[-1:] // bm).astype(jnp.int32)

        x_buf = _sc_dispatch(hn, dest_t, nb * bm)
        y_buf = _experts(block_e, n_used, x_buf, w_gate_up, bgu, w_down, b_down.reshape(ne, 1, dd), bm)
        yg = _sc_gather(y_buf, dest_t.reshape(1, a)).reshape(d // 2 // SC_COLS, TOP_K, ng, SC_COLS)
        out = _final(h1, yg, rf, p2, *final_w, tf, g * (ng // tf), n, out)
    return out.reshape(b, s, d)


def kernel(x, p, mix_norm, w_in, b_forget, w_branch, w_merge_gate, b_merge_gate, w_out, ffn_norm, w_router,
           b_router, w_gate_up, b_gate_up, w_down, b_down, ple_norm, w_ple_gate, w_ple_proj, final_norm):
    depth = p.shape[0]
    assert depth == 1, "the final norm is fused into the (single) layer"
    return _layer(x, p[0], mix_norm[0], w_in[0], b_forget[0], w_branch[0], w_merge_gate[0], b_merge_gate[0],
                  w_out[0], ffn_norm[0], w_router[0], b_router[0], w_gate_up[0], b_gate_up[0], w_down[0],
                  b_down[0], ple_norm[0], w_ple_gate[0], w_ple_proj[0], final_norm,
                  tm=512, tq=512, chunk=256, bm=512, groups=2)
```

```python
import functools

import numpy as np
import jax
import jax.numpy as jnp
from jax import lax
from jax.experimental import pallas as pl
from jax.experimental.pallas import tpu as pltpu
from jax.experimental.pallas import tpu_sc as plsc

FOX_HEADS = 8
FOX_HEAD_DIM = 64
RET_HEADS = 4
RET_KEY_DIM = 64
RET_VAL_DIM = 128
ROPE_BASE = 10000.0
N_EXPERTS = 32
TOP_K = 4
SWIGLU_LIMIT = 7.0
SWIGLU_ALPHA = 1.702
EPS = 1e-6

LANES = 128
FOX_W = FOX_HEADS * FOX_HEAD_DIM
RET_QK_W = RET_HEADS * RET_KEY_DIM
RET_V_W = RET_HEADS * RET_VAL_DIM
VMEM_LIMIT = 56 * 1024 * 1024
HIGH_HALF = -65536
SC_WINDOW = 128
SC_COLS = 256

F32 = jnp.float32
BF16 = jnp.bfloat16


def _rms(x, g):
    return x * lax.rsqrt(jnp.mean(x * x, axis=-1, keepdims=True) + EPS) * g


def _sigmoid(x):
    return 1.0 / (1.0 + jnp.exp(-x))


def _dot(a, b):
    return jnp.dot(a, b, preferred_element_type=F32)


def _dot_nt(a, b):
    return lax.dot_general(a, b, (((1,), (1,)), ((), ())), preferred_element_type=F32)


def _pack_pairs(v):
    bits = pltpu.bitcast(v.astype(BF16).astype(F32), jnp.int32)
    half = v.shape[1] // 2
    return lax.shift_right_logical(bits[:, :half], 16) | (bits[:, half:] & HIGH_HALF)


def _unpack_pairs(chunks):
    return jnp.concatenate([pltpu.bitcast(lax.shift_left(w, 16), F32) for w in chunks]
                           + [pltpu.bitcast(w & HIGH_HALF, F32) for w in chunks], axis=1)


def _store_chunks(ref, v, rows=slice(None)):
    for c in range(ref.shape[0]):
        ref[c, rows, :] = v[:, c * SC_COLS:(c + 1) * SC_COLS]


def _params(n_axes):
    return pltpu.CompilerParams(dimension_semantics=("arbitrary",) * n_axes,
                                vmem_limit_bytes=VMEM_LIMIT)


C_MID = FOX_HEADS
C_LO = 2 * FOX_HEADS
C_ONE = 3 * FOX_HEADS
N_AUG = 3


def _split3(t):
    hi = t.astype(BF16).astype(F32)
    r = t - hi
    mid = r.astype(BF16).astype(F32)
    lo = (r - mid).astype(BF16).astype(F32)
    return hi + pltpu.roll(mid, C_MID, 1) + pltpu.roll(lo, C_LO, 1)


def _inproj_kernel(x_ref, g_ref, w_ref, wvt_ref, wf_ref, bf_ref, cos_ref, sin_ref, pq_ref, pk_ref,
                   q_ref, k_ref, vt_ref, rq_ref, rk_ref, rv_ref, rg_ref, carry_scr, *, tiles_per_seq):
    i = pl.program_id(0)
    tm = x_ref.shape[0]

    @pl.when(i % tiles_per_seq == 0)
    def _():
        carry_scr[...] = jnp.zeros(carry_scr.shape, F32)

    xn = _rms(x_ref[...], g_ref[...]).astype(BF16)
    u = _dot(xn, w_ref[...])
    vt_ref[...] = _dot_nt(wvt_ref[...], xn).astype(BF16)

    lane = lax.broadcasted_iota(jnp.int32, (tm, LANES), 1)
    z = _dot(xn, wf_ref[...]) + bf_ref[...]
    lf = jnp.where(lane < FOX_HEADS, jnp.minimum(z, 0.0) - jnp.log1p(jnp.exp(-jnp.abs(z))), 0.0)
    r = lax.broadcasted_iota(jnp.int32, (tm, tm), 0)
    c = lax.broadcasted_iota(jnp.int32, (tm, tm), 1)
    tri = jnp.where(c <= r, 1.0, 0.0).astype(BF16)
    ps = _dot(tri, _split3(lf).astype(BF16))
    cum = ps + pltpu.roll(ps, LANES - C_MID, 1) + pltpu.roll(ps, LANES - C_LO, 1)
    cum = jnp.where(lane < FOX_HEADS, cum, 0.0) + carry_scr[...]
    carry_scr[...] = cum[tm - 1:tm, :]
    c3 = (_split3(cum) + jnp.where(lane == C_ONE, 1.0, 0.0)).astype(BF16)
    aug_q = _dot(c3, pq_ref[...])
    aug_k = _dot(c3, pk_ref[...])

    own = lane < FOX_HEAD_DIM
    fq = u[:, 0:FOX_W] * (FOX_HEAD_DIM ** -0.5)
    fk = u[:, FOX_W:2 * FOX_W]
    for src, aug, dst in ((fq, aug_q, q_ref), (fk, aug_k, k_ref)):
        for j in range(FOX_W // LANES):
            blk = src[:, j * LANES:(j + 1) * LANES]
            e0, e1 = 2 * j * LANES, (2 * j + 1) * LANES
            dst[:, e0:e0 + LANES] = jnp.where(own, blk, aug[:, e0:e0 + LANES]).astype(BF16)
            dst[:, e1:e1 + LANES] = jnp.where(own, pltpu.roll(blk, FOX_HEAD_DIM, 1),
                                               aug[:, e1:e1 + LANES]).astype(BF16)

    o = 2 * FOX_W
    rq = u[:, o:o + RET_QK_W]; o += RET_QK_W
    rk = u[:, o:o + RET_QK_W]; o += RET_QK_W
    rv_ref[...] = u[:, o:o + RET_V_W].astype(BF16); o += RET_V_W
    rg_ref[...] = u[:, o:o + RET_V_W]

    half = RET_KEY_DIM // 2
    lane_r = lax.broadcasted_iota(jnp.int32, rq.shape, 1)
    first = (lane_r % RET_KEY_DIM) < half
    cos = cos_ref[...]
    sin = sin_ref[...]

    def rot(t):
        partner = jnp.where(first, -pltpu.roll(t, RET_QK_W - half, 1), pltpu.roll(t, half, 1))
        return t * cos + partner * sin

    rq_ref[...] = rot(rq).astype(BF16)
    rk_ref[...] = (rot(rk) * (RET_KEY_DIM ** -0.5)).astype(BF16)


def _placement():
    pq = np.zeros((LANES, FOX_HEADS * LANES), np.float32)
    pk = np.zeros((LANES, FOX_HEADS * LANES), np.float32)
    for h in range(FOX_HEADS):
        base = h * LANES + FOX_HEAD_DIM
        for part, src in enumerate((h, C_MID + h, C_LO + h)):
            pq[C_ONE, base + part] = 1.0
            pq[src, base + N_AUG + part] = 1.0
            pk[src, base + part] = -1.0
            pk[C_ONE, base + N_AUG + part] = 1.0
    return jnp.asarray(pq, BF16), jnp.asarray(pk, BF16)


def _inproj(x2, g, w_main, w_vt, w_f, b_f, cos_t, sin_t, seq, tm):
    n, d = x2.shape
    wn = w_main.shape[1]
    spt = seq // tm
    aw = FOX_HEADS * LANES
    pq, pk = _placement()
    row = lambda i: (i, 0)
    const = lambda i: (0, 0)
    pos = lambda i: (i % spt, 0)
    out_shape = [
        jax.ShapeDtypeStruct((n, aw), BF16), jax.ShapeDtypeStruct((n, aw), BF16),
        jax.ShapeDtypeStruct((FOX_W, n), BF16),
        jax.ShapeDtypeStruct((n, RET_QK_W), BF16), jax.ShapeDtypeStruct((n, RET_QK_W), BF16),
        jax.ShapeDtypeStruct((n, RET_V_W), BF16), jax.ShapeDtypeStruct((n, RET_V_W), F32),
    ]
    out_specs = [
        pl.BlockSpec((tm, aw), row), pl.BlockSpec((tm, aw), row),
        pl.BlockSpec((FOX_W, tm), lambda i: (0, i)),
        pl.BlockSpec((tm, RET_QK_W), row), pl.BlockSpec((tm, RET_QK_W), row),
        pl.BlockSpec((tm, RET_V_W), row), pl.BlockSpec((tm, RET_V_W), row),
    ]
    return pl.pallas_call(
        functools.partial(_inproj_kernel, tiles_per_seq=spt),
        grid=(n // tm,),
        in_specs=[
            pl.BlockSpec((tm, d), row), pl.BlockSpec((1, d), const),
            pl.BlockSpec((d, wn), const), pl.BlockSpec((FOX_W, d), const),
            pl.BlockSpec((d, LANES), const), pl.BlockSpec((1, LANES), const),
            pl.BlockSpec((tm, RET_QK_W), pos), pl.BlockSpec((tm, RET_QK_W), pos),
            pl.BlockSpec((LANES, aw), const), pl.BlockSpec((LANES, aw), const),
        ],
        out_specs=out_specs,
        out_shape=out_shape,
        scratch_shapes=[pltpu.VMEM((1, LANES), F32)],
        compiler_params=_params(1),
        name="inproj",
    )(x2, g, w_main, w_vt, w_f, b_f, cos_t, sin_t, pq, pk)


def _fox_kernel(q_ref, k_ref, vt_ref, o_ref, *, tq):
    seq = q_ref.shape[0]
    key = lax.broadcasted_iota(jnp.int32, (tq, tq), 0)
    qry = lax.broadcasted_iota(jnp.int32, (tq, tq), 1)
    vrow = lax.broadcasted_iota(jnp.int32, (LANES, tq), 0)
    one = jnp.ones((), BF16)
    items = [(qi, j) for qi in range(seq // tq) for j in range(qi + 1)]

    def logits(qi, j):
        out = []
        for a in range(2):
            sl = slice(a * LANES, (a + 1) * LANES)
            s = _dot_nt(k_ref[j * tq:(j + 1) * tq, sl], q_ref[qi * tq:(qi + 1) * tq, sl])
            out.append(jnp.where(key <= qry, s, -jnp.inf) if j == qi else s)
        return out

    s_cur = logits(*items[0])
    carry = None
    for w, (qi, j) in enumerate(items):
        s_next = logits(*items[w + 1]) if w + 1 < len(items) else None
        if j == 0:
            carry = [(jnp.full((1, tq), -jnp.inf, F32), jnp.zeros((LANES, tq), F32)) for _ in range(2)]
        v = vt_ref[:, j * tq:(j + 1) * tq]
        for a in range(2):
            m_prev, acc = carry[a]
            m_new = jnp.maximum(m_prev, jnp.max(s_cur[a], axis=0, keepdims=True))
            alpha = jnp.exp(m_prev - m_new)
            p = jnp.exp(s_cur[a] - m_new).astype(BF16)
            va = jnp.where((vrow // FOX_HEAD_DIM) == a, v, one)
            carry[a] = (m_new, alpha * acc + _dot(va, p))
        if j == qi:
            acc0, acc1 = carry[0][1], carry[1][1]
            ot = jnp.where(vrow < FOX_HEAD_DIM, acc0 / acc0[FOX_HEAD_DIM:FOX_HEAD_DIM + 1, :],
                           acc1 / acc1[0:1, :])
            o_ref[qi * tq:(qi + 1) * tq, :] = ot.T.astype(o_ref.dtype)
        s_cur = s_next


def _fox(q_aug, k_aug, v_t, batch, seq, tq):
    n = q_aug.shape[0]
    pairs = FOX_HEADS // 2
    kern = functools.partial(_fox_kernel, tq=tq)
    return pl.pallas_call(
        kern,
        grid=(batch, pairs),
        in_specs=[
            pl.BlockSpec((seq, 2 * LANES), lambda i, h: (i, h)),
            pl.BlockSpec((seq, 2 * LANES), lambda i, h: (i, h)),
            pl.BlockSpec((LANES, seq), lambda i, h: (h, i)),
        ],
        out_specs=pl.BlockSpec((seq, LANES), lambda i, h: (i, h)),
        out_shape=jax.ShapeDtypeStruct((n, FOX_W), BF16),
        compiler_params=_params(2),
        name="fox_attention",
    )(q_aug, k_aug, v_t)


def _ret_kernel(lg_ref, q_ref, k_ref, v_ref, g_ref, o_ref, state_scr, *, chunk):
    h = pl.program_id(1)
    lg = lg_ref[h]
    s = q_ref.shape[1]
    lane = lax.broadcasted_iota(jnp.int32, (chunk, LANES), 1)
    mine = (lane // RET_KEY_DIM) == (h % 2)
    ri = lax.broadcasted_iota(jnp.int32, (chunk, chunk), 0)
    ci = lax.broadcasted_iota(jnp.int32, (chunk, chunk), 1)
    diff = (ri - ci).astype(F32)
    inner = jnp.where(diff >= 0, jnp.exp(jnp.maximum(diff, 0.0) * lg), 0.0)
    pos = lax.broadcasted_iota(jnp.int32, (chunk, 1), 0).astype(F32)
    q_decay = jnp.exp((pos + 1.0) * lg)
    k_decay = jnp.exp((chunk - 1.0 - pos) * lg)
    chunk_decay = jnp.exp(jnp.full((1, 1), chunk, F32) * lg)
    state_scr[...] = jnp.zeros(state_scr.shape, F32)
    for c in range(s // chunk):
        sl = pl.ds(c * chunk, chunk)
        q = jnp.where(mine, q_ref[0, sl, :], jnp.zeros((), BF16))
        k = jnp.where(mine, k_ref[0, sl, :], jnp.zeros((), BF16))
        v = v_ref[0, sl, :]
        scores = _dot_nt(q, k) * inner
        state = state_scr[...]
        o = _dot(scores.astype(BF16), v) + _dot(q, state.astype(BF16)) * q_decay
        kd = (k.astype(F32) * k_decay).astype(BF16)
        state_scr[...] = state * chunk_decay + lax.dot_general(
            kd, v, (((0,), (0,)), ((), ())), preferred_element_type=F32)
        o = o * lax.rsqrt(jnp.mean(o * o, axis=-1, keepdims=True) + EPS)
        g = g_ref[0, sl, :]
        o_ref[0, sl, :] = (o * (g * _sigmoid(g))).astype(o_ref.dtype)


def _retention(lg, rq, rk, rv, rg, chunk):
    b, s, _ = rq.shape
    kern = functools.partial(_ret_kernel, chunk=chunk)
    qk_spec = pl.BlockSpec((1, s, LANES), lambda i, h, lg_ref: (i, 0, h // 2))
    v_spec = pl.BlockSpec((1, s, LANES), lambda i, h, lg_ref: (i, 0, h))
    return pl.pallas_call(
        kern,
        grid_spec=pltpu.PrefetchScalarGridSpec(
            num_scalar_prefetch=1,
            grid=(b, RET_HEADS),
            in_specs=[qk_spec, qk_spec, v_spec, v_spec],
            out_specs=v_spec,
            scratch_shapes=[pltpu.VMEM((LANES, RET_VAL_DIM), F32)],
        ),
        out_shape=jax.ShapeDtypeStruct((b, s, RET_V_W), BF16),
        compiler_params=_params(2),
        name="retention",
    )(lg, rq, rk, rv, rg)


MERGE_SUB = 256


def _interleave(generators):
    live = []
    pending = list(generators)
    while pending or live:
        if pending:
            live.append(pending.pop(0))
        for g in list(live):
            try:
                next(g)
            except StopIteration:
                live.remove(g)


def _merge_kernel(x_ref, fox_ref, ro_ref, gmix_ref, wmg_ref, bmg_ref, wb_ref, wout_ref, gffn_ref,
                  wr_ref, br_ref, h1_ref, hn_ref, rf_ref, ri_ref, cnt_ref, base_scr):
    i = pl.program_id(0)
    tm, d = x_ref.shape
    sub = min(MERGE_SUB, tm)

    @pl.when(i == 0)
    def _():
        base_scr[...] = jnp.zeros(base_scr.shape, F32)

    lane = lax.broadcasted_iota(jnp.int32, (sub, LANES), 1)
    lane_f = lane.astype(F32)
    r = lax.broadcasted_iota(jnp.int32, (sub, sub), 0)
    c = lax.broadcasted_iota(jnp.int32, (sub, sub), 1)
    tri = jnp.where(c < r, 1.0, 0.0).astype(BF16)
    counts = [base_scr[...]]

    def sub_tile(t):
        rows = slice(t * sub, (t + 1) * sub)
        x = x_ref[rows, :]
        xn = _rms(x, gmix_ref[...]).astype(BF16)
        yield
        pre = _dot(xn, wmg_ref[...])
        yield
        gate = _sigmoid(pre + bmg_ref[...])
        yield
        pf = _dot(fox_ref[rows, :], wb_ref[0])
        pr = _dot(ro_ref[rows, :], wb_ref[1])
        yield
        merged = (gate[:, :d] * pf + gate[:, d:] * pr).astype(BF16)
        yield
        h1 = x + _dot(merged, wout_ref[...])
        yield
        h1_ref[rows, :] = h1
        hn = _rms(h1, gffn_ref[...]).astype(BF16)
        _store_chunks(hn_ref, _pack_pairs(hn), rows)
        yield
        logits = _dot(hn, wr_ref[...]) + br_ref[...]
        yield
        vals, idxs = [], []
        cur = logits
        for _ in range(TOP_K):
            m = jnp.max(cur, axis=-1, keepdims=True)
            idx = jnp.min(jnp.where(cur == m, lane_f, float(LANES)), axis=-1, keepdims=True)
            vals.append(m)
            idxs.append(idx)
            cur = jnp.where(lane_f == idx, -jnp.inf, cur)
        exps = [jnp.exp(v - vals[0]) for v in vals]
        denom = exps[0] + exps[1] + exps[2] + exps[3]
        onehot = jnp.zeros(logits.shape, F32)
        for idx in idxs:
            onehot = onehot + jnp.where(lane_f == idx, 1.0, 0.0)
        yield
        assert len(counts) == t + 1
        before = _dot(tri, onehot.astype(BF16)) + counts[t]
        counts.append(counts[t] + jnp.sum(onehot, axis=0, keepdims=True))
        rf = jnp.zeros(logits.shape, F32)
        ri = jnp.zeros(logits.shape, F32)
        for j in range(TOP_K):
            rank = jnp.sum(jnp.where(lane_f == idxs[j], before, 0.0), axis=-1, keepdims=True)
            rf = jnp.where(lane == j, exps[j] / denom, rf)
            ri = jnp.where(lane == j, idxs[j], ri)
            ri = jnp.where(lane == TOP_K + j, rank, ri)
        rf_ref[rows, :] = rf
        ri_ref[rows, :] = ri.astype(jnp.int32)

    _interleave(sub_tile(t) for t in range(tm // sub))
    base_scr[...] = counts[-1]
    cnt_ref[...] = counts[-1]


def _merge(x2, fox, ro, gmix, wmg, bmg, wb, wout, gffn, wr, br, tm, first_tile, n):
    d = x2.shape[1]
    row = lambda i: (i, 0)
    src = lambda i: (i + first_tile, 0)
    const = lambda i: (0, 0)
    return pl.pallas_call(
        _merge_kernel,
        grid=(n // tm,),
        in_specs=[
            pl.BlockSpec((tm, d), src), pl.BlockSpec((tm, FOX_W), src), pl.BlockSpec((tm, RET_V_W), src),
            pl.BlockSpec((1, d), const), pl.BlockSpec((d, 2 * d), const), pl.BlockSpec((1, 2 * d), const),
            pl.BlockSpec((2, FOX_W, d), lambda i: (0, 0, 0)), pl.BlockSpec((d, d), const),
            pl.BlockSpec((1, d), const), pl.BlockSpec((d, LANES), const), pl.BlockSpec((1, LANES), const),
        ],
        out_specs=[
            pl.BlockSpec((tm, d), row), pl.BlockSpec((d // 2 // SC_COLS, tm, SC_COLS), lambda i: (0, i, 0)),
            pl.BlockSpec((tm, LANES), row), pl.BlockSpec((tm, LANES), row),
            pl.BlockSpec((1, LANES), const),
        ],
        out_shape=[
            jax.ShapeDtypeStruct((n, d), F32), jax.ShapeDtypeStruct((d // 2 // SC_COLS, n, SC_COLS), jnp.int32),
            jax.ShapeDtypeStruct((n, LANES), F32), jax.ShapeDtypeStruct((n, LANES), jnp.int32),
            jax.ShapeDtypeStruct((1, LANES), F32),
        ],
        scratch_shapes=[pltpu.VMEM((1, LANES), F32)],
        compiler_params=_params(1),
        name="merge_router",
    )(x2, fox, ro, gmix, wmg, bmg, wb, wout, gffn, wr, br)


def _sc_mesh():
    return plsc.VectorSubcoreMesh(core_axis_name="core", subcore_axis_name="subcore")


def _sc_dispatch(rows, dest_t, n_out):
    chunks, n, w = rows.shape

    @functools.partial(pl.kernel, out_type=jax.ShapeDtypeStruct((chunks, n_out, w), rows.dtype),
                       mesh=_sc_mesh(), scratch_types=[], name="moe_dispatch")
    def run(x_hbm, i_hbm, o_hbm):
        for c in range(chunks):
            def body(x_vmem, i_vmem, c=c):
                for j in range(TOP_K):
                    pltpu.sync_copy(x_vmem, o_hbm.at[c].at[i_vmem.at[j]])

            pltpu.emit_pipeline(
                body,
                grid=(n // SC_WINDOW,),
                in_specs=[pl.BlockSpec((SC_WINDOW, w), lambda i: (i, 0)),
                          pl.BlockSpec((TOP_K, SC_WINDOW), lambda i: (0, i))],
                out_specs=[],
                core_axis_name=("core", "subcore"),
                dimension_semantics=(pltpu.PARALLEL,),
            )(x_hbm.at[c], i_hbm)

    return run(rows, dest_t)


def _sc_gather(table, idx):
    chunks, _, w = table.shape
    m = idx.shape[1]

    @functools.partial(pl.kernel, out_type=jax.ShapeDtypeStruct((chunks, m, w), table.dtype),
                       mesh=_sc_mesh(), scratch_types=[], name="moe_combine_gather")
    def run(t_hbm, i_hbm, o_hbm):
        for c in range(chunks):
            def body(i_vmem, o_vmem, c=c):
                pltpu.sync_copy(t_hbm.at[c].at[i_vmem.at[0]], o_vmem)

            pltpu.emit_pipeline(
                body,
                grid=(m // SC_WINDOW,),
                in_specs=[pl.BlockSpec((1, SC_WINDOW), lambda i: (0, i))],
                out_specs=[pl.BlockSpec((SC_WINDOW, w), lambda i: (i, 0))],
                core_axis_name=("core", "subcore"),
                dimension_semantics=(pltpu.PARALLEL,),
            )(i_hbm, o_hbm.at[c])

    return run(table, idx)


GLU_GROUP = 2 * LANES
EXPERT_SUB = 256


def _expert_kernel(be_ref, used_ref, x_ref, wgu_ref, bgu_ref, wd_ref, bd_ref, perm_ref, y_ref, wgu_scr, wd_scr):
    i = pl.program_id(0)
    f2 = wgu_ref.shape[2]
    live = i < used_ref[0]

    @pl.when(jnp.logical_and(live, jnp.logical_or(i == 0, be_ref[i] != be_ref[jnp.maximum(i - 1, 0)])))
    def _():
        for b in range(f2 // GLU_GROUP):
            cols = slice(b * GLU_GROUP, (b + 1) * GLU_GROUP)
            wgu_scr[:, cols] = _dot(wgu_ref[0, :, cols].astype(BF16), perm_ref[...]).astype(BF16)
        wd_scr[...] = wd_ref[0].astype(BF16)

    def sub_block(t, sub):
        rows = slice(t * sub, (t + 1) * sub)
        x = _unpack_pairs([x_ref[c, rows, :] for c in range(x_ref.shape[0])]).astype(BF16)
        yield
        gu = _dot(x, wgu_scr[...])
        yield
        gu = gu + bgu_ref[0]
        acts = []
        for b in range(f2 // GLU_GROUP):
            glu = jnp.minimum(gu[:, b * GLU_GROUP:b * GLU_GROUP + LANES], SWIGLU_LIMIT)
            lin = jnp.clip(gu[:, b * GLU_GROUP + LANES:(b + 1) * GLU_GROUP], -SWIGLU_LIMIT, SWIGLU_LIMIT)
            acts.append((glu * _sigmoid(SWIGLU_ALPHA * glu) * (lin + 1.0)).astype(BF16))
        act = jnp.concatenate(acts, axis=1)
        yield
        y = _dot(act, wd_scr[...])
        yield
        _store_chunks(y_ref, _pack_pairs(y + bd_ref[0]), rows)

    @pl.when(live)
    def _():
        bm = x_ref.shape[1]
        sub = min(EXPERT_SUB, bm)
        _interleave(sub_block(t, sub) for t in range(bm // sub))


def _experts(block_e, n_used, x_buf, wgu, bgu, wd, bd, bm):
    p = x_buf.shape[1]
    f, d = wd.shape[1:]
    perm = np.zeros((GLU_GROUP, GLU_GROUP), np.float32)
    for c in range(LANES):
        perm[2 * c, c] = 1.0
        perm[2 * c + 1, LANES + c] = 1.0
    return pl.pallas_call(
        _expert_kernel,
        grid_spec=pltpu.PrefetchScalarGridSpec(
            num_scalar_prefetch=2,
            grid=(p // bm,),
            in_specs=[
                pl.BlockSpec((d // 2 // SC_COLS, bm, SC_COLS), lambda i, be, nu: (0, i, 0)),
                pl.BlockSpec((1, d, 2 * f), lambda i, be, nu: (be[i], 0, 0)),
                pl.BlockSpec((1, 1, 2 * f), lambda i, be, nu: (be[i], 0, 0)),
                pl.BlockSpec((1, f, d), lambda i, be, nu: (be[i], 0, 0)),
                pl.BlockSpec((1, 1, d), lambda i, be, nu: (be[i], 0, 0)),
                pl.BlockSpec((GLU_GROUP, GLU_GROUP), lambda i, be, nu: (0, 0)),
            ],
            out_specs=pl.BlockSpec((d // 2 // SC_COLS, bm, SC_COLS), lambda i, be, nu: (0, i, 0)),
            scratch_shapes=[pltpu.VMEM((d, 2 * f), BF16), pltpu.VMEM((f, d), BF16)],
        ),
        out_shape=jax.ShapeDtypeStruct((d // 2 // SC_COLS, p, SC_COLS), jnp.int32),
        compiler_params=_params(1),
        name="expert_ffn",
    )(block_e, n_used, x_buf, wgu, bgu, wd, bd, jnp.asarray(perm, BF16))


FINAL_SUB = 256


def _final_kernel(h1_ref, yg_ref, rf_ref, p_ref, gple_ref, wpg_ref, wpp_ref, gfin_ref, *rest):
    o_ref = rest[-1]
    tm = h1_ref.shape[0]
    sub = min(FINAL_SUB, tm)

    def sub_tile(t):
        rows = slice(t * sub, (t + 1) * sub)
        rf = rf_ref[rows, :]
        h2 = h1_ref[rows, :]
        for j in range(TOP_K):
            h2 = h2 + _unpack_pairs([yg_ref[c, j, rows, :] for c in range(yg_ref.shape[0])]) * rf[:, j:j + 1]
        hn = _rms(h2, gple_ref[...]).astype(BF16)
        yield
        pre = _dot(hn, wpg_ref[...])
        proj = _dot(p_ref[rows, :].astype(BF16), wpp_ref[...])
        yield
        h3 = h2 + _sigmoid(pre) * proj
        o_ref[rows, :] = _rms(h3, gfin_ref[...])

    _interleave(sub_tile(t) for t in range(tm // sub))


def _final(h1, yg, rf, p2, gple, wpg, wpp, gfin, tm, first_tile, prev_out):
    n_total, d = h1.shape
    n = yg.shape[2]
    pd = p2.shape[1]
    dst = lambda i: (i + first_tile, 0)
    const = lambda i: (0, 0)
    in_specs = [
        pl.BlockSpec((tm, d), dst), pl.BlockSpec((d // 2 // SC_COLS, TOP_K, tm, SC_COLS), lambda i: (0, 0, i, 0)),
        pl.BlockSpec((tm, LANES), dst), pl.BlockSpec((tm, pd), dst),
        pl.BlockSpec((1, d), const), pl.BlockSpec((d, d), const), pl.BlockSpec((pd, d), const),
        pl.BlockSpec((1, d), const),
    ]
    args = [h1, yg, rf, p2, gple, wpg, wpp, gfin]
    aliases = {}
    if prev_out is not None:
        in_specs.append(pl.BlockSpec(memory_space=pl.ANY))
        aliases = {len(args): 0}
        args.append(prev_out)
    return pl.pallas_call(
        _final_kernel,
        grid=(n // tm,),
        in_specs=in_specs,
        out_specs=pl.BlockSpec((tm, d), dst),
        out_shape=jax.ShapeDtypeStruct((n_total, d), F32),
        input_output_aliases=aliases,
        compiler_params=_params(1),
        name="combine_ple_norm",
    )(*args)


def _layer(h, p, mix_norm, w_in, b_forget, w_branch, w_merge_gate, b_merge_gate, w_out, ffn_norm,
           w_router, b_router, w_gate_up, b_gate_up, w_down, b_down, ple_norm, w_ple_gate, w_ple_proj,
           final_norm, *, tm, tq, chunk, bm, groups):
    b, s, d = h.shape
    n = b * s
    x2 = h.reshape(n, d)
    row = lambda t: t.reshape(1, -1)

    c0 = 3 * FOX_W
    w_main = jnp.concatenate([w_in[:, :2 * FOX_W], w_in[:, c0 + FOX_HEADS:]], axis=1).astype(BF16)
    w_vt = w_in[:, 2 * FOX_W:c0].T.astype(BF16)
    w_f = jnp.pad(w_in[:, c0:c0 + FOX_HEADS], ((0, 0), (0, LANES - FOX_HEADS))).astype(BF16)
    b_f = jnp.pad(b_forget, (0, LANES - FOX_HEADS)).reshape(1, LANES)
    half = RET_KEY_DIM // 2
    inv = ROPE_BASE ** (-jnp.arange(half, dtype=F32) / half)
    ang = jnp.arange(s).astype(F32)[:, None] * inv[None, :]
    cos_t = jnp.tile(jnp.cos(ang), (1, RET_QK_W // half))
    sin_t = jnp.tile(jnp.sin(ang), (1, RET_QK_W // half))

    q_aug, k_aug, v_t, rq, rk, rv, rg = _inproj(x2, row(mix_norm), w_main, w_vt, w_f, b_f, cos_t, sin_t, s, tm)
    fox = _fox(q_aug, k_aug, v_t, b, s, tq)

    lg = jnp.log1p(-jnp.exp2(-5.0 - jnp.arange(RET_HEADS, dtype=F32)))
    ro = _retention(lg, rq.reshape(b, s, RET_QK_W), rk.reshape(b, s, RET_QK_W),
                    rv.reshape(b, s, RET_V_W), rg.reshape(b, s, RET_V_W), chunk)

    w_r = jnp.pad(w_router, ((0, 0), (0, LANES - N_EXPERTS))).astype(BF16)
    b_r = jnp.concatenate([b_router, jnp.full((LANES - N_EXPERTS,), -1e30, F32)]).reshape(1, LANES)
    ne, dd, f2 = w_gate_up.shape
    bgu = b_gate_up.reshape(ne, f2 // GLU_GROUP, LANES, 2).transpose(0, 1, 3, 2).reshape(ne, 1, f2)
    merge_w = (row(mix_norm), w_merge_gate.astype(BF16), row(b_merge_gate), w_branch.astype(BF16),
               w_out.astype(BF16), row(ffn_norm), w_r, b_r)
    final_w = (row(ple_norm), w_ple_gate.astype(BF16), w_ple_proj.astype(BF16), row(final_norm))
    p2 = p.reshape(n, -1)

    h1, hn, rf, ri, cnt = _merge(x2, fox, ro.reshape(n, RET_V_W), *merge_w, tm, 0, n)

    a = n * TOP_K
    nb = -(-(a + N_EXPERTS * (bm - 1)) // bm)
    e_idx = ri[:, :TOP_K]
    rank = ri[:, TOP_K:2 * TOP_K]
    counts = cnt[0, :N_EXPERTS].astype(jnp.int32)
    padded = (counts + bm - 1) // bm * bm
    pad_end = jnp.cumsum(padded)
    pad_start = pad_end - padded
    dest_t = (pad_start[e_idx] + rank).T
    block_start = jnp.arange(nb, dtype=jnp.int32) * bm
    block_e = jnp.minimum(jnp.sum(block_start[:, None] >= pad_end[None, :], axis=1),
                          N_EXPERTS - 1).astype(jnp.int32)
    n_used = (pad_end[-1:] // bm).astype(jnp.int32)

    x_buf = _sc_dispatch(hn, dest_t, nb * bm)
    y_buf = _experts(block_e, n_used, x_buf, w_gate_up, bgu, w_down, b_down.reshape(ne, 1, dd), bm)

    ng = n // groups
    out = None
    for g in range(groups):
        idx = dest_t[:, g * ng:(g + 1) * ng].reshape(1, ng * TOP_K)
        yg = _sc_gather(y_buf, idx).reshape(d // 2 // SC_COLS, TOP_K, ng, SC_COLS)
        out = _final(h1, yg, rf, p2, *final_w, tm, g * (ng // tm), out)
    return out.reshape(b, s, d)


def kernel(x, p, mix_norm, w_in, b_forget, w_branch, w_merge_gate, b_merge_gate, w_out, ffn_norm, w_router,
           b_router, w_gate_up, b_gate_up, w_down, b_down, ple_norm, w_ple_gate, w_ple_proj, final_norm):
    depth = p.shape[0]
    assert depth == 1, "the final norm is fused into the (single) layer"
    return _layer(x, p[0], mix_norm[0], w_in[0], b_forget[0], w_branch[0], w_merge_gate[0], b_merge_gate[0],
                  w_out[0], ffn_norm[0], w_router[0], b_router[0], w_gate_up[0], b_gate_up[0], w_down[0],
                  b_down[0], ple_norm[0], w_ple_gate[0], w_ple_proj[0], final_norm,
                  tm=512, tq=512, chunk=256, bm=512, groups=4)
```

```python
import functools

import numpy as np
import jax
import jax.numpy as jnp
from jax import lax
from jax.experimental import pallas as pl
from jax.experimental.pallas import tpu as pltpu
from jax.experimental.pallas import tpu_sc as plsc

FOX_HEADS = 8
FOX_HEAD_DIM = 64
RET_HEADS = 4
RET_KEY_DIM = 64
RET_VAL_DIM = 128
ROPE_BASE = 10000.0
N_EXPERTS = 32
TOP_K = 4
SWIGLU_LIMIT = 7.0
SWIGLU_ALPHA = 1.702
EPS = 1e-6

LANES = 128
FOX_W = FOX_HEADS * FOX_HEAD_DIM
RET_QK_W = RET_HEADS * RET_KEY_DIM
RET_V_W = RET_HEADS * RET_VAL_DIM
VMEM_LIMIT = 56 * 1024 * 1024
HIGH_HALF = -65536
LOG2_E = 1.4426950408889634
SC_WINDOW = 128
SC_COLS = 256

F32 = jnp.float32
BF16 = jnp.bfloat16


def _rms(x, g):
    return x * lax.rsqrt(jnp.mean(x * x, axis=-1, keepdims=True) + EPS) * g


def _sigmoid(x):
    return 1.0 / (1.0 + jnp.exp(-x))


def _dot(a, b):
    return jnp.dot(a, b, preferred_element_type=F32)


def _dot_nt(a, b):
    return lax.dot_general(a, b, (((1,), (1,)), ((), ())), preferred_element_type=F32)


def _pack_pairs(v):
    bits = pltpu.bitcast(v.astype(BF16).astype(F32), jnp.int32)
    half = v.shape[1] // 2
    return lax.shift_right_logical(bits[:, :half], 16) | (bits[:, half:] & HIGH_HALF)


def _unpack_pairs(chunks):
    return jnp.concatenate([pltpu.bitcast(lax.shift_left(w, 16), F32) for w in chunks]
                           + [pltpu.bitcast(w & HIGH_HALF, F32) for w in chunks], axis=1)


def _store_chunks(ref, v, rows=slice(None)):
    for c in range(ref.shape[0]):
        ref[c, rows, :] = v[:, c * SC_COLS:(c + 1) * SC_COLS]


def _interleave(generators):
    live = []
    pending = list(generators)
    while pending or live:
        if pending:
            live.append(pending.pop(0))
        for g in list(live):
            try:
                next(g)
            except StopIteration:
                live.remove(g)


def _params(n_axes):
    return pltpu.CompilerParams(dimension_semantics=("arbitrary",) * n_axes,
                                vmem_limit_bytes=VMEM_LIMIT)


C_MID = FOX_HEADS
C_LO = 2 * FOX_HEADS
C_ONE = 3 * FOX_HEADS
N_AUG = 3


def _split3(t):
    hi = t.astype(BF16).astype(F32)
    r = t - hi
    mid = r.astype(BF16).astype(F32)
    lo = (r - mid).astype(BF16).astype(F32)
    return hi + pltpu.roll(mid, C_MID, 1) + pltpu.roll(lo, C_LO, 1)


INPROJ_SUB = 256


def _inproj_kernel(x_ref, g_ref, w_ref, wvt_ref, wf_ref, bf_ref, cos_ref, sin_ref, pq_ref, pk_ref,
                   q_ref, k_ref, vt_ref, rq_ref, rk_ref, rv_ref, rg_ref, carry_scr, *, tiles_per_seq):
    i = pl.program_id(0)
    tm = x_ref.shape[0]
    sub = min(INPROJ_SUB, tm)

    @pl.when(i % tiles_per_seq == 0)
    def _():
        carry_scr[...] = jnp.zeros(carry_scr.shape, F32)

    lane = lax.broadcasted_iota(jnp.int32, (sub, LANES), 1)
    r = lax.broadcasted_iota(jnp.int32, (sub, sub), 0)
    c = lax.broadcasted_iota(jnp.int32, (sub, sub), 1)
    tri = jnp.where(c <= r, 1.0, 0.0).astype(BF16)
    own = lane < FOX_HEAD_DIM
    half = RET_KEY_DIM // 2
    lane_r = lax.broadcasted_iota(jnp.int32, (sub, RET_QK_W), 1)
    first = (lane_r % RET_KEY_DIM) < half
    carries = [carry_scr[...]]

    def sub_tile(t):
        rows = slice(t * sub, (t + 1) * sub)
        xn = _rms(x_ref[rows, :], g_ref[...]).astype(BF16)
        yield
        u = _dot(xn, w_ref[...])
        vt_ref[:, rows] = _dot_nt(wvt_ref[...], xn).astype(BF16)
        z = _dot(xn, wf_ref[...]) + bf_ref[...]
        yield
        lf = jnp.where(lane < FOX_HEADS, jnp.minimum(z, 0.0) - jnp.log1p(jnp.exp(-jnp.abs(z))), 0.0)
        ps = _dot(tri, _split3(lf).astype(BF16))
        yield
        assert len(carries) == t + 1
        cum = ps + pltpu.roll(ps, LANES - C_MID, 1) + pltpu.roll(ps, LANES - C_LO, 1)
        cum = jnp.where(lane < FOX_HEADS, cum, 0.0) + carries[t]
        carries.append(cum[sub - 1:sub, :])
        c3 = (_split3(cum * LOG2_E) + jnp.where(lane == C_ONE, 1.0, 0.0)).astype(BF16)
        aug_q = _dot(c3, pq_ref[...])
        aug_k = _dot(c3, pk_ref[...])

        o = 2 * FOX_W
        rq = u[:, o:o + RET_QK_W]; o += RET_QK_W
        rk = u[:, o:o + RET_QK_W]; o += RET_QK_W
        rv_ref[rows, :] = u[:, o:o + RET_V_W].astype(BF16); o += RET_V_W
        rg_ref[rows, :] = u[:, o:o + RET_V_W]

        cos = cos_ref[rows, :]
        sin = sin_ref[rows, :]

        def rot(v):
            partner = jnp.where(first, -pltpu.roll(v, RET_QK_W - half, 1), pltpu.roll(v, half, 1))
            return v * cos + partner * sin

        rq_ref[rows, :] = rot(rq).astype(BF16)
        rk_ref[rows, :] = (rot(rk) * (RET_KEY_DIM ** -0.5)).astype(BF16)
        yield
        fq = u[:, 0:FOX_W] * (FOX_HEAD_DIM ** -0.5 * LOG2_E)
        fk = u[:, FOX_W:2 * FOX_W]
        for src, aug, dst in ((fq, aug_q, q_ref), (fk, aug_k, k_ref)):
            for j in range(FOX_W // LANES):
                blk = src[:, j * LANES:(j + 1) * LANES]
                e0, e1 = 2 * j * LANES, (2 * j + 1) * LANES
                dst[rows, e0:e0 + LANES] = jnp.where(own, blk, aug[:, e0:e0 + LANES]).astype(BF16)
                dst[rows, e1:e1 + LANES] = jnp.where(own, pltpu.roll(blk, FOX_HEAD_DIM, 1),
                                                      aug[:, e1:e1 + LANES]).astype(BF16)

    _interleave(sub_tile(t) for t in range(tm // sub))
    carry_scr[...] = carries[-1]


def _placement():
    pq = np.zeros((LANES, FOX_HEADS * LANES), np.float32)
    pk = np.zeros((LANES, FOX_HEADS * LANES), np.float32)
    for h in range(FOX_HEADS):
        base = h * LANES + FOX_HEAD_DIM
        for part, src in enumerate((h, C_MID + h, C_LO + h)):
            pq[C_ONE, base + part] = 1.0
            pq[src, base + N_AUG + part] = 1.0
            pk[src, base + part] = -1.0
            pk[C_ONE, base + N_AUG + part] = 1.0
    return jnp.asarray(pq, BF16), jnp.asarray(pk, BF16)


def _inproj(x2, g, w_main, w_vt, w_f, b_f, cos_t, sin_t, seq, tm):
    n, d = x2.shape
    wn = w_main.shape[1]
    spt = seq // tm
    aw = FOX_HEADS * LANES
    pq, pk = _placement()
    row = lambda i: (i, 0)
    const = lambda i: (0, 0)
    pos = lambda i: (i % spt, 0)
    out_shape = [
        jax.ShapeDtypeStruct((n, aw), BF16), jax.ShapeDtypeStruct((n, aw), BF16),
        jax.ShapeDtypeStruct((FOX_W, n), BF16),
        jax.ShapeDtypeStruct((n, RET_QK_W), BF16), jax.ShapeDtypeStruct((n, RET_QK_W), BF16),
        jax.ShapeDtypeStruct((n, RET_V_W), BF16), jax.ShapeDtypeStruct((n, RET_V_W), F32),
    ]
    out_specs = [
        pl.BlockSpec((tm, aw), row), pl.BlockSpec((tm, aw), row),
        pl.BlockSpec((FOX_W, tm), lambda i: (0, i)),
        pl.BlockSpec((tm, RET_QK_W), row), pl.BlockSpec((tm, RET_QK_W), row),
        pl.BlockSpec((tm, RET_V_W), row), pl.BlockSpec((tm, RET_V_W), row),
    ]
    return pl.pallas_call(
        functools.partial(_inproj_kernel, tiles_per_seq=spt),
        grid=(n // tm,),
        in_specs=[
            pl.BlockSpec((tm, d), row), pl.BlockSpec((1, d), const),
            pl.BlockSpec((d, wn), const), pl.BlockSpec((FOX_W, d), const),
            pl.BlockSpec((d, LANES), const), pl.BlockSpec((1, LANES), const),
            pl.BlockSpec((tm, RET_QK_W), pos), pl.BlockSpec((tm, RET_QK_W), pos),
            pl.BlockSpec((LANES, aw), const), pl.BlockSpec((LANES, aw), const),
        ],
        out_specs=out_specs,
        out_shape=out_shape,
        scratch_shapes=[pltpu.VMEM((1, LANES), F32)],
        compiler_params=_params(1),
        name="inproj",
    )(x2, g, w_main, w_vt, w_f, b_f, cos_t, sin_t, pq, pk)


def _fox_kernel(q_ref, k_ref, vt_ref, o_ref, *, tq):
    seq = q_ref.shape[0]
    key = lax.broadcasted_iota(jnp.int32, (tq, tq), 0)
    qry = lax.broadcasted_iota(jnp.int32, (tq, tq), 1)
    vrow = lax.broadcasted_iota(jnp.int32, (LANES, tq), 0)
    one = jnp.ones((), BF16)
    items = [(qi, j) for qi in range(seq // tq) for j in range(qi + 1)]

    def logits(qi, j):
        out = []
        for a in range(2):
            sl = slice(a * LANES, (a + 1) * LANES)
            s = _dot_nt(k_ref[j * tq:(j + 1) * tq, sl], q_ref[qi * tq:(qi + 1) * tq, sl])
            out.append(jnp.where(key <= qry, s, -jnp.inf) if j == qi else s)
        return out

    s_cur = logits(*items[0])
    carry = None
    for w, (qi, j) in enumerate(items):
        s_next = logits(*items[w + 1]) if w + 1 < len(items) else None
        if j == 0:
            carry = [(jnp.full((1, tq), -jnp.inf, F32), jnp.zeros((LANES, tq), F32)) for _ in range(2)]
        v = vt_ref[:, j * tq:(j + 1) * tq]
        for a in range(2):
            m_prev, acc = carry[a]
            m_new = jnp.maximum(m_prev, jnp.max(s_cur[a], axis=0, keepdims=True))
            alpha = jnp.exp2(m_prev - m_new)
            p = jnp.exp2(s_cur[a] - m_new).astype(BF16)
            va = jnp.where((vrow // FOX_HEAD_DIM) == a, v, one)
            carry[a] = (m_new, alpha * acc + _dot(va, p))
        if j == qi:
            acc0, acc1 = carry[0][1], carry[1][1]
            ot = jnp.where(vrow < FOX_HEAD_DIM, acc0 / acc0[FOX_HEAD_DIM:FOX_HEAD_DIM + 1, :],
                           acc1 / acc1[0:1, :])
            o_ref[qi * tq:(qi + 1) * tq, :] = ot.T.astype(o_ref.dtype)
        s_cur = s_next


def _fox(q_aug, k_aug, v_t, batch, seq, tq):
    n = q_aug.shape[0]
    pairs = FOX_HEADS // 2
    kern = functools.partial(_fox_kernel, tq=tq)
    return pl.pallas_call(
        kern,
        grid=(batch, pairs),
        in_specs=[
            pl.BlockSpec((seq, 2 * LANES), lambda i, h: (i, h)),
            pl.BlockSpec((seq, 2 * LANES), lambda i, h: (i, h)),
            pl.BlockSpec((LANES, seq), lambda i, h: (h, i)),
        ],
        out_specs=pl.BlockSpec((seq, LANES), lambda i, h: (i, h)),
        out_shape=jax.ShapeDtypeStruct((n, FOX_W), BF16),
        compiler_params=_params(2),
        name="fox_attention",
    )(q_aug, k_aug, v_t)


def _ret_kernel(lg_ref, q_ref, k_ref, v_ref, g_ref, o_ref, *, chunk):
    s = q_ref.shape[1]
    lane = lax.broadcasted_iota(jnp.int32, (chunk, LANES), 1)
    ri = lax.broadcasted_iota(jnp.int32, (chunk, chunk), 0)
    ci = lax.broadcasted_iota(jnp.int32, (chunk, chunk), 1)
    diff = (ri - ci).astype(F32)
    pos = lax.broadcasted_iota(jnp.int32, (chunk, 1), 0).astype(F32)

    def head(h):
        lg = lg_ref[h]
        mine = (lane // RET_KEY_DIM) == (h % 2)
        qk = slice((h // 2) * LANES, (h // 2 + 1) * LANES)
        vs = slice(h * RET_VAL_DIM, (h + 1) * RET_VAL_DIM)
        inner = jnp.where(diff >= 0, jnp.exp(jnp.maximum(diff, 0.0) * lg), 0.0)
        q_decay = jnp.exp((pos + 1.0) * lg)
        k_decay = jnp.exp((chunk - 1.0 - pos) * lg)
        chunk_decay = jnp.exp(jnp.full((1, 1), chunk, F32) * lg)
        state = jnp.zeros((LANES, RET_VAL_DIM), F32)
        for c in range(s // chunk):
            rows = slice(c * chunk, (c + 1) * chunk)
            q = jnp.where(mine, q_ref[0, rows, qk], jnp.zeros((), BF16))
            k = jnp.where(mine, k_ref[0, rows, qk], jnp.zeros((), BF16))
            v = v_ref[0, rows, vs]
            scores = _dot_nt(q, k)
            inter = _dot(q, state.astype(BF16))
            kd = (k.astype(F32) * k_decay).astype(BF16)
            update = lax.dot_general(kd, v, (((0,), (0,)), ((), ())), preferred_element_type=F32)
            yield
            o = _dot((scores * inner).astype(BF16), v) + inter * q_decay
            state = state * chunk_decay + update
            yield
            o = o * lax.rsqrt(jnp.mean(o * o, axis=-1, keepdims=True) + EPS)
            g = g_ref[0, rows, vs]
            o_ref[0, rows, vs] = (o * (g * _sigmoid(g))).astype(o_ref.dtype)

    _interleave(head(h) for h in range(RET_HEADS))


def _retention(lg, rq, rk, rv, rg, chunk):
    b, s, _ = rq.shape
    kern = functools.partial(_ret_kernel, chunk=chunk)
    qk_spec = pl.BlockSpec((1, s, RET_QK_W), lambda i, lg_ref: (i, 0, 0))
    v_spec = pl.BlockSpec((1, s, RET_V_W), lambda i, lg_ref: (i, 0, 0))
    return pl.pallas_call(
        kern,
        grid_spec=pltpu.PrefetchScalarGridSpec(
            num_scalar_prefetch=1,
            grid=(b,),
            in_specs=[qk_spec, qk_spec, v_spec, v_spec],
            out_specs=v_spec,
        ),
        out_shape=jax.ShapeDtypeStruct((b, s, RET_V_W), BF16),
        compiler_params=_params(1),
        name="retention",
    )(lg, rq, rk, rv, rg)


MERGE_SUB = 256


def _merge_kernel(x_ref, fox_ref, ro_ref, gmix_ref, wmg_ref, bmg_ref, wb_ref, wout_ref, gffn_ref,
                  wr_ref, br_ref, h1_ref, hn_ref, rf_ref, ri_ref, cnt_ref, base_scr):
    i = pl.program_id(0)
    tm, d = x_ref.shape
    sub = min(MERGE_SUB, tm)

    @pl.when(i == 0)
    def _():
        base_scr[...] = jnp.zeros(base_scr.shape, F32)

    lane = lax.broadcasted_iota(jnp.int32, (sub, LANES), 1)
    lane_f = lane.astype(F32)
    r = lax.broadcasted_iota(jnp.int32, (sub, sub), 0)
    c = lax.broadcasted_iota(jnp.int32, (sub, sub), 1)
    tri = jnp.where(c < r, 1.0, 0.0).astype(BF16)
    counts = [base_scr[...]]

    def sub_tile(t):
        rows = slice(t * sub, (t + 1) * sub)
        x = x_ref[rows, :]
        xn = _rms(x, gmix_ref[...]).astype(BF16)
        yield
        pre = _dot(xn, wmg_ref[...])
        yield
        gate = _sigmoid(pre + bmg_ref[...])
        yield
        pf = _dot(fox_ref[rows, :], wb_ref[0])
        pr = _dot(ro_ref[rows, :], wb_ref[1])
        yield
        merged = (gate[:, :d] * pf + gate[:, d:] * pr).astype(BF16)
        yield
        h1 = x + _dot(merged, wout_ref[...])
        yield
        h1_ref[rows, :] = h1
        hn = _rms(h1, gffn_ref[...]).astype(BF16)
        _store_chunks(hn_ref, _pack_pairs(hn), rows)
        yield
        logits = _dot(hn, wr_ref[...]) + br_ref[...]
        yield
        vals, idxs = [], []
        cur = logits
        for _ in range(TOP_K):
            m = jnp.max(cur, axis=-1, keepdims=True)
            idx = jnp.min(jnp.where(cur == m, lane_f, float(LANES)), axis=-1, keepdims=True)
            vals.append(m)
            idxs.append(idx)
            cur = jnp.where(lane_f == idx, -jnp.inf, cur)
        exps = [jnp.exp(v - vals[0]) for v in vals]
        denom = exps[0] + exps[1] + exps[2] + exps[3]
        onehot = jnp.zeros(logits.shape, F32)
        for idx in idxs:
            onehot = onehot + jnp.where(lane_f == idx, 1.0, 0.0)
        yield
        assert len(counts) == t + 1
        before = _dot(tri, onehot.astype(BF16)) + counts[t]
        counts.append(counts[t] + jnp.sum(onehot, axis=0, keepdims=True))
        rf = jnp.zeros(logits.shape, F32)
        ri = jnp.zeros(logits.shape, F32)
        for j in range(TOP_K):
            rank = jnp.sum(jnp.where(lane_f == idxs[j], before, 0.0), axis=-1, keepdims=True)
            rf = jnp.where(lane == j, exps[j] / denom, rf)
            ri = jnp.where(lane == j, idxs[j], ri)
            ri = jnp.where(lane == TOP_K + j, rank, ri)
        rf_ref[rows, :] = rf
        ri_ref[rows, :] = ri.astype(jnp.int32)

    _interleave(sub_tile(t) for t in range(tm // sub))
    base_scr[...] = counts[-1]
    cnt_ref[...] = counts[-1]


def _merge(x2, fox, ro, gmix, wmg, bmg, wb, wout, gffn, wr, br, tm, first_tile, n):
    d = x2.shape[1]
    row = lambda i: (i, 0)
    src = lambda i: (i + first_tile, 0)
    const = lambda i: (0, 0)
    return pl.pallas_call(
        _merge_kernel,
        grid=(n // tm,),
        in_specs=[
            pl.BlockSpec((tm, d), src), pl.BlockSpec((tm, FOX_W), src), pl.BlockSpec((tm, RET_V_W), src),
            pl.BlockSpec((1, d), const), pl.BlockSpec((d, 2 * d), const), pl.BlockSpec((1, 2 * d), const),
            pl.BlockSpec((2, FOX_W, d), lambda i: (0, 0, 0)), pl.BlockSpec((d, d), const),
            pl.BlockSpec((1, d), const), pl.BlockSpec((d, LANES), const), pl.BlockSpec((1, LANES), const),
        ],
        out_specs=[
            pl.BlockSpec((tm, d), row), pl.BlockSpec((d // 2 // SC_COLS, tm, SC_COLS), lambda i: (0, i, 0)),
            pl.BlockSpec((tm, LANES), row), pl.BlockSpec((tm, LANES), row),
            pl.BlockSpec((1, LANES), const),
        ],
        out_shape=[
            jax.ShapeDtypeStruct((n, d), F32), jax.ShapeDtypeStruct((d // 2 // SC_COLS, n, SC_COLS), jnp.int32),
            jax.ShapeDtypeStruct((n, LANES), F32), jax.ShapeDtypeStruct((n, LANES), jnp.int32),
            jax.ShapeDtypeStruct((1, LANES), F32),
        ],
        scratch_shapes=[pltpu.VMEM((1, LANES), F32)],
        compiler_params=_params(1),
        name="merge_router",
    )(x2, fox, ro, gmix, wmg, bmg, wb, wout, gffn, wr, br)


def _sc_mesh():
    return plsc.VectorSubcoreMesh(core_axis_name="core", subcore_axis_name="subcore")


def _sc_dispatch(rows, dest_t, n_out):
    chunks, n, w = rows.shape

    @functools.partial(pl.kernel, out_type=jax.ShapeDtypeStruct((chunks, n_out, w), rows.dtype),
                       mesh=_sc_mesh(), scratch_types=[], name="moe_dispatch")
    def run(x_hbm, i_hbm, o_hbm):
        for c in range(chunks):
            def body(x_vmem, i_vmem, c=c):
                for j in range(TOP_K):
                    pltpu.sync_copy(x_vmem, o_hbm.at[c].at[i_vmem.at[j]])

            pltpu.emit_pipeline(
                body,
                grid=(n // SC_WINDOW,),
                in_specs=[pl.BlockSpec((SC_WINDOW, w), lambda i: (i, 0)),
                          pl.BlockSpec((TOP_K, SC_WINDOW), lambda i: (0, i))],
                out_specs=[],
                core_axis_name=("core", "subcore"),
                dimension_semantics=(pltpu.PARALLEL,),
            )(x_hbm.at[c], i_hbm)

    return run(rows, dest_t)


def _sc_gather(table, idx):
    chunks, _, w = table.shape
    m = idx.shape[1]

    @functools.partial(pl.kernel, out_type=jax.ShapeDtypeStruct((chunks, m, w), table.dtype),
                       mesh=_sc_mesh(), scratch_types=[], name="moe_combine_gather")
    def run(t_hbm, i_hbm, o_hbm):
        for c in range(chunks):
            def body(i_vmem, o_vmem, c=c):
                pltpu.sync_copy(t_hbm.at[c].at[i_vmem.at[0]], o_vmem)

            pltpu.emit_pipeline(
                body,
                grid=(m // SC_WINDOW,),
                in_specs=[pl.BlockSpec((1, SC_WINDOW), lambda i: (0, i))],
                out_specs=[pl.BlockSpec((SC_WINDOW, w), lambda i: (i, 0))],
                core_axis_name=("core", "subcore"),
                dimension_semantics=(pltpu.PARALLEL,),
            )(i_hbm, o_hbm.at[c])

    return run(table, idx)


GLU_GROUP = 2 * LANES
EXPERT_SUB = 256


def _expert_kernel(be_ref, used_ref, x_ref, wgu_ref, bgu_ref, wd_ref, bd_ref, perm_ref, y_ref, wgu_scr, wd_scr):
    i = pl.program_id(0)
    f2 = wgu_ref.shape[2]
    live = i < used_ref[0]

    @pl.when(jnp.logical_and(live, jnp.logical_or(i == 0, be_ref[i] != be_ref[jnp.maximum(i - 1, 0)])))
    def _():
        for b in range(f2 // GLU_GROUP):
            cols = slice(b * GLU_GROUP, (b + 1) * GLU_GROUP)
            wgu_scr[:, cols] = _dot(wgu_ref[0, :, cols].astype(BF16), perm_ref[...]).astype(BF16)
        wd_scr[...] = wd_ref[0].astype(BF16)

    def sub_block(t, sub):
        rows = slice(t * sub, (t + 1) * sub)
        x = _unpack_pairs([x_ref[c, rows, :] for c in range(x_ref.shape[0])]).astype(BF16)
        yield
        gu = _dot(x, wgu_scr[...])
        yield
        gu = gu + bgu_ref[0]
        acts = []
        for b in range(f2 // GLU_GROUP):
            glu = jnp.minimum(gu[:, b * GLU_GROUP:b * GLU_GROUP + LANES], SWIGLU_LIMIT)
            lin = jnp.clip(gu[:, b * GLU_GROUP + LANES:(b + 1) * GLU_GROUP], -SWIGLU_LIMIT, SWIGLU_LIMIT)
            acts.append((glu * _sigmoid(SWIGLU_ALPHA * glu) * (lin + 1.0)).astype(BF16))
        act = jnp.concatenate(acts, axis=1)
        yield
        y = _dot(act, wd_scr[...])
        yield
        _store_chunks(y_ref, _pack_pairs(y + bd_ref[0]), rows)

    @pl.when(live)
    def _():
        bm = x_ref.shape[1]
        sub = min(EXPERT_SUB, bm)
        _interleave(sub_block(t, sub) for t in range(bm // sub))


def _experts(block_e, n_used, x_buf, wgu, bgu, wd, bd, bm):
    p = x_buf.shape[1]
    f, d = wd.shape[1:]
    perm = np.zeros((GLU_GROUP, GLU_GROUP), np.float32)
    for c in range(LANES):
        perm[2 * c, c] = 1.0
        perm[2 * c + 1, LANES + c] = 1.0
    return pl.pallas_call(
        _expert_kernel,
        grid_spec=pltpu.PrefetchScalarGridSpec(
            num_scalar_prefetch=2,
            grid=(p // bm,),
            in_specs=[
                pl.BlockSpec((d // 2 // SC_COLS, bm, SC_COLS), lambda i, be, nu: (0, i, 0)),
                pl.BlockSpec((1, d, 2 * f), lambda i, be, nu: (be[i], 0, 0)),
                pl.BlockSpec((1, 1, 2 * f), lambda i, be, nu: (be[i], 0, 0)),
                pl.BlockSpec((1, f, d), lambda i, be, nu: (be[i], 0, 0)),
                pl.BlockSpec((1, 1, d), lambda i, be, nu: (be[i], 0, 0)),
                pl.BlockSpec((GLU_GROUP, GLU_GROUP), lambda i, be, nu: (0, 0)),
            ],
            out_specs=pl.BlockSpec((d // 2 // SC_COLS, bm, SC_COLS), lambda i, be, nu: (0, i, 0)),
            scratch_shapes=[pltpu.VMEM((d, 2 * f), BF16), pltpu.VMEM((f, d), BF16)],
        ),
        out_shape=jax.ShapeDtypeStruct((d // 2 // SC_COLS, p, SC_COLS), jnp.int32),
        compiler_params=_params(1),
        name="expert_ffn",
    )(block_e, n_used, x_buf, wgu, bgu, wd, bd, jnp.asarray(perm, BF16))


FINAL_SUB = 256


def _final_kernel(h1_ref, yg_ref, rf_ref, p_ref, gple_ref, wpg_ref, wpp_ref, gfin_ref, *rest):
    o_ref = rest[-1]
    tm = h1_ref.shape[0]
    sub = min(FINAL_SUB, tm)

    def sub_tile(t):
        rows = slice(t * sub, (t + 1) * sub)
        rf = rf_ref[rows, :]
        h2 = h1_ref[rows, :]
        for j in range(TOP_K):
            h2 = h2 + _unpack_pairs([yg_ref[c, j, rows, :] for c in range(yg_ref.shape[0])]) * rf[:, j:j + 1]
        hn = _rms(h2, gple_ref[...]).astype(BF16)
        yield
        pre = _dot(hn, wpg_ref[...])
        proj = _dot(p_ref[rows, :].astype(BF16), wpp_ref[...])
        yield
        h3 = h2 + _sigmoid(pre) * proj
        o_ref[rows, :] = _rms(h3, gfin_ref[...])

    _interleave(sub_tile(t) for t in range(tm // sub))


def _final(h1, yg, rf, p2, gple, wpg, wpp, gfin, tm, first_tile, prev_out):
    n_total, d = h1.shape
    n = yg.shape[2]
    pd = p2.shape[1]
    dst = lambda i: (i + first_tile, 0)
    const = lambda i: (0, 0)
    in_specs = [
        pl.BlockSpec((tm, d), dst), pl.BlockSpec((d // 2 // SC_COLS, TOP_K, tm, SC_COLS), lambda i: (0, 0, i, 0)),
        pl.BlockSpec((tm, LANES), dst), pl.BlockSpec((tm, pd), dst),
        pl.BlockSpec((1, d), const), pl.BlockSpec((d, d), const), pl.BlockSpec((pd, d), const),
        pl.BlockSpec((1, d), const),
    ]
    args = [h1, yg, rf, p2, gple, wpg, wpp, gfin]
    aliases = {}
    if prev_out is not None:
        in_specs.append(pl.BlockSpec(memory_space=pl.ANY))
        aliases = {len(args): 0}
        args.append(prev_out)
    return pl.pallas_call(
        _final_kernel,
        grid=(n // tm,),
        in_specs=in_specs,
        out_specs=pl.BlockSpec((tm, d), dst),
        out_shape=jax.ShapeDtypeStruct((n_total, d), F32),
        input_output_aliases=aliases,
        compiler_params=_params(1),
        name="combine_ple_norm",
    )(*args)


def _layer(h, p, mix_norm, w_in, b_forget, w_branch, w_merge_gate, b_merge_gate, w_out, ffn_norm,
           w_router, b_router, w_gate_up, b_gate_up, w_down, b_down, ple_norm, w_ple_gate, w_ple_proj,
           final_norm, *, tm, tq, chunk, bm, groups):
    b, s, d = h.shape
    n = b * s
    x2 = h.reshape(n, d)
    row = lambda t: t.reshape(1, -1)

    c0 = 3 * FOX_W
    w_main = jnp.concatenate([w_in[:, :2 * FOX_W], w_in[:, c0 + FOX_HEADS:]], axis=1).astype(BF16)
    w_vt = w_in[:, 2 * FOX_W:c0].T.astype(BF16)
    w_f = jnp.pad(w_in[:, c0:c0 + FOX_HEADS], ((0, 0), (0, LANES - FOX_HEADS))).astype(BF16)
    b_f = jnp.pad(b_forget, (0, LANES - FOX_HEADS)).reshape(1, LANES)
    half = RET_KEY_DIM // 2
    inv = ROPE_BASE ** (-jnp.arange(half, dtype=F32) / half)
    ang = jnp.arange(s).astype(F32)[:, None] * inv[None, :]
    cos_t = jnp.tile(jnp.cos(ang), (1, RET_QK_W // half))
    sin_t = jnp.tile(jnp.sin(ang), (1, RET_QK_W // half))

    q_aug, k_aug, v_t, rq, rk, rv, rg = _inproj(x2, row(mix_norm), w_main, w_vt, w_f, b_f, cos_t, sin_t, s, tm)
    fox = _fox(q_aug, k_aug, v_t, b, s, tq)

    lg = jnp.log1p(-jnp.exp2(-5.0 - jnp.arange(RET_HEADS, dtype=F32)))
    ro = _retention(lg, rq.reshape(b, s, RET_QK_W), rk.reshape(b, s, RET_QK_W),
                    rv.reshape(b, s, RET_V_W), rg.reshape(b, s, RET_V_W), chunk)

    w_r = jnp.pad(w_router, ((0, 0), (0, LANES - N_EXPERTS))).astype(BF16)
    b_r = jnp.concatenate([b_router, jnp.full((LANES - N_EXPERTS,), -1e30, F32)]).reshape(1, LANES)
    ne, dd, f2 = w_gate_up.shape
    bgu = b_gate_up.reshape(ne, f2 // GLU_GROUP, LANES, 2).transpose(0, 1, 3, 2).reshape(ne, 1, f2)
    merge_w = (row(mix_norm), w_merge_gate.astype(BF16), row(b_merge_gate), w_branch.astype(BF16),
               w_out.astype(BF16), row(ffn_norm), w_r, b_r)
    final_w = (row(ple_norm), w_ple_gate.astype(BF16), w_ple_proj.astype(BF16), row(final_norm))
    p2 = p.reshape(n, -1)

    h1, hn, rf, ri, cnt = _merge(x2, fox, ro.reshape(n, RET_V_W), *merge_w, tm, 0, n)

    a = n * TOP_K
    nb = -(-(a + N_EXPERTS * (bm - 1)) // bm)
    e_idx = ri[:, :TOP_K]
    rank = ri[:, TOP_K:2 * TOP_K]
    counts = cnt[0, :N_EXPERTS].astype(jnp.int32)
    padded = (counts + bm - 1) // bm * bm
    pad_end = jnp.cumsum(padded)
    pad_start = pad_end - padded
    dest_t = (pad_start[e_idx] + rank).T
    block_start = jnp.arange(nb, dtype=jnp.int32) * bm
    block_e = jnp.minimum(jnp.sum(block_start[:, None] >= pad_end[None, :], axis=1),
                          N_EXPERTS - 1).astype(jnp.int32)
    n_used = (pad_end[-1:] // bm).astype(jnp.int32)

    x_buf = _sc_dispatch(hn, dest_t, nb * bm)
    y_buf = _experts(block_e, n_used, x_buf, w_gate_up, bgu, w_down, b_down.reshape(ne, 1, dd), bm)

    ng = n // groups
    out = None
    for g in range(groups):
        idx = dest_t[:, g * ng:(g + 1) * ng].reshape(1, ng * TOP_K)
        yg = _sc_gather(y_buf, idx).reshape(d // 2 // SC_COLS, TOP_K, ng, SC_COLS)
        out = _final(h1, yg, rf, p2, *final_w, tm, g * (ng // tm), out)
    return out.reshape(b, s, d)


def kernel(x, p, mix_norm, w_in, b_forget, w_branch, w_merge_gate, b_merge_gate, w_out, ffn_norm, w_router,
           b_router, w_gate_up, b_gate_up, w_down, b_down, ple_norm, w_ple_gate, w_ple_proj, final_norm):
    depth = p.shape[0]
    assert depth == 1, "the final norm is fused into the (single) layer"
    return _layer(x, p[0], mix_norm[0], w_in[0], b_forget[0], w_branch[0], w_merge_gate[0], b_merge_gate[0],
                  w_out[0], ffn_norm[0], w_router[0], b_router[0], w_gate_up[0], b_gate_up[0], w_down[0],
                  b_down[0], ple_norm[0], w_ple_gate[0], w_ple_proj[0], final_norm,
                  tm=512, tq=512, chunk=256, bm=512, groups=4)
```

```python
import functools

import numpy as np
import jax
import jax.numpy as jnp
from jax import lax
from jax.experimental import pallas as pl
from jax.experimental.pallas import tpu as pltpu
from jax.experimental.pallas import tpu_sc as plsc

FOX_HEADS = 8
FOX_HEAD_DIM = 64
RET_HEADS = 4
RET_KEY_DIM = 64
RET_VAL_DIM = 128
ROPE_BASE = 10000.0
N_EXPERTS = 32
TOP_K = 4
SWIGLU_LIMIT = 7.0
SWIGLU_ALPHA = 1.702
EPS = 1e-6

LANES = 128
FOX_W = FOX_HEADS * FOX_HEAD_DIM
RET_QK_W = RET_HEADS * RET_KEY_DIM
RET_V_W = RET_HEADS * RET_VAL_DIM
VMEM_LIMIT = 56 * 1024 * 1024
HIGH_HALF = -65536
LOG2_E = 1.4426950408889634
SC_WINDOW = 128
SC_COLS = 256

F32 = jnp.float32
BF16 = jnp.bfloat16


def _rms(x, g):
    return x * lax.rsqrt(jnp.mean(x * x, axis=-1, keepdims=True) + EPS) * g


def _sigmoid(x):
    return 1.0 / (1.0 + jnp.exp(-x))


def _dot(a, b):
    return jnp.dot(a, b, preferred_element_type=F32)


def _dot_nt(a, b):
    return lax.dot_general(a, b, (((1,), (1,)), ((), ())), preferred_element_type=F32)


def _pack_pairs(v):
    bits = pltpu.bitcast(v.astype(BF16).astype(F32), jnp.int32)
    half = v.shape[1] // 2
    return lax.shift_right_logical(bits[:, :half], 16) | (bits[:, half:] & HIGH_HALF)


def _unpack_pairs(chunks):
    return jnp.concatenate([pltpu.bitcast(lax.shift_left(w, 16), F32) for w in chunks]
                           + [pltpu.bitcast(w & HIGH_HALF, F32) for w in chunks], axis=1)


def _store_chunks(ref, v, rows=slice(None)):
    for c in range(ref.shape[0]):
        ref[c, rows, :] = v[:, c * SC_COLS:(c + 1) * SC_COLS]


def _interleave(generators):
    live = []
    pending = list(generators)
    while pending or live:
        if pending:
            live.append(pending.pop(0))
        for g in list(live):
            try:
                next(g)
            except StopIteration:
                live.remove(g)


def _params(n_axes):
    return pltpu.CompilerParams(dimension_semantics=("arbitrary",) * n_axes,
                                vmem_limit_bytes=VMEM_LIMIT)


C_MID = FOX_HEADS
C_LO = 2 * FOX_HEADS
C_ONE = 3 * FOX_HEADS
N_AUG = 3


def _split3(t):
    hi = t.astype(BF16).astype(F32)
    r = t - hi
    mid = r.astype(BF16).astype(F32)
    lo = (r - mid).astype(BF16).astype(F32)
    return hi + pltpu.roll(mid, C_MID, 1) + pltpu.roll(lo, C_LO, 1)


INPROJ_SUB = 256


def _inproj_kernel(x_ref, g_ref, w_ref, wvt_ref, wf_ref, bf_ref, cos_ref, sin_ref, pq_ref, pk_ref,
                   q_ref, k_ref, vt_ref, rq_ref, rk_ref, rv_ref, rg_ref, carry_scr, *, tiles_per_seq):
    i = pl.program_id(0)
    tm = x_ref.shape[0]
    sub = min(INPROJ_SUB, tm)

    @pl.when(i % tiles_per_seq == 0)
    def _():
        carry_scr[...] = jnp.zeros(carry_scr.shape, F32)

    lane = lax.broadcasted_iota(jnp.int32, (sub, LANES), 1)
    r = lax.broadcasted_iota(jnp.int32, (sub, sub), 0)
    c = lax.broadcasted_iota(jnp.int32, (sub, sub), 1)
    tri = jnp.where(c <= r, 1.0, 0.0).astype(BF16)
    own = lane < FOX_HEAD_DIM
    half = RET_KEY_DIM // 2
    lane_r = lax.broadcasted_iota(jnp.int32, (sub, RET_QK_W), 1)
    first = (lane_r % RET_KEY_DIM) < half
    carries = [carry_scr[...]]

    def sub_tile(t):
        rows = slice(t * sub, (t + 1) * sub)
        xn = _rms(x_ref[rows, :], g_ref[...]).astype(BF16)
        yield
        u = _dot(xn, w_ref[...])
        vt_ref[:, rows] = _dot_nt(wvt_ref[...], xn).astype(BF16)
        z = _dot(xn, wf_ref[...]) + bf_ref[...]
        yield
        lf = jnp.where(lane < FOX_HEADS, jnp.minimum(z, 0.0) - jnp.log1p(jnp.exp(-jnp.abs(z))), 0.0)
        ps = _dot(tri, _split3(lf).astype(BF16))
        yield
        assert len(carries) == t + 1
        cum = ps + pltpu.roll(ps, LANES - C_MID, 1) + pltpu.roll(ps, LANES - C_LO, 1)
        cum = jnp.where(lane < FOX_HEADS, cum, 0.0) + carries[t]
        carries.append(cum[sub - 1:sub, :])
        c3 = (_split3(cum * LOG2_E) + jnp.where(lane == C_ONE, 1.0, 0.0)).astype(BF16)
        aug_q = _dot(c3, pq_ref[...])
        aug_k = _dot(c3, pk_ref[...])

        o = 2 * FOX_W
        rq = u[:, o:o + RET_QK_W]; o += RET_QK_W
        rk = u[:, o:o + RET_QK_W]; o += RET_QK_W
        rv_ref[rows, :] = u[:, o:o + RET_V_W].astype(BF16); o += RET_V_W
        rg_ref[rows, :] = u[:, o:o + RET_V_W]

        cos = cos_ref[rows, :]
        sin = sin_ref[rows, :]

        def rot(v):
            partner = jnp.where(first, -pltpu.roll(v, RET_QK_W - half, 1), pltpu.roll(v, half, 1))
            return v * cos + partner * sin

        rq_ref[rows, :] = rot(rq).astype(BF16)
        rk_ref[rows, :] = (rot(rk) * (RET_KEY_DIM ** -0.5)).astype(BF16)
        yield
        fq = u[:, 0:FOX_W] * (FOX_HEAD_DIM ** -0.5 * LOG2_E)
        fk = u[:, FOX_W:2 * FOX_W]
        for src, aug, dst in ((fq, aug_q, q_ref), (fk, aug_k, k_ref)):
            for j in range(FOX_W // LANES):
                blk = src[:, j * LANES:(j + 1) * LANES]
                e0, e1 = 2 * j * LANES, (2 * j + 1) * LANES
                dst[rows, e0:e0 + LANES] = jnp.where(own, blk, aug[:, e0:e0 + LANES]).astype(BF16)
                dst[rows, e1:e1 + LANES] = jnp.where(own, pltpu.roll(blk, FOX_HEAD_DIM, 1),
                                                      aug[:, e1:e1 + LANES]).astype(BF16)

    _interleave(sub_tile(t) for t in range(tm // sub))
    carry_scr[...] = carries[-1]


def _placement():
    pq = np.zeros((LANES, FOX_HEADS * LANES), np.float32)
    pk = np.zeros((LANES, FOX_HEADS * LANES), np.float32)
    for h in range(FOX_HEADS):
        base = h * LANES + FOX_HEAD_DIM
        for part, src in enumerate((h, C_MID + h, C_LO + h)):
            pq[C_ONE, base + part] = 1.0
            pq[src, base + N_AUG + part] = 1.0
            pk[src, base + part] = -1.0
            pk[C_ONE, base + N_AUG + part] = 1.0
    return jnp.asarray(pq, BF16), jnp.asarray(pk, BF16)


def _inproj(x2, g, w_main, w_vt, w_f, b_f, cos_t, sin_t, seq, tm):
    n, d = x2.shape
    wn = w_main.shape[1]
    spt = seq // tm
    aw = FOX_HEADS * LANES
    pq, pk = _placement()
    row = lambda i: (i, 0)
    const = lambda i: (0, 0)
    pos = lambda i: (i % spt, 0)
    out_shape = [
        jax.ShapeDtypeStruct((n, aw), BF16), jax.ShapeDtypeStruct((n, aw), BF16),
        jax.ShapeDtypeStruct((FOX_W, n), BF16),
        jax.ShapeDtypeStruct((n, RET_QK_W), BF16), jax.ShapeDtypeStruct((n, RET_QK_W), BF16),
        jax.ShapeDtypeStruct((n, RET_V_W), BF16), jax.ShapeDtypeStruct((n, RET_V_W), F32),
    ]
    out_specs = [
        pl.BlockSpec((tm, aw), row), pl.BlockSpec((tm, aw), row),
        pl.BlockSpec((FOX_W, tm), lambda i: (0, i)),
        pl.BlockSpec((tm, RET_QK_W), row), pl.BlockSpec((tm, RET_QK_W), row),
        pl.BlockSpec((tm, RET_V_W), row), pl.BlockSpec((tm, RET_V_W), row),
    ]
    return pl.pallas_call(
        functools.partial(_inproj_kernel, tiles_per_seq=spt),
        grid=(n // tm,),
        in_specs=[
            pl.BlockSpec((tm, d), row), pl.BlockSpec((1, d), const),
            pl.BlockSpec((d, wn), const), pl.BlockSpec((FOX_W, d), const),
            pl.BlockSpec((d, LANES), const), pl.BlockSpec((1, LANES), const),
            pl.BlockSpec((tm, RET_QK_W), pos), pl.BlockSpec((tm, RET_QK_W), pos),
            pl.BlockSpec((LANES, aw), const), pl.BlockSpec((LANES, aw), const),
        ],
        out_specs=out_specs,
        out_shape=out_shape,
        scratch_shapes=[pltpu.VMEM((1, LANES), F32)],
        compiler_params=_params(1),
        name="inproj",
    )(x2, g, w_main, w_vt, w_f, b_f, cos_t, sin_t, pq, pk)


def _fox_kernel(q_ref, k_ref, vt_ref, o_ref, *, tq):
    seq = q_ref.shape[0]
    key = lax.broadcasted_iota(jnp.int32, (tq, tq), 0)
    qry = lax.broadcasted_iota(jnp.int32, (tq, tq), 1)
    vrow = lax.broadcasted_iota(jnp.int32, (LANES, tq), 0)
    one = jnp.ones((), BF16)
    items = [(qi, j) for qi in range(seq // tq) for j in range(qi + 1)]

    def logits(qi, j):
        out = []
        for a in range(2):
            sl = slice(a * LANES, (a + 1) * LANES)
            s = _dot_nt(k_ref[j * tq:(j + 1) * tq, sl], q_ref[qi * tq:(qi + 1) * tq, sl])
            out.append(jnp.where(key <= qry, s, -jnp.inf) if j == qi else s)
        return out

    s_cur = logits(*items[0])
    carry = None
    for w, (qi, j) in enumerate(items):
        s_next = logits(*items[w + 1]) if w + 1 < len(items) else None
        if j == 0:
            carry = [(jnp.full((1, tq), -jnp.inf, F32), jnp.zeros((LANES, tq), F32)) for _ in range(2)]
        v = vt_ref[:, j * tq:(j + 1) * tq]
        for a in range(2):
            m_prev, acc = carry[a]
            m_new = jnp.maximum(m_prev, jnp.max(s_cur[a], axis=0, keepdims=True))
            alpha = jnp.exp2(m_prev - m_new)
            p = jnp.exp2(s_cur[a] - m_new).astype(BF16)
            va = jnp.where((vrow // FOX_HEAD_DIM) == a, v, one)
            carry[a] = (m_new, alpha * acc + _dot(va, p))
        if j == qi:
            acc0, acc1 = carry[0][1], carry[1][1]
            ot = jnp.where(vrow < FOX_HEAD_DIM, acc0 / acc0[FOX_HEAD_DIM:FOX_HEAD_DIM + 1, :],
                           acc1 / acc1[0:1, :])
            o_ref[qi * tq:(qi + 1) * tq, :] = ot.T.astype(o_ref.dtype)
        s_cur = s_next


def _fox(q_aug, k_aug, v_t, batch, seq, tq):
    n = q_aug.shape[0]
    pairs = FOX_HEADS // 2
    kern = functools.partial(_fox_kernel, tq=tq)
    return pl.pallas_call(
        kern,
        grid=(batch, pairs),
        in_specs=[
            pl.BlockSpec((seq, 2 * LANES), lambda i, h: (i, h)),
            pl.BlockSpec((seq, 2 * LANES), lambda i, h: (i, h)),
            pl.BlockSpec((LANES, seq), lambda i, h: (h, i)),
        ],
        out_specs=pl.BlockSpec((seq, LANES), lambda i, h: (i, h)),
        out_shape=jax.ShapeDtypeStruct((n, FOX_W), BF16),
        compiler_params=_params(2),
        name="fox_attention",
    )(q_aug, k_aug, v_t)


def _ret_kernel(lg_ref, q_ref, k_ref, v_ref, g_ref, o_ref, *, chunk):
    s = q_ref.shape[1]
    lane = lax.broadcasted_iota(jnp.int32, (chunk, LANES), 1)
    ri = lax.broadcasted_iota(jnp.int32, (chunk, chunk), 0)
    ci = lax.broadcasted_iota(jnp.int32, (chunk, chunk), 1)
    diff = (ri - ci).astype(F32)
    pos = lax.broadcasted_iota(jnp.int32, (chunk, 1), 0).astype(F32)

    def head(h):
        lg = lg_ref[h]
        mine = (lane // RET_KEY_DIM) == (h % 2)
        qk = slice((h // 2) * LANES, (h // 2 + 1) * LANES)
        vs = slice(h * RET_VAL_DIM, (h + 1) * RET_VAL_DIM)
        inner = jnp.where(diff >= 0, jnp.exp(jnp.maximum(diff, 0.0) * lg), 0.0)
        q_decay = jnp.exp((pos + 1.0) * lg)
        k_decay = jnp.exp((chunk - 1.0 - pos) * lg)
        chunk_decay = jnp.exp(jnp.full((1, 1), chunk, F32) * lg)
        state = jnp.zeros((LANES, RET_VAL_DIM), F32)
        for c in range(s // chunk):
            rows = slice(c * chunk, (c + 1) * chunk)
            q = jnp.where(mine, q_ref[0, rows, qk], jnp.zeros((), BF16))
            k = jnp.where(mine, k_ref[0, rows, qk], jnp.zeros((), BF16))
            v = v_ref[0, rows, vs]
            scores = _dot_nt(q, k)
            inter = _dot(q, state.astype(BF16))
            kd = (k.astype(F32) * k_decay).astype(BF16)
            update = lax.dot_general(kd, v, (((0,), (0,)), ((), ())), preferred_element_type=F32)
            yield
            o = _dot((scores * inner).astype(BF16), v) + inter * q_decay
            state = state * chunk_decay + update
            yield
            o = o * lax.rsqrt(jnp.mean(o * o, axis=-1, keepdims=True) + EPS)
            g = g_ref[0, rows, vs]
            o_ref[0, rows, vs] = (o * (g * _sigmoid(g))).astype(o_ref.dtype)

    _interleave(head(h) for h in range(RET_HEADS))


def _retention(lg, rq, rk, rv, rg, chunk):
    b, s, _ = rq.shape
    kern = functools.partial(_ret_kernel, chunk=chunk)
    qk_spec = pl.BlockSpec((1, s, RET_QK_W), lambda i, lg_ref: (i, 0, 0))
    v_spec = pl.BlockSpec((1, s, RET_V_W), lambda i, lg_ref: (i, 0, 0))
    return pl.pallas_call(
        kern,
        grid_spec=pltpu.PrefetchScalarGridSpec(
            num_scalar_prefetch=1,
            grid=(b,),
            in_specs=[qk_spec, qk_spec, v_spec, v_spec],
            out_specs=v_spec,
        ),
        out_shape=jax.ShapeDtypeStruct((b, s, RET_V_W), BF16),
        compiler_params=_params(1),
        name="retention",
    )(lg, rq, rk, rv, rg)


MERGE_SUB = 256


def _merge_kernel(x_ref, fox_ref, ro_ref, gmix_ref, wmg_ref, bmg_ref, wb_ref, wout_ref, gffn_ref,
                  wr_ref, br_ref, h1_ref, hn_ref, rf_ref, ri_ref, cnt_ref, base_scr):
    i = pl.program_id(0)
    tm, d = x_ref.shape
    sub = min(MERGE_SUB, tm)

    @pl.when(i == 0)
    def _():
        base_scr[...] = jnp.zeros(base_scr.shape, F32)

    lane = lax.broadcasted_iota(jnp.int32, (sub, LANES), 1)
    lane_f = lane.astype(F32)
    r = lax.broadcasted_iota(jnp.int32, (sub, sub), 0)
    c = lax.broadcasted_iota(jnp.int32, (sub, sub), 1)
    tri = jnp.where(c < r, 1.0, 0.0).astype(BF16)
    counts = [base_scr[...]]

    def sub_tile(t):
        rows = slice(t * sub, (t + 1) * sub)
        x = x_ref[rows, :]
        xn = _rms(x, gmix_ref[...]).astype(BF16)
        yield
        pre = _dot(xn, wmg_ref[...])
        yield
        gate = _sigmoid(pre + bmg_ref[...])
        yield
        pf = _dot(fox_ref[rows, :], wb_ref[0])
        pr = _dot(ro_ref[rows, :], wb_ref[1])
        yield
        merged = (gate[:, :d] * pf + gate[:, d:] * pr).astype(BF16)
        yield
        h1 = x + _dot(merged, wout_ref[...])
        yield
        h1_ref[rows, :] = h1
        hn = _rms(h1, gffn_ref[...]).astype(BF16)
        _store_chunks(hn_ref, _pack_pairs(hn), rows)
        yield
        logits = _dot(hn, wr_ref[...]) + br_ref[...]
        yield
        vals, idxs = [], []
        cur = logits
        for _ in range(TOP_K):
            m = jnp.max(cur, axis=-1, keepdims=True)
            idx = jnp.min(jnp.where(cur == m, lane_f, float(LANES)), axis=-1, keepdims=True)
            vals.append(m)
            idxs.append(idx)
            cur = jnp.where(lane_f == idx, -jnp.inf, cur)
        exps = [jnp.exp(v - vals[0]) for v in vals]
        denom = exps[0] + exps[1] + exps[2] + exps[3]
        onehot = jnp.zeros(logits.shape, F32)
        for idx in idxs:
            onehot = onehot + jnp.where(lane_f == idx, 1.0, 0.0)
        yield
        assert len(counts) == t + 1
        before = _dot(tri, onehot.astype(BF16)) + counts[t]
        counts.append(counts[t] + jnp.sum(onehot, axis=0, keepdims=True))
        rf = jnp.zeros(logits.shape, F32)
        ri = jnp.zeros(logits.shape, F32)
        for j in range(TOP_K):
            rank = jnp.sum(jnp.where(lane_f == idxs[j], before, 0.0), axis=-1, keepdims=True)
            rf = jnp.where(lane == j, exps[j] / denom, rf)
            ri = jnp.where(lane == j, idxs[j], ri)
            ri = jnp.where(lane == TOP_K + j, rank, ri)
        rf_ref[rows, :] = rf
        ri_ref[rows, :] = ri.astype(jnp.int32)

    _interleave(sub_tile(t) for t in range(tm // sub))
    base_scr[...] = counts[-1]
    cnt_ref[...] = counts[-1]


def _merge(x2, fox, ro, gmix, wmg, bmg, wb, wout, gffn, wr, br, tm, first_tile, n):
    d = x2.shape[1]
    row = lambda i: (i, 0)
    src = lambda i: (i + first_tile, 0)
    const = lambda i: (0, 0)
    return pl.pallas_call(
        _merge_kernel,
        grid=(n // tm,),
        in_specs=[
            pl.BlockSpec((tm, d), src), pl.BlockSpec((tm, FOX_W), src), pl.BlockSpec((tm, RET_V_W), src),
            pl.BlockSpec((1, d), const), pl.BlockSpec((d, 2 * d), const), pl.BlockSpec((1, 2 * d), const),
            pl.BlockSpec((2, FOX_W, d), lambda i: (0, 0, 0)), pl.BlockSpec((d, d), const),
            pl.BlockSpec((1, d), const), pl.BlockSpec((d, LANES), const), pl.BlockSpec((1, LANES), const),
        ],
        out_specs=[
            pl.BlockSpec((tm, d), row), pl.BlockSpec((d // 2 // SC_COLS, tm, SC_COLS), lambda i: (0, i, 0)),
            pl.BlockSpec((tm, LANES), row), pl.BlockSpec((tm, LANES), row),
            pl.BlockSpec((1, LANES), const),
        ],
        out_shape=[
            jax.ShapeDtypeStruct((n, d), F32), jax.ShapeDtypeStruct((d // 2 // SC_COLS, n, SC_COLS), jnp.int32),
            jax.ShapeDtypeStruct((n, LANES), F32), jax.ShapeDtypeStruct((n, LANES), jnp.int32),
            jax.ShapeDtypeStruct((1, LANES), F32),
        ],
        scratch_shapes=[pltpu.VMEM((1, LANES), F32)],
        compiler_params=_params(1),
        name="merge_router",
    )(x2, fox, ro, gmix, wmg, bmg, wb, wout, gffn, wr, br)


def _sc_mesh():
    return plsc.VectorSubcoreMesh(core_axis_name="core", subcore_axis_name="subcore")


def _sc_dispatch(rows, dest_t, n_out):
    chunks, n, w = rows.shape

    @functools.partial(pl.kernel, out_type=jax.ShapeDtypeStruct((chunks, n_out, w), rows.dtype),
                       mesh=_sc_mesh(), scratch_types=[], name="moe_dispatch")
    def run(x_hbm, i_hbm, o_hbm):
        for c in range(chunks):
            def body(x_vmem, i_vmem, c=c):
                for j in range(TOP_K):
                    pltpu.sync_copy(x_vmem, o_hbm.at[c].at[i_vmem.at[j]])

            pltpu.emit_pipeline(
                body,
                grid=(n // SC_WINDOW,),
                in_specs=[pl.BlockSpec((SC_WINDOW, w), lambda i: (i, 0)),
                          pl.BlockSpec((TOP_K, SC_WINDOW), lambda i: (0, i))],
                out_specs=[],
                core_axis_name=("core", "subcore"),
                dimension_semantics=(pltpu.PARALLEL,),
            )(x_hbm.at[c], i_hbm)

    return run(rows, dest_t)


def _sc_gather(table, idx):
    chunks, _, w = table.shape
    m = idx.shape[1]

    @functools.partial(pl.kernel, out_type=jax.ShapeDtypeStruct((chunks, m, w), table.dtype),
                       mesh=_sc_mesh(), scratch_types=[], name="moe_combine_gather")
    def run(t_hbm, i_hbm, o_hbm):
        for c in range(chunks):
            def body(i_vmem, o_vmem, c=c):
                pltpu.sync_copy(t_hbm.at[c].at[i_vmem.at[0]], o_vmem)

            pltpu.emit_pipeline(
                body,
                grid=(m // SC_WINDOW,),
                in_specs=[pl.BlockSpec((1, SC_WINDOW), lambda i: (0, i))],
                out_specs=[pl.BlockSpec((SC_WINDOW, w), lambda i: (i, 0))],
                core_axis_name=("core", "subcore"),
                dimension_semantics=(pltpu.PARALLEL,),
            )(i_hbm, o_hbm.at[c])

    return run(table, idx)


GLU_GROUP = 2 * LANES
EXPERT_SUB = 256


def _expert_kernel(be_ref, valid_ref, x_ref, wgu_ref, bgu_ref, wd_ref, bd_ref, perm_ref, y_ref, wgu_scr, wd_scr):
    i = pl.program_id(0)
    f2 = wgu_ref.shape[2]
    bm = x_ref.shape[1]
    sub = min(EXPERT_SUB, bm)
    valid = valid_ref[i]
    live = valid > 0

    @pl.when(jnp.logical_and(live, jnp.logical_or(i == 0, be_ref[i] != be_ref[jnp.maximum(i - 1, 0)])))
    def _():
        for b in range(f2 // GLU_GROUP):
            cols = slice(b * GLU_GROUP, (b + 1) * GLU_GROUP)
            wgu_scr[:, cols] = _dot(wgu_ref[0, :, cols].astype(BF16), perm_ref[...]).astype(BF16)
        wd_scr[...] = wd_ref[0].astype(BF16)

    def sub_block(t, sub):
        rows = slice(t * sub, (t + 1) * sub)
        x = _unpack_pairs([x_ref[c, rows, :] for c in range(x_ref.shape[0])]).astype(BF16)
        yield
        gu = _dot(x, wgu_scr[...])
        yield
        gu = gu + bgu_ref[0]
        acts = []
        for b in range(f2 // GLU_GROUP):
            glu = jnp.minimum(gu[:, b * GLU_GROUP:b * GLU_GROUP + LANES], SWIGLU_LIMIT)
            lin = jnp.clip(gu[:, b * GLU_GROUP + LANES:(b + 1) * GLU_GROUP], -SWIGLU_LIMIT, SWIGLU_LIMIT)
            acts.append((glu * _sigmoid(SWIGLU_ALPHA * glu) * (lin + 1.0)).astype(BF16))
        act = jnp.concatenate(acts, axis=1)
        yield
        y = _dot(act, wd_scr[...])
        yield
        _store_chunks(y_ref, _pack_pairs(y + bd_ref[0]), rows)

    @pl.when(valid == bm)
    def _():
        _interleave(sub_block(t, sub) for t in range(bm // sub))

    for t in range(bm // sub):
        @pl.when(jnp.logical_and(valid < bm, valid > t * sub))
        def _(t=t):
            _interleave([sub_block(t, sub)])


def _experts(block_e, valid, x_buf, wgu, bgu, wd, bd, bm):
    p = x_buf.shape[1]
    f, d = wd.shape[1:]
    perm = np.zeros((GLU_GROUP, GLU_GROUP), np.float32)
    for c in range(LANES):
        perm[2 * c, c] = 1.0
        perm[2 * c + 1, LANES + c] = 1.0
    return pl.pallas_call(
        _expert_kernel,
        grid_spec=pltpu.PrefetchScalarGridSpec(
            num_scalar_prefetch=2,
            grid=(p // bm,),
            in_specs=[
                pl.BlockSpec((d // 2 // SC_COLS, bm, SC_COLS), lambda i, be, nu: (0, i, 0)),
                pl.BlockSpec((1, d, 2 * f), lambda i, be, nu: (be[i], 0, 0)),
                pl.BlockSpec((1, 1, 2 * f), lambda i, be, nu: (be[i], 0, 0)),
                pl.BlockSpec((1, f, d), lambda i, be, nu: (be[i], 0, 0)),
                pl.BlockSpec((1, 1, d), lambda i, be, nu: (be[i], 0, 0)),
                pl.BlockSpec((GLU_GROUP, GLU_GROUP), lambda i, be, nu: (0, 0)),
            ],
            out_specs=pl.BlockSpec((d // 2 // SC_COLS, bm, SC_COLS), lambda i, be, nu: (0, i, 0)),
            scratch_shapes=[pltpu.VMEM((d, 2 * f), BF16), pltpu.VMEM((f, d), BF16)],
        ),
        out_shape=jax.ShapeDtypeStruct((d // 2 // SC_COLS, p, SC_COLS), jnp.int32),
        compiler_params=_params(1),
        name="expert_ffn",
    )(block_e, valid, x_buf, wgu, bgu, wd, bd, jnp.asarray(perm, BF16))


FINAL_SUB = 256


def _final_kernel(h1_ref, yg_ref, rf_ref, p_ref, gple_ref, wpg_ref, wpp_ref, gfin_ref, *rest):
    o_ref = rest[-1]
    tm = h1_ref.shape[0]
    sub = min(FINAL_SUB, tm)

    def sub_tile(t):
        rows = slice(t * sub, (t + 1) * sub)
        rf = rf_ref[rows, :]
        h2 = h1_ref[rows, :]
        for j in range(TOP_K):
            h2 = h2 + _unpack_pairs([yg_ref[c, j, rows, :] for c in range(yg_ref.shape[0])]) * rf[:, j:j + 1]
        hn = _rms(h2, gple_ref[...]).astype(BF16)
        yield
        pre = _dot(hn, wpg_ref[...])
        proj = _dot(p_ref[rows, :].astype(BF16), wpp_ref[...])
        yield
        h3 = h2 + _sigmoid(pre) * proj
        o_ref[rows, :] = _rms(h3, gfin_ref[...])

    _interleave(sub_tile(t) for t in range(tm // sub))


def _final(h1, yg, rf, p2, gple, wpg, wpp, gfin, tm, first_tile, prev_out):
    n_total, d = h1.shape
    n = yg.shape[2]
    pd = p2.shape[1]
    dst = lambda i: (i + first_tile, 0)
    const = lambda i: (0, 0)
    in_specs = [
        pl.BlockSpec((tm, d), dst), pl.BlockSpec((d // 2 // SC_COLS, TOP_K, tm, SC_COLS), lambda i: (0, 0, i, 0)),
        pl.BlockSpec((tm, LANES), dst), pl.BlockSpec((tm, pd), dst),
        pl.BlockSpec((1, d), const), pl.BlockSpec((d, d), const), pl.BlockSpec((pd, d), const),
        pl.BlockSpec((1, d), const),
    ]
    args = [h1, yg, rf, p2, gple, wpg, wpp, gfin]
    aliases = {}
    if prev_out is not None:
        in_specs.append(pl.BlockSpec(memory_space=pl.ANY))
        aliases = {len(args): 0}
        args.append(prev_out)
    return pl.pallas_call(
        _final_kernel,
        grid=(n // tm,),
        in_specs=in_specs,
        out_specs=pl.BlockSpec((tm, d), dst),
        out_shape=jax.ShapeDtypeStruct((n_total, d), F32),
        input_output_aliases=aliases,
        compiler_params=_params(1),
        name="combine_ple_norm",
    )(*args)


def _layer(h, p, mix_norm, w_in, b_forget, w_branch, w_merge_gate, b_merge_gate, w_out, ffn_norm,
           w_router, b_router, w_gate_up, b_gate_up, w_down, b_down, ple_norm, w_ple_gate, w_ple_proj,
           final_norm, *, tm, tq, chunk, bm, groups):
    b, s, d = h.shape
    n = b * s
    x2 = h.reshape(n, d)
    row = lambda t: t.reshape(1, -1)

    c0 = 3 * FOX_W
    w_main = jnp.concatenate([w_in[:, :2 * FOX_W], w_in[:, c0 + FOX_HEADS:]], axis=1).astype(BF16)
    w_vt = w_in[:, 2 * FOX_W:c0].T.astype(BF16)
    w_f = jnp.pad(w_in[:, c0:c0 + FOX_HEADS], ((0, 0), (0, LANES - FOX_HEADS))).astype(BF16)
    b_f = jnp.pad(b_forget, (0, LANES - FOX_HEADS)).reshape(1, LANES)
    half = RET_KEY_DIM // 2
    inv = ROPE_BASE ** (-jnp.arange(half, dtype=F32) / half)
    ang = jnp.arange(s).astype(F32)[:, None] * inv[None, :]
    cos_t = jnp.tile(jnp.cos(ang), (1, RET_QK_W // half))
    sin_t = jnp.tile(jnp.sin(ang), (1, RET_QK_W // half))

    q_aug, k_aug, v_t, rq, rk, rv, rg = _inproj(x2, row(mix_norm), w_main, w_vt, w_f, b_f, cos_t, sin_t, s, tm)
    fox = _fox(q_aug, k_aug, v_t, b, s, tq)

    lg = jnp.log1p(-jnp.exp2(-5.0 - jnp.arange(RET_HEADS, dtype=F32)))
    ro = _retention(lg, rq.reshape(b, s, RET_QK_W), rk.reshape(b, s, RET_QK_W),
                    rv.reshape(b, s, RET_V_W), rg.reshape(b, s, RET_V_W), chunk)

    w_r = jnp.pad(w_router, ((0, 0), (0, LANES - N_EXPERTS))).astype(BF16)
    b_r = jnp.concatenate([b_router, jnp.full((LANES - N_EXPERTS,), -1e30, F32)]).reshape(1, LANES)
    ne, dd, f2 = w_gate_up.shape
    bgu = b_gate_up.reshape(ne, f2 // GLU_GROUP, LANES, 2).transpose(0, 1, 3, 2).reshape(ne, 1, f2)
    merge_w = (row(mix_norm), w_merge_gate.astype(BF16), row(b_merge_gate), w_branch.astype(BF16),
               w_out.astype(BF16), row(ffn_norm), w_r, b_r)
    final_w = (row(ple_norm), w_ple_gate.astype(BF16), w_ple_proj.astype(BF16), row(final_norm))
    p2 = p.reshape(n, -1)

    h1, hn, rf, ri, cnt = _merge(x2, fox, ro.reshape(n, RET_V_W), *merge_w, tm, 0, n)

    a = n * TOP_K
    nb = -(-(a + N_EXPERTS * (bm - 1)) // bm)
    e_idx = ri[:, :TOP_K]
    rank = ri[:, TOP_K:2 * TOP_K]
    counts = cnt[0, :N_EXPERTS].astype(jnp.int32)
    padded = (counts + bm - 1) // bm * bm
    pad_end = jnp.cumsum(padded)
    pad_start = pad_end - padded
    dest_t = (pad_start[e_idx] + rank).T
    block_start = jnp.arange(nb, dtype=jnp.int32) * bm
    block_e = jnp.minimum(jnp.sum(block_start[:, None] >= pad_end[None, :], axis=1),
                          N_EXPERTS - 1).astype(jnp.int32)
    valid = jnp.clip((pad_start + counts)[block_e] - block_start, 0, bm).astype(jnp.int32)

    x_buf = _sc_dispatch(hn, dest_t, nb * bm)
    y_buf = _experts(block_e, valid, x_buf, w_gate_up, bgu, w_down, b_down.reshape(ne, 1, dd), bm)

    ng = n // groups
    out = None
    for g in range(groups):
        idx = dest_t[:, g * ng:(g + 1) * ng].reshape(1, ng * TOP_K)
        yg = _sc_gather(y_buf, idx).reshape(d // 2 // SC_COLS, TOP_K, ng, SC_COLS)
        out = _final(h1, yg, rf, p2, *final_w, tm, g * (ng // tm), out)
    return out.reshape(b, s, d)


def kernel(x, p, mix_norm, w_in, b_forget, w_branch, w_merge_gate, b_merge_gate, w_out, ffn_norm, w_router,
           b_router, w_gate_up, b_gate_up, w_down, b_down, ple_norm, w_ple_gate, w_ple_proj, final_norm):
    depth = p.shape[0]
    assert depth == 1, "the final norm is fused into the (single) layer"
    return _layer(x, p[0], mix_norm[0], w_in[0], b_forget[0], w_branch[0], w_merge_gate[0], b_merge_gate[0],
                  w_out[0], ffn_norm[0], w_router[0], b_router[0], w_gate_up[0], b_gate_up[0], w_down[0],
                  b_down[0], ple_norm[0], w_ple_gate[0], w_ple_proj[0], final_norm,
                  tm=512, tq=512, chunk=256, bm=1024, groups=4)
```

```python
import functools

import numpy as np
import jax
import jax.numpy as jnp
from jax import lax
from jax.experimental import pallas as pl
from jax.experimental.pallas import tpu as pltpu
from jax.experimental.pallas import tpu_sc as plsc

FOX_HEADS = 8
FOX_HEAD_DIM = 64
RET_HEADS = 4
RET_KEY_DIM = 64
RET_VAL_DIM = 128
ROPE_BASE = 10000.0
N_EXPERTS = 32
TOP_K = 4
SWIGLU_LIMIT = 7.0
SWIGLU_ALPHA = 1.702
EPS = 1e-6

LANES = 128
FOX_W = FOX_HEADS * FOX_HEAD_DIM
RET_QK_W = RET_HEADS * RET_KEY_DIM
RET_V_W = RET_HEADS * RET_VAL_DIM
VMEM_LIMIT = 56 * 1024 * 1024
HIGH_HALF = -65536
LOG2_E = 1.4426950408889634
SC_WINDOW = 128
SC_COLS = 256

F32 = jnp.float32
BF16 = jnp.bfloat16


def _rms(x, g):
    return x * lax.rsqrt(jnp.mean(x * x, axis=-1, keepdims=True) + EPS) * g


def _sigmoid(x):
    return 1.0 / (1.0 + jnp.exp(-x))


def _dot(a, b):
    return jnp.dot(a, b, preferred_element_type=F32)


def _dot_nt(a, b):
    return lax.dot_general(a, b, (((1,), (1,)), ((), ())), preferred_element_type=F32)


def _pack_pairs(v):
    bits = pltpu.bitcast(v.astype(BF16).astype(F32), jnp.int32)
    half = v.shape[1] // 2
    return lax.shift_right_logical(bits[:, :half], 16) | (bits[:, half:] & HIGH_HALF)


def _unpack_pairs(chunks):
    return jnp.concatenate([pltpu.bitcast(lax.shift_left(w, 16), F32) for w in chunks]
                           + [pltpu.bitcast(w & HIGH_HALF, F32) for w in chunks], axis=1)


def _store_chunks(ref, v, rows=slice(None)):
    for c in range(ref.shape[0]):
        ref[c, rows, :] = v[:, c * SC_COLS:(c + 1) * SC_COLS]


def _interleave(generators):
    live = []
    pending = list(generators)
    while pending or live:
        if pending:
            live.append(pending.pop(0))
        for g in list(live):
            try:
                next(g)
            except StopIteration:
                live.remove(g)


def _params(n_axes):
    return pltpu.CompilerParams(dimension_semantics=("arbitrary",) * n_axes,
                                vmem_limit_bytes=VMEM_LIMIT)


C_MID = FOX_HEADS
C_LO = 2 * FOX_HEADS
C_ONE = 3 * FOX_HEADS
N_AUG = 3


def _split3(t):
    hi = t.astype(BF16).astype(F32)
    r = t - hi
    mid = r.astype(BF16).astype(F32)
    lo = (r - mid).astype(BF16).astype(F32)
    return hi + pltpu.roll(mid, C_MID, 1) + pltpu.roll(lo, C_LO, 1)


INPROJ_SUB = 256


def _inproj_kernel(x_ref, g_ref, w_ref, wvt_ref, wf_ref, bf_ref, cos_ref, sin_ref, pq_ref, pk_ref,
                   q_ref, k_ref, vt_ref, rq_ref, rk_ref, rv_ref, rg_ref, carry_scr, *, tiles_per_seq):
    i = pl.program_id(0)
    tm = x_ref.shape[0]
    sub = min(INPROJ_SUB, tm)

    @pl.when(i % tiles_per_seq == 0)
    def _():
        carry_scr[...] = jnp.zeros(carry_scr.shape, F32)

    lane = lax.broadcasted_iota(jnp.int32, (sub, LANES), 1)
    r = lax.broadcasted_iota(jnp.int32, (sub, sub), 0)
    c = lax.broadcasted_iota(jnp.int32, (sub, sub), 1)
    tri = jnp.where(c <= r, 1.0, 0.0).astype(BF16)
    own = lane < FOX_HEAD_DIM
    half = RET_KEY_DIM // 2
    lane_r = lax.broadcasted_iota(jnp.int32, (sub, RET_QK_W), 1)
    first = (lane_r % RET_KEY_DIM) < half
    carries = [carry_scr[...]]

    def sub_tile(t):
        rows = slice(t * sub, (t + 1) * sub)
        xn = _rms(x_ref[rows, :], g_ref[...]).astype(BF16)
        yield
        u = _dot(xn, w_ref[...])
        vt_ref[:, rows] = _dot_nt(wvt_ref[...], xn).astype(BF16)
        z = _dot(xn, wf_ref[...]) + bf_ref[...]
        yield
        lf = jnp.where(lane < FOX_HEADS, jnp.minimum(z, 0.0) - jnp.log1p(jnp.exp(-jnp.abs(z))), 0.0)
        ps = _dot(tri, _split3(lf).astype(BF16))
        yield
        assert len(carries) == t + 1
        cum = ps + pltpu.roll(ps, LANES - C_MID, 1) + pltpu.roll(ps, LANES - C_LO, 1)
        cum = jnp.where(lane < FOX_HEADS, cum, 0.0) + carries[t]
        carries.append(cum[sub - 1:sub, :])
        c3 = (_split3(cum * LOG2_E) + jnp.where(lane == C_ONE, 1.0, 0.0)).astype(BF16)
        aug_q = _dot(c3, pq_ref[...])
        aug_k = _dot(c3, pk_ref[...])

        o = 2 * FOX_W
        rq = u[:, o:o + RET_QK_W]; o += RET_QK_W
        rk = u[:, o:o + RET_QK_W]; o += RET_QK_W
        rv_ref[rows, :] = u[:, o:o + RET_V_W].astype(BF16); o += RET_V_W
        rg_ref[rows, :] = u[:, o:o + RET_V_W]

        cos = cos_ref[rows, :]
        sin = sin_ref[rows, :]

        def rot(v):
            partner = jnp.where(first, -pltpu.roll(v, RET_QK_W - half, 1), pltpu.roll(v, half, 1))
            return v * cos + partner * sin

        rq_ref[rows, :] = rot(rq).astype(BF16)
        rk_ref[rows, :] = (rot(rk) * (RET_KEY_DIM ** -0.5)).astype(BF16)
        yield
        fq = u[:, 0:FOX_W] * (FOX_HEAD_DIM ** -0.5 * LOG2_E)
        fk = u[:, FOX_W:2 * FOX_W]
        for src, aug, dst in ((fq, aug_q, q_ref), (fk, aug_k, k_ref)):
            for j in range(FOX_W // LANES):
                blk = src[:, j * LANES:(j + 1) * LANES]
                e0, e1 = 2 * j * LANES, (2 * j + 1) * LANES
                dst[rows, e0:e0 + LANES] = jnp.where(own, blk, aug[:, e0:e0 + LANES]).astype(BF16)
                dst[rows, e1:e1 + LANES] = jnp.where(own, pltpu.roll(blk, FOX_HEAD_DIM, 1),
                                                      aug[:, e1:e1 + LANES]).astype(BF16)

    _interleave(sub_tile(t) for t in range(tm // sub))
    carry_scr[...] = carries[-1]


def _placement():
    pq = np.zeros((LANES, FOX_HEADS * LANES), np.float32)
    pk = np.zeros((LANES, FOX_HEADS * LANES), np.float32)
    for h in range(FOX_HEADS):
        base = h * LANES + FOX_HEAD_DIM
        for part, src in enumerate((h, C_MID + h, C_LO + h)):
            pq[C_ONE, base + part] = 1.0
            pq[src, base + N_AUG + part] = 1.0
            pk[src, base + part] = -1.0
            pk[C_ONE, base + N_AUG + part] = 1.0
    return jnp.asarray(pq, BF16), jnp.asarray(pk, BF16)


def _inproj(x2, g, w_main, w_vt, w_f, b_f, cos_t, sin_t, seq, tm):
    n, d = x2.shape
    wn = w_main.shape[1]
    spt = seq // tm
    aw = FOX_HEADS * LANES
    pq, pk = _placement()
    row = lambda i: (i, 0)
    const = lambda i: (0, 0)
    pos = lambda i: (i % spt, 0)
    out_shape = [
        jax.ShapeDtypeStruct((n, aw), BF16), jax.ShapeDtypeStruct((n, aw), BF16),
        jax.ShapeDtypeStruct((FOX_W, n), BF16),
        jax.ShapeDtypeStruct((n, RET_QK_W), BF16), jax.ShapeDtypeStruct((n, RET_QK_W), BF16),
        jax.ShapeDtypeStruct((n, RET_V_W), BF16), jax.ShapeDtypeStruct((n, RET_V_W), F32),
    ]
    out_specs = [
        pl.BlockSpec((tm, aw), row), pl.BlockSpec((tm, aw), row),
        pl.BlockSpec((FOX_W, tm), lambda i: (0, i)),
        pl.BlockSpec((tm, RET_QK_W), row), pl.BlockSpec((tm, RET_QK_W), row),
        pl.BlockSpec((tm, RET_V_W), row), pl.BlockSpec((tm, RET_V_W), row),
    ]
    return pl.pallas_call(
        functools.partial(_inproj_kernel, tiles_per_seq=spt),
        grid=(n // tm,),
        in_specs=[
            pl.BlockSpec((tm, d), row), pl.BlockSpec((1, d), const),
            pl.BlockSpec((d, wn), const), pl.BlockSpec((FOX_W, d), const),
            pl.BlockSpec((d, LANES), const), pl.BlockSpec((1, LANES), const),
            pl.BlockSpec((tm, RET_QK_W), pos), pl.BlockSpec((tm, RET_QK_W), pos),
            pl.BlockSpec((LANES, aw), const), pl.BlockSpec((LANES, aw), const),
        ],
        out_specs=out_specs,
        out_shape=out_shape,
        scratch_shapes=[pltpu.VMEM((1, LANES), F32)],
        compiler_params=_params(1),
        name="inproj",
    )(x2, g, w_main, w_vt, w_f, b_f, cos_t, sin_t, pq, pk)


def _fox_kernel(q_ref, k_ref, vt_ref, o_ref, *, tq):
    seq = q_ref.shape[0]
    key = lax.broadcasted_iota(jnp.int32, (tq, tq), 0)
    qry = lax.broadcasted_iota(jnp.int32, (tq, tq), 1)
    vrow = lax.broadcasted_iota(jnp.int32, (LANES, tq), 0)
    one = jnp.ones((), BF16)
    items = [(qi, j) for qi in range(seq // tq) for j in range(qi + 1)]

    def logits(qi, j):
        out = []
        for a in range(2):
            sl = slice(a * LANES, (a + 1) * LANES)
            s = _dot_nt(k_ref[j * tq:(j + 1) * tq, sl], q_ref[qi * tq:(qi + 1) * tq, sl])
            out.append(jnp.where(key <= qry, s, -jnp.inf) if j == qi else s)
        return out

    s_cur = logits(*items[0])
    carry = None
    for w, (qi, j) in enumerate(items):
        s_next = logits(*items[w + 1]) if w + 1 < len(items) else None
        if j == 0:
            carry = [(jnp.full((1, tq), -jnp.inf, F32), jnp.zeros((LANES, tq), F32)) for _ in range(2)]
        v = vt_ref[:, j * tq:(j + 1) * tq]
        for a in range(2):
            m_prev, acc = carry[a]
            m_new = jnp.maximum(m_prev, jnp.max(s_cur[a], axis=0, keepdims=True))
            alpha = jnp.exp2(m_prev - m_new)
            p = jnp.exp2(s_cur[a] - m_new).astype(BF16)
            va = jnp.where((vrow // FOX_HEAD_DIM) == a, v, one)
            carry[a] = (m_new, alpha * acc + _dot(va, p))
        if j == qi:
            acc0, acc1 = carry[0][1], carry[1][1]
            ot = jnp.where(vrow < FOX_HEAD_DIM, acc0 / acc0[FOX_HEAD_DIM:FOX_HEAD_DIM + 1, :],
                           acc1 / acc1[0:1, :])
            o_ref[qi * tq:(qi + 1) * tq, :] = ot.T.astype(o_ref.dtype)
        s_cur = s_next


def _fox(q_aug, k_aug, v_t, batch, seq, tq):
    n = q_aug.shape[0]
    pairs = FOX_HEADS // 2
    kern = functools.partial(_fox_kernel, tq=tq)
    return pl.pallas_call(
        kern,
        grid=(batch, pairs),
        in_specs=[
            pl.BlockSpec((seq, 2 * LANES), lambda i, h: (i, h)),
            pl.BlockSpec((seq, 2 * LANES), lambda i, h: (i, h)),
            pl.BlockSpec((LANES, seq), lambda i, h: (h, i)),
        ],
        out_specs=pl.BlockSpec((seq, LANES), lambda i, h: (i, h)),
        out_shape=jax.ShapeDtypeStruct((n, FOX_W), BF16),
        compiler_params=_params(2),
        name="fox_attention",
    )(q_aug, k_aug, v_t)


def _ret_kernel(lg_ref, q_ref, k_ref, v_ref, g_ref, o_ref, *, chunk):
    s = q_ref.shape[1]
    lane = lax.broadcasted_iota(jnp.int32, (chunk, LANES), 1)
    ri = lax.broadcasted_iota(jnp.int32, (chunk, chunk), 0)
    ci = lax.broadcasted_iota(jnp.int32, (chunk, chunk), 1)
    diff = (ri - ci).astype(F32)
    pos = lax.broadcasted_iota(jnp.int32, (chunk, 1), 0).astype(F32)

    def head(h):
        lg = lg_ref[h]
        mine = (lane // RET_KEY_DIM) == (h % 2)
        qk = slice((h // 2) * LANES, (h // 2 + 1) * LANES)
        vs = slice(h * RET_VAL_DIM, (h + 1) * RET_VAL_DIM)
        inner = jnp.where(diff >= 0, jnp.exp(jnp.maximum(diff, 0.0) * lg), 0.0)
        q_decay = jnp.exp((pos + 1.0) * lg)
        k_decay = jnp.exp((chunk - 1.0 - pos) * lg)
        chunk_decay = jnp.exp(jnp.full((1, 1), chunk, F32) * lg)
        state = jnp.zeros((LANES, RET_VAL_DIM), F32)
        for c in range(s // chunk):
            rows = slice(c * chunk, (c + 1) * chunk)
            q = jnp.where(mine, q_ref[0, rows, qk], jnp.zeros((), BF16))
            k = jnp.where(mine, k_ref[0, rows, qk], jnp.zeros((), BF16))
            v = v_ref[0, rows, vs]
            scores = _dot_nt(q, k)
            inter = _dot(q, state.astype(BF16))
            kd = (k.astype(F32) * k_decay).astype(BF16)
            update = lax.dot_general(kd, v, (((0,), (0,)), ((), ())), preferred_element_type=F32)
            yield
            o = _dot((scores * inner).astype(BF16), v) + inter * q_decay
            state = state * chunk_decay + update
            yield
            o = o * lax.rsqrt(jnp.mean(o * o, axis=-1, keepdims=True) + EPS)
            g = g_ref[0, rows, vs]
            o_ref[0, rows, vs] = (o * (g * _sigmoid(g))).astype(o_ref.dtype)

    _interleave(head(h) for h in range(RET_HEADS))


def _retention(lg, rq, rk, rv, rg, chunk):
    b, s, _ = rq.shape
    kern = functools.partial(_ret_kernel, chunk=chunk)
    qk_spec = pl.BlockSpec((1, s, RET_QK_W), lambda i, lg_ref: (i, 0, 0))
    v_spec = pl.BlockSpec((1, s, RET_V_W), lambda i, lg_ref: (i, 0, 0))
    return pl.pallas_call(
        kern,
        grid_spec=pltpu.PrefetchScalarGridSpec(
            num_scalar_prefetch=1,
            grid=(b,),
            in_specs=[qk_spec, qk_spec, v_spec, v_spec],
            out_specs=v_spec,
        ),
        out_shape=jax.ShapeDtypeStruct((b, s, RET_V_W), BF16),
        compiler_params=_params(1),
        name="retention",
    )(lg, rq, rk, rv, rg)


MERGE_SUB = 256


def _merge_kernel(x_ref, fox_ref, ro_ref, gmix_ref, wmg_ref, bmg_ref, wb_ref, wout_ref, gffn_ref,
                  wr_ref, br_ref, h1_ref, hn_ref, rf_ref, ri_ref, cnt_ref, base_scr):
    i = pl.program_id(0)
    tm, d = x_ref.shape
    sub = min(MERGE_SUB, tm)

    @pl.when(i == 0)
    def _():
        base_scr[...] = jnp.zeros(base_scr.shape, F32)

    lane = lax.broadcasted_iota(jnp.int32, (sub, LANES), 1)
    lane_f = lane.astype(F32)
    r = lax.broadcasted_iota(jnp.int32, (sub, sub), 0)
    c = lax.broadcasted_iota(jnp.int32, (sub, sub), 1)
    tri = jnp.where(c < r, 1.0, 0.0).astype(BF16)
    counts = [base_scr[...]]

    def sub_tile(t):
        rows = slice(t * sub, (t + 1) * sub)
        x = x_ref[rows, :]
        xn = _rms(x, gmix_ref[...]).astype(BF16)
        yield
        pre = _dot(xn, wmg_ref[...])
        yield
        gate = _sigmoid(pre + bmg_ref[...])
        yield
        pf = _dot(fox_ref[rows, :], wb_ref[0])
        pr = _dot(ro_ref[rows, :], wb_ref[1])
        yield
        merged = (gate[:, :d] * pf + gate[:, d:] * pr).astype(BF16)
        yield
        h1 = x + _dot(merged, wout_ref[...])
        yield
        h1_ref[rows, :] = h1
        hn = _rms(h1, gffn_ref[...]).astype(BF16)
        _store_chunks(hn_ref, _pack_pairs(hn), rows)
        yield
        logits = _dot(hn, wr_ref[...]) + br_ref[...]
        yield
        vals, idxs = [], []
        cur = logits
        for _ in range(TOP_K):
            m = jnp.max(cur, axis=-1, keepdims=True)
            idx = jnp.min(jnp.where(cur == m, lane_f, float(LANES)), axis=-1, keepdims=True)
            vals.append(m)
            idxs.append(idx)
            cur = jnp.where(lane_f == idx, -jnp.inf, cur)
        exps = [jnp.exp(v - vals[0]) for v in vals]
        denom = exps[0] + exps[1] + exps[2] + exps[3]
        onehot = jnp.zeros(logits.shape, F32)
        for idx in idxs:
            onehot = onehot + jnp.where(lane_f == idx, 1.0, 0.0)
        yield
        assert len(counts) == t + 1
        before = _dot(tri, onehot.astype(BF16)) + counts[t]
        counts.append(counts[t] + jnp.sum(onehot, axis=0, keepdims=True))
        rf = jnp.zeros(logits.shape, F32)
        ri = jnp.zeros(logits.shape, F32)
        for j in range(TOP_K):
            rank = jnp.sum(jnp.where(lane_f == idxs[j], before, 0.0), axis=-1, keepdims=True)
            rf = jnp.where(lane == j, exps[j] / denom, rf)
            ri = jnp.where(lane == j, idxs[j], ri)
            ri = jnp.where(lane == TOP_K + j, rank, ri)
        rf_ref[rows, :] = rf
        ri_ref[rows, :] = ri.astype(jnp.int32)

    _interleave(sub_tile(t) for t in range(tm // sub))
    base_scr[...] = counts[-1]
    cnt_ref[...] = counts[-1]


def _merge(x2, fox, ro, gmix, wmg, bmg, wb, wout, gffn, wr, br, tm, first_tile, n):
    d = x2.shape[1]
    row = lambda i: (i, 0)
    src = lambda i: (i + first_tile, 0)
    const = lambda i: (0, 0)
    return pl.pallas_call(
        _merge_kernel,
        grid=(n // tm,),
        in_specs=[
            pl.BlockSpec((tm, d), src), pl.BlockSpec((tm, FOX_W), src), pl.BlockSpec((tm, RET_V_W), src),
            pl.BlockSpec((1, d), const), pl.BlockSpec((d, 2 * d), const), pl.BlockSpec((1, 2 * d), const),
            pl.BlockSpec((2, FOX_W, d), lambda i: (0, 0, 0)), pl.BlockSpec((d, d), const),
            pl.BlockSpec((1, d), const), pl.BlockSpec((d, LANES), const), pl.BlockSpec((1, LANES), const),
        ],
        out_specs=[
            pl.BlockSpec((tm, d), row), pl.BlockSpec((d // 2 // SC_COLS, tm, SC_COLS), lambda i: (0, i, 0)),
            pl.BlockSpec((tm, LANES), row), pl.BlockSpec((tm, LANES), row),
            pl.BlockSpec((1, LANES), const),
        ],
        out_shape=[
            jax.ShapeDtypeStruct((n, d), F32), jax.ShapeDtypeStruct((d // 2 // SC_COLS, n, SC_COLS), jnp.int32),
            jax.ShapeDtypeStruct((n, LANES), F32), jax.ShapeDtypeStruct((n, LANES), jnp.int32),
            jax.ShapeDtypeStruct((1, LANES), F32),
        ],
        scratch_shapes=[pltpu.VMEM((1, LANES), F32)],
        compiler_params=_params(1),
        name="merge_router",
    )(x2, fox, ro, gmix, wmg, bmg, wb, wout, gffn, wr, br)


def _sc_mesh():
    return plsc.VectorSubcoreMesh(core_axis_name="core", subcore_axis_name="subcore")


def _sc_dispatch(rows, dest_t, n_out):
    chunks, n, w = rows.shape

    @functools.partial(pl.kernel, out_type=jax.ShapeDtypeStruct((chunks, n_out, w), rows.dtype),
                       mesh=_sc_mesh(), scratch_types=[], name="moe_dispatch")
    def run(x_hbm, i_hbm, o_hbm):
        for c in range(chunks):
            def body(x_vmem, i_vmem, c=c):
                for j in range(TOP_K):
                    pltpu.sync_copy(x_vmem, o_hbm.at[c].at[i_vmem.at[j]])

            pltpu.emit_pipeline(
                body,
                grid=(n // SC_WINDOW,),
                in_specs=[pl.BlockSpec((SC_WINDOW, w), lambda i: (i, 0)),
                          pl.BlockSpec((TOP_K, SC_WINDOW), lambda i: (0, i))],
                out_specs=[],
                core_axis_name=("core", "subcore"),
                dimension_semantics=(pltpu.PARALLEL,),
            )(x_hbm.at[c], i_hbm)

    return run(rows, dest_t)


def _sc_gather(table, idx):
    chunks, _, w = table.shape
    m = idx.shape[1]

    @functools.partial(pl.kernel, out_type=jax.ShapeDtypeStruct((chunks, m, w), table.dtype),
                       mesh=_sc_mesh(), scratch_types=[], name="moe_combine_gather")
    def run(t_hbm, i_hbm, o_hbm):
        for c in range(chunks):
            def body(i_vmem, o_vmem, c=c):
                pltpu.sync_copy(t_hbm.at[c].at[i_vmem.at[0]], o_vmem)

            pltpu.emit_pipeline(
                body,
                grid=(m // SC_WINDOW,),
                in_specs=[pl.BlockSpec((1, SC_WINDOW), lambda i: (0, i))],
                out_specs=[pl.BlockSpec((SC_WINDOW, w), lambda i: (i, 0))],
                core_axis_name=("core", "subcore"),
                dimension_semantics=(pltpu.PARALLEL,),
            )(i_hbm, o_hbm.at[c])

    return run(table, idx)


GLU_GROUP = 2 * LANES
EXPERT_SUB = 256


def _expert_kernel(be_ref, valid_ref, x_ref, wgu_ref, bgu_ref, wd_ref, bd_ref, perm_ref, y_ref, wgu_scr, wd_scr):
    i = pl.program_id(0)
    f2 = wgu_ref.shape[2]
    bm = x_ref.shape[1]
    sub = min(EXPERT_SUB, bm)
    valid = valid_ref[i]
    live = valid > 0

    @pl.when(jnp.logical_and(live, jnp.logical_or(i == 0, be_ref[i] != be_ref[jnp.maximum(i - 1, 0)])))
    def _():
        for b in range(f2 // GLU_GROUP):
            cols = slice(b * GLU_GROUP, (b + 1) * GLU_GROUP)
            wgu_scr[:, cols] = _dot(wgu_ref[0, :, cols].astype(BF16), perm_ref[...]).astype(BF16)
        wd_scr[...] = wd_ref[0].astype(BF16)

    def sub_block(t, sub):
        rows = slice(t * sub, (t + 1) * sub)
        x = _unpack_pairs([x_ref[c, rows, :] for c in range(x_ref.shape[0])]).astype(BF16)
        yield
        gu = _dot(x, wgu_scr[...])
        yield
        gu = gu + bgu_ref[0]
        acts = []
        for b in range(f2 // GLU_GROUP):
            glu = jnp.minimum(gu[:, b * GLU_GROUP:b * GLU_GROUP + LANES], SWIGLU_LIMIT)
            lin = jnp.clip(gu[:, b * GLU_GROUP + LANES:(b + 1) * GLU_GROUP], -SWIGLU_LIMIT, SWIGLU_LIMIT)
            acts.append((glu * _sigmoid(SWIGLU_ALPHA * glu) * (lin + 1.0)).astype(BF16))
        act = jnp.concatenate(acts, axis=1)
        yield
        y = _dot(act, wd_scr[...])
        yield
        _store_chunks(y_ref, _pack_pairs(y + bd_ref[0]), rows)

    @pl.when(valid == bm)
    def _():
        _interleave(sub_block(t, sub) for t in range(bm // sub))

    for t in range(bm // sub):
        @pl.when(jnp.logical_and(valid < bm, valid > bm - (t + 1) * sub))
        def _(t=t):
            _interleave([sub_block(t, sub)])


def _experts(block_e, valid, x_buf, wgu, bgu, wd, bd, bm):
    p = x_buf.shape[1]
    f, d = wd.shape[1:]
    perm = np.zeros((GLU_GROUP, GLU_GROUP), np.float32)
    for c in range(LANES):
        perm[2 * c, c] = 1.0
        perm[2 * c + 1, LANES + c] = 1.0
    return pl.pallas_call(
        _expert_kernel,
        grid_spec=pltpu.PrefetchScalarGridSpec(
            num_scalar_prefetch=2,
            grid=(p // bm,),
            in_specs=[
                pl.BlockSpec((d // 2 // SC_COLS, bm, SC_COLS), lambda i, be, nu: (0, i, 0)),
                pl.BlockSpec((1, d, 2 * f), lambda i, be, nu: (be[i], 0, 0)),
                pl.BlockSpec((1, 1, 2 * f), lambda i, be, nu: (be[i], 0, 0)),
                pl.BlockSpec((1, f, d), lambda i, be, nu: (be[i], 0, 0)),
                pl.BlockSpec((1, 1, d), lambda i, be, nu: (be[i], 0, 0)),
                pl.BlockSpec((GLU_GROUP, GLU_GROUP), lambda i, be, nu: (0, 0)),
            ],
            out_specs=pl.BlockSpec((d // 2 // SC_COLS, bm, SC_COLS), lambda i, be, nu: (0, i, 0)),
            scratch_shapes=[pltpu.VMEM((d, 2 * f), BF16), pltpu.VMEM((f, d), BF16)],
        ),
        out_shape=jax.ShapeDtypeStruct((d // 2 // SC_COLS, p, SC_COLS), jnp.int32),
        compiler_params=_params(1),
        name="expert_ffn",
    )(block_e, valid, x_buf, wgu, bgu, wd, bd, jnp.asarray(perm, BF16))


FINAL_SUB = 256


def _final_kernel(h1_ref, yg_ref, rf_ref, p_ref, gple_ref, wpg_ref, wpp_ref, gfin_ref, *rest):
    o_ref = rest[-1]
    tm = h1_ref.shape[0]
    sub = min(FINAL_SUB, tm)

    def sub_tile(t):
        rows = slice(t * sub, (t + 1) * sub)
        rf = rf_ref[rows, :]
        h2 = h1_ref[rows, :]
        for j in range(TOP_K):
            h2 = h2 + _unpack_pairs([yg_ref[c, j, rows, :] for c in range(yg_ref.shape[0])]) * rf[:, j:j + 1]
        hn = _rms(h2, gple_ref[...]).astype(BF16)
        yield
        pre = _dot(hn, wpg_ref[...])
        proj = _dot(p_ref[rows, :].astype(BF16), wpp_ref[...])
        yield
        h3 = h2 + _sigmoid(pre) * proj
        o_ref[rows, :] = _rms(h3, gfin_ref[...])

    _interleave(sub_tile(t) for t in range(tm // sub))


def _final(h1, yg, rf, p2, gple, wpg, wpp, gfin, tm, first_tile, prev_out):
    n_total, d = h1.shape
    n = yg.shape[2]
    pd = p2.shape[1]
    dst = lambda i: (i + first_tile, 0)
    const = lambda i: (0, 0)
    in_specs = [
        pl.BlockSpec((tm, d), dst), pl.BlockSpec((d // 2 // SC_COLS, TOP_K, tm, SC_COLS), lambda i: (0, 0, i, 0)),
        pl.BlockSpec((tm, LANES), dst), pl.BlockSpec((tm, pd), dst),
        pl.BlockSpec((1, d), const), pl.BlockSpec((d, d), const), pl.BlockSpec((pd, d), const),
        pl.BlockSpec((1, d), const),
    ]
    args = [h1, yg, rf, p2, gple, wpg, wpp, gfin]
    aliases = {}
    if prev_out is not None:
        in_specs.append(pl.BlockSpec(memory_space=pl.ANY))
        aliases = {len(args): 0}
        args.append(prev_out)
    return pl.pallas_call(
        _final_kernel,
        grid=(n // tm,),
        in_specs=in_specs,
        out_specs=pl.BlockSpec((tm, d), dst),
        out_shape=jax.ShapeDtypeStruct((n_total, d), F32),
        input_output_aliases=aliases,
        compiler_params=_params(1),
        name="combine_ple_norm",
    )(*args)


def _layer(h, p, mix_norm, w_in, b_forget, w_branch, w_merge_gate, b_merge_gate, w_out, ffn_norm,
           w_router, b_router, w_gate_up, b_gate_up, w_down, b_down, ple_norm, w_ple_gate, w_ple_proj,
           final_norm, *, tm, tq, chunk, bm, groups):
    b, s, d = h.shape
    n = b * s
    x2 = h.reshape(n, d)
    row = lambda t: t.reshape(1, -1)

    c0 = 3 * FOX_W
    w_main = jnp.concatenate([w_in[:, :2 * FOX_W], w_in[:, c0 + FOX_HEADS:]], axis=1).astype(BF16)
    w_vt = w_in[:, 2 * FOX_W:c0].T.astype(BF16)
    w_f = jnp.pad(w_in[:, c0:c0 + FOX_HEADS], ((0, 0), (0, LANES - FOX_HEADS))).astype(BF16)
    b_f = jnp.pad(b_forget, (0, LANES - FOX_HEADS)).reshape(1, LANES)
    half = RET_KEY_DIM // 2
    inv = ROPE_BASE ** (-jnp.arange(half, dtype=F32) / half)
    ang = jnp.arange(s).astype(F32)[:, None] * inv[None, :]
    cos_t = jnp.tile(jnp.cos(ang), (1, RET_QK_W // half))
    sin_t = jnp.tile(jnp.sin(ang), (1, RET_QK_W // half))

    q_aug, k_aug, v_t, rq, rk, rv, rg = _inproj(x2, row(mix_norm), w_main, w_vt, w_f, b_f, cos_t, sin_t, s, tm)
    fox = _fox(q_aug, k_aug, v_t, b, s, tq)

    lg = jnp.log1p(-jnp.exp2(-5.0 - jnp.arange(RET_HEADS, dtype=F32)))
    ro = _retention(lg, rq.reshape(b, s, RET_QK_W), rk.reshape(b, s, RET_QK_W),
                    rv.reshape(b, s, RET_V_W), rg.reshape(b, s, RET_V_W), chunk)

    w_r = jnp.pad(w_router, ((0, 0), (0, LANES - N_EXPERTS))).astype(BF16)
    b_r = jnp.concatenate([b_router, jnp.full((LANES - N_EXPERTS,), -1e30, F32)]).reshape(1, LANES)
    ne, dd, f2 = w_gate_up.shape
    bgu = b_gate_up.reshape(ne, f2 // GLU_GROUP, LANES, 2).transpose(0, 1, 3, 2).reshape(ne, 1, f2)
    merge_w = (row(mix_norm), w_merge_gate.astype(BF16), row(b_merge_gate), w_branch.astype(BF16),
               w_out.astype(BF16), row(ffn_norm), w_r, b_r)
    final_w = (row(ple_norm), w_ple_gate.astype(BF16), w_ple_proj.astype(BF16), row(final_norm))
    p2 = p.reshape(n, -1)

    h1, hn, rf, ri, cnt = _merge(x2, fox, ro.reshape(n, RET_V_W), *merge_w, tm, 0, n)

    a = n * TOP_K
    nb = -(-(a + N_EXPERTS * (bm - 1)) // bm)
    e_idx = ri[:, :TOP_K]
    rank = ri[:, TOP_K:2 * TOP_K]
    counts = cnt[0, :N_EXPERTS].astype(jnp.int32)
    padded = (counts + bm - 1) // bm * bm
    pad_end = jnp.cumsum(padded)
    first_row = pad_end - counts
    dest_t = (first_row[e_idx] + rank).T
    block_start = jnp.arange(nb, dtype=jnp.int32) * bm
    block_e = jnp.minimum(jnp.sum(block_start[:, None] >= pad_end[None, :], axis=1),
                          N_EXPERTS - 1).astype(jnp.int32)
    valid = jnp.where(block_start < pad_end[-1],
                      jnp.clip(block_start + bm - first_row[block_e], 0, bm), 0).astype(jnp.int32)

    x_buf = _sc_dispatch(hn, dest_t, nb * bm)
    y_buf = _experts(block_e, valid, x_buf, w_gate_up, bgu, w_down, b_down.reshape(ne, 1, dd), bm)

    ng = n // groups
    out = None
    for g in range(groups):
        idx = dest_t[:, g * ng:(g + 1) * ng].reshape(1, ng * TOP_K)
        yg = _sc_gather(y_buf, idx).reshape(d // 2 // SC_COLS, TOP_K, ng, SC_COLS)
        out = _final(h1, yg, rf, p2, *final_w, tm, g * (ng // tm), out)
    return out.reshape(b, s, d)


def kernel(x, p, mix_norm, w_in, b_forget, w_branch, w_merge_gate, b_merge_gate, w_out, ffn_norm, w_router,
           b_router, w_gate_up, b_gate_up, w_down, b_down, ple_norm, w_ple_gate, w_ple_proj, final_norm):
    depth = p.shape[0]
    assert depth == 1, "the final norm is fused into the (single) layer"
    return _layer(x, p[0], mix_norm[0], w_in[0], b_forget[0], w_branch[0], w_merge_gate[0], b_merge_gate[0],
                  w_out[0], ffn_norm[0], w_router[0], b_router[0], w_gate_up[0], b_gate_up[0], w_down[0],
                  b_down[0], ple_norm[0], w_ple_gate[0], w_ple_proj[0], final_norm,
                  tm=512, tq=512, chunk=256, bm=1024, groups=4)
```

```python
import functools

import numpy as np
import jax
import jax.numpy as jnp
from jax import lax
from jax.experimental import pallas as pl
from jax.experimental.pallas import tpu as pltpu
from jax.experimental.pallas import tpu_sc as plsc

FOX_HEADS = 8
FOX_HEAD_DIM = 64
RET_HEADS = 4
RET_KEY_DIM = 64
RET_VAL_DIM = 128
ROPE_BASE = 10000.0
N_EXPERTS = 32
TOP_K = 4
SWIGLU_LIMIT = 7.0
SWIGLU_ALPHA = 1.702
EPS = 1e-6

LANES = 128
FOX_W = FOX_HEADS * FOX_HEAD_DIM
RET_QK_W = RET_HEADS * RET_KEY_DIM
RET_V_W = RET_HEADS * RET_VAL_DIM
VMEM_LIMIT = 56 * 1024 * 1024
HIGH_HALF = -65536
LOG2_E = 1.4426950408889634
SC_WINDOW = 128
SC_COLS = 256

F32 = jnp.float32
BF16 = jnp.bfloat16


def _rms(x, g):
    return x * lax.rsqrt(jnp.mean(x * x, axis=-1, keepdims=True) + EPS) * g


def _sigmoid(x):
    return 1.0 / (1.0 + jnp.exp(-x))


def _dot(a, b):
    return jnp.dot(a, b, preferred_element_type=F32)


def _dot_nt(a, b):
    return lax.dot_general(a, b, (((1,), (1,)), ((), ())), preferred_element_type=F32)


def _pack_pairs(v):
    bits = pltpu.bitcast(v.astype(BF16).astype(F32), jnp.int32)
    half = v.shape[1] // 2
    return lax.shift_right_logical(bits[:, :half], 16) | (bits[:, half:] & HIGH_HALF)


def _unpack_pairs(chunks):
    return jnp.concatenate([pltpu.bitcast(lax.shift_left(w, 16), F32) for w in chunks]
                           + [pltpu.bitcast(w & HIGH_HALF, F32) for w in chunks], axis=1)


def _store_chunks(ref, v, rows=slice(None)):
    for c in range(ref.shape[0]):
        ref[c, rows, :] = v[:, c * SC_COLS:(c + 1) * SC_COLS]


def _interleave(generators):
    live = []
    pending = list(generators)
    while pending or live:
        if pending:
            live.append(pending.pop(0))
        for g in list(live):
            try:
                next(g)
            except StopIteration:
                live.remove(g)


def _resident(shape):
    return pl.BlockSpec(shape, lambda *_: (0,) * len(shape), pipeline_mode=pl.Buffered(1))


def _params(n_axes):
    return pltpu.CompilerParams(dimension_semantics=("arbitrary",) * n_axes,
                                vmem_limit_bytes=VMEM_LIMIT)


C_MID = FOX_HEADS
C_LO = 2 * FOX_HEADS
C_ONE = 3 * FOX_HEADS
N_AUG = 3


def _split3(t):
    hi = t.astype(BF16).astype(F32)
    r = t - hi
    mid = r.astype(BF16).astype(F32)
    lo = (r - mid).astype(BF16).astype(F32)
    return hi + pltpu.roll(mid, C_MID, 1) + pltpu.roll(lo, C_LO, 1)


INPROJ_SUB = 256


def _inproj_kernel(x_ref, g_ref, w_ref, wvt_ref, wf_ref, bf_ref, cos_ref, sin_ref, pq_ref, pk_ref,
                   q_ref, k_ref, vt_ref, rq_ref, rk_ref, rv_ref, rg_ref, carry_scr, *, tiles_per_seq):
    i = pl.program_id(0)
    tm = x_ref.shape[0]
    sub = min(INPROJ_SUB, tm)

    @pl.when(i % tiles_per_seq == 0)
    def _():
        carry_scr[...] = jnp.zeros(carry_scr.shape, F32)

    lane = lax.broadcasted_iota(jnp.int32, (sub, LANES), 1)
    r = lax.broadcasted_iota(jnp.int32, (sub, sub), 0)
    c = lax.broadcasted_iota(jnp.int32, (sub, sub), 1)
    tri = jnp.where(c <= r, 1.0, 0.0).astype(BF16)
    own = lane < FOX_HEAD_DIM
    half = RET_KEY_DIM // 2
    lane_r = lax.broadcasted_iota(jnp.int32, (sub, RET_QK_W), 1)
    first = (lane_r % RET_KEY_DIM) < half
    carries = [carry_scr[...]]

    def sub_tile(t):
        rows = slice(t * sub, (t + 1) * sub)
        xn = _rms(x_ref[rows, :], g_ref[...]).astype(BF16)
        yield
        u = _dot(xn, w_ref[...])
        vt_ref[:, rows] = _dot_nt(wvt_ref[...], xn).astype(BF16)
        z = _dot(xn, wf_ref[...]) + bf_ref[...]
        yield
        lf = jnp.where(lane < FOX_HEADS, jnp.minimum(z, 0.0) - jnp.log1p(jnp.exp(-jnp.abs(z))), 0.0)
        ps = _dot(tri, _split3(lf).astype(BF16))
        yield
        assert len(carries) == t + 1
        cum = ps + pltpu.roll(ps, LANES - C_MID, 1) + pltpu.roll(ps, LANES - C_LO, 1)
        cum = jnp.where(lane < FOX_HEADS, cum, 0.0) + carries[t]
        carries.append(cum[sub - 1:sub, :])
        c3 = (_split3(cum * LOG2_E) + jnp.where(lane == C_ONE, 1.0, 0.0)).astype(BF16)
        aug_q = _dot(c3, pq_ref[...])
        aug_k = _dot(c3, pk_ref[...])

        o = 2 * FOX_W
        rq = u[:, o:o + RET_QK_W]; o += RET_QK_W
        rk = u[:, o:o + RET_QK_W]; o += RET_QK_W
        rv_ref[rows, :] = u[:, o:o + RET_V_W].astype(BF16); o += RET_V_W
        rg_ref[rows, :] = u[:, o:o + RET_V_W]

        cos = cos_ref[rows, :]
        sin = sin_ref[rows, :]

        def rot(v):
            partner = jnp.where(first, -pltpu.roll(v, RET_QK_W - half, 1), pltpu.roll(v, half, 1))
            return v * cos + partner * sin

        rq_ref[rows, :] = rot(rq).astype(BF16)
        rk_ref[rows, :] = (rot(rk) * (RET_KEY_DIM ** -0.5)).astype(BF16)
        yield
        fq = u[:, 0:FOX_W] * (FOX_HEAD_DIM ** -0.5 * LOG2_E)
        fk = u[:, FOX_W:2 * FOX_W]
        for src, aug, dst in ((fq, aug_q, q_ref), (fk, aug_k, k_ref)):
            for j in range(FOX_W // LANES):
                blk = src[:, j * LANES:(j + 1) * LANES]
                e0, e1 = 2 * j * LANES, (2 * j + 1) * LANES
                dst[rows, e0:e0 + LANES] = jnp.where(own, blk, aug[:, e0:e0 + LANES]).astype(BF16)
                dst[rows, e1:e1 + LANES] = jnp.where(own, pltpu.roll(blk, FOX_HEAD_DIM, 1),
                                                      aug[:, e1:e1 + LANES]).astype(BF16)

    _interleave(sub_tile(t) for t in range(tm // sub))
    carry_scr[...] = carries[-1]


def _placement():
    pq = np.zeros((LANES, FOX_HEADS * LANES), np.float32)
    pk = np.zeros((LANES, FOX_HEADS * LANES), np.float32)
    for h in range(FOX_HEADS):
        base = h * LANES + FOX_HEAD_DIM
        for part, src in enumerate((h, C_MID + h, C_LO + h)):
            pq[C_ONE, base + part] = 1.0
            pq[src, base + N_AUG + part] = 1.0
            pk[src, base + part] = -1.0
            pk[C_ONE, base + N_AUG + part] = 1.0
    return jnp.asarray(pq, BF16), jnp.asarray(pk, BF16)


def _inproj(x2, g, w_main, w_vt, w_f, b_f, cos_t, sin_t, seq, tm):
    n, d = x2.shape
    wn = w_main.shape[1]
    spt = seq // tm
    aw = FOX_HEADS * LANES
    pq, pk = _placement()
    row = lambda i: (i, 0)
    const = lambda i: (0, 0)
    pos = lambda i: (i % spt, 0)
    out_shape = [
        jax.ShapeDtypeStruct((n, aw), BF16), jax.ShapeDtypeStruct((n, aw), BF16),
        jax.ShapeDtypeStruct((FOX_W, n), BF16),
        jax.ShapeDtypeStruct((n, RET_QK_W), BF16), jax.ShapeDtypeStruct((n, RET_QK_W), BF16),
        jax.ShapeDtypeStruct((n, RET_V_W), BF16), jax.ShapeDtypeStruct((n, RET_V_W), F32),
    ]
    out_specs = [
        pl.BlockSpec((tm, aw), row), pl.BlockSpec((tm, aw), row),
        pl.BlockSpec((FOX_W, tm), lambda i: (0, i)),
        pl.BlockSpec((tm, RET_QK_W), row), pl.BlockSpec((tm, RET_QK_W), row),
        pl.BlockSpec((tm, RET_V_W), row), pl.BlockSpec((tm, RET_V_W), row),
    ]
    return pl.pallas_call(
        functools.partial(_inproj_kernel, tiles_per_seq=spt),
        grid=(n // tm,),
        in_specs=[
            pl.BlockSpec((tm, d), row), pl.BlockSpec((1, d), const),
            _resident((d, wn)), _resident((FOX_W, d)),
            _resident((d, LANES)), pl.BlockSpec((1, LANES), const),
            pl.BlockSpec((tm, RET_QK_W), pos), pl.BlockSpec((tm, RET_QK_W), pos),
            _resident((LANES, aw)), _resident((LANES, aw)),
        ],
        out_specs=out_specs,
        out_shape=out_shape,
        scratch_shapes=[pltpu.VMEM((1, LANES), F32)],
        compiler_params=_params(1),
        name="inproj",
    )(x2, g, w_main, w_vt, w_f, b_f, cos_t, sin_t, pq, pk)


def _fox_kernel(q_ref, k_ref, vt_ref, o_ref, *, tq):
    seq = q_ref.shape[0]
    key = lax.broadcasted_iota(jnp.int32, (tq, tq), 0)
    qry = lax.broadcasted_iota(jnp.int32, (tq, tq), 1)
    vrow = lax.broadcasted_iota(jnp.int32, (LANES, tq), 0)
    one = jnp.ones((), BF16)
    items = [(qi, j) for qi in range(seq // tq) for j in range(qi + 1)]

    def logits(qi, j):
        out = []
        for a in range(2):
            sl = slice(a * LANES, (a + 1) * LANES)
            s = _dot_nt(k_ref[j * tq:(j + 1) * tq, sl], q_ref[qi * tq:(qi + 1) * tq, sl])
            out.append(jnp.where(key <= qry, s, -jnp.inf) if j == qi else s)
        return out

    s_cur = logits(*items[0])
    carry = None
    for w, (qi, j) in enumerate(items):
        s_next = logits(*items[w + 1]) if w + 1 < len(items) else None
        if j == 0:
            carry = [(jnp.full((1, tq), -jnp.inf, F32), jnp.zeros((LANES, tq), F32)) for _ in range(2)]
        v = vt_ref[:, j * tq:(j + 1) * tq]
        for a in range(2):
            m_prev, acc = carry[a]
            m_new = jnp.maximum(m_prev, jnp.max(s_cur[a], axis=0, keepdims=True))
            alpha = jnp.exp2(m_prev - m_new)
            p = jnp.exp2(s_cur[a] - m_new).astype(BF16)
            va = jnp.where((vrow // FOX_HEAD_DIM) == a, v, one)
            carry[a] = (m_new, alpha * acc + _dot(va, p))
        if j == qi:
            acc0, acc1 = carry[0][1], carry[1][1]
            ot = jnp.where(vrow < FOX_HEAD_DIM, acc0 / acc0[FOX_HEAD_DIM:FOX_HEAD_DIM + 1, :],
                           acc1 / acc1[0:1, :])
            o_ref[qi * tq:(qi + 1) * tq, :] = ot.T.astype(o_ref.dtype)
        s_cur = s_next


def _fox(q_aug, k_aug, v_t, batch, seq, tq):
    n = q_aug.shape[0]
    pairs = FOX_HEADS // 2
    kern = functools.partial(_fox_kernel, tq=tq)
    return pl.pallas_call(
        kern,
        grid=(batch, pairs),
        in_specs=[
            pl.BlockSpec((seq, 2 * LANES), lambda i, h: (i, h)),
            pl.BlockSpec((seq, 2 * LANES), lambda i, h: (i, h)),
            pl.BlockSpec((LANES, seq), lambda i, h: (h, i)),
        ],
        out_specs=pl.BlockSpec((seq, LANES), lambda i, h: (i, h)),
        out_shape=jax.ShapeDtypeStruct((n, FOX_W), BF16),
        compiler_params=_params(2),
        name="fox_attention",
    )(q_aug, k_aug, v_t)


def _ret_kernel(lg_ref, q_ref, k_ref, v_ref, g_ref, o_ref, *, chunk):
    s = q_ref.shape[1]
    lane = lax.broadcasted_iota(jnp.int32, (chunk, LANES), 1)
    ri = lax.broadcasted_iota(jnp.int32, (chunk, chunk), 0)
    ci = lax.broadcasted_iota(jnp.int32, (chunk, chunk), 1)
    diff = (ri - ci).astype(F32)
    pos = lax.broadcasted_iota(jnp.int32, (chunk, 1), 0).astype(F32)

    def head(h):
        lg = lg_ref[h]
        mine = (lane // RET_KEY_DIM) == (h % 2)
        qk = slice((h // 2) * LANES, (h // 2 + 1) * LANES)
        vs = slice(h * RET_VAL_DIM, (h + 1) * RET_VAL_DIM)
        inner = jnp.where(diff >= 0, jnp.exp(jnp.maximum(diff, 0.0) * lg), 0.0)
        q_decay = jnp.exp((pos + 1.0) * lg)
        k_decay = jnp.exp((chunk - 1.0 - pos) * lg)
        chunk_decay = jnp.exp(jnp.full((1, 1), chunk, F32) * lg)
        state = jnp.zeros((LANES, RET_VAL_DIM), F32)
        for c in range(s // chunk):
            rows = slice(c * chunk, (c + 1) * chunk)
            q = jnp.where(mine, q_ref[0, rows, qk], jnp.zeros((), BF16))
            k = jnp.where(mine, k_ref[0, rows, qk], jnp.zeros((), BF16))
            v = v_ref[0, rows, vs]
            scores = _dot_nt(q, k)
            inter = _dot(q, state.astype(BF16))
            kd = (k.astype(F32) * k_decay).astype(BF16)
            update = lax.dot_general(kd, v, (((0,), (0,)), ((), ())), preferred_element_type=F32)
            yield
            o = _dot((scores * inner).astype(BF16), v) + inter * q_decay
            state = state * chunk_decay + update
            yield
            o = o * lax.rsqrt(jnp.mean(o * o, axis=-1, keepdims=True) + EPS)
            g = g_ref[0, rows, vs]
            o_ref[0, rows, vs] = (o * (g * _sigmoid(g))).astype(o_ref.dtype)

    _interleave(head(h) for h in range(RET_HEADS))


def _retention(lg, rq, rk, rv, rg, chunk):
    b, s, _ = rq.shape
    kern = functools.partial(_ret_kernel, chunk=chunk)
    qk_spec = pl.BlockSpec((1, s, RET_QK_W), lambda i, lg_ref: (i, 0, 0))
    v_spec = pl.BlockSpec((1, s, RET_V_W), lambda i, lg_ref: (i, 0, 0))
    return pl.pallas_call(
        kern,
        grid_spec=pltpu.PrefetchScalarGridSpec(
            num_scalar_prefetch=1,
            grid=(b,),
            in_specs=[qk_spec, qk_spec, v_spec, v_spec],
            out_specs=v_spec,
        ),
        out_shape=jax.ShapeDtypeStruct((b, s, RET_V_W), BF16),
        compiler_params=_params(1),
        name="retention",
    )(lg, rq, rk, rv, rg)


MERGE_SUB = 256


def _merge_kernel(x_ref, fox_ref, ro_ref, gmix_ref, wmg_ref, bmg_ref, wb_ref, wout_ref, gffn_ref,
                  wr_ref, br_ref, h1_ref, hn_ref, rf_ref, ri_ref, cnt_ref, base_scr):
    i = pl.program_id(0)
    tm, d = x_ref.shape
    sub = min(MERGE_SUB, tm)

    @pl.when(i == 0)
    def _():
        base_scr[...] = jnp.zeros(base_scr.shape, F32)

    lane = lax.broadcasted_iota(jnp.int32, (sub, LANES), 1)
    lane_f = lane.astype(F32)
    r = lax.broadcasted_iota(jnp.int32, (sub, sub), 0)
    c = lax.broadcasted_iota(jnp.int32, (sub, sub), 1)
    tri = jnp.where(c < r, 1.0, 0.0).astype(BF16)
    counts = [base_scr[...]]

    def sub_tile(t):
        rows = slice(t * sub, (t + 1) * sub)
        x = x_ref[rows, :]
        xn = _rms(x, gmix_ref[...]).astype(BF16)
        yield
        pre = _dot(xn, wmg_ref[...])
        yield
        gate = _sigmoid(pre + bmg_ref[...])
        yield
        pf = _dot(fox_ref[rows, :], wb_ref[0])
        pr = _dot(ro_ref[rows, :], wb_ref[1])
        yield
        merged = (gate[:, :d] * pf + gate[:, d:] * pr).astype(BF16)
        yield
        h1 = x + _dot(merged, wout_ref[...])
        yield
        h1_ref[rows, :] = h1
        hn = _rms(h1, gffn_ref[...]).astype(BF16)
        _store_chunks(hn_ref, _pack_pairs(hn), rows)
        yield
        logits = _dot(hn, wr_ref[...]) + br_ref[...]
        yield
        vals, idxs = [], []
        cur = logits
        for _ in range(TOP_K):
            m = jnp.max(cur, axis=-1, keepdims=True)
            idx = jnp.min(jnp.where(cur == m, lane_f, float(LANES)), axis=-1, keepdims=True)
            vals.append(m)
            idxs.append(idx)
            cur = jnp.where(lane_f == idx, -jnp.inf, cur)
        exps = [jnp.exp(v - vals[0]) for v in vals]
        denom = exps[0] + exps[1] + exps[2] + exps[3]
        onehot = jnp.zeros(logits.shape, F32)
        for idx in idxs:
            onehot = onehot + jnp.where(lane_f == idx, 1.0, 0.0)
        yield
        assert len(counts) == t + 1
        before = _dot(tri, onehot.astype(BF16)) + counts[t]
        counts.append(counts[t] + jnp.sum(onehot, axis=0, keepdims=True))
        rf = jnp.zeros(logits.shape, F32)
        ri = jnp.zeros(logits.shape, F32)
        for j in range(TOP_K):
            rank = jnp.sum(jnp.where(lane_f == idxs[j], before, 0.0), axis=-1, keepdims=True)
            rf = jnp.where(lane == j, exps[j] / denom, rf)
            ri = jnp.where(lane == j, idxs[j], ri)
            ri = jnp.where(lane == TOP_K + j, rank, ri)
        rf_ref[rows, :] = rf
        ri_ref[rows, :] = ri.astype(jnp.int32)

    _interleave(sub_tile(t) for t in range(tm // sub))
    base_scr[...] = counts[-1]
    cnt_ref[...] = counts[-1]


def _merge(x2, fox, ro, gmix, wmg, bmg, wb, wout, gffn, wr, br, tm, first_tile, n):
    d = x2.shape[1]
    row = lambda i: (i, 0)
    src = lambda i: (i + first_tile, 0)
    const = lambda i: (0, 0)
    return pl.pallas_call(
        _merge_kernel,
        grid=(n // tm,),
        in_specs=[
            pl.BlockSpec((tm, d), src), pl.BlockSpec((tm, FOX_W), src), pl.BlockSpec((tm, RET_V_W), src),
            pl.BlockSpec((1, d), const), _resident((d, 2 * d)), pl.BlockSpec((1, 2 * d), const),
            _resident((2, FOX_W, d)), _resident((d, d)),
            pl.BlockSpec((1, d), const), _resident((d, LANES)), pl.BlockSpec((1, LANES), const),
        ],
        out_specs=[
            pl.BlockSpec((tm, d), row), pl.BlockSpec((d // 2 // SC_COLS, tm, SC_COLS), lambda i: (0, i, 0)),
            pl.BlockSpec((tm, LANES), row), pl.BlockSpec((tm, LANES), row),
            pl.BlockSpec((1, LANES), const),
        ],
        out_shape=[
            jax.ShapeDtypeStruct((n, d), F32), jax.ShapeDtypeStruct((d // 2 // SC_COLS, n, SC_COLS), jnp.int32),
            jax.ShapeDtypeStruct((n, LANES), F32), jax.ShapeDtypeStruct((n, LANES), jnp.int32),
            jax.ShapeDtypeStruct((1, LANES), F32),
        ],
        scratch_shapes=[pltpu.VMEM((1, LANES), F32)],
        compiler_params=_params(1),
        name="merge_router",
    )(x2, fox, ro, gmix, wmg, bmg, wb, wout, gffn, wr, br)


def _sc_mesh():
    return plsc.VectorSubcoreMesh(core_axis_name="core", subcore_axis_name="subcore")


def _sc_dispatch(rows, dest_t, n_out):
    chunks, n, w = rows.shape

    @functools.partial(pl.kernel, out_type=jax.ShapeDtypeStruct((chunks, n_out, w), rows.dtype),
                       mesh=_sc_mesh(), scratch_types=[], name="moe_dispatch")
    def run(x_hbm, i_hbm, o_hbm):
        for c in range(chunks):
            def body(x_vmem, i_vmem, c=c):
                for j in range(TOP_K):
                    pltpu.sync_copy(x_vmem, o_hbm.at[c].at[i_vmem.at[j]])

            pltpu.emit_pipeline(
                body,
                grid=(n // SC_WINDOW,),
                in_specs=[pl.BlockSpec((SC_WINDOW, w), lambda i: (i, 0)),
                          pl.BlockSpec((TOP_K, SC_WINDOW), lambda i: (0, i))],
                out_specs=[],
                core_axis_name=("core", "subcore"),
                dimension_semantics=(pltpu.PARALLEL,),
            )(x_hbm.at[c], i_hbm)

    return run(rows, dest_t)


def _sc_gather(table, idx):
    chunks, _, w = table.shape
    m = idx.shape[1]

    @functools.partial(pl.kernel, out_type=jax.ShapeDtypeStruct((chunks, m, w), table.dtype),
                       mesh=_sc_mesh(), scratch_types=[], name="moe_combine_gather")
    def run(t_hbm, i_hbm, o_hbm):
        for c in range(chunks):
            def body(i_vmem, o_vmem, c=c):
                pltpu.sync_copy(t_hbm.at[c].at[i_vmem.at[0]], o_vmem)

            pltpu.emit_pipeline(
                body,
                grid=(m // SC_WINDOW,),
                in_specs=[pl.BlockSpec((1, SC_WINDOW), lambda i: (0, i))],
                out_specs=[pl.BlockSpec((SC_WINDOW, w), lambda i: (i, 0))],
                core_axis_name=("core", "subcore"),
                dimension_semantics=(pltpu.PARALLEL,),
            )(i_hbm, o_hbm.at[c])

    return run(table, idx)


GLU_GROUP = 2 * LANES
EXPERT_SUB = 256


def _expert_kernel(be_ref, valid_ref, x_ref, wgu_ref, bgu_ref, wd_ref, bd_ref, perm_ref, y_ref, wgu_scr, wd_scr):
    i = pl.program_id(0)
    f2 = wgu_ref.shape[2]
    bm = x_ref.shape[1]
    sub = min(EXPERT_SUB, bm)
    valid = valid_ref[i]
    live = valid > 0

    @pl.when(jnp.logical_and(live, jnp.logical_or(i == 0, be_ref[i] != be_ref[jnp.maximum(i - 1, 0)])))
    def _():
        for b in range(f2 // GLU_GROUP):
            cols = slice(b * GLU_GROUP, (b + 1) * GLU_GROUP)
            wgu_scr[:, cols] = _dot(wgu_ref[0, :, cols].astype(BF16), perm_ref[...]).astype(BF16)
        wd_scr[...] = wd_ref[0].astype(BF16)

    def sub_block(t, sub):
        rows = slice(t * sub, (t + 1) * sub)
        x = _unpack_pairs([x_ref[c, rows, :] for c in range(x_ref.shape[0])]).astype(BF16)
        yield
        gu = _dot(x, wgu_scr[...])
        yield
        gu = gu + bgu_ref[0]
        acts = []
        for b in range(f2 // GLU_GROUP):
            glu = jnp.minimum(gu[:, b * GLU_GROUP:b * GLU_GROUP + LANES], SWIGLU_LIMIT)
            lin = jnp.clip(gu[:, b * GLU_GROUP + LANES:(b + 1) * GLU_GROUP], -SWIGLU_LIMIT, SWIGLU_LIMIT)
            acts.append((glu * _sigmoid(SWIGLU_ALPHA * glu) * (lin + 1.0)).astype(BF16))
        act = jnp.concatenate(acts, axis=1)
        yield
        y = _dot(act, wd_scr[...])
        yield
        _store_chunks(y_ref, _pack_pairs(y + bd_ref[0]), rows)

    @pl.when(valid == bm)
    def _():
        _interleave(sub_block(t, sub) for t in range(bm // sub))

    for t in range(bm // sub):
        @pl.when(jnp.logical_and(valid < bm, valid > bm - (t + 1) * sub))
        def _(t=t):
            _interleave([sub_block(t, sub)])


def _experts(block_e, valid, x_buf, wgu, bgu, wd, bd, bm):
    p = x_buf.shape[1]
    f, d = wd.shape[1:]
    perm = np.zeros((GLU_GROUP, GLU_GROUP), np.float32)
    for c in range(LANES):
        perm[2 * c, c] = 1.0
        perm[2 * c + 1, LANES + c] = 1.0
    return pl.pallas_call(
        _expert_kernel,
        grid_spec=pltpu.PrefetchScalarGridSpec(
            num_scalar_prefetch=2,
            grid=(p // bm,),
            in_specs=[
                pl.BlockSpec((d // 2 // SC_COLS, bm, SC_COLS), lambda i, be, nu: (0, i, 0)),
                pl.BlockSpec((1, d, 2 * f), lambda i, be, nu: (be[i], 0, 0)),
                pl.BlockSpec((1, 1, 2 * f), lambda i, be, nu: (be[i], 0, 0)),
                pl.BlockSpec((1, f, d), lambda i, be, nu: (be[i], 0, 0)),
                pl.BlockSpec((1, 1, d), lambda i, be, nu: (be[i], 0, 0)),
                pl.BlockSpec((GLU_GROUP, GLU_GROUP), lambda i, be, nu: (0, 0)),
            ],
            out_specs=pl.BlockSpec((d // 2 // SC_COLS, bm, SC_COLS), lambda i, be, nu: (0, i, 0)),
            scratch_shapes=[pltpu.VMEM((d, 2 * f), BF16), pltpu.VMEM((f, d), BF16)],
        ),
        out_shape=jax.ShapeDtypeStruct((d // 2 // SC_COLS, p, SC_COLS), jnp.int32),
        compiler_params=_params(1),
        name="expert_ffn",
    )(block_e, valid, x_buf, wgu, bgu, wd, bd, jnp.asarray(perm, BF16))


FINAL_SUB = 256


def _final_kernel(h1_ref, yg_ref, rf_ref, p_ref, gple_ref, wpg_ref, wpp_ref, gfin_ref, *rest):
    o_ref = rest[-1]
    tm = h1_ref.shape[0]
    sub = min(FINAL_SUB, tm)

    def sub_tile(t):
        rows = slice(t * sub, (t + 1) * sub)
        rf = rf_ref[rows, :]
        h2 = h1_ref[rows, :]
        for j in range(TOP_K):
            h2 = h2 + _unpack_pairs([yg_ref[c, j, rows, :] for c in range(yg_ref.shape[0])]) * rf[:, j:j + 1]
        hn = _rms(h2, gple_ref[...]).astype(BF16)
        yield
        pre = _dot(hn, wpg_ref[...])
        proj = _dot(p_ref[rows, :].astype(BF16), wpp_ref[...])
        yield
        h3 = h2 + _sigmoid(pre) * proj
        o_ref[rows, :] = _rms(h3, gfin_ref[...])

    _interleave(sub_tile(t) for t in range(tm // sub))


def _final(h1, yg, rf, p2, gple, wpg, wpp, gfin, tm, first_tile, prev_out):
    n_total, d = h1.shape
    n = yg.shape[2]
    pd = p2.shape[1]
    dst = lambda i: (i + first_tile, 0)
    const = lambda i: (0, 0)
    in_specs = [
        pl.BlockSpec((tm, d), dst), pl.BlockSpec((d // 2 // SC_COLS, TOP_K, tm, SC_COLS), lambda i: (0, 0, i, 0)),
        pl.BlockSpec((tm, LANES), dst), pl.BlockSpec((tm, pd), dst),
        pl.BlockSpec((1, d), const), pl.BlockSpec((d, d), const), pl.BlockSpec((pd, d), const),
        pl.BlockSpec((1, d), const),
    ]
    args = [h1, yg, rf, p2, gple, wpg, wpp, gfin]
    aliases = {}
    if prev_out is not None:
        in_specs.append(pl.BlockSpec(memory_space=pl.ANY))
        aliases = {len(args): 0}
        args.append(prev_out)
    return pl.pallas_call(
        _final_kernel,
        grid=(n // tm,),
        in_specs=in_specs,
        out_specs=pl.BlockSpec((tm, d), dst),
        out_shape=jax.ShapeDtypeStruct((n_total, d), F32),
        input_output_aliases=aliases,
        compiler_params=_params(1),
        name="combine_ple_norm",
    )(*args)


def _layer(h, p, mix_norm, w_in, b_forget, w_branch, w_merge_gate, b_merge_gate, w_out, ffn_norm,
           w_router, b_router, w_gate_up, b_gate_up, w_down, b_down, ple_norm, w_ple_gate, w_ple_proj,
           final_norm, *, tm, tq, chunk, bm, groups):
    b, s, d = h.shape
    n = b * s
    x2 = h.reshape(n, d)
    row = lambda t: t.reshape(1, -1)

    c0 = 3 * FOX_W
    w_main = jnp.concatenate([w_in[:, :2 * FOX_W], w_in[:, c0 + FOX_HEADS:]], axis=1).astype(BF16)
    w_vt = w_in[:, 2 * FOX_W:c0].T.astype(BF16)
    w_f = jnp.pad(w_in[:, c0:c0 + FOX_HEADS], ((0, 0), (0, LANES - FOX_HEADS))).astype(BF16)
    b_f = jnp.pad(b_forget, (0, LANES - FOX_HEADS)).reshape(1, LANES)
    half = RET_KEY_DIM // 2
    inv = ROPE_BASE ** (-jnp.arange(half, dtype=F32) / half)
    ang = jnp.arange(s).astype(F32)[:, None] * inv[None, :]
    cos_t = jnp.tile(jnp.cos(ang), (1, RET_QK_W // half))
    sin_t = jnp.tile(jnp.sin(ang), (1, RET_QK_W // half))

    q_aug, k_aug, v_t, rq, rk, rv, rg = _inproj(x2, row(mix_norm), w_main, w_vt, w_f, b_f, cos_t, sin_t, s, tm)
    fox = _fox(q_aug, k_aug, v_t, b, s, tq)

    lg = jnp.log1p(-jnp.exp2(-5.0 - jnp.arange(RET_HEADS, dtype=F32)))
    ro = _retention(lg, rq.reshape(b, s, RET_QK_W), rk.reshape(b, s, RET_QK_W),
                    rv.reshape(b, s, RET_V_W), rg.reshape(b, s, RET_V_W), chunk)

    w_r = jnp.pad(w_router, ((0, 0), (0, LANES - N_EXPERTS))).astype(BF16)
    b_r = jnp.concatenate([b_router, jnp.full((LANES - N_EXPERTS,), -1e30, F32)]).reshape(1, LANES)
    ne, dd, f2 = w_gate_up.shape
    bgu = b_gate_up.reshape(ne, f2 // GLU_GROUP, LANES, 2).transpose(0, 1, 3, 2).reshape(ne, 1, f2)
    merge_w = (row(mix_norm), w_merge_gate.astype(BF16), row(b_merge_gate), w_branch.astype(BF16),
               w_out.astype(BF16), row(ffn_norm), w_r, b_r)
    final_w = (row(ple_norm), w_ple_gate.astype(BF16), w_ple_proj.astype(BF16), row(final_norm))
    p2 = p.reshape(n, -1)

    h1, hn, rf, ri, cnt = _merge(x2, fox, ro.reshape(n, RET_V_W), *merge_w, tm, 0, n)

    a = n * TOP_K
    nb = -(-(a + N_EXPERTS * (bm - 1)) // bm)
    e_idx = ri[:, :TOP_K]
    rank = ri[:, TOP_K:2 * TOP_K]
    counts = cnt[0, :N_EXPERTS].astype(jnp.int32)
    padded = (counts + bm - 1) // bm * bm
    pad_end = jnp.cumsum(padded)
    first_row = pad_end - counts
    dest_t = (first_row[e_idx] + rank).T
    block_start = jnp.arange(nb, dtype=jnp.int32) * bm
    block_e = jnp.minimum(jnp.sum(block_start[:, None] >= pad_end[None, :], axis=1),
                          N_EXPERTS - 1).astype(jnp.int32)
    valid = jnp.where(block_start < pad_end[-1],
                      jnp.clip(block_start + bm - first_row[block_e], 0, bm), 0).astype(jnp.int32)

    x_buf = _sc_dispatch(hn, dest_t, nb * bm)
    y_buf = _experts(block_e, valid, x_buf, w_gate_up, bgu, w_down, b_down.reshape(ne, 1, dd), bm)

    ng = n // groups
    out = None
    for g in range(groups):
        idx = dest_t[:, g * ng:(g + 1) * ng].reshape(1, ng * TOP_K)
        yg = _sc_gather(y_buf, idx).reshape(d // 2 // SC_COLS, TOP_K, ng, SC_COLS)
        out = _final(h1, yg, rf, p2, *final_w, tm, g * (ng // tm), out)
    return out.reshape(b, s, d)


def kernel(x, p, mix_norm, w_in, b_forget, w_branch, w_merge_gate, b_merge_gate, w_out, ffn_norm, w_router,
           b_router, w_gate_up, b_gate_up, w_down, b_down, ple_norm, w_ple_gate, w_ple_proj, final_norm):
    depth = p.shape[0]
    assert depth == 1, "the final norm is fused into the (single) layer"
    return _layer(x, p[0], mix_norm[0], w_in[0], b_forget[0], w_branch[0], w_merge_gate[0], b_merge_gate[0],
                  w_out[0], ffn_norm[0], w_router[0], b_router[0], w_gate_up[0], b_gate_up[0], w_down[0],
                  b_down[0], ple_norm[0], w_ple_gate[0], w_ple_proj[0], final_norm,
                  tm=1024, tq=512, chunk=256, bm=1024, groups=4)
```

```python
import functools

import numpy as np
import jax
import jax.numpy as jnp
from jax import lax
from jax.experimental import pallas as pl
from jax.experimental.pallas import tpu as pltpu
from jax.experimental.pallas import tpu_sc as plsc

FOX_HEADS = 8
FOX_HEAD_DIM = 64
RET_HEADS = 4
RET_KEY_DIM = 64
RET_VAL_DIM = 128
ROPE_BASE = 10000.0
N_EXPERTS = 32
TOP_K = 4
SWIGLU_LIMIT = 7.0
SWIGLU_ALPHA = 1.702
EPS = 1e-6

LANES = 128
FOX_W = FOX_HEADS * FOX_HEAD_DIM
RET_QK_W = RET_HEADS * RET_KEY_DIM
RET_V_W = RET_HEADS * RET_VAL_DIM
VMEM_LIMIT = 56 * 1024 * 1024
HIGH_HALF = -65536
LOG2_E = 1.4426950408889634
SC_WINDOW = 128
SC_COLS = 256

F32 = jnp.float32
BF16 = jnp.bfloat16


def _rms(x, g):
    return x * lax.rsqrt(jnp.mean(x * x, axis=-1, keepdims=True) + EPS) * g


def _sigmoid(x):
    return 1.0 / (1.0 + jnp.exp(-x))


def _dot(a, b):
    return jnp.dot(a, b, preferred_element_type=F32)


def _dot_nt(a, b):
    return lax.dot_general(a, b, (((1,), (1,)), ((), ())), preferred_element_type=F32)


def _pack_pairs(v):
    bits = pltpu.bitcast(v.astype(BF16).astype(F32), jnp.int32)
    half = v.shape[1] // 2
    return lax.shift_right_logical(bits[:, :half], 16) | (bits[:, half:] & HIGH_HALF)


def _unpack_pairs(chunks):
    return jnp.concatenate([pltpu.bitcast(lax.shift_left(w, 16), F32) for w in chunks]
                           + [pltpu.bitcast(w & HIGH_HALF, F32) for w in chunks], axis=1)


def _store_chunks(ref, v, rows=slice(None)):
    for c in range(ref.shape[0]):
        ref[c, rows, :] = v[:, c * SC_COLS:(c + 1) * SC_COLS]


def _interleave(generators):
    live = []
    pending = list(generators)
    while pending or live:
        if pending:
            live.append(pending.pop(0))
        for g in list(live):
            try:
                next(g)
            except StopIteration:
                live.remove(g)


def _resident(shape):
    return pl.BlockSpec(shape, lambda *_: (0,) * len(shape), pipeline_mode=pl.Buffered(1))


def _params(n_axes):
    return pltpu.CompilerParams(dimension_semantics=("arbitrary",) * n_axes,
                                vmem_limit_bytes=VMEM_LIMIT)


C_MID = FOX_HEADS
C_LO = 2 * FOX_HEADS
C_ONE = 3 * FOX_HEADS
N_AUG = 3


def _split3(t):
    hi = t.astype(BF16).astype(F32)
    r = t - hi
    mid = r.astype(BF16).astype(F32)
    lo = (r - mid).astype(BF16).astype(F32)
    return hi + pltpu.roll(mid, C_MID, 1) + pltpu.roll(lo, C_LO, 1)


INPROJ_SUB = 256


def _inproj_kernel(x_ref, g_ref, w_ref, wvt_ref, wf_ref, bf_ref, cos_ref, sin_ref, pq_ref, pk_ref,
                   q_ref, k_ref, vt_ref, rq_ref, rk_ref, rv_ref, rg_ref, carry_scr, *, tiles_per_seq):
    i = pl.program_id(0)
    tm = x_ref.shape[0]
    sub = min(INPROJ_SUB, tm)

    @pl.when(i % tiles_per_seq == 0)
    def _():
        carry_scr[...] = jnp.zeros(carry_scr.shape, F32)

    lane = lax.broadcasted_iota(jnp.int32, (sub, LANES), 1)
    r = lax.broadcasted_iota(jnp.int32, (sub, sub), 0)
    c = lax.broadcasted_iota(jnp.int32, (sub, sub), 1)
    tri = jnp.where(c <= r, 1.0, 0.0).astype(BF16)
    own = lane < FOX_HEAD_DIM
    half = RET_KEY_DIM // 2
    lane_r = lax.broadcasted_iota(jnp.int32, (sub, RET_QK_W), 1)
    first = (lane_r % RET_KEY_DIM) < half
    carries = [carry_scr[...]]

    def sub_tile(t):
        rows = slice(t * sub, (t + 1) * sub)
        xn = _rms(x_ref[rows, :], g_ref[...]).astype(BF16)
        yield
        u = _dot(xn, w_ref[...])
        vt_ref[:, rows] = _dot_nt(wvt_ref[...], xn).astype(BF16)
        z = _dot(xn, wf_ref[...]) + bf_ref[...]
        yield
        lf = jnp.where(lane < FOX_HEADS, jnp.minimum(z, 0.0) - jnp.log1p(jnp.exp(-jnp.abs(z))), 0.0)
        ps = _dot(tri, _split3(lf).astype(BF16))
        yield
        assert len(carries) == t + 1
        cum = ps + pltpu.roll(ps, LANES - C_MID, 1) + pltpu.roll(ps, LANES - C_LO, 1)
        cum = jnp.where(lane < FOX_HEADS, cum, 0.0) + carries[t]
        carries.append(cum[sub - 1:sub, :])
        c3 = (_split3(cum * LOG2_E) + jnp.where(lane == C_ONE, 1.0, 0.0)).astype(BF16)
        aug_q = _dot(c3, pq_ref[...])
        aug_k = _dot(c3, pk_ref[...])

        o = 2 * FOX_W
        rq = u[:, o:o + RET_QK_W]; o += RET_QK_W
        rk = u[:, o:o + RET_QK_W]; o += RET_QK_W
        rv_ref[rows, :] = u[:, o:o + RET_V_W].astype(BF16); o += RET_V_W
        rg_ref[rows, :] = u[:, o:o + RET_V_W]

        cos = cos_ref[rows, :]
        sin = sin_ref[rows, :]

        def rot(v):
            partner = jnp.where(first, -pltpu.roll(v, RET_QK_W - half, 1), pltpu.roll(v, half, 1))
            return v * cos + partner * sin

        rq_ref[rows, :] = rot(rq).astype(BF16)
        rk_ref[rows, :] = (rot(rk) * (RET_KEY_DIM ** -0.5)).astype(BF16)
        yield
        fq = u[:, 0:FOX_W] * (FOX_HEAD_DIM ** -0.5 * LOG2_E)
        fk = u[:, FOX_W:2 * FOX_W]
        for src, aug, dst in ((fq, aug_q, q_ref), (fk, aug_k, k_ref)):
            for j in range(FOX_W // LANES):
                blk = src[:, j * LANES:(j + 1) * LANES]
                e0, e1 = 2 * j * LANES, (2 * j + 1) * LANES
                dst[rows, e0:e0 + LANES] = jnp.where(own, blk, aug[:, e0:e0 + LANES]).astype(BF16)
                dst[rows, e1:e1 + LANES] = jnp.where(own, pltpu.roll(blk, FOX_HEAD_DIM, 1),
                                                      aug[:, e1:e1 + LANES]).astype(BF16)

    _interleave(sub_tile(t) for t in range(tm // sub))
    carry_scr[...] = carries[-1]


def _placement():
    pq = np.zeros((LANES, FOX_HEADS * LANES), np.float32)
    pk = np.zeros((LANES, FOX_HEADS * LANES), np.float32)
    for h in range(FOX_HEADS):
        base = h * LANES + FOX_HEAD_DIM
        for part, src in enumerate((h, C_MID + h, C_LO + h)):
            pq[C_ONE, base + part] = 1.0
            pq[src, base + N_AUG + part] = 1.0
            pk[src, base + part] = -1.0
            pk[C_ONE, base + N_AUG + part] = 1.0
    return jnp.asarray(pq, BF16), jnp.asarray(pk, BF16)


def _inproj(x2, g, w_main, w_vt, w_f, b_f, cos_t, sin_t, seq, tm):
    n, d = x2.shape
    wn = w_main.shape[1]
    spt = seq // tm
    aw = FOX_HEADS * LANES
    pq, pk = _placement()
    row = lambda i: (i, 0)
    const = lambda i: (0, 0)
    pos = lambda i: (i % spt, 0)
    out_shape = [
        jax.ShapeDtypeStruct((n, aw), BF16), jax.ShapeDtypeStruct((n, aw), BF16),
        jax.ShapeDtypeStruct((FOX_W, n), BF16),
        jax.ShapeDtypeStruct((n, RET_QK_W), BF16), jax.ShapeDtypeStruct((n, RET_QK_W), BF16),
        jax.ShapeDtypeStruct((n, RET_V_W), BF16), jax.ShapeDtypeStruct((n, RET_V_W), F32),
    ]
    out_specs = [
        pl.BlockSpec((tm, aw), row), pl.BlockSpec((tm, aw), row),
        pl.BlockSpec((FOX_W, tm), lambda i: (0, i)),
        pl.BlockSpec((tm, RET_QK_W), row), pl.BlockSpec((tm, RET_QK_W), row),
        pl.BlockSpec((tm, RET_V_W), row), pl.BlockSpec((tm, RET_V_W), row),
    ]
    return pl.pallas_call(
        functools.partial(_inproj_kernel, tiles_per_seq=spt),
        grid=(n // tm,),
        in_specs=[
            pl.BlockSpec((tm, d), row), pl.BlockSpec((1, d), const),
            _resident((d, wn)), _resident((FOX_W, d)),
            _resident((d, LANES)), pl.BlockSpec((1, LANES), const),
            pl.BlockSpec((tm, RET_QK_W), pos), pl.BlockSpec((tm, RET_QK_W), pos),
            _resident((LANES, aw)), _resident((LANES, aw)),
        ],
        out_specs=out_specs,
        out_shape=out_shape,
        scratch_shapes=[pltpu.VMEM((1, LANES), F32)],
        compiler_params=_params(1),
        name="inproj",
    )(x2, g, w_main, w_vt, w_f, b_f, cos_t, sin_t, pq, pk)


def _fox_kernel(q_ref, k_ref, vt_ref, o_ref, *, tq):
    seq = q_ref.shape[0]
    key = lax.broadcasted_iota(jnp.int32, (tq, tq), 0)
    qry = lax.broadcasted_iota(jnp.int32, (tq, tq), 1)
    vrow = lax.broadcasted_iota(jnp.int32, (LANES, tq), 0)
    one = jnp.ones((), BF16)
    items = [(qi, j) for qi in range(seq // tq) for j in range(qi + 1)]

    def logits(qi, j):
        out = []
        for a in range(2):
            sl = slice(a * LANES, (a + 1) * LANES)
            s = _dot_nt(k_ref[j * tq:(j + 1) * tq, sl], q_ref[qi * tq:(qi + 1) * tq, sl])
            out.append(jnp.where(key <= qry, s, -jnp.inf) if j == qi else s)
        return out

    s_cur = logits(*items[0])
    carry = None
    for w, (qi, j) in enumerate(items):
        s_next = logits(*items[w + 1]) if w + 1 < len(items) else None
        if j == 0:
            carry = [(jnp.full((1, tq), -jnp.inf, F32), jnp.zeros((LANES, tq), F32)) for _ in range(2)]
        v = vt_ref[:, j * tq:(j + 1) * tq]
        for a in range(2):
            m_prev, acc = carry[a]
            m_new = jnp.maximum(m_prev, jnp.max(s_cur[a], axis=0, keepdims=True))
            alpha = jnp.exp2(m_prev - m_new)
            p = jnp.exp2(s_cur[a] - m_new).astype(BF16)
            va = jnp.where((vrow // FOX_HEAD_DIM) == a, v, one)
            carry[a] = (m_new, alpha * acc + _dot(va, p))
        if j == qi:
            acc0, acc1 = carry[0][1], carry[1][1]
            ot = jnp.where(vrow < FOX_HEAD_DIM, acc0 / acc0[FOX_HEAD_DIM:FOX_HEAD_DIM + 1, :],
                           acc1 / acc1[0:1, :])
            o_ref[qi * tq:(qi + 1) * tq, :] = ot.T.astype(o_ref.dtype)
        s_cur = s_next


def _fox(q_aug, k_aug, v_t, batch, seq, tq):
    n = q_aug.shape[0]
    pairs = FOX_HEADS // 2
    kern = functools.partial(_fox_kernel, tq=tq)
    return pl.pallas_call(
        kern,
        grid=(batch, pairs),
        in_specs=[
            pl.BlockSpec((seq, 2 * LANES), lambda i, h: (i, h)),
            pl.BlockSpec((seq, 2 * LANES), lambda i, h: (i, h)),
            pl.BlockSpec((LANES, seq), lambda i, h: (h, i)),
        ],
        out_specs=pl.BlockSpec((seq, LANES), lambda i, h: (i, h)),
        out_shape=jax.ShapeDtypeStruct((n, FOX_W), BF16),
        compiler_params=_params(2),
        name="fox_attention",
    )(q_aug, k_aug, v_t)


def _ret_kernel(lg_ref, q_ref, k_ref, v_ref, g_ref, o_ref, *, chunk):
    s = q_ref.shape[1]
    lane = lax.broadcasted_iota(jnp.int32, (chunk, LANES), 1)
    ri = lax.broadcasted_iota(jnp.int32, (chunk, chunk), 0)
    ci = lax.broadcasted_iota(jnp.int32, (chunk, chunk), 1)
    diff = (ri - ci).astype(F32)
    pos = lax.broadcasted_iota(jnp.int32, (chunk, 1), 0).astype(F32)

    def head(h):
        lg = lg_ref[h]
        mine = (lane // RET_KEY_DIM) == (h % 2)
        qk = slice((h // 2) * LANES, (h // 2 + 1) * LANES)
        vs = slice(h * RET_VAL_DIM, (h + 1) * RET_VAL_DIM)
        inner = jnp.where(diff >= 0, jnp.exp(jnp.maximum(diff, 0.0) * lg), 0.0)
        q_decay = jnp.exp((pos + 1.0) * lg)
        k_decay = jnp.exp((chunk - 1.0 - pos) * lg)
        chunk_decay = jnp.exp(jnp.full((1, 1), chunk, F32) * lg)
        state = jnp.zeros((LANES, RET_VAL_DIM), F32)
        for c in range(s // chunk):
            rows = slice(c * chunk, (c + 1) * chunk)
            q = jnp.where(mine, q_ref[0, rows, qk], jnp.zeros((), BF16))
            k = jnp.where(mine, k_ref[0, rows, qk], jnp.zeros((), BF16))
            v = v_ref[0, rows, vs]
            scores = _dot_nt(q, k)
            inter = _dot(q, state.astype(BF16))
            kd = (k.astype(F32) * k_decay).astype(BF16)
            update = lax.dot_general(kd, v, (((0,), (0,)), ((), ())), preferred_element_type=F32)
            yield
            o = _dot((scores * inner).astype(BF16), v) + inter * q_decay
            state = state * chunk_decay + update
            yield
            o = o * lax.rsqrt(jnp.mean(o * o, axis=-1, keepdims=True) + EPS)
            g = g_ref[0, rows, vs]
            o_ref[0, rows, vs] = (o * (g * _sigmoid(g))).astype(o_ref.dtype)

    _interleave(head(h) for h in range(RET_HEADS))


def _retention(lg, rq, rk, rv, rg, chunk):
    b, s, _ = rq.shape
    kern = functools.partial(_ret_kernel, chunk=chunk)
    qk_spec = pl.BlockSpec((1, s, RET_QK_W), lambda i, lg_ref: (i, 0, 0))
    v_spec = pl.BlockSpec((1, s, RET_V_W), lambda i, lg_ref: (i, 0, 0))
    return pl.pallas_call(
        kern,
        grid_spec=pltpu.PrefetchScalarGridSpec(
            num_scalar_prefetch=1,
            grid=(b,),
            in_specs=[qk_spec, qk_spec, v_spec, v_spec],
            out_specs=v_spec,
        ),
        out_shape=jax.ShapeDtypeStruct((b, s, RET_V_W), BF16),
        compiler_params=_params(1),
        name="retention",
    )(lg, rq, rk, rv, rg)


MERGE_SUB = 256


def _merge_kernel(x_ref, fox_ref, ro_ref, gmix_ref, wmg_ref, bmg_ref, wb_ref, wout_ref, gffn_ref,
                  wr_ref, br_ref, h1_ref, hn_ref, rf_ref, ri_ref, cnt_ref, base_scr):
    i = pl.program_id(0)
    tm, d = x_ref.shape
    sub = min(MERGE_SUB, tm)

    @pl.when(i == 0)
    def _():
        base_scr[...] = jnp.zeros(base_scr.shape, F32)

    lane = lax.broadcasted_iota(jnp.int32, (sub, LANES), 1)
    lane_f = lane.astype(F32)
    r = lax.broadcasted_iota(jnp.int32, (sub, sub), 0)
    c = lax.broadcasted_iota(jnp.int32, (sub, sub), 1)
    tri = jnp.where(c < r, 1.0, 0.0).astype(BF16)
    counts = [base_scr[...]]

    def sub_tile(t):
        rows = slice(t * sub, (t + 1) * sub)
        x = x_ref[rows, :]
        xn = _rms(x, gmix_ref[...]).astype(BF16)
        yield
        pre = _dot(xn, wmg_ref[...])
        yield
        gate = _sigmoid(pre + bmg_ref[...])
        yield
        pf = _dot(fox_ref[rows, :], wb_ref[0])
        pr = _dot(ro_ref[rows, :], wb_ref[1])
        yield
        merged = (gate[:, :d] * pf + gate[:, d:] * pr).astype(BF16)
        yield
        h1 = x + _dot(merged, wout_ref[...])
        yield
        h1_ref[rows, :] = h1
        hn = _rms(h1, gffn_ref[...]).astype(BF16)
        _store_chunks(hn_ref, _pack_pairs(hn), rows)
        yield
        logits = _dot(hn, wr_ref[...]) + br_ref[...]
        yield
        vals, idxs = [], []
        cur = logits
        for _ in range(TOP_K):
            m = jnp.max(cur, axis=-1, keepdims=True)
            idx = jnp.min(jnp.where(cur == m, lane_f, float(LANES)), axis=-1, keepdims=True)
            vals.append(m)
            idxs.append(idx)
            cur = jnp.where(lane_f == idx, -jnp.inf, cur)
        exps = [jnp.exp(v - vals[0]) for v in vals]
        denom = exps[0] + exps[1] + exps[2] + exps[3]
        onehot = jnp.zeros(logits.shape, F32)
        for idx in idxs:
            onehot = onehot + jnp.where(lane_f == idx, 1.0, 0.0)
        yield
        assert len(counts) == t + 1
        before = _dot(tri, onehot.astype(BF16)) + counts[t]
        counts.append(counts[t] + jnp.sum(onehot, axis=0, keepdims=True))
        rf = jnp.zeros(logits.shape, F32)
        ri = jnp.zeros(logits.shape, F32)
        for j in range(TOP_K):
            rank = jnp.sum(jnp.where(lane_f == idxs[j], before, 0.0), axis=-1, keepdims=True)
            rf = jnp.where(lane == j, exps[j] / denom, rf)
            ri = jnp.where(lane == j, idxs[j], ri)
            ri = jnp.where(lane == TOP_K + j, rank, ri)
        rf_ref[rows, :] = rf
        ri_ref[rows, :] = ri[:, :2 * TOP_K].astype(jnp.int32)

    _interleave(sub_tile(t) for t in range(tm // sub))
    base_scr[...] = counts[-1]
    cnt_ref[...] = counts[-1]


def _merge(x2, fox, ro, gmix, wmg, bmg, wb, wout, gffn, wr, br, tm, first_tile, n):
    d = x2.shape[1]
    row = lambda i: (i, 0)
    src = lambda i: (i + first_tile, 0)
    const = lambda i: (0, 0)
    return pl.pallas_call(
        _merge_kernel,
        grid=(n // tm,),
        in_specs=[
            pl.BlockSpec((tm, d), src), pl.BlockSpec((tm, FOX_W), src), pl.BlockSpec((tm, RET_V_W), src),
            pl.BlockSpec((1, d), const), _resident((d, 2 * d)), pl.BlockSpec((1, 2 * d), const),
            _resident((2, FOX_W, d)), _resident((d, d)),
            pl.BlockSpec((1, d), const), _resident((d, LANES)), pl.BlockSpec((1, LANES), const),
        ],
        out_specs=[
            pl.BlockSpec((tm, d), row), pl.BlockSpec((d // 2 // SC_COLS, tm, SC_COLS), lambda i: (0, i, 0)),
            pl.BlockSpec((tm, LANES), row), pl.BlockSpec((tm, 2 * TOP_K), row),
            pl.BlockSpec((1, LANES), const),
        ],
        out_shape=[
            jax.ShapeDtypeStruct((n, d), F32), jax.ShapeDtypeStruct((d // 2 // SC_COLS, n, SC_COLS), jnp.int32),
            jax.ShapeDtypeStruct((n, LANES), F32), jax.ShapeDtypeStruct((n, 2 * TOP_K), jnp.int32),
            jax.ShapeDtypeStruct((1, LANES), F32),
        ],
        scratch_shapes=[pltpu.VMEM((1, LANES), F32)],
        compiler_params=_params(1),
        name="merge_router",
    )(x2, fox, ro, gmix, wmg, bmg, wb, wout, gffn, wr, br)


def _sc_mesh():
    return plsc.VectorSubcoreMesh(core_axis_name="core", subcore_axis_name="subcore")


def _sc_dispatch(rows, dest_t, n_out):
    chunks, n, w = rows.shape

    @functools.partial(pl.kernel, out_type=jax.ShapeDtypeStruct((chunks, n_out, w), rows.dtype),
                       mesh=_sc_mesh(), scratch_types=[], name="moe_dispatch")
    def run(x_hbm, i_hbm, o_hbm):
        for c in range(chunks):
            def body(x_vmem, i_vmem, c=c):
                for j in range(TOP_K):
                    pltpu.sync_copy(x_vmem, o_hbm.at[c].at[i_vmem.at[j]])

            pltpu.emit_pipeline(
                body,
                grid=(n // SC_WINDOW,),
                in_specs=[pl.BlockSpec((SC_WINDOW, w), lambda i: (i, 0)),
                          pl.BlockSpec((TOP_K, SC_WINDOW), lambda i: (0, i))],
                out_specs=[],
                core_axis_name=("core", "subcore"),
                dimension_semantics=(pltpu.PARALLEL,),
            )(x_hbm.at[c], i_hbm)

    return run(rows, dest_t)


def _sc_gather(table, idx):
    chunks, _, w = table.shape
    m = idx.shape[1]

    @functools.partial(pl.kernel, out_type=jax.ShapeDtypeStruct((chunks, m, w), table.dtype),
                       mesh=_sc_mesh(), scratch_types=[], name="moe_combine_gather")
    def run(t_hbm, i_hbm, o_hbm):
        for c in range(chunks):
            def body(i_vmem, o_vmem, c=c):
                pltpu.sync_copy(t_hbm.at[c].at[i_vmem.at[0]], o_vmem)

            pltpu.emit_pipeline(
                body,
                grid=(m // SC_WINDOW,),
                in_specs=[pl.BlockSpec((1, SC_WINDOW), lambda i: (0, i))],
                out_specs=[pl.BlockSpec((SC_WINDOW, w), lambda i: (i, 0))],
                core_axis_name=("core", "subcore"),
                dimension_semantics=(pltpu.PARALLEL,),
            )(i_hbm, o_hbm.at[c])

    return run(table, idx)


GLU_GROUP = 2 * LANES
EXPERT_SUB = 256


def _expert_kernel(be_ref, valid_ref, first_ref, slot_ref, next_ref, x_ref, wgu_hbm, bgu_ref, wd_hbm, bd_ref,
                   perm_ref, y_ref, wgu_f32, wd_f32, wgu_scr, wd_scr, sem):
    i = pl.program_id(0)
    f2 = wgu_scr.shape[1]
    bm = x_ref.shape[1]
    sub = min(EXPERT_SUB, bm)
    valid = valid_ref[i]

    def weight_copies(e, slot):
        return (pltpu.make_async_copy(wgu_hbm.at[e], wgu_f32.at[slot], sem.at[0, slot]),
                pltpu.make_async_copy(wd_hbm.at[e], wd_f32.at[slot], sem.at[1, slot]))

    @pl.when(first_ref[i] == 1)
    def _():
        slot = slot_ref[i]

        @pl.when(i == 0)
        def _():
            for cp in weight_copies(be_ref[i], slot):
                cp.start()

        for cp in weight_copies(be_ref[i], slot):
            cp.wait()

        @pl.when(next_ref[i] >= 0)
        def _():
            for cp in weight_copies(next_ref[i], 1 - slot):
                cp.start()

        for b in range(f2 // GLU_GROUP):
            cols = slice(b * GLU_GROUP, (b + 1) * GLU_GROUP)
            wgu_scr[:, cols] = _dot(wgu_f32[slot, :, cols].astype(BF16), perm_ref[...]).astype(BF16)
        wd_scr[...] = wd_f32[slot].astype(BF16)

    def sub_block(t, sub):
        rows = slice(t * sub, (t + 1) * sub)
        x = _unpack_pairs([x_ref[c, rows, :] for c in range(x_ref.shape[0])]).astype(BF16)
        yield
        gu = _dot(x, wgu_scr[...])
        yield
        gu = gu + bgu_ref[0]
        acts = []
        for b in range(f2 // GLU_GROUP):
            glu = jnp.minimum(gu[:, b * GLU_GROUP:b * GLU_GROUP + LANES], SWIGLU_LIMIT)
            lin = jnp.clip(gu[:, b * GLU_GROUP + LANES:(b + 1) * GLU_GROUP], -SWIGLU_LIMIT, SWIGLU_LIMIT)
            acts.append((glu * _sigmoid(SWIGLU_ALPHA * glu) * (lin + 1.0)).astype(BF16))
        act = jnp.concatenate(acts, axis=1)
        yield
        y = _dot(act, wd_scr[...])
        yield
        _store_chunks(y_ref, _pack_pairs(y + bd_ref[0]), rows)

    @pl.when(valid == bm)
    def _():
        _interleave(sub_block(t, sub) for t in range(bm // sub))

    for t in range(bm // sub):
        @pl.when(jnp.logical_and(valid < bm, valid > bm - (t + 1) * sub))
        def _(t=t):
            _interleave([sub_block(t, sub)])


def _experts(block_e, valid, x_buf, wgu, bgu, wd, bd, bm):
    p = x_buf.shape[1]
    f, d = wd.shape[1:]
    prev_e = jnp.concatenate([jnp.full((1,), -1, jnp.int32), block_e[:-1]])
    first = jnp.logical_and(valid > 0, block_e != prev_e).astype(jnp.int32)
    slot = ((jnp.cumsum(first) - 1) % 2).astype(jnp.int32)
    group_e = jnp.where(first == 1, block_e, N_EXPERTS)
    later = jnp.flip(lax.cummin(jnp.flip(jnp.concatenate([group_e[1:], jnp.full((1,), N_EXPERTS, jnp.int32)]))))
    nxt = jnp.where(later < N_EXPERTS, later, -1).astype(jnp.int32)
    perm = np.zeros((GLU_GROUP, GLU_GROUP), np.float32)
    for c in range(LANES):
        perm[2 * c, c] = 1.0
        perm[2 * c + 1, LANES + c] = 1.0
    return pl.pallas_call(
        _expert_kernel,
        grid_spec=pltpu.PrefetchScalarGridSpec(
            num_scalar_prefetch=5,
            grid=(p // bm,),
            in_specs=[
                pl.BlockSpec((d // 2 // SC_COLS, bm, SC_COLS), lambda i, be, *_: (0, i, 0)),
                pl.BlockSpec(memory_space=pl.ANY),
                pl.BlockSpec((1, 1, 2 * f), lambda i, be, *_: (be[i], 0, 0)),
                pl.BlockSpec(memory_space=pl.ANY),
                pl.BlockSpec((1, 1, d), lambda i, be, *_: (be[i], 0, 0)),
                pl.BlockSpec((GLU_GROUP, GLU_GROUP), lambda i, be, *_: (0, 0)),
            ],
            out_specs=pl.BlockSpec((d // 2 // SC_COLS, bm, SC_COLS), lambda i, be, *_: (0, i, 0)),
            scratch_shapes=[pltpu.VMEM((2, d, 2 * f), F32), pltpu.VMEM((2, f, d), F32),
                            pltpu.VMEM((d, 2 * f), BF16), pltpu.VMEM((f, d), BF16),
                            pltpu.SemaphoreType.DMA((2, 2))],
        ),
        out_shape=jax.ShapeDtypeStruct((d // 2 // SC_COLS, p, SC_COLS), jnp.int32),
        compiler_params=_params(1),
        name="expert_ffn",
    )(block_e, valid, first, slot, nxt, x_buf, wgu, bgu, wd, bd, jnp.asarray(perm, BF16))


FINAL_SUB = 256


def _final_kernel(h1_ref, yg_ref, rf_ref, p_ref, gple_ref, wpg_ref, wpp_ref, gfin_ref, *rest):
    o_ref = rest[-1]
    tm = h1_ref.shape[0]
    sub = min(FINAL_SUB, tm)

    def sub_tile(t):
        rows = slice(t * sub, (t + 1) * sub)
        rf = rf_ref[rows, :]
        h2 = h1_ref[rows, :]
        for j in range(TOP_K):
            h2 = h2 + _unpack_pairs([yg_ref[c, j, rows, :] for c in range(yg_ref.shape[0])]) * rf[:, j:j + 1]
        hn = _rms(h2, gple_ref[...]).astype(BF16)
        yield
        pre = _dot(hn, wpg_ref[...])
        proj = _dot(p_ref[rows, :].astype(BF16), wpp_ref[...])
        yield
        h3 = h2 + _sigmoid(pre) * proj
        o_ref[rows, :] = _rms(h3, gfin_ref[...])

    _interleave(sub_tile(t) for t in range(tm // sub))


def _final(h1, yg, rf, p2, gple, wpg, wpp, gfin, tm, first_tile, prev_out):
    n_total, d = h1.shape
    n = yg.shape[2]
    pd = p2.shape[1]
    dst = lambda i: (i + first_tile, 0)
    const = lambda i: (0, 0)
    in_specs = [
        pl.BlockSpec((tm, d), dst), pl.BlockSpec((d // 2 // SC_COLS, TOP_K, tm, SC_COLS), lambda i: (0, 0, i, 0)),
        pl.BlockSpec((tm, LANES), dst), pl.BlockSpec((tm, pd), dst),
        pl.BlockSpec((1, d), const), pl.BlockSpec((d, d), const), pl.BlockSpec((pd, d), const),
        pl.BlockSpec((1, d), const),
    ]
    args = [h1, yg, rf, p2, gple, wpg, wpp, gfin]
    aliases = {}
    if prev_out is not None:
        in_specs.append(pl.BlockSpec(memory_space=pl.ANY))
        aliases = {len(args): 0}
        args.append(prev_out)
    return pl.pallas_call(
        _final_kernel,
        grid=(n // tm,),
        in_specs=in_specs,
        out_specs=pl.BlockSpec((tm, d), dst),
        out_shape=jax.ShapeDtypeStruct((n_total, d), F32),
        input_output_aliases=aliases,
        compiler_params=_params(1),
        name="combine_ple_norm",
    )(*args)


def _layer(h, p, mix_norm, w_in, b_forget, w_branch, w_merge_gate, b_merge_gate, w_out, ffn_norm,
           w_router, b_router, w_gate_up, b_gate_up, w_down, b_down, ple_norm, w_ple_gate, w_ple_proj,
           final_norm, *, tm, tq, chunk, bm, groups):
    b, s, d = h.shape
    n = b * s
    x2 = h.reshape(n, d)
    row = lambda t: t.reshape(1, -1)

    c0 = 3 * FOX_W
    w_main = jnp.concatenate([w_in[:, :2 * FOX_W], w_in[:, c0 + FOX_HEADS:]], axis=1).astype(BF16)
    w_vt = w_in[:, 2 * FOX_W:c0].T.astype(BF16)
    w_f = jnp.pad(w_in[:, c0:c0 + FOX_HEADS], ((0, 0), (0, LANES - FOX_HEADS))).astype(BF16)
    b_f = jnp.pad(b_forget, (0, LANES - FOX_HEADS)).reshape(1, LANES)
    half = RET_KEY_DIM // 2
    inv = ROPE_BASE ** (-jnp.arange(half, dtype=F32) / half)
    ang = jnp.arange(s).astype(F32)[:, None] * inv[None, :]
    cos_t = jnp.tile(jnp.cos(ang), (1, RET_QK_W // half))
    sin_t = jnp.tile(jnp.sin(ang), (1, RET_QK_W // half))

    q_aug, k_aug, v_t, rq, rk, rv, rg = _inproj(x2, row(mix_norm), w_main, w_vt, w_f, b_f, cos_t, sin_t, s, tm)
    fox = _fox(q_aug, k_aug, v_t, b, s, tq)

    lg = jnp.log1p(-jnp.exp2(-5.0 - jnp.arange(RET_HEADS, dtype=F32)))
    ro = _retention(lg, rq.reshape(b, s, RET_QK_W), rk.reshape(b, s, RET_QK_W),
                    rv.reshape(b, s, RET_V_W), rg.reshape(b, s, RET_V_W), chunk)

    w_r = jnp.pad(w_router, ((0, 0), (0, LANES - N_EXPERTS))).astype(BF16)
    b_r = jnp.concatenate([b_router, jnp.full((LANES - N_EXPERTS,), -1e30, F32)]).reshape(1, LANES)
    ne, dd, f2 = w_gate_up.shape
    bgu = b_gate_up.reshape(ne, f2 // GLU_GROUP, LANES, 2).transpose(0, 1, 3, 2).reshape(ne, 1, f2)
    merge_w = (row(mix_norm), w_merge_gate.astype(BF16), row(b_merge_gate), w_branch.astype(BF16),
               w_out.astype(BF16), row(ffn_norm), w_r, b_r)
    final_w = (row(ple_norm), w_ple_gate.astype(BF16), w_ple_proj.astype(BF16), row(final_norm))
    p2 = p.reshape(n, -1)

    h1, hn, rf, ri, cnt = _merge(x2, fox, ro.reshape(n, RET_V_W), *merge_w, tm, 0, n)

    a = n * TOP_K
    nb = -(-(a + N_EXPERTS * (bm - 1)) // bm)
    e_idx = ri[:, :TOP_K]
    rank = ri[:, TOP_K:2 * TOP_K]
    counts = cnt[0, :N_EXPERTS].astype(jnp.int32)
    padded = (counts + bm - 1) // bm * bm
    pad_end = jnp.cumsum(padded)
    first_row = pad_end - counts
    dest_t = (first_row[e_idx] + rank).T
    block_start = jnp.arange(nb, dtype=jnp.int32) * bm
    block_e = jnp.minimum(jnp.sum(block_start[:, None] >= pad_end[None, :], axis=1),
                          N_EXPERTS - 1).astype(jnp.int32)
    valid = jnp.where(block_start < pad_end[-1],
                      jnp.clip(block_start + bm - first_row[block_e], 0, bm), 0).astype(jnp.int32)

    x_buf = _sc_dispatch(hn, dest_t, nb * bm)
    y_buf = _experts(block_e, valid, x_buf, w_gate_up, bgu, w_down, b_down.reshape(ne, 1, dd), bm)

    ng = n // groups
    out = None
    for g in range(groups):
        idx = dest_t[:, g * ng:(g + 1) * ng].reshape(1, ng * TOP_K)
        yg = _sc_gather(y_buf, idx).reshape(d // 2 // SC_COLS, TOP_K, ng, SC_COLS)
        out = _final(h1, yg, rf, p2, *final_w, tm, g * (ng // tm), out)
    return out.reshape(b, s, d)


def kernel(x, p, mix_norm, w_in, b_forget, w_branch, w_merge_gate, b_merge_gate, w_out, ffn_norm, w_router,
           b_router, w_gate_up, b_gate_up, w_down, b_down, ple_norm, w_ple_gate, w_ple_proj, final_norm):
    depth = p.shape[0]
    assert depth == 1, "the final norm is fused into the (single) layer"
    return _layer(x, p[0], mix_norm[0], w_in[0], b_forget[0], w_branch[0], w_merge_gate[0], b_merge_gate[0],
                  w_out[0], ffn_norm[0], w_router[0], b_router[0], w_gate_up[0], b_gate_up[0], w_down[0],
                  b_down[0], ple_norm[0], w_ple_gate[0], w_ple_proj[0], final_norm,
                  tm=1024, tq=512, chunk=256, bm=1024, groups=4)
```

```python
import functools

import numpy as np
import jax
import jax.numpy as jnp
from jax import lax
from jax.experimental import pallas as pl
from jax.experimental.pallas import tpu as pltpu
from jax.experimental.pallas import tpu_sc as plsc

FOX_HEADS = 8
FOX_HEAD_DIM = 64
RET_HEADS = 4
RET_KEY_DIM = 64
RET_VAL_DIM = 128
ROPE_BASE = 10000.0
N_EXPERTS = 32
TOP_K = 4
SWIGLU_LIMIT = 7.0
SWIGLU_ALPHA = 1.702
EPS = 1e-6

LANES = 128
FOX_W = FOX_HEADS * FOX_HEAD_DIM
RET_QK_W = RET_HEADS * RET_KEY_DIM
RET_V_W = RET_HEADS * RET_VAL_DIM
VMEM_LIMIT = 56 * 1024 * 1024
HIGH_HALF = -65536
LOG2_E = 1.4426950408889634
SC_WINDOW = 128
SC_COLS = 256

F32 = jnp.float32
BF16 = jnp.bfloat16


def _rms(x, g):
    return x * lax.rsqrt(jnp.mean(x * x, axis=-1, keepdims=True) + EPS) * g


def _sigmoid(x):
    return 1.0 / (1.0 + jnp.exp(-x))


def _dot(a, b):
    return jnp.dot(a, b, preferred_element_type=F32)


def _dot_nt(a, b):
    return lax.dot_general(a, b, (((1,), (1,)), ((), ())), preferred_element_type=F32)


def _pack_pairs(v):
    bits = pltpu.bitcast(v.astype(BF16).astype(F32), jnp.int32)
    half = v.shape[1] // 2
    return lax.shift_right_logical(bits[:, :half], 16) | (bits[:, half:] & HIGH_HALF)


def _unpack_pairs(chunks):
    return jnp.concatenate([pltpu.bitcast(lax.shift_left(w, 16), F32) for w in chunks]
                           + [pltpu.bitcast(w & HIGH_HALF, F32) for w in chunks], axis=1)


def _store_chunks(ref, v, rows=slice(None)):
    for c in range(ref.shape[0]):
        ref[c, rows, :] = v[:, c * SC_COLS:(c + 1) * SC_COLS]


def _interleave(generators):
    live = []
    pending = list(generators)
    while pending or live:
        if pending:
            live.append(pending.pop(0))
        for g in list(live):
            try:
                next(g)
            except StopIteration:
                live.remove(g)


def _resident(shape):
    return pl.BlockSpec(shape, lambda *_: (0,) * len(shape), pipeline_mode=pl.Buffered(1))


def _params(n_axes):
    return pltpu.CompilerParams(dimension_semantics=("arbitrary",) * n_axes,
                                vmem_limit_bytes=VMEM_LIMIT)


C_MID = FOX_HEADS
C_LO = 2 * FOX_HEADS
C_ONE = 3 * FOX_HEADS
N_AUG = 3
AUG_W = 2 * N_AUG
K_AUG0 = FOX_HEADS * AUG_W


def _split3(t):
    hi = t.astype(BF16).astype(F32)
    r = t - hi
    mid = r.astype(BF16).astype(F32)
    lo = (r - mid).astype(BF16).astype(F32)
    return hi + pltpu.roll(mid, C_MID, 1) + pltpu.roll(lo, C_LO, 1)


INPROJ_SUB = 256


def _inproj_kernel(x_ref, g_ref, w_ref, wvt_ref, wf_ref, bf_ref, cos_ref, sin_ref, place_ref,
                   q_ref, k_ref, vt_ref, rq_ref, rk_ref, rv_ref, rg_ref, carry_scr, *, tiles_per_seq):
    i = pl.program_id(0)
    tm = x_ref.shape[0]
    sub = min(INPROJ_SUB, tm)

    @pl.when(i % tiles_per_seq == 0)
    def _():
        carry_scr[...] = jnp.zeros(carry_scr.shape, F32)

    lane = lax.broadcasted_iota(jnp.int32, (sub, LANES), 1)
    r = lax.broadcasted_iota(jnp.int32, (sub, sub), 0)
    c = lax.broadcasted_iota(jnp.int32, (sub, sub), 1)
    tri = jnp.where(c <= r, 1.0, 0.0).astype(BF16)
    own = lane < FOX_HEAD_DIM
    is_aug = jnp.logical_and(lane >= FOX_HEAD_DIM, lane < FOX_HEAD_DIM + AUG_W)
    half = RET_KEY_DIM // 2
    lane_r = lax.broadcasted_iota(jnp.int32, (sub, RET_QK_W), 1)
    first = (lane_r % RET_KEY_DIM) < half
    carries = [carry_scr[...]]

    def sub_tile(t):
        rows = slice(t * sub, (t + 1) * sub)
        xn = _rms(x_ref[rows, :], g_ref[...]).astype(BF16)
        yield
        u = _dot(xn, w_ref[...])
        vt_ref[:, rows] = _dot_nt(wvt_ref[...], xn).astype(BF16)
        z = _dot(xn, wf_ref[...]) + bf_ref[...]
        yield
        lf = jnp.where(lane < FOX_HEADS, jnp.minimum(z, 0.0) - jnp.log1p(jnp.exp(-jnp.abs(z))), 0.0)
        ps = _dot(tri, _split3(lf).astype(BF16))
        yield
        assert len(carries) == t + 1
        cum = ps + pltpu.roll(ps, LANES - C_MID, 1) + pltpu.roll(ps, LANES - C_LO, 1)
        cum = jnp.where(lane < FOX_HEADS, cum, 0.0) + carries[t]
        carries.append(cum[sub - 1:sub, :])
        c3 = (_split3(cum * LOG2_E) + jnp.where(lane == C_ONE, 1.0, 0.0)).astype(BF16)
        aug = _dot(c3, place_ref[...])

        o = 2 * FOX_W
        rq = u[:, o:o + RET_QK_W]; o += RET_QK_W
        rk = u[:, o:o + RET_QK_W]; o += RET_QK_W
        rv_ref[rows, :] = u[:, o:o + RET_V_W].astype(BF16); o += RET_V_W
        rg_ref[rows, :] = u[:, o:o + RET_V_W]

        cos = cos_ref[rows, :]
        sin = sin_ref[rows, :]

        def rot(v):
            partner = jnp.where(first, -pltpu.roll(v, RET_QK_W - half, 1), pltpu.roll(v, half, 1))
            return v * cos + partner * sin

        rq_ref[rows, :] = rot(rq).astype(BF16)
        rk_ref[rows, :] = (rot(rk) * (RET_KEY_DIM ** -0.5)).astype(BF16)
        yield
        fq = u[:, 0:FOX_W] * (FOX_HEAD_DIM ** -0.5 * LOG2_E)
        fk = u[:, FOX_W:2 * FOX_W]
        for src, aug0, dst in ((fq, 0, q_ref), (fk, K_AUG0, k_ref)):
            for h in range(FOX_HEADS):
                blk = src[:, (h // 2) * LANES:(h // 2 + 1) * LANES]
                if h % 2:
                    blk = pltpu.roll(blk, FOX_HEAD_DIM, 1)
                mine = pltpu.roll(aug, (FOX_HEAD_DIM - aug0 - AUG_W * h) % LANES, 1)
                slab = jnp.where(own, blk, jnp.where(is_aug, mine, 0.0))
                dst[rows, h * LANES:(h + 1) * LANES] = slab.astype(BF16)

    _interleave(sub_tile(t) for t in range(tm // sub))
    carry_scr[...] = carries[-1]


def _placement():
    place = np.zeros((LANES, LANES), np.float32)
    for h in range(FOX_HEADS):
        for part, src in enumerate((h, C_MID + h, C_LO + h)):
            place[C_ONE, AUG_W * h + part] = 1.0
            place[src, AUG_W * h + N_AUG + part] = 1.0
            place[src, K_AUG0 + AUG_W * h + part] = -1.0
            place[C_ONE, K_AUG0 + AUG_W * h + N_AUG + part] = 1.0
    return jnp.asarray(place, BF16)


def _inproj(x2, g, w_main, w_vt, w_f, b_f, cos_t, sin_t, seq, tm):
    n, d = x2.shape
    wn = w_main.shape[1]
    spt = seq // tm
    aw = FOX_HEADS * LANES
    place = _placement()
    row = lambda i: (i, 0)
    const = lambda i: (0, 0)
    pos = lambda i: (i % spt, 0)
    out_shape = [
        jax.ShapeDtypeStruct((n, aw), BF16), jax.ShapeDtypeStruct((n, aw), BF16),
        jax.ShapeDtypeStruct((FOX_W, n), BF16),
        jax.ShapeDtypeStruct((n, RET_QK_W), BF16), jax.ShapeDtypeStruct((n, RET_QK_W), BF16),
        jax.ShapeDtypeStruct((n, RET_V_W), BF16), jax.ShapeDtypeStruct((n, RET_V_W), F32),
    ]
    out_specs = [
        pl.BlockSpec((tm, aw), row), pl.BlockSpec((tm, aw), row),
        pl.BlockSpec((FOX_W, tm), lambda i: (0, i)),
        pl.BlockSpec((tm, RET_QK_W), row), pl.BlockSpec((tm, RET_QK_W), row),
        pl.BlockSpec((tm, RET_V_W), row), pl.BlockSpec((tm, RET_V_W), row),
    ]
    return pl.pallas_call(
        functools.partial(_inproj_kernel, tiles_per_seq=spt),
        grid=(n // tm,),
        in_specs=[
            pl.BlockSpec((tm, d), row), pl.BlockSpec((1, d), const),
            _resident((d, wn)), _resident((FOX_W, d)),
            _resident((d, LANES)), pl.BlockSpec((1, LANES), const),
            pl.BlockSpec((tm, RET_QK_W), pos), pl.BlockSpec((tm, RET_QK_W), pos),
            _resident((LANES, LANES)),
        ],
        out_specs=out_specs,
        out_shape=out_shape,
        scratch_shapes=[pltpu.VMEM((1, LANES), F32)],
        compiler_params=_params(1),
        name="inproj",
    )(x2, g, w_main, w_vt, w_f, b_f, cos_t, sin_t, place)


def _fox_kernel(q_ref, k_ref, vt_ref, o_ref, *, tq):
    seq = q_ref.shape[0]
    key = lax.broadcasted_iota(jnp.int32, (tq, tq), 0)
    qry = lax.broadcasted_iota(jnp.int32, (tq, tq), 1)
    vrow = lax.broadcasted_iota(jnp.int32, (LANES, tq), 0)
    one = jnp.ones((), BF16)
    items = [(qi, j) for qi in range(seq // tq) for j in range(qi + 1)]

    def logits(qi, j):
        out = []
        for a in range(2):
            sl = slice(a * LANES, (a + 1) * LANES)
            s = _dot_nt(k_ref[j * tq:(j + 1) * tq, sl], q_ref[qi * tq:(qi + 1) * tq, sl])
            out.append(jnp.where(key <= qry, s, -jnp.inf) if j == qi else s)
        return out

    s_cur = logits(*items[0])
    carry = None
    for w, (qi, j) in enumerate(items):
        s_next = logits(*items[w + 1]) if w + 1 < len(items) else None
        if j == 0:
            carry = [(jnp.full((1, tq), -jnp.inf, F32), jnp.zeros((LANES, tq), F32)) for _ in range(2)]
        v = vt_ref[:, j * tq:(j + 1) * tq]
        for a in range(2):
            m_prev, acc = carry[a]
            m_new = jnp.maximum(m_prev, jnp.max(s_cur[a], axis=0, keepdims=True))
            alpha = jnp.exp2(m_prev - m_new)
            p = jnp.exp2(s_cur[a] - m_new).astype(BF16)
            va = jnp.where((vrow // FOX_HEAD_DIM) == a, v, one)
            carry[a] = (m_new, alpha * acc + _dot(va, p))
        if j == qi:
            acc0, acc1 = carry[0][1], carry[1][1]
            ot = jnp.where(vrow < FOX_HEAD_DIM, acc0 / acc0[FOX_HEAD_DIM:FOX_HEAD_DIM + 1, :],
                           acc1 / acc1[0:1, :])
            o_ref[qi * tq:(qi + 1) * tq, :] = ot.T.astype(o_ref.dtype)
        s_cur = s_next


def _fox(q_aug, k_aug, v_t, batch, seq, tq):
    n = q_aug.shape[0]
    pairs = FOX_HEADS // 2
    kern = functools.partial(_fox_kernel, tq=tq)
    return pl.pallas_call(
        kern,
        grid=(batch, pairs),
        in_specs=[
            pl.BlockSpec((seq, 2 * LANES), lambda i, h: (i, h)),
            pl.BlockSpec((seq, 2 * LANES), lambda i, h: (i, h)),
            pl.BlockSpec((LANES, seq), lambda i, h: (h, i)),
        ],
        out_specs=pl.BlockSpec((seq, LANES), lambda i, h: (i, h)),
        out_shape=jax.ShapeDtypeStruct((n, FOX_W), BF16),
        compiler_params=_params(2),
        name="fox_attention",
    )(q_aug, k_aug, v_t)


def _ret_kernel(lg_ref, q_ref, k_ref, v_ref, g_ref, o_ref, *, chunk):
    s = q_ref.shape[1]
    lane = lax.broadcasted_iota(jnp.int32, (chunk, LANES), 1)
    ri = lax.broadcasted_iota(jnp.int32, (chunk, chunk), 0)
    ci = lax.broadcasted_iota(jnp.int32, (chunk, chunk), 1)
    diff = (ri - ci).astype(F32)
    pos = lax.broadcasted_iota(jnp.int32, (chunk, 1), 0).astype(F32)

    def head(h):
        lg = lg_ref[h]
        mine = (lane // RET_KEY_DIM) == (h % 2)
        qk = slice((h // 2) * LANES, (h // 2 + 1) * LANES)
        vs = slice(h * RET_VAL_DIM, (h + 1) * RET_VAL_DIM)
        inner = jnp.where(diff >= 0, jnp.exp(jnp.maximum(diff, 0.0) * lg), 0.0)
        q_decay = jnp.exp((pos + 1.0) * lg)
        k_decay = jnp.exp((chunk - 1.0 - pos) * lg)
        chunk_decay = jnp.exp(jnp.full((1, 1), chunk, F32) * lg)
        state = jnp.zeros((LANES, RET_VAL_DIM), F32)
        for c in range(s // chunk):
            rows = slice(c * chunk, (c + 1) * chunk)
            q = jnp.where(mine, q_ref[0, rows, qk], jnp.zeros((), BF16))
            k = jnp.where(mine, k_ref[0, rows, qk], jnp.zeros((), BF16))
            v = v_ref[0, rows, vs]
            scores = _dot_nt(q, k)
            inter = _dot(q, state.astype(BF16))
            kd = (k.astype(F32) * k_decay).astype(BF16)
            update = lax.dot_general(kd, v, (((0,), (0,)), ((), ())), preferred_element_type=F32)
            yield
            o = _dot((scores * inner).astype(BF16), v) + inter * q_decay
            state = state * chunk_decay + update
            yield
            o = o * lax.rsqrt(jnp.mean(o * o, axis=-1, keepdims=True) + EPS)
            g = g_ref[0, rows, vs]
            o_ref[0, rows, vs] = (o * (g * _sigmoid(g))).astype(o_ref.dtype)

    _interleave(head(h) for h in range(RET_HEADS))


def _retention(lg, rq, rk, rv, rg, chunk):
    b, s, _ = rq.shape
    kern = functools.partial(_ret_kernel, chunk=chunk)
    qk_spec = pl.BlockSpec((1, s, RET_QK_W), lambda i, lg_ref: (i, 0, 0))
    v_spec = pl.BlockSpec((1, s, RET_V_W), lambda i, lg_ref: (i, 0, 0))
    return pl.pallas_call(
        kern,
        grid_spec=pltpu.PrefetchScalarGridSpec(
            num_scalar_prefetch=1,
            grid=(b,),
            in_specs=[qk_spec, qk_spec, v_spec, v_spec],
            out_specs=v_spec,
        ),
        out_shape=jax.ShapeDtypeStruct((b, s, RET_V_W), BF16),
        compiler_params=_params(1),
        name="retention",
    )(lg, rq, rk, rv, rg)


MERGE_SUB = 256


def _merge_kernel(x_ref, fox_ref, ro_ref, gmix_ref, wmg_ref, bmg_ref, wb_ref, wout_ref, gffn_ref,
                  wr_ref, br_ref, h1_ref, hn_ref, rf_ref, ri_ref, cnt_ref, base_scr):
    i = pl.program_id(0)
    tm, d = x_ref.shape
    sub = min(MERGE_SUB, tm)

    @pl.when(i == 0)
    def _():
        base_scr[...] = jnp.zeros(base_scr.shape, F32)

    lane = lax.broadcasted_iota(jnp.int32, (sub, LANES), 1)
    lane_f = lane.astype(F32)
    r = lax.broadcasted_iota(jnp.int32, (sub, sub), 0)
    c = lax.broadcasted_iota(jnp.int32, (sub, sub), 1)
    tri = jnp.where(c < r, 1.0, 0.0).astype(BF16)
    counts = [base_scr[...]]

    def sub_tile(t):
        rows = slice(t * sub, (t + 1) * sub)
        x = x_ref[rows, :]
        xn = _rms(x, gmix_ref[...]).astype(BF16)
        yield
        pre = _dot(xn, wmg_ref[...])
        yield
        gate = _sigmoid(pre + bmg_ref[...])
        yield
        pf = _dot(fox_ref[rows, :], wb_ref[0])
        pr = _dot(ro_ref[rows, :], wb_ref[1])
        yield
        merged = (gate[:, :d] * pf + gate[:, d:] * pr).astype(BF16)
        yield
        h1 = x + _dot(merged, wout_ref[...])
        yield
        h1_ref[rows, :] = h1
        hn = _rms(h1, gffn_ref[...]).astype(BF16)
        _store_chunks(hn_ref, _pack_pairs(hn), rows)
        yield
        logits = _dot(hn, wr_ref[...]) + br_ref[...]
        yield
        vals, idxs = [], []
        cur = logits
        for _ in range(TOP_K):
            m = jnp.max(cur, axis=-1, keepdims=True)
            idx = jnp.min(jnp.where(cur == m, lane_f, float(LANES)), axis=-1, keepdims=True)
            vals.append(m)
            idxs.append(idx)
            cur = jnp.where(lane_f == idx, -jnp.inf, cur)
        exps = [jnp.exp(v - vals[0]) for v in vals]
        denom = exps[0] + exps[1] + exps[2] + exps[3]
        onehot = jnp.zeros(logits.shape, F32)
        for idx in idxs:
            onehot = onehot + jnp.where(lane_f == idx, 1.0, 0.0)
        yield
        assert len(counts) == t + 1
        before = _dot(tri, onehot.astype(BF16)) + counts[t]
        counts.append(counts[t] + jnp.sum(onehot, axis=0, keepdims=True))
        rf = jnp.zeros(logits.shape, F32)
        ri = jnp.zeros(logits.shape, F32)
        for j in range(TOP_K):
            rank = jnp.sum(jnp.where(lane_f == idxs[j], before, 0.0), axis=-1, keepdims=True)
            rf = jnp.where(lane == j, exps[j] / denom, rf)
            ri = jnp.where(lane == j, idxs[j], ri)
            ri = jnp.where(lane == TOP_K + j, rank, ri)
        rf_ref[rows, :] = rf
        ri_ref[rows, :] = ri[:, :2 * TOP_K].astype(jnp.int32)

    _interleave(sub_tile(t) for t in range(tm // sub))
    base_scr[...] = counts[-1]
    cnt_ref[...] = counts[-1]


def _merge(x2, fox, ro, gmix, wmg, bmg, wb, wout, gffn, wr, br, tm, first_tile, n):
    d = x2.shape[1]
    row = lambda i: (i, 0)
    src = lambda i: (i + first_tile, 0)
    const = lambda i: (0, 0)
    return pl.pallas_call(
        _merge_kernel,
        grid=(n // tm,),
        in_specs=[
            pl.BlockSpec((tm, d), src), pl.BlockSpec((tm, FOX_W), src), pl.BlockSpec((tm, RET_V_W), src),
            pl.BlockSpec((1, d), const), _resident((d, 2 * d)), pl.BlockSpec((1, 2 * d), const),
            _resident((2, FOX_W, d)), _resident((d, d)),
            pl.BlockSpec((1, d), const), _resident((d, LANES)), pl.BlockSpec((1, LANES), const),
        ],
        out_specs=[
            pl.BlockSpec((tm, d), row), pl.BlockSpec((d // 2 // SC_COLS, tm, SC_COLS), lambda i: (0, i, 0)),
            pl.BlockSpec((tm, LANES), row), pl.BlockSpec((tm, 2 * TOP_K), row),
            pl.BlockSpec((1, LANES), const),
        ],
        out_shape=[
            jax.ShapeDtypeStruct((n, d), F32), jax.ShapeDtypeStruct((d // 2 // SC_COLS, n, SC_COLS), jnp.int32),
            jax.ShapeDtypeStruct((n, LANES), F32), jax.ShapeDtypeStruct((n, 2 * TOP_K), jnp.int32),
            jax.ShapeDtypeStruct((1, LANES), F32),
        ],
        scratch_shapes=[pltpu.VMEM((1, LANES), F32)],
        compiler_params=_params(1),
        name="merge_router",
    )(x2, fox, ro, gmix, wmg, bmg, wb, wout, gffn, wr, br)


def _sc_mesh():
    return plsc.VectorSubcoreMesh(core_axis_name="core", subcore_axis_name="subcore")


def _sc_dispatch(rows, dest_t, n_out):
    chunks, n, w = rows.shape

    @functools.partial(pl.kernel, out_type=jax.ShapeDtypeStruct((chunks, n_out, w), rows.dtype),
                       mesh=_sc_mesh(), scratch_types=[], name="moe_dispatch")
    def run(x_hbm, i_hbm, o_hbm):
        for c in range(chunks):
            def body(x_vmem, i_vmem, c=c):
                for j in range(TOP_K):
                    pltpu.sync_copy(x_vmem, o_hbm.at[c].at[i_vmem.at[j]])

            pltpu.emit_pipeline(
                body,
                grid=(n // SC_WINDOW,),
                in_specs=[pl.BlockSpec((SC_WINDOW, w), lambda i: (i, 0)),
                          pl.BlockSpec((TOP_K, SC_WINDOW), lambda i: (0, i))],
                out_specs=[],
                core_axis_name=("core", "subcore"),
                dimension_semantics=(pltpu.PARALLEL,),
            )(x_hbm.at[c], i_hbm)

    return run(rows, dest_t)


def _sc_gather(table, idx):
    chunks, _, w = table.shape
    m = idx.shape[1]

    @functools.partial(pl.kernel, out_type=jax.ShapeDtypeStruct((chunks, m, w), table.dtype),
                       mesh=_sc_mesh(), scratch_types=[], name="moe_combine_gather")
    def run(t_hbm, i_hbm, o_hbm):
        for c in range(chunks):
            def body(i_vmem, o_vmem, c=c):
                pltpu.sync_copy(t_hbm.at[c].at[i_vmem.at[0]], o_vmem)

            pltpu.emit_pipeline(
                body,
                grid=(m // SC_WINDOW,),
                in_specs=[pl.BlockSpec((1, SC_WINDOW), lambda i: (0, i))],
                out_specs=[pl.BlockSpec((SC_WINDOW, w), lambda i: (i, 0))],
                core_axis_name=("core", "subcore"),
                dimension_semantics=(pltpu.PARALLEL,),
            )(i_hbm, o_hbm.at[c])

    return run(table, idx)


GLU_GROUP = 2 * LANES
EXPERT_SUB = 256


def _expert_kernel(be_ref, valid_ref, x_ref, wgu_ref, bgu_ref, wd_ref, bd_ref, perm_ref, y_ref, wgu_scr, wd_scr):
    i = pl.program_id(0)
    f2 = wgu_ref.shape[2]
    bm = x_ref.shape[1]
    sub = min(EXPERT_SUB, bm)
    valid = valid_ref[i]
    live = valid > 0

    @pl.when(jnp.logical_and(live, jnp.logical_or(i == 0, be_ref[i] != be_ref[jnp.maximum(i - 1, 0)])))
    def _():
        for b in range(f2 // GLU_GROUP):
            cols = slice(b * GLU_GROUP, (b + 1) * GLU_GROUP)
            wgu_scr[:, cols] = _dot(wgu_ref[0, :, cols].astype(BF16), perm_ref[...]).astype(BF16)
        wd_scr[...] = wd_ref[0].astype(BF16)

    def sub_block(t, sub):
        rows = slice(t * sub, (t + 1) * sub)
        x = _unpack_pairs([x_ref[c, rows, :] for c in range(x_ref.shape[0])]).astype(BF16)
        yield
        gu = _dot(x, wgu_scr[...])
        yield
        gu = gu + bgu_ref[0]
        acts = []
        for b in range(f2 // GLU_GROUP):
            glu = jnp.minimum(gu[:, b * GLU_GROUP:b * GLU_GROUP + LANES], SWIGLU_LIMIT)
            lin = jnp.clip(gu[:, b * GLU_GROUP + LANES:(b + 1) * GLU_GROUP], -SWIGLU_LIMIT, SWIGLU_LIMIT)
            acts.append((glu * _sigmoid(SWIGLU_ALPHA * glu) * (lin + 1.0)).astype(BF16))
        act = jnp.concatenate(acts, axis=1)
        yield
        y = _dot(act, wd_scr[...])
        yield
        _store_chunks(y_ref, _pack_pairs(y + bd_ref[0]), rows)

    @pl.when(valid == bm)
    def _():
        _interleave(sub_block(t, sub) for t in range(bm // sub))

    for t in range(bm // sub):
        @pl.when(jnp.logical_and(valid < bm, valid > bm - (t + 1) * sub))
        def _(t=t):
            _interleave([sub_block(t, sub)])


def _experts(block_e, valid, x_buf, wgu, bgu, wd, bd, bm):
    p = x_buf.shape[1]
    f, d = wd.shape[1:]
    perm = np.zeros((GLU_GROUP, GLU_GROUP), np.float32)
    for c in range(LANES):
        perm[2 * c, c] = 1.0
        perm[2 * c + 1, LANES + c] = 1.0
    return pl.pallas_call(
        _expert_kernel,
        grid_spec=pltpu.PrefetchScalarGridSpec(
            num_scalar_prefetch=2,
            grid=(p // bm,),
            in_specs=[
                pl.BlockSpec((d // 2 // SC_COLS, bm, SC_COLS), lambda i, be, nv: (0, i, 0)),
                pl.BlockSpec((1, d, 2 * f), lambda i, be, nv: (be[i], 0, 0)),
                pl.BlockSpec((1, 1, 2 * f), lambda i, be, nv: (be[i], 0, 0)),
                pl.BlockSpec((1, f, d), lambda i, be, nv: (be[i], 0, 0)),
                pl.BlockSpec((1, 1, d), lambda i, be, nv: (be[i], 0, 0)),
                pl.BlockSpec((GLU_GROUP, GLU_GROUP), lambda i, be, nv: (0, 0)),
            ],
            out_specs=pl.BlockSpec((d // 2 // SC_COLS, bm, SC_COLS), lambda i, be, nv: (0, i, 0)),
            scratch_shapes=[pltpu.VMEM((d, 2 * f), BF16), pltpu.VMEM((f, d), BF16)],
        ),
        out_shape=jax.ShapeDtypeStruct((d // 2 // SC_COLS, p, SC_COLS), jnp.int32),
        compiler_params=_params(1),
        name="expert_ffn",
    )(block_e, valid, x_buf, wgu, bgu, wd, bd, jnp.asarray(perm, BF16))


FINAL_SUB = 256


def _final_kernel(h1_ref, yg_ref, rf_ref, p_ref, gple_ref, wpg_ref, wpp_ref, gfin_ref, *rest):
    o_ref = rest[-1]
    tm = h1_ref.shape[0]
    sub = min(FINAL_SUB, tm)

    def sub_tile(t):
        rows = slice(t * sub, (t + 1) * sub)
        rf = rf_ref[rows, :]
        h2 = h1_ref[rows, :]
        for j in range(TOP_K):
            h2 = h2 + _unpack_pairs([yg_ref[c, j, rows, :] for c in range(yg_ref.shape[0])]) * rf[:, j:j + 1]
        hn = _rms(h2, gple_ref[...]).astype(BF16)
        yield
        pre = _dot(hn, wpg_ref[...])
        proj = _dot(p_ref[rows, :].astype(BF16), wpp_ref[...])
        yield
        h3 = h2 + _sigmoid(pre) * proj
        o_ref[rows, :] = _rms(h3, gfin_ref[...])

    _interleave(sub_tile(t) for t in range(tm // sub))


def _final(h1, yg, rf, p2, gple, wpg, wpp, gfin, tm, first_tile, prev_out):
    n_total, d = h1.shape
    n = yg.shape[2]
    pd = p2.shape[1]
    dst = lambda i: (i + first_tile, 0)
    const = lambda i: (0, 0)
    in_specs = [
        pl.BlockSpec((tm, d), dst), pl.BlockSpec((d // 2 // SC_COLS, TOP_K, tm, SC_COLS), lambda i: (0, 0, i, 0)),
        pl.BlockSpec((tm, LANES), dst), pl.BlockSpec((tm, pd), dst),
        pl.BlockSpec((1, d), const), pl.BlockSpec((d, d), const), pl.BlockSpec((pd, d), const),
        pl.BlockSpec((1, d), const),
    ]
    args = [h1, yg, rf, p2, gple, wpg, wpp, gfin]
    aliases = {}
    if prev_out is not None:
        in_specs.append(pl.BlockSpec(memory_space=pl.ANY))
        aliases = {len(args): 0}
        args.append(prev_out)
    return pl.pallas_call(
        _final_kernel,
        grid=(n // tm,),
        in_specs=in_specs,
        out_specs=pl.BlockSpec((tm, d), dst),
        out_shape=jax.ShapeDtypeStruct((n_total, d), F32),
        input_output_aliases=aliases,
        compiler_params=_params(1),
        name="combine_ple_norm",
    )(*args)


def _layer(h, p, mix_norm, w_in, b_forget, w_branch, w_merge_gate, b_merge_gate, w_out, ffn_norm,
           w_router, b_router, w_gate_up, b_gate_up, w_down, b_down, ple_norm, w_ple_gate, w_ple_proj,
           final_norm, *, tm, tq, chunk, bm, groups):
    b, s, d = h.shape
    n = b * s
    x2 = h.reshape(n, d)
    row = lambda t: t.reshape(1, -1)

    c0 = 3 * FOX_W
    w_main = jnp.concatenate([w_in[:, :2 * FOX_W], w_in[:, c0 + FOX_HEADS:]], axis=1).astype(BF16)
    w_vt = w_in[:, 2 * FOX_W:c0].T.astype(BF16)
    w_f = jnp.pad(w_in[:, c0:c0 + FOX_HEADS], ((0, 0), (0, LANES - FOX_HEADS))).astype(BF16)
    b_f = jnp.pad(b_forget, (0, LANES - FOX_HEADS)).reshape(1, LANES)
    half = RET_KEY_DIM // 2
    inv = ROPE_BASE ** (-jnp.arange(half, dtype=F32) / half)
    ang = jnp.arange(s).astype(F32)[:, None] * inv[None, :]
    cos_t = jnp.tile(jnp.cos(ang), (1, RET_QK_W // half))
    sin_t = jnp.tile(jnp.sin(ang), (1, RET_QK_W // half))

    q_aug, k_aug, v_t, rq, rk, rv, rg = _inproj(x2, row(mix_norm), w_main, w_vt, w_f, b_f, cos_t, sin_t, s, tm)
    fox = _fox(q_aug, k_aug, v_t, b, s, tq)

    lg = jnp.log1p(-jnp.exp2(-5.0 - jnp.arange(RET_HEADS, dtype=F32)))
    ro = _retention(lg, rq.reshape(b, s, RET_QK_W), rk.reshape(b, s, RET_QK_W),
                    rv.reshape(b, s, RET_V_W), rg.reshape(b, s, RET_V_W), chunk)

    w_r = jnp.pad(w_router, ((0, 0), (0, LANES - N_EXPERTS))).astype(BF16)
    b_r = jnp.concatenate([b_router, jnp.full((LANES - N_EXPERTS,), -1e30, F32)]).reshape(1, LANES)
    ne, dd, f2 = w_gate_up.shape
    bgu = b_gate_up.reshape(ne, f2 // GLU_GROUP, LANES, 2).transpose(0, 1, 3, 2).reshape(ne, 1, f2)
    merge_w = (row(mix_norm), w_merge_gate.astype(BF16), row(b_merge_gate), w_branch.astype(BF16),
               w_out.astype(BF16), row(ffn_norm), w_r, b_r)
    final_w = (row(ple_norm), w_ple_gate.astype(BF16), w_ple_proj.astype(BF16), row(final_norm))
    p2 = p.reshape(n, -1)

    h1, hn, rf, ri, cnt = _merge(x2, fox, ro.reshape(n, RET_V_W), *merge_w, tm, 0, n)

    a = n * TOP_K
    nb = -(-(a + N_EXPERTS * (bm - 1)) // bm)
    e_idx = ri[:, :TOP_K]
    rank = ri[:, TOP_K:2 * TOP_K]
    counts = cnt[0, :N_EXPERTS].astype(jnp.int32)
    padded = (counts + bm - 1) // bm * bm
    pad_end = jnp.cumsum(padded)
    first_row = pad_end - counts
    dest_t = (first_row[e_idx] + rank).T
    block_start = jnp.arange(nb, dtype=jnp.int32) * bm
    block_e = jnp.minimum(jnp.sum(block_start[:, None] >= pad_end[None, :], axis=1),
                          N_EXPERTS - 1).astype(jnp.int32)
    valid = jnp.where(block_start < pad_end[-1],
                      jnp.clip(block_start + bm - first_row[block_e], 0, bm), 0).astype(jnp.int32)

    x_buf = _sc_dispatch(hn, dest_t, nb * bm)
    y_buf = _experts(block_e, valid, x_buf, w_gate_up, bgu, w_down, b_down.reshape(ne, 1, dd), bm)

    ng = n // groups
    out = None
    for g in range(groups):
        idx = dest_t[:, g * ng:(g + 1) * ng].reshape(1, ng * TOP_K)
        yg = _sc_gather(y_buf, idx).reshape(d // 2 // SC_COLS, TOP_K, ng, SC_COLS)
        out = _final(h1, yg, rf, p2, *final_w, tm, g * (ng // tm), out)
    return out.reshape(b, s, d)


def kernel(x, p, mix_norm, w_in, b_forget, w_branch, w_merge_gate, b_merge_gate, w_out, ffn_norm, w_router,
           b_router, w_gate_up, b_gate_up, w_down, b_down, ple_norm, w_ple_gate, w_ple_proj, final_norm):
    depth = p.shape[0]
    assert depth == 1, "the final norm is fused into the (single) layer"
    return _layer(x, p[0], mix_norm[0], w_in[0], b_forget[0], w_branch[0], w_merge_gate[0], b_merge_gate[0],
                  w_out[0], ffn_norm[0], w_router[0], b_router[0], w_gate_up[0], b_gate_up[0], w_down[0],
                  b_down[0], ple_norm[0], w_ple_gate[0], w_ple_proj[0], final_norm,
                  tm=1024, tq=512, chunk=256, bm=1024, groups=4)
```

```python
import functools

import numpy as np
import jax
import jax.numpy as jnp
from jax import lax
from jax.experimental import pallas as pl
from jax.experimental.pallas import tpu as pltpu
from jax.experimental.pallas import tpu_sc as plsc

FOX_HEADS = 8
FOX_HEAD_DIM = 64
RET_HEADS = 4
RET_KEY_DIM = 64
RET_VAL_DIM = 128
ROPE_BASE = 10000.0
N_EXPERTS = 32
TOP_K = 4
SWIGLU_LIMIT = 7.0
SWIGLU_ALPHA = 1.702
EPS = 1e-6

LANES = 128
FOX_W = FOX_HEADS * FOX_HEAD_DIM
RET_QK_W = RET_HEADS * RET_KEY_DIM
RET_V_W = RET_HEADS * RET_VAL_DIM
VMEM_LIMIT = 56 * 1024 * 1024
HIGH_HALF = -65536
LOG2_E = 1.4426950408889634
SC_WINDOW = 128
SC_COLS = 256

F32 = jnp.float32
BF16 = jnp.bfloat16


def _rms(x, g):
    return x * lax.rsqrt(jnp.mean(x * x, axis=-1, keepdims=True) + EPS) * g


def _sigmoid(x):
    return 1.0 / (1.0 + jnp.exp(-x))


def _dot(a, b):
    return jnp.dot(a, b, preferred_element_type=F32)


def _dot_nt(a, b):
    return lax.dot_general(a, b, (((1,), (1,)), ((), ())), preferred_element_type=F32)


def _pack_pairs(v):
    bits = pltpu.bitcast(v.astype(BF16).astype(F32), jnp.int32)
    half = v.shape[1] // 2
    return lax.shift_right_logical(bits[:, :half], 16) | (bits[:, half:] & HIGH_HALF)


def _unpack_pairs(chunks):
    return jnp.concatenate([pltpu.bitcast(lax.shift_left(w, 16), F32) for w in chunks]
                           + [pltpu.bitcast(w & HIGH_HALF, F32) for w in chunks], axis=1)


def _store_chunks(ref, v, rows=slice(None)):
    for c in range(ref.shape[0]):
        ref[c, rows, :] = v[:, c * SC_COLS:(c + 1) * SC_COLS]


def _interleave(generators):
    live = []
    pending = list(generators)
    while pending or live:
        if pending:
            live.append(pending.pop(0))
        for g in list(live):
            try:
                next(g)
            except StopIteration:
                live.remove(g)


def _resident(shape):
    return pl.BlockSpec(shape, lambda *_: (0,) * len(shape), pipeline_mode=pl.Buffered(1))


def _params(n_axes):
    return pltpu.CompilerParams(dimension_semantics=("arbitrary",) * n_axes,
                                vmem_limit_bytes=VMEM_LIMIT)


C_MID = FOX_HEADS
C_LO = 2 * FOX_HEADS
C_ONE = 3 * FOX_HEADS
N_AUG = 3
AUG_W = 2 * N_AUG
K_AUG0 = FOX_HEADS * AUG_W


def _split3(t):
    hi = t.astype(BF16).astype(F32)
    r = t - hi
    mid = r.astype(BF16).astype(F32)
    lo = (r - mid).astype(BF16).astype(F32)
    return hi + pltpu.roll(mid, C_MID, 1) + pltpu.roll(lo, C_LO, 1)


INPROJ_SUB = 256


def _inproj_kernel(x_ref, g_ref, w_ref, wvt_ref, wf_ref, bf_ref, cos_ref, sin_ref, place_ref,
                   q_ref, k_ref, vt_ref, rq_ref, rk_ref, rv_ref, rg_ref, carry_scr, *, tiles_per_seq):
    i = pl.program_id(0)
    tm = x_ref.shape[0]
    sub = min(INPROJ_SUB, tm)

    @pl.when(i % tiles_per_seq == 0)
    def _():
        carry_scr[...] = jnp.zeros(carry_scr.shape, F32)

    lane = lax.broadcasted_iota(jnp.int32, (sub, LANES), 1)
    r = lax.broadcasted_iota(jnp.int32, (sub, sub), 0)
    c = lax.broadcasted_iota(jnp.int32, (sub, sub), 1)
    tri = jnp.where(c <= r, 1.0, 0.0).astype(BF16)
    own = lane < FOX_HEAD_DIM
    is_aug = jnp.logical_and(lane >= FOX_HEAD_DIM, lane < FOX_HEAD_DIM + AUG_W)
    half = RET_KEY_DIM // 2
    lane_r = lax.broadcasted_iota(jnp.int32, (sub, RET_QK_W), 1)
    first = (lane_r % RET_KEY_DIM) < half
    carries = [carry_scr[...]]

    def sub_tile(t):
        rows = slice(t * sub, (t + 1) * sub)
        xn = _rms(x_ref[rows, :], g_ref[...]).astype(BF16)
        yield
        u = _dot(xn, w_ref[...])
        vt_ref[:, rows] = _dot_nt(wvt_ref[...], xn).astype(BF16)
        z = _dot(xn, wf_ref[...]) + bf_ref[...]
        yield
        lf = jnp.where(lane < FOX_HEADS, jnp.minimum(z, 0.0) - jnp.log1p(jnp.exp(-jnp.abs(z))), 0.0)
        ps = _dot(tri, _split3(lf).astype(BF16))
        yield
        assert len(carries) == t + 1
        cum = ps + pltpu.roll(ps, LANES - C_MID, 1) + pltpu.roll(ps, LANES - C_LO, 1)
        cum = jnp.where(lane < FOX_HEADS, cum, 0.0) + carries[t]
        carries.append(cum[sub - 1:sub, :])
        c3 = (_split3(cum * LOG2_E) + jnp.where(lane == C_ONE, 1.0, 0.0)).astype(BF16)
        aug = _dot(c3, place_ref[...])

        o = 2 * FOX_W
        rq = u[:, o:o + RET_QK_W]; o += RET_QK_W
        rk = u[:, o:o + RET_QK_W]; o += RET_QK_W
        rv_ref[rows, :] = u[:, o:o + RET_V_W].astype(BF16); o += RET_V_W
        rg_ref[rows, :] = u[:, o:o + RET_V_W]

        cos = cos_ref[rows, :]
        sin = sin_ref[rows, :]

        def rot(v):
            partner = jnp.where(first, -pltpu.roll(v, RET_QK_W - half, 1), pltpu.roll(v, half, 1))
            return v * cos + partner * sin

        rq_ref[rows, :] = rot(rq).astype(BF16)
        rk_ref[rows, :] = (rot(rk) * (RET_KEY_DIM ** -0.5)).astype(BF16)
        yield
        fq = u[:, 0:FOX_W] * (FOX_HEAD_DIM ** -0.5 * LOG2_E)
        fk = u[:, FOX_W:2 * FOX_W]
        for src, aug0, dst in ((fq, 0, q_ref), (fk, K_AUG0, k_ref)):
            for h in range(FOX_HEADS):
                blk = src[:, (h // 2) * LANES:(h // 2 + 1) * LANES]
                if h % 2:
                    blk = pltpu.roll(blk, FOX_HEAD_DIM, 1)
                mine = pltpu.roll(aug, (FOX_HEAD_DIM - aug0 - AUG_W * h) % LANES, 1)
                slab = jnp.where(own, blk, jnp.where(is_aug, mine, 0.0))
                dst[rows, h * LANES:(h + 1) * LANES] = slab.astype(BF16)

    _interleave(sub_tile(t) for t in range(tm // sub))
    carry_scr[...] = carries[-1]


def _placement():
    place = np.zeros((LANES, LANES), np.float32)
    for h in range(FOX_HEADS):
        for part, src in enumerate((h, C_MID + h, C_LO + h)):
            place[C_ONE, AUG_W * h + part] = 1.0
            place[src, AUG_W * h + N_AUG + part] = 1.0
            place[src, K_AUG0 + AUG_W * h + part] = -1.0
            place[C_ONE, K_AUG0 + AUG_W * h + N_AUG + part] = 1.0
    return jnp.asarray(place, BF16)


def _inproj(x2, g, w_main, w_vt, w_f, b_f, cos_t, sin_t, seq, tm):
    n, d = x2.shape
    wn = w_main.shape[1]
    spt = seq // tm
    aw = FOX_HEADS * LANES
    place = _placement()
    row = lambda i: (i, 0)
    const = lambda i: (0, 0)
    pos = lambda i: (i % spt, 0)
    out_shape = [
        jax.ShapeDtypeStruct((n, aw), BF16), jax.ShapeDtypeStruct((n, aw), BF16),
        jax.ShapeDtypeStruct((FOX_W, n), BF16),
        jax.ShapeDtypeStruct((n, RET_QK_W), BF16), jax.ShapeDtypeStruct((n, RET_QK_W), BF16),
        jax.ShapeDtypeStruct((n, RET_V_W), BF16), jax.ShapeDtypeStruct((n, RET_V_W), F32),
    ]
    out_specs = [
        pl.BlockSpec((tm, aw), row), pl.BlockSpec((tm, aw), row),
        pl.BlockSpec((FOX_W, tm), lambda i: (0, i)),
        pl.BlockSpec((tm, RET_QK_W), row), pl.BlockSpec((tm, RET_QK_W), row),
        pl.BlockSpec((tm, RET_V_W), row), pl.BlockSpec((tm, RET_V_W), row),
    ]
    return pl.pallas_call(
        functools.partial(_inproj_kernel, tiles_per_seq=spt),
        grid=(n // tm,),
        in_specs=[
            pl.BlockSpec((tm, d), row), pl.BlockSpec((1, d), const),
            _resident((d, wn)), _resident((FOX_W, d)),
            _resident((d, LANES)), pl.BlockSpec((1, LANES), const),
            pl.BlockSpec((tm, RET_QK_W), pos), pl.BlockSpec((tm, RET_QK_W), pos),
            _resident((LANES, LANES)),
        ],
        out_specs=out_specs,
        out_shape=out_shape,
        scratch_shapes=[pltpu.VMEM((1, LANES), F32)],
        compiler_params=_params(1),
        name="inproj",
    )(x2, g, w_main, w_vt, w_f, b_f, cos_t, sin_t, place)


def _fox_kernel(q_ref, k_ref, vt_ref, o_ref, *, tq):
    seq = q_ref.shape[0]
    key = lax.broadcasted_iota(jnp.int32, (tq, tq), 0)
    qry = lax.broadcasted_iota(jnp.int32, (tq, tq), 1)
    vrow = lax.broadcasted_iota(jnp.int32, (LANES, tq), 0)
    one = jnp.ones((), BF16)
    items = [(qi, j) for qi in range(seq // tq) for j in range(qi + 1)]

    def logits(qi, j):
        out = []
        for a in range(2):
            sl = slice(a * LANES, (a + 1) * LANES)
            s = _dot_nt(k_ref[j * tq:(j + 1) * tq, sl], q_ref[qi * tq:(qi + 1) * tq, sl])
            out.append(jnp.where(key <= qry, s, -jnp.inf) if j == qi else s)
        return out

    s_cur = logits(*items[0])
    carry = None
    for w, (qi, j) in enumerate(items):
        s_next = logits(*items[w + 1]) if w + 1 < len(items) else None
        if j == 0:
            carry = [(jnp.full((1, tq), -jnp.inf, F32), jnp.zeros((LANES, tq), F32)) for _ in range(2)]
        v = vt_ref[:, j * tq:(j + 1) * tq]
        for a in range(2):
            m_prev, acc = carry[a]
            m_new = jnp.maximum(m_prev, jnp.max(s_cur[a], axis=0, keepdims=True))
            alpha = jnp.exp2(m_prev - m_new)
            p = jnp.exp2(s_cur[a] - m_new).astype(BF16)
            va = jnp.where((vrow // FOX_HEAD_DIM) == a, v, one)
            carry[a] = (m_new, alpha * acc + _dot(va, p))
        if j == qi:
            acc0, acc1 = carry[0][1], carry[1][1]
            ot = jnp.where(vrow < FOX_HEAD_DIM, acc0 / acc0[FOX_HEAD_DIM:FOX_HEAD_DIM + 1, :],
                           acc1 / acc1[0:1, :])
            o_ref[qi * tq:(qi + 1) * tq, :] = ot.T.astype(o_ref.dtype)
        s_cur = s_next


def _fox(q_aug, k_aug, v_t, batch, seq, tq):
    n = q_aug.shape[0]
    pairs = FOX_HEADS // 2
    kern = functools.partial(_fox_kernel, tq=tq)
    return pl.pallas_call(
        kern,
        grid=(batch, pairs),
        in_specs=[
            pl.BlockSpec((seq, 2 * LANES), lambda i, h: (i, h)),
            pl.BlockSpec((seq, 2 * LANES), lambda i, h: (i, h)),
            pl.BlockSpec((LANES, seq), lambda i, h: (h, i)),
        ],
        out_specs=pl.BlockSpec((seq, LANES), lambda i, h: (i, h)),
        out_shape=jax.ShapeDtypeStruct((n, FOX_W), BF16),
        compiler_params=_params(2),
        name="fox_attention",
    )(q_aug, k_aug, v_t)


def _ret_kernel(lg_ref, q_ref, k_ref, v_ref, g_ref, o_ref, *, chunk):
    s = q_ref.shape[1]
    lane = lax.broadcasted_iota(jnp.int32, (chunk, LANES), 1)
    ri = lax.broadcasted_iota(jnp.int32, (chunk, chunk), 0)
    ci = lax.broadcasted_iota(jnp.int32, (chunk, chunk), 1)
    diff = (ri - ci).astype(F32)
    pos = lax.broadcasted_iota(jnp.int32, (chunk, 1), 0).astype(F32)

    def head(h):
        lg = lg_ref[h]
        mine = (lane // RET_KEY_DIM) == (h % 2)
        qk = slice((h // 2) * LANES, (h // 2 + 1) * LANES)
        vs = slice(h * RET_VAL_DIM, (h + 1) * RET_VAL_DIM)
        inner = jnp.where(diff >= 0, jnp.exp(jnp.maximum(diff, 0.0) * lg), 0.0)
        q_decay = jnp.exp((pos + 1.0) * lg)
        k_decay = jnp.exp((chunk - 1.0 - pos) * lg)
        chunk_decay = jnp.exp(jnp.full((1, 1), chunk, F32) * lg)
        state = jnp.zeros((LANES, RET_VAL_DIM), F32)
        for c in range(s // chunk):
            rows = slice(c * chunk, (c + 1) * chunk)
            q = jnp.where(mine, q_ref[0, rows, qk], jnp.zeros((), BF16))
            k = jnp.where(mine, k_ref[0, rows, qk], jnp.zeros((), BF16))
            v = v_ref[0, rows, vs]
            scores = _dot_nt(q, k)
            inter = _dot(q, state.astype(BF16))
            kd = (k.astype(F32) * k_decay).astype(BF16)
            update = lax.dot_general(kd, v, (((0,), (0,)), ((), ())), preferred_element_type=F32)
            yield
            o = _dot((scores * inner).astype(BF16), v) + inter * q_decay
            state = state * chunk_decay + update
            yield
            o = o * lax.rsqrt(jnp.mean(o * o, axis=-1, keepdims=True) + EPS)
            g = g_ref[0, rows, vs]
            o_ref[0, rows, vs] = (o * (g * _sigmoid(g))).astype(o_ref.dtype)

    _interleave(head(h) for h in range(RET_HEADS))


def _retention(lg, rq, rk, rv, rg, chunk):
    b, s, _ = rq.shape
    kern = functools.partial(_ret_kernel, chunk=chunk)
    qk_spec = pl.BlockSpec((1, s, RET_QK_W), lambda i, lg_ref: (i, 0, 0))
    v_spec = pl.BlockSpec((1, s, RET_V_W), lambda i, lg_ref: (i, 0, 0))
    return pl.pallas_call(
        kern,
        grid_spec=pltpu.PrefetchScalarGridSpec(
            num_scalar_prefetch=1,
            grid=(b,),
            in_specs=[qk_spec, qk_spec, v_spec, v_spec],
            out_specs=v_spec,
        ),
        out_shape=jax.ShapeDtypeStruct((b, s, RET_V_W), BF16),
        compiler_params=_params(1),
        name="retention",
    )(lg, rq, rk, rv, rg)


MERGE_SUB = 256


def _merge_kernel(x_ref, fox_ref, ro_ref, gmix_ref, wmg_ref, bmg_ref, wb_ref, wout_ref, gffn_ref,
                  wr_ref, br_ref, h1_ref, hn_ref, rf_ref, ri_ref, cnt_ref, base_scr):
    i = pl.program_id(0)
    tm, d = x_ref.shape
    sub = min(MERGE_SUB, tm)

    @pl.when(i == 0)
    def _():
        base_scr[...] = jnp.zeros(base_scr.shape, F32)

    lane = lax.broadcasted_iota(jnp.int32, (sub, LANES), 1)
    lane_f = lane.astype(F32)
    r = lax.broadcasted_iota(jnp.int32, (sub, sub), 0)
    c = lax.broadcasted_iota(jnp.int32, (sub, sub), 1)
    tri = jnp.where(c < r, 1.0, 0.0).astype(BF16)
    counts = [base_scr[...]]

    def sub_tile(t):
        rows = slice(t * sub, (t + 1) * sub)
        x = x_ref[rows, :]
        xn = _rms(x, gmix_ref[...]).astype(BF16)
        yield
        pre = _dot(xn, wmg_ref[...])
        yield
        gate = _sigmoid(pre + bmg_ref[...])
        yield
        pf = _dot(fox_ref[rows, :], wb_ref[0])
        pr = _dot(ro_ref[rows, :], wb_ref[1])
        yield
        merged = (gate[:, :d] * pf + gate[:, d:] * pr).astype(BF16)
        yield
        h1 = x + _dot(merged, wout_ref[...])
        yield
        h1_ref[rows, :] = h1
        hn = _rms(h1, gffn_ref[...]).astype(BF16)
        _store_chunks(hn_ref, _pack_pairs(hn), rows)
        yield
        logits = _dot(hn, wr_ref[...]) + br_ref[...]
        yield
        vals, idxs = [], []
        cur = logits
        for _ in range(TOP_K):
            m = jnp.max(cur, axis=-1, keepdims=True)
            idx = jnp.min(jnp.where(cur == m, lane_f, float(LANES)), axis=-1, keepdims=True)
            vals.append(m)
            idxs.append(idx)
            cur = jnp.where(lane_f == idx, -jnp.inf, cur)
        exps = [jnp.exp(v - vals[0]) for v in vals]
        denom = exps[0] + exps[1] + exps[2] + exps[3]
        onehot = jnp.zeros(logits.shape, F32)
        for idx in idxs:
            onehot = onehot + jnp.where(lane_f == idx, 1.0, 0.0)
        yield
        assert len(counts) == t + 1
        before = _dot(tri, onehot.astype(BF16)) + counts[t]
        counts.append(counts[t] + jnp.sum(onehot, axis=0, keepdims=True))
        rf = jnp.zeros(logits.shape, F32)
        ri = jnp.zeros(logits.shape, F32)
        for j in range(TOP_K):
            rank = jnp.sum(jnp.where(lane_f == idxs[j], before, 0.0), axis=-1, keepdims=True)
            rf = jnp.where(lane == j, exps[j] / denom, rf)
            ri = jnp.where(lane == j, idxs[j], ri)
            ri = jnp.where(lane == TOP_K + j, rank, ri)
        rf_ref[rows, :] = rf
        ri_ref[rows, :] = ri[:, :2 * TOP_K].astype(jnp.int32)

    _interleave(sub_tile(t) for t in range(tm // sub))
    base_scr[...] = counts[-1]
    cnt_ref[...] = counts[-1]


def _merge(x2, fox, ro, gmix, wmg, bmg, wb, wout, gffn, wr, br, tm, first_tile, n):
    d = x2.shape[1]
    row = lambda i: (i, 0)
    src = lambda i: (i + first_tile, 0)
    const = lambda i: (0, 0)
    return pl.pallas_call(
        _merge_kernel,
        grid=(n // tm,),
        in_specs=[
            pl.BlockSpec((tm, d), src), pl.BlockSpec((tm, FOX_W), src), pl.BlockSpec((tm, RET_V_W), src),
            pl.BlockSpec((1, d), const), _resident((d, 2 * d)), pl.BlockSpec((1, 2 * d), const),
            _resident((2, FOX_W, d)), _resident((d, d)),
            pl.BlockSpec((1, d), const), _resident((d, LANES)), pl.BlockSpec((1, LANES), const),
        ],
        out_specs=[
            pl.BlockSpec((tm, d), row), pl.BlockSpec((d // 2 // SC_COLS, tm, SC_COLS), lambda i: (0, i, 0)),
            pl.BlockSpec((tm, LANES), row), pl.BlockSpec((tm, 2 * TOP_K), row),
            pl.BlockSpec((1, LANES), const),
        ],
        out_shape=[
            jax.ShapeDtypeStruct((n, d), F32), jax.ShapeDtypeStruct((d // 2 // SC_COLS, n, SC_COLS), jnp.int32),
            jax.ShapeDtypeStruct((n, LANES), F32), jax.ShapeDtypeStruct((n, 2 * TOP_K), jnp.int32),
            jax.ShapeDtypeStruct((1, LANES), F32),
        ],
        scratch_shapes=[pltpu.VMEM((1, LANES), F32)],
        compiler_params=_params(1),
        name="merge_router",
    )(x2, fox, ro, gmix, wmg, bmg, wb, wout, gffn, wr, br)


def _sc_mesh():
    return plsc.VectorSubcoreMesh(core_axis_name="core", subcore_axis_name="subcore")


def _sc_dispatch(rows, dest_t, n_out):
    chunks, n, w = rows.shape

    @functools.partial(pl.kernel, out_type=jax.ShapeDtypeStruct((chunks, n_out, w), rows.dtype),
                       mesh=_sc_mesh(), scratch_types=[], name="moe_dispatch")
    def run(x_hbm, i_hbm, o_hbm):
        for c in range(chunks):
            def body(x_vmem, i_vmem, c=c):
                for j in range(TOP_K):
                    pltpu.sync_copy(x_vmem, o_hbm.at[c].at[i_vmem.at[j]])

            pltpu.emit_pipeline(
                body,
                grid=(n // SC_WINDOW,),
                in_specs=[pl.BlockSpec((SC_WINDOW, w), lambda i: (i, 0)),
                          pl.BlockSpec((TOP_K, SC_WINDOW), lambda i: (0, i))],
                out_specs=[],
                core_axis_name=("core", "subcore"),
                dimension_semantics=(pltpu.PARALLEL,),
            )(x_hbm.at[c], i_hbm)

    return run(rows, dest_t)


def _sc_gather(table, idx):
    chunks, _, w = table.shape
    m = idx.shape[1]

    @functools.partial(pl.kernel, out_type=jax.ShapeDtypeStruct((chunks, m, w), table.dtype),
                       mesh=_sc_mesh(), scratch_types=[], name="moe_combine_gather")
    def run(t_hbm, i_hbm, o_hbm):
        for c in range(chunks):
            def body(i_vmem, o_vmem, c=c):
                pltpu.sync_copy(t_hbm.at[c].at[i_vmem.at[0]], o_vmem)

            pltpu.emit_pipeline(
                body,
                grid=(m // SC_WINDOW,),
                in_specs=[pl.BlockSpec((1, SC_WINDOW), lambda i: (0, i))],
                out_specs=[pl.BlockSpec((SC_WINDOW, w), lambda i: (i, 0))],
                core_axis_name=("core", "subcore"),
                dimension_semantics=(pltpu.PARALLEL,),
            )(i_hbm, o_hbm.at[c])

    return run(table, idx)


GLU_GROUP = 2 * LANES
EXPERT_SUB = 256


def _expert_kernel(be_ref, valid_ref, x_ref, wgu_ref, bgu_ref, wd_ref, bd_ref, perm_ref, y_ref, wgu_scr, wd_scr):
    i = pl.program_id(0)
    f2 = wgu_ref.shape[2]
    bm = x_ref.shape[1]
    sub = min(EXPERT_SUB, bm)
    valid = valid_ref[i]
    live = valid > 0

    @pl.when(jnp.logical_and(live, jnp.logical_or(i == 0, be_ref[i] != be_ref[jnp.maximum(i - 1, 0)])))
    def _():
        for b in range(f2 // GLU_GROUP):
            cols = slice(b * GLU_GROUP, (b + 1) * GLU_GROUP)
            wgu_scr[:, cols] = _dot(wgu_ref[0, :, cols].astype(BF16), perm_ref[...]).astype(BF16)
        wd_scr[...] = wd_ref[0].astype(BF16)

    def sub_block(t, sub):
        rows = slice(t * sub, (t + 1) * sub)
        x = _unpack_pairs([x_ref[c, rows, :] for c in range(x_ref.shape[0])]).astype(BF16)
        yield
        gu = _dot(x, wgu_scr[...])
        yield
        gu = gu + bgu_ref[0]
        acts = []
        for b in range(f2 // GLU_GROUP):
            glu = jnp.minimum(gu[:, b * GLU_GROUP:b * GLU_GROUP + LANES], SWIGLU_LIMIT)
            lin = jnp.clip(gu[:, b * GLU_GROUP + LANES:(b + 1) * GLU_GROUP], -SWIGLU_LIMIT, SWIGLU_LIMIT)
            acts.append((glu * _sigmoid(SWIGLU_ALPHA * glu) * (lin + 1.0)).astype(BF16))
        act = jnp.concatenate(acts, axis=1)
        yield
        y = _dot(act, wd_scr[...])
        yield
        _store_chunks(y_ref, _pack_pairs(y + bd_ref[0]), rows)

    @pl.when(valid == bm)
    def _():
        _interleave(sub_block(t, sub) for t in range(bm // sub))

    for t in range(bm // sub):
        @pl.when(jnp.logical_and(valid < bm, valid > bm - (t + 1) * sub))
        def _(t=t):
            _interleave([sub_block(t, sub)])


def _experts(block_e, valid, x_buf, wgu, bgu, wd, bd, bm):
    p = x_buf.shape[1]
    f, d = wd.shape[1:]
    perm = np.zeros((GLU_GROUP, GLU_GROUP), np.float32)
    for c in range(LANES):
        perm[2 * c, c] = 1.0
        perm[2 * c + 1, LANES + c] = 1.0
    return pl.pallas_call(
        _expert_kernel,
        grid_spec=pltpu.PrefetchScalarGridSpec(
            num_scalar_prefetch=2,
            grid=(p // bm,),
            in_specs=[
                pl.BlockSpec((d // 2 // SC_COLS, bm, SC_COLS), lambda i, be, nv: (0, i, 0)),
                pl.BlockSpec((1, d, 2 * f), lambda i, be, nv: (be[i], 0, 0)),
                pl.BlockSpec((1, 1, 2 * f), lambda i, be, nv: (be[i], 0, 0)),
                pl.BlockSpec((1, f, d), lambda i, be, nv: (be[i], 0, 0)),
                pl.BlockSpec((1, 1, d), lambda i, be, nv: (be[i], 0, 0)),
                pl.BlockSpec((GLU_GROUP, GLU_GROUP), lambda i, be, nv: (0, 0)),
            ],
            out_specs=pl.BlockSpec((d // 2 // SC_COLS, bm, SC_COLS), lambda i, be, nv: (0, i, 0)),
            scratch_shapes=[pltpu.VMEM((d, 2 * f), BF16), pltpu.VMEM((f, d), BF16)],
        ),
        out_shape=jax.ShapeDtypeStruct((d // 2 // SC_COLS, p, SC_COLS), jnp.int32),
        compiler_params=_params(1),
        name="expert_ffn",
    )(block_e, valid, x_buf, wgu, bgu, wd, bd, jnp.asarray(perm, BF16))


FINAL_SUB = 256


def _final_kernel(h1_ref, yg_ref, rf_ref, p_ref, gple_ref, wpg_ref, wpp_ref, gfin_ref, *rest):
    o_ref = rest[-1]
    tm = h1_ref.shape[0]
    sub = min(FINAL_SUB, tm)

    def sub_tile(t):
        rows = slice(t * sub, (t + 1) * sub)
        rf = rf_ref[rows, :]
        h2 = h1_ref[rows, :]
        for j in range(TOP_K):
            h2 = h2 + _unpack_pairs([yg_ref[c, j, rows, :] for c in range(yg_ref.shape[0])]) * rf[:, j:j + 1]
        hn = _rms(h2, gple_ref[...]).astype(BF16)
        yield
        pre = _dot(hn, wpg_ref[...])
        proj = _dot(p_ref[rows, :].astype(BF16), wpp_ref[...])
        yield
        h3 = h2 + _sigmoid(pre) * proj
        o_ref[rows, :] = _rms(h3, gfin_ref[...])

    _interleave(sub_tile(t) for t in range(tm // sub))


def _final(h1, yg, rf, p2, gple, wpg, wpp, gfin, tm, first_tile, prev_out):
    n_total, d = h1.shape
    n = yg.shape[2]
    pd = p2.shape[1]
    dst = lambda i: (i + first_tile, 0)
    const = lambda i: (0, 0)
    in_specs = [
        pl.BlockSpec((tm, d), dst), pl.BlockSpec((d // 2 // SC_COLS, TOP_K, tm, SC_COLS), lambda i: (0, 0, i, 0)),
        pl.BlockSpec((tm, LANES), dst), pl.BlockSpec((tm, pd), dst),
        pl.BlockSpec((1, d), const), pl.BlockSpec((d, d), const), pl.BlockSpec((pd, d), const),
        pl.BlockSpec((1, d), const),
    ]
    args = [h1, yg, rf, p2, gple, wpg, wpp, gfin]
    aliases = {}
    if prev_out is not None:
        in_specs.append(pl.BlockSpec(memory_space=pl.ANY))
        aliases = {len(args): 0}
        args.append(prev_out)
    return pl.pallas_call(
        _final_kernel,
        grid=(n // tm,),
        in_specs=in_specs,
        out_specs=pl.BlockSpec((tm, d), dst),
        out_shape=jax.ShapeDtypeStruct((n_total, d), F32),
        input_output_aliases=aliases,
        compiler_params=_params(1),
        name="combine_ple_norm",
    )(*args)


def _layer(h, p, mix_norm, w_in, b_forget, w_branch, w_merge_gate, b_merge_gate, w_out, ffn_norm,
           w_router, b_router, w_gate_up, b_gate_up, w_down, b_down, ple_norm, w_ple_gate, w_ple_proj,
           final_norm, *, tm, tq, chunk, bm, groups):
    b, s, d = h.shape
    n = b * s
    x2 = h.reshape(n, d)
    row = lambda t: t.reshape(1, -1)

    c0 = 3 * FOX_W
    w_main = jnp.concatenate([w_in[:, :2 * FOX_W], w_in[:, c0 + FOX_HEADS:]], axis=1).astype(BF16)
    w_vt = w_in[:, 2 * FOX_W:c0].T.astype(BF16)
    w_f = jnp.pad(w_in[:, c0:c0 + FOX_HEADS], ((0, 0), (0, LANES - FOX_HEADS))).astype(BF16)
    b_f = jnp.pad(b_forget, (0, LANES - FOX_HEADS)).reshape(1, LANES)
    half = RET_KEY_DIM // 2
    inv = ROPE_BASE ** (-jnp.arange(half, dtype=F32) / half)
    ang = jnp.arange(s).astype(F32)[:, None] * inv[None, :]
    cos_t = jnp.tile(jnp.cos(ang), (1, RET_QK_W // half))
    sin_t = jnp.tile(jnp.sin(ang), (1, RET_QK_W // half))

    q_aug, k_aug, v_t, rq, rk, rv, rg = _inproj(x2, row(mix_norm), w_main, w_vt, w_f, b_f, cos_t, sin_t, s, tm)
    fox = _fox(q_aug, k_aug, v_t, b, s, tq)

    lg = jnp.log1p(-jnp.exp2(-5.0 - jnp.arange(RET_HEADS, dtype=F32)))
    ro = _retention(lg, rq.reshape(b, s, RET_QK_W), rk.reshape(b, s, RET_QK_W),
                    rv.reshape(b, s, RET_V_W), rg.reshape(b, s, RET_V_W), chunk)

    w_r = jnp.pad(w_router, ((0, 0), (0, LANES - N_EXPERTS))).astype(BF16)
    b_r = jnp.concatenate([b_router, jnp.full((LANES - N_EXPERTS,), -1e30, F32)]).reshape(1, LANES)
    ne, dd, f2 = w_gate_up.shape
    bgu = b_gate_up.reshape(ne, f2 // GLU_GROUP, LANES, 2).transpose(0, 1, 3, 2).reshape(ne, 1, f2)
    merge_w = (row(mix_norm), w_merge_gate.astype(BF16), row(b_merge_gate), w_branch.astype(BF16),
               w_out.astype(BF16), row(ffn_norm), w_r, b_r)
    final_w = (row(ple_norm), w_ple_gate.astype(BF16), w_ple_proj.astype(BF16), row(final_norm))
    p2 = p.reshape(n, -1)

    h1, hn, rf, ri, cnt = _merge(x2, fox, ro.reshape(n, RET_V_W), *merge_w, tm, 0, n)

    a = n * TOP_K
    nb = -(-(a + N_EXPERTS * (bm - 1)) // bm)
    e_idx = ri[:, :TOP_K]
    rank = ri[:, TOP_K:2 * TOP_K]
    counts = cnt[0, :N_EXPERTS].astype(jnp.int32)
    padded = (counts + bm - 1) // bm * bm
    pad_end = jnp.cumsum(padded)
    first_row = pad_end - counts
    dest_t = (first_row[e_idx] + rank).T
    block_start = jnp.arange(nb, dtype=jnp.int32) * bm
    block_e = jnp.minimum(jnp.sum(block_start[:, None] >= pad_end[None, :], axis=1),
                          N_EXPERTS - 1).astype(jnp.int32)
    valid = jnp.where(block_start < pad_end[-1],
                      jnp.clip(block_start + bm - first_row[block_e], 0, bm), 0).astype(jnp.int32)

    x_buf = _sc_dispatch(hn, dest_t, nb * bm)
    y_buf = _experts(block_e, valid, x_buf, w_gate_up, bgu, w_down, b_down.reshape(ne, 1, dd), bm)

    ng = n // groups
    out = None
    for g in range(groups):
        idx = dest_t[:, g * ng:(g + 1) * ng].reshape(1, ng * TOP_K)
        yg = _sc_gather(y_buf, idx).reshape(d // 2 // SC_COLS, TOP_K, ng, SC_COLS)
        out = _final(h1, yg, rf, p2, *final_w, tm, g * (ng // tm), out)
    return out.reshape(b, s, d)


def kernel(x, p, mix_norm, w_in, b_forget, w_branch, w_merge_gate, b_merge_gate, w_out, ffn_norm, w_router,
           b_router, w_gate_up, b_gate_up, w_down, b_down, ple_norm, w_ple_gate, w_ple_proj, final_norm):
    depth = p.shape[0]
    assert depth == 1, "the final norm is fused into the (single) layer"
    return _layer(x, p[0], mix_norm[0], w_in[0], b_forget[0], w_branch[0], w_merge_gate[0], b_merge_gate[0],
                  w_out[0], ffn_norm[0], w_router[0], b_router[0], w_gate_up[0], b_gate_up[0], w_down[0],
                  b_down[0], ple_norm[0], w_ple_gate[0], w_ple_proj[0], final_norm,
                  tm=1024, tq=512, chunk=128, bm=1024, groups=4)
```

```python
import functools

import numpy as np
import jax
import jax.numpy as jnp
from jax import lax
from jax.experimental import pallas as pl
from jax.experimental.pallas import tpu as pltpu
from jax.experimental.pallas import tpu_sc as plsc

FOX_HEADS = 8
FOX_HEAD_DIM = 64
RET_HEADS = 4
RET_KEY_DIM = 64
RET_VAL_DIM = 128
ROPE_BASE = 10000.0
N_EXPERTS = 32
TOP_K = 4
SWIGLU_LIMIT = 7.0
SWIGLU_ALPHA = 1.702
EPS = 1e-6

LANES = 128
FOX_W = FOX_HEADS * FOX_HEAD_DIM
RET_QK_W = RET_HEADS * RET_KEY_DIM
RET_V_W = RET_HEADS * RET_VAL_DIM
VMEM_LIMIT = 56 * 1024 * 1024
HIGH_HALF = -65536
LOG2_E = 1.4426950408889634
SC_WINDOW = 128
SC_COLS = 256

F32 = jnp.float32
BF16 = jnp.bfloat16


def _rms(x, g):
    return x * lax.rsqrt(jnp.mean(x * x, axis=-1, keepdims=True) + EPS) * g


def _sigmoid(x):
    return 1.0 / (1.0 + jnp.exp(-x))


def _dot(a, b):
    return jnp.dot(a, b, preferred_element_type=F32)


def _dot_nt(a, b):
    return lax.dot_general(a, b, (((1,), (1,)), ((), ())), preferred_element_type=F32)


def _pack_pairs(v):
    bits = pltpu.bitcast(v.astype(BF16).astype(F32), jnp.int32)
    half = v.shape[1] // 2
    return lax.shift_right_logical(bits[:, :half], 16) | (bits[:, half:] & HIGH_HALF)


def _unpack_pairs(chunks):
    return jnp.concatenate([pltpu.bitcast(lax.shift_left(w, 16), F32) for w in chunks]
                           + [pltpu.bitcast(w & HIGH_HALF, F32) for w in chunks], axis=1)


def _store_chunks(ref, v, rows=slice(None)):
    for c in range(ref.shape[0]):
        ref[c, rows, :] = v[:, c * SC_COLS:(c + 1) * SC_COLS]


def _interleave(generators):
    live = []
    pending = list(generators)
    while pending or live:
        if pending:
            live.append(pending.pop(0))
        for g in list(live):
            try:
                next(g)
            except StopIteration:
                live.remove(g)


def _resident(shape):
    return pl.BlockSpec(shape, lambda *_: (0,) * len(shape), pipeline_mode=pl.Buffered(1))


def _params(n_axes):
    return pltpu.CompilerParams(dimension_semantics=("arbitrary",) * n_axes,
                                vmem_limit_bytes=VMEM_LIMIT)


C_MID = FOX_HEADS
C_LO = 2 * FOX_HEADS
C_ONE = 3 * FOX_HEADS
N_AUG = 3
AUG_W = 2 * N_AUG
K_AUG0 = FOX_HEADS * AUG_W


def _split3(t):
    hi = t.astype(BF16).astype(F32)
    r = t - hi
    mid = r.astype(BF16).astype(F32)
    lo = (r - mid).astype(BF16).astype(F32)
    return hi + pltpu.roll(mid, C_MID, 1) + pltpu.roll(lo, C_LO, 1)


INPROJ_SUB = 256


def _inproj_kernel(x_ref, g_ref, w_ref, wvt_ref, wf_ref, bf_ref, cos_ref, sin_ref, place_ref,
                   q_ref, k_ref, vt_ref, rq_ref, rk_ref, rv_ref, rg_ref, carry_scr, *, tiles_per_seq):
    i = pl.program_id(0)
    tm = x_ref.shape[0]
    sub = min(INPROJ_SUB, tm)

    @pl.when(i % tiles_per_seq == 0)
    def _():
        carry_scr[...] = jnp.zeros(carry_scr.shape, F32)

    lane = lax.broadcasted_iota(jnp.int32, (sub, LANES), 1)
    r = lax.broadcasted_iota(jnp.int32, (sub, sub), 0)
    c = lax.broadcasted_iota(jnp.int32, (sub, sub), 1)
    tri = jnp.where(c <= r, 1.0, 0.0).astype(BF16)
    own = lane < FOX_HEAD_DIM
    is_aug = jnp.logical_and(lane >= FOX_HEAD_DIM, lane < FOX_HEAD_DIM + AUG_W)
    half = RET_KEY_DIM // 2
    lane_r = lax.broadcasted_iota(jnp.int32, (sub, RET_QK_W), 1)
    first = (lane_r % RET_KEY_DIM) < half
    carries = [carry_scr[...]]

    def sub_tile(t):
        rows = slice(t * sub, (t + 1) * sub)
        xn = _rms(x_ref[rows, :], g_ref[...]).astype(BF16)
        yield
        u = _dot(xn, w_ref[...])
        vt_ref[:, rows] = _dot_nt(wvt_ref[...], xn).astype(BF16)
        z = _dot(xn, wf_ref[...]) + bf_ref[...]
        yield
        lf = jnp.where(lane < FOX_HEADS, jnp.minimum(z, 0.0) - jnp.log1p(jnp.exp(-jnp.abs(z))), 0.0)
        ps = _dot(tri, _split3(lf).astype(BF16))
        yield
        assert len(carries) == t + 1
        cum = ps + pltpu.roll(ps, LANES - C_MID, 1) + pltpu.roll(ps, LANES - C_LO, 1)
        cum = jnp.where(lane < FOX_HEADS, cum, 0.0) + carries[t]
        carries.append(cum[sub - 1:sub, :])
        c3 = (_split3(cum * LOG2_E) + jnp.where(lane == C_ONE, 1.0, 0.0)).astype(BF16)
        aug = _dot(c3, place_ref[...])

        o = 2 * FOX_W
        rq = u[:, o:o + RET_QK_W]; o += RET_QK_W
        rk = u[:, o:o + RET_QK_W]; o += RET_QK_W
        rv_ref[rows, :] = u[:, o:o + RET_V_W].astype(BF16); o += RET_V_W
        rg_ref[rows, :] = u[:, o:o + RET_V_W]

        cos = cos_ref[rows, :]
        sin = sin_ref[rows, :]

        def rot(v):
            partner = jnp.where(first, -pltpu.roll(v, RET_QK_W - half, 1), pltpu.roll(v, half, 1))
            return v * cos + partner * sin

        rq_ref[rows, :] = rot(rq).astype(BF16)
        rk_ref[rows, :] = (rot(rk) * (RET_KEY_DIM ** -0.5)).astype(BF16)
        yield
        fq = u[:, 0:FOX_W] * (FOX_HEAD_DIM ** -0.5 * LOG2_E)
        fk = u[:, FOX_W:2 * FOX_W]
        for src, aug0, dst in ((fq, 0, q_ref), (fk, K_AUG0, k_ref)):
            for h in range(FOX_HEADS):
                blk = src[:, (h // 2) * LANES:(h // 2 + 1) * LANES]
                if h % 2:
                    blk = pltpu.roll(blk, FOX_HEAD_DIM, 1)
                mine = pltpu.roll(aug, (FOX_HEAD_DIM - aug0 - AUG_W * h) % LANES, 1)
                slab = jnp.where(own, blk, jnp.where(is_aug, mine, 0.0))
                dst[rows, h * LANES:(h + 1) * LANES] = slab.astype(BF16)

    _interleave(sub_tile(t) for t in range(tm // sub))
    carry_scr[...] = carries[-1]


def _placement():
    place = np.zeros((LANES, LANES), np.float32)
    for h in range(FOX_HEADS):
        for part, src in enumerate((h, C_MID + h, C_LO + h)):
            place[C_ONE, AUG_W * h + part] = 1.0
            place[src, AUG_W * h + N_AUG + part] = 1.0
            place[src, K_AUG0 + AUG_W * h + part] = -1.0
            place[C_ONE, K_AUG0 + AUG_W * h + N_AUG + part] = 1.0
    return jnp.asarray(place, BF16)


def _inproj(x2, g, w_main, w_vt, w_f, b_f, cos_t, sin_t, seq, tm):
    n, d = x2.shape
    wn = w_main.shape[1]
    spt = seq // tm
    aw = FOX_HEADS * LANES
    place = _placement()
    row = lambda i: (i, 0)
    const = lambda i: (0, 0)
    pos = lambda i: (i % spt, 0)
    out_shape = [
        jax.ShapeDtypeStruct((n, aw), BF16), jax.ShapeDtypeStruct((n, aw), BF16),
        jax.ShapeDtypeStruct((FOX_W, n), BF16),
        jax.ShapeDtypeStruct((n, RET_QK_W), BF16), jax.ShapeDtypeStruct((n, RET_QK_W), BF16),
        jax.ShapeDtypeStruct((n, RET_V_W), BF16), jax.ShapeDtypeStruct((n, RET_V_W), F32),
    ]
    out_specs = [
        pl.BlockSpec((tm, aw), row), pl.BlockSpec((tm, aw), row),
        pl.BlockSpec((FOX_W, tm), lambda i: (0, i)),
        pl.BlockSpec((tm, RET_QK_W), row), pl.BlockSpec((tm, RET_QK_W), row),
        pl.BlockSpec((tm, RET_V_W), row), pl.BlockSpec((tm, RET_V_W), row),
    ]
    return pl.pallas_call(
        functools.partial(_inproj_kernel, tiles_per_seq=spt),
        grid=(n // tm,),
        in_specs=[
            pl.BlockSpec((tm, d), row), pl.BlockSpec((1, d), const),
            _resident((d, wn)), _resident((FOX_W, d)),
            _resident((d, LANES)), pl.BlockSpec((1, LANES), const),
            pl.BlockSpec((tm, RET_QK_W), pos), pl.BlockSpec((tm, RET_QK_W), pos),
            _resident((LANES, LANES)),
        ],
        out_specs=out_specs,
        out_shape=out_shape,
        scratch_shapes=[pltpu.VMEM((1, LANES), F32)],
        compiler_params=_params(1),
        name="inproj",
    )(x2, g, w_main, w_vt, w_f, b_f, cos_t, sin_t, place)


def _fox_kernel(q_ref, k_ref, vt_ref, o_ref, *, tq):
    seq = q_ref.shape[0]
    half = tq // 2
    vrow = lax.broadcasted_iota(jnp.int32, (LANES, tq), 0)
    one = jnp.ones((), BF16)
    items = [(qi, j) for qi in range(seq // tq) for j in range(qi + 1)]

    def logits(qi, j):
        out = []
        for a in range(2):
            sl = slice(a * LANES, (a + 1) * LANES)
            kk = k_ref[j * tq:(j + 1) * tq, sl]
            qq = q_ref[qi * tq:(qi + 1) * tq, sl]
            if j != qi:
                out.append([(0, tq, _dot_nt(kk, qq))])
                continue
            parts = []
            for c0, nk in ((0, half), (half, tq)):
                sc = _dot_nt(kk[:nk], qq[c0:c0 + half])
                key = lax.broadcasted_iota(jnp.int32, sc.shape, 0)
                qry = lax.broadcasted_iota(jnp.int32, sc.shape, 1) + c0
                parts.append((c0, nk, jnp.where(key <= qry, sc, -jnp.inf)))
            out.append(parts)
        return out

    s_cur = logits(*items[0])
    carry = None
    for w, (qi, j) in enumerate(items):
        s_next = logits(*items[w + 1]) if w + 1 < len(items) else None
        if j == 0:
            carry = [(jnp.full((1, tq), -jnp.inf, F32), jnp.zeros((LANES, tq), F32)) for _ in range(2)]
        v = vt_ref[:, j * tq:(j + 1) * tq]
        for a in range(2):
            m_prev, acc = carry[a]
            va = jnp.where((vrow // FOX_HEAD_DIM) == a, v, one)
            ms, accs = [], []
            for c0, nk, sc in s_cur[a]:
                cols = slice(c0, c0 + sc.shape[1])
                m_new = jnp.maximum(m_prev[:, cols], jnp.max(sc, axis=0, keepdims=True))
                alpha = jnp.exp2(m_prev[:, cols] - m_new)
                p = jnp.exp2(sc - m_new).astype(BF16)
                ms.append(m_new)
                accs.append(alpha * acc[:, cols] + _dot(va[:, :nk], p))
            carry[a] = (ms[0], accs[0]) if len(ms) == 1 else (jnp.concatenate(ms, axis=1),
                                                               jnp.concatenate(accs, axis=1))
        if j == qi:
            acc0, acc1 = carry[0][1], carry[1][1]
            ot = jnp.where(vrow < FOX_HEAD_DIM, acc0 / acc0[FOX_HEAD_DIM:FOX_HEAD_DIM + 1, :],
                           acc1 / acc1[0:1, :])
            o_ref[qi * tq:(qi + 1) * tq, :] = ot.T.astype(o_ref.dtype)
        s_cur = s_next


def _fox(q_aug, k_aug, v_t, batch, seq, tq):
    n = q_aug.shape[0]
    pairs = FOX_HEADS // 2
    kern = functools.partial(_fox_kernel, tq=tq)
    return pl.pallas_call(
        kern,
        grid=(batch, pairs),
        in_specs=[
            pl.BlockSpec((seq, 2 * LANES), lambda i, h: (i, h)),
            pl.BlockSpec((seq, 2 * LANES), lambda i, h: (i, h)),
            pl.BlockSpec((LANES, seq), lambda i, h: (h, i)),
        ],
        out_specs=pl.BlockSpec((seq, LANES), lambda i, h: (i, h)),
        out_shape=jax.ShapeDtypeStruct((n, FOX_W), BF16),
        compiler_params=_params(2),
        name="fox_attention",
    )(q_aug, k_aug, v_t)


def _ret_kernel(lg_ref, q_ref, k_ref, v_ref, g_ref, o_ref, *, chunk):
    s = q_ref.shape[1]
    lane = lax.broadcasted_iota(jnp.int32, (chunk, LANES), 1)
    ri = lax.broadcasted_iota(jnp.int32, (chunk, chunk), 0)
    ci = lax.broadcasted_iota(jnp.int32, (chunk, chunk), 1)
    diff = (ri - ci).astype(F32)
    pos = lax.broadcasted_iota(jnp.int32, (chunk, 1), 0).astype(F32)

    def head(h):
        lg = lg_ref[h]
        mine = (lane // RET_KEY_DIM) == (h % 2)
        qk = slice((h // 2) * LANES, (h // 2 + 1) * LANES)
        vs = slice(h * RET_VAL_DIM, (h + 1) * RET_VAL_DIM)
        inner = jnp.where(diff >= 0, jnp.exp(jnp.maximum(diff, 0.0) * lg), 0.0)
        q_decay = jnp.exp((pos + 1.0) * lg)
        k_decay = jnp.exp((chunk - 1.0 - pos) * lg)
        chunk_decay = jnp.exp(jnp.full((1, 1), chunk, F32) * lg)
        state = jnp.zeros((LANES, RET_VAL_DIM), F32)
        for c in range(s // chunk):
            rows = slice(c * chunk, (c + 1) * chunk)
            q = jnp.where(mine, q_ref[0, rows, qk], jnp.zeros((), BF16))
            k = jnp.where(mine, k_ref[0, rows, qk], jnp.zeros((), BF16))
            v = v_ref[0, rows, vs]
            scores = _dot_nt(q, k)
            inter = _dot(q, state.astype(BF16))
            kd = (k.astype(F32) * k_decay).astype(BF16)
            update = lax.dot_general(kd, v, (((0,), (0,)), ((), ())), preferred_element_type=F32)
            yield
            o = _dot((scores * inner).astype(BF16), v) + inter * q_decay
            state = state * chunk_decay + update
            yield
            o = o * lax.rsqrt(jnp.mean(o * o, axis=-1, keepdims=True) + EPS)
            g = g_ref[0, rows, vs]
            o_ref[0, rows, vs] = (o * (g * _sigmoid(g))).astype(o_ref.dtype)

    _interleave(head(h) for h in range(RET_HEADS))


def _retention(lg, rq, rk, rv, rg, chunk):
    b, s, _ = rq.shape
    kern = functools.partial(_ret_kernel, chunk=chunk)
    qk_spec = pl.BlockSpec((1, s, RET_QK_W), lambda i, lg_ref: (i, 0, 0))
    v_spec = pl.BlockSpec((1, s, RET_V_W), lambda i, lg_ref: (i, 0, 0))
    return pl.pallas_call(
        kern,
        grid_spec=pltpu.PrefetchScalarGridSpec(
            num_scalar_prefetch=1,
            grid=(b,),
            in_specs=[qk_spec, qk_spec, v_spec, v_spec],
            out_specs=v_spec,
        ),
        out_shape=jax.ShapeDtypeStruct((b, s, RET_V_W), BF16),
        compiler_params=_params(1),
        name="retention",
    )(lg, rq, rk, rv, rg)


MERGE_SUB = 256


def _merge_kernel(x_ref, fox_ref, ro_ref, gmix_ref, wmg_ref, bmg_ref, wb_ref, wout_ref, gffn_ref,
                  wr_ref, br_ref, h1_ref, hn_ref, rf_ref, ri_ref, cnt_ref, base_scr):
    i = pl.program_id(0)
    tm, d = x_ref.shape
    sub = min(MERGE_SUB, tm)

    @pl.when(i == 0)
    def _():
        base_scr[...] = jnp.zeros(base_scr.shape, F32)

    lane = lax.broadcasted_iota(jnp.int32, (sub, LANES), 1)
    lane_f = lane.astype(F32)
    r = lax.broadcasted_iota(jnp.int32, (sub, sub), 0)
    c = lax.broadcasted_iota(jnp.int32, (sub, sub), 1)
    tri = jnp.where(c < r, 1.0, 0.0).astype(BF16)
    counts = [base_scr[...]]

    def sub_tile(t):
        rows = slice(t * sub, (t + 1) * sub)
        x = x_ref[rows, :]
        xn = _rms(x, gmix_ref[...]).astype(BF16)
        yield
        pre = _dot(xn, wmg_ref[...])
        yield
        gate = _sigmoid(pre + bmg_ref[...])
        yield
        pf = _dot(fox_ref[rows, :], wb_ref[0])
        pr = _dot(ro_ref[rows, :], wb_ref[1])
        yield
        merged = (gate[:, :d] * pf + gate[:, d:] * pr).astype(BF16)
        yield
        h1 = x + _dot(merged, wout_ref[...])
        yield
        h1_ref[rows, :] = h1
        hn = _rms(h1, gffn_ref[...]).astype(BF16)
        _store_chunks(hn_ref, _pack_pairs(hn), rows)
        yield
        logits = _dot(hn, wr_ref[...]) + br_ref[...]
        yield
        vals, idxs = [], []
        cur = logits
        for _ in range(TOP_K):
            m = jnp.max(cur, axis=-1, keepdims=True)
            idx = jnp.min(jnp.where(cur == m, lane_f, float(LANES)), axis=-1, keepdims=True)
            vals.append(m)
            idxs.append(idx)
            cur = jnp.where(lane_f == idx, -jnp.inf, cur)
        exps = [jnp.exp(v - vals[0]) for v in vals]
        denom = exps[0] + exps[1] + exps[2] + exps[3]
        onehot = jnp.zeros(logits.shape, F32)
        for idx in idxs:
            onehot = onehot + jnp.where(lane_f == idx, 1.0, 0.0)
        yield
        assert len(counts) == t + 1
        before = _dot(tri, onehot.astype(BF16)) + counts[t]
        counts.append(counts[t] + jnp.sum(onehot, axis=0, keepdims=True))
        rf = jnp.zeros(logits.shape, F32)
        ri = jnp.zeros(logits.shape, F32)
        for j in range(TOP_K):
            rank = jnp.sum(jnp.where(lane_f == idxs[j], before, 0.0), axis=-1, keepdims=True)
            rf = jnp.where(lane == j, exps[j] / denom, rf)
            ri = jnp.where(lane == j, idxs[j], ri)
            ri = jnp.where(lane == TOP_K + j, rank, ri)
        rf_ref[rows, :] = rf
        ri_ref[rows, :] = ri[:, :2 * TOP_K].astype(jnp.int32)

    _interleave(sub_tile(t) for t in range(tm // sub))
    base_scr[...] = counts[-1]
    cnt_ref[...] = counts[-1]


def _merge(x2, fox, ro, gmix, wmg, bmg, wb, wout, gffn, wr, br, tm, first_tile, n):
    d = x2.shape[1]
    row = lambda i: (i, 0)
    src = lambda i: (i + first_tile, 0)
    const = lambda i: (0, 0)
    return pl.pallas_call(
        _merge_kernel,
        grid=(n // tm,),
        in_specs=[
            pl.BlockSpec((tm, d), src), pl.BlockSpec((tm, FOX_W), src), pl.BlockSpec((tm, RET_V_W), src),
            pl.BlockSpec((1, d), const), _resident((d, 2 * d)), pl.BlockSpec((1, 2 * d), const),
            _resident((2, FOX_W, d)), _resident((d, d)),
            pl.BlockSpec((1, d), const), _resident((d, LANES)), pl.BlockSpec((1, LANES), const),
        ],
        out_specs=[
            pl.BlockSpec((tm, d), row), pl.BlockSpec((d // 2 // SC_COLS, tm, SC_COLS), lambda i: (0, i, 0)),
            pl.BlockSpec((tm, LANES), row), pl.BlockSpec((tm, 2 * TOP_K), row),
            pl.BlockSpec((1, LANES), const),
        ],
        out_shape=[
            jax.ShapeDtypeStruct((n, d), F32), jax.ShapeDtypeStruct((d // 2 // SC_COLS, n, SC_COLS), jnp.int32),
            jax.ShapeDtypeStruct((n, LANES), F32), jax.ShapeDtypeStruct((n, 2 * TOP_K), jnp.int32),
            jax.ShapeDtypeStruct((1, LANES), F32),
        ],
        scratch_shapes=[pltpu.VMEM((1, LANES), F32)],
        compiler_params=_params(1),
        name="merge_router",
    )(x2, fox, ro, gmix, wmg, bmg, wb, wout, gffn, wr, br)


def _sc_mesh():
    return plsc.VectorSubcoreMesh(core_axis_name="core", subcore_axis_name="subcore")


def _sc_dispatch(rows, dest_t, n_out):
    chunks, n, w = rows.shape

    @functools.partial(pl.kernel, out_type=jax.ShapeDtypeStruct((chunks, n_out, w), rows.dtype),
                       mesh=_sc_mesh(), scratch_types=[], name="moe_dispatch")
    def run(x_hbm, i_hbm, o_hbm):
        for c in range(chunks):
            def body(x_vmem, i_vmem, c=c):
                for j in range(TOP_K):
                    pltpu.sync_copy(x_vmem, o_hbm.at[c].at[i_vmem.at[j]])

            pltpu.emit_pipeline(
                body,
                grid=(n // SC_WINDOW,),
                in_specs=[pl.BlockSpec((SC_WINDOW, w), lambda i: (i, 0)),
                          pl.BlockSpec((TOP_K, SC_WINDOW), lambda i: (0, i))],
                out_specs=[],
                core_axis_name=("core", "subcore"),
                dimension_semantics=(pltpu.PARALLEL,),
            )(x_hbm.at[c], i_hbm)

    return run(rows, dest_t)


def _sc_gather(table, idx):
    chunks, _, w = table.shape
    m = idx.shape[1]

    @functools.partial(pl.kernel, out_type=jax.ShapeDtypeStruct((chunks, m, w), table.dtype),
                       mesh=_sc_mesh(), scratch_types=[], name="moe_combine_gather")
    def run(t_hbm, i_hbm, o_hbm):
        for c in range(chunks):
            def body(i_vmem, o_vmem, c=c):
                pltpu.sync_copy(t_hbm.at[c].at[i_vmem.at[0]], o_vmem)

            pltpu.emit_pipeline(
                body,
                grid=(m // SC_WINDOW,),
                in_specs=[pl.BlockSpec((1, SC_WINDOW), lambda i: (0, i))],
                out_specs=[pl.BlockSpec((SC_WINDOW, w), lambda i: (i, 0))],
                core_axis_name=("core", "subcore"),
                dimension_semantics=(pltpu.PARALLEL,),
            )(i_hbm, o_hbm.at[c])

    return run(table, idx)


GLU_GROUP = 2 * LANES
EXPERT_SUB = 256


def _expert_kernel(be_ref, valid_ref, x_ref, wgu_ref, bgu_ref, wd_ref, bd_ref, perm_ref, y_ref, wgu_scr, wd_scr):
    i = pl.program_id(0)
    f2 = wgu_ref.shape[2]
    bm = x_ref.shape[1]
    sub = min(EXPERT_SUB, bm)
    valid = valid_ref[i]
    live = valid > 0

    @pl.when(jnp.logical_and(live, jnp.logical_or(i == 0, be_ref[i] != be_ref[jnp.maximum(i - 1, 0)])))
    def _():
        for b in range(f2 // GLU_GROUP):
            cols = slice(b * GLU_GROUP, (b + 1) * GLU_GROUP)
            wgu_scr[:, cols] = _dot(wgu_ref[0, :, cols].astype(BF16), perm_ref[...]).astype(BF16)
        wd_scr[...] = wd_ref[0].astype(BF16)

    def sub_block(t, sub):
        rows = slice(t * sub, (t + 1) * sub)
        x = _unpack_pairs([x_ref[c, rows, :] for c in range(x_ref.shape[0])]).astype(BF16)
        yield
        gu = _dot(x, wgu_scr[...])
        yield
        gu = gu + bgu_ref[0]
        acts = []
        for b in range(f2 // GLU_GROUP):
            glu = jnp.minimum(gu[:, b * GLU_GROUP:b * GLU_GROUP + LANES], SWIGLU_LIMIT)
            lin = jnp.clip(gu[:, b * GLU_GROUP + LANES:(b + 1) * GLU_GROUP], -SWIGLU_LIMIT, SWIGLU_LIMIT)
            acts.append((glu * _sigmoid(SWIGLU_ALPHA * glu) * (lin + 1.0)).astype(BF16))
        act = jnp.concatenate(acts, axis=1)
        yield
        y = _dot(act, wd_scr[...])
        yield
        _store_chunks(y_ref, _pack_pairs(y + bd_ref[0]), rows)

    @pl.when(valid == bm)
    def _():
        _interleave(sub_block(t, sub) for t in range(bm // sub))

    for t in range(bm // sub):
        @pl.when(jnp.logical_and(valid < bm, valid > bm - (t + 1) * sub))
        def _(t=t):
            _interleave([sub_block(t, sub)])


def _experts(block_e, valid, x_buf, wgu, bgu, wd, bd, bm):
    p = x_buf.shape[1]
    f, d = wd.shape[1:]
    perm = np.zeros((GLU_GROUP, GLU_GROUP), np.float32)
    for c in range(LANES):
        perm[2 * c, c] = 1.0
        perm[2 * c + 1, LANES + c] = 1.0
    return pl.pallas_call(
        _expert_kernel,
        grid_spec=pltpu.PrefetchScalarGridSpec(
            num_scalar_prefetch=2,
            grid=(p // bm,),
            in_specs=[
                pl.BlockSpec((d // 2 // SC_COLS, bm, SC_COLS), lambda i, be, nv: (0, i, 0)),
                pl.BlockSpec((1, d, 2 * f), lambda i, be, nv: (be[i], 0, 0)),
                pl.BlockSpec((1, 1, 2 * f), lambda i, be, nv: (be[i], 0, 0)),
                pl.BlockSpec((1, f, d), lambda i, be, nv: (be[i], 0, 0)),
                pl.BlockSpec((1, 1, d), lambda i, be, nv: (be[i], 0, 0)),
                pl.BlockSpec((GLU_GROUP, GLU_GROUP), lambda i, be, nv: (0, 0)),
            ],
            out_specs=pl.BlockSpec((d // 2 // SC_COLS, bm, SC_COLS), lambda i, be, nv: (0, i, 0)),
            scratch_shapes=[pltpu.VMEM((d, 2 * f), BF16), pltpu.VMEM((f, d), BF16)],
        ),
        out_shape=jax.ShapeDtypeStruct((d // 2 // SC_COLS, p, SC_COLS), jnp.int32),
        compiler_params=_params(1),
        name="expert_ffn",
    )(block_e, valid, x_buf, wgu, bgu, wd, bd, jnp.asarray(perm, BF16))


FINAL_SUB = 256


def _final_kernel(h1_ref, yg_ref, rf_ref, p_ref, gple_ref, wpg_ref, wpp_ref, gfin_ref, *rest):
    o_ref = rest[-1]
    tm = h1_ref.shape[0]
    sub = min(FINAL_SUB, tm)

    def sub_tile(t):
        rows = slice(t * sub, (t + 1) * sub)
        rf = rf_ref[rows, :]
        h2 = h1_ref[rows, :]
        for j in range(TOP_K):
            h2 = h2 + _unpack_pairs([yg_ref[c, j, rows, :] for c in range(yg_ref.shape[0])]) * rf[:, j:j + 1]
        hn = _rms(h2, gple_ref[...]).astype(BF16)
        yield
        pre = _dot(hn, wpg_ref[...])
        proj = _dot(p_ref[rows, :].astype(BF16), wpp_ref[...])
        yield
        h3 = h2 + _sigmoid(pre) * proj
        o_ref[rows, :] = _rms(h3, gfin_ref[...])

    _interleave(sub_tile(t) for t in range(tm // sub))


def _final(h1, yg, rf, p2, gple, wpg, wpp, gfin, tm, first_tile, prev_out):
    n_total, d = h1.shape
    n = yg.shape[2]
    pd = p2.shape[1]
    dst = lambda i: (i + first_tile, 0)
    const = lambda i: (0, 0)
    in_specs = [
        pl.BlockSpec((tm, d), dst), pl.BlockSpec((d // 2 // SC_COLS, TOP_K, tm, SC_COLS), lambda i: (0, 0, i, 0)),
        pl.BlockSpec((tm, LANES), dst), pl.BlockSpec((tm, pd), dst),
        pl.BlockSpec((1, d), const), pl.BlockSpec((d, d), const), pl.BlockSpec((pd, d), const),
        pl.BlockSpec((1, d), const),
    ]
    args = [h1, yg, rf, p2, gple, wpg, wpp, gfin]
    aliases = {}
    if prev_out is not None:
        in_specs.append(pl.BlockSpec(memory_space=pl.ANY))
        aliases = {len(args): 0}
        args.append(prev_out)
    return pl.pallas_call(
        _final_kernel,
        grid=(n // tm,),
        in_specs=in_specs,
        out_specs=pl.BlockSpec((tm, d), dst),
        out_shape=jax.ShapeDtypeStruct((n_total, d), F32),
        input_output_aliases=aliases,
        compiler_params=_params(1),
        name="combine_ple_norm",
    )(*args)


def _layer(h, p, mix_norm, w_in, b_forget, w_branch, w_merge_gate, b_merge_gate, w_out, ffn_norm,
           w_router, b_router, w_gate_up, b_gate_up, w_down, b_down, ple_norm, w_ple_gate, w_ple_proj,
           final_norm, *, tm, tq, chunk, bm, groups):
    b, s, d = h.shape
    n = b * s
    x2 = h.reshape(n, d)
    row = lambda t: t.reshape(1, -1)

    c0 = 3 * FOX_W
    w_main = jnp.concatenate([w_in[:, :2 * FOX_W], w_in[:, c0 + FOX_HEADS:]], axis=1).astype(BF16)
    w_vt = w_in[:, 2 * FOX_W:c0].T.astype(BF16)
    w_f = jnp.pad(w_in[:, c0:c0 + FOX_HEADS], ((0, 0), (0, LANES - FOX_HEADS))).astype(BF16)
    b_f = jnp.pad(b_forget, (0, LANES - FOX_HEADS)).reshape(1, LANES)
    half = RET_KEY_DIM // 2
    inv = ROPE_BASE ** (-jnp.arange(half, dtype=F32) / half)
    ang = jnp.arange(s).astype(F32)[:, None] * inv[None, :]
    cos_t = jnp.tile(jnp.cos(ang), (1, RET_QK_W // half))
    sin_t = jnp.tile(jnp.sin(ang), (1, RET_QK_W // half))

    q_aug, k_aug, v_t, rq, rk, rv, rg = _inproj(x2, row(mix_norm), w_main, w_vt, w_f, b_f, cos_t, sin_t, s, tm)
    fox = _fox(q_aug, k_aug, v_t, b, s, tq)

    lg = jnp.log1p(-jnp.exp2(-5.0 - jnp.arange(RET_HEADS, dtype=F32)))
    ro = _retention(lg, rq.reshape(b, s, RET_QK_W), rk.reshape(b, s, RET_QK_W),
                    rv.reshape(b, s, RET_V_W), rg.reshape(b, s, RET_V_W), chunk)

    w_r = jnp.pad(w_router, ((0, 0), (0, LANES - N_EXPERTS))).astype(BF16)
    b_r = jnp.concatenate([b_router, jnp.full((LANES - N_EXPERTS,), -1e30, F32)]).reshape(1, LANES)
    ne, dd, f2 = w_gate_up.shape
    bgu = b_gate_up.reshape(ne, f2 // GLU_GROUP, LANES, 2).transpose(0, 1, 3, 2).reshape(ne, 1, f2)
    merge_w = (row(mix_norm), w_merge_gate.astype(BF16), row(b_merge_gate), w_branch.astype(BF16),
               w_out.astype(BF16), row(ffn_norm), w_r, b_r)
    final_w = (row(ple_norm), w_ple_gate.astype(BF16), w_ple_proj.astype(BF16), row(final_norm))
    p2 = p.reshape(n, -1)

    h1, hn, rf, ri, cnt = _merge(x2, fox, ro.reshape(n, RET_V_W), *merge_w, tm, 0, n)

    a = n * TOP_K
    nb = -(-(a + N_EXPERTS * (bm - 1)) // bm)
    e_idx = ri[:, :TOP_K]
    rank = ri[:, TOP_K:2 * TOP_K]
    counts = cnt[0, :N_EXPERTS].astype(jnp.int32)
    padded = (counts + bm - 1) // bm * bm
    pad_end = jnp.cumsum(padded)
    first_row = pad_end - counts
    dest_t = (first_row[e_idx] + rank).T
    block_start = jnp.arange(nb, dtype=jnp.int32) * bm
    block_e = jnp.minimum(jnp.sum(block_start[:, None] >= pad_end[None, :], axis=1),
                          N_EXPERTS - 1).astype(jnp.int32)
    valid = jnp.where(block_start < pad_end[-1],
                      jnp.clip(block_start + bm - first_row[block_e], 0, bm), 0).astype(jnp.int32)

    x_buf = _sc_dispatch(hn, dest_t, nb * bm)
    y_buf = _experts(block_e, valid, x_buf, w_gate_up, bgu, w_down, b_down.reshape(ne, 1, dd), bm)

    shares = [1] + [2 ** max(g - 1, 0) for g in range(1, groups)]
    unit = n // sum(shares)
    out = None
    start = 0
    for share in shares:
        ng = share * unit
        idx = dest_t[:, start:start + ng].reshape(1, ng * TOP_K)
        yg = _sc_gather(y_buf, idx).reshape(d // 2 // SC_COLS, TOP_K, ng, SC_COLS)
        out = _final(h1, yg, rf, p2, *final_w, tm, start // tm, out)
        start += ng
    return out.reshape(b, s, d)


def kernel(x, p, mix_norm, w_in, b_forget, w_branch, w_merge_gate, b_merge_gate, w_out, ffn_norm, w_router,
           b_router, w_gate_up, b_gate_up, w_down, b_down, ple_norm, w_ple_gate, w_ple_proj, final_norm):
    depth = p.shape[0]
    assert depth == 1, "the final norm is fused into the (single) layer"
    return _layer(x, p[0], mix_norm[0], w_in[0], b_forget[0], w_branch[0], w_merge_gate[0], b_merge_gate[0],
                  w_out[0], ffn_norm[0], w_router[0], b_router[0], w_gate_up[0], b_gate_up[0], w_down[0],
                  b_down[0], ple_norm[0], w_ple_gate[0], w_ple_proj[0], final_norm,
                  tm=1024, tq=512, chunk=128, bm=1024, groups=4)
```

```python
import functools

import numpy as np
import jax
import jax.numpy as jnp
from jax import lax
from jax.experimental import pallas as pl
from jax.experimental.pallas import tpu as pltpu
from jax.experimental.pallas import tpu_sc as plsc

FOX_HEADS = 8
FOX_HEAD_DIM = 64
RET_HEADS = 4
RET_KEY_DIM = 64
RET_VAL_DIM = 128
ROPE_BASE = 10000.0
N_EXPERTS = 32
TOP_K = 4
SWIGLU_LIMIT = 7.0
SWIGLU_ALPHA = 1.702
EPS = 1e-6

LANES = 128
FOX_W = FOX_HEADS * FOX_HEAD_DIM
RET_QK_W = RET_HEADS * RET_KEY_DIM
RET_V_W = RET_HEADS * RET_VAL_DIM
VMEM_LIMIT = 56 * 1024 * 1024
HIGH_HALF = -65536
LOG2_E = 1.4426950408889634
SC_WINDOW = 128
SC_COLS = 256

F32 = jnp.float32
BF16 = jnp.bfloat16


def _rms(x, g):
    return x * lax.rsqrt(jnp.mean(x * x, axis=-1, keepdims=True) + EPS) * g


def _sigmoid(x):
    return 1.0 / (1.0 + jnp.exp(-x))


def _dot(a, b):
    return jnp.dot(a, b, preferred_element_type=F32)


def _dot_nt(a, b):
    return lax.dot_general(a, b, (((1,), (1,)), ((), ())), preferred_element_type=F32)


def _pack_pairs(v):
    bits = pltpu.bitcast(v.astype(BF16).astype(F32), jnp.int32)
    half = v.shape[1] // 2
    return lax.shift_right_logical(bits[:, :half], 16) | (bits[:, half:] & HIGH_HALF)


def _unpack_pairs(chunks):
    return jnp.concatenate([pltpu.bitcast(lax.shift_left(w, 16), F32) for w in chunks]
                           + [pltpu.bitcast(w & HIGH_HALF, F32) for w in chunks], axis=1)


def _store_chunks(ref, v, rows=slice(None)):
    for c in range(ref.shape[0]):
        ref[c, rows, :] = v[:, c * SC_COLS:(c + 1) * SC_COLS]


def _interleave(generators):
    live = []
    pending = list(generators)
    while pending or live:
        if pending:
            live.append(pending.pop(0))
        for g in list(live):
            try:
                next(g)
            except StopIteration:
                live.remove(g)


def _resident(shape):
    return pl.BlockSpec(shape, lambda *_: (0,) * len(shape), pipeline_mode=pl.Buffered(1))


def _params(n_axes):
    return pltpu.CompilerParams(dimension_semantics=("arbitrary",) * n_axes,
                                vmem_limit_bytes=VMEM_LIMIT)


C_MID = FOX_HEADS
C_LO = 2 * FOX_HEADS
C_ONE = 3 * FOX_HEADS
N_AUG = 3
AUG_W = 2 * N_AUG
K_AUG0 = FOX_HEADS * AUG_W


def _split3(t):
    hi = t.astype(BF16).astype(F32)
    r = t - hi
    mid = r.astype(BF16).astype(F32)
    lo = (r - mid).astype(BF16).astype(F32)
    return hi + pltpu.roll(mid, C_MID, 1) + pltpu.roll(lo, C_LO, 1)


INPROJ_SUB = 256


def _inproj_kernel(x_ref, g_ref, w_ref, wvt_ref, wf_ref, bf_ref, cos_ref, sin_ref, place_ref,
                   q_ref, k_ref, vt_ref, rq_ref, rk_ref, rv_ref, rg_ref, carry_scr, *, tiles_per_seq):
    i = pl.program_id(0)
    tm = x_ref.shape[0]
    sub = min(INPROJ_SUB, tm)

    @pl.when(i % tiles_per_seq == 0)
    def _():
        carry_scr[...] = jnp.zeros(carry_scr.shape, F32)

    lane = lax.broadcasted_iota(jnp.int32, (sub, LANES), 1)
    r = lax.broadcasted_iota(jnp.int32, (sub, sub), 0)
    c = lax.broadcasted_iota(jnp.int32, (sub, sub), 1)
    tri = jnp.where(c <= r, 1.0, 0.0).astype(BF16)
    own = lane < FOX_HEAD_DIM
    is_aug = jnp.logical_and(lane >= FOX_HEAD_DIM, lane < FOX_HEAD_DIM + AUG_W)
    half = RET_KEY_DIM // 2
    lane_r = lax.broadcasted_iota(jnp.int32, (sub, RET_QK_W), 1)
    first = (lane_r % RET_KEY_DIM) < half
    carries = [carry_scr[...]]

    def sub_tile(t):
        rows = slice(t * sub, (t + 1) * sub)
        xn = _rms(x_ref[rows, :], g_ref[...]).astype(BF16)
        yield
        u = _dot(xn, w_ref[...])
        vt_ref[:, rows] = _dot_nt(wvt_ref[...], xn).astype(BF16)
        z = _dot(xn, wf_ref[...]) + bf_ref[...]
        yield
        lf = jnp.where(lane < FOX_HEADS, jnp.minimum(z, 0.0) - jnp.log1p(jnp.exp(-jnp.abs(z))), 0.0)
        ps = _dot(tri, _split3(lf).astype(BF16))
        yield
        assert len(carries) == t + 1
        cum = ps + pltpu.roll(ps, LANES - C_MID, 1) + pltpu.roll(ps, LANES - C_LO, 1)
        cum = jnp.where(lane < FOX_HEADS, cum, 0.0) + carries[t]
        carries.append(cum[sub - 1:sub, :])
        c3 = (_split3(cum * LOG2_E) + jnp.where(lane == C_ONE, 1.0, 0.0)).astype(BF16)
        aug = _dot(c3, place_ref[...])

        o = 2 * FOX_W
        rq = u[:, o:o + RET_QK_W]; o += RET_QK_W
        rk = u[:, o:o + RET_QK_W]; o += RET_QK_W
        rv_ref[rows, :] = u[:, o:o + RET_V_W].astype(BF16); o += RET_V_W
        rg_ref[rows, :] = u[:, o:o + RET_V_W]

        cos = cos_ref[rows, :]
        sin = sin_ref[rows, :]

        def rot(v):
            partner = jnp.where(first, -pltpu.roll(v, RET_QK_W - half, 1), pltpu.roll(v, half, 1))
            return v * cos + partner * sin

        rq_ref[rows, :] = rot(rq).astype(BF16)
        rk_ref[rows, :] = (rot(rk) * (RET_KEY_DIM ** -0.5)).astype(BF16)
        yield
        fq = u[:, 0:FOX_W] * (FOX_HEAD_DIM ** -0.5 * LOG2_E)
        fk = u[:, FOX_W:2 * FOX_W]
        for src, aug0, dst in ((fq, 0, q_ref), (fk, K_AUG0, k_ref)):
            for h in range(FOX_HEADS):
                blk = src[:, (h // 2) * LANES:(h // 2 + 1) * LANES]
                if h % 2:
                    blk = pltpu.roll(blk, FOX_HEAD_DIM, 1)
                mine = pltpu.roll(aug, (FOX_HEAD_DIM - aug0 - AUG_W * h) % LANES, 1)
                slab = jnp.where(own, blk, jnp.where(is_aug, mine, 0.0))
                dst[rows, h * LANES:(h + 1) * LANES] = slab.astype(BF16)

    _interleave(sub_tile(t) for t in range(tm // sub))
    carry_scr[...] = carries[-1]


def _placement():
    place = np.zeros((LANES, LANES), np.float32)
    for h in range(FOX_HEADS):
        for part, src in enumerate((h, C_MID + h, C_LO + h)):
            place[C_ONE, AUG_W * h + part] = 1.0
            place[src, AUG_W * h + N_AUG + part] = 1.0
            place[src, K_AUG0 + AUG_W * h + part] = -1.0
            place[C_ONE, K_AUG0 + AUG_W * h + N_AUG + part] = 1.0
    return jnp.asarray(place, BF16)


def _inproj(x2, g, w_main, w_vt, w_f, b_f, cos_t, sin_t, seq, tm):
    n, d = x2.shape
    wn = w_main.shape[1]
    spt = seq // tm
    aw = FOX_HEADS * LANES
    place = _placement()
    row = lambda i: (i, 0)
    const = lambda i: (0, 0)
    pos = lambda i: (i % spt, 0)
    out_shape = [
        jax.ShapeDtypeStruct((n, aw), BF16), jax.ShapeDtypeStruct((n, aw), BF16),
        jax.ShapeDtypeStruct((FOX_W, n), BF16),
        jax.ShapeDtypeStruct((n, RET_QK_W), BF16), jax.ShapeDtypeStruct((n, RET_QK_W), BF16),
        jax.ShapeDtypeStruct((n, RET_V_W), BF16), jax.ShapeDtypeStruct((n, RET_V_W), F32),
    ]
    out_specs = [
        pl.BlockSpec((tm, aw), row), pl.BlockSpec((tm, aw), row),
        pl.BlockSpec((FOX_W, tm), lambda i: (0, i)),
        pl.BlockSpec((tm, RET_QK_W), row), pl.BlockSpec((tm, RET_QK_W), row),
        pl.BlockSpec((tm, RET_V_W), row), pl.BlockSpec((tm, RET_V_W), row),
    ]
    return pl.pallas_call(
        functools.partial(_inproj_kernel, tiles_per_seq=spt),
        grid=(n // tm,),
        in_specs=[
            pl.BlockSpec((tm, d), row), pl.BlockSpec((1, d), const),
            _resident((d, wn)), _resident((FOX_W, d)),
            _resident((d, LANES)), pl.BlockSpec((1, LANES), const),
            pl.BlockSpec((tm, RET_QK_W), pos), pl.BlockSpec((tm, RET_QK_W), pos),
            _resident((LANES, LANES)),
        ],
        out_specs=out_specs,
        out_shape=out_shape,
        scratch_shapes=[pltpu.VMEM((1, LANES), F32)],
        compiler_params=_params(1),
        name="inproj",
    )(x2, g, w_main, w_vt, w_f, b_f, cos_t, sin_t, place)


def _fox_kernel(q_ref, k_ref, vt_ref, o_ref, *, tq):
    seq = q_ref.shape[0]
    half = tq // 2
    vrow = lax.broadcasted_iota(jnp.int32, (LANES, tq), 0)
    one = jnp.ones((), BF16)
    items = [(qi, j) for qi in range(seq // tq) for j in range(qi + 1)]

    def logits(qi, j):
        out = []
        for a in range(2):
            sl = slice(a * LANES, (a + 1) * LANES)
            kk = k_ref[j * tq:(j + 1) * tq, sl]
            qq = q_ref[qi * tq:(qi + 1) * tq, sl]
            if j != qi:
                out.append([(0, tq, _dot_nt(kk, qq))])
                continue
            parts = []
            for c0, nk in ((0, half), (half, tq)):
                sc = _dot_nt(kk[:nk], qq[c0:c0 + half])
                key = lax.broadcasted_iota(jnp.int32, sc.shape, 0)
                qry = lax.broadcasted_iota(jnp.int32, sc.shape, 1) + c0
                parts.append((c0, nk, jnp.where(key <= qry, sc, -jnp.inf)))
            out.append(parts)
        return out

    s_cur = logits(*items[0])
    carry = None
    for w, (qi, j) in enumerate(items):
        s_next = logits(*items[w + 1]) if w + 1 < len(items) else None
        if j == 0:
            carry = [(jnp.full((1, tq), -jnp.inf, F32), jnp.zeros((LANES, tq), F32)) for _ in range(2)]
        v = vt_ref[:, j * tq:(j + 1) * tq]
        for a in range(2):
            m_prev, acc = carry[a]
            va = jnp.where((vrow // FOX_HEAD_DIM) == a, v, one)
            ms, accs = [], []
            for c0, nk, sc in s_cur[a]:
                cols = slice(c0, c0 + sc.shape[1])
                m_new = jnp.maximum(m_prev[:, cols], jnp.max(sc, axis=0, keepdims=True))
                alpha = jnp.exp2(m_prev[:, cols] - m_new)
                p = jnp.exp2(sc - m_new).astype(BF16)
                ms.append(m_new)
                accs.append(alpha * acc[:, cols] + _dot(va[:, :nk], p))
            carry[a] = (ms[0], accs[0]) if len(ms) == 1 else (jnp.concatenate(ms, axis=1),
                                                               jnp.concatenate(accs, axis=1))
        if j == qi:
            acc0, acc1 = carry[0][1], carry[1][1]
            ot = jnp.where(vrow < FOX_HEAD_DIM, acc0 / acc0[FOX_HEAD_DIM:FOX_HEAD_DIM + 1, :],
                           acc1 / acc1[0:1, :])
            o_ref[qi * tq:(qi + 1) * tq, :] = ot.T.astype(o_ref.dtype)
        s_cur = s_next


def _fox(q_aug, k_aug, v_t, batch, seq, tq):
    n = q_aug.shape[0]
    pairs = FOX_HEADS // 2
    kern = functools.partial(_fox_kernel, tq=tq)
    return pl.pallas_call(
        kern,
        grid=(batch, pairs),
        in_specs=[
            pl.BlockSpec((seq, 2 * LANES), lambda i, h: (i, h)),
            pl.BlockSpec((seq, 2 * LANES), lambda i, h: (i, h)),
            pl.BlockSpec((LANES, seq), lambda i, h: (h, i)),
        ],
        out_specs=pl.BlockSpec((seq, LANES), lambda i, h: (i, h)),
        out_shape=jax.ShapeDtypeStruct((n, FOX_W), BF16),
        compiler_params=_params(2),
        name="fox_attention",
    )(q_aug, k_aug, v_t)


def _ret_kernel(lg_ref, q_ref, k_ref, v_ref, g_ref, o_ref, *, chunk):
    s = q_ref.shape[1]
    lane = lax.broadcasted_iota(jnp.int32, (chunk, LANES), 1)
    ri = lax.broadcasted_iota(jnp.int32, (chunk, chunk), 0)
    ci = lax.broadcasted_iota(jnp.int32, (chunk, chunk), 1)
    diff = (ri - ci).astype(F32)
    pos = lax.broadcasted_iota(jnp.int32, (chunk, 1), 0).astype(F32)

    def head(h):
        lg = lg_ref[h]
        mine = (lane // RET_KEY_DIM) == (h % 2)
        qk = slice((h // 2) * LANES, (h // 2 + 1) * LANES)
        vs = slice(h * RET_VAL_DIM, (h + 1) * RET_VAL_DIM)
        inner = jnp.where(diff >= 0, jnp.exp(jnp.maximum(diff, 0.0) * lg), 0.0)
        q_decay = jnp.exp((pos + 1.0) * lg)
        k_decay = jnp.exp((chunk - 1.0 - pos) * lg)
        chunk_decay = jnp.exp(jnp.full((1, 1), chunk, F32) * lg)
        state = jnp.zeros((LANES, RET_VAL_DIM), F32)
        for c in range(s // chunk):
            rows = slice(c * chunk, (c + 1) * chunk)
            q = jnp.where(mine, q_ref[0, rows, qk], jnp.zeros((), BF16))
            k = jnp.where(mine, k_ref[0, rows, qk], jnp.zeros((), BF16))
            v = v_ref[0, rows, vs]
            scores = _dot_nt(q, k)
            inter = _dot(q, state.astype(BF16))
            kd = (k.astype(F32) * k_decay).astype(BF16)
            update = lax.dot_general(kd, v, (((0,), (0,)), ((), ())), preferred_element_type=F32)
            yield
            o = _dot((scores * inner).astype(BF16), v) + inter * q_decay
            state = state * chunk_decay + update
            yield
            o = o * lax.rsqrt(jnp.mean(o * o, axis=-1, keepdims=True) + EPS)
            g = g_ref[0, rows, vs]
            o_ref[0, rows, vs] = (o * (g * _sigmoid(g))).astype(o_ref.dtype)

    _interleave(head(h) for h in range(RET_HEADS))


def _retention(lg, rq, rk, rv, rg, chunk):
    b, s, _ = rq.shape
    kern = functools.partial(_ret_kernel, chunk=chunk)
    qk_spec = pl.BlockSpec((1, s, RET_QK_W), lambda i, lg_ref: (i, 0, 0))
    v_spec = pl.BlockSpec((1, s, RET_V_W), lambda i, lg_ref: (i, 0, 0))
    return pl.pallas_call(
        kern,
        grid_spec=pltpu.PrefetchScalarGridSpec(
            num_scalar_prefetch=1,
            grid=(b,),
            in_specs=[qk_spec, qk_spec, v_spec, v_spec],
            out_specs=v_spec,
        ),
        out_shape=jax.ShapeDtypeStruct((b, s, RET_V_W), BF16),
        compiler_params=_params(1),
        name="retention",
    )(lg, rq, rk, rv, rg)


MERGE_SUB = 256


def _merge_kernel(x_ref, fox_ref, ro_ref, gmix_ref, wmg_ref, bmg_ref, wb_ref, wout_ref, gffn_ref,
                  wr_ref, br_ref, h1_ref, hn_ref, rf_ref, ri_ref, cnt_ref, base_scr):
    i = pl.program_id(0)
    tm, d = x_ref.shape
    sub = min(MERGE_SUB, tm)

    @pl.when(i == 0)
    def _():
        base_scr[...] = jnp.zeros(base_scr.shape, F32)

    lane = lax.broadcasted_iota(jnp.int32, (sub, LANES), 1)
    lane_f = lane.astype(F32)
    r = lax.broadcasted_iota(jnp.int32, (sub, sub), 0)
    c = lax.broadcasted_iota(jnp.int32, (sub, sub), 1)
    tri = jnp.where(c < r, 1.0, 0.0).astype(BF16)
    counts = [base_scr[...]]

    def sub_tile(t):
        rows = slice(t * sub, (t + 1) * sub)
        x = x_ref[rows, :]
        xn = _rms(x, gmix_ref[...]).astype(BF16)
        yield
        pre = _dot(xn, wmg_ref[...])
        yield
        gate = _sigmoid(pre + bmg_ref[...])
        yield
        pf = _dot(fox_ref[rows, :], wb_ref[0])
        pr = _dot(ro_ref[rows, :], wb_ref[1])
        yield
        merged = (gate[:, :d] * pf + gate[:, d:] * pr).astype(BF16)
        yield
        h1 = x + _dot(merged, wout_ref[...])
        yield
        h1_ref[rows, :] = h1
        hn = _rms(h1, gffn_ref[...]).astype(BF16)
        _store_chunks(hn_ref, _pack_pairs(hn), rows)
        yield
        logits = _dot(hn, wr_ref[...]) + br_ref[...]
        yield
        vals, idxs = [], []
        cur = logits
        for _ in range(TOP_K):
            m = jnp.max(cur, axis=-1, keepdims=True)
            idx = jnp.min(jnp.where(cur == m, lane_f, float(LANES)), axis=-1, keepdims=True)
            vals.append(m)
            idxs.append(idx)
            cur = jnp.where(lane_f == idx, -jnp.inf, cur)
        exps = [jnp.exp(v - vals[0]) for v in vals]
        denom = exps[0] + exps[1] + exps[2] + exps[3]
        onehot = jnp.zeros(logits.shape, F32)
        for idx in idxs:
            onehot = onehot + jnp.where(lane_f == idx, 1.0, 0.0)
        yield
        assert len(counts) == t + 1
        before = _dot(tri, onehot.astype(BF16)) + counts[t]
        counts.append(counts[t] + jnp.sum(onehot, axis=0, keepdims=True))
        rf = jnp.zeros(logits.shape, F32)
        ri = jnp.zeros(logits.shape, F32)
        for j in range(TOP_K):
            rank = jnp.sum(jnp.where(lane_f == idxs[j], before, 0.0), axis=-1, keepdims=True)
            rf = jnp.where(lane == j, exps[j] / denom, rf)
            ri = jnp.where(lane == j, idxs[j], ri)
            ri = jnp.where(lane == TOP_K + j, rank, ri)
        rf_ref[rows, :] = rf
        ri_ref[rows, :] = ri[:, :2 * TOP_K].astype(jnp.int32)

    _interleave(sub_tile(t) for t in range(tm // sub))
    base_scr[...] = counts[-1]
    cnt_ref[...] = counts[-1]


def _merge(x2, fox, ro, gmix, wmg, bmg, wb, wout, gffn, wr, br, tm, first_tile, n):
    d = x2.shape[1]
    row = lambda i: (i, 0)
    src = lambda i: (i + first_tile, 0)
    const = lambda i: (0, 0)
    return pl.pallas_call(
        _merge_kernel,
        grid=(n // tm,),
        in_specs=[
            pl.BlockSpec((tm, d), src), pl.BlockSpec((tm, FOX_W), src), pl.BlockSpec((tm, RET_V_W), src),
            pl.BlockSpec((1, d), const), _resident((d, 2 * d)), pl.BlockSpec((1, 2 * d), const),
            _resident((2, FOX_W, d)), _resident((d, d)),
            pl.BlockSpec((1, d), const), _resident((d, LANES)), pl.BlockSpec((1, LANES), const),
        ],
        out_specs=[
            pl.BlockSpec((tm, d), row), pl.BlockSpec((d // 2 // SC_COLS, tm, SC_COLS), lambda i: (0, i, 0)),
            pl.BlockSpec((tm, LANES), row), pl.BlockSpec((tm, 2 * TOP_K), row),
            pl.BlockSpec((1, LANES), const),
        ],
        out_shape=[
            jax.ShapeDtypeStruct((n, d), F32), jax.ShapeDtypeStruct((d // 2 // SC_COLS, n, SC_COLS), jnp.int32),
            jax.ShapeDtypeStruct((n, LANES), F32), jax.ShapeDtypeStruct((n, 2 * TOP_K), jnp.int32),
            jax.ShapeDtypeStruct((1, LANES), F32),
        ],
        scratch_shapes=[pltpu.VMEM((1, LANES), F32)],
        compiler_params=_params(1),
        name="merge_router",
    )(x2, fox, ro, gmix, wmg, bmg, wb, wout, gffn, wr, br)


def _sc_mesh():
    return plsc.VectorSubcoreMesh(core_axis_name="core", subcore_axis_name="subcore")


def _sc_dispatch(rows, dest_t, n_out):
    chunks, n, w = rows.shape

    @functools.partial(pl.kernel, out_type=jax.ShapeDtypeStruct((chunks, n_out, w), rows.dtype),
                       mesh=_sc_mesh(), scratch_types=[], name="moe_dispatch")
    def run(x_hbm, i_hbm, o_hbm):
        for c in range(chunks):
            def body(x_vmem, i_vmem, c=c):
                for j in range(TOP_K):
                    pltpu.sync_copy(x_vmem, o_hbm.at[c].at[i_vmem.at[j]])

            pltpu.emit_pipeline(
                body,
                grid=(n // SC_WINDOW,),
                in_specs=[pl.BlockSpec((SC_WINDOW, w), lambda i: (i, 0)),
                          pl.BlockSpec((TOP_K, SC_WINDOW), lambda i: (0, i))],
                out_specs=[],
                core_axis_name=("core", "subcore"),
                dimension_semantics=(pltpu.PARALLEL,),
            )(x_hbm.at[c], i_hbm)

    return run(rows, dest_t)


def _sc_gather(table, idx):
    chunks, _, w = table.shape
    m = idx.shape[1]

    @functools.partial(pl.kernel, out_type=jax.ShapeDtypeStruct((chunks, m, w), table.dtype),
                       mesh=_sc_mesh(), scratch_types=[], name="moe_combine_gather")
    def run(t_hbm, i_hbm, o_hbm):
        for c in range(chunks):
            def body(i_vmem, o_vmem, c=c):
                pltpu.sync_copy(t_hbm.at[c].at[i_vmem.at[0]], o_vmem)

            pltpu.emit_pipeline(
                body,
                grid=(m // SC_WINDOW,),
                in_specs=[pl.BlockSpec((1, SC_WINDOW), lambda i: (0, i))],
                out_specs=[pl.BlockSpec((SC_WINDOW, w), lambda i: (i, 0))],
                core_axis_name=("core", "subcore"),
                dimension_semantics=(pltpu.PARALLEL,),
            )(i_hbm, o_hbm.at[c])

    return run(table, idx)


GLU_GROUP = 2 * LANES
EXPERT_SUB = 256


def _expert_kernel(be_ref, valid_ref, x_ref, wgu_ref, bgu_ref, wd_ref, bd_ref, perm_ref, y_ref, wgu_scr, wd_scr):
    i = pl.program_id(0)
    f2 = wgu_ref.shape[2]
    bm = x_ref.shape[1]
    sub = min(EXPERT_SUB, bm)
    valid = valid_ref[i]
    live = valid > 0

    @pl.when(jnp.logical_and(live, jnp.logical_or(i == 0, be_ref[i] != be_ref[jnp.maximum(i - 1, 0)])))
    def _():
        for b in range(f2 // GLU_GROUP):
            cols = slice(b * GLU_GROUP, (b + 1) * GLU_GROUP)
            wgu_scr[:, cols] = _dot(wgu_ref[0, :, cols].astype(BF16), perm_ref[...]).astype(BF16)
        wd_scr[...] = wd_ref[0].astype(BF16)

    def sub_block(t, sub):
        rows = slice(t * sub, (t + 1) * sub)
        x = _unpack_pairs([x_ref[c, rows, :] for c in range(x_ref.shape[0])]).astype(BF16)
        yield
        gu = _dot(x, wgu_scr[...])
        yield
        gu = gu + bgu_ref[0]
        acts = []
        for b in range(f2 // GLU_GROUP):
            glu = jnp.minimum(gu[:, b * GLU_GROUP:b * GLU_GROUP + LANES], SWIGLU_LIMIT)
            lin = jnp.clip(gu[:, b * GLU_GROUP + LANES:(b + 1) * GLU_GROUP], -SWIGLU_LIMIT, SWIGLU_LIMIT)
            acts.append((glu * _sigmoid(SWIGLU_ALPHA * glu) * (lin + 1.0)).astype(BF16))
        act = jnp.concatenate(acts, axis=1)
        yield
        y = _dot(act, wd_scr[...])
        yield
        _store_chunks(y_ref, _pack_pairs(y + bd_ref[0]), rows)

    @pl.when(valid == bm)
    def _():
        _interleave(sub_block(t, sub) for t in range(bm // sub))

    for t in range(bm // sub):
        @pl.when(jnp.logical_and(valid < bm, valid > bm - (t + 1) * sub))
        def _(t=t):
            _interleave([sub_block(t, sub)])


def _experts(block_e, valid, x_buf, wgu, bgu, wd, bd, bm):
    p = x_buf.shape[1]
    f, d = wd.shape[1:]
    perm = np.zeros((GLU_GROUP, GLU_GROUP), np.float32)
    for c in range(LANES):
        perm[2 * c, c] = 1.0
        perm[2 * c + 1, LANES + c] = 1.0
    return pl.pallas_call(
        _expert_kernel,
        grid_spec=pltpu.PrefetchScalarGridSpec(
            num_scalar_prefetch=2,
            grid=(p // bm,),
            in_specs=[
                pl.BlockSpec((d // 2 // SC_COLS, bm, SC_COLS), lambda i, be, nv: (0, i, 0)),
                pl.BlockSpec((1, d, 2 * f), lambda i, be, nv: (be[i], 0, 0)),
                pl.BlockSpec((1, 1, 2 * f), lambda i, be, nv: (be[i], 0, 0)),
                pl.BlockSpec((1, f, d), lambda i, be, nv: (be[i], 0, 0)),
                pl.BlockSpec((1, 1, d), lambda i, be, nv: (be[i], 0, 0)),
                pl.BlockSpec((GLU_GROUP, GLU_GROUP), lambda i, be, nv: (0, 0)),
            ],
            out_specs=pl.BlockSpec((d // 2 // SC_COLS, bm, SC_COLS), lambda i, be, nv: (0, i, 0)),
            scratch_shapes=[pltpu.VMEM((d, 2 * f), BF16), pltpu.VMEM((f, d), BF16)],
        ),
        out_shape=jax.ShapeDtypeStruct((d // 2 // SC_COLS, p, SC_COLS), jnp.int32),
        compiler_params=_params(1),
        name="expert_ffn",
    )(block_e, valid, x_buf, wgu, bgu, wd, bd, jnp.asarray(perm, BF16))


FINAL_SUB = 256


def _final_kernel(h1_ref, yg_ref, rf_ref, p_ref, gple_ref, wpg_ref, wpp_ref, gfin_ref, *rest):
    o_ref = rest[-1]
    tm = h1_ref.shape[0]
    sub = min(FINAL_SUB, tm)

    def sub_tile(t):
        rows = slice(t * sub, (t + 1) * sub)
        rf = rf_ref[rows, :]
        h2 = h1_ref[rows, :]
        for j in range(TOP_K):
            h2 = h2 + _unpack_pairs([yg_ref[c, j, rows, :] for c in range(yg_ref.shape[0])]) * rf[:, j:j + 1]
        hn = _rms(h2, gple_ref[...]).astype(BF16)
        yield
        pre = _dot(hn, wpg_ref[...])
        proj = _dot(p_ref[rows, :].astype(BF16), wpp_ref[...])
        yield
        h3 = h2 + _sigmoid(pre) * proj
        o_ref[rows, :] = _rms(h3, gfin_ref[...])

    _interleave(sub_tile(t) for t in range(tm // sub))


def _final(h1, yg, rf, p2, gple, wpg, wpp, gfin, tm, first_tile, prev_out):
    n_total, d = h1.shape
    n = yg.shape[2]
    pd = p2.shape[1]
    dst = lambda i: (i + first_tile, 0)
    const = lambda i: (0, 0)
    in_specs = [
        pl.BlockSpec((tm, d), dst), pl.BlockSpec((d // 2 // SC_COLS, TOP_K, tm, SC_COLS), lambda i: (0, 0, i, 0)),
        pl.BlockSpec((tm, LANES), dst), pl.BlockSpec((tm, pd), dst),
        pl.BlockSpec((1, d), const), pl.BlockSpec((d, d), const), pl.BlockSpec((pd, d), const),
        pl.BlockSpec((1, d), const),
    ]
    args = [h1, yg, rf, p2, gple, wpg, wpp, gfin]
    aliases = {}
    if prev_out is not None:
        in_specs.append(pl.BlockSpec(memory_space=pl.ANY))
        aliases = {len(args): 0}
        args.append(prev_out)
    return pl.pallas_call(
        _final_kernel,
        grid=(n // tm,),
        in_specs=in_specs,
        out_specs=pl.BlockSpec((tm, d), dst),
        out_shape=jax.ShapeDtypeStruct((n_total, d), F32),
        input_output_aliases=aliases,
        compiler_params=_params(1),
        name="combine_ple_norm",
    )(*args)


def _layer(h, p, mix_norm, w_in, b_forget, w_branch, w_merge_gate, b_merge_gate, w_out, ffn_norm,
           w_router, b_router, w_gate_up, b_gate_up, w_down, b_down, ple_norm, w_ple_gate, w_ple_proj,
           final_norm, *, tm, tq, chunk, bm, groups):
    b, s, d = h.shape
    n = b * s
    x2 = h.reshape(n, d)
    row = lambda t: t.reshape(1, -1)

    c0 = 3 * FOX_W
    w_main = jnp.concatenate([w_in[:, :2 * FOX_W], w_in[:, c0 + FOX_HEADS:]], axis=1).astype(BF16)
    w_vt = w_in[:, 2 * FOX_W:c0].T.astype(BF16)
    w_f = jnp.pad(w_in[:, c0:c0 + FOX_HEADS], ((0, 0), (0, LANES - FOX_HEADS))).astype(BF16)
    b_f = jnp.pad(b_forget, (0, LANES - FOX_HEADS)).reshape(1, LANES)
    half = RET_KEY_DIM // 2
    inv = ROPE_BASE ** (-jnp.arange(half, dtype=F32) / half)
    ang = jnp.arange(s).astype(F32)[:, None] * inv[None, :]
    cos_t = jnp.tile(jnp.cos(ang), (1, RET_QK_W // half))
    sin_t = jnp.tile(jnp.sin(ang), (1, RET_QK_W // half))

    q_aug, k_aug, v_t, rq, rk, rv, rg = _inproj(x2, row(mix_norm), w_main, w_vt, w_f, b_f, cos_t, sin_t, s, tm)
    fox = _fox(q_aug, k_aug, v_t, b, s, tq)

    lg = jnp.log1p(-jnp.exp2(-5.0 - jnp.arange(RET_HEADS, dtype=F32)))
    ro = _retention(lg, rq.reshape(b, s, RET_QK_W), rk.reshape(b, s, RET_QK_W),
                    rv.reshape(b, s, RET_V_W), rg.reshape(b, s, RET_V_W), chunk)

    w_r = jnp.pad(w_router, ((0, 0), (0, LANES - N_EXPERTS))).astype(BF16)
    b_r = jnp.concatenate([b_router, jnp.full((LANES - N_EXPERTS,), -1e30, F32)]).reshape(1, LANES)
    ne, dd, f2 = w_gate_up.shape
    bgu = b_gate_up.reshape(ne, f2 // GLU_GROUP, LANES, 2).transpose(0, 1, 3, 2).reshape(ne, 1, f2)
    merge_w = (row(mix_norm), w_merge_gate.astype(BF16), row(b_merge_gate), w_branch.astype(BF16),
               w_out.astype(BF16), row(ffn_norm), w_r, b_r)
    final_w = (row(ple_norm), w_ple_gate.astype(BF16), w_ple_proj.astype(BF16), row(final_norm))
    p2 = p.reshape(n, -1)

    h1, hn, rf, ri, cnt = _merge(x2, fox, ro.reshape(n, RET_V_W), *merge_w, tm, 0, n)

    a = n * TOP_K
    nb = -(-(a + N_EXPERTS * (bm - 1)) // bm)
    ri_t = ri.T
    counts = cnt[0, :N_EXPERTS].astype(jnp.int32)
    padded = (counts + bm - 1) // bm * bm
    pad_end = jnp.cumsum(padded)
    first_row = pad_end - counts
    experts = jnp.arange(N_EXPERTS, dtype=jnp.int32)

    def lookup(ids):
        return jnp.sum(jnp.where(ids[None] == experts.reshape((-1,) + (1,) * ids.ndim),
                                 first_row.reshape((-1,) + (1,) * ids.ndim), 0), axis=0)

    dest_t = lookup(ri_t[:TOP_K]) + ri_t[TOP_K:]
    block_start = jnp.arange(nb, dtype=jnp.int32) * bm
    block_e = jnp.minimum(jnp.sum(block_start[:, None] >= pad_end[None, :], axis=1),
                          N_EXPERTS - 1).astype(jnp.int32)
    valid = jnp.where(block_start < pad_end[-1],
                      jnp.clip(block_start + bm - lookup(block_e), 0, bm), 0).astype(jnp.int32)

    x_buf = _sc_dispatch(hn, dest_t, nb * bm)
    y_buf = _experts(block_e, valid, x_buf, w_gate_up, bgu, w_down, b_down.reshape(ne, 1, dd), bm)

    shares = [1] + [2 ** max(g - 1, 0) for g in range(1, groups)]
    unit = n // sum(shares)
    out = None
    start = 0
    for share in shares:
        ng = share * unit
        idx = dest_t[:, start:start + ng].reshape(1, ng * TOP_K)
        yg = _sc_gather(y_buf, idx).reshape(d // 2 // SC_COLS, TOP_K, ng, SC_COLS)
        out = _final(h1, yg, rf, p2, *final_w, tm, start // tm, out)
        start += ng
    return out.reshape(b, s, d)


def kernel(x, p, mix_norm, w_in, b_forget, w_branch, w_merge_gate, b_merge_gate, w_out, ffn_norm, w_router,
           b_router, w_gate_up, b_gate_up, w_down, b_down, ple_norm, w_ple_gate, w_ple_proj, final_norm):
    depth = p.shape[0]
    assert depth == 1, "the final norm is fused into the (single) layer"
    return _layer(x, p[0], mix_norm[0], w_in[0], b_forget[0], w_branch[0], w_merge_gate[0], b_merge_gate[0],
                  w_out[0], ffn_norm[0], w_router[0], b_router[0], w_gate_up[0], b_gate_up[0], w_down[0],
                  b_down[0], ple_norm[0], w_ple_gate[0], w_ple_proj[0], final_norm,
                  tm=1024, tq=512, chunk=128, bm=1024, groups=5)
```

```python
import functools

import numpy as np
import jax
import jax.numpy as jnp
from jax import lax
from jax.experimental import pallas as pl
from jax.experimental.pallas import tpu as pltpu
from jax.experimental.pallas import tpu_sc as plsc

FOX_HEADS = 8
FOX_HEAD_DIM = 64
RET_HEADS = 4
RET_KEY_DIM = 64
RET_VAL_DIM = 128
ROPE_BASE = 10000.0
N_EXPERTS = 32
TOP_K = 4
SWIGLU_LIMIT = 7.0
SWIGLU_ALPHA = 1.702
EPS = 1e-6

LANES = 128
FOX_W = FOX_HEADS * FOX_HEAD_DIM
RET_QK_W = RET_HEADS * RET_KEY_DIM
RET_V_W = RET_HEADS * RET_VAL_DIM
VMEM_LIMIT = 56 * 1024 * 1024
HIGH_HALF = -65536
LOG2_E = 1.4426950408889634
SC_WINDOW = 128
SC_COLS = 256

F32 = jnp.float32
BF16 = jnp.bfloat16


def _rms(x, g):
    return x * lax.rsqrt(jnp.mean(x * x, axis=-1, keepdims=True) + EPS) * g


def _sigmoid(x):
    return 1.0 / (1.0 + jnp.exp(-x))


def _dot(a, b):
    return jnp.dot(a, b, preferred_element_type=F32)


def _dot_nt(a, b):
    return lax.dot_general(a, b, (((1,), (1,)), ((), ())), preferred_element_type=F32)


def _pack_pairs(v):
    bits = pltpu.bitcast(v.astype(BF16).astype(F32), jnp.int32)
    half = v.shape[1] // 2
    return lax.shift_right_logical(bits[:, :half], 16) | (bits[:, half:] & HIGH_HALF)


def _unpack_pairs(chunks):
    return jnp.concatenate([pltpu.bitcast(lax.shift_left(w, 16), F32) for w in chunks]
                           + [pltpu.bitcast(w & HIGH_HALF, F32) for w in chunks], axis=1)


def _store_chunks(ref, v, rows=slice(None)):
    for c in range(ref.shape[0]):
        ref[c, rows, :] = v[:, c * SC_COLS:(c + 1) * SC_COLS]


def _interleave(generators):
    live = []
    pending = list(generators)
    while pending or live:
        if pending:
            live.append(pending.pop(0))
        for g in list(live):
            try:
                next(g)
            except StopIteration:
                live.remove(g)


def _resident(shape):
    return pl.BlockSpec(shape, lambda *_: (0,) * len(shape), pipeline_mode=pl.Buffered(1))


def _params(n_axes):
    return pltpu.CompilerParams(dimension_semantics=("arbitrary",) * n_axes,
                                vmem_limit_bytes=VMEM_LIMIT)


C_MID = FOX_HEADS
C_LO = 2 * FOX_HEADS
C_ONE = 3 * FOX_HEADS
N_AUG = 3
AUG_W = 2 * N_AUG
K_AUG0 = FOX_HEADS * AUG_W


def _split3(t):
    hi = t.astype(BF16).astype(F32)
    r = t - hi
    mid = r.astype(BF16).astype(F32)
    lo = (r - mid).astype(BF16).astype(F32)
    return hi + pltpu.roll(mid, C_MID, 1) + pltpu.roll(lo, C_LO, 1)


INPROJ_SUB = 256


def _inproj_kernel(x_ref, g_ref, w_ref, wvt_ref, wf_ref, bf_ref, cos_ref, sin_ref, place_ref,
                   q_ref, k_ref, vt_ref, rq_ref, rk_ref, rv_ref, rg_ref, carry_scr, *, tiles_per_seq):
    i = pl.program_id(0)
    tm = x_ref.shape[0]
    sub = min(INPROJ_SUB, tm)

    @pl.when(i % tiles_per_seq == 0)
    def _():
        carry_scr[...] = jnp.zeros(carry_scr.shape, F32)

    lane = lax.broadcasted_iota(jnp.int32, (sub, LANES), 1)
    r = lax.broadcasted_iota(jnp.int32, (sub, sub), 0)
    c = lax.broadcasted_iota(jnp.int32, (sub, sub), 1)
    tri = jnp.where(c <= r, 1.0, 0.0).astype(BF16)
    own = lane < FOX_HEAD_DIM
    is_aug = jnp.logical_and(lane >= FOX_HEAD_DIM, lane < FOX_HEAD_DIM + AUG_W)
    half = RET_KEY_DIM // 2
    lane_r = lax.broadcasted_iota(jnp.int32, (sub, RET_QK_W), 1)
    first = (lane_r % RET_KEY_DIM) < half
    carries = [carry_scr[...]]

    def sub_tile(t):
        rows = slice(t * sub, (t + 1) * sub)
        xn = _rms(x_ref[rows, :], g_ref[...]).astype(BF16)
        yield
        u = _dot(xn, w_ref[...])
        vt_ref[:, rows] = _dot_nt(wvt_ref[...], xn).astype(BF16)
        z = _dot(xn, wf_ref[...]) + bf_ref[...]
        yield
        lf = jnp.where(lane < FOX_HEADS, jnp.minimum(z, 0.0) - jnp.log1p(jnp.exp(-jnp.abs(z))), 0.0)
        ps = _dot(tri, _split3(lf).astype(BF16))
        yield
        assert len(carries) == t + 1
        cum = ps + pltpu.roll(ps, LANES - C_MID, 1) + pltpu.roll(ps, LANES - C_LO, 1)
        cum = jnp.where(lane < FOX_HEADS, cum, 0.0) + carries[t]
        carries.append(cum[sub - 1:sub, :])
        c3 = (_split3(cum * LOG2_E) + jnp.where(lane == C_ONE, 1.0, 0.0)).astype(BF16)
        aug = _dot(c3, place_ref[...])

        o = 2 * FOX_W
        rq = u[:, o:o + RET_QK_W]; o += RET_QK_W
        rk = u[:, o:o + RET_QK_W]; o += RET_QK_W
        rv_ref[rows, :] = u[:, o:o + RET_V_W].astype(BF16); o += RET_V_W
        rg_ref[rows, :] = u[:, o:o + RET_V_W]

        cos = cos_ref[rows, :]
        sin = sin_ref[rows, :]

        def rot(v):
            partner = jnp.where(first, -pltpu.roll(v, RET_QK_W - half, 1), pltpu.roll(v, half, 1))
            return v * cos + partner * sin

        rq_ref[rows, :] = rot(rq).astype(BF16)
        rk_ref[rows, :] = (rot(rk) * (RET_KEY_DIM ** -0.5)).astype(BF16)
        yield
        fq = u[:, 0:FOX_W] * (FOX_HEAD_DIM ** -0.5 * LOG2_E)
        fk = u[:, FOX_W:2 * FOX_W]
        for src, aug0, dst in ((fq, 0, q_ref), (fk, K_AUG0, k_ref)):
            for h in range(FOX_HEADS):
                blk = src[:, (h // 2) * LANES:(h // 2 + 1) * LANES]
                if h % 2:
                    blk = pltpu.roll(blk, FOX_HEAD_DIM, 1)
                mine = pltpu.roll(aug, (FOX_HEAD_DIM - aug0 - AUG_W * h) % LANES, 1)
                slab = jnp.where(own, blk, jnp.where(is_aug, mine, 0.0))
                dst[rows, h * LANES:(h + 1) * LANES] = slab.astype(BF16)

    _interleave(sub_tile(t) for t in range(tm // sub))
    carry_scr[...] = carries[-1]


def _placement():
    place = np.zeros((LANES, LANES), np.float32)
    for h in range(FOX_HEADS):
        for part, src in enumerate((h, C_MID + h, C_LO + h)):
            place[C_ONE, AUG_W * h + part] = 1.0
            place[src, AUG_W * h + N_AUG + part] = 1.0
            place[src, K_AUG0 + AUG_W * h + part] = -1.0
            place[C_ONE, K_AUG0 + AUG_W * h + N_AUG + part] = 1.0
    return jnp.asarray(place, BF16)


def _inproj(x2, g, w_main, w_vt, w_f, b_f, cos_t, sin_t, seq, tm):
    n, d = x2.shape
    wn = w_main.shape[1]
    spt = seq // tm
    aw = FOX_HEADS * LANES
    place = _placement()
    row = lambda i: (i, 0)
    const = lambda i: (0, 0)
    pos = lambda i: (i % spt, 0)
    out_shape = [
        jax.ShapeDtypeStruct((n, aw), BF16), jax.ShapeDtypeStruct((n, aw), BF16),
        jax.ShapeDtypeStruct((FOX_W, n), BF16),
        jax.ShapeDtypeStruct((n, RET_QK_W), BF16), jax.ShapeDtypeStruct((n, RET_QK_W), BF16),
        jax.ShapeDtypeStruct((n, RET_V_W), BF16), jax.ShapeDtypeStruct((n, RET_V_W), F32),
    ]
    out_specs = [
        pl.BlockSpec((tm, aw), row), pl.BlockSpec((tm, aw), row),
        pl.BlockSpec((FOX_W, tm), lambda i: (0, i)),
        pl.BlockSpec((tm, RET_QK_W), row), pl.BlockSpec((tm, RET_QK_W), row),
        pl.BlockSpec((tm, RET_V_W), row), pl.BlockSpec((tm, RET_V_W), row),
    ]
    return pl.pallas_call(
        functools.partial(_inproj_kernel, tiles_per_seq=spt),
        grid=(n // tm,),
        in_specs=[
            pl.BlockSpec((tm, d), row), pl.BlockSpec((1, d), const),
            _resident((d, wn)), _resident((FOX_W, d)),
            _resident((d, LANES)), pl.BlockSpec((1, LANES), const),
            pl.BlockSpec((tm, RET_QK_W), pos), pl.BlockSpec((tm, RET_QK_W), pos),
            _resident((LANES, LANES)),
        ],
        out_specs=out_specs,
        out_shape=out_shape,
        scratch_shapes=[pltpu.VMEM((1, LANES), F32)],
        compiler_params=_params(1),
        name="inproj",
    )(x2, g, w_main, w_vt, w_f, b_f, cos_t, sin_t, place)


def _fox_kernel(q_ref, k_ref, vt_ref, o_ref, *, tq):
    seq = q_ref.shape[0]
    half = tq // 2
    vrow = lax.broadcasted_iota(jnp.int32, (LANES, tq), 0)
    one = jnp.ones((), BF16)
    items = [(qi, j) for qi in range(seq // tq) for j in range(qi + 1)]

    def logits(qi, j):
        out = []
        for a in range(2):
            sl = slice(a * LANES, (a + 1) * LANES)
            kk = k_ref[j * tq:(j + 1) * tq, sl]
            qq = q_ref[qi * tq:(qi + 1) * tq, sl]
            if j != qi:
                out.append([(0, tq, _dot_nt(kk, qq))])
                continue
            parts = []
            for c0, nk in ((0, half), (half, tq)):
                sc = _dot_nt(kk[:nk], qq[c0:c0 + half])
                key = lax.broadcasted_iota(jnp.int32, sc.shape, 0)
                qry = lax.broadcasted_iota(jnp.int32, sc.shape, 1) + c0
                parts.append((c0, nk, jnp.where(key <= qry, sc, -jnp.inf)))
            out.append(parts)
        return out

    s_cur = logits(*items[0])
    carry = None
    for w, (qi, j) in enumerate(items):
        s_next = logits(*items[w + 1]) if w + 1 < len(items) else None
        if j == 0:
            carry = [(jnp.full((1, tq), -jnp.inf, F32), jnp.zeros((LANES, tq), F32)) for _ in range(2)]
        v = vt_ref[:, j * tq:(j + 1) * tq]
        for a in range(2):
            m_prev, acc = carry[a]
            va = jnp.where((vrow // FOX_HEAD_DIM) == a, v, one)
            ms, accs = [], []
            for c0, nk, sc in s_cur[a]:
                cols = slice(c0, c0 + sc.shape[1])
                m_new = jnp.maximum(m_prev[:, cols], jnp.max(sc, axis=0, keepdims=True))
                alpha = jnp.exp2(m_prev[:, cols] - m_new)
                p = jnp.exp2(sc - m_new).astype(BF16)
                ms.append(m_new)
                accs.append(alpha * acc[:, cols] + _dot(va[:, :nk], p))
            carry[a] = (ms[0], accs[0]) if len(ms) == 1 else (jnp.concatenate(ms, axis=1),
                                                               jnp.concatenate(accs, axis=1))
        if j == qi:
            acc0, acc1 = carry[0][1], carry[1][1]
            ot = jnp.where(vrow < FOX_HEAD_DIM, acc0 / acc0[FOX_HEAD_DIM:FOX_HEAD_DIM + 1, :],
                           acc1 / acc1[0:1, :])
            o_ref[qi * tq:(qi + 1) * tq, :] = ot.T.astype(o_ref.dtype)
        s_cur = s_next


def _fox(q_aug, k_aug, v_t, batch, seq, tq):
    n = q_aug.shape[0]
    pairs = FOX_HEADS // 2
    kern = functools.partial(_fox_kernel, tq=tq)
    return pl.pallas_call(
        kern,
        grid=(batch, pairs),
        in_specs=[
            pl.BlockSpec((seq, 2 * LANES), lambda i, h: (i, h)),
            pl.BlockSpec((seq, 2 * LANES), lambda i, h: (i, h)),
            pl.BlockSpec((LANES, seq), lambda i, h: (h, i)),
        ],
        out_specs=pl.BlockSpec((seq, LANES), lambda i, h: (i, h)),
        out_shape=jax.ShapeDtypeStruct((n, FOX_W), BF16),
        compiler_params=_params(2),
        name="fox_attention",
    )(q_aug, k_aug, v_t)


def _ret_kernel(lg_ref, q_ref, k_ref, v_ref, g_ref, o_ref, *, chunk):
    s = q_ref.shape[1]
    lane = lax.broadcasted_iota(jnp.int32, (chunk, LANES), 1)
    ri = lax.broadcasted_iota(jnp.int32, (chunk, chunk), 0)
    ci = lax.broadcasted_iota(jnp.int32, (chunk, chunk), 1)
    diff = (ri - ci).astype(F32)
    pos = lax.broadcasted_iota(jnp.int32, (chunk, 1), 0).astype(F32)

    def head(h):
        lg = lg_ref[h]
        mine = (lane // RET_KEY_DIM) == (h % 2)
        qk = slice((h // 2) * LANES, (h // 2 + 1) * LANES)
        vs = slice(h * RET_VAL_DIM, (h + 1) * RET_VAL_DIM)
        inner = jnp.where(diff >= 0, jnp.exp(jnp.maximum(diff, 0.0) * lg), 0.0)
        q_decay = jnp.exp((pos + 1.0) * lg)
        k_decay = jnp.exp((chunk - 1.0 - pos) * lg)
        chunk_decay = jnp.exp(jnp.full((1, 1), chunk, F32) * lg)
        state = jnp.zeros((LANES, RET_VAL_DIM), F32)
        for c in range(s // chunk):
            rows = slice(c * chunk, (c + 1) * chunk)
            q = jnp.where(mine, q_ref[0, rows, qk], jnp.zeros((), BF16))
            k = jnp.where(mine, k_ref[0, rows, qk], jnp.zeros((), BF16))
            v = v_ref[0, rows, vs]
            scores = _dot_nt(q, k)
            inter = _dot(q, state.astype(BF16))
            kd = (k.astype(F32) * k_decay).astype(BF16)
            update = lax.dot_general(kd, v, (((0,), (0,)), ((), ())), preferred_element_type=F32)
            yield
            o = _dot((scores * inner).astype(BF16), v) + inter * q_decay
            state = state * chunk_decay + update
            yield
            o = o * lax.rsqrt(jnp.mean(o * o, axis=-1, keepdims=True) + EPS)
            g = g_ref[0, rows, vs]
            o_ref[0, rows, vs] = (o * (g * _sigmoid(g))).astype(o_ref.dtype)

    _interleave(head(h) for h in range(RET_HEADS))


def _retention(lg, rq, rk, rv, rg, chunk):
    b, s, _ = rq.shape
    kern = functools.partial(_ret_kernel, chunk=chunk)
    qk_spec = pl.BlockSpec((1, s, RET_QK_W), lambda i, lg_ref: (i, 0, 0))
    v_spec = pl.BlockSpec((1, s, RET_V_W), lambda i, lg_ref: (i, 0, 0))
    return pl.pallas_call(
        kern,
        grid_spec=pltpu.PrefetchScalarGridSpec(
            num_scalar_prefetch=1,
            grid=(b,),
            in_specs=[qk_spec, qk_spec, v_spec, v_spec],
            out_specs=v_spec,
        ),
        out_shape=jax.ShapeDtypeStruct((b, s, RET_V_W), BF16),
        compiler_params=_params(1),
        name="retention",
    )(lg, rq, rk, rv, rg)


MERGE_SUB = 256


def _merge_kernel(x_ref, fox_ref, ro_ref, gmix_ref, wmg_ref, bmg_ref, wb_ref, wout_ref, gffn_ref,
                  wr_ref, br_ref, h1_ref, hn_ref, rf_ref, ri_ref, cnt_ref, base_scr):
    i = pl.program_id(0)
    tm, d = x_ref.shape
    sub = min(MERGE_SUB, tm)

    @pl.when(i == 0)
    def _():
        base_scr[...] = jnp.zeros(base_scr.shape, F32)

    lane = lax.broadcasted_iota(jnp.int32, (sub, LANES), 1)
    lane_f = lane.astype(F32)
    r = lax.broadcasted_iota(jnp.int32, (sub, sub), 0)
    c = lax.broadcasted_iota(jnp.int32, (sub, sub), 1)
    tri = jnp.where(c < r, 1.0, 0.0).astype(BF16)
    counts = [base_scr[...]]

    def sub_tile(t):
        rows = slice(t * sub, (t + 1) * sub)
        x = x_ref[rows, :]
        xn = _rms(x, gmix_ref[...]).astype(BF16)
        yield
        pre = _dot(xn, wmg_ref[...])
        yield
        gate = _sigmoid(pre + bmg_ref[...])
        yield
        pf = _dot(fox_ref[rows, :], wb_ref[0])
        pr = _dot(ro_ref[rows, :], wb_ref[1])
        yield
        merged = (gate[:, :d] * pf + gate[:, d:] * pr).astype(BF16)
        yield
        h1 = x + _dot(merged, wout_ref[...])
        yield
        h1_ref[rows, :] = h1
        hn = _rms(h1, gffn_ref[...]).astype(BF16)
        _store_chunks(hn_ref, _pack_pairs(hn), rows)
        yield
        logits = _dot(hn, wr_ref[...]) + br_ref[...]
        yield
        vals, idxs = [], []
        cur = logits
        for _ in range(TOP_K):
            m = jnp.max(cur, axis=-1, keepdims=True)
            idx = jnp.min(jnp.where(cur == m, lane_f, float(LANES)), axis=-1, keepdims=True)
            vals.append(m)
            idxs.append(idx)
            cur = jnp.where(lane_f == idx, -jnp.inf, cur)
        exps = [jnp.exp(v - vals[0]) for v in vals]
        denom = exps[0] + exps[1] + exps[2] + exps[3]
        onehot = jnp.zeros(logits.shape, F32)
        for idx in idxs:
            onehot = onehot + jnp.where(lane_f == idx, 1.0, 0.0)
        yield
        assert len(counts) == t + 1
        before = _dot(tri, onehot.astype(BF16)) + counts[t]
        counts.append(counts[t] + jnp.sum(onehot, axis=0, keepdims=True))
        rf = jnp.zeros(logits.shape, F32)
        ri = jnp.zeros(logits.shape, F32)
        for j in range(TOP_K):
            rank = jnp.sum(jnp.where(lane_f == idxs[j], before, 0.0), axis=-1, keepdims=True)
            rf = jnp.where(lane == j, exps[j] / denom, rf)
            ri = jnp.where(lane == j, idxs[j], ri)
            ri = jnp.where(lane == TOP_K + j, rank, ri)
        rf_ref[rows, :] = rf
        ri_ref[rows, :] = ri[:, :2 * TOP_K].astype(jnp.int32)

    _interleave(sub_tile(t) for t in range(tm // sub))
    base_scr[...] = counts[-1]
    cnt_ref[...] = counts[-1]


def _merge(x2, fox, ro, gmix, wmg, bmg, wb, wout, gffn, wr, br, tm, first_tile, n):
    d = x2.shape[1]
    row = lambda i: (i, 0)
    src = lambda i: (i + first_tile, 0)
    const = lambda i: (0, 0)
    return pl.pallas_call(
        _merge_kernel,
        grid=(n // tm,),
        in_specs=[
            pl.BlockSpec((tm, d), src), pl.BlockSpec((tm, FOX_W), src), pl.BlockSpec((tm, RET_V_W), src),
            pl.BlockSpec((1, d), const), _resident((d, 2 * d)), pl.BlockSpec((1, 2 * d), const),
            _resident((2, FOX_W, d)), _resident((d, d)),
            pl.BlockSpec((1, d), const), _resident((d, LANES)), pl.BlockSpec((1, LANES), const),
        ],
        out_specs=[
            pl.BlockSpec((tm, d), row), pl.BlockSpec((d // 2 // SC_COLS, tm, SC_COLS), lambda i: (0, i, 0)),
            pl.BlockSpec((tm, LANES), row), pl.BlockSpec((tm, 2 * TOP_K), row),
            pl.BlockSpec((1, LANES), const),
        ],
        out_shape=[
            jax.ShapeDtypeStruct((n, d), F32), jax.ShapeDtypeStruct((d // 2 // SC_COLS, n, SC_COLS), jnp.int32),
            jax.ShapeDtypeStruct((n, LANES), F32), jax.ShapeDtypeStruct((n, 2 * TOP_K), jnp.int32),
            jax.ShapeDtypeStruct((1, LANES), F32),
        ],
        scratch_shapes=[pltpu.VMEM((1, LANES), F32)],
        compiler_params=_params(1),
        name="merge_router",
    )(x2, fox, ro, gmix, wmg, bmg, wb, wout, gffn, wr, br)


def _sc_mesh():
    return plsc.VectorSubcoreMesh(core_axis_name="core", subcore_axis_name="subcore")


def _sc_dispatch(rows, dest_t, n_out):
    chunks, n, w = rows.shape

    @functools.partial(pl.kernel, out_type=jax.ShapeDtypeStruct((chunks, n_out, w), rows.dtype),
                       mesh=_sc_mesh(), scratch_types=[], name="moe_dispatch")
    def run(x_hbm, i_hbm, o_hbm):
        for c in range(chunks):
            def body(x_vmem, i_vmem, c=c):
                for j in range(TOP_K):
                    pltpu.sync_copy(x_vmem, o_hbm.at[c].at[i_vmem.at[j]])

            pltpu.emit_pipeline(
                body,
                grid=(n // SC_WINDOW,),
                in_specs=[pl.BlockSpec((SC_WINDOW, w), lambda i: (i, 0)),
                          pl.BlockSpec((TOP_K, SC_WINDOW), lambda i: (0, i))],
                out_specs=[],
                core_axis_name=("core", "subcore"),
                dimension_semantics=(pltpu.PARALLEL,),
            )(x_hbm.at[c], i_hbm)

    return run(rows, dest_t)


def _sc_gather(table, idx):
    chunks, _, w = table.shape
    m = idx.shape[1]

    @functools.partial(pl.kernel, out_type=jax.ShapeDtypeStruct((chunks, m, w), table.dtype),
                       mesh=_sc_mesh(), scratch_types=[], name="moe_combine_gather")
    def run(t_hbm, i_hbm, o_hbm):
        for c in range(chunks):
            def body(i_vmem, o_vmem, c=c):
                pltpu.sync_copy(t_hbm.at[c].at[i_vmem.at[0]], o_vmem)

            pltpu.emit_pipeline(
                body,
                grid=(m // SC_WINDOW,),
                in_specs=[pl.BlockSpec((1, SC_WINDOW), lambda i: (0, i))],
                out_specs=[pl.BlockSpec((SC_WINDOW, w), lambda i: (i, 0))],
                core_axis_name=("core", "subcore"),
                dimension_semantics=(pltpu.PARALLEL,),
            )(i_hbm, o_hbm.at[c])

    return run(table, idx)


GLU_GROUP = 2 * LANES
EXPERT_SUB = 256


def _expert_kernel(be_ref, valid_ref, x_ref, wgu_ref, bgu_ref, wd_ref, bd_ref, perm_ref, y_ref, wgu_scr, wd_scr):
    i = pl.program_id(0)
    f2 = wgu_ref.shape[2]
    bm = x_ref.shape[1]
    sub = min(EXPERT_SUB, bm)
    valid = valid_ref[i]
    live = valid > 0

    @pl.when(jnp.logical_and(live, jnp.logical_or(i == 0, be_ref[i] != be_ref[jnp.maximum(i - 1, 0)])))
    def _():
        for b in range(f2 // GLU_GROUP):
            cols = slice(b * GLU_GROUP, (b + 1) * GLU_GROUP)
            wgu_scr[:, cols] = _dot(wgu_ref[0, :, cols].astype(BF16), perm_ref[...]).astype(BF16)
        wd_scr[...] = wd_ref[0].astype(BF16)

    def sub_block(t, sub):
        rows = slice(t * sub, (t + 1) * sub)
        x = _unpack_pairs([x_ref[c, rows, :] for c in range(x_ref.shape[0])]).astype(BF16)
        yield
        gu = _dot(x, wgu_scr[...])
        yield
        gu = gu + bgu_ref[0]
        acts = []
        for b in range(f2 // GLU_GROUP):
            glu = jnp.minimum(gu[:, b * GLU_GROUP:b * GLU_GROUP + LANES], SWIGLU_LIMIT)
            lin = jnp.clip(gu[:, b * GLU_GROUP + LANES:(b + 1) * GLU_GROUP], -SWIGLU_LIMIT, SWIGLU_LIMIT)
            acts.append((glu * _sigmoid(SWIGLU_ALPHA * glu) * (lin + 1.0)).astype(BF16))
        act = jnp.concatenate(acts, axis=1)
        yield
        y = _dot(act, wd_scr[...])
        yield
        _store_chunks(y_ref, _pack_pairs(y + bd_ref[0]), rows)

    @pl.when(valid == bm)
    def _():
        _interleave(sub_block(t, sub) for t in range(bm // sub))

    for t in range(bm // sub):
        @pl.when(jnp.logical_and(valid < bm, valid > bm - (t + 1) * sub))
        def _(t=t):
            _interleave([sub_block(t, sub)])


def _experts(block_e, valid, x_buf, wgu, bgu, wd, bd, bm):
    p = x_buf.shape[1]
    f, d = wd.shape[1:]
    perm = np.zeros((GLU_GROUP, GLU_GROUP), np.float32)
    for c in range(LANES):
        perm[2 * c, c] = 1.0
        perm[2 * c + 1, LANES + c] = 1.0
    return pl.pallas_call(
        _expert_kernel,
        grid_spec=pltpu.PrefetchScalarGridSpec(
            num_scalar_prefetch=2,
            grid=(p // bm,),
            in_specs=[
                pl.BlockSpec((d // 2 // SC_COLS, bm, SC_COLS), lambda i, be, nv: (0, i, 0)),
                pl.BlockSpec((1, d, 2 * f), lambda i, be, nv: (be[i], 0, 0)),
                pl.BlockSpec((1, 1, 2 * f), lambda i, be, nv: (be[i], 0, 0)),
                pl.BlockSpec((1, f, d), lambda i, be, nv: (be[i], 0, 0)),
                pl.BlockSpec((1, 1, d), lambda i, be, nv: (be[i], 0, 0)),
                pl.BlockSpec((GLU_GROUP, GLU_GROUP), lambda i, be, nv: (0, 0)),
            ],
            out_specs=pl.BlockSpec((d // 2 // SC_COLS, bm, SC_COLS), lambda i, be, nv: (0, i, 0)),
            scratch_shapes=[pltpu.VMEM((d, 2 * f), BF16), pltpu.VMEM((f, d), BF16)],
        ),
        out_shape=jax.ShapeDtypeStruct((d // 2 // SC_COLS, p, SC_COLS), jnp.int32),
        compiler_params=_params(1),
        name="expert_ffn",
    )(block_e, valid, x_buf, wgu, bgu, wd, bd, jnp.asarray(perm, BF16))


FINAL_SUB = 256


def _final_kernel(h1_ref, yg_ref, rf_ref, p_ref, gple_ref, wpg_ref, wpp_ref, gfin_ref, *rest):
    o_ref = rest[-1]
    tm = h1_ref.shape[0]
    sub = min(FINAL_SUB, tm)

    def sub_tile(t):
        rows = slice(t * sub, (t + 1) * sub)
        rf = rf_ref[rows, :]
        h2 = h1_ref[rows, :]
        for j in range(TOP_K):
            h2 = h2 + _unpack_pairs([yg_ref[c, j, rows, :] for c in range(yg_ref.shape[0])]) * rf[:, j:j + 1]
        hn = _rms(h2, gple_ref[...]).astype(BF16)
        yield
        pre = _dot(hn, wpg_ref[...])
        proj = _dot(p_ref[rows, :].astype(BF16), wpp_ref[...])
        yield
        h3 = h2 + _sigmoid(pre) * proj
        o_ref[rows, :] = _rms(h3, gfin_ref[...])

    _interleave(sub_tile(t) for t in range(tm // sub))


def _final(h1, yg, rf, p2, gple, wpg, wpp, gfin, tm, first_tile, prev_out):
    n_total, d = h1.shape
    n = yg.shape[2]
    pd = p2.shape[1]
    dst = lambda i: (i + first_tile, 0)
    const = lambda i: (0, 0)
    in_specs = [
        pl.BlockSpec((tm, d), dst), pl.BlockSpec((d // 2 // SC_COLS, TOP_K, tm, SC_COLS), lambda i: (0, 0, i, 0)),
        pl.BlockSpec((tm, LANES), dst), pl.BlockSpec((tm, pd), dst),
        pl.BlockSpec((1, d), const), pl.BlockSpec((d, d), const), pl.BlockSpec((pd, d), const),
        pl.BlockSpec((1, d), const),
    ]
    args = [h1, yg, rf, p2, gple, wpg, wpp, gfin]
    aliases = {}
    if prev_out is not None:
        in_specs.append(pl.BlockSpec(memory_space=pl.ANY))
        aliases = {len(args): 0}
        args.append(prev_out)
    return pl.pallas_call(
        _final_kernel,
        grid=(n // tm,),
        in_specs=in_specs,
        out_specs=pl.BlockSpec((tm, d), dst),
        out_shape=jax.ShapeDtypeStruct((n_total, d), F32),
        input_output_aliases=aliases,
        compiler_params=_params(1),
        name="combine_ple_norm",
    )(*args)


def _layer(h, p, mix_norm, w_in, b_forget, w_branch, w_merge_gate, b_merge_gate, w_out, ffn_norm,
           w_router, b_router, w_gate_up, b_gate_up, w_down, b_down, ple_norm, w_ple_gate, w_ple_proj,
           final_norm, *, tm, tq, chunk, bm, groups):
    b, s, d = h.shape
    n = b * s
    x2 = h.reshape(n, d)
    row = lambda t: t.reshape(1, -1)

    c0 = 3 * FOX_W
    w_main = jnp.concatenate([w_in[:, :2 * FOX_W], w_in[:, c0 + FOX_HEADS:]], axis=1).astype(BF16)
    w_vt = w_in[:, 2 * FOX_W:c0].T.astype(BF16)
    w_f = jnp.pad(w_in[:, c0:c0 + FOX_HEADS], ((0, 0), (0, LANES - FOX_HEADS))).astype(BF16)
    b_f = jnp.pad(b_forget, (0, LANES - FOX_HEADS)).reshape(1, LANES)
    half = RET_KEY_DIM // 2
    inv = ROPE_BASE ** (-jnp.arange(half, dtype=F32) / half)
    ang = jnp.arange(s).astype(F32)[:, None] * inv[None, :]
    cos_t = jnp.tile(jnp.cos(ang), (1, RET_QK_W // half))
    sin_t = jnp.tile(jnp.sin(ang), (1, RET_QK_W // half))

    q_aug, k_aug, v_t, rq, rk, rv, rg = _inproj(x2, row(mix_norm), w_main, w_vt, w_f, b_f, cos_t, sin_t, s, tm)
    fox = _fox(q_aug, k_aug, v_t, b, s, tq)

    lg = jnp.log1p(-jnp.exp2(-5.0 - jnp.arange(RET_HEADS, dtype=F32)))
    ro = _retention(lg, rq.reshape(b, s, RET_QK_W), rk.reshape(b, s, RET_QK_W),
                    rv.reshape(b, s, RET_V_W), rg.reshape(b, s, RET_V_W), chunk)

    w_r = jnp.pad(w_router, ((0, 0), (0, LANES - N_EXPERTS))).astype(BF16)
    b_r = jnp.concatenate([b_router, jnp.full((LANES - N_EXPERTS,), -1e30, F32)]).reshape(1, LANES)
    ne, dd, f2 = w_gate_up.shape
    bgu = b_gate_up.reshape(ne, f2 // GLU_GROUP, LANES, 2).transpose(0, 1, 3, 2).reshape(ne, 1, f2)
    merge_w = (row(mix_norm), w_merge_gate.astype(BF16), row(b_merge_gate), w_branch.astype(BF16),
               w_out.astype(BF16), row(ffn_norm), w_r, b_r)
    final_w = (row(ple_norm), w_ple_gate.astype(BF16), w_ple_proj.astype(BF16), row(final_norm))
    p2 = p.reshape(n, -1)

    h1, hn, rf, ri, cnt = _merge(x2, fox, ro.reshape(n, RET_V_W), *merge_w, tm, 0, n)

    a = n * TOP_K
    nb = -(-(a + N_EXPERTS * (bm - 1)) // bm)
    ri_t = ri.T
    counts = cnt[0, :N_EXPERTS].astype(jnp.int32)
    padded = (counts + bm - 1) // bm * bm
    pad_end = jnp.cumsum(padded)
    first_row = pad_end - counts
    experts = jnp.arange(N_EXPERTS, dtype=jnp.int32)

    def lookup(ids):
        return jnp.sum(jnp.where(ids[None] == experts.reshape((-1,) + (1,) * ids.ndim),
                                 first_row.reshape((-1,) + (1,) * ids.ndim), 0), axis=0)

    dest_t = lookup(ri_t[:TOP_K]) + ri_t[TOP_K:]
    block_start = jnp.arange(nb, dtype=jnp.int32) * bm
    block_e = jnp.minimum(jnp.sum(block_start[:, None] >= pad_end[None, :], axis=1),
                          N_EXPERTS - 1).astype(jnp.int32)
    valid = jnp.where(block_start < pad_end[-1],
                      jnp.clip(block_start + bm - lookup(block_e), 0, bm), 0).astype(jnp.int32)

    x_buf = _sc_dispatch(hn, dest_t, nb * bm)
    y_buf = _experts(block_e, valid, x_buf, w_gate_up, bgu, w_down, b_down.reshape(ne, 1, dd), bm)

    shares = [1] + [2 ** max(g - 1, 0) for g in range(1, groups)]
    unit = n // sum(shares)
    out = None
    start = 0
    for share in shares:
        ng = share * unit
        idx = dest_t[:, start:start + ng].reshape(1, ng * TOP_K)
        yg = _sc_gather(y_buf, idx).reshape(d // 2 // SC_COLS, TOP_K, ng, SC_COLS)
        out = _final(h1, yg, rf, p2, *final_w, tm, start // tm, out)
        start += ng
    return out.reshape(b, s, d)


def kernel(x, p, mix_norm, w_in, b_forget, w_branch, w_merge_gate, b_merge_gate, w_out, ffn_norm, w_router,
           b_router, w_gate_up, b_gate_up, w_down, b_down, ple_norm, w_ple_gate, w_ple_proj, final_norm):
    depth = p.shape[0]
    assert depth == 1, "the final norm is fused into the (single) layer"
    return _layer(x, p[0], mix_norm[0], w_in[0], b_forget[0], w_branch[0], w_merge_gate[0], b_merge_gate[0],
                  w_out[0], ffn_norm[0], w_router[0], b_router[0], w_gate_up[0], b_gate_up[0], w_down[0],
                  b_down[0], ple_norm[0], w_ple_gate[0], w_ple_proj[0], final_norm,
                  tm=1024, tq=512, chunk=128, bm=1024, groups=4)
```

```python
import functools

import numpy as np
import jax
import jax.numpy as jnp
from jax import lax
from jax.experimental import pallas as pl
from jax.experimental.pallas import tpu as pltpu
from jax.experimental.pallas import tpu_sc as plsc

FOX_HEADS = 8
FOX_HEAD_DIM = 64
RET_HEADS = 4
RET_KEY_DIM = 64
RET_VAL_DIM = 128
ROPE_BASE = 10000.0
N_EXPERTS = 32
TOP_K = 4
SWIGLU_LIMIT = 7.0
SWIGLU_ALPHA = 1.702
EPS = 1e-6

LANES = 128
FOX_W = FOX_HEADS * FOX_HEAD_DIM
RET_QK_W = RET_HEADS * RET_KEY_DIM
RET_V_W = RET_HEADS * RET_VAL_DIM
VMEM_LIMIT = 56 * 1024 * 1024
HIGH_HALF = -65536
LOG2_E = 1.4426950408889634
SC_WINDOW = 128
SC_COLS = 256

F32 = jnp.float32
BF16 = jnp.bfloat16


def _rms(x, g):
    return x * lax.rsqrt(jnp.mean(x * x, axis=-1, keepdims=True) + EPS) * g


def _sigmoid(x):
    return 1.0 / (1.0 + jnp.exp(-x))


def _dot(a, b):
    return jnp.dot(a, b, preferred_element_type=F32)


def _dot_nt(a, b):
    return lax.dot_general(a, b, (((1,), (1,)), ((), ())), preferred_element_type=F32)


def _pack_pairs(v):
    bits = pltpu.bitcast(v.astype(BF16).astype(F32), jnp.int32)
    half = v.shape[1] // 2
    return lax.shift_right_logical(bits[:, :half], 16) | (bits[:, half:] & HIGH_HALF)


def _unpack_pairs(chunks):
    return jnp.concatenate([pltpu.bitcast(lax.shift_left(w, 16), F32) for w in chunks]
                           + [pltpu.bitcast(w & HIGH_HALF, F32) for w in chunks], axis=1)


def _store_chunks(ref, v, rows=slice(None)):
    for c in range(ref.shape[0]):
        ref[c, rows, :] = v[:, c * SC_COLS:(c + 1) * SC_COLS]


def _interleave(generators):
    live = []
    pending = list(generators)
    while pending or live:
        if pending:
            live.append(pending.pop(0))
        for g in list(live):
            try:
                next(g)
            except StopIteration:
                live.remove(g)


def _resident(shape):
    return pl.BlockSpec(shape, lambda *_: (0,) * len(shape), pipeline_mode=pl.Buffered(1))


def _params(n_axes):
    return pltpu.CompilerParams(dimension_semantics=("arbitrary",) * n_axes,
                                vmem_limit_bytes=VMEM_LIMIT)


C_MID = FOX_HEADS
C_LO = 2 * FOX_HEADS
C_ONE = 3 * FOX_HEADS
N_AUG = 3
AUG_W = 2 * N_AUG
K_AUG0 = FOX_HEADS * AUG_W


def _split3(t):
    hi = t.astype(BF16).astype(F32)
    r = t - hi
    mid = r.astype(BF16).astype(F32)
    lo = (r - mid).astype(BF16).astype(F32)
    return hi + pltpu.roll(mid, C_MID, 1) + pltpu.roll(lo, C_LO, 1)


INPROJ_SUB = 256


def _inproj_kernel(x_ref, g_ref, w_ref, wvt_ref, wf_ref, bf_ref, cos_ref, sin_ref, place_ref,
                   q_ref, k_ref, vt_ref, rq_ref, rk_ref, rv_ref, rg_ref, carry_scr, *, tiles_per_seq):
    i = pl.program_id(0)
    tm = x_ref.shape[0]
    sub = min(INPROJ_SUB, tm)

    @pl.when(i % tiles_per_seq == 0)
    def _():
        carry_scr[...] = jnp.zeros(carry_scr.shape, F32)

    lane = lax.broadcasted_iota(jnp.int32, (sub, LANES), 1)
    r = lax.broadcasted_iota(jnp.int32, (sub, sub), 0)
    c = lax.broadcasted_iota(jnp.int32, (sub, sub), 1)
    tri = jnp.where(c <= r, 1.0, 0.0).astype(BF16)
    own = lane < FOX_HEAD_DIM
    is_aug = jnp.logical_and(lane >= FOX_HEAD_DIM, lane < FOX_HEAD_DIM + AUG_W)
    half = RET_KEY_DIM // 2
    lane_r = lax.broadcasted_iota(jnp.int32, (sub, RET_QK_W), 1)
    first = (lane_r % RET_KEY_DIM) < half
    carries = [carry_scr[...]]

    def sub_tile(t):
        rows = slice(t * sub, (t + 1) * sub)
        xn = _rms(x_ref[rows, :], g_ref[...]).astype(BF16)
        yield
        u = _dot(xn, w_ref[...])
        vt_ref[:, rows] = _dot_nt(wvt_ref[...], xn).astype(BF16)
        z = _dot(xn, wf_ref[...]) + bf_ref[...]
        yield
        lf = jnp.where(lane < FOX_HEADS, jnp.minimum(z, 0.0) - jnp.log1p(jnp.exp(-jnp.abs(z))), 0.0)
        ps = _dot(tri, _split3(lf).astype(BF16))
        yield
        assert len(carries) == t + 1
        cum = ps + pltpu.roll(ps, LANES - C_MID, 1) + pltpu.roll(ps, LANES - C_LO, 1)
        cum = jnp.where(lane < FOX_HEADS, cum, 0.0) + carries[t]
        carries.append(cum[sub - 1:sub, :])
        c3 = (_split3(cum * LOG2_E) + jnp.where(lane == C_ONE, 1.0, 0.0)).astype(BF16)
        aug = _dot(c3, place_ref[...])

        o = 2 * FOX_W
        rq = u[:, o:o + RET_QK_W]; o += RET_QK_W
        rk = u[:, o:o + RET_QK_W]; o += RET_QK_W
        rv_ref[rows, :] = u[:, o:o + RET_V_W].astype(BF16); o += RET_V_W
        rg_ref[rows, :] = u[:, o:o + RET_V_W]

        cos = cos_ref[rows, :]
        sin = sin_ref[rows, :]

        def rot(v):
            partner = jnp.where(first, -pltpu.roll(v, RET_QK_W - half, 1), pltpu.roll(v, half, 1))
            return v * cos + partner * sin

        rq_ref[rows, :] = rot(rq).astype(BF16)
        rk_ref[rows, :] = (rot(rk) * (RET_KEY_DIM ** -0.5)).astype(BF16)
        yield
        fq = u[:, 0:FOX_W] * (FOX_HEAD_DIM ** -0.5 * LOG2_E)
        fk = u[:, FOX_W:2 * FOX_W]
        for src, aug0, dst in ((fq, 0, q_ref), (fk, K_AUG0, k_ref)):
            for h in range(FOX_HEADS):
                blk = src[:, (h // 2) * LANES:(h // 2 + 1) * LANES]
                if h % 2:
                    blk = pltpu.roll(blk, FOX_HEAD_DIM, 1)
                mine = pltpu.roll(aug, (FOX_HEAD_DIM - aug0 - AUG_W * h) % LANES, 1)
                slab = jnp.where(own, blk, jnp.where(is_aug, mine, 0.0))
                dst[rows, h * LANES:(h + 1) * LANES] = slab.astype(BF16)

    _interleave(sub_tile(t) for t in range(tm // sub))
    carry_scr[...] = carries[-1]


def _placement():
    place = np.zeros((LANES, LANES), np.float32)
    for h in range(FOX_HEADS):
        for part, src in enumerate((h, C_MID + h, C_LO + h)):
            place[C_ONE, AUG_W * h + part] = 1.0
            place[src, AUG_W * h + N_AUG + part] = 1.0
            place[src, K_AUG0 + AUG_W * h + part] = -1.0
            place[C_ONE, K_AUG0 + AUG_W * h + N_AUG + part] = 1.0
    return jnp.asarray(place, BF16)


def _inproj(x2, g, w_main, w_vt, w_f, b_f, cos_t, sin_t, seq, tm):
    n, d = x2.shape
    wn = w_main.shape[1]
    spt = seq // tm
    aw = FOX_HEADS * LANES
    place = _placement()
    row = lambda i: (i, 0)
    const = lambda i: (0, 0)
    pos = lambda i: (i % spt, 0)
    out_shape = [
        jax.ShapeDtypeStruct((n, aw), BF16), jax.ShapeDtypeStruct((n, aw), BF16),
        jax.ShapeDtypeStruct((FOX_W, n), BF16),
        jax.ShapeDtypeStruct((n, RET_QK_W), BF16), jax.ShapeDtypeStruct((n, RET_QK_W), BF16),
        jax.ShapeDtypeStruct((n, RET_V_W), BF16), jax.ShapeDtypeStruct((n, RET_V_W), F32),
    ]
    out_specs = [
        pl.BlockSpec((tm, aw), row), pl.BlockSpec((tm, aw), row),
        pl.BlockSpec((FOX_W, tm), lambda i: (0, i)),
        pl.BlockSpec((tm, RET_QK_W), row), pl.BlockSpec((tm, RET_QK_W), row),
        pl.BlockSpec((tm, RET_V_W), row), pl.BlockSpec((tm, RET_V_W), row),
    ]
    return pl.pallas_call(
        functools.partial(_inproj_kernel, tiles_per_seq=spt),
        grid=(n // tm,),
        in_specs=[
            pl.BlockSpec((tm, d), row), pl.BlockSpec((1, d), const),
            _resident((d, wn)), _resident((FOX_W, d)),
            _resident((d, LANES)), pl.BlockSpec((1, LANES), const),
            pl.BlockSpec((tm, RET_QK_W), pos), pl.BlockSpec((tm, RET_QK_W), pos),
            _resident((LANES, LANES)),
        ],
        out_specs=out_specs,
        out_shape=out_shape,
        scratch_shapes=[pltpu.VMEM((1, LANES), F32)],
        compiler_params=_params(1),
        name="inproj",
    )(x2, g, w_main, w_vt, w_f, b_f, cos_t, sin_t, place)


def _fox_kernel(q_ref, k_ref, vt_ref, o_ref, *, tq):
    seq = q_ref.shape[0]
    half = tq // 2
    vrow = lax.broadcasted_iota(jnp.int32, (LANES, tq), 0)
    one = jnp.ones((), BF16)
    items = [(qi, j) for qi in range(seq // tq) for j in range(qi + 1)]

    def logits(qi, j):
        out = []
        for a in range(2):
            sl = slice(a * LANES, (a + 1) * LANES)
            kk = k_ref[j * tq:(j + 1) * tq, sl]
            qq = q_ref[qi * tq:(qi + 1) * tq, sl]
            if j != qi:
                out.append([(0, tq, _dot_nt(kk, qq))])
                continue
            parts = []
            for c0, nk in ((0, half), (half, tq)):
                sc = _dot_nt(kk[:nk], qq[c0:c0 + half])
                key = lax.broadcasted_iota(jnp.int32, sc.shape, 0)
                qry = lax.broadcasted_iota(jnp.int32, sc.shape, 1) + c0
                parts.append((c0, nk, jnp.where(key <= qry, sc, -jnp.inf)))
            out.append(parts)
        return out

    s_cur = logits(*items[0])
    carry = None
    for w, (qi, j) in enumerate(items):
        s_next = logits(*items[w + 1]) if w + 1 < len(items) else None
        if j == 0:
            carry = [(jnp.full((1, tq), -jnp.inf, F32), jnp.zeros((LANES, tq), F32)) for _ in range(2)]
        v = vt_ref[:, j * tq:(j + 1) * tq]
        for a in range(2):
            m_prev, acc = carry[a]
            va = jnp.where((vrow // FOX_HEAD_DIM) == a, v, one)
            ms, accs = [], []
            for c0, nk, sc in s_cur[a]:
                cols = slice(c0, c0 + sc.shape[1])
                m_new = jnp.maximum(m_prev[:, cols], jnp.max(sc, axis=0, keepdims=True))
                alpha = jnp.exp2(m_prev[:, cols] - m_new)
                p = jnp.exp2(sc - m_new).astype(BF16)
                ms.append(m_new)
                accs.append(alpha * acc[:, cols] + _dot(va[:, :nk], p))
            carry[a] = (ms[0], accs[0]) if len(ms) == 1 else (jnp.concatenate(ms, axis=1),
                                                               jnp.concatenate(accs, axis=1))
        if j == qi:
            acc0, acc1 = carry[0][1], carry[1][1]
            ot = jnp.where(vrow < FOX_HEAD_DIM, acc0 / acc0[FOX_HEAD_DIM:FOX_HEAD_DIM + 1, :],
                           acc1 / acc1[0:1, :])
            o_ref[qi * tq:(qi + 1) * tq, :] = ot.T.astype(o_ref.dtype)
        s_cur = s_next


def _fox(q_aug, k_aug, v_t, batch, seq, tq):
    n = q_aug.shape[0]
    pairs = FOX_HEADS // 2
    kern = functools.partial(_fox_kernel, tq=tq)
    return pl.pallas_call(
        kern,
        grid=(batch, pairs),
        in_specs=[
            pl.BlockSpec((seq, 2 * LANES), lambda i, h: (i, h)),
            pl.BlockSpec((seq, 2 * LANES), lambda i, h: (i, h)),
            pl.BlockSpec((LANES, seq), lambda i, h: (h, i)),
        ],
        out_specs=pl.BlockSpec((seq, LANES), lambda i, h: (i, h)),
        out_shape=jax.ShapeDtypeStruct((n, FOX_W), BF16),
        compiler_params=_params(2),
        name="fox_attention",
    )(q_aug, k_aug, v_t)


def _ret_kernel(lg_ref, q_ref, k_ref, v_ref, g_ref, o_ref, *, chunk):
    s = q_ref.shape[1]
    lane = lax.broadcasted_iota(jnp.int32, (chunk, LANES), 1)
    ri = lax.broadcasted_iota(jnp.int32, (chunk, chunk), 0)
    ci = lax.broadcasted_iota(jnp.int32, (chunk, chunk), 1)
    diff = (ri - ci).astype(F32)
    pos = lax.broadcasted_iota(jnp.int32, (chunk, 1), 0).astype(F32)

    def head(h):
        lg = lg_ref[h]
        mine = (lane // RET_KEY_DIM) == (h % 2)
        qk = slice((h // 2) * LANES, (h // 2 + 1) * LANES)
        vs = slice(h * RET_VAL_DIM, (h + 1) * RET_VAL_DIM)
        inner = jnp.where(diff >= 0, jnp.exp(jnp.maximum(diff, 0.0) * lg), 0.0)
        q_decay = jnp.exp((pos + 1.0) * lg)
        k_decay = jnp.exp((chunk - 1.0 - pos) * lg)
        chunk_decay = jnp.exp(jnp.full((1, 1), chunk, F32) * lg)
        state = jnp.zeros((LANES, RET_VAL_DIM), F32)
        for c in range(s // chunk):
            rows = slice(c * chunk, (c + 1) * chunk)
            q = jnp.where(mine, q_ref[0, rows, qk], jnp.zeros((), BF16))
            k = jnp.where(mine, k_ref[0, rows, qk], jnp.zeros((), BF16))
            v = v_ref[0, rows, vs]
            scores = _dot_nt(q, k)
            inter = _dot(q, state.astype(BF16))
            kd = (k.astype(F32) * k_decay).astype(BF16)
            update = lax.dot_general(kd, v, (((0,), (0,)), ((), ())), preferred_element_type=F32)
            yield
            o = _dot((scores * inner).astype(BF16), v) + inter * q_decay
            state = state * chunk_decay + update
            yield
            o = o * lax.rsqrt(jnp.mean(o * o, axis=-1, keepdims=True) + EPS)
            g = g_ref[0, rows, vs]
            o_ref[0, rows, vs] = (o * (g * _sigmoid(g))).astype(o_ref.dtype)

    _interleave(head(h) for h in range(RET_HEADS))


def _retention(lg, rq, rk, rv, rg, chunk):
    b, s, _ = rq.shape
    kern = functools.partial(_ret_kernel, chunk=chunk)
    qk_spec = pl.BlockSpec((1, s, RET_QK_W), lambda i, lg_ref: (i, 0, 0))
    v_spec = pl.BlockSpec((1, s, RET_V_W), lambda i, lg_ref: (i, 0, 0))
    return pl.pallas_call(
        kern,
        grid_spec=pltpu.PrefetchScalarGridSpec(
            num_scalar_prefetch=1,
            grid=(b,),
            in_specs=[qk_spec, qk_spec, v_spec, v_spec],
            out_specs=v_spec,
        ),
        out_shape=jax.ShapeDtypeStruct((b, s, RET_V_W), BF16),
        compiler_params=_params(1),
        name="retention",
    )(lg, rq, rk, rv, rg)


MERGE_SUB = 256


def _merge_kernel(x_ref, fox_ref, ro_ref, gmix_ref, wmg_ref, bmg_ref, wb_ref, wout_ref, gffn_ref,
                  wr_ref, br_ref, h1_ref, hn_ref, rf_ref, ri_ref, cnt_ref, base_scr):
    i = pl.program_id(0)
    tm, d = x_ref.shape
    sub = min(MERGE_SUB, tm)

    @pl.when(i == 0)
    def _():
        base_scr[...] = jnp.zeros(base_scr.shape, F32)

    lane = lax.broadcasted_iota(jnp.int32, (sub, LANES), 1)
    lane_f = lane.astype(F32)
    r = lax.broadcasted_iota(jnp.int32, (sub, sub), 0)
    c = lax.broadcasted_iota(jnp.int32, (sub, sub), 1)
    tri = jnp.where(c < r, 1.0, 0.0).astype(BF16)
    counts = [base_scr[...]]

    def sub_tile(t):
        rows = slice(t * sub, (t + 1) * sub)
        x = x_ref[rows, :]
        xn = _rms(x, gmix_ref[...]).astype(BF16)
        yield
        pre = _dot(xn, wmg_ref[...])
        yield
        gate = _sigmoid(pre + bmg_ref[...])
        yield
        pf = _dot(fox_ref[rows, :], wb_ref[0])
        pr = _dot(ro_ref[rows, :], wb_ref[1])
        yield
        merged = (gate[:, :d] * pf + gate[:, d:] * pr).astype(BF16)
        yield
        h1 = x + _dot(merged, wout_ref[...])
        yield
        h1_ref[rows, :] = h1
        hn = _rms(h1, gffn_ref[...]).astype(BF16)
        _store_chunks(hn_ref, _pack_pairs(hn), rows)
        yield
        logits = _dot(hn, wr_ref[...]) + br_ref[...]
        yield
        vals, idxs = [], []
        cur = logits
        for _ in range(TOP_K):
            m = jnp.max(cur, axis=-1, keepdims=True)
            idx = jnp.argmax(cur, axis=-1, keepdims=True).astype(F32)
            vals.append(m)
            idxs.append(idx)
            cur = jnp.where(lane_f == idx, -jnp.inf, cur)
        exps = [jnp.exp(v - vals[0]) for v in vals]
        denom = exps[0] + exps[1] + exps[2] + exps[3]
        onehot = jnp.zeros(logits.shape, F32)
        for idx in idxs:
            onehot = onehot + jnp.where(lane_f == idx, 1.0, 0.0)
        yield
        assert len(counts) == t + 1
        before = _dot(tri, onehot.astype(BF16)) + counts[t]
        counts.append(counts[t] + jnp.sum(onehot, axis=0, keepdims=True))
        rf = jnp.zeros(logits.shape, F32)
        ri = jnp.zeros(logits.shape, F32)
        for j in range(TOP_K):
            rank = jnp.sum(jnp.where(lane_f == idxs[j], before, 0.0), axis=-1, keepdims=True)
            rf = jnp.where(lane == j, exps[j] / denom, rf)
            ri = jnp.where(lane == j, idxs[j], ri)
            ri = jnp.where(lane == TOP_K + j, rank, ri)
        rf_ref[rows, :] = rf[:, :2 * TOP_K]
        ri_ref[rows, :] = ri[:, :2 * TOP_K].astype(jnp.int32)

    _interleave(sub_tile(t) for t in range(tm // sub))
    base_scr[...] = counts[-1]
    cnt_ref[...] = counts[-1]


def _merge(x2, fox, ro, gmix, wmg, bmg, wb, wout, gffn, wr, br, tm, first_tile, n):
    d = x2.shape[1]
    row = lambda i: (i, 0)
    src = lambda i: (i + first_tile, 0)
    const = lambda i: (0, 0)
    return pl.pallas_call(
        _merge_kernel,
        grid=(n // tm,),
        in_specs=[
            pl.BlockSpec((tm, d), src), pl.BlockSpec((tm, FOX_W), src), pl.BlockSpec((tm, RET_V_W), src),
            pl.BlockSpec((1, d), const), _resident((d, 2 * d)), pl.BlockSpec((1, 2 * d), const),
            _resident((2, FOX_W, d)), _resident((d, d)),
            pl.BlockSpec((1, d), const), _resident((d, LANES)), pl.BlockSpec((1, LANES), const),
        ],
        out_specs=[
            pl.BlockSpec((tm, d), row), pl.BlockSpec((d // 2 // SC_COLS, tm, SC_COLS), lambda i: (0, i, 0)),
            pl.BlockSpec((tm, 2 * TOP_K), row), pl.BlockSpec((tm, 2 * TOP_K), row),
            pl.BlockSpec((1, LANES), const),
        ],
        out_shape=[
            jax.ShapeDtypeStruct((n, d), F32), jax.ShapeDtypeStruct((d // 2 // SC_COLS, n, SC_COLS), jnp.int32),
            jax.ShapeDtypeStruct((n, 2 * TOP_K), F32), jax.ShapeDtypeStruct((n, 2 * TOP_K), jnp.int32),
            jax.ShapeDtypeStruct((1, LANES), F32),
        ],
        scratch_shapes=[pltpu.VMEM((1, LANES), F32)],
        compiler_params=_params(1),
        name="merge_router",
    )(x2, fox, ro, gmix, wmg, bmg, wb, wout, gffn, wr, br)


def _sc_mesh():
    return plsc.VectorSubcoreMesh(core_axis_name="core", subcore_axis_name="subcore")


def _sc_dispatch(rows, dest_t, n_out):
    chunks, n, w = rows.shape

    @functools.partial(pl.kernel, out_type=jax.ShapeDtypeStruct((chunks, n_out, w), rows.dtype),
                       mesh=_sc_mesh(), scratch_types=[], name="moe_dispatch")
    def run(x_hbm, i_hbm, o_hbm):
        for c in range(chunks):
            def body(x_vmem, i_vmem, c=c):
                for j in range(TOP_K):
                    pltpu.sync_copy(x_vmem, o_hbm.at[c].at[i_vmem.at[j]])

            pltpu.emit_pipeline(
                body,
                grid=(n // SC_WINDOW,),
                in_specs=[pl.BlockSpec((SC_WINDOW, w), lambda i: (i, 0)),
                          pl.BlockSpec((TOP_K, SC_WINDOW), lambda i: (0, i))],
                out_specs=[],
                core_axis_name=("core", "subcore"),
                dimension_semantics=(pltpu.PARALLEL,),
            )(x_hbm.at[c], i_hbm)

    return run(rows, dest_t)


def _sc_gather(table, idx):
    chunks, _, w = table.shape
    m = idx.shape[1]

    @functools.partial(pl.kernel, out_type=jax.ShapeDtypeStruct((chunks, m, w), table.dtype),
                       mesh=_sc_mesh(), scratch_types=[], name="moe_combine_gather")
    def run(t_hbm, i_hbm, o_hbm):
        for c in range(chunks):
            def body(i_vmem, o_vmem, c=c):
                pltpu.sync_copy(t_hbm.at[c].at[i_vmem.at[0]], o_vmem)

            pltpu.emit_pipeline(
                body,
                grid=(m // SC_WINDOW,),
                in_specs=[pl.BlockSpec((1, SC_WINDOW), lambda i: (0, i))],
                out_specs=[pl.BlockSpec((SC_WINDOW, w), lambda i: (i, 0))],
                core_axis_name=("core", "subcore"),
                dimension_semantics=(pltpu.PARALLEL,),
            )(i_hbm, o_hbm.at[c])

    return run(table, idx)


GLU_GROUP = 2 * LANES
EXPERT_SUB = 256


def _expert_kernel(be_ref, valid_ref, x_ref, wgu_ref, bgu_ref, wd_ref, bd_ref, perm_ref, y_ref, wgu_scr, wd_scr):
    i = pl.program_id(0)
    f2 = wgu_ref.shape[2]
    bm = x_ref.shape[1]
    sub = min(EXPERT_SUB, bm)
    valid = valid_ref[i]
    live = valid > 0

    @pl.when(jnp.logical_and(live, jnp.logical_or(i == 0, be_ref[i] != be_ref[jnp.maximum(i - 1, 0)])))
    def _():
        for b in range(f2 // GLU_GROUP):
            cols = slice(b * GLU_GROUP, (b + 1) * GLU_GROUP)
            wgu_scr[:, cols] = _dot(wgu_ref[0, :, cols].astype(BF16), perm_ref[...]).astype(BF16)
        wd_scr[...] = wd_ref[0].astype(BF16)

    def sub_block(t, sub):
        rows = slice(t * sub, (t + 1) * sub)
        x = _unpack_pairs([x_ref[c, rows, :] for c in range(x_ref.shape[0])]).astype(BF16)
        yield
        gu = _dot(x, wgu_scr[...])
        yield
        gu = gu + bgu_ref[0]
        acts = []
        for b in range(f2 // GLU_GROUP):
            glu = jnp.minimum(gu[:, b * GLU_GROUP:b * GLU_GROUP + LANES], SWIGLU_LIMIT)
            lin = jnp.clip(gu[:, b * GLU_GROUP + LANES:(b + 1) * GLU_GROUP], -SWIGLU_LIMIT, SWIGLU_LIMIT)
            acts.append((glu * _sigmoid(SWIGLU_ALPHA * glu) * (lin + 1.0)).astype(BF16))
        act = jnp.concatenate(acts, axis=1)
        yield
        y = _dot(act, wd_scr[...])
        yield
        _store_chunks(y_ref, _pack_pairs(y + bd_ref[0]), rows)

    @pl.when(valid == bm)
    def _():
        _interleave(sub_block(t, sub) for t in range(bm // sub))

    for t in range(bm // sub):
        @pl.when(jnp.logical_and(valid < bm, valid > bm - (t + 1) * sub))
        def _(t=t):
            _interleave([sub_block(t, sub)])


def _experts(block_e, valid, x_buf, wgu, bgu, wd, bd, bm):
    p = x_buf.shape[1]
    f, d = wd.shape[1:]
    perm = np.zeros((GLU_GROUP, GLU_GROUP), np.float32)
    for c in range(LANES):
        perm[2 * c, c] = 1.0
        perm[2 * c + 1, LANES + c] = 1.0
    return pl.pallas_call(
        _expert_kernel,
        grid_spec=pltpu.PrefetchScalarGridSpec(
            num_scalar_prefetch=2,
            grid=(p // bm,),
            in_specs=[
                pl.BlockSpec((d // 2 // SC_COLS, bm, SC_COLS), lambda i, be, nv: (0, i, 0)),
                pl.BlockSpec((1, d, 2 * f), lambda i, be, nv: (be[i], 0, 0)),
                pl.BlockSpec((1, 1, 2 * f), lambda i, be, nv: (be[i], 0, 0)),
                pl.BlockSpec((1, f, d), lambda i, be, nv: (be[i], 0, 0)),
                pl.BlockSpec((1, 1, d), lambda i, be, nv: (be[i], 0, 0)),
                pl.BlockSpec((GLU_GROUP, GLU_GROUP), lambda i, be, nv: (0, 0)),
            ],
            out_specs=pl.BlockSpec((d // 2 // SC_COLS, bm, SC_COLS), lambda i, be, nv: (0, i, 0)),
            scratch_shapes=[pltpu.VMEM((d, 2 * f), BF16), pltpu.VMEM((f, d), BF16)],
        ),
        out_shape=jax.ShapeDtypeStruct((d // 2 // SC_COLS, p, SC_COLS), jnp.int32),
        compiler_params=_params(1),
        name="expert_ffn",
    )(block_e, valid, x_buf, wgu, bgu, wd, bd, jnp.asarray(perm, BF16))


FINAL_SUB = 256


def _final_kernel(h1_ref, yg_ref, rf_ref, p_ref, gple_ref, wpg_ref, wpp_ref, gfin_ref, *rest):
    o_ref = rest[-1]
    tm = h1_ref.shape[0]
    sub = min(FINAL_SUB, tm)

    def sub_tile(t):
        rows = slice(t * sub, (t + 1) * sub)
        rf = rf_ref[rows, :]
        h2 = h1_ref[rows, :]
        for j in range(TOP_K):
            h2 = h2 + _unpack_pairs([yg_ref[c, j, rows, :] for c in range(yg_ref.shape[0])]) * rf[:, j:j + 1]
        hn = _rms(h2, gple_ref[...]).astype(BF16)
        yield
        pre = _dot(hn, wpg_ref[...])
        proj = _dot(p_ref[rows, :].astype(BF16), wpp_ref[...])
        yield
        h3 = h2 + _sigmoid(pre) * proj
        o_ref[rows, :] = _rms(h3, gfin_ref[...])

    _interleave(sub_tile(t) for t in range(tm // sub))


def _final(h1, yg, rf, p2, gple, wpg, wpp, gfin, tm, first_tile, prev_out):
    n_total, d = h1.shape
    n = yg.shape[2]
    pd = p2.shape[1]
    dst = lambda i: (i + first_tile, 0)
    const = lambda i: (0, 0)
    in_specs = [
        pl.BlockSpec((tm, d), dst), pl.BlockSpec((d // 2 // SC_COLS, TOP_K, tm, SC_COLS), lambda i: (0, 0, i, 0)),
        pl.BlockSpec((tm, 2 * TOP_K), dst), pl.BlockSpec((tm, pd), dst),
        pl.BlockSpec((1, d), const), pl.BlockSpec((d, d), const), pl.BlockSpec((pd, d), const),
        pl.BlockSpec((1, d), const),
    ]
    args = [h1, yg, rf, p2, gple, wpg, wpp, gfin]
    aliases = {}
    if prev_out is not None:
        in_specs.append(pl.BlockSpec(memory_space=pl.ANY))
        aliases = {len(args): 0}
        args.append(prev_out)
    return pl.pallas_call(
        _final_kernel,
        grid=(n // tm,),
        in_specs=in_specs,
        out_specs=pl.BlockSpec((tm, d), dst),
        out_shape=jax.ShapeDtypeStruct((n_total, d), F32),
        input_output_aliases=aliases,
        compiler_params=_params(1),
        name="combine_ple_norm",
    )(*args)


def _layer(h, p, mix_norm, w_in, b_forget, w_branch, w_merge_gate, b_merge_gate, w_out, ffn_norm,
           w_router, b_router, w_gate_up, b_gate_up, w_down, b_down, ple_norm, w_ple_gate, w_ple_proj,
           final_norm, *, tm, tq, chunk, bm, groups):
    b, s, d = h.shape
    n = b * s
    x2 = h.reshape(n, d)
    row = lambda t: t.reshape(1, -1)

    c0 = 3 * FOX_W
    w_main = jnp.concatenate([w_in[:, :2 * FOX_W], w_in[:, c0 + FOX_HEADS:]], axis=1).astype(BF16)
    w_vt = w_in[:, 2 * FOX_W:c0].T.astype(BF16)
    w_f = jnp.pad(w_in[:, c0:c0 + FOX_HEADS], ((0, 0), (0, LANES - FOX_HEADS))).astype(BF16)
    b_f = jnp.pad(b_forget, (0, LANES - FOX_HEADS)).reshape(1, LANES)
    half = RET_KEY_DIM // 2
    inv = ROPE_BASE ** (-jnp.arange(half, dtype=F32) / half)
    ang = jnp.arange(s).astype(F32)[:, None] * inv[None, :]
    cos_t = jnp.tile(jnp.cos(ang), (1, RET_QK_W // half))
    sin_t = jnp.tile(jnp.sin(ang), (1, RET_QK_W // half))

    q_aug, k_aug, v_t, rq, rk, rv, rg = _inproj(x2, row(mix_norm), w_main, w_vt, w_f, b_f, cos_t, sin_t, s, tm)
    fox = _fox(q_aug, k_aug, v_t, b, s, tq)

    lg = jnp.log1p(-jnp.exp2(-5.0 - jnp.arange(RET_HEADS, dtype=F32)))
    ro = _retention(lg, rq.reshape(b, s, RET_QK_W), rk.reshape(b, s, RET_QK_W),
                    rv.reshape(b, s, RET_V_W), rg.reshape(b, s, RET_V_W), chunk)

    w_r = jnp.pad(w_router, ((0, 0), (0, LANES - N_EXPERTS))).astype(BF16)
    b_r = jnp.concatenate([b_router, jnp.full((LANES - N_EXPERTS,), -1e30, F32)]).reshape(1, LANES)
    ne, dd, f2 = w_gate_up.shape
    bgu = b_gate_up.reshape(ne, f2 // GLU_GROUP, LANES, 2).transpose(0, 1, 3, 2).reshape(ne, 1, f2)
    merge_w = (row(mix_norm), w_merge_gate.astype(BF16), row(b_merge_gate), w_branch.astype(BF16),
               w_out.astype(BF16), row(ffn_norm), w_r, b_r)
    final_w = (row(ple_norm), w_ple_gate.astype(BF16), w_ple_proj.astype(BF16), row(final_norm))
    p2 = p.reshape(n, -1)

    h1, hn, rf, ri, cnt = _merge(x2, fox, ro.reshape(n, RET_V_W), *merge_w, tm, 0, n)

    a = n * TOP_K
    nb = -(-(a + N_EXPERTS * (bm - 1)) // bm)
    ri_t = ri.T
    counts = cnt[0, :N_EXPERTS].astype(jnp.int32)
    padded = (counts + bm - 1) // bm * bm
    pad_end = jnp.cumsum(padded)
    first_row = pad_end - counts
    experts = jnp.arange(N_EXPERTS, dtype=jnp.int32)

    def lookup(ids):
        return jnp.sum(jnp.where(ids[None] == experts.reshape((-1,) + (1,) * ids.ndim),
                                 first_row.reshape((-1,) + (1,) * ids.ndim), 0), axis=0)

    dest_t = lookup(ri_t[:TOP_K]) + ri_t[TOP_K:]
    block_start = jnp.arange(nb, dtype=jnp.int32) * bm
    block_e = jnp.minimum(jnp.sum(block_start[:, None] >= pad_end[None, :], axis=1),
                          N_EXPERTS - 1).astype(jnp.int32)
    valid = jnp.where(block_start < pad_end[-1],
                      jnp.clip(block_start + bm - lookup(block_e), 0, bm), 0).astype(jnp.int32)

    x_buf = _sc_dispatch(hn, dest_t, nb * bm)
    y_buf = _experts(block_e, valid, x_buf, w_gate_up, bgu, w_down, b_down.reshape(ne, 1, dd), bm)

    shares = [1] + [2 ** max(g - 1, 0) for g in range(1, groups)]
    unit = n // sum(shares)
    out = None
    start = 0
    for share in shares:
        ng = share * unit
        idx = dest_t[:, start:start + ng].reshape(1, ng * TOP_K)
        yg = _sc_gather(y_buf, idx).reshape(d // 2 // SC_COLS, TOP_K, ng, SC_COLS)
        out = _final(h1, yg, rf, p2, *final_w, tm, start // tm, out)
        start += ng
    return out.reshape(b, s, d)


def kernel(x, p, mix_norm, w_in, b_forget, w_branch, w_merge_gate, b_merge_gate, w_out, ffn_norm, w_router,
           b_router, w_gate_up, b_gate_up, w_down, b_down, ple_norm, w_ple_gate, w_ple_proj, final_norm):
    depth = p.shape[0]
    assert depth == 1, "the final norm is fused into the (single) layer"
    return _layer(x, p[0], mix_norm[0], w_in[0], b_forget[0], w_branch[0], w_merge_gate[0], b_merge_gate[0],
                  w_out[0], ffn_norm[0], w_router[0], b_router[0], w_gate_up[0], b_gate_up[0], w_down[0],
                  b_down[0], ple_norm[0], w_ple_gate[0], w_ple_proj[0], final_norm,
                  tm=1024, tq=512, chunk=128, bm=1024, groups=5)
```

```python
import functools

import numpy as np
import jax
import jax.numpy as jnp
from jax import lax
from jax.experimental import pallas as pl
from jax.experimental.pallas import tpu as pltpu
from jax.experimental.pallas import tpu_sc as plsc

FOX_HEADS = 8
FOX_HEAD_DIM = 64
RET_HEADS = 4
RET_KEY_DIM = 64
RET_VAL_DIM = 128
ROPE_BASE = 10000.0
N_EXPERTS = 32
TOP_K = 4
SWIGLU_LIMIT = 7.0
SWIGLU_ALPHA = 1.702
EPS = 1e-6

LANES = 128
FOX_W = FOX_HEADS * FOX_HEAD_DIM
RET_QK_W = RET_HEADS * RET_KEY_DIM
RET_V_W = RET_HEADS * RET_VAL_DIM
VMEM_LIMIT = 56 * 1024 * 1024
HIGH_HALF = -65536
LOG2_E = 1.4426950408889634
SC_WINDOW = 128
SC_COLS = 256

F32 = jnp.float32
BF16 = jnp.bfloat16


def _rms(x, g):
    return x * lax.rsqrt(jnp.mean(x * x, axis=-1, keepdims=True) + EPS) * g


def _sigmoid(x):
    return 1.0 / (1.0 + jnp.exp(-x))


def _dot(a, b):
    return jnp.dot(a, b, preferred_element_type=F32)


def _dot_nt(a, b):
    return lax.dot_general(a, b, (((1,), (1,)), ((), ())), preferred_element_type=F32)


def _pack_pairs(v):
    bits = pltpu.bitcast(v.astype(BF16).astype(F32), jnp.int32)
    half = v.shape[1] // 2
    return lax.shift_right_logical(bits[:, :half], 16) | (bits[:, half:] & HIGH_HALF)


def _unpack_pairs(chunks):
    return jnp.concatenate([pltpu.bitcast(lax.shift_left(w, 16), F32) for w in chunks]
                           + [pltpu.bitcast(w & HIGH_HALF, F32) for w in chunks], axis=1)


def _store_chunks(ref, v, rows=slice(None)):
    for c in range(ref.shape[0]):
        ref[c, rows, :] = v[:, c * SC_COLS:(c + 1) * SC_COLS]


def _interleave(generators):
    live = []
    pending = list(generators)
    while pending or live:
        if pending:
            live.append(pending.pop(0))
        for g in list(live):
            try:
                next(g)
            except StopIteration:
                live.remove(g)


def _resident(shape):
    return pl.BlockSpec(shape, lambda *_: (0,) * len(shape), pipeline_mode=pl.Buffered(1))


def _params(n_axes):
    return pltpu.CompilerParams(dimension_semantics=("arbitrary",) * n_axes,
                                vmem_limit_bytes=VMEM_LIMIT)


C_MID = FOX_HEADS
C_LO = 2 * FOX_HEADS
C_ONE = 3 * FOX_HEADS
N_AUG = 3
AUG_W = 2 * N_AUG
K_AUG0 = FOX_HEADS * AUG_W


def _split3(t):
    hi = t.astype(BF16).astype(F32)
    r = t - hi
    mid = r.astype(BF16).astype(F32)
    lo = (r - mid).astype(BF16).astype(F32)
    return hi + pltpu.roll(mid, C_MID, 1) + pltpu.roll(lo, C_LO, 1)


INPROJ_SUB = 256


def _inproj_kernel(x_ref, g_ref, w_ref, wvt_ref, wf_ref, bf_ref, cos_ref, sin_ref, place_ref,
                   q_ref, k_ref, vt_ref, rq_ref, rk_ref, rv_ref, rg_ref, carry_scr, *, tiles_per_seq):
    i = pl.program_id(0)
    tm = x_ref.shape[0]
    sub = min(INPROJ_SUB, tm)

    @pl.when(i % tiles_per_seq == 0)
    def _():
        carry_scr[...] = jnp.zeros(carry_scr.shape, F32)

    lane = lax.broadcasted_iota(jnp.int32, (sub, LANES), 1)
    r = lax.broadcasted_iota(jnp.int32, (sub, sub), 0)
    c = lax.broadcasted_iota(jnp.int32, (sub, sub), 1)
    tri = jnp.where(c <= r, 1.0, 0.0).astype(BF16)
    own = lane < FOX_HEAD_DIM
    is_aug = jnp.logical_and(lane >= FOX_HEAD_DIM, lane < FOX_HEAD_DIM + AUG_W)
    half = RET_KEY_DIM // 2
    lane_r = lax.broadcasted_iota(jnp.int32, (sub, RET_QK_W), 1)
    first = (lane_r % RET_KEY_DIM) < half
    carries = [carry_scr[...]]

    def sub_tile(t):
        rows = slice(t * sub, (t + 1) * sub)
        xn = _rms(x_ref[rows, :], g_ref[...]).astype(BF16)
        yield
        u = _dot(xn, w_ref[...])
        vt_ref[:, rows] = _dot_nt(wvt_ref[...], xn).astype(BF16)
        z = _dot(xn, wf_ref[...]) + bf_ref[...]
        yield
        lf = jnp.where(lane < FOX_HEADS, jnp.minimum(z, 0.0) - jnp.log1p(jnp.exp(-jnp.abs(z))), 0.0)
        ps = _dot(tri, _split3(lf).astype(BF16))
        yield
        assert len(carries) == t + 1
        cum = ps + pltpu.roll(ps, LANES - C_MID, 1) + pltpu.roll(ps, LANES - C_LO, 1)
        cum = jnp.where(lane < FOX_HEADS, cum, 0.0) + carries[t]
        carries.append(cum[sub - 1:sub, :])
        c3 = (_split3(cum * LOG2_E) + jnp.where(lane == C_ONE, 1.0, 0.0)).astype(BF16)
        aug = _dot(c3, place_ref[...])

        o = 2 * FOX_W
        rq = u[:, o:o + RET_QK_W]; o += RET_QK_W
        rk = u[:, o:o + RET_QK_W]; o += RET_QK_W
        rv_ref[rows, :] = u[:, o:o + RET_V_W].astype(BF16); o += RET_V_W
        rg_ref[rows, :] = u[:, o:o + RET_V_W]

        cos = cos_ref[rows, :]
        sin = sin_ref[rows, :]

        def rot(v):
            partner = jnp.where(first, -pltpu.roll(v, RET_QK_W - half, 1), pltpu.roll(v, half, 1))
            return v * cos + partner * sin

        rq_ref[rows, :] = rot(rq).astype(BF16)
        rk_ref[rows, :] = (rot(rk) * (RET_KEY_DIM ** -0.5)).astype(BF16)
        yield
        fq = u[:, 0:FOX_W] * (FOX_HEAD_DIM ** -0.5 * LOG2_E)
        fk = u[:, FOX_W:2 * FOX_W]
        for src, aug0, dst in ((fq, 0, q_ref), (fk, K_AUG0, k_ref)):
            for h in range(FOX_HEADS):
                blk = src[:, (h // 2) * LANES:(h // 2 + 1) * LANES]
                if h % 2:
                    blk = pltpu.roll(blk, FOX_HEAD_DIM, 1)
                mine = pltpu.roll(aug, (FOX_HEAD_DIM - aug0 - AUG_W * h) % LANES, 1)
                slab = jnp.where(own, blk, jnp.where(is_aug, mine, 0.0))
                dst[rows, h * LANES:(h + 1) * LANES] = slab.astype(BF16)

    _interleave(sub_tile(t) for t in range(tm // sub))
    carry_scr[...] = carries[-1]


def _placement():
    place = np.zeros((LANES, LANES), np.float32)
    for h in range(FOX_HEADS):
        for part, src in enumerate((h, C_MID + h, C_LO + h)):
            place[C_ONE, AUG_W * h + part] = 1.0
            place[src, AUG_W * h + N_AUG + part] = 1.0
            place[src, K_AUG0 + AUG_W * h + part] = -1.0
            place[C_ONE, K_AUG0 + AUG_W * h + N_AUG + part] = 1.0
    return jnp.asarray(place, BF16)


def _inproj(x2, g, w_main, w_vt, w_f, b_f, cos_t, sin_t, seq, tm):
    n, d = x2.shape
    wn = w_main.shape[1]
    spt = seq // tm
    aw = FOX_HEADS * LANES
    place = _placement()
    row = lambda i: (i, 0)
    const = lambda i: (0, 0)
    pos = lambda i: (i % spt, 0)
    out_shape = [
        jax.ShapeDtypeStruct((n, aw), BF16), jax.ShapeDtypeStruct((n, aw), BF16),
        jax.ShapeDtypeStruct((FOX_W, n), BF16),
        jax.ShapeDtypeStruct((n, RET_QK_W), BF16), jax.ShapeDtypeStruct((n, RET_QK_W), BF16),
        jax.ShapeDtypeStruct((n, RET_V_W), BF16), jax.ShapeDtypeStruct((n, RET_V_W), F32),
    ]
    out_specs = [
        pl.BlockSpec((tm, aw), row), pl.BlockSpec((tm, aw), row),
        pl.BlockSpec((FOX_W, tm), lambda i: (0, i)),
        pl.BlockSpec((tm, RET_QK_W), row), pl.BlockSpec((tm, RET_QK_W), row),
        pl.BlockSpec((tm, RET_V_W), row), pl.BlockSpec((tm, RET_V_W), row),
    ]
    return pl.pallas_call(
        functools.partial(_inproj_kernel, tiles_per_seq=spt),
        grid=(n // tm,),
        in_specs=[
            pl.BlockSpec((tm, d), row), pl.BlockSpec((1, d), const),
            _resident((d, wn)), _resident((FOX_W, d)),
            _resident((d, LANES)), pl.BlockSpec((1, LANES), const),
            pl.BlockSpec((tm, RET_QK_W), pos), pl.BlockSpec((tm, RET_QK_W), pos),
            _resident((LANES, LANES)),
        ],
        out_specs=out_specs,
        out_shape=out_shape,
        scratch_shapes=[pltpu.VMEM((1, LANES), F32)],
        compiler_params=_params(1),
        name="inproj",
    )(x2, g, w_main, w_vt, w_f, b_f, cos_t, sin_t, place)


def _fox_kernel(q_ref, k_ref, vt_ref, o_ref, *, tq):
    seq = q_ref.shape[0]
    half = tq // 2
    vrow = lax.broadcasted_iota(jnp.int32, (LANES, tq), 0)
    one = jnp.ones((), BF16)
    items = [(qi, j) for qi in range(seq // tq) for j in range(qi + 1)]

    def logits(qi, j):
        out = []
        for a in range(2):
            sl = slice(a * LANES, (a + 1) * LANES)
            kk = k_ref[j * tq:(j + 1) * tq, sl]
            qq = q_ref[qi * tq:(qi + 1) * tq, sl]
            if j != qi:
                out.append([(0, tq, _dot_nt(kk, qq))])
                continue
            parts = []
            for c0, nk in ((0, half), (half, tq)):
                sc = _dot_nt(kk[:nk], qq[c0:c0 + half])
                key = lax.broadcasted_iota(jnp.int32, sc.shape, 0)
                qry = lax.broadcasted_iota(jnp.int32, sc.shape, 1) + c0
                parts.append((c0, nk, jnp.where(key <= qry, sc, -jnp.inf)))
            out.append(parts)
        return out

    s_cur = logits(*items[0])
    carry = None
    for w, (qi, j) in enumerate(items):
        s_next = logits(*items[w + 1]) if w + 1 < len(items) else None
        if j == 0:
            carry = [(jnp.full((1, tq), -jnp.inf, F32), jnp.zeros((LANES, tq), F32)) for _ in range(2)]
        v = vt_ref[:, j * tq:(j + 1) * tq]
        for a in range(2):
            m_prev, acc = carry[a]
            va = jnp.where((vrow // FOX_HEAD_DIM) == a, v, one)
            ms, accs = [], []
            for c0, nk, sc in s_cur[a]:
                cols = slice(c0, c0 + sc.shape[1])
                m_new = jnp.maximum(m_prev[:, cols], jnp.max(sc, axis=0, keepdims=True))
                alpha = jnp.exp2(m_prev[:, cols] - m_new)
                p = jnp.exp2(sc - m_new).astype(BF16)
                ms.append(m_new)
                accs.append(alpha * acc[:, cols] + _dot(va[:, :nk], p))
            carry[a] = (ms[0], accs[0]) if len(ms) == 1 else (jnp.concatenate(ms, axis=1),
                                                               jnp.concatenate(accs, axis=1))
        if j == qi:
            acc0, acc1 = carry[0][1], carry[1][1]
            ot = jnp.where(vrow < FOX_HEAD_DIM, acc0 / acc0[FOX_HEAD_DIM:FOX_HEAD_DIM + 1, :],
                           acc1 / acc1[0:1, :])
            o_ref[qi * tq:(qi + 1) * tq, :] = ot.T.astype(o_ref.dtype)
        s_cur = s_next


def _fox(q_aug, k_aug, v_t, batch, seq, tq):
    n = q_aug.shape[0]
    pairs = FOX_HEADS // 2
    kern = functools.partial(_fox_kernel, tq=tq)
    return pl.pallas_call(
        kern,
        grid=(batch, pairs),
        in_specs=[
            pl.BlockSpec((seq, 2 * LANES), lambda i, h: (i, h)),
            pl.BlockSpec((seq, 2 * LANES), lambda i, h: (i, h)),
            pl.BlockSpec((LANES, seq), lambda i, h: (h, i)),
        ],
        out_specs=pl.BlockSpec((seq, LANES), lambda i, h: (i, h)),
        out_shape=jax.ShapeDtypeStruct((n, FOX_W), BF16),
        compiler_params=_params(2),
        name="fox_attention",
    )(q_aug, k_aug, v_t)


def _ret_kernel(lg_ref, q_ref, k_ref, v_ref, g_ref, o_ref, *, chunk):
    s = q_ref.shape[1]
    lane = lax.broadcasted_iota(jnp.int32, (chunk, LANES), 1)
    ri = lax.broadcasted_iota(jnp.int32, (chunk, chunk), 0)
    ci = lax.broadcasted_iota(jnp.int32, (chunk, chunk), 1)
    diff = (ri - ci).astype(F32)
    pos = lax.broadcasted_iota(jnp.int32, (chunk, 1), 0).astype(F32)

    def head(h):
        lg = lg_ref[h]
        mine = (lane // RET_KEY_DIM) == (h % 2)
        qk = slice((h // 2) * LANES, (h // 2 + 1) * LANES)
        vs = slice(h * RET_VAL_DIM, (h + 1) * RET_VAL_DIM)
        inner = jnp.where(diff >= 0, jnp.exp(jnp.maximum(diff, 0.0) * lg), 0.0)
        q_decay = jnp.exp((pos + 1.0) * lg)
        k_decay = jnp.exp((chunk - 1.0 - pos) * lg)
        chunk_decay = jnp.exp(jnp.full((1, 1), chunk, F32) * lg)
        state = jnp.zeros((LANES, RET_VAL_DIM), F32)
        for c in range(s // chunk):
            rows = slice(c * chunk, (c + 1) * chunk)
            q = jnp.where(mine, q_ref[0, rows, qk], jnp.zeros((), BF16))
            k = jnp.where(mine, k_ref[0, rows, qk], jnp.zeros((), BF16))
            v = v_ref[0, rows, vs]
            scores = _dot_nt(q, k)
            inter = _dot(q, state.astype(BF16))
            kd = (k.astype(F32) * k_decay).astype(BF16)
            update = lax.dot_general(kd, v, (((0,), (0,)), ((), ())), preferred_element_type=F32)
            yield
            o = _dot((scores * inner).astype(BF16), v) + inter * q_decay
            state = state * chunk_decay + update
            yield
            o = o * lax.rsqrt(jnp.mean(o * o, axis=-1, keepdims=True) + EPS)
            g = g_ref[0, rows, vs]
            o_ref[0, rows, vs] = (o * (g * _sigmoid(g))).astype(o_ref.dtype)

    _interleave(head(h) for h in range(RET_HEADS))


def _retention(lg, rq, rk, rv, rg, chunk):
    b, s, _ = rq.shape
    kern = functools.partial(_ret_kernel, chunk=chunk)
    qk_spec = pl.BlockSpec((1, s, RET_QK_W), lambda i, lg_ref: (i, 0, 0))
    v_spec = pl.BlockSpec((1, s, RET_V_W), lambda i, lg_ref: (i, 0, 0))
    return pl.pallas_call(
        kern,
        grid_spec=pltpu.PrefetchScalarGridSpec(
            num_scalar_prefetch=1,
            grid=(b,),
            in_specs=[qk_spec, qk_spec, v_spec, v_spec],
            out_specs=v_spec,
        ),
        out_shape=jax.ShapeDtypeStruct((b, s, RET_V_W), BF16),
        compiler_params=_params(1),
        name="retention",
    )(lg, rq, rk, rv, rg)


MERGE_SUB = 256


def _merge_kernel(x_ref, fox_ref, ro_ref, gmix_ref, wmg_ref, bmg_ref, wb_ref, wout_ref, gffn_ref,
                  wr_ref, br_ref, h1_ref, hn_ref, rf_ref, ri_ref, cnt_ref, base_scr):
    i = pl.program_id(0)
    tm, d = x_ref.shape
    sub = min(MERGE_SUB, tm)

    @pl.when(i == 0)
    def _():
        base_scr[...] = jnp.zeros(base_scr.shape, F32)

    lane = lax.broadcasted_iota(jnp.int32, (sub, LANES), 1)
    lane_f = lane.astype(F32)
    r = lax.broadcasted_iota(jnp.int32, (sub, sub), 0)
    c = lax.broadcasted_iota(jnp.int32, (sub, sub), 1)
    tri = jnp.where(c < r, 1.0, 0.0).astype(BF16)
    counts = [base_scr[...]]

    def sub_tile(t):
        rows = slice(t * sub, (t + 1) * sub)
        x = x_ref[rows, :]
        xn = _rms(x, gmix_ref[...]).astype(BF16)
        yield
        pre = _dot(xn, wmg_ref[...])
        yield
        gate = _sigmoid(pre + bmg_ref[...])
        yield
        pf = _dot(fox_ref[rows, :], wb_ref[0])
        pr = _dot(ro_ref[rows, :], wb_ref[1])
        yield
        merged = (gate[:, :d] * pf + gate[:, d:] * pr).astype(BF16)
        yield
        h1 = x + _dot(merged, wout_ref[...])
        yield
        h1_ref[rows, :] = h1
        hn = _rms(h1, gffn_ref[...]).astype(BF16)
        _store_chunks(hn_ref, _pack_pairs(hn), rows)
        yield
        logits = _dot(hn, wr_ref[...]) + br_ref[...]
        yield
        vals, idxs = [], []
        cur = logits
        for _ in range(TOP_K):
            m = jnp.max(cur, axis=-1, keepdims=True)
            idx = jnp.argmax(cur, axis=-1, keepdims=True).astype(F32)
            vals.append(m)
            idxs.append(idx)
            cur = jnp.where(lane_f == idx, -jnp.inf, cur)
        exps = [jnp.exp(v - vals[0]) for v in vals]
        denom = exps[0] + exps[1] + exps[2] + exps[3]
        onehot = jnp.zeros(logits.shape, F32)
        for idx in idxs:
            onehot = onehot + jnp.where(lane_f == idx, 1.0, 0.0)
        yield
        assert len(counts) == t + 1
        before = _dot(tri, onehot.astype(BF16)) + counts[t]
        counts.append(counts[t] + jnp.sum(onehot, axis=0, keepdims=True))
        rf = jnp.zeros(logits.shape, F32)
        ri = jnp.zeros(logits.shape, F32)
        for j in range(TOP_K):
            rank = jnp.sum(jnp.where(lane_f == idxs[j], before, 0.0), axis=-1, keepdims=True)
            rf = jnp.where(lane == j, exps[j] / denom, rf)
            ri = jnp.where(lane == j, idxs[j], ri)
            ri = jnp.where(lane == TOP_K + j, rank, ri)
        rf_ref[rows, :] = rf[:, :2 * TOP_K]
        ri_ref[rows, :] = ri[:, :2 * TOP_K].astype(jnp.int32)

    _interleave(sub_tile(t) for t in range(tm // sub))
    base_scr[...] = counts[-1]
    cnt_ref[...] = counts[-1]


def _merge(x2, fox, ro, gmix, wmg, bmg, wb, wout, gffn, wr, br, tm):
    n, d = x2.shape
    row = lambda i: (i, 0)
    const = lambda i: (0, 0)
    return pl.pallas_call(
        _merge_kernel,
        grid=(n // tm,),
        in_specs=[
            pl.BlockSpec((tm, d), row), pl.BlockSpec((tm, FOX_W), row), pl.BlockSpec((tm, RET_V_W), row),
            pl.BlockSpec((1, d), const), _resident((d, 2 * d)), pl.BlockSpec((1, 2 * d), const),
            _resident((2, FOX_W, d)), _resident((d, d)),
            pl.BlockSpec((1, d), const), _resident((d, LANES)), pl.BlockSpec((1, LANES), const),
        ],
        out_specs=[
            pl.BlockSpec((tm, d), row), pl.BlockSpec((d // 2 // SC_COLS, tm, SC_COLS), lambda i: (0, i, 0)),
            pl.BlockSpec((tm, 2 * TOP_K), row), pl.BlockSpec((tm, 2 * TOP_K), row),
            pl.BlockSpec((1, LANES), const),
        ],
        out_shape=[
            jax.ShapeDtypeStruct((n, d), F32), jax.ShapeDtypeStruct((d // 2 // SC_COLS, n, SC_COLS), jnp.int32),
            jax.ShapeDtypeStruct((n, 2 * TOP_K), F32), jax.ShapeDtypeStruct((n, 2 * TOP_K), jnp.int32),
            jax.ShapeDtypeStruct((1, LANES), F32),
        ],
        scratch_shapes=[pltpu.VMEM((1, LANES), F32)],
        compiler_params=_params(1),
        name="merge_router",
    )(x2, fox, ro, gmix, wmg, bmg, wb, wout, gffn, wr, br)


def _sc_mesh():
    return plsc.VectorSubcoreMesh(core_axis_name="core", subcore_axis_name="subcore")


def _sc_dispatch(rows, dest_t, n_out):
    chunks, n, w = rows.shape

    @functools.partial(pl.kernel, out_type=jax.ShapeDtypeStruct((chunks, n_out, w), rows.dtype),
                       mesh=_sc_mesh(), scratch_types=[], name="moe_dispatch")
    def run(x_hbm, i_hbm, o_hbm):
        for c in range(chunks):
            def body(x_vmem, i_vmem, c=c):
                for j in range(TOP_K):
                    pltpu.sync_copy(x_vmem, o_hbm.at[c].at[i_vmem.at[j]])

            pltpu.emit_pipeline(
                body,
                grid=(n // SC_WINDOW,),
                in_specs=[pl.BlockSpec((SC_WINDOW, w), lambda i: (i, 0)),
                          pl.BlockSpec((TOP_K, SC_WINDOW), lambda i: (0, i))],
                out_specs=[],
                core_axis_name=("core", "subcore"),
                dimension_semantics=(pltpu.PARALLEL,),
            )(x_hbm.at[c], i_hbm)

    return run(rows, dest_t)


def _sc_gather(table, idx):
    chunks, _, w = table.shape
    m = idx.shape[1]

    @functools.partial(pl.kernel, out_type=jax.ShapeDtypeStruct((chunks, m, w), table.dtype),
                       mesh=_sc_mesh(), scratch_types=[], name="moe_combine_gather")
    def run(t_hbm, i_hbm, o_hbm):
        for c in range(chunks):
            def body(i_vmem, o_vmem, c=c):
                pltpu.sync_copy(t_hbm.at[c].at[i_vmem.at[0]], o_vmem)

            pltpu.emit_pipeline(
                body,
                grid=(m // SC_WINDOW,),
                in_specs=[pl.BlockSpec((1, SC_WINDOW), lambda i: (0, i))],
                out_specs=[pl.BlockSpec((SC_WINDOW, w), lambda i: (i, 0))],
                core_axis_name=("core", "subcore"),
                dimension_semantics=(pltpu.PARALLEL,),
            )(i_hbm, o_hbm.at[c])

    return run(table, idx)


GLU_GROUP = 2 * LANES
EXPERT_SUB = 256


def _expert_kernel(be_ref, valid_ref, x_ref, wgu_ref, bgu_ref, wd_ref, bd_ref, perm_ref, y_ref, wgu_scr, wd_scr):
    i = pl.program_id(0)
    f2 = wgu_ref.shape[2]
    bm = x_ref.shape[1]
    sub = min(EXPERT_SUB, bm)
    valid = valid_ref[i]
    live = valid > 0

    @pl.when(jnp.logical_and(live, jnp.logical_or(i == 0, be_ref[i] != be_ref[jnp.maximum(i - 1, 0)])))
    def _():
        for b in range(f2 // GLU_GROUP):
            cols = slice(b * GLU_GROUP, (b + 1) * GLU_GROUP)
            wgu_scr[:, cols] = _dot(wgu_ref[0, :, cols].astype(BF16), perm_ref[...]).astype(BF16)
        wd_scr[...] = wd_ref[0].astype(BF16)

    def sub_block(t, sub):
        rows = slice(t * sub, (t + 1) * sub)
        x = _unpack_pairs([x_ref[c, rows, :] for c in range(x_ref.shape[0])]).astype(BF16)
        yield
        gu = _dot(x, wgu_scr[...])
        yield
        gu = gu + bgu_ref[0]
        acts = []
        for b in range(f2 // GLU_GROUP):
            glu = jnp.minimum(gu[:, b * GLU_GROUP:b * GLU_GROUP + LANES], SWIGLU_LIMIT)
            lin = jnp.clip(gu[:, b * GLU_GROUP + LANES:(b + 1) * GLU_GROUP], -SWIGLU_LIMIT, SWIGLU_LIMIT)
            acts.append((glu * _sigmoid(SWIGLU_ALPHA * glu) * (lin + 1.0)).astype(BF16))
        act = jnp.concatenate(acts, axis=1)
        yield
        y = _dot(act, wd_scr[...])
        yield
        _store_chunks(y_ref, _pack_pairs(y + bd_ref[0]), rows)

    @pl.when(valid == bm)
    def _():
        _interleave(sub_block(t, sub) for t in range(bm // sub))

    for t in range(bm // sub):
        @pl.when(jnp.logical_and(valid < bm, valid > bm - (t + 1) * sub))
        def _(t=t):
            _interleave([sub_block(t, sub)])


def _experts(block_e, valid, x_buf, wgu, bgu, wd, bd, bm):
    p = x_buf.shape[1]
    f, d = wd.shape[1:]
    perm = np.zeros((GLU_GROUP, GLU_GROUP), np.float32)
    for c in range(LANES):
        perm[2 * c, c] = 1.0
        perm[2 * c + 1, LANES + c] = 1.0
    return pl.pallas_call(
        _expert_kernel,
        grid_spec=pltpu.PrefetchScalarGridSpec(
            num_scalar_prefetch=2,
            grid=(p // bm,),
            in_specs=[
                pl.BlockSpec((d // 2 // SC_COLS, bm, SC_COLS), lambda i, be, nv: (0, i, 0)),
                pl.BlockSpec((1, d, 2 * f), lambda i, be, nv: (be[i], 0, 0)),
                pl.BlockSpec((1, 1, 2 * f), lambda i, be, nv: (be[i], 0, 0)),
                pl.BlockSpec((1, f, d), lambda i, be, nv: (be[i], 0, 0)),
                pl.BlockSpec((1, 1, d), lambda i, be, nv: (be[i], 0, 0)),
                pl.BlockSpec((GLU_GROUP, GLU_GROUP), lambda i, be, nv: (0, 0)),
            ],
            out_specs=pl.BlockSpec((d // 2 // SC_COLS, bm, SC_COLS), lambda i, be, nv: (0, i, 0)),
            scratch_shapes=[pltpu.VMEM((d, 2 * f), BF16), pltpu.VMEM((f, d), BF16)],
        ),
        out_shape=jax.ShapeDtypeStruct((d // 2 // SC_COLS, p, SC_COLS), jnp.int32),
        compiler_params=_params(1),
        name="expert_ffn",
    )(block_e, valid, x_buf, wgu, bgu, wd, bd, jnp.asarray(perm, BF16))


FINAL_SUB = 256


def _final_kernel(h1_ref, yg_ref, rf_ref, p_ref, gple_ref, wpg_ref, wpp_ref, gfin_ref, *rest):
    o_ref = rest[-1]
    tm = h1_ref.shape[0]
    sub = min(FINAL_SUB, tm)

    def sub_tile(t):
        rows = slice(t * sub, (t + 1) * sub)
        rf = rf_ref[rows, :]
        h2 = h1_ref[rows, :]
        for j in range(TOP_K):
            h2 = h2 + _unpack_pairs([yg_ref[c, j, rows, :] for c in range(yg_ref.shape[0])]) * rf[:, j:j + 1]
        hn = _rms(h2, gple_ref[...]).astype(BF16)
        yield
        pre = _dot(hn, wpg_ref[...])
        proj = _dot(p_ref[rows, :].astype(BF16), wpp_ref[...])
        yield
        h3 = h2 + _sigmoid(pre) * proj
        o_ref[rows, :] = _rms(h3, gfin_ref[...])

    _interleave(sub_tile(t) for t in range(tm // sub))


def _final(h1, yg, rf, p2, gple, wpg, wpp, gfin, tm, first_tile, prev_out):
    n_total, d = h1.shape
    n = yg.shape[2]
    pd = p2.shape[1]
    dst = lambda i: (i + first_tile, 0)
    const = lambda i: (0, 0)
    in_specs = [
        pl.BlockSpec((tm, d), dst), pl.BlockSpec((d // 2 // SC_COLS, TOP_K, tm, SC_COLS), lambda i: (0, 0, i, 0)),
        pl.BlockSpec((tm, 2 * TOP_K), dst), pl.BlockSpec((tm, pd), dst),
        pl.BlockSpec((1, d), const), pl.BlockSpec((d, d), const), pl.BlockSpec((pd, d), const),
        pl.BlockSpec((1, d), const),
    ]
    args = [h1, yg, rf, p2, gple, wpg, wpp, gfin]
    aliases = {}
    if prev_out is not None:
        in_specs.append(pl.BlockSpec(memory_space=pl.ANY))
        aliases = {len(args): 0}
        args.append(prev_out)
    return pl.pallas_call(
        _final_kernel,
        grid=(n // tm,),
        in_specs=in_specs,
        out_specs=pl.BlockSpec((tm, d), dst),
        out_shape=jax.ShapeDtypeStruct((n_total, d), F32),
        input_output_aliases=aliases,
        compiler_params=_params(1),
        name="combine_ple_norm",
    )(*args)


def _layer(h, p, mix_norm, w_in, b_forget, w_branch, w_merge_gate, b_merge_gate, w_out, ffn_norm,
           w_router, b_router, w_gate_up, b_gate_up, w_down, b_down, ple_norm, w_ple_gate, w_ple_proj,
           final_norm, *, tm, tq, chunk, bm, groups):
    b, s, d = h.shape
    n = b * s
    x2 = h.reshape(n, d)
    row = lambda t: t.reshape(1, -1)

    c0 = 3 * FOX_W
    w_main = jnp.concatenate([w_in[:, :2 * FOX_W], w_in[:, c0 + FOX_HEADS:]], axis=1).astype(BF16)
    w_vt = w_in[:, 2 * FOX_W:c0].T.astype(BF16)
    w_f = jnp.pad(w_in[:, c0:c0 + FOX_HEADS], ((0, 0), (0, LANES - FOX_HEADS))).astype(BF16)
    b_f = jnp.pad(b_forget, (0, LANES - FOX_HEADS)).reshape(1, LANES)
    half = RET_KEY_DIM // 2
    inv = ROPE_BASE ** (-jnp.arange(half, dtype=F32) / half)
    ang = jnp.arange(s).astype(F32)[:, None] * inv[None, :]
    cos_t = jnp.tile(jnp.cos(ang), (1, RET_QK_W // half))
    sin_t = jnp.tile(jnp.sin(ang), (1, RET_QK_W // half))

    q_aug, k_aug, v_t, rq, rk, rv, rg = _inproj(x2, row(mix_norm), w_main, w_vt, w_f, b_f, cos_t, sin_t, s, tm)
    fox = _fox(q_aug, k_aug, v_t, b, s, tq)

    lg = jnp.log1p(-jnp.exp2(-5.0 - jnp.arange(RET_HEADS, dtype=F32)))
    ro = _retention(lg, rq.reshape(b, s, RET_QK_W), rk.reshape(b, s, RET_QK_W),
                    rv.reshape(b, s, RET_V_W), rg.reshape(b, s, RET_V_W), chunk)

    w_r = jnp.pad(w_router, ((0, 0), (0, LANES - N_EXPERTS))).astype(BF16)
    b_r = jnp.concatenate([b_router, jnp.full((LANES - N_EXPERTS,), -jnp.inf, F32)]).reshape(1, LANES)
    ne, dd, f2 = w_gate_up.shape
    bgu = b_gate_up.reshape(ne, f2 // GLU_GROUP, LANES, 2).transpose(0, 1, 3, 2).reshape(ne, 1, f2)
    merge_w = (row(mix_norm), w_merge_gate.astype(BF16), row(b_merge_gate), w_branch.astype(BF16),
               w_out.astype(BF16), row(ffn_norm), w_r, b_r)
    final_w = (row(ple_norm), w_ple_gate.astype(BF16), w_ple_proj.astype(BF16), row(final_norm))
    p2 = p.reshape(n, -1)

    h1, hn, rf, ri, cnt = _merge(x2, fox, ro.reshape(n, RET_V_W), *merge_w, tm)

    a = n * TOP_K
    nb = -(-(a + N_EXPERTS * (bm - 1)) // bm)
    ri_t = ri.T
    counts = cnt[0, :N_EXPERTS].astype(jnp.int32)
    padded = (counts + bm - 1) // bm * bm
    pad_end = jnp.cumsum(padded)
    first_row = pad_end - counts
    experts = jnp.arange(N_EXPERTS, dtype=jnp.int32)

    def lookup(ids):
        return jnp.sum(jnp.where(ids[None] == experts.reshape((-1,) + (1,) * ids.ndim),
                                 first_row.reshape((-1,) + (1,) * ids.ndim), 0), axis=0)

    dest_t = lookup(ri_t[:TOP_K]) + ri_t[TOP_K:]
    block_start = jnp.arange(nb, dtype=jnp.int32) * bm
    block_e = jnp.minimum(jnp.sum(block_start[:, None] >= pad_end[None, :], axis=1),
                          N_EXPERTS - 1).astype(jnp.int32)
    valid = jnp.where(block_start < pad_end[-1],
                      jnp.clip(block_start + bm - lookup(block_e), 0, bm), 0).astype(jnp.int32)

    x_buf = _sc_dispatch(hn, dest_t, nb * bm)
    y_buf = _experts(block_e, valid, x_buf, w_gate_up, bgu, w_down, b_down.reshape(ne, 1, dd), bm)

    shares = [1] + [2 ** max(g - 1, 0) for g in range(1, groups)]
    unit = n // sum(shares)
    out = None
    start = 0
    for share in shares:
        ng = share * unit
        idx = dest_t[:, start:start + ng].reshape(1, ng * TOP_K)
        yg = _sc_gather(y_buf, idx).reshape(d // 2 // SC_COLS, TOP_K, ng, SC_COLS)
        out = _final(h1, yg, rf, p2, *final_w, tm, start // tm, out)
        start += ng
    return out.reshape(b, s, d)


def kernel(x, p, mix_norm, w_in, b_forget, w_branch, w_merge_gate, b_merge_gate, w_out, ffn_norm, w_router,
           b_router, w_gate_up, b_gate_up, w_down, b_down, ple_norm, w_ple_gate, w_ple_proj, final_norm):
    depth = p.shape[0]
    assert depth == 1, "the final norm is fused into the (single) layer"
    return _layer(x, p[0], mix_norm[0], w_in[0], b_forget[0], w_branch[0], w_merge_gate[0], b_merge_gate[0],
                  w_out[0], ffn_norm[0], w_router[0], b_router[0], w_gate_up[0], b_gate_up[0], w_down[0],
                  b_down[0], ple_norm[0], w_ple_gate[0], w_ple_proj[0], final_norm,
                  tm=1024, tq=512, chunk=128, bm=1024, groups=5)
```

```python
import functools

import numpy as np
import jax
import jax.numpy as jnp
from jax import lax
from jax.experimental import pallas as pl
from jax.experimental.pallas import tpu as pltpu
from jax.experimental.pallas import tpu_sc as plsc

FOX_HEADS = 8
FOX_HEAD_DIM = 64
RET_HEADS = 4
RET_KEY_DIM = 64
RET_VAL_DIM = 128
ROPE_BASE = 10000.0
N_EXPERTS = 32
TOP_K = 4
SWIGLU_LIMIT = 7.0
SWIGLU_ALPHA = 1.702
EPS = 1e-6

LANES = 128
FOX_W = FOX_HEADS * FOX_HEAD_DIM
RET_QK_W = RET_HEADS * RET_KEY_DIM
RET_V_W = RET_HEADS * RET_VAL_DIM
VMEM_LIMIT = 56 * 1024 * 1024
HIGH_HALF = -65536
LOG2_E = 1.4426950408889634
SC_WINDOW = 128
SC_COLS = 256

F32 = jnp.float32
BF16 = jnp.bfloat16


def _rms(x, g):
    return x * lax.rsqrt(jnp.mean(x * x, axis=-1, keepdims=True) + EPS) * g


def _sigmoid(x):
    return 1.0 / (1.0 + jnp.exp(-x))


def _dot(a, b):
    return jnp.dot(a, b, preferred_element_type=F32)


def _dot_nt(a, b):
    return lax.dot_general(a, b, (((1,), (1,)), ((), ())), preferred_element_type=F32)


def _pack_pairs(v):
    bits = pltpu.bitcast(v.astype(BF16).astype(F32), jnp.int32)
    half = v.shape[1] // 2
    return lax.shift_right_logical(bits[:, :half], 16) | (bits[:, half:] & HIGH_HALF)


def _unpack_pairs(chunks):
    return jnp.concatenate([pltpu.bitcast(lax.shift_left(w, 16), F32) for w in chunks]
                           + [pltpu.bitcast(w & HIGH_HALF, F32) for w in chunks], axis=1)


def _store_chunks(ref, v, rows=slice(None)):
    for c in range(ref.shape[0]):
        ref[c, rows, :] = v[:, c * SC_COLS:(c + 1) * SC_COLS]


def _interleave(generators):
    live = []
    pending = list(generators)
    while pending or live:
        if pending:
            live.append(pending.pop(0))
        for g in list(live):
            try:
                next(g)
            except StopIteration:
                live.remove(g)


def _resident(shape):
    return pl.BlockSpec(shape, lambda *_: (0,) * len(shape), pipeline_mode=pl.Buffered(1))


def _params(n_axes):
    return pltpu.CompilerParams(dimension_semantics=("arbitrary",) * n_axes,
                                vmem_limit_bytes=VMEM_LIMIT)


C_MID = FOX_HEADS
C_LO = 2 * FOX_HEADS
C_ONE = 3 * FOX_HEADS
N_AUG = 3
AUG_W = 2 * N_AUG
K_AUG0 = FOX_HEADS * AUG_W


def _split3(t):
    hi = t.astype(BF16).astype(F32)
    r = t - hi
    mid = r.astype(BF16).astype(F32)
    lo = (r - mid).astype(BF16).astype(F32)
    return hi + pltpu.roll(mid, C_MID, 1) + pltpu.roll(lo, C_LO, 1)


INPROJ_SUB = 256


def _inproj_kernel(x_ref, g_ref, w_ref, wvt_ref, wf_ref, bf_ref, cos_ref, sin_ref, place_ref,
                   q_ref, k_ref, vt_ref, rq_ref, rk_ref, rv_ref, rg_ref, carry_scr, *, tiles_per_seq):
    i = pl.program_id(0)
    tm = x_ref.shape[0]
    sub = min(INPROJ_SUB, tm)

    @pl.when(i % tiles_per_seq == 0)
    def _():
        carry_scr[...] = jnp.zeros(carry_scr.shape, F32)

    lane = lax.broadcasted_iota(jnp.int32, (sub, LANES), 1)
    r = lax.broadcasted_iota(jnp.int32, (sub, sub), 0)
    c = lax.broadcasted_iota(jnp.int32, (sub, sub), 1)
    tri = jnp.where(c <= r, 1.0, 0.0).astype(BF16)
    own = lane < FOX_HEAD_DIM
    is_aug = jnp.logical_and(lane >= FOX_HEAD_DIM, lane < FOX_HEAD_DIM + AUG_W)
    half = RET_KEY_DIM // 2
    lane_r = lax.broadcasted_iota(jnp.int32, (sub, RET_QK_W), 1)
    first = (lane_r % RET_KEY_DIM) < half
    carries = [carry_scr[...]]

    def sub_tile(t):
        rows = slice(t * sub, (t + 1) * sub)
        xn = _rms(x_ref[rows, :], g_ref[...]).astype(BF16)
        yield
        u = _dot(xn, w_ref[...])
        vt_ref[:, rows] = _dot_nt(wvt_ref[...], xn).astype(BF16)
        z = _dot(xn, wf_ref[...]) + bf_ref[...]
        yield
        lf = jnp.where(lane < FOX_HEADS, jnp.minimum(z, 0.0) - jnp.log1p(jnp.exp(-jnp.abs(z))), 0.0)
        ps = _dot(tri, _split3(lf).astype(BF16))
        yield
        assert len(carries) == t + 1
        cum = ps + pltpu.roll(ps, LANES - C_MID, 1) + pltpu.roll(ps, LANES - C_LO, 1)
        cum = jnp.where(lane < FOX_HEADS, cum, 0.0) + carries[t]
        carries.append(cum[sub - 1:sub, :])
        c3 = (_split3(cum * LOG2_E) + jnp.where(lane == C_ONE, 1.0, 0.0)).astype(BF16)
        aug = _dot(c3, place_ref[...])

        o = 2 * FOX_W
        rq = u[:, o:o + RET_QK_W]; o += RET_QK_W
        rk = u[:, o:o + RET_QK_W]; o += RET_QK_W
        rv_ref[rows, :] = u[:, o:o + RET_V_W].astype(BF16); o += RET_V_W
        rg_ref[rows, :] = u[:, o:o + RET_V_W]

        cos = cos_ref[rows, :]
        sin = sin_ref[rows, :]

        def rot(v):
            partner = jnp.where(first, -pltpu.roll(v, RET_QK_W - half, 1), pltpu.roll(v, half, 1))
            return v * cos + partner * sin

        rq_ref[rows, :] = rot(rq).astype(BF16)
        rk_ref[rows, :] = (rot(rk) * (RET_KEY_DIM ** -0.5)).astype(BF16)
        yield
        fq = u[:, 0:FOX_W] * (FOX_HEAD_DIM ** -0.5 * LOG2_E)
        fk = u[:, FOX_W:2 * FOX_W]
        for src, aug0, dst in ((fq, 0, q_ref), (fk, K_AUG0, k_ref)):
            for h in range(FOX_HEADS):
                blk = src[:, (h // 2) * LANES:(h // 2 + 1) * LANES]
                if h % 2:
                    blk = pltpu.roll(blk, FOX_HEAD_DIM, 1)
                mine = pltpu.roll(aug, (FOX_HEAD_DIM - aug0 - AUG_W * h) % LANES, 1)
                slab = jnp.where(own, blk, jnp.where(is_aug, mine, 0.0))
                dst[rows, h * LANES:(h + 1) * LANES] = slab.astype(BF16)

    _interleave(sub_tile(t) for t in range(tm // sub))
    carry_scr[...] = carries[-1]


def _placement():
    place = np.zeros((LANES, LANES), np.float32)
    for h in range(FOX_HEADS):
        for part, src in enumerate((h, C_MID + h, C_LO + h)):
            place[C_ONE, AUG_W * h + part] = 1.0
            place[src, AUG_W * h + N_AUG + part] = 1.0
            place[src, K_AUG0 + AUG_W * h + part] = -1.0
            place[C_ONE, K_AUG0 + AUG_W * h + N_AUG + part] = 1.0
    return jnp.asarray(place, BF16)


def _inproj(x2, g, w_main, w_vt, w_f, b_f, cos_t, sin_t, seq, tm):
    n, d = x2.shape
    wn = w_main.shape[1]
    spt = seq // tm
    aw = FOX_HEADS * LANES
    place = _placement()
    row = lambda i: (i, 0)
    const = lambda i: (0, 0)
    pos = lambda i: (i % spt, 0)
    out_shape = [
        jax.ShapeDtypeStruct((n, aw), BF16), jax.ShapeDtypeStruct((n, aw), BF16),
        jax.ShapeDtypeStruct((FOX_W, n), BF16),
        jax.ShapeDtypeStruct((n, RET_QK_W), BF16), jax.ShapeDtypeStruct((n, RET_QK_W), BF16),
        jax.ShapeDtypeStruct((n, RET_V_W), BF16), jax.ShapeDtypeStruct((n, RET_V_W), F32),
    ]
    out_specs = [
        pl.BlockSpec((tm, aw), row), pl.BlockSpec((tm, aw), row),
        pl.BlockSpec((FOX_W, tm), lambda i: (0, i)),
        pl.BlockSpec((tm, RET_QK_W), row), pl.BlockSpec((tm, RET_QK_W), row),
        pl.BlockSpec((tm, RET_V_W), row), pl.BlockSpec((tm, RET_V_W), row),
    ]
    return pl.pallas_call(
        functools.partial(_inproj_kernel, tiles_per_seq=spt),
        grid=(n // tm,),
        in_specs=[
            pl.BlockSpec((tm, d), row), pl.BlockSpec((1, d), const),
            _resident((d, wn)), _resident((FOX_W, d)),
            _resident((d, LANES)), pl.BlockSpec((1, LANES), const),
            pl.BlockSpec((tm, RET_QK_W), pos), pl.BlockSpec((tm, RET_QK_W), pos),
            _resident((LANES, LANES)),
        ],
        out_specs=out_specs,
        out_shape=out_shape,
        scratch_shapes=[pltpu.VMEM((1, LANES), F32)],
        compiler_params=_params(1),
        name="inproj",
    )(x2, g, w_main, w_vt, w_f, b_f, cos_t, sin_t, place)


FOX_AHEAD = 3


def _fox_kernel(q_ref, k_ref, vt_ref, o_ref, *, tq):
    seq = q_ref.shape[0]
    half = tq // 2
    vrow = lax.broadcasted_iota(jnp.int32, (LANES, tq), 0)
    one = jnp.ones((), BF16)
    items = [(qi, j) for qi in range(seq // tq) for j in range(qi + 1)]

    def logits(qi, j):
        out = []
        for a in range(2):
            sl = slice(a * LANES, (a + 1) * LANES)
            kk = k_ref[j * tq:(j + 1) * tq, sl]
            qq = q_ref[qi * tq:(qi + 1) * tq, sl]
            if j != qi:
                out.append([(0, tq, _dot_nt(kk, qq))])
                continue
            parts = []
            for c0, nk in ((0, half), (half, tq)):
                sc = _dot_nt(kk[:nk], qq[c0:c0 + half])
                key = lax.broadcasted_iota(jnp.int32, sc.shape, 0)
                qry = lax.broadcasted_iota(jnp.int32, sc.shape, 1) + c0
                parts.append((c0, nk, jnp.where(key <= qry, sc, -jnp.inf)))
            out.append(parts)
        return out

    queue = [logits(*it) for it in items[:FOX_AHEAD]]
    carry = None
    for w, (qi, j) in enumerate(items):
        if w + FOX_AHEAD < len(items):
            queue.append(logits(*items[w + FOX_AHEAD]))
        s_cur = queue.pop(0)
        if j == 0:
            carry = [(jnp.full((1, tq), -jnp.inf, F32), jnp.zeros((LANES, tq), F32)) for _ in range(2)]
        v = vt_ref[:, j * tq:(j + 1) * tq]
        for a in range(2):
            m_prev, acc = carry[a]
            va = jnp.where((vrow // FOX_HEAD_DIM) == a, v, one)
            ms, accs = [], []
            for c0, nk, sc in s_cur[a]:
                cols = slice(c0, c0 + sc.shape[1])
                m_new = jnp.maximum(m_prev[:, cols], jnp.max(sc, axis=0, keepdims=True))
                alpha = jnp.exp2(m_prev[:, cols] - m_new)
                p = jnp.exp2(sc - m_new).astype(BF16)
                ms.append(m_new)
                accs.append(alpha * acc[:, cols] + _dot(va[:, :nk], p))
            carry[a] = (ms[0], accs[0]) if len(ms) == 1 else (jnp.concatenate(ms, axis=1),
                                                               jnp.concatenate(accs, axis=1))
        if j == qi:
            acc0, acc1 = carry[0][1], carry[1][1]
            ot = jnp.where(vrow < FOX_HEAD_DIM, acc0 / acc0[FOX_HEAD_DIM:FOX_HEAD_DIM + 1, :],
                           acc1 / acc1[0:1, :])
            o_ref[qi * tq:(qi + 1) * tq, :] = ot.T.astype(o_ref.dtype)


def _fox(q_aug, k_aug, v_t, batch, seq, tq):
    n = q_aug.shape[0]
    pairs = FOX_HEADS // 2
    kern = functools.partial(_fox_kernel, tq=tq)
    return pl.pallas_call(
        kern,
        grid=(batch, pairs),
        in_specs=[
            pl.BlockSpec((seq, 2 * LANES), lambda i, h: (i, h)),
            pl.BlockSpec((seq, 2 * LANES), lambda i, h: (i, h)),
            pl.BlockSpec((LANES, seq), lambda i, h: (h, i)),
        ],
        out_specs=pl.BlockSpec((seq, LANES), lambda i, h: (i, h)),
        out_shape=jax.ShapeDtypeStruct((n, FOX_W), BF16),
        compiler_params=_params(2),
        name="fox_attention",
    )(q_aug, k_aug, v_t)


def _ret_kernel(lg_ref, q_ref, k_ref, v_ref, g_ref, o_ref, *, chunk):
    s = q_ref.shape[1]
    lane = lax.broadcasted_iota(jnp.int32, (chunk, LANES), 1)
    ri = lax.broadcasted_iota(jnp.int32, (chunk, chunk), 0)
    ci = lax.broadcasted_iota(jnp.int32, (chunk, chunk), 1)
    diff = (ri - ci).astype(F32)
    pos = lax.broadcasted_iota(jnp.int32, (chunk, 1), 0).astype(F32)

    def head(h):
        lg = lg_ref[h]
        mine = (lane // RET_KEY_DIM) == (h % 2)
        qk = slice((h // 2) * LANES, (h // 2 + 1) * LANES)
        vs = slice(h * RET_VAL_DIM, (h + 1) * RET_VAL_DIM)
        inner = jnp.where(diff >= 0, jnp.exp(jnp.maximum(diff, 0.0) * lg), 0.0)
        q_decay = jnp.exp((pos + 1.0) * lg)
        k_decay = jnp.exp((chunk - 1.0 - pos) * lg)
        chunk_decay = jnp.exp(jnp.full((1, 1), chunk, F32) * lg)
        state = jnp.zeros((LANES, RET_VAL_DIM), F32)
        for c in range(s // chunk):
            rows = slice(c * chunk, (c + 1) * chunk)
            q = jnp.where(mine, q_ref[0, rows, qk], jnp.zeros((), BF16))
            k = jnp.where(mine, k_ref[0, rows, qk], jnp.zeros((), BF16))
            v = v_ref[0, rows, vs]
            scores = _dot_nt(q, k)
            inter = _dot(q, state.astype(BF16))
            kd = (k.astype(F32) * k_decay).astype(BF16)
            update = lax.dot_general(kd, v, (((0,), (0,)), ((), ())), preferred_element_type=F32)
            yield
            o = _dot((scores * inner).astype(BF16), v) + inter * q_decay
            state = state * chunk_decay + update
            yield
            o = o * lax.rsqrt(jnp.mean(o * o, axis=-1, keepdims=True) + EPS)
            g = g_ref[0, rows, vs]
            o_ref[0, rows, vs] = (o * (g * _sigmoid(g))).astype(o_ref.dtype)

    _interleave(head(h) for h in range(RET_HEADS))


def _retention(lg, rq, rk, rv, rg, chunk):
    b, s, _ = rq.shape
    kern = functools.partial(_ret_kernel, chunk=chunk)
    qk_spec = pl.BlockSpec((1, s, RET_QK_W), lambda i, lg_ref: (i, 0, 0))
    v_spec = pl.BlockSpec((1, s, RET_V_W), lambda i, lg_ref: (i, 0, 0))
    return pl.pallas_call(
        kern,
        grid_spec=pltpu.PrefetchScalarGridSpec(
            num_scalar_prefetch=1,
            grid=(b,),
            in_specs=[qk_spec, qk_spec, v_spec, v_spec],
            out_specs=v_spec,
        ),
        out_shape=jax.ShapeDtypeStruct((b, s, RET_V_W), BF16),
        compiler_params=_params(1),
        name="retention",
    )(lg, rq, rk, rv, rg)


MERGE_SUB = 256


def _merge_kernel(x_ref, fox_ref, ro_ref, gmix_ref, wmg_ref, bmg_ref, wb_ref, wout_ref, gffn_ref,
                  wr_ref, br_ref, h1_ref, hn_ref, rf_ref, ri_ref, cnt_ref, base_scr):
    i = pl.program_id(0)
    tm, d = x_ref.shape
    sub = min(MERGE_SUB, tm)

    @pl.when(i == 0)
    def _():
        base_scr[...] = jnp.zeros(base_scr.shape, F32)

    lane = lax.broadcasted_iota(jnp.int32, (sub, LANES), 1)
    lane_f = lane.astype(F32)
    r = lax.broadcasted_iota(jnp.int32, (sub, sub), 0)
    c = lax.broadcasted_iota(jnp.int32, (sub, sub), 1)
    tri = jnp.where(c < r, 1.0, 0.0).astype(BF16)
    counts = [base_scr[...]]

    def sub_tile(t):
        rows = slice(t * sub, (t + 1) * sub)
        x = x_ref[rows, :]
        xn = _rms(x, gmix_ref[...]).astype(BF16)
        yield
        pre = _dot(xn, wmg_ref[...])
        yield
        gate = _sigmoid(pre + bmg_ref[...])
        yield
        pf = _dot(fox_ref[rows, :], wb_ref[0])
        pr = _dot(ro_ref[rows, :], wb_ref[1])
        yield
        merged = (gate[:, :d] * pf + gate[:, d:] * pr).astype(BF16)
        yield
        h1 = x + _dot(merged, wout_ref[...])
        yield
        h1_ref[rows, :] = h1
        hn = _rms(h1, gffn_ref[...]).astype(BF16)
        _store_chunks(hn_ref, _pack_pairs(hn), rows)
        yield
        logits = _dot(hn, wr_ref[...]) + br_ref[...]
        yield
        vals, idxs = [], []
        cur = logits
        for _ in range(TOP_K):
            m = jnp.max(cur, axis=-1, keepdims=True)
            idx = jnp.argmax(cur, axis=-1, keepdims=True).astype(F32)
            vals.append(m)
            idxs.append(idx)
            cur = jnp.where(lane_f == idx, -jnp.inf, cur)
        exps = [jnp.exp(v - vals[0]) for v in vals]
        denom = exps[0] + exps[1] + exps[2] + exps[3]
        onehot = jnp.zeros(logits.shape, F32)
        for idx in idxs:
            onehot = onehot + jnp.where(lane_f == idx, 1.0, 0.0)
        yield
        assert len(counts) == t + 1
        before = _dot(tri, onehot.astype(BF16)) + counts[t]
        counts.append(counts[t] + jnp.sum(onehot, axis=0, keepdims=True))
        rf = jnp.zeros(logits.shape, F32)
        ri = jnp.zeros(logits.shape, F32)
        for j in range(TOP_K):
            rank = jnp.sum(jnp.where(lane_f == idxs[j], before, 0.0), axis=-1, keepdims=True)
            rf = jnp.where(lane == j, exps[j] / denom, rf)
            ri = jnp.where(lane == j, idxs[j], ri)
            ri = jnp.where(lane == TOP_K + j, rank, ri)
        rf_ref[rows, :] = rf[:, :2 * TOP_K]
        ri_ref[rows, :] = ri[:, :2 * TOP_K].astype(jnp.int32)

    _interleave(sub_tile(t) for t in range(tm // sub))
    base_scr[...] = counts[-1]
    cnt_ref[...] = counts[-1]


def _merge(x2, fox, ro, gmix, wmg, bmg, wb, wout, gffn, wr, br, tm):
    n, d = x2.shape
    row = lambda i: (i, 0)
    const = lambda i: (0, 0)
    return pl.pallas_call(
        _merge_kernel,
        grid=(n // tm,),
        in_specs=[
            pl.BlockSpec((tm, d), row), pl.BlockSpec((tm, FOX_W), row), pl.BlockSpec((tm, RET_V_W), row),
            pl.BlockSpec((1, d), const), _resident((d, 2 * d)), pl.BlockSpec((1, 2 * d), const),
            _resident((2, FOX_W, d)), _resident((d, d)),
            pl.BlockSpec((1, d), const), _resident((d, LANES)), pl.BlockSpec((1, LANES), const),
        ],
        out_specs=[
            pl.BlockSpec((tm, d), row), pl.BlockSpec((d // 2 // SC_COLS, tm, SC_COLS), lambda i: (0, i, 0)),
            pl.BlockSpec((tm, 2 * TOP_K), row), pl.BlockSpec((tm, 2 * TOP_K), row),
            pl.BlockSpec((1, LANES), const),
        ],
        out_shape=[
            jax.ShapeDtypeStruct((n, d), F32), jax.ShapeDtypeStruct((d // 2 // SC_COLS, n, SC_COLS), jnp.int32),
            jax.ShapeDtypeStruct((n, 2 * TOP_K), F32), jax.ShapeDtypeStruct((n, 2 * TOP_K), jnp.int32),
            jax.ShapeDtypeStruct((1, LANES), F32),
        ],
        scratch_shapes=[pltpu.VMEM((1, LANES), F32)],
        compiler_params=_params(1),
        name="merge_router",
    )(x2, fox, ro, gmix, wmg, bmg, wb, wout, gffn, wr, br)


def _sc_mesh():
    return plsc.VectorSubcoreMesh(core_axis_name="core", subcore_axis_name="subcore")


def _sc_dispatch(rows, dest_t, n_out):
    chunks, n, w = rows.shape

    @functools.partial(pl.kernel, out_type=jax.ShapeDtypeStruct((chunks, n_out, w), rows.dtype),
                       mesh=_sc_mesh(), scratch_types=[], name="moe_dispatch")
    def run(x_hbm, i_hbm, o_hbm):
        for c in range(chunks):
            def body(x_vmem, i_vmem, c=c):
                for j in range(TOP_K):
                    pltpu.sync_copy(x_vmem, o_hbm.at[c].at[i_vmem.at[j]])

            pltpu.emit_pipeline(
                body,
                grid=(n // SC_WINDOW,),
                in_specs=[pl.BlockSpec((SC_WINDOW, w), lambda i: (i, 0)),
                          pl.BlockSpec((TOP_K, SC_WINDOW), lambda i: (0, i))],
                out_specs=[],
                core_axis_name=("core", "subcore"),
                dimension_semantics=(pltpu.PARALLEL,),
            )(x_hbm.at[c], i_hbm)

    return run(rows, dest_t)


def _sc_gather(table, idx):
    chunks, _, w = table.shape
    m = idx.shape[1]

    @functools.partial(pl.kernel, out_type=jax.ShapeDtypeStruct((chunks, m, w), table.dtype),
                       mesh=_sc_mesh(), scratch_types=[], name="moe_combine_gather")
    def run(t_hbm, i_hbm, o_hbm):
        for c in range(chunks):
            def body(i_vmem, o_vmem, c=c):
                pltpu.sync_copy(t_hbm.at[c].at[i_vmem.at[0]], o_vmem)

            pltpu.emit_pipeline(
                body,
                grid=(m // SC_WINDOW,),
                in_specs=[pl.BlockSpec((1, SC_WINDOW), lambda i: (0, i))],
                out_specs=[pl.BlockSpec((SC_WINDOW, w), lambda i: (i, 0))],
                core_axis_name=("core", "subcore"),
                dimension_semantics=(pltpu.PARALLEL,),
            )(i_hbm, o_hbm.at[c])

    return run(table, idx)


GLU_GROUP = 2 * LANES
EXPERT_SUB = 256


def _expert_kernel(be_ref, valid_ref, x_ref, wgu_ref, bgu_ref, wd_ref, bd_ref, perm_ref, y_ref, wgu_scr, wd_scr):
    i = pl.program_id(0)
    f2 = wgu_ref.shape[2]
    bm = x_ref.shape[1]
    sub = min(EXPERT_SUB, bm)
    valid = valid_ref[i]
    live = valid > 0

    @pl.when(jnp.logical_and(live, jnp.logical_or(i == 0, be_ref[i] != be_ref[jnp.maximum(i - 1, 0)])))
    def _():
        for b in range(f2 // GLU_GROUP):
            cols = slice(b * GLU_GROUP, (b + 1) * GLU_GROUP)
            wgu_scr[:, cols] = _dot(wgu_ref[0, :, cols].astype(BF16), perm_ref[...]).astype(BF16)
        wd_scr[...] = wd_ref[0].astype(BF16)

    def sub_block(t, sub):
        rows = slice(t * sub, (t + 1) * sub)
        x = _unpack_pairs([x_ref[c, rows, :] for c in range(x_ref.shape[0])]).astype(BF16)
        yield
        gu = _dot(x, wgu_scr[...])
        yield
        gu = gu + bgu_ref[0]
        acts = []
        for b in range(f2 // GLU_GROUP):
            glu = jnp.minimum(gu[:, b * GLU_GROUP:b * GLU_GROUP + LANES], SWIGLU_LIMIT)
            lin = jnp.clip(gu[:, b * GLU_GROUP + LANES:(b + 1) * GLU_GROUP], -SWIGLU_LIMIT, SWIGLU_LIMIT)
            acts.append((glu * _sigmoid(SWIGLU_ALPHA * glu) * (lin + 1.0)).astype(BF16))
        act = jnp.concatenate(acts, axis=1)
        yield
        y = _dot(act, wd_scr[...])
        yield
        _store_chunks(y_ref, _pack_pairs(y + bd_ref[0]), rows)

    @pl.when(valid == bm)
    def _():
        _interleave(sub_block(t, sub) for t in range(bm // sub))

    for t in range(bm // sub):
        @pl.when(jnp.logical_and(valid < bm, valid > bm - (t + 1) * sub))
        def _(t=t):
            _interleave([sub_block(t, sub)])


def _experts(block_e, valid, x_buf, wgu, bgu, wd, bd, bm):
    p = x_buf.shape[1]
    f, d = wd.shape[1:]
    perm = np.zeros((GLU_GROUP, GLU_GROUP), np.float32)
    for c in range(LANES):
        perm[2 * c, c] = 1.0
        perm[2 * c + 1, LANES + c] = 1.0
    return pl.pallas_call(
        _expert_kernel,
        grid_spec=pltpu.PrefetchScalarGridSpec(
            num_scalar_prefetch=2,
            grid=(p // bm,),
            in_specs=[
                pl.BlockSpec((d // 2 // SC_COLS, bm, SC_COLS), lambda i, be, nv: (0, i, 0)),
                pl.BlockSpec((1, d, 2 * f), lambda i, be, nv: (be[i], 0, 0)),
                pl.BlockSpec((1, 1, 2 * f), lambda i, be, nv: (be[i], 0, 0)),
                pl.BlockSpec((1, f, d), lambda i, be, nv: (be[i], 0, 0)),
                pl.BlockSpec((1, 1, d), lambda i, be, nv: (be[i], 0, 0)),
                pl.BlockSpec((GLU_GROUP, GLU_GROUP), lambda i, be, nv: (0, 0)),
            ],
            out_specs=pl.BlockSpec((d // 2 // SC_COLS, bm, SC_COLS), lambda i, be, nv: (0, i, 0)),
            scratch_shapes=[pltpu.VMEM((d, 2 * f), BF16), pltpu.VMEM((f, d), BF16)],
        ),
        out_shape=jax.ShapeDtypeStruct((d // 2 // SC_COLS, p, SC_COLS), jnp.int32),
        compiler_params=_params(1),
        name="expert_ffn",
    )(block_e, valid, x_buf, wgu, bgu, wd, bd, jnp.asarray(perm, BF16))


FINAL_SUB = 256


def _final_kernel(h1_ref, yg_ref, rf_ref, p_ref, gple_ref, wpg_ref, wpp_ref, gfin_ref, *rest):
    o_ref = rest[-1]
    tm = h1_ref.shape[0]
    sub = min(FINAL_SUB, tm)

    def sub_tile(t):
        rows = slice(t * sub, (t + 1) * sub)
        rf = rf_ref[rows, :]
        h2 = h1_ref[rows, :]
        for j in range(TOP_K):
            h2 = h2 + _unpack_pairs([yg_ref[c, j, rows, :] for c in range(yg_ref.shape[0])]) * rf[:, j:j + 1]
        hn = _rms(h2, gple_ref[...]).astype(BF16)
        yield
        pre = _dot(hn, wpg_ref[...])
        proj = _dot(p_ref[rows, :].astype(BF16), wpp_ref[...])
        yield
        h3 = h2 + _sigmoid(pre) * proj
        o_ref[rows, :] = _rms(h3, gfin_ref[...])

    _interleave(sub_tile(t) for t in range(tm // sub))


def _final(h1, yg, rf, p2, gple, wpg, wpp, gfin, tm, first_tile, prev_out):
    n_total, d = h1.shape
    n = yg.shape[2]
    pd = p2.shape[1]
    dst = lambda i: (i + first_tile, 0)
    const = lambda i: (0, 0)
    in_specs = [
        pl.BlockSpec((tm, d), dst), pl.BlockSpec((d // 2 // SC_COLS, TOP_K, tm, SC_COLS), lambda i: (0, 0, i, 0)),
        pl.BlockSpec((tm, 2 * TOP_K), dst), pl.BlockSpec((tm, pd), dst),
        pl.BlockSpec((1, d), const), pl.BlockSpec((d, d), const), pl.BlockSpec((pd, d), const),
        pl.BlockSpec((1, d), const),
    ]
    args = [h1, yg, rf, p2, gple, wpg, wpp, gfin]
    aliases = {}
    if prev_out is not None:
        in_specs.append(pl.BlockSpec(memory_space=pl.ANY))
        aliases = {len(args): 0}
        args.append(prev_out)
    return pl.pallas_call(
        _final_kernel,
        grid=(n // tm,),
        in_specs=in_specs,
        out_specs=pl.BlockSpec((tm, d), dst),
        out_shape=jax.ShapeDtypeStruct((n_total, d), F32),
        input_output_aliases=aliases,
        compiler_params=_params(1),
        name="combine_ple_norm",
    )(*args)


def _layer(h, p, mix_norm, w_in, b_forget, w_branch, w_merge_gate, b_merge_gate, w_out, ffn_norm,
           w_router, b_router, w_gate_up, b_gate_up, w_down, b_down, ple_norm, w_ple_gate, w_ple_proj,
           final_norm, *, tm, tq, chunk, bm, groups):
    b, s, d = h.shape
    n = b * s
    x2 = h.reshape(n, d)
    row = lambda t: t.reshape(1, -1)

    c0 = 3 * FOX_W
    w_main = jnp.concatenate([w_in[:, :2 * FOX_W], w_in[:, c0 + FOX_HEADS:]], axis=1).astype(BF16)
    w_vt = w_in[:, 2 * FOX_W:c0].T.astype(BF16)
    w_f = jnp.pad(w_in[:, c0:c0 + FOX_HEADS], ((0, 0), (0, LANES - FOX_HEADS))).astype(BF16)
    b_f = jnp.pad(b_forget, (0, LANES - FOX_HEADS)).reshape(1, LANES)
    half = RET_KEY_DIM // 2
    inv = ROPE_BASE ** (-jnp.arange(half, dtype=F32) / half)
    ang = jnp.arange(s).astype(F32)[:, None] * inv[None, :]
    cos_t = jnp.tile(jnp.cos(ang), (1, RET_QK_W // half))
    sin_t = jnp.tile(jnp.sin(ang), (1, RET_QK_W // half))

    q_aug, k_aug, v_t, rq, rk, rv, rg = _inproj(x2, row(mix_norm), w_main, w_vt, w_f, b_f, cos_t, sin_t, s, tm)
    fox = _fox(q_aug, k_aug, v_t, b, s, tq)

    lg = jnp.log1p(-jnp.exp2(-5.0 - jnp.arange(RET_HEADS, dtype=F32)))
    ro = _retention(lg, rq.reshape(b, s, RET_QK_W), rk.reshape(b, s, RET_QK_W),
                    rv.reshape(b, s, RET_V_W), rg.reshape(b, s, RET_V_W), chunk)

    w_r = jnp.pad(w_router, ((0, 0), (0, LANES - N_EXPERTS))).astype(BF16)
    b_r = jnp.concatenate([b_router, jnp.full((LANES - N_EXPERTS,), -jnp.inf, F32)]).reshape(1, LANES)
    ne, dd, f2 = w_gate_up.shape
    bgu = b_gate_up.reshape(ne, f2 // GLU_GROUP, LANES, 2).transpose(0, 1, 3, 2).reshape(ne, 1, f2)
    merge_w = (row(mix_norm), w_merge_gate.astype(BF16), row(b_merge_gate), w_branch.astype(BF16),
               w_out.astype(BF16), row(ffn_norm), w_r, b_r)
    final_w = (row(ple_norm), w_ple_gate.astype(BF16), w_ple_proj.astype(BF16), row(final_norm))
    p2 = p.reshape(n, -1)

    h1, hn, rf, ri, cnt = _merge(x2, fox, ro.reshape(n, RET_V_W), *merge_w, tm)

    a = n * TOP_K
    nb = -(-(a + N_EXPERTS * (bm - 1)) // bm)
    ri_t = ri.T
    counts = cnt[0, :N_EXPERTS].astype(jnp.int32)
    padded = (counts + bm - 1) // bm * bm
    pad_end = jnp.cumsum(padded)
    first_row = pad_end - counts
    experts = jnp.arange(N_EXPERTS, dtype=jnp.int32)

    def lookup(ids):
        return jnp.sum(jnp.where(ids[None] == experts.reshape((-1,) + (1,) * ids.ndim),
                                 first_row.reshape((-1,) + (1,) * ids.ndim), 0), axis=0)

    dest_t = lookup(ri_t[:TOP_K]) + ri_t[TOP_K:]
    block_start = jnp.arange(nb, dtype=jnp.int32) * bm
    block_e = jnp.minimum(jnp.sum(block_start[:, None] >= pad_end[None, :], axis=1),
                          N_EXPERTS - 1).astype(jnp.int32)
    valid = jnp.where(block_start < pad_end[-1],
                      jnp.clip(block_start + bm - lookup(block_e), 0, bm), 0).astype(jnp.int32)

    x_buf = _sc_dispatch(hn, dest_t, nb * bm)
    y_buf = _experts(block_e, valid, x_buf, w_gate_up, bgu, w_down, b_down.reshape(ne, 1, dd), bm)

    shares = [1] + [2 ** max(g - 1, 0) for g in range(1, groups)]
    unit = n // sum(shares)
    out = None
    start = 0
    for share in shares:
        ng = share * unit
        idx = dest_t[:, start:start + ng].reshape(1, ng * TOP_K)
        yg = _sc_gather(y_buf, idx).reshape(d // 2 // SC_COLS, TOP_K, ng, SC_COLS)
        out = _final(h1, yg, rf, p2, *final_w, tm, start // tm, out)
        start += ng
    return out.reshape(b, s, d)


def kernel(x, p, mix_norm, w_in, b_forget, w_branch, w_merge_gate, b_merge_gate, w_out, ffn_norm, w_router,
           b_router, w_gate_up, b_gate_up, w_down, b_down, ple_norm, w_ple_gate, w_ple_proj, final_norm):
    depth = p.shape[0]
    assert depth == 1, "the final norm is fused into the (single) layer"
    return _layer(x, p[0], mix_norm[0], w_in[0], b_forget[0], w_branch[0], w_merge_gate[0], b_merge_gate[0],
                  w_out[0], ffn_norm[0], w_router[0], b_router[0], w_gate_up[0], b_gate_up[0], w_down[0],
                  b_down[0], ple_norm[0], w_ple_gate[0], w_ple_proj[0], final_norm,
                  tm=1024, tq=512, chunk=128, bm=1024, groups=5)
```

```python
import functools

import numpy as np
import jax
import jax.numpy as jnp
from jax import lax
from jax.experimental import pallas as pl
from jax.experimental.pallas import tpu as pltpu
from jax.experimental.pallas import tpu_sc as plsc

FOX_HEADS = 8
FOX_HEAD_DIM = 64
RET_HEADS = 4
RET_KEY_DIM = 64
RET_VAL_DIM = 128
ROPE_BASE = 10000.0
N_EXPERTS = 32
TOP_K = 4
SWIGLU_LIMIT = 7.0
SWIGLU_ALPHA = 1.702
EPS = 1e-6

LANES = 128
FOX_W = FOX_HEADS * FOX_HEAD_DIM
RET_QK_W = RET_HEADS * RET_KEY_DIM
RET_V_W = RET_HEADS * RET_VAL_DIM
VMEM_LIMIT = 56 * 1024 * 1024
HIGH_HALF = -65536
LOG2_E = 1.4426950408889634
SC_WINDOW = 128
SC_COLS = 256

F32 = jnp.float32
BF16 = jnp.bfloat16


def _rms(x, g):
    return x * lax.rsqrt(jnp.mean(x * x, axis=-1, keepdims=True) + EPS) * g


def _sigmoid(x):
    return 1.0 / (1.0 + jnp.exp(-x))


def _dot(a, b):
    return jnp.dot(a, b, preferred_element_type=F32)


def _dot_nt(a, b):
    return lax.dot_general(a, b, (((1,), (1,)), ((), ())), preferred_element_type=F32)


def _pack_pairs(v):
    bits = pltpu.bitcast(v.astype(BF16).astype(F32), jnp.int32)
    half = v.shape[1] // 2
    return lax.shift_right_logical(bits[:, :half], 16) | (bits[:, half:] & HIGH_HALF)


def _unpack_pairs(chunks):
    return jnp.concatenate([pltpu.bitcast(lax.shift_left(w, 16), F32) for w in chunks]
                           + [pltpu.bitcast(w & HIGH_HALF, F32) for w in chunks], axis=1)


def _store_chunks(ref, v, rows=slice(None)):
    for c in range(ref.shape[0]):
        ref[c, rows, :] = v[:, c * SC_COLS:(c + 1) * SC_COLS]


def _interleave(generators):
    live = []
    pending = list(generators)
    while pending or live:
        if pending:
            live.append(pending.pop(0))
        for g in list(live):
            try:
                next(g)
            except StopIteration:
                live.remove(g)


def _resident(shape):
    return pl.BlockSpec(shape, lambda *_: (0,) * len(shape), pipeline_mode=pl.Buffered(1))


def _params(n_axes):
    return pltpu.CompilerParams(dimension_semantics=("arbitrary",) * n_axes,
                                vmem_limit_bytes=VMEM_LIMIT)


C_MID = FOX_HEADS
C_LO = 2 * FOX_HEADS
C_ONE = 3 * FOX_HEADS
N_AUG = 3
AUG_W = 2 * N_AUG
K_AUG0 = FOX_HEADS * AUG_W


def _split3(t):
    hi = t.astype(BF16).astype(F32)
    r = t - hi
    mid = r.astype(BF16).astype(F32)
    lo = (r - mid).astype(BF16).astype(F32)
    return hi + pltpu.roll(mid, C_MID, 1) + pltpu.roll(lo, C_LO, 1)


INPROJ_SUB = 256


def _inproj_kernel(x_ref, g_ref, w_ref, wvt_ref, wf_ref, bf_ref, cos_ref, sin_ref, place_ref,
                   q_ref, k_ref, vt_ref, rq_ref, rk_ref, rv_ref, rg_ref, carry_scr, *, tiles_per_seq):
    i = pl.program_id(0)
    tm = x_ref.shape[0]
    sub = min(INPROJ_SUB, tm)

    @pl.when(i % tiles_per_seq == 0)
    def _():
        carry_scr[...] = jnp.zeros(carry_scr.shape, F32)

    lane = lax.broadcasted_iota(jnp.int32, (sub, LANES), 1)
    r = lax.broadcasted_iota(jnp.int32, (sub, sub), 0)
    c = lax.broadcasted_iota(jnp.int32, (sub, sub), 1)
    tri = jnp.where(c <= r, 1.0, 0.0).astype(BF16)
    own = lane < FOX_HEAD_DIM
    is_aug = jnp.logical_and(lane >= FOX_HEAD_DIM, lane < FOX_HEAD_DIM + AUG_W)
    half = RET_KEY_DIM // 2
    lane_r = lax.broadcasted_iota(jnp.int32, (sub, RET_QK_W), 1)
    first = (lane_r % RET_KEY_DIM) < half
    carries = [carry_scr[...]]

    def sub_tile(t):
        rows = slice(t * sub, (t + 1) * sub)
        xn = _rms(x_ref[rows, :], g_ref[...]).astype(BF16)
        yield
        u = _dot(xn, w_ref[...])
        vt_ref[:, rows] = _dot_nt(wvt_ref[...], xn).astype(BF16)
        z = _dot(xn, wf_ref[...]) + bf_ref[...]
        yield
        lf = jnp.where(lane < FOX_HEADS, jnp.minimum(z, 0.0) - jnp.log1p(jnp.exp(-jnp.abs(z))), 0.0)
        ps = _dot(tri, _split3(lf).astype(BF16))
        yield
        assert len(carries) == t + 1
        cum = ps + pltpu.roll(ps, LANES - C_MID, 1) + pltpu.roll(ps, LANES - C_LO, 1)
        cum = jnp.where(lane < FOX_HEADS, cum, 0.0) + carries[t]
        carries.append(cum[sub - 1:sub, :])
        c3 = (_split3(cum * LOG2_E) + jnp.where(lane == C_ONE, 1.0, 0.0)).astype(BF16)
        aug = _dot(c3, place_ref[...])

        o = 2 * FOX_W
        rq = u[:, o:o + RET_QK_W]; o += RET_QK_W
        rk = u[:, o:o + RET_QK_W]; o += RET_QK_W
        rv_ref[rows, :] = u[:, o:o + RET_V_W].astype(BF16); o += RET_V_W
        rg_ref[rows, :] = u[:, o:o + RET_V_W]

        cos = cos_ref[rows, :]
        sin = sin_ref[rows, :]

        def rot(v):
            partner = jnp.where(first, -pltpu.roll(v, RET_QK_W - half, 1), pltpu.roll(v, half, 1))
            return v * cos + partner * sin

        rq_ref[rows, :] = rot(rq).astype(BF16)
        rk_ref[rows, :] = (rot(rk) * (RET_KEY_DIM ** -0.5)).astype(BF16)
        yield
        fq = u[:, 0:FOX_W] * (FOX_HEAD_DIM ** -0.5 * LOG2_E)
        fk = u[:, FOX_W:2 * FOX_W]
        for src, aug0, dst in ((fq, 0, q_ref), (fk, K_AUG0, k_ref)):
            for h in range(FOX_HEADS):
                blk = src[:, (h // 2) * LANES:(h // 2 + 1) * LANES]
                if h % 2:
                    blk = pltpu.roll(blk, FOX_HEAD_DIM, 1)
                mine = pltpu.roll(aug, (FOX_HEAD_DIM - aug0 - AUG_W * h) % LANES, 1)
                slab = jnp.where(own, blk, jnp.where(is_aug, mine, 0.0))
                dst[rows, h * LANES:(h + 1) * LANES] = slab.astype(BF16)

    _interleave(sub_tile(t) for t in range(tm // sub))
    carry_scr[...] = carries[-1]


def _placement():
    place = np.zeros((LANES, LANES), np.float32)
    for h in range(FOX_HEADS):
        for part, src in enumerate((h, C_MID + h, C_LO + h)):
            place[C_ONE, AUG_W * h + part] = 1.0
            place[src, AUG_W * h + N_AUG + part] = 1.0
            place[src, K_AUG0 + AUG_W * h + part] = -1.0
            place[C_ONE, K_AUG0 + AUG_W * h + N_AUG + part] = 1.0
    return jnp.asarray(place, BF16)


def _inproj(x2, g, w_main, w_vt, w_f, b_f, cos_t, sin_t, seq, tm):
    n, d = x2.shape
    wn = w_main.shape[1]
    spt = seq // tm
    aw = FOX_HEADS * LANES
    place = _placement()
    row = lambda i: (i, 0)
    const = lambda i: (0, 0)
    pos = lambda i: (i % spt, 0)
    out_shape = [
        jax.ShapeDtypeStruct((n, aw), BF16), jax.ShapeDtypeStruct((n, aw), BF16),
        jax.ShapeDtypeStruct((FOX_W, n), BF16),
        jax.ShapeDtypeStruct((n, RET_QK_W), BF16), jax.ShapeDtypeStruct((n, RET_QK_W), BF16),
        jax.ShapeDtypeStruct((n, RET_V_W), BF16), jax.ShapeDtypeStruct((n, RET_V_W), F32),
    ]
    out_specs = [
        pl.BlockSpec((tm, aw), row), pl.BlockSpec((tm, aw), row),
        pl.BlockSpec((FOX_W, tm), lambda i: (0, i)),
        pl.BlockSpec((tm, RET_QK_W), row), pl.BlockSpec((tm, RET_QK_W), row),
        pl.BlockSpec((tm, RET_V_W), row), pl.BlockSpec((tm, RET_V_W), row),
    ]
    return pl.pallas_call(
        functools.partial(_inproj_kernel, tiles_per_seq=spt),
        grid=(n // tm,),
        in_specs=[
            pl.BlockSpec((tm, d), row), pl.BlockSpec((1, d), const),
            _resident((d, wn)), _resident((FOX_W, d)),
            _resident((d, LANES)), pl.BlockSpec((1, LANES), const),
            pl.BlockSpec((tm, RET_QK_W), pos), pl.BlockSpec((tm, RET_QK_W), pos),
            _resident((LANES, LANES)),
        ],
        out_specs=out_specs,
        out_shape=out_shape,
        scratch_shapes=[pltpu.VMEM((1, LANES), F32)],
        compiler_params=_params(1),
        name="inproj",
    )(x2, g, w_main, w_vt, w_f, b_f, cos_t, sin_t, place)


FOX_AHEAD = 3


def _fox_kernel(q_ref, k_ref, vt_ref, o_ref, *, tq):
    seq = q_ref.shape[0]
    half = tq // 2
    vrow = lax.broadcasted_iota(jnp.int32, (LANES, tq), 0)
    one = jnp.ones((), BF16)
    items = [(qi, j) for qi in range(seq // tq) for j in range(qi + 1)]

    def logits(qi, j):
        out = []
        for a in range(2):
            sl = slice(a * LANES, (a + 1) * LANES)
            kk = k_ref[j * tq:(j + 1) * tq, sl]
            qq = q_ref[qi * tq:(qi + 1) * tq, sl]
            if j != qi:
                out.append([(0, tq, _dot_nt(kk, qq))])
                continue
            parts = []
            for c0, nk in ((0, half), (half, tq)):
                sc = _dot_nt(kk[:nk], qq[c0:c0 + half])
                key = lax.broadcasted_iota(jnp.int32, sc.shape, 0)
                qry = lax.broadcasted_iota(jnp.int32, sc.shape, 1) + c0
                parts.append((c0, nk, jnp.where(key <= qry, sc, -jnp.inf)))
            out.append(parts)
        return out

    queue = [logits(*it) for it in items[:FOX_AHEAD]]
    carry = None
    for w, (qi, j) in enumerate(items):
        if w + FOX_AHEAD < len(items):
            queue.append(logits(*items[w + FOX_AHEAD]))
        s_cur = queue.pop(0)
        if j == 0:
            carry = [(jnp.full((1, tq), -jnp.inf, F32), jnp.zeros((LANES, tq), F32)) for _ in range(2)]
        v = vt_ref[:, j * tq:(j + 1) * tq]
        for a in range(2):
            m_prev, acc = carry[a]
            va = jnp.where((vrow // FOX_HEAD_DIM) == a, v, one)
            ms, accs = [], []
            for c0, nk, sc in s_cur[a]:
                cols = slice(c0, c0 + sc.shape[1])
                m_new = jnp.maximum(m_prev[:, cols], jnp.max(sc, axis=0, keepdims=True))
                alpha = jnp.exp2(m_prev[:, cols] - m_new)
                p = jnp.exp2(sc - m_new).astype(BF16)
                ms.append(m_new)
                accs.append(alpha * acc[:, cols] + _dot(va[:, :nk], p))
            carry[a] = (ms[0], accs[0]) if len(ms) == 1 else (jnp.concatenate(ms, axis=1),
                                                               jnp.concatenate(accs, axis=1))
        if j == qi:
            acc0, acc1 = carry[0][1], carry[1][1]
            ot = jnp.where(vrow < FOX_HEAD_DIM, acc0 / acc0[FOX_HEAD_DIM:FOX_HEAD_DIM + 1, :],
                           acc1 / acc1[0:1, :])
            o_ref[qi * tq:(qi + 1) * tq, :] = ot.T.astype(o_ref.dtype)


def _fox(q_aug, k_aug, v_t, batch, seq, tq):
    n = q_aug.shape[0]
    pairs = FOX_HEADS // 2
    kern = functools.partial(_fox_kernel, tq=tq)
    return pl.pallas_call(
        kern,
        grid=(batch, pairs),
        in_specs=[
            pl.BlockSpec((seq, 2 * LANES), lambda i, h: (i, h)),
            pl.BlockSpec((seq, 2 * LANES), lambda i, h: (i, h)),
            pl.BlockSpec((LANES, seq), lambda i, h: (h, i)),
        ],
        out_specs=pl.BlockSpec((seq, LANES), lambda i, h: (i, h)),
        out_shape=jax.ShapeDtypeStruct((n, FOX_W), BF16),
        compiler_params=_params(2),
        name="fox_attention",
    )(q_aug, k_aug, v_t)


def _ret_kernel(lg_ref, q_ref, k_ref, v_ref, g_ref, o_ref, *, chunk):
    s = q_ref.shape[1]
    lane = lax.broadcasted_iota(jnp.int32, (chunk, LANES), 1)
    ri = lax.broadcasted_iota(jnp.int32, (chunk, chunk), 0)
    ci = lax.broadcasted_iota(jnp.int32, (chunk, chunk), 1)
    diff = (ri - ci).astype(F32)
    pos = lax.broadcasted_iota(jnp.int32, (chunk, 1), 0).astype(F32)

    def head(h):
        lg = lg_ref[h]
        mine = (lane // RET_KEY_DIM) == (h % 2)
        qk = slice((h // 2) * LANES, (h // 2 + 1) * LANES)
        vs = slice(h * RET_VAL_DIM, (h + 1) * RET_VAL_DIM)
        inner = jnp.where(diff >= 0, jnp.exp(jnp.maximum(diff, 0.0) * lg), 0.0)
        q_decay = jnp.exp((pos + 1.0) * lg)
        k_decay = jnp.exp((chunk - 1.0 - pos) * lg)
        chunk_decay = jnp.exp(jnp.full((1, 1), chunk, F32) * lg)
        state = jnp.zeros((LANES, RET_VAL_DIM), F32)
        for c in range(s // chunk):
            rows = slice(c * chunk, (c + 1) * chunk)
            q = jnp.where(mine, q_ref[0, rows, qk], jnp.zeros((), BF16))
            k = jnp.where(mine, k_ref[0, rows, qk], jnp.zeros((), BF16))
            v = v_ref[0, rows, vs]
            scores = _dot_nt(q, k)
            inter = _dot(q, state.astype(BF16))
            kd = (k.astype(F32) * k_decay).astype(BF16)
            update = lax.dot_general(kd, v, (((0,), (0,)), ((), ())), preferred_element_type=F32)
            yield
            o = _dot((scores * inner).astype(BF16), v) + inter * q_decay
            state = state * chunk_decay + update
            yield
            o = o * lax.rsqrt(jnp.mean(o * o, axis=-1, keepdims=True) + EPS)
            g = g_ref[0, rows, vs]
            o_ref[0, rows, vs] = (o * (g * _sigmoid(g))).astype(o_ref.dtype)

    _interleave(head(h) for h in range(RET_HEADS))


def _retention(lg, rq, rk, rv, rg, chunk):
    b, s, _ = rq.shape
    kern = functools.partial(_ret_kernel, chunk=chunk)
    qk_spec = pl.BlockSpec((1, s, RET_QK_W), lambda i, lg_ref: (i, 0, 0))
    v_spec = pl.BlockSpec((1, s, RET_V_W), lambda i, lg_ref: (i, 0, 0))
    return pl.pallas_call(
        kern,
        grid_spec=pltpu.PrefetchScalarGridSpec(
            num_scalar_prefetch=1,
            grid=(b,),
            in_specs=[qk_spec, qk_spec, v_spec, v_spec],
            out_specs=v_spec,
        ),
        out_shape=jax.ShapeDtypeStruct((b, s, RET_V_W), BF16),
        compiler_params=_params(1),
        name="retention",
    )(lg, rq, rk, rv, rg)


MERGE_SUB = 256


def _merge_kernel(x_ref, fox_ref, ro_ref, gmix_ref, wmg_ref, bmg_ref, wb_ref, wout_ref, gffn_ref,
                  wr_ref, br_ref, h1_ref, hn_ref, rf_ref, ri_ref, cnt_ref, base_scr):
    i = pl.program_id(0)
    tm, d = x_ref.shape
    sub = min(MERGE_SUB, tm)

    @pl.when(i == 0)
    def _():
        base_scr[...] = jnp.zeros(base_scr.shape, F32)

    lane = lax.broadcasted_iota(jnp.int32, (sub, LANES), 1)
    lane_f = lane.astype(F32)
    r = lax.broadcasted_iota(jnp.int32, (sub, sub), 0)
    c = lax.broadcasted_iota(jnp.int32, (sub, sub), 1)
    tri = jnp.where(c < r, 1.0, 0.0).astype(BF16)
    counts = [base_scr[...]]

    def sub_tile(t):
        rows = slice(t * sub, (t + 1) * sub)
        x = x_ref[rows, :]
        xn = _rms(x, gmix_ref[...]).astype(BF16)
        yield
        pre = _dot(xn, wmg_ref[...])
        yield
        gate = _sigmoid(pre + bmg_ref[...])
        yield
        pf = _dot(fox_ref[rows, :], wb_ref[0])
        pr = _dot(ro_ref[rows, :], wb_ref[1])
        yield
        merged = (gate[:, :d] * pf + gate[:, d:] * pr).astype(BF16)
        yield
        h1 = x + _dot(merged, wout_ref[...])
        yield
        h1_ref[rows, :] = h1
        hn = _rms(h1, gffn_ref[...]).astype(BF16)
        _store_chunks(hn_ref, _pack_pairs(hn), rows)
        yield
        logits = _dot(hn, wr_ref[...]) + br_ref[...]
        yield
        vals, idxs = [], []
        cur = logits
        for _ in range(TOP_K):
            m = jnp.max(cur, axis=-1, keepdims=True)
            idx = jnp.min(jnp.where(cur == m, lane_f, float(LANES)), axis=-1, keepdims=True)
            vals.append(m)
            idxs.append(idx)
            cur = jnp.where(lane_f == idx, -jnp.inf, cur)
        exps = [jnp.exp(v - vals[0]) for v in vals]
        denom = exps[0] + exps[1] + exps[2] + exps[3]
        onehot = jnp.zeros(logits.shape, F32)
        for idx in idxs:
            onehot = onehot + jnp.where(lane_f == idx, 1.0, 0.0)
        yield
        assert len(counts) == t + 1
        before = _dot(tri, onehot.astype(BF16)) + counts[t]
        counts.append(counts[t] + jnp.sum(onehot, axis=0, keepdims=True))
        rf = jnp.zeros(logits.shape, F32)
        ri = jnp.zeros(logits.shape, F32)
        for j in range(TOP_K):
            rank = jnp.sum(jnp.where(lane_f == idxs[j], before, 0.0), axis=-1, keepdims=True)
            rf = jnp.where(lane == j, exps[j] / denom, rf)
            ri = jnp.where(lane == j, idxs[j], ri)
            ri = jnp.where(lane == TOP_K + j, rank, ri)
        rf_ref[rows, :] = rf[:, :2 * TOP_K]
        ri_ref[rows, :] = ri[:, :2 * TOP_K].astype(jnp.int32)

    _interleave(sub_tile(t) for t in range(tm // sub))
    base_scr[...] = counts[-1]
    cnt_ref[...] = counts[-1]


def _merge(x2, fox, ro, gmix, wmg, bmg, wb, wout, gffn, wr, br, tm):
    n, d = x2.shape
    row = lambda i: (i, 0)
    const = lambda i: (0, 0)
    return pl.pallas_call(
        _merge_kernel,
        grid=(n // tm,),
        in_specs=[
            pl.BlockSpec((tm, d), row), pl.BlockSpec((tm, FOX_W), row), pl.BlockSpec((tm, RET_V_W), row),
            pl.BlockSpec((1, d), const), _resident((d, 2 * d)), pl.BlockSpec((1, 2 * d), const),
            _resident((2, FOX_W, d)), _resident((d, d)),
            pl.BlockSpec((1, d), const), _resident((d, LANES)), pl.BlockSpec((1, LANES), const),
        ],
        out_specs=[
            pl.BlockSpec((tm, d), row), pl.BlockSpec((d // 2 // SC_COLS, tm, SC_COLS), lambda i: (0, i, 0)),
            pl.BlockSpec((tm, 2 * TOP_K), row), pl.BlockSpec((tm, 2 * TOP_K), row),
            pl.BlockSpec((1, LANES), const),
        ],
        out_shape=[
            jax.ShapeDtypeStruct((n, d), F32), jax.ShapeDtypeStruct((d // 2 // SC_COLS, n, SC_COLS), jnp.int32),
            jax.ShapeDtypeStruct((n, 2 * TOP_K), F32), jax.ShapeDtypeStruct((n, 2 * TOP_K), jnp.int32),
            jax.ShapeDtypeStruct((1, LANES), F32),
        ],
        scratch_shapes=[pltpu.VMEM((1, LANES), F32)],
        compiler_params=_params(1),
        name="merge_router",
    )(x2, fox, ro, gmix, wmg, bmg, wb, wout, gffn, wr, br)


def _sc_mesh():
    return plsc.VectorSubcoreMesh(core_axis_name="core", subcore_axis_name="subcore")


def _sc_dispatch(rows, dest_t, n_out):
    chunks, n, w = rows.shape

    @functools.partial(pl.kernel, out_type=jax.ShapeDtypeStruct((chunks, n_out, w), rows.dtype),
                       mesh=_sc_mesh(), scratch_types=[], name="moe_dispatch")
    def run(x_hbm, i_hbm, o_hbm):
        for c in range(chunks):
            def body(x_vmem, i_vmem, c=c):
                for j in range(TOP_K):
                    pltpu.sync_copy(x_vmem, o_hbm.at[c].at[i_vmem.at[j]])

            pltpu.emit_pipeline(
                body,
                grid=(n // SC_WINDOW,),
                in_specs=[pl.BlockSpec((SC_WINDOW, w), lambda i: (i, 0)),
                          pl.BlockSpec((TOP_K, SC_WINDOW), lambda i: (0, i))],
                out_specs=[],
                core_axis_name=("core", "subcore"),
                dimension_semantics=(pltpu.PARALLEL,),
            )(x_hbm.at[c], i_hbm)

    return run(rows, dest_t)


def _sc_gather(table, idx):
    chunks, _, w = table.shape
    m = idx.shape[1]

    @functools.partial(pl.kernel, out_type=jax.ShapeDtypeStruct((chunks, m, w), table.dtype),
                       mesh=_sc_mesh(), scratch_types=[], name="moe_combine_gather")
    def run(t_hbm, i_hbm, o_hbm):
        for c in range(chunks):
            def body(i_vmem, o_vmem, c=c):
                pltpu.sync_copy(t_hbm.at[c].at[i_vmem.at[0]], o_vmem)

            pltpu.emit_pipeline(
                body,
                grid=(m // SC_WINDOW,),
                in_specs=[pl.BlockSpec((1, SC_WINDOW), lambda i: (0, i))],
                out_specs=[pl.BlockSpec((SC_WINDOW, w), lambda i: (i, 0))],
                core_axis_name=("core", "subcore"),
                dimension_semantics=(pltpu.PARALLEL,),
            )(i_hbm, o_hbm.at[c])

    return run(table, idx)


GLU_GROUP = 2 * LANES
EXPERT_SUB = 256


def _expert_kernel(be_ref, valid_ref, x_ref, wgu_ref, bgu_ref, wd_ref, bd_ref, perm_ref, y_ref, wgu_scr, wd_scr):
    i = pl.program_id(0)
    f2 = wgu_ref.shape[2]
    bm = x_ref.shape[1]
    sub = min(EXPERT_SUB, bm)
    valid = valid_ref[i]
    live = valid > 0

    @pl.when(jnp.logical_and(live, jnp.logical_or(i == 0, be_ref[i] != be_ref[jnp.maximum(i - 1, 0)])))
    def _():
        for b in range(f2 // GLU_GROUP):
            cols = slice(b * GLU_GROUP, (b + 1) * GLU_GROUP)
            wgu_scr[:, cols] = _dot(wgu_ref[0, :, cols].astype(BF16), perm_ref[...]).astype(BF16)
        wd_scr[...] = wd_ref[0].astype(BF16)

    def sub_block(t, sub):
        rows = slice(t * sub, (t + 1) * sub)
        x = _unpack_pairs([x_ref[c, rows, :] for c in range(x_ref.shape[0])]).astype(BF16)
        yield
        gu = _dot(x, wgu_scr[...])
        yield
        gu = gu + bgu_ref[0]
        acts = []
        for b in range(f2 // GLU_GROUP):
            glu = jnp.minimum(gu[:, b * GLU_GROUP:b * GLU_GROUP + LANES], SWIGLU_LIMIT)
            lin = jnp.clip(gu[:, b * GLU_GROUP + LANES:(b + 1) * GLU_GROUP], -SWIGLU_LIMIT, SWIGLU_LIMIT)
            acts.append((glu * _sigmoid(SWIGLU_ALPHA * glu) * (lin + 1.0)).astype(BF16))
        act = jnp.concatenate(acts, axis=1)
        yield
        y = _dot(act, wd_scr[...])
        yield
        _store_chunks(y_ref, _pack_pairs(y + bd_ref[0]), rows)

    @pl.when(valid == bm)
    def _():
        _interleave(sub_block(t, sub) for t in range(bm // sub))

    for t in range(bm // sub):
        @pl.when(jnp.logical_and(valid < bm, valid > bm - (t + 1) * sub))
        def _(t=t):
            _interleave([sub_block(t, sub)])


def _experts(block_e, valid, x_buf, wgu, bgu, wd, bd, bm):
    p = x_buf.shape[1]
    f, d = wd.shape[1:]
    perm = np.zeros((GLU_GROUP, GLU_GROUP), np.float32)
    for c in range(LANES):
        perm[2 * c, c] = 1.0
        perm[2 * c + 1, LANES + c] = 1.0
    return pl.pallas_call(
        _expert_kernel,
        grid_spec=pltpu.PrefetchScalarGridSpec(
            num_scalar_prefetch=2,
            grid=(p // bm,),
            in_specs=[
                pl.BlockSpec((d // 2 // SC_COLS, bm, SC_COLS), lambda i, be, nv: (0, i, 0)),
                pl.BlockSpec((1, d, 2 * f), lambda i, be, nv: (be[i], 0, 0)),
                pl.BlockSpec((1, 1, 2 * f), lambda i, be, nv: (be[i], 0, 0)),
                pl.BlockSpec((1, f, d), lambda i, be, nv: (be[i], 0, 0)),
                pl.BlockSpec((1, 1, d), lambda i, be, nv: (be[i], 0, 0)),
                pl.BlockSpec((GLU_GROUP, GLU_GROUP), lambda i, be, nv: (0, 0)),
            ],
            out_specs=pl.BlockSpec((d // 2 // SC_COLS, bm, SC_COLS), lambda i, be, nv: (0, i, 0)),
            scratch_shapes=[pltpu.VMEM((d, 2 * f), BF16), pltpu.VMEM((f, d), BF16)],
        ),
        out_shape=jax.ShapeDtypeStruct((d // 2 // SC_COLS, p, SC_COLS), jnp.int32),
        compiler_params=_params(1),
        name="expert_ffn",
    )(block_e, valid, x_buf, wgu, bgu, wd, bd, jnp.asarray(perm, BF16))


FINAL_SUB = 256


def _final_kernel(h1_ref, yg_ref, rf_ref, p_ref, gple_ref, wpg_ref, wpp_ref, gfin_ref, *rest):
    o_ref = rest[-1]
    tm = h1_ref.shape[0]
    sub = min(FINAL_SUB, tm)

    def sub_tile(t):
        rows = slice(t * sub, (t + 1) * sub)
        rf = rf_ref[rows, :]
        h2 = h1_ref[rows, :]
        for j in range(TOP_K):
            h2 = h2 + _unpack_pairs([yg_ref[c, j, rows, :] for c in range(yg_ref.shape[0])]) * rf[:, j:j + 1]
        hn = _rms(h2, gple_ref[...]).astype(BF16)
        yield
        pre = _dot(hn, wpg_ref[...])
        proj = _dot(p_ref[rows, :].astype(BF16), wpp_ref[...])
        yield
        h3 = h2 + _sigmoid(pre) * proj
        o_ref[rows, :] = _rms(h3, gfin_ref[...])

    _interleave(sub_tile(t) for t in range(tm // sub))


def _final(h1, yg, rf, p2, gple, wpg, wpp, gfin, tm, first_tile, prev_out):
    n_total, d = h1.shape
    n = yg.shape[2]
    pd = p2.shape[1]
    dst = lambda i: (i + first_tile, 0)
    const = lambda i: (0, 0)
    in_specs = [
        pl.BlockSpec((tm, d), dst), pl.BlockSpec((d // 2 // SC_COLS, TOP_K, tm, SC_COLS), lambda i: (0, 0, i, 0)),
        pl.BlockSpec((tm, 2 * TOP_K), dst), pl.BlockSpec((tm, pd), dst),
        pl.BlockSpec((1, d), const), pl.BlockSpec((d, d), const), pl.BlockSpec((pd, d), const),
        pl.BlockSpec((1, d), const),
    ]
    args = [h1, yg, rf, p2, gple, wpg, wpp, gfin]
    aliases = {}
    if prev_out is not None:
        in_specs.append(pl.BlockSpec(memory_space=pl.ANY))
        aliases = {len(args): 0}
        args.append(prev_out)
    return pl.pallas_call(
        _final_kernel,
        grid=(n // tm,),
        in_specs=in_specs,
        out_specs=pl.BlockSpec((tm, d), dst),
        out_shape=jax.ShapeDtypeStruct((n_total, d), F32),
        input_output_aliases=aliases,
        compiler_params=_params(1),
        name="combine_ple_norm",
    )(*args)


def _layer(h, p, mix_norm, w_in, b_forget, w_branch, w_merge_gate, b_merge_gate, w_out, ffn_norm,
           w_router, b_router, w_gate_up, b_gate_up, w_down, b_down, ple_norm, w_ple_gate, w_ple_proj,
           final_norm, *, tm, tq, chunk, bm, groups):
    b, s, d = h.shape
    n = b * s
    x2 = h.reshape(n, d)
    row = lambda t: t.reshape(1, -1)

    c0 = 3 * FOX_W
    w_main = jnp.concatenate([w_in[:, :2 * FOX_W], w_in[:, c0 + FOX_HEADS:]], axis=1).astype(BF16)
    w_vt = w_in[:, 2 * FOX_W:c0].T.astype(BF16)
    w_f = jnp.pad(w_in[:, c0:c0 + FOX_HEADS], ((0, 0), (0, LANES - FOX_HEADS))).astype(BF16)
    b_f = jnp.pad(b_forget, (0, LANES - FOX_HEADS)).reshape(1, LANES)
    half = RET_KEY_DIM // 2
    inv = ROPE_BASE ** (-jnp.arange(half, dtype=F32) / half)
    ang = jnp.arange(s).astype(F32)[:, None] * inv[None, :]
    cos_t = jnp.tile(jnp.cos(ang), (1, RET_QK_W // half))
    sin_t = jnp.tile(jnp.sin(ang), (1, RET_QK_W // half))

    q_aug, k_aug, v_t, rq, rk, rv, rg = _inproj(x2, row(mix_norm), w_main, w_vt, w_f, b_f, cos_t, sin_t, s, tm)
    fox = _fox(q_aug, k_aug, v_t, b, s, tq)

    lg = jnp.log1p(-jnp.exp2(-5.0 - jnp.arange(RET_HEADS, dtype=F32)))
    ro = _retention(lg, rq.reshape(b, s, RET_QK_W), rk.reshape(b, s, RET_QK_W),
                    rv.reshape(b, s, RET_V_W), rg.reshape(b, s, RET_V_W), chunk)

    w_r = jnp.pad(w_router, ((0, 0), (0, LANES - N_EXPERTS))).astype(BF16)
    b_r = jnp.concatenate([b_router, jnp.full((LANES - N_EXPERTS,), -jnp.inf, F32)]).reshape(1, LANES)
    ne, dd, f2 = w_gate_up.shape
    bgu = b_gate_up.reshape(ne, f2 // GLU_GROUP, LANES, 2).transpose(0, 1, 3, 2).reshape(ne, 1, f2)
    merge_w = (row(mix_norm), w_merge_gate.astype(BF16), row(b_merge_gate), w_branch.astype(BF16),
               w_out.astype(BF16), row(ffn_norm), w_r, b_r)
    final_w = (row(ple_norm), w_ple_gate.astype(BF16), w_ple_proj.astype(BF16), row(final_norm))
    p2 = p.reshape(n, -1)

    h1, hn, rf, ri, cnt = _merge(x2, fox, ro.reshape(n, RET_V_W), *merge_w, tm)

    a = n * TOP_K
    nb = -(-(a + N_EXPERTS * (bm - 1)) // bm)
    ri_t = ri.T
    counts = cnt[0, :N_EXPERTS].astype(jnp.int32)
    padded = (counts + bm - 1) // bm * bm
    pad_end = jnp.cumsum(padded)
    first_row = pad_end - counts
    experts = jnp.arange(N_EXPERTS, dtype=jnp.int32)

    def lookup(ids):
        return jnp.sum(jnp.where(ids[None] == experts.reshape((-1,) + (1,) * ids.ndim),
                                 first_row.reshape((-1,) + (1,) * ids.ndim), 0), axis=0)

    dest_t = lookup(ri_t[:TOP_K]) + ri_t[TOP_K:]
    block_start = jnp.arange(nb, dtype=jnp.int32) * bm
    block_e = jnp.minimum(jnp.sum(block_start[:, None] >= pad_end[None, :], axis=1),
                          N_EXPERTS - 1).astype(jnp.int32)
    valid = jnp.where(block_start < pad_end[-1],
                      jnp.clip(block_start + bm - lookup(block_e), 0, bm), 0).astype(jnp.int32)

    x_buf = _sc_dispatch(hn, dest_t, nb * bm)
    y_buf = _experts(block_e, valid, x_buf, w_gate_up, bgu, w_down, b_down.reshape(ne, 1, dd), bm)

    shares = [1] + [2 ** max(g - 1, 0) for g in range(1, groups)]
    unit = n // sum(shares)
    out = None
    start = 0
    for share in shares:
        ng = share * unit
        idx = dest_t[:, start:start + ng].reshape(1, ng * TOP_K)
        yg = _sc_gather(y_buf, idx).reshape(d // 2 // SC_COLS, TOP_K, ng, SC_COLS)
        out = _final(h1, yg, rf, p2, *final_w, tm, start // tm, out)
        start += ng
    return out.reshape(b, s, d)


def kernel(x, p, mix_norm, w_in, b_forget, w_branch, w_merge_gate, b_merge_gate, w_out, ffn_norm, w_router,
           b_router, w_gate_up, b_gate_up, w_down, b_down, ple_norm, w_ple_gate, w_ple_proj, final_norm):
    depth = p.shape[0]
    assert depth == 1, "the final norm is fused into the (single) layer"
    return _layer(x, p[0], mix_norm[0], w_in[0], b_forget[0], w_branch[0], w_merge_gate[0], b_merge_gate[0],
                  w_out[0], ffn_norm[0], w_router[0], b_router[0], w_gate_up[0], b_gate_up[0], w_down[0],
                  b_down[0], ple_norm[0], w_ple_gate[0], w_ple_proj[0], final_norm,
                  tm=1024, tq=512, chunk=128, bm=1024, groups=5)
```

```python
import functools

import numpy as np
import jax
import jax.numpy as jnp
from jax import lax
from jax.experimental import pallas as pl
from jax.experimental.pallas import tpu as pltpu
from jax.experimental.pallas import tpu_sc as plsc

FOX_HEADS = 8
FOX_HEAD_DIM = 64
RET_HEADS = 4
RET_KEY_DIM = 64
RET_VAL_DIM = 128
ROPE_BASE = 10000.0
N_EXPERTS = 32
TOP_K = 4
SWIGLU_LIMIT = 7.0
SWIGLU_ALPHA = 1.702
EPS = 1e-6

LANES = 128
FOX_W = FOX_HEADS * FOX_HEAD_DIM
RET_QK_W = RET_HEADS * RET_KEY_DIM
RET_V_W = RET_HEADS * RET_VAL_DIM
VMEM_LIMIT = 56 * 1024 * 1024
HIGH_HALF = -65536
LOG2_E = 1.4426950408889634
SC_WINDOW = 128
SC_COLS = 256

F32 = jnp.float32
BF16 = jnp.bfloat16


def _rms(x, g):
    return x * lax.rsqrt(jnp.mean(x * x, axis=-1, keepdims=True) + EPS) * g


def _sigmoid(x):
    return 1.0 / (1.0 + jnp.exp(-x))


def _dot(a, b):
    return jnp.dot(a, b, preferred_element_type=F32)


def _dot_nt(a, b):
    return lax.dot_general(a, b, (((1,), (1,)), ((), ())), preferred_element_type=F32)


def _pack_pairs(v):
    bits = pltpu.bitcast(v.astype(BF16).astype(F32), jnp.int32)
    half = v.shape[1] // 2
    return lax.shift_right_logical(bits[:, :half], 16) | (bits[:, half:] & HIGH_HALF)


def _unpack_pairs(chunks):
    return jnp.concatenate([pltpu.bitcast(lax.shift_left(w, 16), F32) for w in chunks]
                           + [pltpu.bitcast(w & HIGH_HALF, F32) for w in chunks], axis=1)


def _store_chunks(ref, v, rows=slice(None)):
    for c in range(ref.shape[0]):
        ref[c, rows, :] = v[:, c * SC_COLS:(c + 1) * SC_COLS]


def _interleave(generators):
    live = []
    pending = list(generators)
    while pending or live:
        if pending:
            live.append(pending.pop(0))
        for g in list(live):
            try:
                next(g)
            except StopIteration:
                live.remove(g)


def _resident(shape):
    return pl.BlockSpec(shape, lambda *_: (0,) * len(shape), pipeline_mode=pl.Buffered(1))


def _params(n_axes):
    return pltpu.CompilerParams(dimension_semantics=("arbitrary",) * n_axes,
                                vmem_limit_bytes=VMEM_LIMIT)


C_MID = FOX_HEADS
C_LO = 2 * FOX_HEADS
C_ONE = 3 * FOX_HEADS
N_AUG = 3
AUG_W = 2 * N_AUG
K_AUG0 = FOX_HEADS * AUG_W


def _split3(t):
    hi = t.astype(BF16).astype(F32)
    r = t - hi
    mid = r.astype(BF16).astype(F32)
    lo = (r - mid).astype(BF16).astype(F32)
    return hi + pltpu.roll(mid, C_MID, 1) + pltpu.roll(lo, C_LO, 1)


INPROJ_SUB = 256


def _inproj_kernel(x_ref, g_ref, w_ref, wvt_ref, wf_ref, bf_ref, cos_ref, sin_ref, place_ref,
                   q_ref, k_ref, vt_ref, rq_ref, rk_ref, rv_ref, rg_ref, carry_scr, *, tiles_per_seq):
    i = pl.program_id(0)
    tm = x_ref.shape[0]
    sub = min(INPROJ_SUB, tm)

    @pl.when(i % tiles_per_seq == 0)
    def _():
        carry_scr[...] = jnp.zeros(carry_scr.shape, F32)

    lane = lax.broadcasted_iota(jnp.int32, (sub, LANES), 1)
    r = lax.broadcasted_iota(jnp.int32, (sub, sub), 0)
    c = lax.broadcasted_iota(jnp.int32, (sub, sub), 1)
    tri = jnp.where(c <= r, 1.0, 0.0).astype(BF16)
    own = lane < FOX_HEAD_DIM
    is_aug = jnp.logical_and(lane >= FOX_HEAD_DIM, lane < FOX_HEAD_DIM + AUG_W)
    half = RET_KEY_DIM // 2
    lane_r = lax.broadcasted_iota(jnp.int32, (sub, RET_QK_W), 1)
    first = (lane_r % RET_KEY_DIM) < half
    carries = [carry_scr[...]]

    def sub_tile(t):
        rows = slice(t * sub, (t + 1) * sub)
        xn = _rms(x_ref[rows, :], g_ref[...]).astype(BF16)
        yield
        u = _dot(xn, w_ref[...])
        vt_ref[:, rows] = _dot_nt(wvt_ref[...], xn).astype(BF16)
        z = _dot(xn, wf_ref[...]) + bf_ref[...]
        yield
        lf = jnp.where(lane < FOX_HEADS, jnp.minimum(z, 0.0) - jnp.log1p(jnp.exp(-jnp.abs(z))), 0.0)
        ps = _dot(tri, _split3(lf).astype(BF16))
        yield
        assert len(carries) == t + 1
        cum = ps + pltpu.roll(ps, LANES - C_MID, 1) + pltpu.roll(ps, LANES - C_LO, 1)
        cum = jnp.where(lane < FOX_HEADS, cum, 0.0) + carries[t]
        carries.append(cum[sub - 1:sub, :])
        c3 = (_split3(cum * LOG2_E) + jnp.where(lane == C_ONE, 1.0, 0.0)).astype(BF16)
        aug = _dot(c3, place_ref[...])

        o = 2 * FOX_W
        rq = u[:, o:o + RET_QK_W]; o += RET_QK_W
        rk = u[:, o:o + RET_QK_W]; o += RET_QK_W
        rv_ref[rows, :] = u[:, o:o + RET_V_W].astype(BF16); o += RET_V_W
        rg_ref[rows, :] = u[:, o:o + RET_V_W]

        cos = cos_ref[rows, :]
        sin = sin_ref[rows, :]

        def rot(v):
            partner = jnp.where(first, -pltpu.roll(v, RET_QK_W - half, 1), pltpu.roll(v, half, 1))
            return v * cos + partner * sin

        rq_ref[rows, :] = rot(rq).astype(BF16)
        rk_ref[rows, :] = (rot(rk) * (RET_KEY_DIM ** -0.5)).astype(BF16)
        yield
        fq = u[:, 0:FOX_W] * (FOX_HEAD_DIM ** -0.5 * LOG2_E)
        fk = u[:, FOX_W:2 * FOX_W]
        for src, aug0, dst in ((fq, 0, q_ref), (fk, K_AUG0, k_ref)):
            for h in range(FOX_HEADS):
                blk = src[:, (h // 2) * LANES:(h // 2 + 1) * LANES]
                if h % 2:
                    blk = pltpu.roll(blk, FOX_HEAD_DIM, 1)
                mine = pltpu.roll(aug, (FOX_HEAD_DIM - aug0 - AUG_W * h) % LANES, 1)
                slab = jnp.where(own, blk, jnp.where(is_aug, mine, 0.0))
                dst[rows, h * LANES:(h + 1) * LANES] = slab.astype(BF16)

    _interleave(sub_tile(t) for t in range(tm // sub))
    carry_scr[...] = carries[-1]


def _placement():
    place = np.zeros((LANES, LANES), np.float32)
    for h in range(FOX_HEADS):
        for part, src in enumerate((h, C_MID + h, C_LO + h)):
            place[C_ONE, AUG_W * h + part] = 1.0
            place[src, AUG_W * h + N_AUG + part] = 1.0
            place[src, K_AUG0 + AUG_W * h + part] = -1.0
            place[C_ONE, K_AUG0 + AUG_W * h + N_AUG + part] = 1.0
    return jnp.asarray(place, BF16)


def _inproj(x2, g, w_main, w_vt, w_f, b_f, cos_t, sin_t, seq, tm):
    n, d = x2.shape
    wn = w_main.shape[1]
    spt = seq // tm
    aw = FOX_HEADS * LANES
    place = _placement()
    row = lambda i: (i, 0)
    const = lambda i: (0, 0)
    pos = lambda i: (i % spt, 0)
    out_shape = [
        jax.ShapeDtypeStruct((n, aw), BF16), jax.ShapeDtypeStruct((n, aw), BF16),
        jax.ShapeDtypeStruct((FOX_W, n), BF16),
        jax.ShapeDtypeStruct((n, RET_QK_W), BF16), jax.ShapeDtypeStruct((n, RET_QK_W), BF16),
        jax.ShapeDtypeStruct((n, RET_V_W), BF16), jax.ShapeDtypeStruct((n, RET_V_W), F32),
    ]
    out_specs = [
        pl.BlockSpec((tm, aw), row), pl.BlockSpec((tm, aw), row),
        pl.BlockSpec((FOX_W, tm), lambda i: (0, i)),
        pl.BlockSpec((tm, RET_QK_W), row), pl.BlockSpec((tm, RET_QK_W), row),
        pl.BlockSpec((tm, RET_V_W), row), pl.BlockSpec((tm, RET_V_W), row),
    ]
    return pl.pallas_call(
        functools.partial(_inproj_kernel, tiles_per_seq=spt),
        grid=(n // tm,),
        in_specs=[
            pl.BlockSpec((tm, d), row), pl.BlockSpec((1, d), const),
            _resident((d, wn)), _resident((FOX_W, d)),
            _resident((d, LANES)), pl.BlockSpec((1, LANES), const),
            pl.BlockSpec((tm, RET_QK_W), pos), pl.BlockSpec((tm, RET_QK_W), pos),
            _resident((LANES, LANES)),
        ],
        out_specs=out_specs,
        out_shape=out_shape,
        scratch_shapes=[pltpu.VMEM((1, LANES), F32)],
        compiler_params=_params(1),
        name="inproj",
    )(x2, g, w_main, w_vt, w_f, b_f, cos_t, sin_t, place)


FOX_AHEAD = 3


def _fox_kernel(q_ref, k_ref, vt_ref, o_ref, *, tq):
    seq = q_ref.shape[0]
    half = tq // 2
    vrow = lax.broadcasted_iota(jnp.int32, (LANES, tq), 0)
    one = jnp.ones((), BF16)
    items = [(qi, j) for qi in range(seq // tq) for j in range(qi + 1)]

    def logits(qi, j):
        out = []
        for a in range(2):
            sl = slice(a * LANES, (a + 1) * LANES)
            kk = k_ref[j * tq:(j + 1) * tq, sl]
            qq = q_ref[qi * tq:(qi + 1) * tq, sl]
            if j != qi:
                out.append([(0, tq, _dot_nt(kk, qq))])
                continue
            parts = []
            for c0, nk in ((0, half), (half, tq)):
                sc = _dot_nt(kk[:nk], qq[c0:c0 + half])
                key = lax.broadcasted_iota(jnp.int32, sc.shape, 0)
                qry = lax.broadcasted_iota(jnp.int32, sc.shape, 1) + c0
                parts.append((c0, nk, jnp.where(key <= qry, sc, -jnp.inf)))
            out.append(parts)
        return out

    queue = [logits(*it) for it in items[:FOX_AHEAD]]
    carry = None
    for w, (qi, j) in enumerate(items):
        if w + FOX_AHEAD < len(items):
            queue.append(logits(*items[w + FOX_AHEAD]))
        s_cur = queue.pop(0)
        if j == 0:
            carry = [(jnp.full((1, tq), -jnp.inf, F32), jnp.zeros((LANES, tq), F32)) for _ in range(2)]
        v = vt_ref[:, j * tq:(j + 1) * tq]
        for a in range(2):
            m_prev, acc = carry[a]
            va = jnp.where((vrow // FOX_HEAD_DIM) == a, v, one)
            ms, accs = [], []
            for c0, nk, sc in s_cur[a]:
                cols = slice(c0, c0 + sc.shape[1])
                m_new = jnp.maximum(m_prev[:, cols], jnp.max(sc, axis=0, keepdims=True))
                alpha = jnp.exp2(m_prev[:, cols] - m_new)
                p = jnp.exp2(sc - m_new).astype(BF16)
                ms.append(m_new)
                accs.append(alpha * acc[:, cols] + _dot(va[:, :nk], p))
            carry[a] = (ms[0], accs[0]) if len(ms) == 1 else (jnp.concatenate(ms, axis=1),
                                                               jnp.concatenate(accs, axis=1))
        if j == qi:
            acc0, acc1 = carry[0][1], carry[1][1]
            ot = jnp.where(vrow < FOX_HEAD_DIM, acc0 / acc0[FOX_HEAD_DIM:FOX_HEAD_DIM + 1, :],
                           acc1 / acc1[0:1, :])
            o_ref[qi * tq:(qi + 1) * tq, :] = ot.T.astype(o_ref.dtype)


def _fox(q_aug, k_aug, v_t, batch, seq, tq):
    n = q_aug.shape[0]
    pairs = FOX_HEADS // 2
    kern = functools.partial(_fox_kernel, tq=tq)
    return pl.pallas_call(
        kern,
        grid=(batch, pairs),
        in_specs=[
            pl.BlockSpec((seq, 2 * LANES), lambda i, h: (i, h)),
            pl.BlockSpec((seq, 2 * LANES), lambda i, h: (i, h)),
            pl.BlockSpec((LANES, seq), lambda i, h: (h, i)),
        ],
        out_specs=pl.BlockSpec((seq, LANES), lambda i, h: (i, h)),
        out_shape=jax.ShapeDtypeStruct((n, FOX_W), BF16),
        compiler_params=_params(2),
        name="fox_attention",
    )(q_aug, k_aug, v_t)


def _ret_kernel(lg_ref, q_ref, k_ref, v_ref, g_ref, o_ref, *, chunk):
    s = q_ref.shape[1]
    lane = lax.broadcasted_iota(jnp.int32, (chunk, LANES), 1)
    ri = lax.broadcasted_iota(jnp.int32, (chunk, chunk), 0)
    ci = lax.broadcasted_iota(jnp.int32, (chunk, chunk), 1)
    diff = (ri - ci).astype(F32)
    pos = lax.broadcasted_iota(jnp.int32, (chunk, 1), 0).astype(F32)

    def head(h):
        lg = lg_ref[h]
        mine = (lane // RET_KEY_DIM) == (h % 2)
        qk = slice((h // 2) * LANES, (h // 2 + 1) * LANES)
        vs = slice(h * RET_VAL_DIM, (h + 1) * RET_VAL_DIM)
        inner = jnp.where(diff >= 0, jnp.exp(jnp.maximum(diff, 0.0) * lg), 0.0)
        q_decay = jnp.exp((pos + 1.0) * lg)
        k_decay = jnp.exp((chunk - 1.0 - pos) * lg)
        chunk_decay = jnp.exp(jnp.full((1, 1), chunk, F32) * lg)
        state = jnp.zeros((LANES, RET_VAL_DIM), F32)
        for c in range(s // chunk):
            rows = slice(c * chunk, (c + 1) * chunk)
            q = jnp.where(mine, q_ref[0, rows, qk], jnp.zeros((), BF16))
            k = jnp.where(mine, k_ref[0, rows, qk], jnp.zeros((), BF16))
            v = v_ref[0, rows, vs]
            scores = _dot_nt(q, k)
            inter = _dot(q, state.astype(BF16))
            kd = (k.astype(F32) * k_decay).astype(BF16)
            update = lax.dot_general(kd, v, (((0,), (0,)), ((), ())), preferred_element_type=F32)
            yield
            o = _dot((scores * inner).astype(BF16), v) + inter * q_decay
            state = state * chunk_decay + update
            yield
            o = o * lax.rsqrt(jnp.mean(o * o, axis=-1, keepdims=True) + EPS)
            g = g_ref[0, rows, vs]
            o_ref[0, rows, vs] = (o * (g * _sigmoid(g))).astype(o_ref.dtype)

    _interleave(head(h) for h in range(RET_HEADS))


def _retention(lg, rq, rk, rv, rg, chunk):
    b, s, _ = rq.shape
    kern = functools.partial(_ret_kernel, chunk=chunk)
    qk_spec = pl.BlockSpec((1, s, RET_QK_W), lambda i, lg_ref: (i, 0, 0))
    v_spec = pl.BlockSpec((1, s, RET_V_W), lambda i, lg_ref: (i, 0, 0))
    return pl.pallas_call(
        kern,
        grid_spec=pltpu.PrefetchScalarGridSpec(
            num_scalar_prefetch=1,
            grid=(b,),
            in_specs=[qk_spec, qk_spec, v_spec, v_spec],
            out_specs=v_spec,
        ),
        out_shape=jax.ShapeDtypeStruct((b, s, RET_V_W), BF16),
        compiler_params=_params(1),
        name="retention",
    )(lg, rq, rk, rv, rg)


MERGE_SUB = 256


def _merge_kernel(x_ref, fox_ref, ro_ref, gmix_ref, wmg_ref, bmg_ref, wb_ref, wout_ref, gffn_ref,
                  wr_ref, br_ref, h1_ref, hn_ref, rf_ref, ri_ref, cnt_ref, base_scr):
    i = pl.program_id(0)
    tm, d = x_ref.shape
    sub = min(MERGE_SUB, tm)

    @pl.when(i == 0)
    def _():
        base_scr[...] = jnp.zeros(base_scr.shape, F32)

    lane = lax.broadcasted_iota(jnp.int32, (sub, LANES), 1)
    lane_f = lane.astype(F32)
    r = lax.broadcasted_iota(jnp.int32, (sub, sub), 0)
    c = lax.broadcasted_iota(jnp.int32, (sub, sub), 1)
    tri = jnp.where(c < r, 1.0, 0.0).astype(BF16)
    counts = [base_scr[...]]

    def sub_tile(t):
        rows = slice(t * sub, (t + 1) * sub)
        x = x_ref[rows, :]
        pf = _dot(fox_ref[rows, :], wb_ref[0])
        pr = _dot(ro_ref[rows, :], wb_ref[1])
        xn = _rms(x, gmix_ref[...]).astype(BF16)
        yield
        pre = _dot(xn, wmg_ref[...])
        yield
        gate = _sigmoid(pre + bmg_ref[...])
        yield
        merged = (gate[:, :d] * pf + gate[:, d:] * pr).astype(BF16)
        yield
        h1 = x + _dot(merged, wout_ref[...])
        yield
        h1_ref[rows, :] = h1
        hn = _rms(h1, gffn_ref[...]).astype(BF16)
        _store_chunks(hn_ref, _pack_pairs(hn), rows)
        yield
        logits = _dot(hn, wr_ref[...]) + br_ref[...]
        yield
        vals, idxs = [], []
        cur = logits
        for _ in range(TOP_K):
            m = jnp.max(cur, axis=-1, keepdims=True)
            idx = jnp.min(jnp.where(cur == m, lane_f, float(LANES)), axis=-1, keepdims=True)
            vals.append(m)
            idxs.append(idx)
            cur = jnp.where(lane_f == idx, -jnp.inf, cur)
        exps = [jnp.exp(v - vals[0]) for v in vals]
        denom = exps[0] + exps[1] + exps[2] + exps[3]
        onehot = jnp.zeros(logits.shape, F32)
        for idx in idxs:
            onehot = onehot + jnp.where(lane_f == idx, 1.0, 0.0)
        yield
        assert len(counts) == t + 1
        before = _dot(tri, onehot.astype(BF16)) + counts[t]
        counts.append(counts[t] + jnp.sum(onehot, axis=0, keepdims=True))
        rf = jnp.zeros(logits.shape, F32)
        ri = jnp.zeros(logits.shape, F32)
        for j in range(TOP_K):
            rank = jnp.sum(jnp.where(lane_f == idxs[j], before, 0.0), axis=-1, keepdims=True)
            rf = jnp.where(lane == j, exps[j] / denom, rf)
            ri = jnp.where(lane == j, idxs[j], ri)
            ri = jnp.where(lane == TOP_K + j, rank, ri)
        rf_ref[rows, :] = rf[:, :2 * TOP_K]
        ri_ref[rows, :] = ri[:, :2 * TOP_K].astype(jnp.int32)

    _interleave(sub_tile(t) for t in range(tm // sub))
    base_scr[...] = counts[-1]
    cnt_ref[...] = counts[-1]


def _merge(x2, fox, ro, gmix, wmg, bmg, wb, wout, gffn, wr, br, tm):
    n, d = x2.shape
    row = lambda i: (i, 0)
    const = lambda i: (0, 0)
    return pl.pallas_call(
        _merge_kernel,
        grid=(n // tm,),
        in_specs=[
            pl.BlockSpec((tm, d), row), pl.BlockSpec((tm, FOX_W), row), pl.BlockSpec((tm, RET_V_W), row),
            pl.BlockSpec((1, d), const), _resident((d, 2 * d)), pl.BlockSpec((1, 2 * d), const),
            _resident((2, FOX_W, d)), _resident((d, d)),
            pl.BlockSpec((1, d), const), _resident((d, LANES)), pl.BlockSpec((1, LANES), const),
        ],
        out_specs=[
            pl.BlockSpec((tm, d), row), pl.BlockSpec((d // 2 // SC_COLS, tm, SC_COLS), lambda i: (0, i, 0)),
            pl.BlockSpec((tm, 2 * TOP_K), row), pl.BlockSpec((tm, 2 * TOP_K), row),
            pl.BlockSpec((1, LANES), const),
        ],
        out_shape=[
            jax.ShapeDtypeStruct((n, d), F32), jax.ShapeDtypeStruct((d // 2 // SC_COLS, n, SC_COLS), jnp.int32),
            jax.ShapeDtypeStruct((n, 2 * TOP_K), F32), jax.ShapeDtypeStruct((n, 2 * TOP_K), jnp.int32),
            jax.ShapeDtypeStruct((1, LANES), F32),
        ],
        scratch_shapes=[pltpu.VMEM((1, LANES), F32)],
        compiler_params=_params(1),
        name="merge_router",
    )(x2, fox, ro, gmix, wmg, bmg, wb, wout, gffn, wr, br)


def _sc_mesh():
    return plsc.VectorSubcoreMesh(core_axis_name="core", subcore_axis_name="subcore")


def _sc_dispatch(rows, dest_t, n_out):
    chunks, n, w = rows.shape

    @functools.partial(pl.kernel, out_type=jax.ShapeDtypeStruct((chunks, n_out, w), rows.dtype),
                       mesh=_sc_mesh(), scratch_types=[], name="moe_dispatch")
    def run(x_hbm, i_hbm, o_hbm):
        for c in range(chunks):
            def body(x_vmem, i_vmem, c=c):
                for j in range(TOP_K):
                    pltpu.sync_copy(x_vmem, o_hbm.at[c].at[i_vmem.at[j]])

            pltpu.emit_pipeline(
                body,
                grid=(n // SC_WINDOW,),
                in_specs=[pl.BlockSpec((SC_WINDOW, w), lambda i: (i, 0)),
                          pl.BlockSpec((TOP_K, SC_WINDOW), lambda i: (0, i))],
                out_specs=[],
                core_axis_name=("core", "subcore"),
                dimension_semantics=(pltpu.PARALLEL,),
            )(x_hbm.at[c], i_hbm)

    return run(rows, dest_t)


def _sc_gather(table, idx):
    chunks, _, w = table.shape
    m = idx.shape[1]

    @functools.partial(pl.kernel, out_type=jax.ShapeDtypeStruct((chunks, m, w), table.dtype),
                       mesh=_sc_mesh(), scratch_types=[], name="moe_combine_gather")
    def run(t_hbm, i_hbm, o_hbm):
        for c in range(chunks):
            def body(i_vmem, o_vmem, c=c):
                pltpu.sync_copy(t_hbm.at[c].at[i_vmem.at[0]], o_vmem)

            pltpu.emit_pipeline(
                body,
                grid=(m // SC_WINDOW,),
                in_specs=[pl.BlockSpec((1, SC_WINDOW), lambda i: (0, i))],
                out_specs=[pl.BlockSpec((SC_WINDOW, w), lambda i: (i, 0))],
                core_axis_name=("core", "subcore"),
                dimension_semantics=(pltpu.PARALLEL,),
            )(i_hbm, o_hbm.at[c])

    return run(table, idx)


GLU_GROUP = 2 * LANES
EXPERT_SUB = 256


def _expert_kernel(be_ref, valid_ref, x_ref, wgu_ref, bgu_ref, wd_ref, bd_ref, perm_ref, y_ref, wgu_scr, wd_scr):
    i = pl.program_id(0)
    f2 = wgu_ref.shape[2]
    bm = x_ref.shape[1]
    sub = min(EXPERT_SUB, bm)
    valid = valid_ref[i]
    live = valid > 0

    @pl.when(jnp.logical_and(live, jnp.logical_or(i == 0, be_ref[i] != be_ref[jnp.maximum(i - 1, 0)])))
    def _():
        for b in range(f2 // GLU_GROUP):
            cols = slice(b * GLU_GROUP, (b + 1) * GLU_GROUP)
            wgu_scr[:, cols] = _dot(wgu_ref[0, :, cols].astype(BF16), perm_ref[...]).astype(BF16)
        wd_scr[...] = wd_ref[0].astype(BF16)

    def sub_block(t, sub):
        rows = slice(t * sub, (t + 1) * sub)
        x = _unpack_pairs([x_ref[c, rows, :] for c in range(x_ref.shape[0])]).astype(BF16)
        yield
        gu = _dot(x, wgu_scr[...])
        yield
        gu = gu + bgu_ref[0]
        acts = []
        for b in range(f2 // GLU_GROUP):
            glu = jnp.minimum(gu[:, b * GLU_GROUP:b * GLU_GROUP + LANES], SWIGLU_LIMIT)
            lin = jnp.clip(gu[:, b * GLU_GROUP + LANES:(b + 1) * GLU_GROUP], -SWIGLU_LIMIT, SWIGLU_LIMIT)
            acts.append((glu * _sigmoid(SWIGLU_ALPHA * glu) * (lin + 1.0)).astype(BF16))
        act = jnp.concatenate(acts, axis=1)
        yield
        y = _dot(act, wd_scr[...])
        yield
        _store_chunks(y_ref, _pack_pairs(y + bd_ref[0]), rows)

    @pl.when(valid == bm)
    def _():
        _interleave(sub_block(t, sub) for t in range(bm // sub))

    for t in range(bm // sub):
        @pl.when(jnp.logical_and(valid < bm, valid > bm - (t + 1) * sub))
        def _(t=t):
            _interleave([sub_block(t, sub)])


def _experts(block_e, valid, x_buf, wgu, bgu, wd, bd, bm):
    p = x_buf.shape[1]
    f, d = wd.shape[1:]
    perm = np.zeros((GLU_GROUP, GLU_GROUP), np.float32)
    for c in range(LANES):
        perm[2 * c, c] = 1.0
        perm[2 * c + 1, LANES + c] = 1.0
    return pl.pallas_call(
        _expert_kernel,
        grid_spec=pltpu.PrefetchScalarGridSpec(
            num_scalar_prefetch=2,
            grid=(p // bm,),
            in_specs=[
                pl.BlockSpec((d // 2 // SC_COLS, bm, SC_COLS), lambda i, be, nv: (0, i, 0)),
                pl.BlockSpec((1, d, 2 * f), lambda i, be, nv: (be[i], 0, 0)),
                pl.BlockSpec((1, 1, 2 * f), lambda i, be, nv: (be[i], 0, 0)),
                pl.BlockSpec((1, f, d), lambda i, be, nv: (be[i], 0, 0)),
                pl.BlockSpec((1, 1, d), lambda i, be, nv: (be[i], 0, 0)),
                pl.BlockSpec((GLU_GROUP, GLU_GROUP), lambda i, be, nv: (0, 0)),
            ],
            out_specs=pl.BlockSpec((d // 2 // SC_COLS, bm, SC_COLS), lambda i, be, nv: (0, i, 0)),
            scratch_shapes=[pltpu.VMEM((d, 2 * f), BF16), pltpu.VMEM((f, d), BF16)],
        ),
        out_shape=jax.ShapeDtypeStruct((d // 2 // SC_COLS, p, SC_COLS), jnp.int32),
        compiler_params=_params(1),
        name="expert_ffn",
    )(block_e, valid, x_buf, wgu, bgu, wd, bd, jnp.asarray(perm, BF16))


FINAL_SUB = 256


def _final_kernel(h1_ref, yg_ref, rf_ref, p_ref, gple_ref, wpg_ref, wpp_ref, gfin_ref, *rest):
    o_ref = rest[-1]
    tm = h1_ref.shape[0]
    sub = min(FINAL_SUB, tm)

    def sub_tile(t):
        rows = slice(t * sub, (t + 1) * sub)
        rf = rf_ref[rows, :]
        h2 = h1_ref[rows, :]
        for j in range(TOP_K):
            h2 = h2 + _unpack_pairs([yg_ref[c, j, rows, :] for c in range(yg_ref.shape[0])]) * rf[:, j:j + 1]
        hn = _rms(h2, gple_ref[...]).astype(BF16)
        yield
        pre = _dot(hn, wpg_ref[...])
        proj = _dot(p_ref[rows, :].astype(BF16), wpp_ref[...])
        yield
        h3 = h2 + _sigmoid(pre) * proj
        o_ref[rows, :] = _rms(h3, gfin_ref[...])

    _interleave(sub_tile(t) for t in range(tm // sub))


def _final(h1, yg, rf, p2, gple, wpg, wpp, gfin, tm, first_tile, prev_out):
    n_total, d = h1.shape
    n = yg.shape[2]
    pd = p2.shape[1]
    dst = lambda i: (i + first_tile, 0)
    const = lambda i: (0, 0)
    in_specs = [
        pl.BlockSpec((tm, d), dst), pl.BlockSpec((d // 2 // SC_COLS, TOP_K, tm, SC_COLS), lambda i: (0, 0, i, 0)),
        pl.BlockSpec((tm, 2 * TOP_K), dst), pl.BlockSpec((tm, pd), dst),
        pl.BlockSpec((1, d), const), pl.BlockSpec((d, d), const), pl.BlockSpec((pd, d), const),
        pl.BlockSpec((1, d), const),
    ]
    args = [h1, yg, rf, p2, gple, wpg, wpp, gfin]
    aliases = {}
    if prev_out is not None:
        in_specs.append(pl.BlockSpec(memory_space=pl.ANY))
        aliases = {len(args): 0}
        args.append(prev_out)
    return pl.pallas_call(
        _final_kernel,
        grid=(n // tm,),
        in_specs=in_specs,
        out_specs=pl.BlockSpec((tm, d), dst),
        out_shape=jax.ShapeDtypeStruct((n_total, d), F32),
        input_output_aliases=aliases,
        compiler_params=_params(1),
        name="combine_ple_norm",
    )(*args)


def _layer(h, p, mix_norm, w_in, b_forget, w_branch, w_merge_gate, b_merge_gate, w_out, ffn_norm,
           w_router, b_router, w_gate_up, b_gate_up, w_down, b_down, ple_norm, w_ple_gate, w_ple_proj,
           final_norm, *, tm, tq, chunk, bm, groups):
    b, s, d = h.shape
    n = b * s
    x2 = h.reshape(n, d)
    row = lambda t: t.reshape(1, -1)

    c0 = 3 * FOX_W
    w_main = jnp.concatenate([w_in[:, :2 * FOX_W], w_in[:, c0 + FOX_HEADS:]], axis=1).astype(BF16)
    w_vt = w_in[:, 2 * FOX_W:c0].T.astype(BF16)
    w_f = jnp.pad(w_in[:, c0:c0 + FOX_HEADS], ((0, 0), (0, LANES - FOX_HEADS))).astype(BF16)
    b_f = jnp.pad(b_forget, (0, LANES - FOX_HEADS)).reshape(1, LANES)
    half = RET_KEY_DIM // 2
    inv = ROPE_BASE ** (-jnp.arange(half, dtype=F32) / half)
    ang = jnp.arange(s).astype(F32)[:, None] * inv[None, :]
    cos_t = jnp.tile(jnp.cos(ang), (1, RET_QK_W // half))
    sin_t = jnp.tile(jnp.sin(ang), (1, RET_QK_W // half))

    q_aug, k_aug, v_t, rq, rk, rv, rg = _inproj(x2, row(mix_norm), w_main, w_vt, w_f, b_f, cos_t, sin_t, s, tm)
    fox = _fox(q_aug, k_aug, v_t, b, s, tq)

    lg = jnp.log1p(-jnp.exp2(-5.0 - jnp.arange(RET_HEADS, dtype=F32)))
    ro = _retention(lg, rq.reshape(b, s, RET_QK_W), rk.reshape(b, s, RET_QK_W),
                    rv.reshape(b, s, RET_V_W), rg.reshape(b, s, RET_V_W), chunk)

    w_r = jnp.pad(w_router, ((0, 0), (0, LANES - N_EXPERTS))).astype(BF16)
    b_r = jnp.concatenate([b_router, jnp.full((LANES - N_EXPERTS,), -jnp.inf, F32)]).reshape(1, LANES)
    ne, dd, f2 = w_gate_up.shape
    bgu = b_gate_up.reshape(ne, f2 // GLU_GROUP, LANES, 2).transpose(0, 1, 3, 2).reshape(ne, 1, f2)
    merge_w = (row(mix_norm), w_merge_gate.astype(BF16), row(b_merge_gate), w_branch.astype(BF16),
               w_out.astype(BF16), row(ffn_norm), w_r, b_r)
    final_w = (row(ple_norm), w_ple_gate.astype(BF16), w_ple_proj.astype(BF16), row(final_norm))
    p2 = p.reshape(n, -1)

    h1, hn, rf, ri, cnt = _merge(x2, fox, ro.reshape(n, RET_V_W), *merge_w, tm)

    a = n * TOP_K
    nb = -(-(a + N_EXPERTS * (bm - 1)) // bm)
    ri_t = ri.T
    counts = cnt[0, :N_EXPERTS].astype(jnp.int32)
    padded = (counts + bm - 1) // bm * bm
    pad_end = jnp.cumsum(padded)
    first_row = pad_end - counts
    experts = jnp.arange(N_EXPERTS, dtype=jnp.int32)

    def lookup(ids):
        return jnp.sum(jnp.where(ids[None] == experts.reshape((-1,) + (1,) * ids.ndim),
                                 first_row.reshape((-1,) + (1,) * ids.ndim), 0), axis=0)

    dest_t = lookup(ri_t[:TOP_K]) + ri_t[TOP_K:]
    block_start = jnp.arange(nb, dtype=jnp.int32) * bm
    block_e = jnp.minimum(jnp.sum(block_start[:, None] >= pad_end[None, :], axis=1),
                          N_EXPERTS - 1).astype(jnp.int32)
    valid = jnp.where(block_start < pad_end[-1],
                      jnp.clip(block_start + bm - lookup(block_e), 0, bm), 0).astype(jnp.int32)

    x_buf = _sc_dispatch(hn, dest_t, nb * bm)
    y_buf = _experts(block_e, valid, x_buf, w_gate_up, bgu, w_down, b_down.reshape(ne, 1, dd), bm)

    shares = [1] + [2 ** max(g - 1, 0) for g in range(1, groups)]
    unit = n // sum(shares)
    out = None
    start = 0
    for share in shares:
        ng = share * unit
        idx = dest_t[:, start:start + ng].reshape(1, ng * TOP_K)
        yg = _sc_gather(y_buf, idx).reshape(d // 2 // SC_COLS, TOP_K, ng, SC_COLS)
        out = _final(h1, yg, rf, p2, *final_w, tm, start // tm, out)
        start += ng
    return out.reshape(b, s, d)


def kernel(x, p, mix_norm, w_in, b_forget, w_branch, w_merge_gate, b_merge_gate, w_out, ffn_norm, w_router,
           b_router, w_gate_up, b_gate_up, w_down, b_down, ple_norm, w_ple_gate, w_ple_proj, final_norm):
    depth = p.shape[0]
    assert depth == 1, "the final norm is fused into the (single) layer"
    return _layer(x, p[0], mix_norm[0], w_in[0], b_forget[0], w_branch[0], w_merge_gate[0], b_merge_gate[0],
                  w_out[0], ffn_norm[0], w_router[0], b_router[0], w_gate_up[0], b_gate_up[0], w_down[0],
                  b_down[0], ple_norm[0], w_ple_gate[0], w_ple_proj[0], final_norm,
                  tm=1024, tq=512, chunk=128, bm=1024, groups=5)
```

```python
import functools

import numpy as np
import jax
import jax.numpy as jnp
from jax import lax
from jax.experimental import pallas as pl
from jax.experimental.pallas import tpu as pltpu
from jax.experimental.pallas import tpu_sc as plsc

FOX_HEADS = 8
FOX_HEAD_DIM = 64
RET_HEADS = 4
RET_KEY_DIM = 64
RET_VAL_DIM = 128
ROPE_BASE = 10000.0
N_EXPERTS = 32
TOP_K = 4
SWIGLU_LIMIT = 7.0
SWIGLU_ALPHA = 1.702
EPS = 1e-6

LANES = 128
FOX_W = FOX_HEADS * FOX_HEAD_DIM
RET_QK_W = RET_HEADS * RET_KEY_DIM
RET_V_W = RET_HEADS * RET_VAL_DIM
VMEM_LIMIT = 56 * 1024 * 1024
HIGH_HALF = -65536
LOG2_E = 1.4426950408889634
SC_WINDOW = 128
SC_COLS = 256

F32 = jnp.float32
BF16 = jnp.bfloat16


def _rms(x, g):
    return x * lax.rsqrt(jnp.mean(x * x, axis=-1, keepdims=True) + EPS) * g


def _sigmoid(x):
    return 1.0 / (1.0 + jnp.exp(-x))


def _dot(a, b):
    return jnp.dot(a, b, preferred_element_type=F32)


def _dot_nt(a, b):
    return lax.dot_general(a, b, (((1,), (1,)), ((), ())), preferred_element_type=F32)


def _pack_pairs(v):
    bits = pltpu.bitcast(v.astype(BF16).astype(F32), jnp.int32)
    half = v.shape[1] // 2
    return lax.shift_right_logical(bits[:, :half], 16) | (bits[:, half:] & HIGH_HALF)


def _unpack_pairs(chunks):
    return jnp.concatenate([pltpu.bitcast(lax.shift_left(w, 16), F32) for w in chunks]
                           + [pltpu.bitcast(w & HIGH_HALF, F32) for w in chunks], axis=1)


def _store_chunks(ref, v, rows=slice(None)):
    for c in range(ref.shape[0]):
        ref[c, rows, :] = v[:, c * SC_COLS:(c + 1) * SC_COLS]


def _interleave(generators):
    live = []
    pending = list(generators)
    while pending or live:
        if pending:
            live.append(pending.pop(0))
        for g in list(live):
            try:
                next(g)
            except StopIteration:
                live.remove(g)


def _resident(shape):
    return pl.BlockSpec(shape, lambda *_: (0,) * len(shape), pipeline_mode=pl.Buffered(1))


def _params(n_axes):
    return pltpu.CompilerParams(dimension_semantics=("arbitrary",) * n_axes,
                                vmem_limit_bytes=VMEM_LIMIT)


C_MID = FOX_HEADS
C_LO = 2 * FOX_HEADS
C_ONE = 3 * FOX_HEADS
N_AUG = 3
AUG_W = 2 * N_AUG
K_AUG0 = FOX_HEADS * AUG_W


def _split3(t):
    hi = t.astype(BF16).astype(F32)
    r = t - hi
    mid = r.astype(BF16).astype(F32)
    lo = (r - mid).astype(BF16).astype(F32)
    return hi + pltpu.roll(mid, C_MID, 1) + pltpu.roll(lo, C_LO, 1)


INPROJ_SUB = 256


def _inproj_kernel(x_ref, g_ref, w_ref, wvt_ref, wf_ref, bf_ref, cos_ref, sin_ref, place_ref,
                   q_ref, k_ref, vt_ref, rq_ref, rk_ref, rv_ref, rg_ref, carry_scr, *, tiles_per_seq):
    i = pl.program_id(0)
    tm = x_ref.shape[0]
    sub = min(INPROJ_SUB, tm)

    @pl.when(i % tiles_per_seq == 0)
    def _():
        carry_scr[...] = jnp.zeros(carry_scr.shape, F32)

    lane = lax.broadcasted_iota(jnp.int32, (sub, LANES), 1)
    r = lax.broadcasted_iota(jnp.int32, (sub, sub), 0)
    c = lax.broadcasted_iota(jnp.int32, (sub, sub), 1)
    tri = jnp.where(c <= r, 1.0, 0.0).astype(BF16)
    own = lane < FOX_HEAD_DIM
    is_aug = jnp.logical_and(lane >= FOX_HEAD_DIM, lane < FOX_HEAD_DIM + AUG_W)
    half = RET_KEY_DIM // 2
    lane_r = lax.broadcasted_iota(jnp.int32, (sub, RET_QK_W), 1)
    first = (lane_r % RET_KEY_DIM) < half
    carries = [carry_scr[...]]

    def sub_tile(t):
        rows = slice(t * sub, (t + 1) * sub)
        xn = _rms(x_ref[rows, :], g_ref[...]).astype(BF16)
        yield
        u = _dot(xn, w_ref[...])
        vt_ref[:, rows] = _dot_nt(wvt_ref[...], xn).astype(BF16)
        z = _dot(xn, wf_ref[...]) + bf_ref[...]
        yield
        lf = jnp.where(lane < FOX_HEADS, jnp.minimum(z, 0.0) - jnp.log1p(jnp.exp(-jnp.abs(z))), 0.0)
        ps = _dot(tri, _split3(lf).astype(BF16))
        yield
        assert len(carries) == t + 1
        cum = ps + pltpu.roll(ps, LANES - C_MID, 1) + pltpu.roll(ps, LANES - C_LO, 1)
        cum = jnp.where(lane < FOX_HEADS, cum, 0.0) + carries[t]
        carries.append(cum[sub - 1:sub, :])
        c3 = (_split3(cum * LOG2_E) + jnp.where(lane == C_ONE, 1.0, 0.0)).astype(BF16)
        aug = _dot(c3, place_ref[...])

        o = 2 * FOX_W
        rq = u[:, o:o + RET_QK_W]; o += RET_QK_W
        rk = u[:, o:o + RET_QK_W]; o += RET_QK_W
        rv_ref[rows, :] = u[:, o:o + RET_V_W].astype(BF16); o += RET_V_W
        rg_ref[rows, :] = u[:, o:o + RET_V_W]

        cos = cos_ref[rows, :]
        sin = sin_ref[rows, :]

        def rot(v):
            partner = jnp.where(first, -pltpu.roll(v, RET_QK_W - half, 1), pltpu.roll(v, half, 1))
            return v * cos + partner * sin

        rq_ref[rows, :] = rot(rq).astype(BF16)
        rk_ref[rows, :] = (rot(rk) * (RET_KEY_DIM ** -0.5)).astype(BF16)
        yield
        fq = u[:, 0:FOX_W] * (FOX_HEAD_DIM ** -0.5 * LOG2_E)
        fk = u[:, FOX_W:2 * FOX_W]
        for src, aug0, dst in ((fq, 0, q_ref), (fk, K_AUG0, k_ref)):
            for h in range(FOX_HEADS):
                blk = src[:, (h // 2) * LANES:(h // 2 + 1) * LANES]
                if h % 2:
                    blk = pltpu.roll(blk, FOX_HEAD_DIM, 1)
                mine = pltpu.roll(aug, (FOX_HEAD_DIM - aug0 - AUG_W * h) % LANES, 1)
                slab = jnp.where(own, blk, jnp.where(is_aug, mine, 0.0))
                dst[rows, h * LANES:(h + 1) * LANES] = slab.astype(BF16)

    _interleave(sub_tile(t) for t in range(tm // sub))
    carry_scr[...] = carries[-1]


def _placement():
    place = np.zeros((LANES, LANES), np.float32)
    for h in range(FOX_HEADS):
        for part, src in enumerate((h, C_MID + h, C_LO + h)):
            place[C_ONE, AUG_W * h + part] = 1.0
            place[src, AUG_W * h + N_AUG + part] = 1.0
            place[src, K_AUG0 + AUG_W * h + part] = -1.0
            place[C_ONE, K_AUG0 + AUG_W * h + N_AUG + part] = 1.0
    return jnp.asarray(place, BF16)


def _inproj(x2, g, w_main, w_vt, w_f, b_f, cos_t, sin_t, seq, tm):
    n, d = x2.shape
    wn = w_main.shape[1]
    spt = seq // tm
    aw = FOX_HEADS * LANES
    place = _placement()
    row = lambda i: (i, 0)
    const = lambda i: (0, 0)
    pos = lambda i: (i % spt, 0)
    out_shape = [
        jax.ShapeDtypeStruct((n, aw), BF16), jax.ShapeDtypeStruct((n, aw), BF16),
        jax.ShapeDtypeStruct((FOX_W, n), BF16),
        jax.ShapeDtypeStruct((n, RET_QK_W), BF16), jax.ShapeDtypeStruct((n, RET_QK_W), BF16),
        jax.ShapeDtypeStruct((n, RET_V_W), BF16), jax.ShapeDtypeStruct((n, RET_V_W), F32),
    ]
    out_specs = [
        pl.BlockSpec((tm, aw), row), pl.BlockSpec((tm, aw), row),
        pl.BlockSpec((FOX_W, tm), lambda i: (0, i)),
        pl.BlockSpec((tm, RET_QK_W), row), pl.BlockSpec((tm, RET_QK_W), row),
        pl.BlockSpec((tm, RET_V_W), row), pl.BlockSpec((tm, RET_V_W), row),
    ]
    return pl.pallas_call(
        functools.partial(_inproj_kernel, tiles_per_seq=spt),
        grid=(n // tm,),
        in_specs=[
            pl.BlockSpec((tm, d), row), pl.BlockSpec((1, d), const),
            _resident((d, wn)), _resident((FOX_W, d)),
            _resident((d, LANES)), pl.BlockSpec((1, LANES), const),
            pl.BlockSpec((tm, RET_QK_W), pos), pl.BlockSpec((tm, RET_QK_W), pos),
            _resident((LANES, LANES)),
        ],
        out_specs=out_specs,
        out_shape=out_shape,
        scratch_shapes=[pltpu.VMEM((1, LANES), F32)],
        compiler_params=_params(1),
        name="inproj",
    )(x2, g, w_main, w_vt, w_f, b_f, cos_t, sin_t, place)


FOX_AHEAD = 3


def _fox_items(q_ref, k_ref, vt_ref, o_ref, tq):
    seq = q_ref.shape[0]
    half = tq // 2
    vrow = lax.broadcasted_iota(jnp.int32, (LANES, tq), 0)
    one = jnp.ones((), BF16)
    items = [(qi, j) for qi in range(seq // tq) for j in range(qi + 1)]

    def logits(qi, j):
        out = []
        for a in range(2):
            sl = slice(a * LANES, (a + 1) * LANES)
            kk = k_ref[j * tq:(j + 1) * tq, sl]
            qq = q_ref[qi * tq:(qi + 1) * tq, sl]
            if j != qi:
                out.append([(0, tq, _dot_nt(kk, qq))])
                continue
            parts = []
            for c0, nk in ((0, half), (half, tq)):
                sc = _dot_nt(kk[:nk], qq[c0:c0 + half])
                key = lax.broadcasted_iota(jnp.int32, sc.shape, 0)
                qry = lax.broadcasted_iota(jnp.int32, sc.shape, 1) + c0
                parts.append((c0, nk, jnp.where(key <= qry, sc, -jnp.inf)))
            out.append(parts)
        return out

    queue = [logits(*it) for it in items[:FOX_AHEAD]]
    carry = None
    for w, (qi, j) in enumerate(items):
        if w + FOX_AHEAD < len(items):
            queue.append(logits(*items[w + FOX_AHEAD]))
        s_cur = queue.pop(0)
        if j == 0:
            carry = [(jnp.full((1, tq), -jnp.inf, F32), jnp.zeros((LANES, tq), F32)) for _ in range(2)]
        v = vt_ref[:, j * tq:(j + 1) * tq]
        for a in range(2):
            m_prev, acc = carry[a]
            va = jnp.where((vrow // FOX_HEAD_DIM) == a, v, one)
            ms, accs = [], []
            for c0, nk, sc in s_cur[a]:
                cols = slice(c0, c0 + sc.shape[1])
                m_new = jnp.maximum(m_prev[:, cols], jnp.max(sc, axis=0, keepdims=True))
                alpha = jnp.exp2(m_prev[:, cols] - m_new)
                p = jnp.exp2(sc - m_new).astype(BF16)
                ms.append(m_new)
                accs.append(alpha * acc[:, cols] + _dot(va[:, :nk], p))
            carry[a] = (ms[0], accs[0]) if len(ms) == 1 else (jnp.concatenate(ms, axis=1),
                                                               jnp.concatenate(accs, axis=1))
        if j == qi:
            acc0, acc1 = carry[0][1], carry[1][1]
            ot = jnp.where(vrow < FOX_HEAD_DIM, acc0 / acc0[FOX_HEAD_DIM:FOX_HEAD_DIM + 1, :],
                           acc1 / acc1[0:1, :])
            o_ref[qi * tq:(qi + 1) * tq, :] = ot.T.astype(o_ref.dtype)
        yield


def _ret_head(lg, parity, q_ref, k_ref, v_ref, g_ref, o_ref, chunk):
    s = q_ref.shape[1]
    lane = lax.broadcasted_iota(jnp.int32, (chunk, LANES), 1)
    ri = lax.broadcasted_iota(jnp.int32, (chunk, chunk), 0)
    ci = lax.broadcasted_iota(jnp.int32, (chunk, chunk), 1)
    diff = (ri - ci).astype(F32)
    pos = lax.broadcasted_iota(jnp.int32, (chunk, 1), 0).astype(F32)
    mine = (lane // RET_KEY_DIM) == parity
    inner = jnp.where(diff >= 0, jnp.exp(jnp.maximum(diff, 0.0) * lg), 0.0)
    q_decay = jnp.exp((pos + 1.0) * lg)
    k_decay = jnp.exp((chunk - 1.0 - pos) * lg)
    chunk_decay = jnp.exp(jnp.full((1, 1), chunk, F32) * lg)
    state = jnp.zeros((LANES, RET_VAL_DIM), F32)
    for c in range(s // chunk):
        rows = slice(c * chunk, (c + 1) * chunk)
        q = jnp.where(mine, q_ref[0, rows, :], jnp.zeros((), BF16))
        k = jnp.where(mine, k_ref[0, rows, :], jnp.zeros((), BF16))
        v = v_ref[0, rows, :]
        scores = _dot_nt(q, k)
        inter = _dot(q, state.astype(BF16))
        kd = (k.astype(F32) * k_decay).astype(BF16)
        update = lax.dot_general(kd, v, (((0,), (0,)), ((), ())), preferred_element_type=F32)
        o = _dot((scores * inner).astype(BF16), v) + inter * q_decay
        state = state * chunk_decay + update
        o = o * lax.rsqrt(jnp.mean(o * o, axis=-1, keepdims=True) + EPS)
        g = g_ref[0, rows, :]
        o_ref[0, rows, :] = (o * (g * _sigmoid(g))).astype(o_ref.dtype)
        if c % 2:
            yield


def _mixers_kernel(lg_ref, q_ref, k_ref, vt_ref, rq_ref, rk_ref, rv_ref, rg_ref, fox_ref, ro_ref, *, tq, chunk):
    h = pl.program_id(1)
    _interleave([_fox_items(q_ref, k_ref, vt_ref, fox_ref, tq),
                 _ret_head(lg_ref[h], h % 2, rq_ref, rk_ref, rv_ref, rg_ref, ro_ref, chunk)])


def _mixers(lg, q_aug, k_aug, v_t, rq, rk, rv, rg, batch, seq, tq, chunk):
    n = q_aug.shape[0]
    assert FOX_HEADS // 2 == RET_HEADS
    kern = functools.partial(_mixers_kernel, tq=tq, chunk=chunk)
    qk_spec = pl.BlockSpec((1, seq, LANES), lambda i, h, lg_ref: (i, 0, h // 2))
    v_spec = pl.BlockSpec((1, seq, LANES), lambda i, h, lg_ref: (i, 0, h))
    return pl.pallas_call(
        kern,
        grid_spec=pltpu.PrefetchScalarGridSpec(
            num_scalar_prefetch=1,
            grid=(batch, RET_HEADS),
            in_specs=[
                pl.BlockSpec((seq, 2 * LANES), lambda i, h, lg_ref: (i, h)),
                pl.BlockSpec((seq, 2 * LANES), lambda i, h, lg_ref: (i, h)),
                pl.BlockSpec((LANES, seq), lambda i, h, lg_ref: (h, i)),
                qk_spec, qk_spec, v_spec, v_spec,
            ],
            out_specs=[pl.BlockSpec((seq, LANES), lambda i, h, lg_ref: (i, h)), v_spec],
        ),
        out_shape=[jax.ShapeDtypeStruct((n, FOX_W), BF16), jax.ShapeDtypeStruct((batch, seq, RET_V_W), BF16)],
        compiler_params=_params(2),
        name="fox_and_retention",
    )(lg, q_aug, k_aug, v_t, rq, rk, rv, rg)


MERGE_SUB = 256


def _merge_kernel(x_ref, fox_ref, ro_ref, gmix_ref, wmg_ref, bmg_ref, wb_ref, wout_ref, gffn_ref,
                  wr_ref, br_ref, h1_ref, hn_ref, rf_ref, ri_ref, cnt_ref, base_scr):
    i = pl.program_id(0)
    tm, d = x_ref.shape
    sub = min(MERGE_SUB, tm)

    @pl.when(i == 0)
    def _():
        base_scr[...] = jnp.zeros(base_scr.shape, F32)

    lane = lax.broadcasted_iota(jnp.int32, (sub, LANES), 1)
    lane_f = lane.astype(F32)
    r = lax.broadcasted_iota(jnp.int32, (sub, sub), 0)
    c = lax.broadcasted_iota(jnp.int32, (sub, sub), 1)
    tri = jnp.where(c < r, 1.0, 0.0).astype(BF16)
    counts = [base_scr[...]]

    def sub_tile(t):
        rows = slice(t * sub, (t + 1) * sub)
        x = x_ref[rows, :]
        pf = _dot(fox_ref[rows, :], wb_ref[0])
        pr = _dot(ro_ref[rows, :], wb_ref[1])
        xn = _rms(x, gmix_ref[...]).astype(BF16)
        yield
        pre = _dot(xn, wmg_ref[...])
        yield
        gate = _sigmoid(pre + bmg_ref[...])
        yield
        merged = (gate[:, :d] * pf + gate[:, d:] * pr).astype(BF16)
        yield
        h1 = x + _dot(merged, wout_ref[...])
        yield
        h1_ref[rows, :] = h1
        hn = _rms(h1, gffn_ref[...]).astype(BF16)
        _store_chunks(hn_ref, _pack_pairs(hn), rows)
        yield
        logits = _dot(hn, wr_ref[...]) + br_ref[...]
        yield
        vals, idxs = [], []
        cur = logits
        for _ in range(TOP_K):
            m = jnp.max(cur, axis=-1, keepdims=True)
            idx = jnp.min(jnp.where(cur == m, lane_f, float(LANES)), axis=-1, keepdims=True)
            vals.append(m)
            idxs.append(idx)
            cur = jnp.where(lane_f == idx, -jnp.inf, cur)
        exps = [jnp.exp(v - vals[0]) for v in vals]
        denom = exps[0] + exps[1] + exps[2] + exps[3]
        onehot = jnp.zeros(logits.shape, F32)
        for idx in idxs:
            onehot = onehot + jnp.where(lane_f == idx, 1.0, 0.0)
        yield
        assert len(counts) == t + 1
        before = _dot(tri, onehot.astype(BF16)) + counts[t]
        counts.append(counts[t] + jnp.sum(onehot, axis=0, keepdims=True))
        rf = jnp.zeros(logits.shape, F32)
        ri = jnp.zeros(logits.shape, F32)
        for j in range(TOP_K):
            rank = jnp.sum(jnp.where(lane_f == idxs[j], before, 0.0), axis=-1, keepdims=True)
            rf = jnp.where(lane == j, exps[j] / denom, rf)
            ri = jnp.where(lane == j, idxs[j], ri)
            ri = jnp.where(lane == TOP_K + j, rank, ri)
        rf_ref[rows, :] = rf[:, :2 * TOP_K]
        ri_ref[rows, :] = ri[:, :2 * TOP_K].astype(jnp.int32)

    _interleave(sub_tile(t) for t in range(tm // sub))
    base_scr[...] = counts[-1]
    cnt_ref[...] = counts[-1]


def _merge(x2, fox, ro, gmix, wmg, bmg, wb, wout, gffn, wr, br, tm):
    n, d = x2.shape
    row = lambda i: (i, 0)
    const = lambda i: (0, 0)
    return pl.pallas_call(
        _merge_kernel,
        grid=(n // tm,),
        in_specs=[
            pl.BlockSpec((tm, d), row), pl.BlockSpec((tm, FOX_W), row), pl.BlockSpec((tm, RET_V_W), row),
            pl.BlockSpec((1, d), const), _resident((d, 2 * d)), pl.BlockSpec((1, 2 * d), const),
            _resident((2, FOX_W, d)), _resident((d, d)),
            pl.BlockSpec((1, d), const), _resident((d, LANES)), pl.BlockSpec((1, LANES), const),
        ],
        out_specs=[
            pl.BlockSpec((tm, d), row), pl.BlockSpec((d // 2 // SC_COLS, tm, SC_COLS), lambda i: (0, i, 0)),
            pl.BlockSpec((tm, 2 * TOP_K), row), pl.BlockSpec((tm, 2 * TOP_K), row),
            pl.BlockSpec((1, LANES), const),
        ],
        out_shape=[
            jax.ShapeDtypeStruct((n, d), F32), jax.ShapeDtypeStruct((d // 2 // SC_COLS, n, SC_COLS), jnp.int32),
            jax.ShapeDtypeStruct((n, 2 * TOP_K), F32), jax.ShapeDtypeStruct((n, 2 * TOP_K), jnp.int32),
            jax.ShapeDtypeStruct((1, LANES), F32),
        ],
        scratch_shapes=[pltpu.VMEM((1, LANES), F32)],
        compiler_params=_params(1),
        name="merge_router",
    )(x2, fox, ro, gmix, wmg, bmg, wb, wout, gffn, wr, br)


def _sc_mesh():
    return plsc.VectorSubcoreMesh(core_axis_name="core", subcore_axis_name="subcore")


def _sc_dispatch(rows, dest_t, n_out):
    chunks, n, w = rows.shape

    @functools.partial(pl.kernel, out_type=jax.ShapeDtypeStruct((chunks, n_out, w), rows.dtype),
                       mesh=_sc_mesh(), scratch_types=[], name="moe_dispatch")
    def run(x_hbm, i_hbm, o_hbm):
        for c in range(chunks):
            def body(x_vmem, i_vmem, c=c):
                for j in range(TOP_K):
                    pltpu.sync_copy(x_vmem, o_hbm.at[c].at[i_vmem.at[j]])

            pltpu.emit_pipeline(
                body,
                grid=(n // SC_WINDOW,),
                in_specs=[pl.BlockSpec((SC_WINDOW, w), lambda i: (i, 0)),
                          pl.BlockSpec((TOP_K, SC_WINDOW), lambda i: (0, i))],
                out_specs=[],
                core_axis_name=("core", "subcore"),
                dimension_semantics=(pltpu.PARALLEL,),
            )(x_hbm.at[c], i_hbm)

    return run(rows, dest_t)


def _sc_gather(table, idx):
    chunks, _, w = table.shape
    m = idx.shape[1]

    @functools.partial(pl.kernel, out_type=jax.ShapeDtypeStruct((chunks, m, w), table.dtype),
                       mesh=_sc_mesh(), scratch_types=[], name="moe_combine_gather")
    def run(t_hbm, i_hbm, o_hbm):
        for c in range(chunks):
            def body(i_vmem, o_vmem, c=c):
                pltpu.sync_copy(t_hbm.at[c].at[i_vmem.at[0]], o_vmem)

            pltpu.emit_pipeline(
                body,
                grid=(m // SC_WINDOW,),
                in_specs=[pl.BlockSpec((1, SC_WINDOW), lambda i: (0, i))],
                out_specs=[pl.BlockSpec((SC_WINDOW, w), lambda i: (i, 0))],
                core_axis_name=("core", "subcore"),
                dimension_semantics=(pltpu.PARALLEL,),
            )(i_hbm, o_hbm.at[c])

    return run(table, idx)


GLU_GROUP = 2 * LANES
EXPERT_SUB = 256


def _expert_kernel(be_ref, valid_ref, x_ref, wgu_ref, bgu_ref, wd_ref, bd_ref, perm_ref, y_ref, wgu_scr, wd_scr):
    i = pl.program_id(0)
    f2 = wgu_ref.shape[2]
    bm = x_ref.shape[1]
    sub = min(EXPERT_SUB, bm)
    valid = valid_ref[i]
    live = valid > 0

    @pl.when(jnp.logical_and(live, jnp.logical_or(i == 0, be_ref[i] != be_ref[jnp.maximum(i - 1, 0)])))
    def _():
        for b in range(f2 // GLU_GROUP):
            cols = slice(b * GLU_GROUP, (b + 1) * GLU_GROUP)
            wgu_scr[:, cols] = _dot(wgu_ref[0, :, cols].astype(BF16), perm_ref[...]).astype(BF16)
        wd_scr[...] = wd_ref[0].astype(BF16)

    def sub_block(t, sub):
        rows = slice(t * sub, (t + 1) * sub)
        x = _unpack_pairs([x_ref[c, rows, :] for c in range(x_ref.shape[0])]).astype(BF16)
        yield
        gu = _dot(x, wgu_scr[...])
        yield
        gu = gu + bgu_ref[0]
        acts = []
        for b in range(f2 // GLU_GROUP):
            glu = jnp.minimum(gu[:, b * GLU_GROUP:b * GLU_GROUP + LANES], SWIGLU_LIMIT)
            lin = jnp.clip(gu[:, b * GLU_GROUP + LANES:(b + 1) * GLU_GROUP], -SWIGLU_LIMIT, SWIGLU_LIMIT)
            acts.append((glu * _sigmoid(SWIGLU_ALPHA * glu) * (lin + 1.0)).astype(BF16))
        act = jnp.concatenate(acts, axis=1)
        yield
        y = _dot(act, wd_scr[...])
        yield
        _store_chunks(y_ref, _pack_pairs(y + bd_ref[0]), rows)

    @pl.when(valid == bm)
    def _():
        _interleave(sub_block(t, sub) for t in range(bm // sub))

    for t in range(bm // sub):
        @pl.when(jnp.logical_and(valid < bm, valid > bm - (t + 1) * sub))
        def _(t=t):
            _interleave([sub_block(t, sub)])


def _experts(block_e, valid, x_buf, wgu, bgu, wd, bd, bm):
    p = x_buf.shape[1]
    f, d = wd.shape[1:]
    perm = np.zeros((GLU_GROUP, GLU_GROUP), np.float32)
    for c in range(LANES):
        perm[2 * c, c] = 1.0
        perm[2 * c + 1, LANES + c] = 1.0
    return pl.pallas_call(
        _expert_kernel,
        grid_spec=pltpu.PrefetchScalarGridSpec(
            num_scalar_prefetch=2,
            grid=(p // bm,),
            in_specs=[
                pl.BlockSpec((d // 2 // SC_COLS, bm, SC_COLS), lambda i, be, nv: (0, i, 0)),
                pl.BlockSpec((1, d, 2 * f), lambda i, be, nv: (be[i], 0, 0)),
                pl.BlockSpec((1, 1, 2 * f), lambda i, be, nv: (be[i], 0, 0)),
                pl.BlockSpec((1, f, d), lambda i, be, nv: (be[i], 0, 0)),
                pl.BlockSpec((1, 1, d), lambda i, be, nv: (be[i], 0, 0)),
                pl.BlockSpec((GLU_GROUP, GLU_GROUP), lambda i, be, nv: (0, 0)),
            ],
            out_specs=pl.BlockSpec((d // 2 // SC_COLS, bm, SC_COLS), lambda i, be, nv: (0, i, 0)),
            scratch_shapes=[pltpu.VMEM((d, 2 * f), BF16), pltpu.VMEM((f, d), BF16)],
        ),
        out_shape=jax.ShapeDtypeStruct((d // 2 // SC_COLS, p, SC_COLS), jnp.int32),
        compiler_params=_params(1),
        name="expert_ffn",
    )(block_e, valid, x_buf, wgu, bgu, wd, bd, jnp.asarray(perm, BF16))


FINAL_SUB = 256


def _final_kernel(h1_ref, yg_ref, rf_ref, p_ref, gple_ref, wpg_ref, wpp_ref, gfin_ref, *rest):
    o_ref = rest[-1]
    tm = h1_ref.shape[0]
    sub = min(FINAL_SUB, tm)

    def sub_tile(t):
        rows = slice(t * sub, (t + 1) * sub)
        rf = rf_ref[rows, :]
        h2 = h1_ref[rows, :]
        for j in range(TOP_K):
            h2 = h2 + _unpack_pairs([yg_ref[c, j, rows, :] for c in range(yg_ref.shape[0])]) * rf[:, j:j + 1]
        hn = _rms(h2, gple_ref[...]).astype(BF16)
        yield
        pre = _dot(hn, wpg_ref[...])
        proj = _dot(p_ref[rows, :].astype(BF16), wpp_ref[...])
        yield
        h3 = h2 + _sigmoid(pre) * proj
        o_ref[rows, :] = _rms(h3, gfin_ref[...])

    _interleave(sub_tile(t) for t in range(tm // sub))


def _final(h1, yg, rf, p2, gple, wpg, wpp, gfin, tm, first_tile, prev_out):
    n_total, d = h1.shape
    n = yg.shape[2]
    pd = p2.shape[1]
    dst = lambda i: (i + first_tile, 0)
    const = lambda i: (0, 0)
    in_specs = [
        pl.BlockSpec((tm, d), dst), pl.BlockSpec((d // 2 // SC_COLS, TOP_K, tm, SC_COLS), lambda i: (0, 0, i, 0)),
        pl.BlockSpec((tm, 2 * TOP_K), dst), pl.BlockSpec((tm, pd), dst),
        pl.BlockSpec((1, d), const), pl.BlockSpec((d, d), const), pl.BlockSpec((pd, d), const),
        pl.BlockSpec((1, d), const),
    ]
    args = [h1, yg, rf, p2, gple, wpg, wpp, gfin]
    aliases = {}
    if prev_out is not None:
        in_specs.append(pl.BlockSpec(memory_space=pl.ANY))
        aliases = {len(args): 0}
        args.append(prev_out)
    return pl.pallas_call(
        _final_kernel,
        grid=(n // tm,),
        in_specs=in_specs,
        out_specs=pl.BlockSpec((tm, d), dst),
        out_shape=jax.ShapeDtypeStruct((n_total, d), F32),
        input_output_aliases=aliases,
        compiler_params=_params(1),
        name="combine_ple_norm",
    )(*args)


def _layer(h, p, mix_norm, w_in, b_forget, w_branch, w_merge_gate, b_merge_gate, w_out, ffn_norm,
           w_router, b_router, w_gate_up, b_gate_up, w_down, b_down, ple_norm, w_ple_gate, w_ple_proj,
           final_norm, *, tm, tq, chunk, bm, groups):
    b, s, d = h.shape
    n = b * s
    x2 = h.reshape(n, d)
    row = lambda t: t.reshape(1, -1)

    c0 = 3 * FOX_W
    w_main = jnp.concatenate([w_in[:, :2 * FOX_W], w_in[:, c0 + FOX_HEADS:]], axis=1).astype(BF16)
    w_vt = w_in[:, 2 * FOX_W:c0].T.astype(BF16)
    w_f = jnp.pad(w_in[:, c0:c0 + FOX_HEADS], ((0, 0), (0, LANES - FOX_HEADS))).astype(BF16)
    b_f = jnp.pad(b_forget, (0, LANES - FOX_HEADS)).reshape(1, LANES)
    half = RET_KEY_DIM // 2
    inv = ROPE_BASE ** (-jnp.arange(half, dtype=F32) / half)
    ang = jnp.arange(s).astype(F32)[:, None] * inv[None, :]
    cos_t = jnp.tile(jnp.cos(ang), (1, RET_QK_W // half))
    sin_t = jnp.tile(jnp.sin(ang), (1, RET_QK_W // half))

    q_aug, k_aug, v_t, rq, rk, rv, rg = _inproj(x2, row(mix_norm), w_main, w_vt, w_f, b_f, cos_t, sin_t, s, tm)
    lg = jnp.log1p(-jnp.exp2(-5.0 - jnp.arange(RET_HEADS, dtype=F32)))
    fox, ro = _mixers(lg, q_aug, k_aug, v_t, rq.reshape(b, s, RET_QK_W), rk.reshape(b, s, RET_QK_W),
                      rv.reshape(b, s, RET_V_W), rg.reshape(b, s, RET_V_W), b, s, tq, chunk)

    w_r = jnp.pad(w_router, ((0, 0), (0, LANES - N_EXPERTS))).astype(BF16)
    b_r = jnp.concatenate([b_router, jnp.full((LANES - N_EXPERTS,), -jnp.inf, F32)]).reshape(1, LANES)
    ne, dd, f2 = w_gate_up.shape
    bgu = b_gate_up.reshape(ne, f2 // GLU_GROUP, LANES, 2).transpose(0, 1, 3, 2).reshape(ne, 1, f2)
    merge_w = (row(mix_norm), w_merge_gate.astype(BF16), row(b_merge_gate), w_branch.astype(BF16),
               w_out.astype(BF16), row(ffn_norm), w_r, b_r)
    final_w = (row(ple_norm), w_ple_gate.astype(BF16), w_ple_proj.astype(BF16), row(final_norm))
    p2 = p.reshape(n, -1)

    h1, hn, rf, ri, cnt = _merge(x2, fox, ro.reshape(n, RET_V_W), *merge_w, tm)

    a = n * TOP_K
    nb = -(-(a + N_EXPERTS * (bm - 1)) // bm)
    ri_t = ri.T
    counts = cnt[0, :N_EXPERTS].astype(jnp.int32)
    padded = (counts + bm - 1) // bm * bm
    pad_end = jnp.cumsum(padded)
    first_row = pad_end - counts
    experts = jnp.arange(N_EXPERTS, dtype=jnp.int32)

    def lookup(ids):
        return jnp.sum(jnp.where(ids[None] == experts.reshape((-1,) + (1,) * ids.ndim),
                                 first_row.reshape((-1,) + (1,) * ids.ndim), 0), axis=0)

    dest_t = lookup(ri_t[:TOP_K]) + ri_t[TOP_K:]
    block_start = jnp.arange(nb, dtype=jnp.int32) * bm
    block_e = jnp.minimum(jnp.sum(block_start[:, None] >= pad_end[None, :], axis=1),
                          N_EXPERTS - 1).astype(jnp.int32)
    valid = jnp.where(block_start < pad_end[-1],
                      jnp.clip(block_start + bm - lookup(block_e), 0, bm), 0).astype(jnp.int32)

    x_buf = _sc_dispatch(hn, dest_t, nb * bm)
    y_buf = _experts(block_e, valid, x_buf, w_gate_up, bgu, w_down, b_down.reshape(ne, 1, dd), bm)

    shares = [1] + [2 ** max(g - 1, 0) for g in range(1, groups)]
    unit = n // sum(shares)
    out = None
    start = 0
    for share in shares:
        ng = share * unit
        idx = dest_t[:, start:start + ng].reshape(1, ng * TOP_K)
        yg = _sc_gather(y_buf, idx).reshape(d // 2 // SC_COLS, TOP_K, ng, SC_COLS)
        out = _final(h1, yg, rf, p2, *final_w, tm, start // tm, out)
        start += ng
    return out.reshape(b, s, d)


def kernel(x, p, mix_norm, w_in, b_forget, w_branch, w_merge_gate, b_merge_gate, w_out, ffn_norm, w_router,
           b_router, w_gate_up, b_gate_up, w_down, b_down, ple_norm, w_ple_gate, w_ple_proj, final_norm):
    depth = p.shape[0]
    assert depth == 1, "the final norm is fused into the (single) layer"
    return _layer(x, p[0], mix_norm[0], w_in[0], b_forget[0], w_branch[0], w_merge_gate[0], b_merge_gate[0],
                  w_out[0], ffn_norm[0], w_router[0], b_router[0], w_gate_up[0], b_gate_up[0], w_down[0],
                  b_down[0], ple_norm[0], w_ple_gate[0], w_ple_proj[0], final_norm,
                  tm=1024, tq=512, chunk=128, bm=1024, groups=5)
```

```python
import functools

import numpy as np
import jax
import jax.numpy as jnp
from jax import lax
from jax.experimental import pallas as pl
from jax.experimental.pallas import tpu as pltpu
from jax.experimental.pallas import tpu_sc as plsc

FOX_HEADS = 8
FOX_HEAD_DIM = 64
RET_HEADS = 4
RET_KEY_DIM = 64
RET_VAL_DIM = 128
ROPE_BASE = 10000.0
N_EXPERTS = 32
TOP_K = 4
SWIGLU_LIMIT = 7.0
SWIGLU_ALPHA = 1.702
EPS = 1e-6

LANES = 128
FOX_W = FOX_HEADS * FOX_HEAD_DIM
RET_QK_W = RET_HEADS * RET_KEY_DIM
RET_V_W = RET_HEADS * RET_VAL_DIM
VMEM_LIMIT = 56 * 1024 * 1024
HIGH_HALF = -65536
LOG2_E = 1.4426950408889634
SC_WINDOW = 128
SC_COLS = 256

F32 = jnp.float32
BF16 = jnp.bfloat16


def _rms(x, g):
    return x * lax.rsqrt(jnp.mean(x * x, axis=-1, keepdims=True) + EPS) * g


def _sigmoid(x):
    return 1.0 / (1.0 + jnp.exp(-x))


def _dot(a, b):
    return jnp.dot(a, b, preferred_element_type=F32)


def _dot_nt(a, b):
    return lax.dot_general(a, b, (((1,), (1,)), ((), ())), preferred_element_type=F32)


def _pack_pairs(v):
    bits = pltpu.bitcast(v.astype(BF16).astype(F32), jnp.int32)
    half = v.shape[1] // 2
    return lax.shift_right_logical(bits[:, :half], 16) | (bits[:, half:] & HIGH_HALF)


def _unpack_pairs(chunks):
    return jnp.concatenate([pltpu.bitcast(lax.shift_left(w, 16), F32) for w in chunks]
                           + [pltpu.bitcast(w & HIGH_HALF, F32) for w in chunks], axis=1)


def _store_chunks(ref, v, rows=slice(None)):
    for c in range(ref.shape[0]):
        ref[c, rows, :] = v[:, c * SC_COLS:(c + 1) * SC_COLS]


def _interleave(generators):
    live = []
    pending = list(generators)
    while pending or live:
        if pending:
            live.append(pending.pop(0))
        for g in list(live):
            try:
                next(g)
            except StopIteration:
                live.remove(g)


def _resident(shape):
    return pl.BlockSpec(shape, lambda *_: (0,) * len(shape), pipeline_mode=pl.Buffered(1))


def _params(n_axes):
    return pltpu.CompilerParams(dimension_semantics=("arbitrary",) * n_axes,
                                vmem_limit_bytes=VMEM_LIMIT)


C_MID = FOX_HEADS
C_LO = 2 * FOX_HEADS
C_ONE = 3 * FOX_HEADS
N_AUG = 3
AUG_W = 2 * N_AUG
K_AUG0 = FOX_HEADS * AUG_W


def _split3(t):
    hi = t.astype(BF16).astype(F32)
    r = t - hi
    mid = r.astype(BF16).astype(F32)
    lo = (r - mid).astype(BF16).astype(F32)
    return hi + pltpu.roll(mid, C_MID, 1) + pltpu.roll(lo, C_LO, 1)


INPROJ_SUB = 256


def _inproj_kernel(x_ref, g_ref, w_ref, wvt_ref, wf_ref, bf_ref, cos_ref, sin_ref, place_ref,
                   q_ref, k_ref, vt_ref, rq_ref, rk_ref, rv_ref, rg_ref, carry_scr, *, tiles_per_seq):
    i = pl.program_id(0)
    tm = x_ref.shape[0]
    sub = min(INPROJ_SUB, tm)

    @pl.when(i % tiles_per_seq == 0)
    def _():
        carry_scr[...] = jnp.zeros(carry_scr.shape, F32)

    lane = lax.broadcasted_iota(jnp.int32, (sub, LANES), 1)
    r = lax.broadcasted_iota(jnp.int32, (sub, sub), 0)
    c = lax.broadcasted_iota(jnp.int32, (sub, sub), 1)
    tri = jnp.where(c <= r, 1.0, 0.0).astype(BF16)
    own = lane < FOX_HEAD_DIM
    is_aug = jnp.logical_and(lane >= FOX_HEAD_DIM, lane < FOX_HEAD_DIM + AUG_W)
    half = RET_KEY_DIM // 2
    lane_r = lax.broadcasted_iota(jnp.int32, (sub, RET_QK_W), 1)
    first = (lane_r % RET_KEY_DIM) < half
    carries = [carry_scr[...]]

    def sub_tile(t):
        rows = slice(t * sub, (t + 1) * sub)
        xn = _rms(x_ref[rows, :], g_ref[...]).astype(BF16)
        yield
        u = _dot(xn, w_ref[...])
        vt_ref[:, rows] = _dot_nt(wvt_ref[...], xn).astype(BF16)
        z = _dot(xn, wf_ref[...]) + bf_ref[...]
        yield
        lf = jnp.where(lane < FOX_HEADS, jnp.minimum(z, 0.0) - jnp.log1p(jnp.exp(-jnp.abs(z))), 0.0)
        ps = _dot(tri, _split3(lf).astype(BF16))
        yield
        assert len(carries) == t + 1
        cum = ps + pltpu.roll(ps, LANES - C_MID, 1) + pltpu.roll(ps, LANES - C_LO, 1)
        cum = jnp.where(lane < FOX_HEADS, cum, 0.0) + carries[t]
        carries.append(cum[sub - 1:sub, :])
        c3 = (_split3(cum * LOG2_E) + jnp.where(lane == C_ONE, 1.0, 0.0)).astype(BF16)
        aug = _dot(c3, place_ref[...])

        o = 2 * FOX_W
        rq = u[:, o:o + RET_QK_W]; o += RET_QK_W
        rk = u[:, o:o + RET_QK_W]; o += RET_QK_W
        rv_ref[rows, :] = u[:, o:o + RET_V_W].astype(BF16); o += RET_V_W
        rg_ref[rows, :] = u[:, o:o + RET_V_W]

        cos = cos_ref[rows, :]
        sin = sin_ref[rows, :]

        def rot(v):
            partner = jnp.where(first, -pltpu.roll(v, RET_QK_W - half, 1), pltpu.roll(v, half, 1))
            return v * cos + partner * sin

        rq_ref[rows, :] = rot(rq).astype(BF16)
        rk_ref[rows, :] = (rot(rk) * (RET_KEY_DIM ** -0.5)).astype(BF16)
        yield
        fq = u[:, 0:FOX_W] * (FOX_HEAD_DIM ** -0.5 * LOG2_E)
        fk = u[:, FOX_W:2 * FOX_W]
        for src, aug0, dst in ((fq, 0, q_ref), (fk, K_AUG0, k_ref)):
            for h in range(FOX_HEADS):
                blk = src[:, (h // 2) * LANES:(h // 2 + 1) * LANES]
                if h % 2:
                    blk = pltpu.roll(blk, FOX_HEAD_DIM, 1)
                mine = pltpu.roll(aug, (FOX_HEAD_DIM - aug0 - AUG_W * h) % LANES, 1)
                slab = jnp.where(own, blk, jnp.where(is_aug, mine, 0.0))
                dst[rows, h * LANES:(h + 1) * LANES] = slab.astype(BF16)

    _interleave(sub_tile(t) for t in range(tm // sub))
    carry_scr[...] = carries[-1]


def _placement():
    place = np.zeros((LANES, LANES), np.float32)
    for h in range(FOX_HEADS):
        for part, src in enumerate((h, C_MID + h, C_LO + h)):
            place[C_ONE, AUG_W * h + part] = 1.0
            place[src, AUG_W * h + N_AUG + part] = 1.0
            place[src, K_AUG0 + AUG_W * h + part] = -1.0
            place[C_ONE, K_AUG0 + AUG_W * h + N_AUG + part] = 1.0
    return jnp.asarray(place, BF16)


def _inproj(x2, g, w_main, w_vt, w_f, b_f, cos_t, sin_t, seq, tm):
    n, d = x2.shape
    wn = w_main.shape[1]
    spt = seq // tm
    aw = FOX_HEADS * LANES
    place = _placement()
    row = lambda i: (i, 0)
    const = lambda i: (0, 0)
    pos = lambda i: (i % spt, 0)
    out_shape = [
        jax.ShapeDtypeStruct((n, aw), BF16), jax.ShapeDtypeStruct((n, aw), BF16),
        jax.ShapeDtypeStruct((FOX_W, n), BF16),
        jax.ShapeDtypeStruct((n, RET_QK_W), BF16), jax.ShapeDtypeStruct((n, RET_QK_W), BF16),
        jax.ShapeDtypeStruct((n, RET_V_W), BF16), jax.ShapeDtypeStruct((n, RET_V_W), F32),
    ]
    out_specs = [
        pl.BlockSpec((tm, aw), row), pl.BlockSpec((tm, aw), row),
        pl.BlockSpec((FOX_W, tm), lambda i: (0, i)),
        pl.BlockSpec((tm, RET_QK_W), row), pl.BlockSpec((tm, RET_QK_W), row),
        pl.BlockSpec((tm, RET_V_W), row), pl.BlockSpec((tm, RET_V_W), row),
    ]
    return pl.pallas_call(
        functools.partial(_inproj_kernel, tiles_per_seq=spt),
        grid=(n // tm,),
        in_specs=[
            pl.BlockSpec((tm, d), row), pl.BlockSpec((1, d), const),
            _resident((d, wn)), _resident((FOX_W, d)),
            _resident((d, LANES)), pl.BlockSpec((1, LANES), const),
            pl.BlockSpec((tm, RET_QK_W), pos), pl.BlockSpec((tm, RET_QK_W), pos),
            _resident((LANES, LANES)),
        ],
        out_specs=out_specs,
        out_shape=out_shape,
        scratch_shapes=[pltpu.VMEM((1, LANES), F32)],
        compiler_params=_params(1),
        name="inproj",
    )(x2, g, w_main, w_vt, w_f, b_f, cos_t, sin_t, place)


FOX_AHEAD = 3


def _fox_kernel(q_ref, k_ref, vt_ref, o_ref, *, tq):
    seq = q_ref.shape[0]
    half = tq // 2
    vrow = lax.broadcasted_iota(jnp.int32, (LANES, tq), 0)
    one = jnp.ones((), BF16)
    items = [(qi, j) for qi in range(seq // tq) for j in range(qi + 1)]

    def logits(qi, j):
        out = []
        for a in range(2):
            sl = slice(a * LANES, (a + 1) * LANES)
            kk = k_ref[j * tq:(j + 1) * tq, sl]
            qq = q_ref[qi * tq:(qi + 1) * tq, sl]
            if j != qi:
                out.append([(0, tq, _dot_nt(kk, qq))])
                continue
            parts = []
            for c0, nk in ((0, half), (half, tq)):
                sc = _dot_nt(kk[:nk], qq[c0:c0 + half])
                key = lax.broadcasted_iota(jnp.int32, sc.shape, 0)
                qry = lax.broadcasted_iota(jnp.int32, sc.shape, 1) + c0
                parts.append((c0, nk, jnp.where(key <= qry, sc, -jnp.inf)))
            out.append(parts)
        return out

    queue = [logits(*it) for it in items[:FOX_AHEAD]]
    carry = None
    for w, (qi, j) in enumerate(items):
        if w + FOX_AHEAD < len(items):
            queue.append(logits(*items[w + FOX_AHEAD]))
        s_cur = queue.pop(0)
        if j == 0:
            carry = [(jnp.full((1, tq), -jnp.inf, F32), jnp.zeros((LANES, tq), F32)) for _ in range(2)]
        v = vt_ref[:, j * tq:(j + 1) * tq]
        for a in range(2):
            m_prev, acc = carry[a]
            va = jnp.where((vrow // FOX_HEAD_DIM) == a, v, one)
            ms, accs = [], []
            for c0, nk, sc in s_cur[a]:
                cols = slice(c0, c0 + sc.shape[1])
                m_new = jnp.maximum(m_prev[:, cols], jnp.max(sc, axis=0, keepdims=True))
                alpha = jnp.exp2(m_prev[:, cols] - m_new)
                p = jnp.exp2(sc - m_new).astype(BF16)
                ms.append(m_new)
                accs.append(alpha * acc[:, cols] + _dot(va[:, :nk], p))
            carry[a] = (ms[0], accs[0]) if len(ms) == 1 else (jnp.concatenate(ms, axis=1),
                                                               jnp.concatenate(accs, axis=1))
        if j == qi:
            acc0, acc1 = carry[0][1], carry[1][1]
            ot = jnp.where(vrow < FOX_HEAD_DIM, acc0 / acc0[FOX_HEAD_DIM:FOX_HEAD_DIM + 1, :],
                           acc1 / acc1[0:1, :])
            o_ref[qi * tq:(qi + 1) * tq, :] = ot.T.astype(o_ref.dtype)


def _fox(q_aug, k_aug, v_t, batch, seq, tq):
    n = q_aug.shape[0]
    pairs = FOX_HEADS // 2
    kern = functools.partial(_fox_kernel, tq=tq)
    return pl.pallas_call(
        kern,
        grid=(batch, pairs),
        in_specs=[
            pl.BlockSpec((seq, 2 * LANES), lambda i, h: (i, h)),
            pl.BlockSpec((seq, 2 * LANES), lambda i, h: (i, h)),
            pl.BlockSpec((LANES, seq), lambda i, h: (h, i)),
        ],
        out_specs=pl.BlockSpec((seq, LANES), lambda i, h: (i, h)),
        out_shape=jax.ShapeDtypeStruct((n, FOX_W), BF16),
        compiler_params=_params(2),
        name="fox_attention",
    )(q_aug, k_aug, v_t)


def _ret_kernel(lg_ref, q_ref, k_ref, v_ref, g_ref, o_ref, *, chunk):
    s = q_ref.shape[1]
    lane = lax.broadcasted_iota(jnp.int32, (chunk, LANES), 1)
    ri = lax.broadcasted_iota(jnp.int32, (chunk, chunk), 0)
    ci = lax.broadcasted_iota(jnp.int32, (chunk, chunk), 1)
    diff = (ri - ci).astype(F32)
    pos = lax.broadcasted_iota(jnp.int32, (chunk, 1), 0).astype(F32)

    def head(h):
        lg = lg_ref[h]
        mine = (lane // RET_KEY_DIM) == (h % 2)
        qk = slice((h // 2) * LANES, (h // 2 + 1) * LANES)
        vs = slice(h * RET_VAL_DIM, (h + 1) * RET_VAL_DIM)
        inner = jnp.where(diff >= 0, jnp.exp(jnp.maximum(diff, 0.0) * lg), 0.0)
        q_decay = jnp.exp((pos + 1.0) * lg)
        k_decay = jnp.exp((chunk - 1.0 - pos) * lg)
        chunk_decay = jnp.exp(jnp.full((1, 1), chunk, F32) * lg)
        state = jnp.zeros((LANES, RET_VAL_DIM), F32)
        for c in range(s // chunk):
            rows = slice(c * chunk, (c + 1) * chunk)
            q = jnp.where(mine, q_ref[0, rows, qk], jnp.zeros((), BF16))
            k = jnp.where(mine, k_ref[0, rows, qk], jnp.zeros((), BF16))
            v = v_ref[0, rows, vs]
            scores = _dot_nt(q, k)
            inter = _dot(q, state.astype(BF16))
            kd = (k.astype(F32) * k_decay).astype(BF16)
            update = lax.dot_general(kd, v, (((0,), (0,)), ((), ())), preferred_element_type=F32)
            yield
            o = _dot((scores * inner).astype(BF16), v) + inter * q_decay
            state = state * chunk_decay + update
            yield
            o = o * lax.rsqrt(jnp.mean(o * o, axis=-1, keepdims=True) + EPS)
            g = g_ref[0, rows, vs]
            o_ref[0, rows, vs] = (o * (g * _sigmoid(g))).astype(o_ref.dtype)

    _interleave(head(h) for h in range(RET_HEADS))


def _retention(lg, rq, rk, rv, rg, chunk):
    b, s, _ = rq.shape
    kern = functools.partial(_ret_kernel, chunk=chunk)
    qk_spec = pl.BlockSpec((1, s, RET_QK_W), lambda i, lg_ref: (i, 0, 0))
    v_spec = pl.BlockSpec((1, s, RET_V_W), lambda i, lg_ref: (i, 0, 0))
    return pl.pallas_call(
        kern,
        grid_spec=pltpu.PrefetchScalarGridSpec(
            num_scalar_prefetch=1,
            grid=(b,),
            in_specs=[qk_spec, qk_spec, v_spec, v_spec],
            out_specs=v_spec,
        ),
        out_shape=jax.ShapeDtypeStruct((b, s, RET_V_W), BF16),
        compiler_params=_params(1),
        name="retention",
    )(lg, rq, rk, rv, rg)


MERGE_SUB = 256


def _merge_kernel(x_ref, fox_ref, ro_ref, gmix_ref, wmg_ref, bmg_ref, wb_ref, wout_ref, gffn_ref,
                  wr_ref, br_ref, h1_ref, hn_ref, rf_ref, ri_ref, cnt_ref, base_scr):
    i = pl.program_id(0)
    tm, d = x_ref.shape
    sub = min(MERGE_SUB, tm)

    @pl.when(i == 0)
    def _():
        base_scr[...] = jnp.zeros(base_scr.shape, F32)

    lane = lax.broadcasted_iota(jnp.int32, (sub, LANES), 1)
    lane_f = lane.astype(F32)
    r = lax.broadcasted_iota(jnp.int32, (sub, sub), 0)
    c = lax.broadcasted_iota(jnp.int32, (sub, sub), 1)
    tri = jnp.where(c < r, 1.0, 0.0).astype(BF16)
    counts = [base_scr[...]]

    def sub_tile(t):
        rows = slice(t * sub, (t + 1) * sub)
        x = x_ref[rows, :]
        pf = _dot(fox_ref[rows, :], wb_ref[0])
        pr = _dot(ro_ref[rows, :], wb_ref[1])
        xn = _rms(x, gmix_ref[...]).astype(BF16)
        yield
        pre = _dot(xn, wmg_ref[...])
        yield
        gate = _sigmoid(pre + bmg_ref[...])
        yield
        merged = (gate[:, :d] * pf + gate[:, d:] * pr).astype(BF16)
        yield
        h1 = x + _dot(merged, wout_ref[...])
        yield
        h1_ref[rows, :] = h1
        hn = _rms(h1, gffn_ref[...]).astype(BF16)
        _store_chunks(hn_ref, _pack_pairs(hn), rows)
        yield
        logits = _dot(hn, wr_ref[...]) + br_ref[...]
        yield
        vals, idxs = [], []
        cur = logits
        for _ in range(TOP_K):
            m = jnp.max(cur, axis=-1, keepdims=True)
            idx = jnp.min(jnp.where(cur == m, lane_f, float(LANES)), axis=-1, keepdims=True)
            vals.append(m)
            idxs.append(idx)
            cur = jnp.where(lane_f == idx, -jnp.inf, cur)
        exps = [jnp.exp(v - vals[0]) for v in vals]
        denom = exps[0] + exps[1] + exps[2] + exps[3]
        onehot = jnp.zeros(logits.shape, F32)
        for idx in idxs:
            onehot = onehot + jnp.where(lane_f == idx, 1.0, 0.0)
        yield
        assert len(counts) == t + 1
        before = _dot(tri, onehot.astype(BF16)) + counts[t]
        counts.append(counts[t] + jnp.sum(onehot, axis=0, keepdims=True))
        rf = jnp.zeros(logits.shape, F32)
        ri = jnp.zeros(logits.shape, F32)
        for j in range(TOP_K):
            rank = jnp.sum(jnp.where(lane_f == idxs[j], before, 0.0), axis=-1, keepdims=True)
            rf = jnp.where(lane == j, exps[j] / denom, rf)
            ri = jnp.where(lane == j, idxs[j], ri)
            ri = jnp.where(lane == TOP_K + j, rank, ri)
        rf_ref[rows, :] = rf[:, :2 * TOP_K]
        ri_ref[rows, :] = ri[:, :2 * TOP_K].astype(jnp.int32)

    _interleave(sub_tile(t) for t in range(tm // sub))
    base_scr[...] = counts[-1]
    cnt_ref[...] = counts[-1]


def _merge(x2, fox, ro, gmix, wmg, bmg, wb, wout, gffn, wr, br, tm):
    n, d = x2.shape
    row = lambda i: (i, 0)
    const = lambda i: (0, 0)
    return pl.pallas_call(
        _merge_kernel,
        grid=(n // tm,),
        in_specs=[
            pl.BlockSpec((tm, d), row), pl.BlockSpec((tm, FOX_W), row), pl.BlockSpec((tm, RET_V_W), row),
            pl.BlockSpec((1, d), const), _resident((d, 2 * d)), pl.BlockSpec((1, 2 * d), const),
            _resident((2, FOX_W, d)), _resident((d, d)),
            pl.BlockSpec((1, d), const), _resident((d, LANES)), pl.BlockSpec((1, LANES), const),
        ],
        out_specs=[
            pl.BlockSpec((tm, d), row), pl.BlockSpec((d // 2 // SC_COLS, tm, SC_COLS), lambda i: (0, i, 0)),
            pl.BlockSpec((tm, 2 * TOP_K), row), pl.BlockSpec((tm, 2 * TOP_K), row),
            pl.BlockSpec((1, LANES), const),
        ],
        out_shape=[
            jax.ShapeDtypeStruct((n, d), F32), jax.ShapeDtypeStruct((d // 2 // SC_COLS, n, SC_COLS), jnp.int32),
            jax.ShapeDtypeStruct((n, 2 * TOP_K), F32), jax.ShapeDtypeStruct((n, 2 * TOP_K), jnp.int32),
            jax.ShapeDtypeStruct((1, LANES), F32),
        ],
        scratch_shapes=[pltpu.VMEM((1, LANES), F32)],
        compiler_params=_params(1),
        name="merge_router",
    )(x2, fox, ro, gmix, wmg, bmg, wb, wout, gffn, wr, br)


def _sc_mesh():
    return plsc.VectorSubcoreMesh(core_axis_name="core", subcore_axis_name="subcore")


def _sc_dispatch(rows, dest_t, n_out):
    chunks, n, w = rows.shape

    @functools.partial(pl.kernel, out_type=jax.ShapeDtypeStruct((chunks, n_out, w), rows.dtype),
                       mesh=_sc_mesh(), scratch_types=[], name="moe_dispatch")
    def run(x_hbm, i_hbm, o_hbm):
        for c in range(chunks):
            def body(x_vmem, i_vmem, c=c):
                for j in range(TOP_K):
                    pltpu.sync_copy(x_vmem, o_hbm.at[c].at[i_vmem.at[j]])

            pltpu.emit_pipeline(
                body,
                grid=(n // SC_WINDOW,),
                in_specs=[pl.BlockSpec((SC_WINDOW, w), lambda i: (i, 0)),
                          pl.BlockSpec((TOP_K, SC_WINDOW), lambda i: (0, i))],
                out_specs=[],
                core_axis_name=("core", "subcore"),
                dimension_semantics=(pltpu.PARALLEL,),
            )(x_hbm.at[c], i_hbm)

    return run(rows, dest_t)


def _sc_gather(table, idx):
    chunks, _, w = table.shape
    m = idx.shape[1]

    @functools.partial(pl.kernel, out_type=jax.ShapeDtypeStruct((chunks, m, w), table.dtype),
                       mesh=_sc_mesh(), scratch_types=[], name="moe_combine_gather")
    def run(t_hbm, i_hbm, o_hbm):
        for c in range(chunks):
            def body(i_vmem, o_vmem, c=c):
                pltpu.sync_copy(t_hbm.at[c].at[i_vmem.at[0]], o_vmem)

            pltpu.emit_pipeline(
                body,
                grid=(m // SC_WINDOW,),
                in_specs=[pl.BlockSpec((1, SC_WINDOW), lambda i: (0, i))],
                out_specs=[pl.BlockSpec((SC_WINDOW, w), lambda i: (i, 0))],
                core_axis_name=("core", "subcore"),
                dimension_semantics=(pltpu.PARALLEL,),
            )(i_hbm, o_hbm.at[c])

    return run(table, idx)


GLU_GROUP = 2 * LANES
EXPERT_SUB = 128


def _expert_kernel(be_ref, valid_ref, x_ref, wgu_ref, bgu_ref, wd_ref, bd_ref, perm_ref, y_ref, wgu_scr, wd_scr):
    i = pl.program_id(0)
    f2 = wgu_ref.shape[2]
    bm = x_ref.shape[1]
    sub = min(EXPERT_SUB, bm)
    valid = valid_ref[i]
    live = valid > 0

    @pl.when(jnp.logical_and(live, jnp.logical_or(i == 0, be_ref[i] != be_ref[jnp.maximum(i - 1, 0)])))
    def _():
        for b in range(f2 // GLU_GROUP):
            cols = slice(b * GLU_GROUP, (b + 1) * GLU_GROUP)
            wgu_scr[:, cols] = _dot(wgu_ref[0, :, cols].astype(BF16), perm_ref[...]).astype(BF16)
        wd_scr[...] = wd_ref[0].astype(BF16)

    def sub_block(t, sub):
        rows = slice(t * sub, (t + 1) * sub)
        x = _unpack_pairs([x_ref[c, rows, :] for c in range(x_ref.shape[0])]).astype(BF16)
        yield
        gu = _dot(x, wgu_scr[...])
        yield
        gu = gu + bgu_ref[0]
        acts = []
        for b in range(f2 // GLU_GROUP):
            glu = jnp.minimum(gu[:, b * GLU_GROUP:b * GLU_GROUP + LANES], SWIGLU_LIMIT)
            lin = jnp.clip(gu[:, b * GLU_GROUP + LANES:(b + 1) * GLU_GROUP], -SWIGLU_LIMIT, SWIGLU_LIMIT)
            acts.append((glu * _sigmoid(SWIGLU_ALPHA * glu) * (lin + 1.0)).astype(BF16))
        act = jnp.concatenate(acts, axis=1)
        yield
        y = _dot(act, wd_scr[...])
        yield
        _store_chunks(y_ref, _pack_pairs(y + bd_ref[0]), rows)

    @pl.when(valid == bm)
    def _():
        _interleave(sub_block(t, sub) for t in range(bm // sub))

    for t in range(bm // sub):
        @pl.when(jnp.logical_and(valid < bm, valid > bm - (t + 1) * sub))
        def _(t=t):
            _interleave([sub_block(t, sub)])


def _experts(block_e, valid, x_buf, wgu, bgu, wd, bd, bm):
    p = x_buf.shape[1]
    f, d = wd.shape[1:]
    perm = np.zeros((GLU_GROUP, GLU_GROUP), np.float32)
    for c in range(LANES):
        perm[2 * c, c] = 1.0
        perm[2 * c + 1, LANES + c] = 1.0
    return pl.pallas_call(
        _expert_kernel,
        grid_spec=pltpu.PrefetchScalarGridSpec(
            num_scalar_prefetch=2,
            grid=(p // bm,),
            in_specs=[
                pl.BlockSpec((d // 2 // SC_COLS, bm, SC_COLS), lambda i, be, nv: (0, i, 0)),
                pl.BlockSpec((1, d, 2 * f), lambda i, be, nv: (be[i], 0, 0)),
                pl.BlockSpec((1, 1, 2 * f), lambda i, be, nv: (be[i], 0, 0)),
                pl.BlockSpec((1, f, d), lambda i, be, nv: (be[i], 0, 0)),
                pl.BlockSpec((1, 1, d), lambda i, be, nv: (be[i], 0, 0)),
                pl.BlockSpec((GLU_GROUP, GLU_GROUP), lambda i, be, nv: (0, 0)),
            ],
            out_specs=pl.BlockSpec((d // 2 // SC_COLS, bm, SC_COLS), lambda i, be, nv: (0, i, 0)),
            scratch_shapes=[pltpu.VMEM((d, 2 * f), BF16), pltpu.VMEM((f, d), BF16)],
        ),
        out_shape=jax.ShapeDtypeStruct((d // 2 // SC_COLS, p, SC_COLS), jnp.int32),
        compiler_params=_params(1),
        name="expert_ffn",
    )(block_e, valid, x_buf, wgu, bgu, wd, bd, jnp.asarray(perm, BF16))


FINAL_SUB = 256


def _final_kernel(h1_ref, yg_ref, rf_ref, p_ref, gple_ref, wpg_ref, wpp_ref, gfin_ref, *rest):
    o_ref = rest[-1]
    tm = h1_ref.shape[0]
    sub = min(FINAL_SUB, tm)

    def sub_tile(t):
        rows = slice(t * sub, (t + 1) * sub)
        rf = rf_ref[rows, :]
        h2 = h1_ref[rows, :]
        for j in range(TOP_K):
            h2 = h2 + _unpack_pairs([yg_ref[c, j, rows, :] for c in range(yg_ref.shape[0])]) * rf[:, j:j + 1]
        hn = _rms(h2, gple_ref[...]).astype(BF16)
        yield
        pre = _dot(hn, wpg_ref[...])
        proj = _dot(p_ref[rows, :].astype(BF16), wpp_ref[...])
        yield
        h3 = h2 + _sigmoid(pre) * proj
        o_ref[rows, :] = _rms(h3, gfin_ref[...])

    _interleave(sub_tile(t) for t in range(tm // sub))


def _final(h1, yg, rf, p2, gple, wpg, wpp, gfin, tm, first_tile, prev_out):
    n_total, d = h1.shape
    n = yg.shape[2]
    pd = p2.shape[1]
    dst = lambda i: (i + first_tile, 0)
    const = lambda i: (0, 0)
    in_specs = [
        pl.BlockSpec((tm, d), dst), pl.BlockSpec((d // 2 // SC_COLS, TOP_K, tm, SC_COLS), lambda i: (0, 0, i, 0)),
        pl.BlockSpec((tm, 2 * TOP_K), dst), pl.BlockSpec((tm, pd), dst),
        pl.BlockSpec((1, d), const), pl.BlockSpec((d, d), const), pl.BlockSpec((pd, d), const),
        pl.BlockSpec((1, d), const),
    ]
    args = [h1, yg, rf, p2, gple, wpg, wpp, gfin]
    aliases = {}
    if prev_out is not None:
        in_specs.append(pl.BlockSpec(memory_space=pl.ANY))
        aliases = {len(args): 0}
        args.append(prev_out)
    return pl.pallas_call(
        _final_kernel,
        grid=(n // tm,),
        in_specs=in_specs,
        out_specs=pl.BlockSpec((tm, d), dst),
        out_shape=jax.ShapeDtypeStruct((n_total, d), F32),
        input_output_aliases=aliases,
        compiler_params=_params(1),
        name="combine_ple_norm",
    )(*args)


def _layer(h, p, mix_norm, w_in, b_forget, w_branch, w_merge_gate, b_merge_gate, w_out, ffn_norm,
           w_router, b_router, w_gate_up, b_gate_up, w_down, b_down, ple_norm, w_ple_gate, w_ple_proj,
           final_norm, *, tm, tq, chunk, bm, groups):
    b, s, d = h.shape
    n = b * s
    x2 = h.reshape(n, d)
    row = lambda t: t.reshape(1, -1)

    c0 = 3 * FOX_W
    w_main = jnp.concatenate([w_in[:, :2 * FOX_W], w_in[:, c0 + FOX_HEADS:]], axis=1).astype(BF16)
    w_vt = w_in[:, 2 * FOX_W:c0].T.astype(BF16)
    w_f = jnp.pad(w_in[:, c0:c0 + FOX_HEADS], ((0, 0), (0, LANES - FOX_HEADS))).astype(BF16)
    b_f = jnp.pad(b_forget, (0, LANES - FOX_HEADS)).reshape(1, LANES)
    half = RET_KEY_DIM // 2
    inv = ROPE_BASE ** (-jnp.arange(half, dtype=F32) / half)
    ang = jnp.arange(s).astype(F32)[:, None] * inv[None, :]
    cos_t = jnp.tile(jnp.cos(ang), (1, RET_QK_W // half))
    sin_t = jnp.tile(jnp.sin(ang), (1, RET_QK_W // half))

    q_aug, k_aug, v_t, rq, rk, rv, rg = _inproj(x2, row(mix_norm), w_main, w_vt, w_f, b_f, cos_t, sin_t, s, tm)
    fox = _fox(q_aug, k_aug, v_t, b, s, tq)

    lg = jnp.log1p(-jnp.exp2(-5.0 - jnp.arange(RET_HEADS, dtype=F32)))
    ro = _retention(lg, rq.reshape(b, s, RET_QK_W), rk.reshape(b, s, RET_QK_W),
                    rv.reshape(b, s, RET_V_W), rg.reshape(b, s, RET_V_W), chunk)

    w_r = jnp.pad(w_router, ((0, 0), (0, LANES - N_EXPERTS))).astype(BF16)
    b_r = jnp.concatenate([b_router, jnp.full((LANES - N_EXPERTS,), -jnp.inf, F32)]).reshape(1, LANES)
    ne, dd, f2 = w_gate_up.shape
    bgu = b_gate_up.reshape(ne, f2 // GLU_GROUP, LANES, 2).transpose(0, 1, 3, 2).reshape(ne, 1, f2)
    merge_w = (row(mix_norm), w_merge_gate.astype(BF16), row(b_merge_gate), w_branch.astype(BF16),
               w_out.astype(BF16), row(ffn_norm), w_r, b_r)
    final_w = (row(ple_norm), w_ple_gate.astype(BF16), w_ple_proj.astype(BF16), row(final_norm))
    p2 = p.reshape(n, -1)

    h1, hn, rf, ri, cnt = _merge(x2, fox, ro.reshape(n, RET_V_W), *merge_w, tm)

    a = n * TOP_K
    nb = -(-(a + N_EXPERTS * (bm - 1)) // bm)
    ri_t = ri.T
    counts = cnt[0, :N_EXPERTS].astype(jnp.int32)
    padded = (counts + bm - 1) // bm * bm
    pad_end = jnp.cumsum(padded)
    first_row = pad_end - counts
    experts = jnp.arange(N_EXPERTS, dtype=jnp.int32)

    def lookup(ids):
        return jnp.sum(jnp.where(ids[None] == experts.reshape((-1,) + (1,) * ids.ndim),
                                 first_row.reshape((-1,) + (1,) * ids.ndim), 0), axis=0)

    dest_t = lookup(ri_t[:TOP_K]) + ri_t[TOP_K:]
    block_start = jnp.arange(nb, dtype=jnp.int32) * bm
    block_e = jnp.minimum(jnp.sum(block_start[:, None] >= pad_end[None, :], axis=1),
                          N_EXPERTS - 1).astype(jnp.int32)
    valid = jnp.where(block_start < pad_end[-1],
                      jnp.clip(block_start + bm - lookup(block_e), 0, bm), 0).astype(jnp.int32)

    x_buf = _sc_dispatch(hn, dest_t, nb * bm)
    y_buf = _experts(block_e, valid, x_buf, w_gate_up, bgu, w_down, b_down.reshape(ne, 1, dd), bm)

    shares = [1] + [2 ** max(g - 1, 0) for g in range(1, groups)]
    unit = n // sum(shares)
    out = None
    start = 0
    for share in shares:
        ng = share * unit
        idx = dest_t[:, start:start + ng].reshape(1, ng * TOP_K)
        yg = _sc_gather(y_buf, idx).reshape(d // 2 // SC_COLS, TOP_K, ng, SC_COLS)
        out = _final(h1, yg, rf, p2, *final_w, tm, start // tm, out)
        start += ng
    return out.reshape(b, s, d)


def kernel(x, p, mix_norm, w_in, b_forget, w_branch, w_merge_gate, b_merge_gate, w_out, ffn_norm, w_router,
           b_router, w_gate_up, b_gate_up, w_down, b_down, ple_norm, w_ple_gate, w_ple_proj, final_norm):
    depth = p.shape[0]
    assert depth == 1, "the final norm is fused into the (single) layer"
    return _layer(x, p[0], mix_norm[0], w_in[0], b_forget[0], w_branch[0], w_merge_gate[0], b_merge_gate[0],
                  w_out[0], ffn_norm[0], w_router[0], b_router[0], w_gate_up[0], b_gate_up[0], w_down[0],
                  b_down[0], ple_norm[0], w_ple_gate[0], w_ple_proj[0], final_norm,
                  tm=1024, tq=512, chunk=128, bm=1024, groups=5)
```
